```python
import math
import jax
import jax.numpy as jnp
from jax import lax
import numpy as np

D_MODEL = 1024
BATCH = 2
SEQ = 8192
DEPTH = 2

HEAD_DIM = 64
MIX_WIDTH = 768
MEM_HEADS = 4
MEM_WIDTH = MEM_HEADS * HEAD_DIM
N_MEM = 256
A_Q_HEADS = MIX_WIDTH // HEAD_DIM
A_KV_HEADS = 4
A_GROUP = A_Q_HEADS // A_KV_HEADS
WINDOW = 128
BLOCK = 128
B_HEADS = MIX_WIDTH // (2 * HEAD_DIM)
B_VDIM = 2 * HEAD_DIM
Q_BLOCK = 128
D_FF = 3584
N_EXPERTS = 8
TOP_K = 2
MOE_BLOCK = 128
EPS = 1e-6
NEG_INF = -1e30
N_EVEN = (DEPTH + 1) // 2
N_ODD = DEPTH // 2
A_QW = A_Q_HEADS * HEAD_DIM
A_KVW = A_KV_HEADS * HEAD_DIM
A_IN = A_QW + 2 * A_KVW + MEM_WIDTH
B_IN = 3 * MIX_WIDTH + MEM_WIDTH
OUT_IN = MIX_WIDTH + MEM_WIDTH

kernel_name = 'hybrid_swa_diffattn_memory_moe_encoder'


def rmsnorm(x, g):
    xf = x.astype(jnp.float32)
    y = xf * lax.rsqrt(jnp.mean(xf * xf, axis=-1, keepdims=True) + EPS)
    return (y * g.astype(jnp.float32)).astype(x.dtype)


def alibi_slopes(n):
    return jnp.exp2(-8.0 * jnp.arange(1, n + 1, dtype=jnp.float32) / n)


def windowed_gqa_sink(q, k, v, sink):
    B, S = q.shape[0], q.shape[1]
    nb = S // BLOCK
    qb = q.reshape(B, nb, BLOCK, A_KV_HEADS, A_GROUP, HEAD_DIM)

    def neighbours(t):
        tp = jnp.pad(t, ((0, 0), (BLOCK, BLOCK), (0, 0), (0, 0)))
        tp = tp.reshape(B, nb + 2, BLOCK, A_KV_HEADS, HEAD_DIM)
        return jnp.concatenate([tp[:, :-2], tp[:, 1:-1], tp[:, 2:]], axis=2)

    kb = neighbours(k)
    vb = neighbours(v)
    s = jnp.einsum('bnqkgd,bnskd->bnkgqs', qb, kb, preferred_element_type=jnp.float32) * HEAD_DIM ** -0.5
    qi = jnp.arange(BLOCK)[:, None]
    kj = jnp.arange(3 * BLOCK)[None, :]
    dist = jnp.abs(BLOCK + qi - kj)
    kpos = (jnp.arange(nb)[:, None] - 1) * BLOCK + jnp.arange(3 * BLOCK)[None, :]
    valid = (dist[None] <= WINDOW) & (kpos[:, None, :] >= 0) & (kpos[:, None, :] < S)
    slopes = alibi_slopes(A_Q_HEADS).reshape(A_KV_HEADS, A_GROUP)
    s = s - slopes[:, :, None, None] * dist.astype(jnp.float32)
    s = jnp.where(valid[None, :, None, None], s, NEG_INF)
    sink_col = jnp.broadcast_to(sink.astype(jnp.float32).reshape(A_KV_HEADS, A_GROUP, 1, 1), s.shape[:-1] + (1,))
    p = jax.nn.softmax(jnp.concatenate([s, sink_col], axis=-1), axis=-1)[..., :-1]
    out = jnp.einsum('bnkgqs,bnskd->bnqkgd', p.astype(v.dtype), vb)
    return out.reshape(B, S, A_QW)


def diff_attention(q, k, v, lam, subln_g, lambda_init):
    B, S = q.shape[0], q.shape[1]
    nq = S // Q_BLOCK
    slopes = alibi_slopes(B_HEADS)
    kpos = jnp.arange(S, dtype=jnp.float32)
    qb = q.reshape(B, nq, Q_BLOCK, B_HEADS, 2, HEAD_DIM).transpose(1, 0, 2, 3, 4, 5)
    starts = jnp.arange(nq, dtype=jnp.int32) * Q_BLOCK

    def one_block(args):
        qblk, start = args
        s = jnp.einsum('bqhcd,bshcd->bhcqs', qblk, k, preferred_element_type=jnp.float32) * HEAD_DIM ** -0.5
        qpos = (start + jnp.arange(Q_BLOCK, dtype=jnp.int32)).astype(jnp.float32)
        dist = jnp.abs(qpos[:, None] - kpos[None, :])
        s = s - slopes[None, :, None, None, None] * dist[None, None, None]
        p = jax.nn.softmax(s, axis=-1)
        w = p[:, :, 0] - lam * p[:, :, 1]
        return jnp.einsum('bhqs,bshe->bqhe', w.astype(v.dtype), v)

    out = lax.map(one_block, (qb, starts))
    out = out.transpose(1, 0, 2, 3, 4).reshape(B, S, B_HEADS, B_VDIM)
    out = rmsnorm(out, subln_g) * (1.0 - lambda_init)
    return out.reshape(B, S, MIX_WIDTH)


def memory_attention(qm, mk, mv):
    s = jnp.einsum('bqhd,bmhd->bhqm', qm, mk, preferred_element_type=jnp.float32) * HEAD_DIM ** -0.5
    p = jax.nn.softmax(s, axis=-1)
    o = jnp.einsum('bhqm,bmhd->bqhd', p.astype(mv.dtype), mv)
    return o.reshape(qm.shape[0], qm.shape[1], MEM_WIDTH)


def swiglu(h, wg, wu, wd):
    return (jax.nn.silu(h @ wg) * (h @ wu)) @ wd


def moe_swiglu(x, w_router, w_gate, w_up, w_down):
    B, S, D = x.shape
    T = B * S
    A = T * TOP_K
    t = x.reshape(T, D)
    logits = jnp.einsum('td,de->te', t, w_router, preferred_element_type=jnp.float32)
    top_v, top_e = lax.top_k(logits, TOP_K)
    gate = jax.nn.softmax(top_v, axis=-1)
    e_flat = top_e.reshape(A)
    tok_flat = jnp.repeat(jnp.arange(T, dtype=jnp.int32), TOP_K)
    g_flat = gate.reshape(A)
    order = jnp.argsort(e_flat)
    e_s = e_flat[order]
    tok_s = tok_flat[order]
    g_s = g_flat[order]
    counts = jnp.bincount(e_flat, length=N_EXPERTS)
    starts = jnp.cumsum(counts) - counts
    padded = (counts + MOE_BLOCK - 1) // MOE_BLOCK * MOE_BLOCK
    pends = jnp.cumsum(padded)
    pstarts = pends - padded
    dest = pstarts[e_s] + (jnp.arange(A, dtype=jnp.int32) - starts[e_s])
    n_blocks = A // MOE_BLOCK + N_EXPERTS
    P = n_blocks * MOE_BLOCK
    tok_p = jnp.zeros((P,), jnp.int32).at[dest].set(tok_s)
    g_p = jnp.zeros((P,), jnp.float32).at[dest].set(g_s)
    blk_e = jnp.minimum(jnp.searchsorted(pends, jnp.arange(n_blocks, dtype=jnp.int32) * MOE_BLOCK, side='right'), N_EXPERTS - 1)

    def expert_block(args):
        idx, e = args
        xb = t[idx]
        return swiglu(xb, w_gate[e], w_up[e], w_down[e])

    yb = lax.map(expert_block, (tok_p.reshape(n_blocks, MOE_BLOCK), blk_e))
    y = jnp.zeros((T, D), jnp.float32).at[tok_p].add(yb.reshape(P, D).astype(jnp.float32) * g_p[:, None])
    return y.astype(x.dtype).reshape(B, S, D)


def setup_inputs(seed: int = 0) -> dict:
    key = jax.random.key(seed)
    ks = jax.random.split(key, 20)
    f32 = jnp.float32

    def w(k, shape, fan_in):
        return jax.random.normal(k, shape, f32) * fan_in ** -0.5

    x = jax.random.normal(ks[0], (BATCH, SEQ, D_MODEL), f32)
    mem = jax.random.normal(ks[1], (BATCH, N_MEM, D_MODEL), f32)
    norm_g = 1.0 + 0.05 * jax.random.normal(ks[2], (DEPTH, 4, D_MODEL), f32)
    mem_norm_g = 1.0 + 0.05 * jax.random.normal(ks[3], (D_MODEL,), f32)
    w_mem_kv = w(ks[4], (D_MODEL, 2 * MEM_WIDTH), D_MODEL)
    a_w_in = w(ks[5], (N_EVEN, D_MODEL, A_IN), D_MODEL)
    a_sink = 0.5 * jax.random.normal(ks[6], (N_EVEN, A_Q_HEADS), f32)
    a_w_out = w(ks[7], (N_EVEN, OUT_IN, D_MODEL), OUT_IN)
    b_w_in = w(ks[8], (N_ODD, D_MODEL, B_IN), D_MODEL)
    b_lambda = 0.1 * jax.random.normal(ks[9], (N_ODD, 4, HEAD_DIM), f32)
    b_subln_g = 1.0 + 0.05 * jax.random.normal(ks[10], (N_ODD, B_VDIM), f32)
    b_w_out = w(ks[11], (N_ODD, OUT_IN, D_MODEL), OUT_IN)
    ffn_w_gate = w(ks[12], (N_EVEN, D_MODEL, D_FF), D_MODEL)
    ffn_w_up = w(ks[13], (N_EVEN, D_MODEL, D_FF), D_MODEL)
    ffn_w_down = w(ks[14], (N_EVEN, D_FF, D_MODEL), D_FF)
    moe_w_router = w(ks[15], (N_ODD, D_MODEL, N_EXPERTS), D_MODEL)
    moe_w_gate = w(ks[16], (N_ODD, N_EXPERTS, D_MODEL, D_FF), D_MODEL)
    moe_w_up = w(ks[17], (N_ODD, N_EXPERTS, D_MODEL, D_FF), D_MODEL)
    moe_w_down = w(ks[18], (N_ODD, N_EXPERTS, D_FF, D_MODEL), D_FF)
    return {'x': x, 'mem': mem, 'norm_g': norm_g, 'mem_norm_g': mem_norm_g, 'w_mem_kv': w_mem_kv,
            'a_w_in': a_w_in, 'a_sink': a_sink, 'a_w_out': a_w_out,
            'b_w_in': b_w_in, 'b_lambda': b_lambda, 'b_subln_g': b_subln_g, 'b_w_out': b_w_out,
            'ffn_w_gate': ffn_w_gate, 'ffn_w_up': ffn_w_up, 'ffn_w_down': ffn_w_down,
            'moe_w_router': moe_w_router, 'moe_w_gate': moe_w_gate, 'moe_w_up': moe_w_up, 'moe_w_down': moe_w_down}


def reference(x, mem, norm_g, mem_norm_g, w_mem_kv,
              a_w_in, a_sink, a_w_out,
              b_w_in, b_lambda, b_subln_g, b_w_out,
              ffn_w_gate, ffn_w_up, ffn_w_down,
              moe_w_router, moe_w_gate, moe_w_up, moe_w_down):
    B, S, _ = x.shape
    memn = rmsnorm(mem, mem_norm_g)
    mkv = memn @ w_mem_kv
    n_mem = mem.shape[1]
    mk = mkv[..., :MEM_WIDTH].reshape(B, n_mem, MEM_HEADS, HEAD_DIM)
    mv = mkv[..., MEM_WIDTH:].reshape(B, n_mem, MEM_HEADS, HEAD_DIM)

    for i in range(DEPTH):
        j = i // 2
        g = norm_g[i]
        h = rmsnorm(x, g[0])
        if i % 2 == 0:
            proj = h @ a_w_in[j]
            q, k, v, qm = jnp.split(proj, [A_QW, A_QW + A_KVW, A_QW + 2 * A_KVW], axis=-1)
            mix = windowed_gqa_sink(q.reshape(B, S, A_Q_HEADS, HEAD_DIM),
                                    k.reshape(B, S, A_KV_HEADS, HEAD_DIM),
                                    v.reshape(B, S, A_KV_HEADS, HEAD_DIM), a_sink[j])
            w_out = a_w_out[j]
        else:
            proj = h @ b_w_in[j]
            q, k, v, qm = jnp.split(proj, [MIX_WIDTH, 2 * MIX_WIDTH, 3 * MIX_WIDTH], axis=-1)
            lambda_init = 0.8 - 0.6 * math.exp(-0.3 * i)
            lp = b_lambda[j].astype(jnp.float32)
            lam = jnp.exp(jnp.sum(lp[0] * lp[1])) - jnp.exp(jnp.sum(lp[2] * lp[3])) + lambda_init
            mix = diff_attention(q.reshape(B, S, B_HEADS, 2, HEAD_DIM),
                                 k.reshape(B, S, B_HEADS, 2, HEAD_DIM),
                                 v.reshape(B, S, B_HEADS, B_VDIM), lam, b_subln_g[j], lambda_init)
            w_out = b_w_out[j]
        mem_out = memory_attention(qm.reshape(B, S, MEM_HEADS, HEAD_DIM), mk, mv)
        o = jnp.concatenate([mix, mem_out], axis=-1) @ w_out
        x = x + rmsnorm(o, g[1])
        h = rmsnorm(x, g[2])
        if i % 2 == 0:
            f = swiglu(h, ffn_w_gate[j], ffn_w_up[j], ffn_w_down[j])
        else:
            f = moe_swiglu(h, moe_w_router[j], moe_w_gate[j], moe_w_up[j], moe_w_down[j])
        x = x + rmsnorm(f, g[3])
    return x
```

```python
import functools
import math

import jax
import jax.numpy as jnp
from jax import lax
from jax.experimental import pallas as pl
from jax.experimental.pallas import tpu as pltpu

D_MODEL = 1024
HEAD_DIM = 64
MIX_WIDTH = 768
MEM_WIDTH = 256
MEM_HEADS = 4
A_Q_HEADS = 12
A_KV_HEADS = 4
A_GROUP = 3
WINDOW = 128
B_HEADS = 6
B_VDIM = 128
D_FF = 3584
N_EXPERTS = 8
TOP_K = 2
EPS = 1e-6
NEG_INF = -1e30
LANES = 128
VMEM_LIMIT_BYTES = 48 * 1024 * 1024

BF16 = jnp.bfloat16
F32 = jnp.float32


def _params(*sem):
    return pltpu.CompilerParams(dimension_semantics=sem, vmem_limit_bytes=VMEM_LIMIT_BYTES)


def _rms(x, g):
    return x * lax.rsqrt(jnp.mean(x * x, axis=-1, keepdims=True) + EPS) * g


def _dot(a, b):
    return jnp.dot(a, b, preferred_element_type=F32)


def _dot_nt(a, b):
    return lax.dot_general(a, b, (((1,), (1,)), ((), ())), preferred_element_type=F32)


def _alibi_slopes(n):
    return [2.0 ** (-8.0 * (i + 1) / n) for i in range(n)]


def _norm_matmul_kernel(x_ref, g_ref, w_ref, o_ref, *, chunk):
    h = _rms(x_ref[...], g_ref[...]).astype(BF16)
    for c in range(o_ref.shape[1] // chunk):
        sl = slice(c * chunk, (c + 1) * chunk)
        o_ref[:, sl] = _dot(h, w_ref[:, sl]).astype(o_ref.dtype)


def _norm_matmul(x, g, w, tm=512, chunk=512):
    t, d = x.shape
    n = w.shape[1]
    tm = min(tm, t)
    return pl.pallas_call(
        functools.partial(_norm_matmul_kernel, chunk=min(chunk, n)),
        grid=(t // tm,),
        in_specs=[pl.BlockSpec((tm, d), lambda i: (i, 0)),
                  pl.BlockSpec((1, d), lambda i: (0, 0)),
                  pl.BlockSpec((d, n), lambda i: (0, 0))],
        out_specs=pl.BlockSpec((tm, n), lambda i: (i, 0)),
        out_shape=jax.ShapeDtypeStruct((t, n), BF16),
        compiler_params=_params("parallel"),
        name="norm_matmul",
    )(x, g.reshape(1, d), w)


def _window_kernel(sink_ref, q_ref, kp_ref, kc_ref, kn_ref, vp_ref, vc_ref, vn_ref, o_ref):
    n = pl.program_id(1)
    nb = pl.num_programs(1)
    blk = q_ref.shape[0]
    k = jnp.concatenate([kp_ref[...], kc_ref[...], kn_ref[...]], axis=0)
    v = jnp.concatenate([vp_ref[...], vc_ref[...], vn_ref[...]], axis=0)
    qi = lax.broadcasted_iota(jnp.int32, (blk, 3 * blk), 0)
    kj = lax.broadcasted_iota(jnp.int32, (blk, 3 * blk), 1)
    dist_i = jnp.abs(blk + qi - kj)
    valid = dist_i <= WINDOW
    valid = valid & jnp.logical_or(n > 0, kj >= blk)
    valid = valid & jnp.logical_or(n < nb - 1, kj < 2 * blk)
    dist = dist_i.astype(F32)
    q = q_ref[...] * 0.125
    slopes = _alibi_slopes(A_Q_HEADS)
    outs = []
    for h in range(A_Q_HEADS):
        kh = h // A_GROUP
        qh = q[:, h * HEAD_DIM:(h + 1) * HEAD_DIM]
        kk = k[:, kh * HEAD_DIM:(kh + 1) * HEAD_DIM]
        vv = v[:, kh * HEAD_DIM:(kh + 1) * HEAD_DIM]
        s = _dot_nt(qh, kk) - slopes[h] * dist
        s = jnp.where(valid, s, NEG_INF)
        sink = sink_ref[h]
        m = jnp.maximum(jnp.max(s, axis=-1, keepdims=True), sink)
        p = jnp.exp(s - m)
        denom = jnp.sum(p, axis=-1, keepdims=True) + jnp.exp(sink - m)
        o = _dot(p.astype(BF16), vv)
        outs.append(o * (1.0 / denom))
    o_ref[...] = jnp.concatenate(outs, axis=-1).astype(o_ref.dtype)


def _window_attention(proj, sink, batch, seq):
    blk = WINDOW
    nb = seq // blk
    kcol = MIX_WIDTH // 256
    vcol = kcol + 1

    def cur(b, n, s):
        return b * nb + n

    def prev(b, n, s):
        return b * nb + jnp.maximum(n - 1, 0)

    def nxt(b, n, s):
        return b * nb + jnp.minimum(n + 1, nb - 1)

    def spec(row, col):
        return pl.BlockSpec((blk, 256), lambda b, n, s: (row(b, n, s), col))

    return pl.pallas_call(
        _window_kernel,
        grid_spec=pltpu.PrefetchScalarGridSpec(
            num_scalar_prefetch=1,
            grid=(batch, nb),
            in_specs=[pl.BlockSpec((blk, MIX_WIDTH), lambda b, n, s: (b * nb + n, 0)),
                      spec(prev, kcol), spec(cur, kcol), spec(nxt, kcol),
                      spec(prev, vcol), spec(cur, vcol), spec(nxt, vcol)],
            out_specs=pl.BlockSpec((blk, MIX_WIDTH), lambda b, n, s: (b * nb + n, 0)),
        ),
        out_shape=jax.ShapeDtypeStruct((batch * seq, MIX_WIDTH), BF16),
        compiler_params=_params("parallel", "parallel"),
        name="window_attention",
    )(sink, proj, proj, proj, proj, proj, proj, proj)


def _mem_attn_kernel(q_ref, mkv_ref, o_ref):
    q = q_ref[...] * 0.125
    mkv = mkv_ref[...]
    outs = []
    for h in range(MEM_HEADS):
        sl = slice(h * HEAD_DIM, (h + 1) * HEAD_DIM)
        s = _dot_nt(q[:, sl], mkv[:, sl])
        m = jnp.max(s, axis=-1, keepdims=True)
        p = jnp.exp(s - m)
        denom = jnp.sum(p, axis=-1, keepdims=True)
        o = _dot(p.astype(BF16), mkv[:, MEM_WIDTH + h * HEAD_DIM:MEM_WIDTH + (h + 1) * HEAD_DIM])
        outs.append(o * (1.0 / denom))
    o_ref[...] = jnp.concatenate(outs, axis=-1).astype(o_ref.dtype)


def _mem_attention(proj, qcol, mkv, batch, seq, tm=512):
    tm = min(tm, seq)
    per_batch = seq // tm
    n_mem = mkv.shape[0] // batch
    return pl.pallas_call(
        _mem_attn_kernel,
        grid=(batch * per_batch,),
        in_specs=[pl.BlockSpec((tm, MEM_WIDTH), lambda i: (i, qcol)),
                  pl.BlockSpec((n_mem, 2 * MEM_WIDTH), lambda i: (i // per_batch, 0))],
        out_specs=pl.BlockSpec((tm, MEM_WIDTH), lambda i: (i, 0)),
        out_shape=jax.ShapeDtypeStruct((batch * seq, MEM_WIDTH), BF16),
        compiler_params=_params("parallel"),
        name="mem_attention",
    )(proj, mkv)


def _out_proj_kernel(mix_ref, mem_ref, w1_ref, w2_ref, x_ref, g_ref, o_ref):
    o = _dot(mix_ref[...], w1_ref[...]) + _dot(mem_ref[...], w2_ref[...])
    o_ref[...] = x_ref[...] + _rms(o, g_ref[...])


def _out_proj(mix, mem_out, w_out, x, g, tm=512):
    t, d = x.shape
    tm = min(tm, t)
    w1 = w_out[:MIX_WIDTH]
    w2 = w_out[MIX_WIDTH:]
    return pl.pallas_call(
        _out_proj_kernel,
        grid=(t // tm,),
        in_specs=[pl.BlockSpec((tm, MIX_WIDTH), lambda i: (i, 0)),
                  pl.BlockSpec((tm, MEM_WIDTH), lambda i: (i, 0)),
                  pl.BlockSpec((MIX_WIDTH, d), lambda i: (0, 0)),
                  pl.BlockSpec((MEM_WIDTH, d), lambda i: (0, 0)),
                  pl.BlockSpec((tm, d), lambda i: (i, 0)),
                  pl.BlockSpec((1, d), lambda i: (0, 0))],
        out_specs=pl.BlockSpec((tm, d), lambda i: (i, 0)),
        out_shape=jax.ShapeDtypeStruct((t, d), F32),
        compiler_params=_params("parallel"),
        name="out_proj",
    )(mix, mem_out, w1, w2, x, g.reshape(1, d))


def _swiglu_tile(h, wg, wu, wd):
    a = _dot(h, wg)
    u = _dot(h, wu)
    z = (a * jax.nn.sigmoid(a) * u).astype(BF16)
    return _dot(z, wd)


def _ffn_kernel(x_ref, gin_ref, gout_ref, wg_ref, wu_ref, wd_ref, o_ref, h_sc, acc_sc):
    f = pl.program_id(1)

    @pl.when(f == 0)
    def _():
        h_sc[...] = _rms(x_ref[...], gin_ref[...]).astype(BF16)
        acc_sc[...] = jnp.zeros_like(acc_sc)

    acc_sc[...] += _swiglu_tile(h_sc[...], wg_ref[...], wu_ref[...], wd_ref[...])

    @pl.when(f == pl.num_programs(1) - 1)
    def _():
        o_ref[...] = x_ref[...] + _rms(acc_sc[...], gout_ref[...])


def _ffn(x, g_in, g_out, wg, wu, wd, tm=512, tf=512):
    t, d = x.shape
    ff = wg.shape[1]
    tm = min(tm, t)
    return pl.pallas_call(
        _ffn_kernel,
        grid=(t // tm, ff // tf),
        in_specs=[pl.BlockSpec((tm, d), lambda i, f: (i, 0)),
                  pl.BlockSpec((1, d), lambda i, f: (0, 0)),
                  pl.BlockSpec((1, d), lambda i, f: (0, 0)),
                  pl.BlockSpec((d, tf), lambda i, f: (0, f)),
                  pl.BlockSpec((d, tf), lambda i, f: (0, f)),
                  pl.BlockSpec((tf, d), lambda i, f: (f, 0))],
        out_specs=pl.BlockSpec((tm, d), lambda i, f: (i, 0)),
        out_shape=jax.ShapeDtypeStruct((t, d), F32),
        scratch_shapes=[pltpu.VMEM((tm, d), BF16), pltpu.VMEM((tm, d), F32)],
        compiler_params=_params("parallel", "arbitrary"),
        name="ffn",
    )(x, g_in.reshape(1, d), g_out.reshape(1, d), wg, wu, wd)


def _diff_attn_kernel(slope_ref, q_ref, k_ref, v_ref, lam_ref, g_ref, o_ref,
                      m_sc, l_sc, acc_sc, *, tk, out_scale, lambda_init):
    h = pl.program_id(1)
    i = pl.program_id(2)
    tq = q_ref.shape[0]
    seq = k_ref.shape[0]
    slope = slope_ref[h]
    q = q_ref[...] * 0.125
    lane = lax.broadcasted_iota(jnp.int32, q.shape, 1)
    zero = jnp.zeros_like(q)
    qc = (jnp.where(lane < HEAD_DIM, q, zero), jnp.where(lane >= HEAD_DIM, q, zero))
    qpos = i * tq + lax.broadcasted_iota(jnp.int32, (tq, tk), 0)
    kiota = lax.broadcasted_iota(jnp.int32, (tq, tk), 1)
    m_sc[...] = jnp.full_like(m_sc, -jnp.inf)
    l_sc[...] = jnp.zeros_like(l_sc)
    acc_sc[...] = jnp.zeros_like(acc_sc)

    def body(j, carry):
        start = pl.multiple_of(j * tk, tk)
        k = k_ref[pl.ds(start, tk), :]
        v = v_ref[pl.ds(start, tk), :]
        bias = slope * jnp.abs(qpos - (start + kiota)).astype(F32)
        for c in range(2):
            s = _dot_nt(qc[c], k) - bias
            m_prev = m_sc[c]
            m_new = jnp.maximum(m_prev, jnp.max(s, axis=-1, keepdims=True))
            alpha = jnp.exp(m_prev - m_new)
            p = jnp.exp(s - m_new)
            l_sc[c] = alpha * l_sc[c] + jnp.sum(p, axis=-1, keepdims=True)
            acc_sc[c] = alpha * acc_sc[c] + _dot(p.astype(BF16), v)
            m_sc[c] = m_new
        return carry

    lax.fori_loop(0, seq // tk, body, 0)

    lp = lam_ref[...]
    lam = (jnp.exp(jnp.sum(lp[0:1] * lp[1:2], axis=-1, keepdims=True))
           - jnp.exp(jnp.sum(lp[2:3] * lp[3:4], axis=-1, keepdims=True)) + lambda_init)
    o = acc_sc[0] * (1.0 / l_sc[0]) - lam * (acc_sc[1] * (1.0 / l_sc[1]))
    o_ref[...] = (_rms(o, g_ref[...]) * out_scale).astype(o_ref.dtype)


def _diff_attention(proj, b_lambda, subln_g, lambda_init, batch, seq, tq=512, tk=512):
    tq = min(tq, seq)
    tk = min(tk, seq)
    nq = seq // tq
    kcol = MIX_WIDTH // B_VDIM
    vcol = 2 * kcol
    slopes = jnp.asarray(_alibi_slopes(B_HEADS), F32)
    kernel = functools.partial(_diff_attn_kernel, tk=tk, out_scale=1.0 - lambda_init,
                               lambda_init=lambda_init)
    return pl.pallas_call(
        kernel,
        grid_spec=pltpu.PrefetchScalarGridSpec(
            num_scalar_prefetch=1,
            grid=(batch, B_HEADS, nq),
            in_specs=[pl.BlockSpec((tq, B_VDIM), lambda b, h, i, s: (b * nq + i, h)),
                      pl.BlockSpec((seq, B_VDIM), lambda b, h, i, s: (b, kcol + h)),
                      pl.BlockSpec((seq, B_VDIM), lambda b, h, i, s: (b, vcol + h)),
                      pl.BlockSpec((4, HEAD_DIM), lambda b, h, i, s: (0, 0)),
                      pl.BlockSpec((1, B_VDIM), lambda b, h, i, s: (0, 0))],
            out_specs=pl.BlockSpec((tq, B_VDIM), lambda b, h, i, s: (b * nq + i, h)),
            scratch_shapes=[pltpu.VMEM((2, tq, 1), F32), pltpu.VMEM((2, tq, 1), F32),
                            pltpu.VMEM((2, tq, B_VDIM), F32)],
        ),
        out_shape=jax.ShapeDtypeStruct((batch * seq, MIX_WIDTH), BF16),
        compiler_params=_params("parallel", "parallel", "parallel"),
        name="diff_attention",
    )(slopes, proj, proj, proj, b_lambda, subln_g.reshape(1, B_VDIM))


def _router_kernel(x_ref, g_ref, whi_ref, wlo_ref, h_ref, e_ref, gate_ref):
    h = _rms(x_ref[...], g_ref[...])
    h_hi = h.astype(BF16)
    h_lo = (h - h_hi.astype(F32)).astype(BF16)
    whi = whi_ref[...]
    logits = _dot(h_hi, whi) + (_dot(h_hi, wlo_ref[...]) + _dot(h_lo, whi))
    lane = lax.broadcasted_iota(jnp.int32, logits.shape, 1)
    logits = jnp.where(lane < N_EXPERTS, logits, -jnp.inf)
    m1 = jnp.max(logits, axis=-1, keepdims=True)
    i1 = jnp.min(jnp.where(logits == m1, lane, LANES), axis=-1, keepdims=True)
    rest = jnp.where(lane == i1, -jnp.inf, logits)
    m2 = jnp.max(rest, axis=-1, keepdims=True)
    i2 = jnp.min(jnp.where(rest == m2, lane, LANES), axis=-1, keepdims=True)
    r = jnp.exp(m2 - m1)
    g1 = 1.0 / (1.0 + r)
    g2 = r * g1
    e_ref[...] = jnp.where(lane == 0, i1, jnp.where(lane == 1, i2, 0))
    gate_ref[...] = jnp.where(lane == 0, g1, jnp.where(lane == 1, g2, 0.0))
    h_ref[...] = h


def _router(x, g, w_router, tm=512):
    t, d = x.shape
    tm = min(tm, t)
    wpad = jnp.zeros((d, LANES), F32).at[:, :N_EXPERTS].set(w_router)
    whi = wpad.astype(BF16)
    wlo = (wpad - whi.astype(F32)).astype(BF16)
    return pl.pallas_call(
        _router_kernel,
        grid=(t // tm,),
        in_specs=[pl.BlockSpec((tm, d), lambda i: (i, 0)),
                  pl.BlockSpec((1, d), lambda i: (0, 0)),
                  pl.BlockSpec((d, LANES), lambda i: (0, 0)),
                  pl.BlockSpec((d, LANES), lambda i: (0, 0))],
        out_specs=[pl.BlockSpec((tm, d), lambda i: (i, 0)),
                   pl.BlockSpec((tm, LANES), lambda i: (i, 0)),
                   pl.BlockSpec((tm, LANES), lambda i: (i, 0))],
        out_shape=[jax.ShapeDtypeStruct((t, d), F32),
                   jax.ShapeDtypeStruct((t, LANES), jnp.int32),
                   jax.ShapeDtypeStruct((t, LANES), F32)],
        compiler_params=_params("parallel"),
        name="router",
    )(x, g.reshape(1, d), whi, wlo)


def _moe_kernel(tok_ref, blk_e_ref, nused_ref, h_hbm, wg_ref, wu_ref, wd_ref, o_ref,
                x_sc, xb_sc, acc_sc, sem):
    i = pl.program_id(0)
    f = pl.program_id(1)
    tm = x_sc.shape[0]
    used = i < nused_ref[0]

    def row_copy(t, r):
        return pltpu.make_async_copy(h_hbm.at[pl.ds(t, 1)], x_sc.at[pl.ds(r, 1)], sem)

    @pl.when(jnp.logical_and(used, f == 0))
    def _():
        def start(r, c):
            row_copy(tok_ref[i * tm + r], r).start()
            return c

        lax.fori_loop(0, tm, start, 0)

        def wait(r, c):
            row_copy(0, r).wait()
            return c

        lax.fori_loop(0, tm, wait, 0)
        xb_sc[...] = x_sc[...].astype(BF16)
        acc_sc[...] = jnp.zeros_like(acc_sc)

    @pl.when(used)
    def _():
        acc_sc[...] += _swiglu_tile(xb_sc[...], wg_ref[...], wu_ref[...], wd_ref[...])

    @pl.when(f == pl.num_programs(1) - 1)
    def _():
        o_ref[...] = jnp.where(used, acc_sc[...], 0.0)


def _moe_experts(h, tok_p, blk_e, nused, wg, wu, wd, tm, tf=512):
    t, d = h.shape
    p = tok_p.shape[0]
    nblk = p // tm
    nf = wg.shape[2] // tf

    def fidx(i, f, nused_ref):
        return jnp.where(i < nused_ref[0], f, nf - 1)

    return pl.pallas_call(
        _moe_kernel,
        grid_spec=pltpu.PrefetchScalarGridSpec(
            num_scalar_prefetch=3,
            grid=(nblk, nf),
            in_specs=[pl.BlockSpec(memory_space=pl.ANY),
                      pl.BlockSpec((None, d, tf), lambda i, f, tok, be, nu: (be[i], 0, fidx(i, f, nu))),
                      pl.BlockSpec((None, d, tf), lambda i, f, tok, be, nu: (be[i], 0, fidx(i, f, nu))),
                      pl.BlockSpec((None, tf, d), lambda i, f, tok, be, nu: (be[i], fidx(i, f, nu), 0))],
            out_specs=pl.BlockSpec((tm, d), lambda i, f, tok, be, nu: (i, 0)),
            scratch_shapes=[pltpu.VMEM((tm, d), F32), pltpu.VMEM((tm, d), BF16),
                            pltpu.VMEM((tm, d), F32), pltpu.SemaphoreType.DMA(())],
        ),
        out_shape=jax.ShapeDtypeStruct((p, d), F32),
        compiler_params=_params("arbitrary", "arbitrary"),
        name="moe_experts",
    )(tok_p, blk_e, nused, h, wg, wu, wd)


def _combine_kernel(dest_ref, ys_hbm, gate_ref, x_ref, g_ref, o_ref, r_sc, sem):
    i = pl.program_id(0)
    tc = x_ref.shape[0]

    def row_copy(src, k, r):
        return pltpu.make_async_copy(ys_hbm.at[pl.ds(src, 1)], r_sc.at[k, pl.ds(r, 1)], sem)

    def start(r, c):
        a = TOP_K * (i * tc + r)
        row_copy(dest_ref[a], 0, r).start()
        row_copy(dest_ref[a + 1], 1, r).start()
        return c

    lax.fori_loop(0, tc, start, 0)

    def wait(r, c):
        row_copy(0, 0, r).wait()
        row_copy(0, 1, r).wait()
        return c

    lax.fori_loop(0, tc, wait, 0)
    gate = gate_ref[...]
    y = gate[:, 0:1] * r_sc[0] + gate[:, 1:2] * r_sc[1]
    o_ref[...] = x_ref[...] + _rms(y, g_ref[...])


def _combine(ys, dest, gates, x, g, tc=256):
    t, d = x.shape
    tc = min(tc, t)
    return pl.pallas_call(
        _combine_kernel,
        grid_spec=pltpu.PrefetchScalarGridSpec(
            num_scalar_prefetch=1,
            grid=(t // tc,),
            in_specs=[pl.BlockSpec(memory_space=pl.ANY),
                      pl.BlockSpec((tc, LANES), lambda i, dst: (i, 0)),
                      pl.BlockSpec((tc, d), lambda i, dst: (i, 0)),
                      pl.BlockSpec((1, d), lambda i, dst: (0, 0))],
            out_specs=pl.BlockSpec((tc, d), lambda i, dst: (i, 0)),
            scratch_shapes=[pltpu.VMEM((TOP_K, tc, d), F32), pltpu.SemaphoreType.DMA(())],
        ),
        out_shape=jax.ShapeDtypeStruct((t, d), F32),
        compiler_params=_params("arbitrary"),
        name="moe_combine",
    )(dest, ys, gates, x, g.reshape(1, d))


def _route(top_e, tm):
    t = top_e.shape[0]
    a = t * TOP_K
    e_flat = top_e.reshape(a)
    onehot = (e_flat[:, None] == jnp.arange(N_EXPERTS, dtype=jnp.int32)[None, :]).astype(jnp.int32)
    csum = jnp.cumsum(onehot, axis=0)
    rank = jnp.sum(onehot * (csum - 1), axis=1)
    counts = csum[-1]
    padded = (counts + tm - 1) // tm * tm
    pends = jnp.cumsum(padded)
    pstarts = pends - padded
    dest = (pstarts[e_flat] + rank).astype(jnp.int32)
    nblk = a // tm + N_EXPERTS
    tok_p = jnp.zeros((nblk * tm,), jnp.int32).at[dest].set(jnp.arange(a, dtype=jnp.int32) // TOP_K)
    blk_e = jnp.minimum(jnp.searchsorted(pends, jnp.arange(nblk, dtype=jnp.int32) * tm, side='right'),
                        N_EXPERTS - 1).astype(jnp.int32)
    nused = (pends[-1:] // tm).astype(jnp.int32)
    return dest, tok_p, blk_e, nused


def _moe(x, g_in, g_out, w_router, wg, wu, wd, tm=512):
    h, e_pad, gates = _router(x, g_in, w_router)
    dest, tok_p, blk_e, nused = _route(e_pad[:, :TOP_K], tm)
    ys = _moe_experts(h, tok_p, blk_e, nused, wg, wu, wd, tm)
    return _combine(ys, dest, gates, x, g_out)


def kernel(x, mem, norm_g, mem_norm_g, w_mem_kv, a_w_in, a_sink, a_w_out, b_w_in, b_lambda,
           b_subln_g, b_w_out, ffn_w_gate, ffn_w_up, ffn_w_down, moe_w_router, moe_w_gate,
           moe_w_up, moe_w_down):
    batch, seq, d = x.shape
    n_mem = mem.shape[1]
    bf = lambda w: w.astype(BF16)
    xt = x.reshape(batch * seq, d)
    mkv = _norm_matmul(mem.reshape(batch * n_mem, d), mem_norm_g, bf(w_mem_kv))

    g = norm_g[0]
    proj = _norm_matmul(xt, g[0], bf(a_w_in[0]))
    mix = _window_attention(proj, a_sink[0], batch, seq)
    mem_out = _mem_attention(proj, (A_Q_HEADS + 2 * A_KV_HEADS) * HEAD_DIM // MEM_WIDTH, mkv, batch, seq)
    xt = _out_proj(mix, mem_out, bf(a_w_out[0]), xt, g[1])
    xt = _ffn(xt, g[2], g[3], bf(ffn_w_gate[0]), bf(ffn_w_up[0]), bf(ffn_w_down[0]))

    g = norm_g[1]
    lambda_init = 0.8 - 0.6 * math.exp(-0.3 * 1)
    proj = _norm_matmul(xt, g[0], bf(b_w_in[0]))
    mix = _diff_attention(proj, b_lambda[0], b_subln_g[0], lambda_init, batch, seq)
    mem_out = _mem_attention(proj, 3 * MIX_WIDTH // MEM_WIDTH, mkv, batch, seq)
    xt = _out_proj(mix, mem_out, bf(b_w_out[0]), xt, g[1])
    xt = _moe(xt, g[2], g[3], moe_w_router[0], bf(moe_w_gate[0]), bf(moe_w_up[0]), bf(moe_w_down[0]))
    return xt.reshape(batch, seq, d)
```

```python
import functools
import math

import jax
import jax.numpy as jnp
import numpy as np
from jax import lax
from jax.experimental import pallas as pl
from jax.experimental.pallas import tpu as pltpu

D_MODEL = 1024
HEAD_DIM = 64
MIX_WIDTH = 768
MEM_WIDTH = 256
MEM_HEADS = 4
A_Q_HEADS = 12
A_KV_HEADS = 4
A_GROUP = 3
WINDOW = 128
B_HEADS = 6
B_VDIM = 128
D_FF = 3584
N_EXPERTS = 8
TOP_K = 2
EPS = 1e-6
NEG_INF = -1e30
LOG2E = math.log2(math.e)
QK_SCALE = HEAD_DIM ** -0.5 * LOG2E
LANES = 128
VMEM_LIMIT_BYTES = 48 * 1024 * 1024

BF16 = jnp.bfloat16
F32 = jnp.float32


def _params(*sem):
    return pltpu.CompilerParams(dimension_semantics=sem, vmem_limit_bytes=VMEM_LIMIT_BYTES)


def _rms(x, g):
    return x * lax.rsqrt(jnp.mean(x * x, axis=-1, keepdims=True) + EPS) * g


def _dot(a, b):
    return jnp.dot(a, b, preferred_element_type=F32)


def _dot_nt(a, b):
    return lax.dot_general(a, b, (((1,), (1,)), ((), ())), preferred_element_type=F32)


def _alibi_slopes(n):
    return [2.0 ** (-8.0 * (i + 1) / n) for i in range(n)]


def _norm_matmul_kernel(x_ref, g_ref, w_ref, cs_ref, o_ref, *, chunk):
    h = _rms(x_ref[...], g_ref[...]).astype(BF16)
    for c in range(o_ref.shape[1] // chunk):
        sl = slice(c * chunk, (c + 1) * chunk)
        o_ref[:, sl] = (_dot(h, w_ref[:, sl]) * cs_ref[:, sl]).astype(o_ref.dtype)


def _norm_matmul(x, g, w, col_scale, tm=512, chunk=512):
    t, d = x.shape
    n = w.shape[1]
    tm = min(tm, t)
    return pl.pallas_call(
        functools.partial(_norm_matmul_kernel, chunk=min(chunk, n)),
        grid=(t // tm,),
        in_specs=[pl.BlockSpec((tm, d), lambda i: (i, 0)),
                  pl.BlockSpec((1, d), lambda i: (0, 0)),
                  pl.BlockSpec((d, n), lambda i: (0, 0)),
                  pl.BlockSpec((1, n), lambda i: (0, 0))],
        out_specs=pl.BlockSpec((tm, n), lambda i: (i, 0)),
        out_shape=jax.ShapeDtypeStruct((t, n), BF16),
        compiler_params=_params("parallel"),
        name="norm_matmul",
    )(x, g.reshape(1, d), w, col_scale.reshape(1, n))


def _query_col_scale(n, mix_q_cols):
    cols = np.arange(n)
    return jnp.asarray(np.where((cols < mix_q_cols) | (cols >= n - MEM_WIDTH), QK_SCALE, 1.0), F32)


def _window_kernel(sink_ref, q_ref, kp_ref, kc_ref, kn_ref, vp_ref, vc_ref, vn_ref, o_ref):
    n = pl.program_id(1)
    nb = pl.num_programs(1)
    blk = q_ref.shape[0]
    k = jnp.concatenate([kp_ref[...], kc_ref[...], kn_ref[...]], axis=0)
    v = jnp.concatenate([vp_ref[...], vc_ref[...], vn_ref[...]], axis=0)
    qi = lax.broadcasted_iota(jnp.int32, (blk, 3 * blk), 0)
    kj = lax.broadcasted_iota(jnp.int32, (blk, 3 * blk), 1)
    dist_i = jnp.abs(blk + qi - kj)
    valid = dist_i <= WINDOW
    valid = valid & jnp.logical_or(n > 0, kj >= blk)
    valid = valid & jnp.logical_or(n < nb - 1, kj < 2 * blk)
    dist = dist_i.astype(F32)
    q = q_ref[...]
    slopes = _alibi_slopes(A_Q_HEADS)
    outs = []
    for h in range(A_Q_HEADS):
        kh = h // A_GROUP
        qh = q[:, h * HEAD_DIM:(h + 1) * HEAD_DIM]
        kk = k[:, kh * HEAD_DIM:(kh + 1) * HEAD_DIM]
        vv = v[:, kh * HEAD_DIM:(kh + 1) * HEAD_DIM]
        s = _dot_nt(qh, kk) - (slopes[h] * LOG2E) * dist
        s = jnp.where(valid, s, NEG_INF)
        sink = sink_ref[h] * LOG2E
        m = jnp.maximum(jnp.max(s, axis=-1, keepdims=True), sink)
        p = jnp.exp2(s - m)
        denom = jnp.sum(p, axis=-1, keepdims=True) + jnp.exp2(sink - m)
        o = _dot(p.astype(BF16), vv)
        outs.append(o * (1.0 / denom))
    o_ref[...] = jnp.concatenate(outs, axis=-1).astype(o_ref.dtype)


def _window_attention(proj, sink, batch, seq):
    blk = WINDOW
    nb = seq // blk
    kcol = MIX_WIDTH // 256
    vcol = kcol + 1

    def cur(b, n, s):
        return b * nb + n

    def prev(b, n, s):
        return b * nb + jnp.maximum(n - 1, 0)

    def nxt(b, n, s):
        return b * nb + jnp.minimum(n + 1, nb - 1)

    def spec(row, col):
        return pl.BlockSpec((blk, 256), lambda b, n, s: (row(b, n, s), col))

    return pl.pallas_call(
        _window_kernel,
        grid_spec=pltpu.PrefetchScalarGridSpec(
            num_scalar_prefetch=1,
            grid=(batch, nb),
            in_specs=[pl.BlockSpec((blk, MIX_WIDTH), lambda b, n, s: (b * nb + n, 0)),
                      spec(prev, kcol), spec(cur, kcol), spec(nxt, kcol),
                      spec(prev, vcol), spec(cur, vcol), spec(nxt, vcol)],
            out_specs=pl.BlockSpec((blk, MIX_WIDTH), lambda b, n, s: (b * nb + n, 0)),
        ),
        out_shape=jax.ShapeDtypeStruct((batch * seq, MIX_WIDTH), BF16),
        compiler_params=_params("parallel", "parallel"),
        name="window_attention",
    )(sink, proj, proj, proj, proj, proj, proj, proj)


def _mem_attn_kernel(q_ref, mkv_ref, o_ref):
    q = q_ref[...]
    mkv = mkv_ref[...]
    outs = []
    for h in range(MEM_HEADS):
        sl = slice(h * HEAD_DIM, (h + 1) * HEAD_DIM)
        s = _dot_nt(q[:, sl], mkv[:, sl])
        m = jnp.max(s, axis=-1, keepdims=True)
        p = jnp.exp2(s - m)
        denom = jnp.sum(p, axis=-1, keepdims=True)
        o = _dot(p.astype(BF16), mkv[:, MEM_WIDTH + h * HEAD_DIM:MEM_WIDTH + (h + 1) * HEAD_DIM])
        outs.append(o * (1.0 / denom))
    o_ref[...] = jnp.concatenate(outs, axis=-1).astype(o_ref.dtype)


def _mem_attention(proj, qcol, mkv, batch, seq, tm=512):
    tm = min(tm, seq)
    per_batch = seq // tm
    n_mem = mkv.shape[0] // batch
    return pl.pallas_call(
        _mem_attn_kernel,
        grid=(batch * per_batch,),
        in_specs=[pl.BlockSpec((tm, MEM_WIDTH), lambda i: (i, qcol)),
                  pl.BlockSpec((n_mem, 2 * MEM_WIDTH), lambda i: (i // per_batch, 0))],
        out_specs=pl.BlockSpec((tm, MEM_WIDTH), lambda i: (i, 0)),
        out_shape=jax.ShapeDtypeStruct((batch * seq, MEM_WIDTH), BF16),
        compiler_params=_params("parallel"),
        name="mem_attention",
    )(proj, mkv)


def _out_proj_kernel(mix_ref, mem_ref, w1_ref, w2_ref, x_ref, g_ref, o_ref):
    o = _dot(mix_ref[...], w1_ref[...]) + _dot(mem_ref[...], w2_ref[...])
    o_ref[...] = x_ref[...] + _rms(o, g_ref[...])


def _out_proj(mix, mem_out, w_out, x, g, tm=512):
    t, d = x.shape
    tm = min(tm, t)
    w1 = w_out[:MIX_WIDTH]
    w2 = w_out[MIX_WIDTH:]
    return pl.pallas_call(
        _out_proj_kernel,
        grid=(t // tm,),
        in_specs=[pl.BlockSpec((tm, MIX_WIDTH), lambda i: (i, 0)),
                  pl.BlockSpec((tm, MEM_WIDTH), lambda i: (i, 0)),
                  pl.BlockSpec((MIX_WIDTH, d), lambda i: (0, 0)),
                  pl.BlockSpec((MEM_WIDTH, d), lambda i: (0, 0)),
                  pl.BlockSpec((tm, d), lambda i: (i, 0)),
                  pl.BlockSpec((1, d), lambda i: (0, 0))],
        out_specs=pl.BlockSpec((tm, d), lambda i: (i, 0)),
        out_shape=jax.ShapeDtypeStruct((t, d), F32),
        compiler_params=_params("parallel"),
        name="out_proj",
    )(mix, mem_out, w1, w2, x, g.reshape(1, d))


def _swiglu_tile(h, wg, wu, wd):
    a = _dot(h, wg)
    u = _dot(h, wu)
    z = (a * jax.nn.sigmoid(a) * u).astype(BF16)
    return _dot(z, wd)


def _ffn_kernel(x_ref, gin_ref, gout_ref, wg_ref, wu_ref, wd_ref, o_ref, h_sc, acc_sc):
    f = pl.program_id(1)

    @pl.when(f == 0)
    def _():
        h_sc[...] = _rms(x_ref[...], gin_ref[...]).astype(BF16)
        acc_sc[...] = jnp.zeros_like(acc_sc)

    acc_sc[...] += _swiglu_tile(h_sc[...], wg_ref[...], wu_ref[...], wd_ref[...])

    @pl.when(f == pl.num_programs(1) - 1)
    def _():
        o_ref[...] = x_ref[...] + _rms(acc_sc[...], gout_ref[...])


def _ffn(x, g_in, g_out, wg, wu, wd, tm=512, tf=512):
    t, d = x.shape
    ff = wg.shape[1]
    tm = min(tm, t)
    return pl.pallas_call(
        _ffn_kernel,
        grid=(t // tm, ff // tf),
        in_specs=[pl.BlockSpec((tm, d), lambda i, f: (i, 0)),
                  pl.BlockSpec((1, d), lambda i, f: (0, 0)),
                  pl.BlockSpec((1, d), lambda i, f: (0, 0)),
                  pl.BlockSpec((d, tf), lambda i, f: (0, f)),
                  pl.BlockSpec((d, tf), lambda i, f: (0, f)),
                  pl.BlockSpec((tf, d), lambda i, f: (f, 0))],
        out_specs=pl.BlockSpec((tm, d), lambda i, f: (i, 0)),
        out_shape=jax.ShapeDtypeStruct((t, d), F32),
        scratch_shapes=[pltpu.VMEM((tm, d), BF16), pltpu.VMEM((tm, d), F32)],
        compiler_params=_params("parallel", "arbitrary"),
        name="ffn",
    )(x, g_in.reshape(1, d), g_out.reshape(1, d), wg, wu, wd)


_POS_SPLIT = 256
_LEFT, _DIAG, _RIGHT = 0, 1, 2


def _diff_attn_kernel(slope_ref, q_ref, k_ref, v_ref, qf_ref, kf_ref, dist_ref, lam_ref, g_ref,
                      o_ref, qa_sc, ka_sc, va_sc, m_sc, acc_sc, sa_sc, sb_sc, *, tk, out_scale,
                      lambda_init):
    h = pl.program_id(1)
    i = pl.program_id(2)
    tq = q_ref.shape[0]
    seq = k_ref.shape[0]
    nk = seq // tk
    slope = slope_ref[h]

    @pl.when(i == 0)
    def _():
        ka_sc[:, :B_VDIM] = k_ref[...]
        va_sc[:, :B_VDIM] = v_ref[...]
        lane = lax.broadcasted_iota(jnp.int32, (seq, LANES), 1)
        va_sc[:, B_VDIM:] = jnp.where(lane == 0, 1.0, 0.0).astype(BF16)
        kf = kf_ref[...]
        for c in range(nk):
            ka_sc[c * tk:(c + 1) * tk, B_VDIM:] = kf

    q = q_ref[...]
    lane = lax.broadcasted_iota(jnp.int32, q.shape, 1)
    zero = jnp.zeros_like(q)
    q0 = jnp.where(lane < HEAD_DIM, q, zero)
    q1 = jnp.where(lane >= HEAD_DIM, q, zero)
    qf = qf_ref[...]
    for variant, feat in ((_LEFT, -qf), (_DIAG, zero), (_RIGHT, qf)):
        qa_sc[variant, :tq, :B_VDIM] = q0
        qa_sc[variant, tq:, :B_VDIM] = q1
        qa_sc[variant, :tq, B_VDIM:] = feat
        qa_sc[variant, tq:, B_VDIM:] = feat
    m_sc[...] = jnp.full_like(m_sc, NEG_INF)
    acc_sc[...] = jnp.zeros_like(acc_sc)

    def scores(j, s_ref):
        variant = jnp.where(j < i, _LEFT, jnp.where(j == i, _DIAG, _RIGHT))
        start = pl.multiple_of(j * tk, tk)
        s_ref[...] = _dot_nt(qa_sc[variant], ka_sc[pl.ds(start, tk), :])

    def diag_bias(j, s_ref):
        @pl.when(j == i)
        def _():
            bias = slope * dist_ref[...]
            s_ref[:tq] -= bias
            s_ref[tq:] -= bias

    def softmax_pv(j, s_ref):
        start = pl.multiple_of(j * tk, tk)
        shift = slope * (jnp.abs(j - i) * tk).astype(F32)
        s = s_ref[...]
        m_prev = m_sc[...]
        m_new = jnp.maximum(m_prev, jnp.max(s, axis=-1, keepdims=True) - shift)
        p = jnp.exp2(s - (m_new + shift)).astype(BF16)
        acc_sc[...] = jnp.exp2(m_prev - m_new) * acc_sc[...] + _dot(p, va_sc[pl.ds(start, tk), :])
        m_sc[...] = m_new

    scores(0, sa_sc)
    diag_bias(0, sa_sc)

    def pair(jj, carry):
        j = 2 * jj
        scores(j + 1, sb_sc)
        softmax_pv(j, sa_sc)
        diag_bias(j + 1, sb_sc)
        nxt = jnp.minimum(j + 2, nk - 1)
        scores(nxt, sa_sc)
        softmax_pv(j + 1, sb_sc)
        diag_bias(nxt, sa_sc)
        return carry

    lax.fori_loop(0, nk // 2, pair, 0)

    lp = lam_ref[...]
    lam = (jnp.exp(jnp.sum(lp[0:1] * lp[1:2], axis=-1, keepdims=True))
           - jnp.exp(jnp.sum(lp[2:3] * lp[3:4], axis=-1, keepdims=True)) + lambda_init)
    acc = acc_sc[...]
    o0 = acc[:tq, :B_VDIM] * (1.0 / acc[:tq, B_VDIM:B_VDIM + 1])
    o1 = acc[tq:, :B_VDIM] * (1.0 / acc[tq:, B_VDIM:B_VDIM + 1])
    o_ref[...] = (_rms(o0 - lam * o1, g_ref[...]) * out_scale).astype(o_ref.dtype)


def _bf16_pieces(x, n=3):
    out = []
    r = np.float64(x)
    for _ in range(n):
        p = np.float64(np.float32(r).astype(jnp.bfloat16).astype(np.float32))
        out.append(p)
        r = r - p
    return out


def _alibi_features(slopes2, t):
    pos = np.arange(t)
    hi = (pos // _POS_SPLIT) * _POS_SPLIT
    lo = pos % _POS_SPLIT
    qf = np.zeros((len(slopes2), t, LANES), np.float32)
    kf = np.zeros((len(slopes2), t, LANES), np.float32)
    for h, s in enumerate(slopes2):
        for n, piece in enumerate(_bf16_pieces(s)):
            for base, part in ((0, hi), (3, lo)):
                qf[h, :, base + n] = -piece
                kf[h, :, base + n] = part
                qf[h, :, 6 + base + n] = part
                kf[h, :, 6 + base + n] = piece
    return jnp.asarray(qf, BF16), jnp.asarray(kf, BF16)


def _diff_attention(proj, b_lambda, subln_g, lambda_init, batch, seq, t=512):
    t = min(t, seq // 2)
    nq = seq // t
    assert seq % (2 * t) == 0
    kcol = MIX_WIDTH // B_VDIM
    vcol = 2 * kcol
    slopes2 = [s * LOG2E for s in _alibi_slopes(B_HEADS)]
    qf, kf = _alibi_features(slopes2, t)
    pos = np.arange(t)
    dist = jnp.asarray(np.abs(pos[:, None] - pos[None, :]), F32)
    kernel = functools.partial(_diff_attn_kernel, tk=t, out_scale=1.0 - lambda_init,
                               lambda_init=lambda_init)
    return pl.pallas_call(
        kernel,
        grid_spec=pltpu.PrefetchScalarGridSpec(
            num_scalar_prefetch=1,
            grid=(batch, B_HEADS, nq),
            in_specs=[pl.BlockSpec((t, B_VDIM), lambda b, h, i, s: (b * nq + i, h)),
                      pl.BlockSpec((seq, B_VDIM), lambda b, h, i, s: (b, kcol + h)),
                      pl.BlockSpec((seq, B_VDIM), lambda b, h, i, s: (b, vcol + h)),
                      pl.BlockSpec((None, t, LANES), lambda b, h, i, s: (h, 0, 0)),
                      pl.BlockSpec((None, t, LANES), lambda b, h, i, s: (h, 0, 0)),
                      pl.BlockSpec((t, t), lambda b, h, i, s: (0, 0)),
                      pl.BlockSpec((4, HEAD_DIM), lambda b, h, i, s: (0, 0)),
                      pl.BlockSpec((1, B_VDIM), lambda b, h, i, s: (0, 0))],
            out_specs=pl.BlockSpec((t, B_VDIM), lambda b, h, i, s: (b * nq + i, h)),
            scratch_shapes=[pltpu.VMEM((3, 2 * t, 2 * LANES), BF16),
                            pltpu.VMEM((seq, 2 * LANES), BF16),
                            pltpu.VMEM((seq, 2 * LANES), BF16),
                            pltpu.VMEM((2 * t, 1), F32),
                            pltpu.VMEM((2 * t, 2 * LANES), F32),
                            pltpu.VMEM((2 * t, t), F32),
                            pltpu.VMEM((2 * t, t), F32)],
        ),
        out_shape=jax.ShapeDtypeStruct((batch * seq, MIX_WIDTH), BF16),
        compiler_params=_params("arbitrary", "arbitrary", "arbitrary"),
        name="diff_attention",
    )(jnp.asarray(slopes2, F32), proj, proj, proj, qf, kf, dist, b_lambda,
      subln_g.reshape(1, B_VDIM))


def _router_kernel(x_ref, g_ref, whi_ref, wlo_ref, h_ref, e_ref, gate_ref):
    h = _rms(x_ref[...], g_ref[...])
    h_hi = h.astype(BF16)
    h_lo = (h - h_hi.astype(F32)).astype(BF16)
    whi = whi_ref[...]
    logits = _dot(h_hi, whi) + (_dot(h_hi, wlo_ref[...]) + _dot(h_lo, whi))
    lane = lax.broadcasted_iota(jnp.int32, logits.shape, 1)
    logits = jnp.where(lane < N_EXPERTS, logits, -jnp.inf)
    m1 = jnp.max(logits, axis=-1, keepdims=True)
    i1 = jnp.min(jnp.where(logits == m1, lane, LANES), axis=-1, keepdims=True)
    rest = jnp.where(lane == i1, -jnp.inf, logits)
    m2 = jnp.max(rest, axis=-1, keepdims=True)
    i2 = jnp.min(jnp.where(rest == m2, lane, LANES), axis=-1, keepdims=True)
    r = jnp.exp(m2 - m1)
    g1 = 1.0 / (1.0 + r)
    g2 = r * g1
    e_ref[...] = jnp.where(lane == 0, i1, jnp.where(lane == 1, i2, 0))
    gate_ref[...] = jnp.where(lane == 0, g1, jnp.where(lane == 1, g2, 0.0))
    h_ref[...] = h


def _router(x, g, w_router, tm=512):
    t, d = x.shape
    tm = min(tm, t)
    wpad = jnp.zeros((d, LANES), F32).at[:, :N_EXPERTS].set(w_router)
    whi = wpad.astype(BF16)
    wlo = (wpad - whi.astype(F32)).astype(BF16)
    return pl.pallas_call(
        _router_kernel,
        grid=(t // tm,),
        in_specs=[pl.BlockSpec((tm, d), lambda i: (i, 0)),
                  pl.BlockSpec((1, d), lambda i: (0, 0)),
                  pl.BlockSpec((d, LANES), lambda i: (0, 0)),
                  pl.BlockSpec((d, LANES), lambda i: (0, 0))],
        out_specs=[pl.BlockSpec((tm, d), lambda i: (i, 0)),
                   pl.BlockSpec((tm, LANES), lambda i: (i, 0)),
                   pl.BlockSpec((tm, LANES), lambda i: (i, 0))],
        out_shape=[jax.ShapeDtypeStruct((t, d), F32),
                   jax.ShapeDtypeStruct((t, LANES), jnp.int32),
                   jax.ShapeDtypeStruct((t, LANES), F32)],
        compiler_params=_params("parallel"),
        name="router",
    )(x, g.reshape(1, d), whi, wlo)


def _moe_kernel(tok_ref, blk_e_ref, nused_ref, h_hbm, wg_ref, wu_ref, wd_ref, o_ref,
                x_sc, xb_sc, acc_sc, sem):
    i = pl.program_id(0)
    f = pl.program_id(1)
    tm = x_sc.shape[0]
    used = i < nused_ref[0]

    def row_copy(t, r):
        return pltpu.make_async_copy(h_hbm.at[pl.ds(t, 1)], x_sc.at[pl.ds(r, 1)], sem)

    @pl.when(jnp.logical_and(used, f == 0))
    def _():
        def start(r, c):
            row_copy(tok_ref[i * tm + r], r).start()
            return c

        lax.fori_loop(0, tm, start, 0)

        def wait(r, c):
            row_copy(0, r).wait()
            return c

        lax.fori_loop(0, tm, wait, 0)
        xb_sc[...] = x_sc[...].astype(BF16)
        acc_sc[...] = jnp.zeros_like(acc_sc)

    @pl.when(used)
    def _():
        acc_sc[...] += _swiglu_tile(xb_sc[...], wg_ref[...], wu_ref[...], wd_ref[...])

    @pl.when(f == pl.num_programs(1) - 1)
    def _():
        o_ref[...] = jnp.where(used, acc_sc[...], 0.0)


def _moe_experts(h, tok_p, blk_e, nused, wg, wu, wd, tm, tf=512):
    t, d = h.shape
    p = tok_p.shape[0]
    nblk = p // tm
    nf = wg.shape[2] // tf

    def fidx(i, f, nused_ref):
        return jnp.where(i < nused_ref[0], f, nf - 1)

    return pl.pallas_call(
        _moe_kernel,
        grid_spec=pltpu.PrefetchScalarGridSpec(
            num_scalar_prefetch=3,
            grid=(nblk, nf),
            in_specs=[pl.BlockSpec(memory_space=pl.ANY),
                      pl.BlockSpec((None, d, tf), lambda i, f, tok, be, nu: (be[i], 0, fidx(i, f, nu))),
                      pl.BlockSpec((None, d, tf), lambda i, f, tok, be, nu: (be[i], 0, fidx(i, f, nu))),
                      pl.BlockSpec((None, tf, d), lambda i, f, tok, be, nu: (be[i], fidx(i, f, nu), 0))],
            out_specs=pl.BlockSpec((tm, d), lambda i, f, tok, be, nu: (i, 0)),
            scratch_shapes=[pltpu.VMEM((tm, d), F32), pltpu.VMEM((tm, d), BF16),
                            pltpu.VMEM((tm, d), F32), pltpu.SemaphoreType.DMA(())],
        ),
        out_shape=jax.ShapeDtypeStruct((p, d), F32),
        compiler_params=_params("arbitrary", "arbitrary"),
        name="moe_experts",
    )(tok_p, blk_e, nused, h, wg, wu, wd)


def _combine_kernel(dest_ref, ys_hbm, gate_ref, x_ref, g_ref, o_ref, r_sc, sem):
    i = pl.program_id(0)
    tc = x_ref.shape[0]

    def row_copy(src, k, r):
        return pltpu.make_async_copy(ys_hbm.at[pl.ds(src, 1)], r_sc.at[k, pl.ds(r, 1)], sem)

    def start(r, c):
        a = TOP_K * (i * tc + r)
        row_copy(dest_ref[a], 0, r).start()
        row_copy(dest_ref[a + 1], 1, r).start()
        return c

    lax.fori_loop(0, tc, start, 0)

    def wait(r, c):
        row_copy(0, 0, r).wait()
        row_copy(0, 1, r).wait()
        return c

    lax.fori_loop(0, tc, wait, 0)
    gate = gate_ref[...]
    y = gate[:, 0:1] * r_sc[0] + gate[:, 1:2] * r_sc[1]
    o_ref[...] = x_ref[...] + _rms(y, g_ref[...])


def _combine(ys, dest, gates, x, g, tc=256):
    t, d = x.shape
    tc = min(tc, t)
    return pl.pallas_call(
        _combine_kernel,
        grid_spec=pltpu.PrefetchScalarGridSpec(
            num_scalar_prefetch=1,
            grid=(t // tc,),
            in_specs=[pl.BlockSpec(memory_space=pl.ANY),
                      pl.BlockSpec((tc, LANES), lambda i, dst: (i, 0)),
                      pl.BlockSpec((tc, d), lambda i, dst: (i, 0)),
                      pl.BlockSpec((1, d), lambda i, dst: (0, 0))],
            out_specs=pl.BlockSpec((tc, d), lambda i, dst: (i, 0)),
            scratch_shapes=[pltpu.VMEM((TOP_K, tc, d), F32), pltpu.SemaphoreType.DMA(())],
        ),
        out_shape=jax.ShapeDtypeStruct((t, d), F32),
        compiler_params=_params("arbitrary"),
        name="moe_combine",
    )(dest, ys, gates, x, g.reshape(1, d))


def _route(top_e, tm):
    t = top_e.shape[0]
    a = t * TOP_K
    e_flat = top_e.reshape(a)
    onehot = (e_flat[:, None] == jnp.arange(N_EXPERTS, dtype=jnp.int32)[None, :]).astype(jnp.int32)
    csum = jnp.cumsum(onehot, axis=0)
    rank = jnp.sum(onehot * (csum - 1), axis=1)
    counts = csum[-1]
    padded = (counts + tm - 1) // tm * tm
    pends = jnp.cumsum(padded)
    pstarts = pends - padded
    dest = (pstarts[e_flat] + rank).astype(jnp.int32)
    nblk = a // tm + N_EXPERTS
    tok_p = jnp.zeros((nblk * tm,), jnp.int32).at[dest].set(jnp.arange(a, dtype=jnp.int32) // TOP_K)
    blk_start = jnp.arange(nblk, dtype=jnp.int32) * tm
    blk_e = jnp.minimum(jnp.sum((pends[None, :] <= blk_start[:, None]).astype(jnp.int32), axis=1),
                        N_EXPERTS - 1)
    nused = (pends[-1:] // tm).astype(jnp.int32)
    return dest, tok_p, blk_e, nused


def _moe(x, g_in, g_out, w_router, wg, wu, wd, tm=512):
    h, e_pad, gates = _router(x, g_in, w_router)
    dest, tok_p, blk_e, nused = _route(e_pad[:, :TOP_K], tm)
    ys = _moe_experts(h, tok_p, blk_e, nused, wg, wu, wd, tm)
    return _combine(ys, dest, gates, x, g_out)


def kernel(x, mem, norm_g, mem_norm_g, w_mem_kv, a_w_in, a_sink, a_w_out, b_w_in, b_lambda,
           b_subln_g, b_w_out, ffn_w_gate, ffn_w_up, ffn_w_down, moe_w_router, moe_w_gate,
           moe_w_up, moe_w_down):
    batch, seq, d = x.shape
    n_mem = mem.shape[1]
    bf = lambda w: w.astype(BF16)
    xt = x.reshape(batch * seq, d)
    mkv = _norm_matmul(mem.reshape(batch * n_mem, d), mem_norm_g, bf(w_mem_kv),
                       jnp.ones((w_mem_kv.shape[1],), F32))

    g = norm_g[0]
    proj = _norm_matmul(xt, g[0], bf(a_w_in[0]), _query_col_scale(a_w_in.shape[2], A_Q_HEADS * HEAD_DIM))
    mix = _window_attention(proj, a_sink[0], batch, seq)
    mem_out = _mem_attention(proj, (A_Q_HEADS + 2 * A_KV_HEADS) * HEAD_DIM // MEM_WIDTH, mkv, batch, seq)
    xt = _out_proj(mix, mem_out, bf(a_w_out[0]), xt, g[1])
    xt = _ffn(xt, g[2], g[3], bf(ffn_w_gate[0]), bf(ffn_w_up[0]), bf(ffn_w_down[0]))

    g = norm_g[1]
    lambda_init = 0.8 - 0.6 * math.exp(-0.3 * 1)
    proj = _norm_matmul(xt, g[0], bf(b_w_in[0]), _query_col_scale(b_w_in.shape[2], MIX_WIDTH))
    mix = _diff_attention(proj, b_lambda[0], b_subln_g[0], lambda_init, batch, seq)
    mem_out = _mem_attention(proj, 3 * MIX_WIDTH // MEM_WIDTH, mkv, batch, seq)
    xt = _out_proj(mix, mem_out, bf(b_w_out[0]), xt, g[1])
    xt = _moe(xt, g[2], g[3], moe_w_router[0], bf(moe_w_gate[0]), bf(moe_w_up[0]), bf(moe_w_down[0]))
    return xt.reshape(batch, seq, d)
```

```python
import functools
import math

import jax
import jax.numpy as jnp
import numpy as np
from jax import lax
from jax.experimental import pallas as pl
from jax.experimental.pallas import tpu as pltpu

D_MODEL = 1024
HEAD_DIM = 64
MIX_WIDTH = 768
MEM_WIDTH = 256
MEM_HEADS = 4
A_Q_HEADS = 12
A_KV_HEADS = 4
A_GROUP = 3
WINDOW = 128
B_HEADS = 6
B_VDIM = 128
D_FF = 3584
N_EXPERTS = 8
TOP_K = 2
EPS = 1e-6
NEG_INF = -1e30
LOG2E = math.log2(math.e)
QK_SCALE = HEAD_DIM ** -0.5 * LOG2E
LANES = 128
VMEM_LIMIT_BYTES = 48 * 1024 * 1024

BF16 = jnp.bfloat16
F32 = jnp.float32


def _params(*sem):
    return pltpu.CompilerParams(dimension_semantics=sem, vmem_limit_bytes=VMEM_LIMIT_BYTES)


def _rms(x, g):
    return x * lax.rsqrt(jnp.mean(x * x, axis=-1, keepdims=True) + EPS) * g


def _dot(a, b):
    return jnp.dot(a, b, preferred_element_type=F32)


def _dot_nt(a, b):
    return lax.dot_general(a, b, (((1,), (1,)), ((), ())), preferred_element_type=F32)


def _alibi_slopes(n):
    return [2.0 ** (-8.0 * (i + 1) / n) for i in range(n)]


def _norm_matmul_kernel(x_ref, g_ref, w_ref, cs_ref, o_ref, *, chunk):
    h = _rms(x_ref[...], g_ref[...]).astype(BF16)
    for c in range(o_ref.shape[1] // chunk):
        sl = slice(c * chunk, (c + 1) * chunk)
        o_ref[:, sl] = (_dot(h, w_ref[:, sl]) * cs_ref[:, sl]).astype(o_ref.dtype)


def _norm_matmul(x, g, w, col_scale, tm=512, chunk=512):
    t, d = x.shape
    n = w.shape[1]
    tm = min(tm, t)
    return pl.pallas_call(
        functools.partial(_norm_matmul_kernel, chunk=min(chunk, n)),
        grid=(t // tm,),
        in_specs=[pl.BlockSpec((tm, d), lambda i: (i, 0)),
                  pl.BlockSpec((1, d), lambda i: (0, 0)),
                  pl.BlockSpec((d, n), lambda i: (0, 0)),
                  pl.BlockSpec((1, n), lambda i: (0, 0))],
        out_specs=pl.BlockSpec((tm, n), lambda i: (i, 0)),
        out_shape=jax.ShapeDtypeStruct((t, n), BF16),
        compiler_params=_params("parallel"),
        name="norm_matmul",
    )(x, g.reshape(1, d), w, col_scale.reshape(1, n))


def _query_col_scale(n, mix_q_cols):
    cols = np.arange(n)
    return jnp.asarray(np.where((cols < mix_q_cols) | (cols >= n - MEM_WIDTH), QK_SCALE, 1.0), F32)


def _window_kernel(sink_ref, q_ref, kp_ref, kc_ref, kn_ref, vp_ref, vc_ref, vn_ref, o_ref):
    n = pl.program_id(1)
    nb = pl.num_programs(1)
    blk = q_ref.shape[0]
    k = jnp.concatenate([kp_ref[...], kc_ref[...], kn_ref[...]], axis=0)
    v = jnp.concatenate([vp_ref[...], vc_ref[...], vn_ref[...]], axis=0)
    qi = lax.broadcasted_iota(jnp.int32, (blk, 3 * blk), 0)
    kj = lax.broadcasted_iota(jnp.int32, (blk, 3 * blk), 1)
    dist_i = jnp.abs(blk + qi - kj)
    valid = dist_i <= WINDOW
    valid = valid & jnp.logical_or(n > 0, kj >= blk)
    valid = valid & jnp.logical_or(n < nb - 1, kj < 2 * blk)
    dist = dist_i.astype(F32)
    q = q_ref[...]
    slopes = _alibi_slopes(A_Q_HEADS)
    outs = []
    for h in range(A_Q_HEADS):
        kh = h // A_GROUP
        qh = q[:, h * HEAD_DIM:(h + 1) * HEAD_DIM]
        kk = k[:, kh * HEAD_DIM:(kh + 1) * HEAD_DIM]
        vv = v[:, kh * HEAD_DIM:(kh + 1) * HEAD_DIM]
        s = _dot_nt(qh, kk) - (slopes[h] * LOG2E) * dist
        s = jnp.where(valid, s, NEG_INF)
        sink = sink_ref[h] * LOG2E
        m = jnp.maximum(jnp.max(s, axis=-1, keepdims=True), sink)
        p = jnp.exp2(s - m)
        denom = jnp.sum(p, axis=-1, keepdims=True) + jnp.exp2(sink - m)
        o = _dot(p.astype(BF16), vv)
        outs.append(o * (1.0 / denom))
    o_ref[...] = jnp.concatenate(outs, axis=-1).astype(o_ref.dtype)


def _window_attention(proj, sink, batch, seq):
    blk = WINDOW
    nb = seq // blk
    kcol = MIX_WIDTH // 256
    vcol = kcol + 1

    def cur(b, n, s):
        return b * nb + n

    def prev(b, n, s):
        return b * nb + jnp.maximum(n - 1, 0)

    def nxt(b, n, s):
        return b * nb + jnp.minimum(n + 1, nb - 1)

    def spec(row, col):
        return pl.BlockSpec((blk, 256), lambda b, n, s: (row(b, n, s), col))

    return pl.pallas_call(
        _window_kernel,
        grid_spec=pltpu.PrefetchScalarGridSpec(
            num_scalar_prefetch=1,
            grid=(batch, nb),
            in_specs=[pl.BlockSpec((blk, MIX_WIDTH), lambda b, n, s: (b * nb + n, 0)),
                      spec(prev, kcol), spec(cur, kcol), spec(nxt, kcol),
                      spec(prev, vcol), spec(cur, vcol), spec(nxt, vcol)],
            out_specs=pl.BlockSpec((blk, MIX_WIDTH), lambda b, n, s: (b * nb + n, 0)),
        ),
        out_shape=jax.ShapeDtypeStruct((batch * seq, MIX_WIDTH), BF16),
        compiler_params=_params("parallel", "parallel"),
        name="window_attention",
    )(sink, proj, proj, proj, proj, proj, proj, proj)


def _mem_attn_kernel(q_ref, mkv_ref, o_ref):
    q = q_ref[...]
    mkv = mkv_ref[...]
    outs = []
    for h in range(MEM_HEADS):
        sl = slice(h * HEAD_DIM, (h + 1) * HEAD_DIM)
        s = _dot_nt(q[:, sl], mkv[:, sl])
        m = jnp.max(s, axis=-1, keepdims=True)
        p = jnp.exp2(s - m)
        denom = jnp.sum(p, axis=-1, keepdims=True)
        o = _dot(p.astype(BF16), mkv[:, MEM_WIDTH + h * HEAD_DIM:MEM_WIDTH + (h + 1) * HEAD_DIM])
        outs.append(o * (1.0 / denom))
    o_ref[...] = jnp.concatenate(outs, axis=-1).astype(o_ref.dtype)


def _mem_attention(proj, qcol, mkv, batch, seq, tm=512):
    tm = min(tm, seq)
    per_batch = seq // tm
    n_mem = mkv.shape[0] // batch
    return pl.pallas_call(
        _mem_attn_kernel,
        grid=(batch * per_batch,),
        in_specs=[pl.BlockSpec((tm, MEM_WIDTH), lambda i: (i, qcol)),
                  pl.BlockSpec((n_mem, 2 * MEM_WIDTH), lambda i: (i // per_batch, 0))],
        out_specs=pl.BlockSpec((tm, MEM_WIDTH), lambda i: (i, 0)),
        out_shape=jax.ShapeDtypeStruct((batch * seq, MEM_WIDTH), BF16),
        compiler_params=_params("parallel"),
        name="mem_attention",
    )(proj, mkv)


def _out_proj_kernel(mix_ref, mem_ref, w1_ref, w2_ref, x_ref, g_ref, o_ref):
    o = _dot(mix_ref[...], w1_ref[...]) + _dot(mem_ref[...], w2_ref[...])
    o_ref[...] = x_ref[...] + _rms(o, g_ref[...])


def _out_proj(mix, mem_out, w_out, x, g, tm=512):
    t, d = x.shape
    tm = min(tm, t)
    w1 = w_out[:MIX_WIDTH]
    w2 = w_out[MIX_WIDTH:]
    return pl.pallas_call(
        _out_proj_kernel,
        grid=(t // tm,),
        in_specs=[pl.BlockSpec((tm, MIX_WIDTH), lambda i: (i, 0)),
                  pl.BlockSpec((tm, MEM_WIDTH), lambda i: (i, 0)),
                  pl.BlockSpec((MIX_WIDTH, d), lambda i: (0, 0)),
                  pl.BlockSpec((MEM_WIDTH, d), lambda i: (0, 0)),
                  pl.BlockSpec((tm, d), lambda i: (i, 0)),
                  pl.BlockSpec((1, d), lambda i: (0, 0))],
        out_specs=pl.BlockSpec((tm, d), lambda i: (i, 0)),
        out_shape=jax.ShapeDtypeStruct((t, d), F32),
        compiler_params=_params("parallel"),
        name="out_proj",
    )(mix, mem_out, w1, w2, x, g.reshape(1, d))


def _swiglu_tile(h, wg, wu, wd):
    a = _dot(h, wg)
    u = _dot(h, wu)
    z = (a * jax.nn.sigmoid(a) * u).astype(BF16)
    return _dot(z, wd)


def _ffn_kernel(x_ref, gin_ref, gout_ref, wg_ref, wu_ref, wd_ref, o_ref, h_sc, acc_sc):
    f = pl.program_id(1)

    @pl.when(f == 0)
    def _():
        h_sc[...] = _rms(x_ref[...], gin_ref[...]).astype(BF16)
        acc_sc[...] = jnp.zeros_like(acc_sc)

    acc_sc[...] += _swiglu_tile(h_sc[...], wg_ref[...], wu_ref[...], wd_ref[...])

    @pl.when(f == pl.num_programs(1) - 1)
    def _():
        o_ref[...] = x_ref[...] + _rms(acc_sc[...], gout_ref[...])


def _ffn(x, g_in, g_out, wg, wu, wd, tm=512, tf=512):
    t, d = x.shape
    ff = wg.shape[1]
    tm = min(tm, t)
    return pl.pallas_call(
        _ffn_kernel,
        grid=(t // tm, ff // tf),
        in_specs=[pl.BlockSpec((tm, d), lambda i, f: (i, 0)),
                  pl.BlockSpec((1, d), lambda i, f: (0, 0)),
                  pl.BlockSpec((1, d), lambda i, f: (0, 0)),
                  pl.BlockSpec((d, tf), lambda i, f: (0, f)),
                  pl.BlockSpec((d, tf), lambda i, f: (0, f)),
                  pl.BlockSpec((tf, d), lambda i, f: (f, 0))],
        out_specs=pl.BlockSpec((tm, d), lambda i, f: (i, 0)),
        out_shape=jax.ShapeDtypeStruct((t, d), F32),
        scratch_shapes=[pltpu.VMEM((tm, d), BF16), pltpu.VMEM((tm, d), F32)],
        compiler_params=_params("parallel", "arbitrary"),
        name="ffn",
    )(x, g_in.reshape(1, d), g_out.reshape(1, d), wg, wu, wd)


_POS_SPLIT = 256
_LEFT, _DIAG, _RIGHT = 0, 1, 2
_BOUND_LANE = 12
_CHUNK_LANE = 15
_ZERO_EXP2_ARG = 152.0
_MIN_DENOM = 2.0 ** -60


def _bf16_split(x):
    hi = x.astype(BF16)
    r = x - hi.astype(F32)
    mid = r.astype(BF16)
    lo = (r - mid.astype(F32)).astype(BF16)
    return hi, mid, lo


def _diff_attn_kernel(reach_ref, slope_ref, q_ref, k_ref, v_ref, qf_ref, kf_ref, dist_ref, lam_ref,
                      g_ref, o_ref, qa_sc, ka_sc, va_sc, k2_sc, m_sc, acc_sc, sa_sc, sb_sc, *, tk,
                      out_scale, lambda_init):
    h = pl.program_id(1)
    i = pl.program_id(2)
    tq = q_ref.shape[0]
    seq = k_ref.shape[0]
    nk = seq // tk
    slope = slope_ref[h]
    reach = reach_ref[h]

    @pl.when(i == 0)
    def _():
        k = k_ref[...]
        ka_sc[:, :B_VDIM] = k
        va_sc[:, :B_VDIM] = v_ref[...]
        lane = lax.broadcasted_iota(jnp.int32, (seq, LANES), 1)
        va_sc[:, B_VDIM:] = jnp.where(lane == 0, 1.0, 0.0).astype(BF16)
        kf = kf_ref[...]
        flane = lax.broadcasted_iota(jnp.int32, kf.shape, 1)
        chunk_lanes = jnp.logical_and(flane >= _CHUNK_LANE, flane < _CHUNK_LANE + 3)
        for c in range(nk):
            ka_sc[c * tk:(c + 1) * tk, B_VDIM:] = jnp.where(chunk_lanes, float(c), kf).astype(BF16)
        kk = k.astype(F32)
        kk = kk * kk
        for c, sel in enumerate((lane < HEAD_DIM, lane >= HEAD_DIM)):
            n2 = jnp.sum(jnp.where(sel, kk, 0.0), axis=-1, keepdims=True)
            k2_sc[c] = jnp.broadcast_to(jnp.max(n2, axis=0, keepdims=True), k2_sc.shape[1:])

    q = q_ref[...]
    lane = lax.broadcasted_iota(jnp.int32, q.shape, 1)
    zero = jnp.zeros_like(q)
    q0 = jnp.where(lane < HEAD_DIM, q, zero)
    q1 = jnp.where(lane >= HEAD_DIM, q, zero)
    qq = q.astype(F32)
    qq = qq * qq
    n0 = jnp.sum(jnp.where(lane < HEAD_DIM, qq, 0.0), axis=-1, keepdims=True)
    n1 = jnp.sum(jnp.where(lane >= HEAD_DIM, qq, 0.0), axis=-1, keepdims=True)
    bound = jnp.concatenate([jnp.sqrt(n0) * jnp.sqrt(k2_sc[0][0:1, 0:1]),
                             jnp.sqrt(n1) * jnp.sqrt(k2_sc[1][0:1, 0:1])], axis=0)
    off = slope * (i * tk).astype(F32)
    qf = qf_ref[...]
    lane2 = lax.broadcasted_iota(jnp.int32, (2 * tq, LANES), 1)

    def build_queries(with_bound):
        for variant, feat, row_term in ((_LEFT, -qf, bound + off), (_DIAG, zero, bound),
                                        (_RIGHT, qf, bound - off)):
            feat = jnp.concatenate([feat, feat], axis=0)
            if with_bound:
                for n, piece in enumerate(_bf16_split(row_term)):
                    feat = jnp.where(lane2 == _BOUND_LANE + n, piece, feat)
            qa_sc[variant, :, B_VDIM:] = feat
            qa_sc[variant, :tq, :B_VDIM] = q0
            qa_sc[variant, tq:, :B_VDIM] = q1

    def scores(j, s_ref):
        variant = jnp.where(j < i, _LEFT, jnp.where(j == i, _DIAG, _RIGHT))
        start = pl.multiple_of(j * tk, tk)
        s_ref[...] = _dot_nt(qa_sc[variant], ka_sc[pl.ds(start, tk), :])

    def diag_bias(j, s_ref):
        @pl.when(j == i)
        def _():
            bias = slope * dist_ref[...]
            s_ref[:tq] -= bias
            s_ref[tq:] -= bias

    def exp_pv(j, s_ref):
        start = pl.multiple_of(j * tk, tk)
        p = jnp.exp2(s_ref[...]).astype(BF16)
        acc_sc[...] += _dot(p, va_sc[pl.ds(start, tk), :])

    build_queries(True)
    acc_sc[...] = jnp.zeros_like(acc_sc)
    first_pair = jnp.maximum(i - reach, 0) // 2
    last_pair = (jnp.minimum(i + reach + 1, nk) + 1) // 2
    scores(2 * first_pair, sa_sc)
    diag_bias(2 * first_pair, sa_sc)

    def pair(jj, carry):
        j = 2 * jj
        scores(j + 1, sb_sc)
        exp_pv(j, sa_sc)
        diag_bias(j + 1, sb_sc)
        nxt = jnp.minimum(j + 2, nk - 1)
        scores(nxt, sa_sc)
        exp_pv(j + 1, sb_sc)
        diag_bias(nxt, sa_sc)
        return carry

    lax.fori_loop(first_pair, last_pair, pair, 0)

    denom_min = jnp.min(acc_sc[:, B_VDIM:B_VDIM + 1])

    @pl.when(jnp.logical_not(denom_min >= _MIN_DENOM))
    def _():
        build_queries(False)
        m_sc[...] = jnp.full_like(m_sc, NEG_INF)
        acc_sc[...] = jnp.zeros_like(acc_sc)

        def chunk(j, carry):
            scores(j, sa_sc)
            diag_bias(j, sa_sc)
            adj = jnp.where(j > i, off, jnp.where(j < i, -off, 0.0))
            start = pl.multiple_of(j * tk, tk)
            s = sa_sc[...]
            m_prev = m_sc[...]
            m_new = jnp.maximum(m_prev, jnp.max(s, axis=-1, keepdims=True) + adj)
            p = jnp.exp2(s - (m_new - adj)).astype(BF16)
            acc_sc[...] = jnp.exp2(m_prev - m_new) * acc_sc[...] + _dot(p, va_sc[pl.ds(start, tk), :])
            m_sc[...] = m_new
            return carry

        lax.fori_loop(0, nk, chunk, 0)

    lp = lam_ref[...]
    lam = (jnp.exp(jnp.sum(lp[0:1] * lp[1:2], axis=-1, keepdims=True))
           - jnp.exp(jnp.sum(lp[2:3] * lp[3:4], axis=-1, keepdims=True)) + lambda_init)
    acc = acc_sc[...]
    o0 = acc[:tq, :B_VDIM] * (1.0 / acc[:tq, B_VDIM:B_VDIM + 1])
    o1 = acc[tq:, :B_VDIM] * (1.0 / acc[tq:, B_VDIM:B_VDIM + 1])
    o_ref[...] = (_rms(o0 - lam * o1, g_ref[...]) * out_scale).astype(o_ref.dtype)


def _bf16_pieces(x, n=3):
    out = []
    r = np.float64(x)
    for _ in range(n):
        p = np.float64(np.float32(r).astype(jnp.bfloat16).astype(np.float32))
        out.append(p)
        r = r - p
    return out


def _alibi_features(slopes2, t):
    pos = np.arange(t)
    hi = (pos // _POS_SPLIT) * _POS_SPLIT
    lo = pos % _POS_SPLIT
    qf = np.zeros((len(slopes2), t, LANES), np.float32)
    kf = np.zeros((len(slopes2), t, LANES), np.float32)
    for h, s in enumerate(slopes2):
        for n, piece in enumerate(_bf16_pieces(s)):
            for base, part in ((0, hi), (3, lo)):
                qf[h, :, base + n] = -piece
                kf[h, :, base + n] = part
                qf[h, :, 6 + base + n] = part
                kf[h, :, 6 + base + n] = piece
        for n, piece in enumerate(_bf16_pieces(s * t)):
            qf[h, :, _CHUNK_LANE + n] = -piece
        kf[h, :, _BOUND_LANE:_BOUND_LANE + 3] = -1.0
    return jnp.asarray(qf, BF16), jnp.asarray(kf, BF16)


def _diff_attention(proj, b_lambda, subln_g, lambda_init, batch, seq, t=512):
    t = min(t, seq // 2)
    nq = seq // t
    assert seq % (2 * t) == 0
    kcol = MIX_WIDTH // B_VDIM
    vcol = 2 * kcol
    slopes2 = [s * LOG2E for s in _alibi_slopes(B_HEADS)]
    reach = [min(nq, int(math.floor((_ZERO_EXP2_ARG / s - 1.0) / t)) + 1) for s in slopes2]
    qf, kf = _alibi_features(slopes2, t)
    pos = np.arange(t)
    dist = jnp.asarray(np.abs(pos[:, None] - pos[None, :]), F32)
    kernel = functools.partial(_diff_attn_kernel, tk=t, out_scale=1.0 - lambda_init,
                               lambda_init=lambda_init)
    return pl.pallas_call(
        kernel,
        grid_spec=pltpu.PrefetchScalarGridSpec(
            num_scalar_prefetch=2,
            grid=(batch, B_HEADS, nq),
            in_specs=[pl.BlockSpec((t, B_VDIM), lambda b, h, i, r, s: (b * nq + i, h)),
                      pl.BlockSpec((seq, B_VDIM), lambda b, h, i, r, s: (b, kcol + h)),
                      pl.BlockSpec((seq, B_VDIM), lambda b, h, i, r, s: (b, vcol + h)),
                      pl.BlockSpec((None, t, LANES), lambda b, h, i, r, s: (h, 0, 0)),
                      pl.BlockSpec((None, t, LANES), lambda b, h, i, r, s: (h, 0, 0)),
                      pl.BlockSpec((t, t), lambda b, h, i, r, s: (0, 0)),
                      pl.BlockSpec((4, HEAD_DIM), lambda b, h, i, r, s: (0, 0)),
                      pl.BlockSpec((1, B_VDIM), lambda b, h, i, r, s: (0, 0))],
            out_specs=pl.BlockSpec((t, B_VDIM), lambda b, h, i, r, s: (b * nq + i, h)),
            scratch_shapes=[pltpu.VMEM((3, 2 * t, 2 * LANES), BF16),
                            pltpu.VMEM((seq, 2 * LANES), BF16),
                            pltpu.VMEM((seq, 2 * LANES), BF16),
                            pltpu.VMEM((2, 8, LANES), F32),
                            pltpu.VMEM((2 * t, 1), F32),
                            pltpu.VMEM((2 * t, 2 * LANES), F32),
                            pltpu.VMEM((2 * t, t), F32),
                            pltpu.VMEM((2 * t, t), F32)],
        ),
        out_shape=jax.ShapeDtypeStruct((batch * seq, MIX_WIDTH), BF16),
        compiler_params=_params("arbitrary", "arbitrary", "arbitrary"),
        name="diff_attention",
    )(jnp.asarray(reach, jnp.int32), jnp.asarray(slopes2, F32), proj, proj, proj, qf, kf, dist,
      b_lambda, subln_g.reshape(1, B_VDIM))


def _router_kernel(x_ref, g_ref, whi_ref, wlo_ref, h_ref, e_ref, gate_ref):
    h = _rms(x_ref[...], g_ref[...])
    h_hi = h.astype(BF16)
    h_lo = (h - h_hi.astype(F32)).astype(BF16)
    whi = whi_ref[...]
    logits = _dot(h_hi, whi) + (_dot(h_hi, wlo_ref[...]) + _dot(h_lo, whi))
    lane = lax.broadcasted_iota(jnp.int32, logits.shape, 1)
    logits = jnp.where(lane < N_EXPERTS, logits, -jnp.inf)
    m1 = jnp.max(logits, axis=-1, keepdims=True)
    i1 = jnp.min(jnp.where(logits == m1, lane, LANES), axis=-1, keepdims=True)
    rest = jnp.where(lane == i1, -jnp.inf, logits)
    m2 = jnp.max(rest, axis=-1, keepdims=True)
    i2 = jnp.min(jnp.where(rest == m2, lane, LANES), axis=-1, keepdims=True)
    r = jnp.exp(m2 - m1)
    g1 = 1.0 / (1.0 + r)
    g2 = r * g1
    e_ref[...] = jnp.where(lane == 0, i1, jnp.where(lane == 1, i2, 0))
    gate_ref[...] = jnp.where(lane == 0, g1, jnp.where(lane == 1, g2, 0.0))
    h_ref[...] = h


def _router(x, g, w_router, tm=512):
    t, d = x.shape
    tm = min(tm, t)
    wpad = jnp.zeros((d, LANES), F32).at[:, :N_EXPERTS].set(w_router)
    whi = wpad.astype(BF16)
    wlo = (wpad - whi.astype(F32)).astype(BF16)
    return pl.pallas_call(
        _router_kernel,
        grid=(t // tm,),
        in_specs=[pl.BlockSpec((tm, d), lambda i: (i, 0)),
                  pl.BlockSpec((1, d), lambda i: (0, 0)),
                  pl.BlockSpec((d, LANES), lambda i: (0, 0)),
                  pl.BlockSpec((d, LANES), lambda i: (0, 0))],
        out_specs=[pl.BlockSpec((tm, d), lambda i: (i, 0)),
                   pl.BlockSpec((tm, LANES), lambda i: (i, 0)),
                   pl.BlockSpec((tm, LANES), lambda i: (i, 0))],
        out_shape=[jax.ShapeDtypeStruct((t, d), F32),
                   jax.ShapeDtypeStruct((t, LANES), jnp.int32),
                   jax.ShapeDtypeStruct((t, LANES), F32)],
        compiler_params=_params("parallel"),
        name="router",
    )(x, g.reshape(1, d), whi, wlo)


def _moe_kernel(tok_ref, blk_e_ref, nused_ref, h_hbm, wg_ref, wu_ref, wd_ref, o_ref,
                x_sc, xb_sc, acc_sc, sem):
    i = pl.program_id(0)
    f = pl.program_id(1)
    tm = x_sc.shape[0]
    used = i < nused_ref[0]

    def row_copy(t, r):
        return pltpu.make_async_copy(h_hbm.at[pl.ds(t, 1)], x_sc.at[pl.ds(r, 1)], sem)

    @pl.when(jnp.logical_and(used, f == 0))
    def _():
        def start(r, c):
            row_copy(tok_ref[i * tm + r], r).start()
            return c

        lax.fori_loop(0, tm, start, 0)

        def wait(r, c):
            row_copy(0, r).wait()
            return c

        lax.fori_loop(0, tm, wait, 0)
        xb_sc[...] = x_sc[...].astype(BF16)
        acc_sc[...] = jnp.zeros_like(acc_sc)

    @pl.when(used)
    def _():
        acc_sc[...] += _swiglu_tile(xb_sc[...], wg_ref[...], wu_ref[...], wd_ref[...])

    @pl.when(f == pl.num_programs(1) - 1)
    def _():
        o_ref[...] = jnp.where(used, acc_sc[...], 0.0)


def _moe_experts(h, tok_p, blk_e, nused, wg, wu, wd, tm, tf=512):
    t, d = h.shape
    p = tok_p.shape[0]
    nblk = p // tm
    nf = wg.shape[2] // tf

    def fidx(i, f, nused_ref):
        return jnp.where(i < nused_ref[0], f, nf - 1)

    return pl.pallas_call(
        _moe_kernel,
        grid_spec=pltpu.PrefetchScalarGridSpec(
            num_scalar_prefetch=3,
            grid=(nblk, nf),
            in_specs=[pl.BlockSpec(memory_space=pl.ANY),
                      pl.BlockSpec((None, d, tf), lambda i, f, tok, be, nu: (be[i], 0, fidx(i, f, nu))),
                      pl.BlockSpec((None, d, tf), lambda i, f, tok, be, nu: (be[i], 0, fidx(i, f, nu))),
                      pl.BlockSpec((None, tf, d), lambda i, f, tok, be, nu: (be[i], fidx(i, f, nu), 0))],
            out_specs=pl.BlockSpec((tm, d), lambda i, f, tok, be, nu: (i, 0)),
            scratch_shapes=[pltpu.VMEM((tm, d), F32), pltpu.VMEM((tm, d), BF16),
                            pltpu.VMEM((tm, d), F32), pltpu.SemaphoreType.DMA(())],
        ),
        out_shape=jax.ShapeDtypeStruct((p, d), F32),
        compiler_params=_params("arbitrary", "arbitrary"),
        name="moe_experts",
    )(tok_p, blk_e, nused, h, wg, wu, wd)


def _combine_kernel(dest_ref, ys_hbm, gate_ref, x_ref, g_ref, o_ref, r_sc, sem):
    i = pl.program_id(0)
    tc = x_ref.shape[0]

    def row_copy(src, k, r):
        return pltpu.make_async_copy(ys_hbm.at[pl.ds(src, 1)], r_sc.at[k, pl.ds(r, 1)], sem)

    def start(r, c):
        a = TOP_K * (i * tc + r)
        row_copy(dest_ref[a], 0, r).start()
        row_copy(dest_ref[a + 1], 1, r).start()
        return c

    lax.fori_loop(0, tc, start, 0)

    def wait(r, c):
        row_copy(0, 0, r).wait()
        row_copy(0, 1, r).wait()
        return c

    lax.fori_loop(0, tc, wait, 0)
    gate = gate_ref[...]
    y = gate[:, 0:1] * r_sc[0] + gate[:, 1:2] * r_sc[1]
    o_ref[...] = x_ref[...] + _rms(y, g_ref[...])


def _combine(ys, dest, gates, x, g, tc=256):
    t, d = x.shape
    tc = min(tc, t)
    return pl.pallas_call(
        _combine_kernel,
        grid_spec=pltpu.PrefetchScalarGridSpec(
            num_scalar_prefetch=1,
            grid=(t // tc,),
            in_specs=[pl.BlockSpec(memory_space=pl.ANY),
                      pl.BlockSpec((tc, LANES), lambda i, dst: (i, 0)),
                      pl.BlockSpec((tc, d), lambda i, dst: (i, 0)),
                      pl.BlockSpec((1, d), lambda i, dst: (0, 0))],
            out_specs=pl.BlockSpec((tc, d), lambda i, dst: (i, 0)),
            scratch_shapes=[pltpu.VMEM((TOP_K, tc, d), F32), pltpu.SemaphoreType.DMA(())],
        ),
        out_shape=jax.ShapeDtypeStruct((t, d), F32),
        compiler_params=_params("arbitrary"),
        name="moe_combine",
    )(dest, ys, gates, x, g.reshape(1, d))


def _route(top_e, tm):
    t = top_e.shape[0]
    a = t * TOP_K
    e_flat = top_e.reshape(a)
    onehot = (e_flat[:, None] == jnp.arange(N_EXPERTS, dtype=jnp.int32)[None, :]).astype(jnp.int32)
    csum = jnp.cumsum(onehot, axis=0)
    rank = jnp.sum(onehot * (csum - 1), axis=1)
    counts = csum[-1]
    padded = (counts + tm - 1) // tm * tm
    pends = jnp.cumsum(padded)
    pstarts = pends - padded
    dest = (pstarts[e_flat] + rank).astype(jnp.int32)
    nblk = a // tm + N_EXPERTS
    tok_p = jnp.zeros((nblk * tm,), jnp.int32).at[dest].set(jnp.arange(a, dtype=jnp.int32) // TOP_K)
    blk_start = jnp.arange(nblk, dtype=jnp.int32) * tm
    blk_e = jnp.minimum(jnp.sum((pends[None, :] <= blk_start[:, None]).astype(jnp.int32), axis=1),
                        N_EXPERTS - 1)
    nused = (pends[-1:] // tm).astype(jnp.int32)
    return dest, tok_p, blk_e, nused


def _moe(x, g_in, g_out, w_router, wg, wu, wd, tm=512):
    h, e_pad, gates = _router(x, g_in, w_router)
    dest, tok_p, blk_e, nused = _route(e_pad[:, :TOP_K], tm)
    ys = _moe_experts(h, tok_p, blk_e, nused, wg, wu, wd, tm)
    return _combine(ys, dest, gates, x, g_out)


def kernel(x, mem, norm_g, mem_norm_g, w_mem_kv, a_w_in, a_sink, a_w_out, b_w_in, b_lambda,
           b_subln_g, b_w_out, ffn_w_gate, ffn_w_up, ffn_w_down, moe_w_router, moe_w_gate,
           moe_w_up, moe_w_down):
    batch, seq, d = x.shape
    n_mem = mem.shape[1]
    bf = lambda w: w.astype(BF16)
    xt = x.reshape(batch * seq, d)
    mkv = _norm_matmul(mem.reshape(batch * n_mem, d), mem_norm_g, bf(w_mem_kv),
                       jnp.ones((w_mem_kv.shape[1],), F32))

    g = norm_g[0]
    proj = _norm_matmul(xt, g[0], bf(a_w_in[0]), _query_col_scale(a_w_in.shape[2], A_Q_HEADS * HEAD_DIM))
    mix = _window_attention(proj, a_sink[0], batch, seq)
    mem_out = _mem_attention(proj, (A_Q_HEADS + 2 * A_KV_HEADS) * HEAD_DIM // MEM_WIDTH, mkv, batch, seq)
    xt = _out_proj(mix, mem_out, bf(a_w_out[0]), xt, g[1])
    xt = _ffn(xt, g[2], g[3], bf(ffn_w_gate[0]), bf(ffn_w_up[0]), bf(ffn_w_down[0]))

    g = norm_g[1]
    lambda_init = 0.8 - 0.6 * math.exp(-0.3 * 1)
    proj = _norm_matmul(xt, g[0], bf(b_w_in[0]), _query_col_scale(b_w_in.shape[2], MIX_WIDTH))
    mix = _diff_attention(proj, b_lambda[0], b_subln_g[0], lambda_init, batch, seq)
    mem_out = _mem_attention(proj, 3 * MIX_WIDTH // MEM_WIDTH, mkv, batch, seq)
    xt = _out_proj(mix, mem_out, bf(b_w_out[0]), xt, g[1])
    xt = _moe(xt, g[2], g[3], moe_w_router[0], bf(moe_w_gate[0]), bf(moe_w_up[0]), bf(moe_w_down[0]))
    return xt.reshape(batch, seq, d)
```

```python
import functools
import math

import jax
import jax.numpy as jnp
import numpy as np
from jax import lax
from jax.experimental import pallas as pl
from jax.experimental.pallas import tpu as pltpu

D_MODEL = 1024
HEAD_DIM = 64
MIX_WIDTH = 768
MEM_WIDTH = 256
MEM_HEADS = 4
A_Q_HEADS = 12
A_KV_HEADS = 4
A_GROUP = 3
WINDOW = 128
B_HEADS = 6
B_VDIM = 128
D_FF = 3584
N_EXPERTS = 8
TOP_K = 2
EPS = 1e-6
NEG_INF = -1e30
LOG2E = math.log2(math.e)
QK_SCALE = HEAD_DIM ** -0.5 * LOG2E
LANES = 128
SUBLANES = 8
VMEM_LIMIT_BYTES = 48 * 1024 * 1024

BF16 = jnp.bfloat16
F32 = jnp.float32


def _params(*sem):
    return pltpu.CompilerParams(dimension_semantics=sem, vmem_limit_bytes=VMEM_LIMIT_BYTES)


def _rms(x, g):
    return x * lax.rsqrt(jnp.mean(x * x, axis=-1, keepdims=True) + EPS) * g


def _dot(a, b):
    return jnp.dot(a, b, preferred_element_type=F32)


def _dot_nt(a, b):
    return lax.dot_general(a, b, (((1,), (1,)), ((), ())), preferred_element_type=F32)


def _alibi_slopes(n):
    return [2.0 ** (-8.0 * (i + 1) / n) for i in range(n)]


def _norm_matmul_kernel(x_ref, g_ref, w_ref, cs_ref, o_ref, *, chunk):
    h = _rms(x_ref[...], g_ref[...]).astype(BF16)
    for c in range(o_ref.shape[1] // chunk):
        sl = slice(c * chunk, (c + 1) * chunk)
        o_ref[:, sl] = (_dot(h, w_ref[:, sl]) * cs_ref[:, sl]).astype(o_ref.dtype)


def _norm_matmul(x, g, w, col_scale, tm=512, chunk=512):
    t, d = x.shape
    n = w.shape[1]
    tm = min(tm, t)
    return pl.pallas_call(
        functools.partial(_norm_matmul_kernel, chunk=min(chunk, n)),
        grid=(t // tm,),
        in_specs=[pl.BlockSpec((tm, d), lambda i: (i, 0)),
                  pl.BlockSpec((1, d), lambda i: (0, 0)),
                  pl.BlockSpec((d, n), lambda i: (0, 0)),
                  pl.BlockSpec((1, n), lambda i: (0, 0))],
        out_specs=pl.BlockSpec((tm, n), lambda i: (i, 0)),
        out_shape=jax.ShapeDtypeStruct((t, n), BF16),
        compiler_params=_params("parallel"),
        name="norm_matmul",
    )(x, g.reshape(1, d), w, col_scale.reshape(1, n))


def _query_col_scale(n, mix_q_cols):
    cols = np.arange(n)
    return jnp.asarray(np.where((cols < mix_q_cols) | (cols >= n - MEM_WIDTH), QK_SCALE, 1.0), F32)


def _window_kernel(sink_ref, q_ref, kp_ref, kc_ref, kn_ref, vp_ref, vc_ref, vn_ref, o_ref):
    n = pl.program_id(1)
    nb = pl.num_programs(1)
    blk = q_ref.shape[0]
    k = jnp.concatenate([kp_ref[...], kc_ref[...], kn_ref[...]], axis=0)
    v = jnp.concatenate([vp_ref[...], vc_ref[...], vn_ref[...]], axis=0)
    qi = lax.broadcasted_iota(jnp.int32, (blk, 3 * blk), 0)
    kj = lax.broadcasted_iota(jnp.int32, (blk, 3 * blk), 1)
    dist_i = jnp.abs(blk + qi - kj)
    valid = dist_i <= WINDOW
    valid = valid & jnp.logical_or(n > 0, kj >= blk)
    valid = valid & jnp.logical_or(n < nb - 1, kj < 2 * blk)
    dist = dist_i.astype(F32)
    q = q_ref[...]
    slopes = _alibi_slopes(A_Q_HEADS)
    outs = []
    for h in range(A_Q_HEADS):
        kh = h // A_GROUP
        qh = q[:, h * HEAD_DIM:(h + 1) * HEAD_DIM]
        kk = k[:, kh * HEAD_DIM:(kh + 1) * HEAD_DIM]
        vv = v[:, kh * HEAD_DIM:(kh + 1) * HEAD_DIM]
        s = _dot_nt(qh, kk) - (slopes[h] * LOG2E) * dist
        s = jnp.where(valid, s, NEG_INF)
        sink = sink_ref[h] * LOG2E
        m = jnp.maximum(jnp.max(s, axis=-1, keepdims=True), sink)
        p = jnp.exp2(s - m)
        denom = jnp.sum(p, axis=-1, keepdims=True) + jnp.exp2(sink - m)
        o = _dot(p.astype(BF16), vv)
        outs.append(o * (1.0 / denom))
    o_ref[...] = jnp.concatenate(outs, axis=-1).astype(o_ref.dtype)


def _window_attention(proj, sink, batch, seq):
    blk = WINDOW
    nb = seq // blk
    kcol = MIX_WIDTH // 256
    vcol = kcol + 1

    def cur(b, n, s):
        return b * nb + n

    def prev(b, n, s):
        return b * nb + jnp.maximum(n - 1, 0)

    def nxt(b, n, s):
        return b * nb + jnp.minimum(n + 1, nb - 1)

    def spec(row, col):
        return pl.BlockSpec((blk, 256), lambda b, n, s: (row(b, n, s), col))

    return pl.pallas_call(
        _window_kernel,
        grid_spec=pltpu.PrefetchScalarGridSpec(
            num_scalar_prefetch=1,
            grid=(batch, nb),
            in_specs=[pl.BlockSpec((blk, MIX_WIDTH), lambda b, n, s: (b * nb + n, 0)),
                      spec(prev, kcol), spec(cur, kcol), spec(nxt, kcol),
                      spec(prev, vcol), spec(cur, vcol), spec(nxt, vcol)],
            out_specs=pl.BlockSpec((blk, MIX_WIDTH), lambda b, n, s: (b * nb + n, 0)),
        ),
        out_shape=jax.ShapeDtypeStruct((batch * seq, MIX_WIDTH), BF16),
        compiler_params=_params("parallel", "parallel"),
        name="window_attention",
    )(sink, proj, proj, proj, proj, proj, proj, proj)


def _mem_attn_kernel(q_ref, mkv_ref, o_ref):
    q = q_ref[...]
    mkv = mkv_ref[...]
    outs = []
    for h in range(MEM_HEADS):
        sl = slice(h * HEAD_DIM, (h + 1) * HEAD_DIM)
        s = _dot_nt(q[:, sl], mkv[:, sl])
        m = jnp.max(s, axis=-1, keepdims=True)
        p = jnp.exp2(s - m)
        denom = jnp.sum(p, axis=-1, keepdims=True)
        o = _dot(p.astype(BF16), mkv[:, MEM_WIDTH + h * HEAD_DIM:MEM_WIDTH + (h + 1) * HEAD_DIM])
        outs.append(o * (1.0 / denom))
    o_ref[...] = jnp.concatenate(outs, axis=-1).astype(o_ref.dtype)


def _mem_attention(proj, qcol, mkv, batch, seq, tm=512):
    tm = min(tm, seq)
    per_batch = seq // tm
    n_mem = mkv.shape[0] // batch
    return pl.pallas_call(
        _mem_attn_kernel,
        grid=(batch * per_batch,),
        in_specs=[pl.BlockSpec((tm, MEM_WIDTH), lambda i: (i, qcol)),
                  pl.BlockSpec((n_mem, 2 * MEM_WIDTH), lambda i: (i // per_batch, 0))],
        out_specs=pl.BlockSpec((tm, MEM_WIDTH), lambda i: (i, 0)),
        out_shape=jax.ShapeDtypeStruct((batch * seq, MEM_WIDTH), BF16),
        compiler_params=_params("parallel"),
        name="mem_attention",
    )(proj, mkv)


def _out_proj_kernel(mix_ref, mem_ref, w1_ref, w2_ref, x_ref, g_ref, o_ref):
    o = _dot(mix_ref[...], w1_ref[...]) + _dot(mem_ref[...], w2_ref[...])
    o_ref[...] = x_ref[...] + _rms(o, g_ref[...])


def _out_proj(mix, mem_out, w_out, x, g, tm=512):
    t, d = x.shape
    tm = min(tm, t)
    w1 = w_out[:MIX_WIDTH]
    w2 = w_out[MIX_WIDTH:]
    return pl.pallas_call(
        _out_proj_kernel,
        grid=(t // tm,),
        in_specs=[pl.BlockSpec((tm, MIX_WIDTH), lambda i: (i, 0)),
                  pl.BlockSpec((tm, MEM_WIDTH), lambda i: (i, 0)),
                  pl.BlockSpec((MIX_WIDTH, d), lambda i: (0, 0)),
                  pl.BlockSpec((MEM_WIDTH, d), lambda i: (0, 0)),
                  pl.BlockSpec((tm, d), lambda i: (i, 0)),
                  pl.BlockSpec((1, d), lambda i: (0, 0))],
        out_specs=pl.BlockSpec((tm, d), lambda i: (i, 0)),
        out_shape=jax.ShapeDtypeStruct((t, d), F32),
        compiler_params=_params("parallel"),
        name="out_proj",
    )(mix, mem_out, w1, w2, x, g.reshape(1, d))


def _swiglu_tile(h, wg, wu, wd):
    a = _dot(h, wg)
    u = _dot(h, wu)
    z = (a * jax.nn.sigmoid(a) * u).astype(BF16)
    return _dot(z, wd)


def _ffn_kernel(x_ref, gin_ref, gout_ref, wg_ref, wu_ref, wd_ref, o_ref, h_sc, acc_sc):
    f = pl.program_id(1)

    @pl.when(f == 0)
    def _():
        h_sc[...] = _rms(x_ref[...], gin_ref[...]).astype(BF16)
        acc_sc[...] = jnp.zeros_like(acc_sc)

    acc_sc[...] += _swiglu_tile(h_sc[...], wg_ref[...], wu_ref[...], wd_ref[...])

    @pl.when(f == pl.num_programs(1) - 1)
    def _():
        o_ref[...] = x_ref[...] + _rms(acc_sc[...], gout_ref[...])


def _ffn(x, g_in, g_out, wg, wu, wd, tm=512, tf=512):
    t, d = x.shape
    ff = wg.shape[1]
    tm = min(tm, t)
    return pl.pallas_call(
        _ffn_kernel,
        grid=(t // tm, ff // tf),
        in_specs=[pl.BlockSpec((tm, d), lambda i, f: (i, 0)),
                  pl.BlockSpec((1, d), lambda i, f: (0, 0)),
                  pl.BlockSpec((1, d), lambda i, f: (0, 0)),
                  pl.BlockSpec((d, tf), lambda i, f: (0, f)),
                  pl.BlockSpec((d, tf), lambda i, f: (0, f)),
                  pl.BlockSpec((tf, d), lambda i, f: (f, 0))],
        out_specs=pl.BlockSpec((tm, d), lambda i, f: (i, 0)),
        out_shape=jax.ShapeDtypeStruct((t, d), F32),
        scratch_shapes=[pltpu.VMEM((tm, d), BF16), pltpu.VMEM((tm, d), F32)],
        compiler_params=_params("parallel", "arbitrary"),
        name="ffn",
    )(x, g_in.reshape(1, d), g_out.reshape(1, d), wg, wu, wd)


_POS_SPLIT = 256
_LEFT, _DIAG, _RIGHT = 0, 1, 2
_BOUND_LANE = 12
_CHUNK_LANE = 15
_ZERO_EXP2_ARG = 152.0
_MIN_DENOM = 2.0 ** -60


def _bf16_split(x):
    hi = x.astype(BF16)
    r = x - hi.astype(F32)
    mid = r.astype(BF16)
    lo = (r - mid.astype(F32)).astype(BF16)
    return hi, mid, lo


def _diff_attn_kernel(reach_ref, slope_ref, q_ref, k_ref, v_ref, qf_ref, kf_ref, dist_ref, lam_ref,
                      g_ref, o_ref, qa_sc, ka_sc, va_sc, k2_sc, m_sc, acc_sc, sa_sc, sb_sc, *, tk,
                      out_scale, lambda_init):
    h = pl.program_id(1)
    i = pl.program_id(2)
    tq = q_ref.shape[0]
    seq = k_ref.shape[0]
    nk = seq // tk
    slope = slope_ref[h]
    reach = reach_ref[h]

    @pl.when(i == 0)
    def _():
        k = k_ref[...]
        ka_sc[:, :B_VDIM] = k
        va_sc[:, :B_VDIM] = v_ref[...]
        lane = lax.broadcasted_iota(jnp.int32, (seq, LANES), 1)
        va_sc[:, B_VDIM:] = jnp.where(lane == 0, 1.0, 0.0).astype(BF16)
        kf = kf_ref[...]
        flane = lax.broadcasted_iota(jnp.int32, kf.shape, 1)
        chunk_lanes = jnp.logical_and(flane >= _CHUNK_LANE, flane < _CHUNK_LANE + 3)
        for c in range(nk):
            ka_sc[c * tk:(c + 1) * tk, B_VDIM:] = jnp.where(chunk_lanes, float(c), kf).astype(BF16)
        kk = k.astype(F32)
        kk = kk * kk
        for c, sel in enumerate((lane < HEAD_DIM, lane >= HEAD_DIM)):
            n2 = jnp.sum(jnp.where(sel, kk, 0.0), axis=-1, keepdims=True)
            k2_sc[c] = jnp.broadcast_to(jnp.max(n2, axis=0, keepdims=True), k2_sc.shape[1:])

    q = q_ref[...]
    lane = lax.broadcasted_iota(jnp.int32, q.shape, 1)
    zero = jnp.zeros_like(q)
    q0 = jnp.where(lane < HEAD_DIM, q, zero)
    q1 = jnp.where(lane >= HEAD_DIM, q, zero)
    qq = q.astype(F32)
    qq = qq * qq
    n0 = jnp.sum(jnp.where(lane < HEAD_DIM, qq, 0.0), axis=-1, keepdims=True)
    n1 = jnp.sum(jnp.where(lane >= HEAD_DIM, qq, 0.0), axis=-1, keepdims=True)
    bound = jnp.concatenate([jnp.sqrt(n0) * jnp.sqrt(k2_sc[0][0:1, 0:1]),
                             jnp.sqrt(n1) * jnp.sqrt(k2_sc[1][0:1, 0:1])], axis=0)
    off = slope * (i * tk).astype(F32)
    qf = qf_ref[...]
    lane2 = lax.broadcasted_iota(jnp.int32, (2 * tq, LANES), 1)

    def build_queries(with_bound):
        for variant, feat, row_term in ((_LEFT, -qf, bound + off), (_DIAG, zero, bound),
                                        (_RIGHT, qf, bound - off)):
            feat = jnp.concatenate([feat, feat], axis=0)
            if with_bound:
                for n, piece in enumerate(_bf16_split(row_term)):
                    feat = jnp.where(lane2 == _BOUND_LANE + n, piece, feat)
            qa_sc[variant, :, B_VDIM:] = feat
            qa_sc[variant, :tq, :B_VDIM] = q0
            qa_sc[variant, tq:, :B_VDIM] = q1

    def scores(j, s_ref):
        variant = jnp.where(j < i, _LEFT, jnp.where(j == i, _DIAG, _RIGHT))
        start = pl.multiple_of(j * tk, tk)
        s_ref[...] = _dot_nt(qa_sc[variant], ka_sc[pl.ds(start, tk), :])

    def diag_bias(j, s_ref):
        @pl.when(j == i)
        def _():
            bias = slope * dist_ref[...]
            s_ref[:tq] -= bias
            s_ref[tq:] -= bias

    def exp_pv(j, s_ref):
        start = pl.multiple_of(j * tk, tk)
        p = jnp.exp2(s_ref[...]).astype(BF16)
        acc_sc[...] += _dot(p, va_sc[pl.ds(start, tk), :])

    build_queries(True)
    acc_sc[...] = jnp.zeros_like(acc_sc)
    first_pair = jnp.maximum(i - reach, 0) // 2
    last_pair = (jnp.minimum(i + reach + 1, nk) + 1) // 2
    scores(2 * first_pair, sa_sc)
    diag_bias(2 * first_pair, sa_sc)

    def pair(jj, carry):
        j = 2 * jj
        scores(j + 1, sb_sc)
        exp_pv(j, sa_sc)
        diag_bias(j + 1, sb_sc)
        nxt = jnp.minimum(j + 2, nk - 1)
        scores(nxt, sa_sc)
        exp_pv(j + 1, sb_sc)
        diag_bias(nxt, sa_sc)
        return carry

    lax.fori_loop(first_pair, last_pair, pair, 0)

    denom_min = jnp.min(acc_sc[:, B_VDIM:B_VDIM + 1])

    @pl.when(jnp.logical_not(denom_min >= _MIN_DENOM))
    def _():
        build_queries(False)
        m_sc[...] = jnp.full_like(m_sc, NEG_INF)
        acc_sc[...] = jnp.zeros_like(acc_sc)

        def chunk(j, carry):
            scores(j, sa_sc)
            diag_bias(j, sa_sc)
            adj = jnp.where(j > i, off, jnp.where(j < i, -off, 0.0))
            start = pl.multiple_of(j * tk, tk)
            s = sa_sc[...]
            m_prev = m_sc[...]
            m_new = jnp.maximum(m_prev, jnp.max(s, axis=-1, keepdims=True) + adj)
            p = jnp.exp2(s - (m_new - adj)).astype(BF16)
            acc_sc[...] = jnp.exp2(m_prev - m_new) * acc_sc[...] + _dot(p, va_sc[pl.ds(start, tk), :])
            m_sc[...] = m_new
            return carry

        lax.fori_loop(0, nk, chunk, 0)

    lp = lam_ref[...]
    lam = (jnp.exp(jnp.sum(lp[0:1] * lp[1:2], axis=-1, keepdims=True))
           - jnp.exp(jnp.sum(lp[2:3] * lp[3:4], axis=-1, keepdims=True)) + lambda_init)
    acc = acc_sc[...]
    o0 = acc[:tq, :B_VDIM] * (1.0 / acc[:tq, B_VDIM:B_VDIM + 1])
    o1 = acc[tq:, :B_VDIM] * (1.0 / acc[tq:, B_VDIM:B_VDIM + 1])
    o_ref[...] = (_rms(o0 - lam * o1, g_ref[...]) * out_scale).astype(o_ref.dtype)


def _bf16_pieces(x, n=3):
    out = []
    r = np.float64(x)
    for _ in range(n):
        p = np.float64(np.float32(r).astype(jnp.bfloat16).astype(np.float32))
        out.append(p)
        r = r - p
    return out


def _alibi_features(slopes2, t):
    pos = np.arange(t)
    hi = (pos // _POS_SPLIT) * _POS_SPLIT
    lo = pos % _POS_SPLIT
    qf = np.zeros((len(slopes2), t, LANES), np.float32)
    kf = np.zeros((len(slopes2), t, LANES), np.float32)
    for h, s in enumerate(slopes2):
        for n, piece in enumerate(_bf16_pieces(s)):
            for base, part in ((0, hi), (3, lo)):
                qf[h, :, base + n] = -piece
                kf[h, :, base + n] = part
                qf[h, :, 6 + base + n] = part
                kf[h, :, 6 + base + n] = piece
        for n, piece in enumerate(_bf16_pieces(s * t)):
            qf[h, :, _CHUNK_LANE + n] = -piece
        kf[h, :, _BOUND_LANE:_BOUND_LANE + 3] = -1.0
    return jnp.asarray(qf, BF16), jnp.asarray(kf, BF16)


def _diff_attention(proj, b_lambda, subln_g, lambda_init, batch, seq, t=512):
    t = min(t, seq // 2)
    nq = seq // t
    assert seq % (2 * t) == 0
    kcol = MIX_WIDTH // B_VDIM
    vcol = 2 * kcol
    slopes2 = [s * LOG2E for s in _alibi_slopes(B_HEADS)]
    reach = [min(nq, int(math.floor((_ZERO_EXP2_ARG / s - 1.0) / t)) + 1) for s in slopes2]
    qf, kf = _alibi_features(slopes2, t)
    pos = np.arange(t)
    dist = jnp.asarray(np.abs(pos[:, None] - pos[None, :]), F32)
    kernel = functools.partial(_diff_attn_kernel, tk=t, out_scale=1.0 - lambda_init,
                               lambda_init=lambda_init)
    return pl.pallas_call(
        kernel,
        grid_spec=pltpu.PrefetchScalarGridSpec(
            num_scalar_prefetch=2,
            grid=(batch, B_HEADS, nq),
            in_specs=[pl.BlockSpec((t, B_VDIM), lambda b, h, i, r, s: (b * nq + i, h)),
                      pl.BlockSpec((seq, B_VDIM), lambda b, h, i, r, s: (b, kcol + h)),
                      pl.BlockSpec((seq, B_VDIM), lambda b, h, i, r, s: (b, vcol + h)),
                      pl.BlockSpec((None, t, LANES), lambda b, h, i, r, s: (h, 0, 0)),
                      pl.BlockSpec((None, t, LANES), lambda b, h, i, r, s: (h, 0, 0)),
                      pl.BlockSpec((t, t), lambda b, h, i, r, s: (0, 0)),
                      pl.BlockSpec((4, HEAD_DIM), lambda b, h, i, r, s: (0, 0)),
                      pl.BlockSpec((1, B_VDIM), lambda b, h, i, r, s: (0, 0))],
            out_specs=pl.BlockSpec((t, B_VDIM), lambda b, h, i, r, s: (b * nq + i, h)),
            scratch_shapes=[pltpu.VMEM((3, 2 * t, 2 * LANES), BF16),
                            pltpu.VMEM((seq, 2 * LANES), BF16),
                            pltpu.VMEM((seq, 2 * LANES), BF16),
                            pltpu.VMEM((2, 8, LANES), F32),
                            pltpu.VMEM((2 * t, 1), F32),
                            pltpu.VMEM((2 * t, 2 * LANES), F32),
                            pltpu.VMEM((2 * t, t), F32),
                            pltpu.VMEM((2 * t, t), F32)],
        ),
        out_shape=jax.ShapeDtypeStruct((batch * seq, MIX_WIDTH), BF16),
        compiler_params=_params("arbitrary", "arbitrary", "arbitrary"),
        name="diff_attention",
    )(jnp.asarray(reach, jnp.int32), jnp.asarray(slopes2, F32), proj, proj, proj, qf, kf, dist,
      b_lambda, subln_g.reshape(1, B_VDIM))


def _router_kernel(x_ref, g_ref, whi_ref, wlo_ref, h_ref, e_ref, gate_ref):
    h = _rms(x_ref[...], g_ref[...])
    h_hi = h.astype(BF16)
    h_lo = (h - h_hi.astype(F32)).astype(BF16)
    whi = whi_ref[...]
    logits = _dot(h_hi, whi) + (_dot(h_hi, wlo_ref[...]) + _dot(h_lo, whi))
    lane = lax.broadcasted_iota(jnp.int32, logits.shape, 1)
    logits = jnp.where(lane < N_EXPERTS, logits, -jnp.inf)
    m1 = jnp.max(logits, axis=-1, keepdims=True)
    i1 = jnp.min(jnp.where(logits == m1, lane, LANES), axis=-1, keepdims=True)
    rest = jnp.where(lane == i1, -jnp.inf, logits)
    m2 = jnp.max(rest, axis=-1, keepdims=True)
    i2 = jnp.min(jnp.where(rest == m2, lane, LANES), axis=-1, keepdims=True)
    r = jnp.exp(m2 - m1)
    g1 = 1.0 / (1.0 + r)
    g2 = r * g1
    e_ref[...] = jnp.where(lane == 0, i1, jnp.where(lane == 1, i2, 0))
    gate_ref[...] = jnp.where(lane == 0, g1, jnp.where(lane == 1, g2, 0.0))
    h_ref[...] = h_hi


def _router(x, g, w_router, tm=512):
    t, d = x.shape
    tm = min(tm, t)
    wpad = jnp.zeros((d, LANES), F32).at[:, :N_EXPERTS].set(w_router)
    whi = wpad.astype(BF16)
    wlo = (wpad - whi.astype(F32)).astype(BF16)
    return pl.pallas_call(
        _router_kernel,
        grid=(t // tm,),
        in_specs=[pl.BlockSpec((tm, d), lambda i: (i, 0)),
                  pl.BlockSpec((1, d), lambda i: (0, 0)),
                  pl.BlockSpec((d, LANES), lambda i: (0, 0)),
                  pl.BlockSpec((d, LANES), lambda i: (0, 0))],
        out_specs=[pl.BlockSpec((tm, d), lambda i: (i, 0)),
                   pl.BlockSpec((tm, LANES), lambda i: (i, 0)),
                   pl.BlockSpec((tm, LANES), lambda i: (i, 0))],
        out_shape=[jax.ShapeDtypeStruct((t, d), BF16),
                   jax.ShapeDtypeStruct((t, LANES), jnp.int32),
                   jax.ShapeDtypeStruct((t, LANES), F32)],
        compiler_params=_params("parallel"),
        name="router",
    )(x, g.reshape(1, d), whi, wlo)


def _dispatch_kernel(slab_ref, fill_ref, h_ref, dest_ref, gate_ref, xs_hbm, gs_hbm,
                     xslab_sc, gslab_sc, zx_sc, zg_sc, sem, zsem):
    i = pl.program_id(0)
    n = pl.num_programs(0)
    slot = i % 2
    tc = h_ref.shape[0]
    tm = zx_sc.shape[0]

    def slab_copies(step, sl):
        out = []
        for e in range(N_EXPERTS):
            start = pl.multiple_of(slab_ref[step * N_EXPERTS + e], SUBLANES)
            rows = pl.ds(e * tc, tc)
            out.append(pltpu.make_async_copy(xslab_sc.at[sl, rows], xs_hbm.at[pl.ds(start, tc)], sem.at[sl]))
            out.append(pltpu.make_async_copy(gslab_sc.at[sl, rows], gs_hbm.at[pl.ds(start, tc)], sem.at[sl]))
        return out

    def fill_copies(start):
        start = pl.multiple_of(start, SUBLANES)
        return [pltpu.make_async_copy(zx_sc, xs_hbm.at[pl.ds(start, tm)], zsem),
                pltpu.make_async_copy(zg_sc, gs_hbm.at[pl.ds(start, tm)], zsem)]

    @pl.when(i == 0)
    def _():
        zx_sc[...] = jnp.zeros_like(zx_sc)
        zg_sc[...] = jnp.zeros_like(zg_sc)
        for half in range(2):
            for e in range(N_EXPERTS):
                for c in fill_copies(fill_ref[half * N_EXPERTS + e]):
                    c.start()
            for e in range(N_EXPERTS):
                for c in fill_copies(fill_ref[half * N_EXPERTS + e]):
                    c.wait()
        first_free = fill_ref[2 * N_EXPERTS]
        n_tiles = xs_hbm.shape[0] // tm

        def start_tile(b, carry):
            for c in fill_copies(b * tm):
                c.start()
            return carry

        def wait_tile(b, carry):
            for c in fill_copies(b * tm):
                c.wait()
            return carry

        lax.fori_loop(first_free, n_tiles, start_tile, 0)
        lax.fori_loop(first_free, n_tiles, wait_tile, 0)

    dest = dest_ref[0]
    pos = lax.broadcasted_iota(jnp.int32, (tc, tc), 0)
    sel0, sel1 = [], []
    for e in range(N_EXPERTS):
        start = slab_ref[i * N_EXPERTS + e]
        sel0.append(jnp.where(dest[0:1, :] - start == pos, 1.0, 0.0))
        sel1.append(jnp.where(dest[1:2, :] - start == pos, 1.0, 0.0))
    sel0 = jnp.concatenate(sel0, axis=0).astype(BF16)
    sel1 = jnp.concatenate(sel1, axis=0).astype(BF16)
    gate = gate_ref[...]
    lane = lax.broadcasted_iota(jnp.int32, gate.shape, 1)

    def gate_pieces(col):
        out = jnp.zeros(gate.shape, F32)
        for n_, piece in enumerate(_bf16_split(col)):
            out = jnp.where(lane == n_, piece.astype(F32), out)
        return out.astype(BF16)

    xslab_sc[slot] = _dot(sel0 + sel1, h_ref[...])
    gslab_sc[slot] = _dot(sel0, gate_pieces(gate[:, 0:1])) + _dot(sel1, gate_pieces(gate[:, 1:2]))

    @pl.when(i > 0)
    def _():
        for c in slab_copies(i - 1, 1 - slot):
            c.wait()

    for c in slab_copies(i, slot):
        c.start()

    @pl.when(i == n - 1)
    def _():
        for c in slab_copies(i, slot):
            c.wait()


def _dispatch(h, dest_t, gates, slab, fill, p_rows, tm, tc):
    t, d = h.shape
    nb = t // tc
    return pl.pallas_call(
        _dispatch_kernel,
        grid_spec=pltpu.PrefetchScalarGridSpec(
            num_scalar_prefetch=2,
            grid=(nb,),
            in_specs=[pl.BlockSpec((tc, d), lambda i, s, z: (i, 0)),
                      pl.BlockSpec((1, TOP_K, tc), lambda i, s, z: (i, 0, 0)),
                      pl.BlockSpec((tc, LANES), lambda i, s, z: (i, 0))],
            out_specs=[pl.BlockSpec(memory_space=pl.ANY), pl.BlockSpec(memory_space=pl.ANY)],
            scratch_shapes=[pltpu.VMEM((2, N_EXPERTS * tc, d), F32),
                            pltpu.VMEM((2, N_EXPERTS * tc, LANES), F32),
                            pltpu.VMEM((tm, d), F32), pltpu.VMEM((tm, LANES), F32),
                            pltpu.SemaphoreType.DMA((2,)), pltpu.SemaphoreType.DMA(())],
        ),
        out_shape=[jax.ShapeDtypeStruct((p_rows, d), F32), jax.ShapeDtypeStruct((p_rows, LANES), F32)],
        compiler_params=_params("arbitrary"),
        name="moe_dispatch",
    )(slab, fill, h, dest_t, gates)


def _moe_kernel(blk_e_ref, used_ref, xblk_ref, x_ref, gs_ref, wg_ref, wu_ref, wd_ref, o_ref,
                xb_sc, acc_sc):
    i = pl.program_id(0)
    f = pl.program_id(1)
    used = used_ref[i] > 0

    @pl.when(jnp.logical_and(used, f == 0))
    def _():
        xb_sc[...] = x_ref[...].astype(BF16)
        acc_sc[...] = jnp.zeros_like(acc_sc)

    @pl.when(used)
    def _():
        acc_sc[...] += _swiglu_tile(xb_sc[...], wg_ref[...], wu_ref[...], wd_ref[...])

    @pl.when(f == pl.num_programs(1) - 1)
    def _():
        gs = gs_ref[...]
        gate = gs[:, 0:1] + gs[:, 1:2] + gs[:, 2:3]
        o_ref[...] = jnp.where(used, acc_sc[...] * gate, 0.0)


def _moe_experts(xs, gs, blk_e, blk_used, xblk, wg, wu, wd, tm, tf=512):
    p, d = xs.shape
    nblk = p // tm
    nf = wg.shape[2] // tf

    def fidx(i, f, used_ref):
        return jnp.where(used_ref[i] > 0, f, nf - 1)

    return pl.pallas_call(
        _moe_kernel,
        grid_spec=pltpu.PrefetchScalarGridSpec(
            num_scalar_prefetch=3,
            grid=(nblk, nf),
            in_specs=[pl.BlockSpec((tm, d), lambda i, f, be, us, xb: (xb[i], 0)),
                      pl.BlockSpec((tm, LANES), lambda i, f, be, us, xb: (xb[i], 0)),
                      pl.BlockSpec((None, d, tf), lambda i, f, be, us, xb: (be[i], 0, fidx(i, f, us))),
                      pl.BlockSpec((None, d, tf), lambda i, f, be, us, xb: (be[i], 0, fidx(i, f, us))),
                      pl.BlockSpec((None, tf, d), lambda i, f, be, us, xb: (be[i], fidx(i, f, us), 0))],
            out_specs=pl.BlockSpec((tm, d), lambda i, f, be, us, xb: (i, 0)),
            scratch_shapes=[pltpu.VMEM((tm, d), BF16), pltpu.VMEM((tm, d), F32)],
        ),
        out_shape=jax.ShapeDtypeStruct((p, d), F32),
        compiler_params=_params("arbitrary", "arbitrary"),
        name="moe_experts",
    )(blk_e, blk_used, xblk, xs, gs, wg, wu, wd)


def _combine_kernel(slab_ref, ys_hbm, dest_ref, x_ref, g_ref, o_ref, slab_sc, sem):
    i = pl.program_id(0)
    n = pl.num_programs(0)
    slot = i % 2
    tc = x_ref.shape[0]

    def slab_copies(step, sl):
        return [pltpu.make_async_copy(
                    ys_hbm.at[pl.ds(pl.multiple_of(slab_ref[step * N_EXPERTS + e], SUBLANES), tc)],
                    slab_sc.at[sl, pl.ds(e * tc, tc)], sem.at[sl])
                for e in range(N_EXPERTS)]

    @pl.when(i == 0)
    def _():
        for c in slab_copies(0, 0):
            c.start()

    @pl.when(i + 1 < n)
    def _():
        for c in slab_copies(i + 1, 1 - slot):
            c.start()

    dest = dest_ref[...]
    pos = lax.broadcasted_iota(jnp.int32, (tc, tc), 1)
    sel = []
    for e in range(N_EXPERTS):
        start = slab_ref[i * N_EXPERTS + e]
        hit = jnp.logical_or(dest[:, 0:1] - start == pos, dest[:, 1:2] - start == pos)
        sel.append(jnp.where(hit, 1.0, 0.0))
    sel = jnp.concatenate(sel, axis=1).astype(BF16)

    for c in slab_copies(i, slot):
        c.wait()
    y = _dot(sel, slab_sc[slot].astype(BF16))
    o_ref[...] = x_ref[...] + _rms(y, g_ref[...])


def _combine(ys, dest, slab, x, g, tc):
    t, d = x.shape
    return pl.pallas_call(
        _combine_kernel,
        grid_spec=pltpu.PrefetchScalarGridSpec(
            num_scalar_prefetch=1,
            grid=(t // tc,),
            in_specs=[pl.BlockSpec(memory_space=pl.ANY),
                      pl.BlockSpec((tc, LANES), lambda i, s: (i, 0)),
                      pl.BlockSpec((tc, d), lambda i, s: (i, 0)),
                      pl.BlockSpec((1, d), lambda i, s: (0, 0))],
            out_specs=pl.BlockSpec((tc, d), lambda i, s: (i, 0)),
            scratch_shapes=[pltpu.VMEM((2, N_EXPERTS * tc, d), F32), pltpu.SemaphoreType.DMA((2,))],
        ),
        out_shape=jax.ShapeDtypeStruct((t, d), F32),
        compiler_params=_params("arbitrary"),
        name="moe_combine",
    )(slab, ys, dest, x, g.reshape(1, d))


def _route(top_e, tm, tc):
    t = top_e.shape[0]
    a = t * TOP_K
    nb = t // tc
    e_flat = top_e.reshape(a)
    onehot = (e_flat[:, None] == jnp.arange(N_EXPERTS, dtype=jnp.int32)[None, :]).astype(jnp.int32)
    csum = jnp.cumsum(onehot, axis=0)
    rank = jnp.sum(onehot * (csum - 1), axis=1)
    upto = csum[TOP_K * tc - 1::TOP_K * tc]
    before = jnp.concatenate([jnp.zeros((1, N_EXPERTS), jnp.int32), upto[:nb - 1]], axis=0)
    rows = (upto - before + SUBLANES - 1) // SUBLANES * SUBLANES
    rows_before = jnp.cumsum(rows, axis=0) - rows
    total = jnp.sum(rows, axis=0)
    region = (total + 2 * tm - 1) // tm * tm
    rends = jnp.cumsum(region)
    rstarts = rends - region
    slab = (rstarts[None, :] + rows_before).astype(jnp.int32)
    blk_of = jnp.arange(a, dtype=jnp.int32) // (TOP_K * tc)
    dest = (slab[blk_of, e_flat] + rank - before[blk_of, e_flat]).astype(jnp.int32).reshape(t, TOP_K)
    slab = slab.reshape(nb * N_EXPERTS)
    row_end = rstarts + total
    fill = jnp.concatenate([row_end, rends - tm, rends[-1:] // tm]).astype(jnp.int32)
    nblk = -(-(a + nb * N_EXPERTS * (SUBLANES - 1)) // tm) + 2 * N_EXPERTS
    blk_start = jnp.arange(nblk, dtype=jnp.int32) * tm
    blk_e = jnp.minimum(jnp.sum((rends[None, :] <= blk_start[:, None]).astype(jnp.int32), axis=1),
                        N_EXPERTS - 1)
    blk_used = jnp.logical_and(blk_start < row_end[blk_e], blk_start < rends[-1]).astype(jnp.int32)
    first_used = jnp.argmax(blk_used).astype(jnp.int32)
    xblk = jnp.where(blk_used > 0, jnp.arange(nblk, dtype=jnp.int32), first_used)
    return dest, slab, fill, blk_e, blk_used, xblk, nblk * tm


def _moe(x, g_in, g_out, w_router, wg, wu, wd, tm=512, tc=256):
    t = x.shape[0]
    tc = min(tc, t)
    h, e_pad, gates = _router(x, g_in, w_router)
    dest, slab, fill, blk_e, blk_used, xblk, p_rows = _route(e_pad[:, :TOP_K], tm, tc)
    dest_t = dest.reshape(t // tc, tc, TOP_K).transpose(0, 2, 1)
    dest_pad = jnp.zeros((t, LANES), jnp.int32).at[:, :TOP_K].set(dest)
    xs, gs = _dispatch(h, dest_t, gates, slab, fill, p_rows, tm, tc)
    ys = _moe_experts(xs, gs, blk_e, blk_used, xblk, wg, wu, wd, tm)
    return _combine(ys, dest_pad, slab, x, g_out, tc)


def kernel(x, mem, norm_g, mem_norm_g, w_mem_kv, a_w_in, a_sink, a_w_out, b_w_in, b_lambda,
           b_subln_g, b_w_out, ffn_w_gate, ffn_w_up, ffn_w_down, moe_w_router, moe_w_gate,
           moe_w_up, moe_w_down):
    batch, seq, d = x.shape
    n_mem = mem.shape[1]
    bf = lambda w: w.astype(BF16)
    xt = x.reshape(batch * seq, d)
    mkv = _norm_matmul(mem.reshape(batch * n_mem, d), mem_norm_g, bf(w_mem_kv),
                       jnp.ones((w_mem_kv.shape[1],), F32))

    g = norm_g[0]
    proj = _norm_matmul(xt, g[0], bf(a_w_in[0]), _query_col_scale(a_w_in.shape[2], A_Q_HEADS * HEAD_DIM))
    mix = _window_attention(proj, a_sink[0], batch, seq)
    mem_out = _mem_attention(proj, (A_Q_HEADS + 2 * A_KV_HEADS) * HEAD_DIM // MEM_WIDTH, mkv, batch, seq)
    xt = _out_proj(mix, mem_out, bf(a_w_out[0]), xt, g[1])
    xt = _ffn(xt, g[2], g[3], bf(ffn_w_gate[0]), bf(ffn_w_up[0]), bf(ffn_w_down[0]))

    g = norm_g[1]
    lambda_init = 0.8 - 0.6 * math.exp(-0.3 * 1)
    proj = _norm_matmul(xt, g[0], bf(b_w_in[0]), _query_col_scale(b_w_in.shape[2], MIX_WIDTH))
    mix = _diff_attention(proj, b_lambda[0], b_subln_g[0], lambda_init, batch, seq)
    mem_out = _mem_attention(proj, 3 * MIX_WIDTH // MEM_WIDTH, mkv, batch, seq)
    xt = _out_proj(mix, mem_out, bf(b_w_out[0]), xt, g[1])
    xt = _moe(xt, g[2], g[3], moe_w_router[0], bf(moe_w_gate[0]), bf(moe_w_up[0]), bf(moe_w_down[0]))
    return xt.reshape(batch, seq, d)
```

```python
import functools
import math

import jax
import jax.numpy as jnp
import numpy as np
from jax import lax
from jax.experimental import pallas as pl
from jax.experimental.pallas import tpu as pltpu

D_MODEL = 1024
HEAD_DIM = 64
MIX_WIDTH = 768
MEM_WIDTH = 256
MEM_HEADS = 4
A_Q_HEADS = 12
A_KV_HEADS = 4
A_GROUP = 3
WINDOW = 128
B_HEADS = 6
B_VDIM = 128
D_FF = 3584
N_EXPERTS = 8
TOP_K = 2
EPS = 1e-6
NEG_INF = -1e30
LOG2E = math.log2(math.e)
QK_SCALE = HEAD_DIM ** -0.5 * LOG2E
LANES = 128
SUBLANES = 8
VMEM_LIMIT_BYTES = 48 * 1024 * 1024

BF16 = jnp.bfloat16
F32 = jnp.float32


def _params(*sem):
    return pltpu.CompilerParams(dimension_semantics=sem, vmem_limit_bytes=VMEM_LIMIT_BYTES)


def _rms(x, g):
    return x * lax.rsqrt(jnp.mean(x * x, axis=-1, keepdims=True) + EPS) * g


def _dot(a, b):
    return jnp.dot(a, b, preferred_element_type=F32)


def _dot_nt(a, b):
    return lax.dot_general(a, b, (((1,), (1,)), ((), ())), preferred_element_type=F32)


def _alibi_slopes(n):
    return [2.0 ** (-8.0 * (i + 1) / n) for i in range(n)]


def _norm_matmul_kernel(x_ref, g_ref, w_ref, cs_ref, o_ref, *, chunk):
    h = _rms(x_ref[...], g_ref[...]).astype(BF16)
    for c in range(o_ref.shape[1] // chunk):
        sl = slice(c * chunk, (c + 1) * chunk)
        o_ref[:, sl] = (_dot(h, w_ref[:, sl]) * cs_ref[:, sl]).astype(o_ref.dtype)


def _norm_matmul(x, g, w, col_scale, tm=512, chunk=512):
    t, d = x.shape
    n = w.shape[1]
    tm = min(tm, t)
    return pl.pallas_call(
        functools.partial(_norm_matmul_kernel, chunk=min(chunk, n)),
        grid=(t // tm,),
        in_specs=[pl.BlockSpec((tm, d), lambda i: (i, 0)),
                  pl.BlockSpec((1, d), lambda i: (0, 0)),
                  pl.BlockSpec((d, n), lambda i: (0, 0)),
                  pl.BlockSpec((1, n), lambda i: (0, 0))],
        out_specs=pl.BlockSpec((tm, n), lambda i: (i, 0)),
        out_shape=jax.ShapeDtypeStruct((t, n), BF16),
        compiler_params=_params("parallel"),
        name="norm_matmul",
    )(x, g.reshape(1, d), w, col_scale.reshape(1, n))


def _query_col_scale(n, mix_q_cols):
    cols = np.arange(n)
    return jnp.asarray(np.where((cols < mix_q_cols) | (cols >= n - MEM_WIDTH), QK_SCALE, 1.0), F32)


def _window_kernel(sink_ref, q_ref, kp_ref, kc_ref, kn_ref, vp_ref, vc_ref, vn_ref, o_ref):
    n = pl.program_id(1)
    nb = pl.num_programs(1)
    blk = q_ref.shape[0]
    k = jnp.concatenate([kp_ref[...], kc_ref[...], kn_ref[...]], axis=0)
    v = jnp.concatenate([vp_ref[...], vc_ref[...], vn_ref[...]], axis=0)
    qi = lax.broadcasted_iota(jnp.int32, (blk, 3 * blk), 0)
    kj = lax.broadcasted_iota(jnp.int32, (blk, 3 * blk), 1)
    dist_i = jnp.abs(blk + qi - kj)
    valid = dist_i <= WINDOW
    valid = valid & jnp.logical_or(n > 0, kj >= blk)
    valid = valid & jnp.logical_or(n < nb - 1, kj < 2 * blk)
    dist = dist_i.astype(F32)
    q = q_ref[...]
    slopes = _alibi_slopes(A_Q_HEADS)
    outs = []
    for h in range(A_Q_HEADS):
        kh = h // A_GROUP
        qh = q[:, h * HEAD_DIM:(h + 1) * HEAD_DIM]
        kk = k[:, kh * HEAD_DIM:(kh + 1) * HEAD_DIM]
        vv = v[:, kh * HEAD_DIM:(kh + 1) * HEAD_DIM]
        s = _dot_nt(qh, kk) - (slopes[h] * LOG2E) * dist
        s = jnp.where(valid, s, NEG_INF)
        sink = sink_ref[h] * LOG2E
        m = jnp.maximum(jnp.max(s, axis=-1, keepdims=True), sink)
        p = jnp.exp2(s - m)
        denom = jnp.sum(p, axis=-1, keepdims=True) + jnp.exp2(sink - m)
        o = _dot(p.astype(BF16), vv)
        outs.append(o * (1.0 / denom))
    o_ref[...] = jnp.concatenate(outs, axis=-1).astype(o_ref.dtype)


def _window_attention(proj, sink, batch, seq):
    blk = WINDOW
    nb = seq // blk
    kcol = MIX_WIDTH // 256
    vcol = kcol + 1

    def cur(b, n, s):
        return b * nb + n

    def prev(b, n, s):
        return b * nb + jnp.maximum(n - 1, 0)

    def nxt(b, n, s):
        return b * nb + jnp.minimum(n + 1, nb - 1)

    def spec(row, col):
        return pl.BlockSpec((blk, 256), lambda b, n, s: (row(b, n, s), col))

    return pl.pallas_call(
        _window_kernel,
        grid_spec=pltpu.PrefetchScalarGridSpec(
            num_scalar_prefetch=1,
            grid=(batch, nb),
            in_specs=[pl.BlockSpec((blk, MIX_WIDTH), lambda b, n, s: (b * nb + n, 0)),
                      spec(prev, kcol), spec(cur, kcol), spec(nxt, kcol),
                      spec(prev, vcol), spec(cur, vcol), spec(nxt, vcol)],
            out_specs=pl.BlockSpec((blk, MIX_WIDTH), lambda b, n, s: (b * nb + n, 0)),
        ),
        out_shape=jax.ShapeDtypeStruct((batch * seq, MIX_WIDTH), BF16),
        compiler_params=_params("parallel", "parallel"),
        name="window_attention",
    )(sink, proj, proj, proj, proj, proj, proj, proj)


def _mem_attn_kernel(q_ref, mkv_ref, o_ref):
    q = q_ref[...]
    mkv = mkv_ref[...]
    outs = []
    for h in range(MEM_HEADS):
        sl = slice(h * HEAD_DIM, (h + 1) * HEAD_DIM)
        s = _dot_nt(q[:, sl], mkv[:, sl])
        m = jnp.max(s, axis=-1, keepdims=True)
        p = jnp.exp2(s - m)
        denom = jnp.sum(p, axis=-1, keepdims=True)
        o = _dot(p.astype(BF16), mkv[:, MEM_WIDTH + h * HEAD_DIM:MEM_WIDTH + (h + 1) * HEAD_DIM])
        outs.append(o * (1.0 / denom))
    o_ref[...] = jnp.concatenate(outs, axis=-1).astype(o_ref.dtype)


def _mem_attention(proj, qcol, mkv, batch, seq, tm=512):
    tm = min(tm, seq)
    per_batch = seq // tm
    n_mem = mkv.shape[0] // batch
    return pl.pallas_call(
        _mem_attn_kernel,
        grid=(batch * per_batch,),
        in_specs=[pl.BlockSpec((tm, MEM_WIDTH), lambda i: (i, qcol)),
                  pl.BlockSpec((n_mem, 2 * MEM_WIDTH), lambda i: (i // per_batch, 0))],
        out_specs=pl.BlockSpec((tm, MEM_WIDTH), lambda i: (i, 0)),
        out_shape=jax.ShapeDtypeStruct((batch * seq, MEM_WIDTH), BF16),
        compiler_params=_params("parallel"),
        name="mem_attention",
    )(proj, mkv)


def _out_proj_kernel(mix_ref, mem_ref, w1_ref, w2_ref, x_ref, g_ref, o_ref):
    o = _dot(mix_ref[...], w1_ref[...]) + _dot(mem_ref[...], w2_ref[...])
    o_ref[...] = x_ref[...] + _rms(o, g_ref[...])


def _out_proj(mix, mem_out, w_out, x, g, tm=512):
    t, d = x.shape
    tm = min(tm, t)
    w1 = w_out[:MIX_WIDTH]
    w2 = w_out[MIX_WIDTH:]
    return pl.pallas_call(
        _out_proj_kernel,
        grid=(t // tm,),
        in_specs=[pl.BlockSpec((tm, MIX_WIDTH), lambda i: (i, 0)),
                  pl.BlockSpec((tm, MEM_WIDTH), lambda i: (i, 0)),
                  pl.BlockSpec((MIX_WIDTH, d), lambda i: (0, 0)),
                  pl.BlockSpec((MEM_WIDTH, d), lambda i: (0, 0)),
                  pl.BlockSpec((tm, d), lambda i: (i, 0)),
                  pl.BlockSpec((1, d), lambda i: (0, 0))],
        out_specs=pl.BlockSpec((tm, d), lambda i: (i, 0)),
        out_shape=jax.ShapeDtypeStruct((t, d), F32),
        compiler_params=_params("parallel"),
        name="out_proj",
    )(mix, mem_out, w1, w2, x, g.reshape(1, d))


def _swiglu_tile(h, wg_ref, wu_ref, wd_ref, chunk=256):
    out = None
    for c in range(wg_ref.shape[1] // chunk):
        sl = slice(c * chunk, (c + 1) * chunk)
        a = _dot(h, wg_ref[:, sl])
        u = _dot(h, wu_ref[:, sl])
        z = (a * jax.nn.sigmoid(a) * u).astype(BF16)
        y = _dot(z, wd_ref[sl, :])
        out = y if out is None else out + y
    return out


def _ffn_kernel(x_ref, gin_ref, gout_ref, wg_ref, wu_ref, wd_ref, o_ref, h_sc, acc_sc):
    f = pl.program_id(1)

    @pl.when(f == 0)
    def _():
        h_sc[...] = _rms(x_ref[...], gin_ref[...]).astype(BF16)
        acc_sc[...] = jnp.zeros_like(acc_sc)

    acc_sc[...] += _swiglu_tile(h_sc[...], wg_ref, wu_ref, wd_ref)

    @pl.when(f == pl.num_programs(1) - 1)
    def _():
        o_ref[...] = x_ref[...] + _rms(acc_sc[...], gout_ref[...])


def _ffn(x, g_in, g_out, wg, wu, wd, tm=512, tf=1792):
    t, d = x.shape
    ff = wg.shape[1]
    tm = min(tm, t)
    return pl.pallas_call(
        _ffn_kernel,
        grid=(t // tm, ff // tf),
        in_specs=[pl.BlockSpec((tm, d), lambda i, f: (i, 0)),
                  pl.BlockSpec((1, d), lambda i, f: (0, 0)),
                  pl.BlockSpec((1, d), lambda i, f: (0, 0)),
                  pl.BlockSpec((d, tf), lambda i, f: (0, f)),
                  pl.BlockSpec((d, tf), lambda i, f: (0, f)),
                  pl.BlockSpec((tf, d), lambda i, f: (f, 0))],
        out_specs=pl.BlockSpec((tm, d), lambda i, f: (i, 0)),
        out_shape=jax.ShapeDtypeStruct((t, d), F32),
        scratch_shapes=[pltpu.VMEM((tm, d), BF16), pltpu.VMEM((tm, d), F32)],
        compiler_params=_params("parallel", "arbitrary"),
        name="ffn",
    )(x, g_in.reshape(1, d), g_out.reshape(1, d), wg, wu, wd)


_POS_SPLIT = 256
_LEFT, _DIAG, _RIGHT = 0, 1, 2
_BOUND_LANE = 12
_CHUNK_LANE = 15
_ZERO_EXP2_ARG = 152.0
_MIN_DENOM = 2.0 ** -60


def _bf16_split(x):
    hi = x.astype(BF16)
    r = x - hi.astype(F32)
    mid = r.astype(BF16)
    lo = (r - mid.astype(F32)).astype(BF16)
    return hi, mid, lo


def _diff_attn_kernel(reach_ref, slope_ref, q_ref, k_ref, v_ref, qf_ref, kf_ref, dist_ref, lam_ref,
                      g_ref, o_ref, qa_sc, ka_sc, va_sc, k2_sc, m_sc, acc_sc, sa_sc, sb_sc, *, tk,
                      out_scale, lambda_init):
    h = pl.program_id(1)
    i = pl.program_id(2)
    tq = q_ref.shape[0]
    seq = k_ref.shape[0]
    nk = seq // tk
    slope = slope_ref[h]
    reach = reach_ref[h]

    @pl.when(i == 0)
    def _():
        k = k_ref[...]
        ka_sc[:, :B_VDIM] = k
        va_sc[:, :B_VDIM] = v_ref[...]
        lane = lax.broadcasted_iota(jnp.int32, (seq, LANES), 1)
        va_sc[:, B_VDIM:] = jnp.where(lane == 0, 1.0, 0.0).astype(BF16)
        kf = kf_ref[...]
        flane = lax.broadcasted_iota(jnp.int32, kf.shape, 1)
        chunk_lanes = jnp.logical_and(flane >= _CHUNK_LANE, flane < _CHUNK_LANE + 3)
        for c in range(nk):
            ka_sc[c * tk:(c + 1) * tk, B_VDIM:] = jnp.where(chunk_lanes, float(c), kf).astype(BF16)
        kk = k.astype(F32)
        kk = kk * kk
        for c, sel in enumerate((lane < HEAD_DIM, lane >= HEAD_DIM)):
            n2 = jnp.sum(jnp.where(sel, kk, 0.0), axis=-1, keepdims=True)
            k2_sc[c] = jnp.broadcast_to(jnp.max(n2, axis=0, keepdims=True), k2_sc.shape[1:])

    q = q_ref[...]
    lane = lax.broadcasted_iota(jnp.int32, q.shape, 1)
    zero = jnp.zeros_like(q)
    q0 = jnp.where(lane < HEAD_DIM, q, zero)
    q1 = jnp.where(lane >= HEAD_DIM, q, zero)
    qq = q.astype(F32)
    qq = qq * qq
    n0 = jnp.sum(jnp.where(lane < HEAD_DIM, qq, 0.0), axis=-1, keepdims=True)
    n1 = jnp.sum(jnp.where(lane >= HEAD_DIM, qq, 0.0), axis=-1, keepdims=True)
    bound = jnp.concatenate([jnp.sqrt(n0) * jnp.sqrt(k2_sc[0][0:1, 0:1]),
                             jnp.sqrt(n1) * jnp.sqrt(k2_sc[1][0:1, 0:1])], axis=0)
    off = slope * (i * tk).astype(F32)
    qf = qf_ref[...]
    lane2 = lax.broadcasted_iota(jnp.int32, (2 * tq, LANES), 1)

    def build_queries(with_bound):
        for variant, feat, row_term in ((_LEFT, -qf, bound + off), (_DIAG, zero, bound),
                                        (_RIGHT, qf, bound - off)):
            feat = jnp.concatenate([feat, feat], axis=0)
            if with_bound:
                for n, piece in enumerate(_bf16_split(row_term)):
                    feat = jnp.where(lane2 == _BOUND_LANE + n, piece, feat)
            qa_sc[variant, :, B_VDIM:] = feat
            qa_sc[variant, :tq, :B_VDIM] = q0
            qa_sc[variant, tq:, :B_VDIM] = q1

    def scores(j, s_ref):
        variant = jnp.where(j < i, _LEFT, jnp.where(j == i, _DIAG, _RIGHT))
        start = pl.multiple_of(j * tk, tk)
        s_ref[...] = _dot_nt(qa_sc[variant], ka_sc[pl.ds(start, tk), :])

    def diag_bias(j, s_ref):
        @pl.when(j == i)
        def _():
            bias = slope * dist_ref[...]
            s_ref[:tq] -= bias
            s_ref[tq:] -= bias

    def exp_pv(j, s_ref):
        start = pl.multiple_of(j * tk, tk)
        p = jnp.exp2(s_ref[...]).astype(BF16)
        acc_sc[...] += _dot(p, va_sc[pl.ds(start, tk), :])

    build_queries(True)
    acc_sc[...] = jnp.zeros_like(acc_sc)
    first_pair = jnp.maximum(i - reach, 0) // 2
    last_pair = (jnp.minimum(i + reach + 1, nk) + 1) // 2
    scores(2 * first_pair, sa_sc)
    diag_bias(2 * first_pair, sa_sc)

    def pair(jj, carry):
        j = 2 * jj
        scores(j + 1, sb_sc)
        exp_pv(j, sa_sc)
        diag_bias(j + 1, sb_sc)
        nxt = jnp.minimum(j + 2, nk - 1)
        scores(nxt, sa_sc)
        exp_pv(j + 1, sb_sc)
        diag_bias(nxt, sa_sc)
        return carry

    lax.fori_loop(first_pair, last_pair, pair, 0)

    denom_min = jnp.min(acc_sc[:, B_VDIM:B_VDIM + 1])

    @pl.when(jnp.logical_not(denom_min >= _MIN_DENOM))
    def _():
        build_queries(False)
        m_sc[...] = jnp.full_like(m_sc, NEG_INF)
        acc_sc[...] = jnp.zeros_like(acc_sc)

        def chunk(j, carry):
            scores(j, sa_sc)
            diag_bias(j, sa_sc)
            adj = jnp.where(j > i, off, jnp.where(j < i, -off, 0.0))
            start = pl.multiple_of(j * tk, tk)
            s = sa_sc[...]
            m_prev = m_sc[...]
            m_new = jnp.maximum(m_prev, jnp.max(s, axis=-1, keepdims=True) + adj)
            p = jnp.exp2(s - (m_new - adj)).astype(BF16)
            acc_sc[...] = jnp.exp2(m_prev - m_new) * acc_sc[...] + _dot(p, va_sc[pl.ds(start, tk), :])
            m_sc[...] = m_new
            return carry

        lax.fori_loop(0, nk, chunk, 0)

    lp = lam_ref[...]
    lam = (jnp.exp(jnp.sum(lp[0:1] * lp[1:2], axis=-1, keepdims=True))
           - jnp.exp(jnp.sum(lp[2:3] * lp[3:4], axis=-1, keepdims=True)) + lambda_init)
    acc = acc_sc[...]
    o0 = acc[:tq, :B_VDIM] * (1.0 / acc[:tq, B_VDIM:B_VDIM + 1])
    o1 = acc[tq:, :B_VDIM] * (1.0 / acc[tq:, B_VDIM:B_VDIM + 1])
    o_ref[...] = (_rms(o0 - lam * o1, g_ref[...]) * out_scale).astype(o_ref.dtype)


def _bf16_pieces(x, n=3):
    out = []
    r = np.float64(x)
    for _ in range(n):
        p = np.float64(np.float32(r).astype(jnp.bfloat16).astype(np.float32))
        out.append(p)
        r = r - p
    return out


def _alibi_features(slopes2, t):
    pos = np.arange(t)
    hi = (pos // _POS_SPLIT) * _POS_SPLIT
    lo = pos % _POS_SPLIT
    qf = np.zeros((len(slopes2), t, LANES), np.float32)
    kf = np.zeros((len(slopes2), t, LANES), np.float32)
    for h, s in enumerate(slopes2):
        for n, piece in enumerate(_bf16_pieces(s)):
            for base, part in ((0, hi), (3, lo)):
                qf[h, :, base + n] = -piece
                kf[h, :, base + n] = part
                qf[h, :, 6 + base + n] = part
                kf[h, :, 6 + base + n] = piece
        for n, piece in enumerate(_bf16_pieces(s * t)):
            qf[h, :, _CHUNK_LANE + n] = -piece
        kf[h, :, _BOUND_LANE:_BOUND_LANE + 3] = -1.0
    return jnp.asarray(qf, BF16), jnp.asarray(kf, BF16)


def _diff_attention(proj, b_lambda, subln_g, lambda_init, batch, seq, t=512):
    t = min(t, seq // 2)
    nq = seq // t
    assert seq % (2 * t) == 0
    kcol = MIX_WIDTH // B_VDIM
    vcol = 2 * kcol
    slopes2 = [s * LOG2E for s in _alibi_slopes(B_HEADS)]
    reach = [min(nq, int(math.floor((_ZERO_EXP2_ARG / s - 1.0) / t)) + 1) for s in slopes2]
    qf, kf = _alibi_features(slopes2, t)
    pos = np.arange(t)
    dist = jnp.asarray(np.abs(pos[:, None] - pos[None, :]), F32)
    kernel = functools.partial(_diff_attn_kernel, tk=t, out_scale=1.0 - lambda_init,
                               lambda_init=lambda_init)
    return pl.pallas_call(
        kernel,
        grid_spec=pltpu.PrefetchScalarGridSpec(
            num_scalar_prefetch=2,
            grid=(batch, B_HEADS, nq),
            in_specs=[pl.BlockSpec((t, B_VDIM), lambda b, h, i, r, s: (b * nq + i, h)),
                      pl.BlockSpec((seq, B_VDIM), lambda b, h, i, r, s: (b, kcol + h)),
                      pl.BlockSpec((seq, B_VDIM), lambda b, h, i, r, s: (b, vcol + h)),
                      pl.BlockSpec((None, t, LANES), lambda b, h, i, r, s: (h, 0, 0)),
                      pl.BlockSpec((None, t, LANES), lambda b, h, i, r, s: (h, 0, 0)),
                      pl.BlockSpec((t, t), lambda b, h, i, r, s: (0, 0)),
                      pl.BlockSpec((4, HEAD_DIM), lambda b, h, i, r, s: (0, 0)),
                      pl.BlockSpec((1, B_VDIM), lambda b, h, i, r, s: (0, 0))],
            out_specs=pl.BlockSpec((t, B_VDIM), lambda b, h, i, r, s: (b * nq + i, h)),
            scratch_shapes=[pltpu.VMEM((3, 2 * t, 2 * LANES), BF16),
                            pltpu.VMEM((seq, 2 * LANES), BF16),
                            pltpu.VMEM((seq, 2 * LANES), BF16),
                            pltpu.VMEM((2, 8, LANES), F32),
                            pltpu.VMEM((2 * t, 1), F32),
                            pltpu.VMEM((2 * t, 2 * LANES), F32),
                            pltpu.VMEM((2 * t, t), F32),
                            pltpu.VMEM((2 * t, t), F32)],
        ),
        out_shape=jax.ShapeDtypeStruct((batch * seq, MIX_WIDTH), BF16),
        compiler_params=_params("arbitrary", "arbitrary", "arbitrary"),
        name="diff_attention",
    )(jnp.asarray(reach, jnp.int32), jnp.asarray(slopes2, F32), proj, proj, proj, qf, kf, dist,
      b_lambda, subln_g.reshape(1, B_VDIM))


def _router_kernel(x_ref, g_ref, whi_ref, wlo_ref, h_ref, e_ref, gate_ref):
    h = _rms(x_ref[...], g_ref[...])
    h_hi = h.astype(BF16)
    h_lo = (h - h_hi.astype(F32)).astype(BF16)
    whi = whi_ref[...]
    logits = _dot(h_hi, whi) + (_dot(h_hi, wlo_ref[...]) + _dot(h_lo, whi))
    lane = lax.broadcasted_iota(jnp.int32, logits.shape, 1)
    logits = jnp.where(lane < N_EXPERTS, logits, -jnp.inf)
    m1 = jnp.max(logits, axis=-1, keepdims=True)
    i1 = jnp.min(jnp.where(logits == m1, lane, LANES), axis=-1, keepdims=True)
    rest = jnp.where(lane == i1, -jnp.inf, logits)
    m2 = jnp.max(rest, axis=-1, keepdims=True)
    i2 = jnp.min(jnp.where(rest == m2, lane, LANES), axis=-1, keepdims=True)
    r = jnp.exp(m2 - m1)
    g1 = 1.0 / (1.0 + r)
    g2 = r * g1
    e_ref[...] = jnp.where(lane == 0, i1, jnp.where(lane == 1, i2, 0))
    gate_ref[...] = jnp.where(lane == 0, g1, jnp.where(lane == 1, g2, 0.0))
    h_ref[...] = h_hi


def _router(x, g, w_router, tm=512):
    t, d = x.shape
    tm = min(tm, t)
    wpad = jnp.zeros((d, LANES), F32).at[:, :N_EXPERTS].set(w_router)
    whi = wpad.astype(BF16)
    wlo = (wpad - whi.astype(F32)).astype(BF16)
    return pl.pallas_call(
        _router_kernel,
        grid=(t // tm,),
        in_specs=[pl.BlockSpec((tm, d), lambda i: (i, 0)),
                  pl.BlockSpec((1, d), lambda i: (0, 0)),
                  pl.BlockSpec((d, LANES), lambda i: (0, 0)),
                  pl.BlockSpec((d, LANES), lambda i: (0, 0))],
        out_specs=[pl.BlockSpec((tm, d), lambda i: (i, 0)),
                   pl.BlockSpec((tm, LANES), lambda i: (i, 0)),
                   pl.BlockSpec((tm, LANES), lambda i: (i, 0))],
        out_shape=[jax.ShapeDtypeStruct((t, d), BF16),
                   jax.ShapeDtypeStruct((t, LANES), jnp.int32),
                   jax.ShapeDtypeStruct((t, LANES), F32)],
        compiler_params=_params("parallel"),
        name="router",
    )(x, g.reshape(1, d), whi, wlo)


def _dispatch_kernel(slab_ref, fill_ref, h_ref, dest_ref, gate_ref, xs_hbm, gs_hbm,
                     xslab_sc, gslab_sc, zx_sc, zg_sc, sem, zsem):
    i = pl.program_id(0)
    n = pl.num_programs(0)
    slot = i % 2
    tc = h_ref.shape[0]
    tm = zx_sc.shape[0]

    def slab_dma(step, sl, op):
        for e in range(N_EXPERTS):
            start = pl.multiple_of(slab_ref[step * N_EXPERTS + e], SUBLANES)
            short = slab_ref[(n + step) * N_EXPERTS + e] <= tc // 2
            for size, cond in ((tc // 2, short), (tc, jnp.logical_not(short))):
                @pl.when(cond)
                def _():
                    for src, dst in ((xslab_sc, xs_hbm), (gslab_sc, gs_hbm)):
                        copy = pltpu.make_async_copy(src.at[sl, pl.ds(e * tc, size)],
                                                     dst.at[pl.ds(start, size)], sem.at[sl])
                        getattr(copy, op)()

    def fill_copies(start):
        start = pl.multiple_of(start, SUBLANES)
        return [pltpu.make_async_copy(zx_sc, xs_hbm.at[pl.ds(start, tm)], zsem),
                pltpu.make_async_copy(zg_sc, gs_hbm.at[pl.ds(start, tm)], zsem)]

    @pl.when(i == 0)
    def _():
        zx_sc[...] = jnp.zeros_like(zx_sc)
        zg_sc[...] = jnp.zeros_like(zg_sc)
        for half in range(2):
            for e in range(N_EXPERTS):
                for c in fill_copies(fill_ref[half * N_EXPERTS + e]):
                    c.start()
            for e in range(N_EXPERTS):
                for c in fill_copies(fill_ref[half * N_EXPERTS + e]):
                    c.wait()
        first_free = fill_ref[2 * N_EXPERTS]
        n_tiles = xs_hbm.shape[0] // tm

        def start_tile(b, carry):
            for c in fill_copies(b * tm):
                c.start()
            return carry

        def wait_tile(b, carry):
            for c in fill_copies(b * tm):
                c.wait()
            return carry

        lax.fori_loop(first_free, n_tiles, start_tile, 0)
        lax.fori_loop(first_free, n_tiles, wait_tile, 0)

    dest = dest_ref[0]
    pos = lax.broadcasted_iota(jnp.int32, (tc, tc), 0)
    sel0, sel1 = [], []
    for e in range(N_EXPERTS):
        start = slab_ref[i * N_EXPERTS + e]
        sel0.append(jnp.where(dest[0:1, :] - start == pos, 1.0, 0.0))
        sel1.append(jnp.where(dest[1:2, :] - start == pos, 1.0, 0.0))
    sel0 = jnp.concatenate(sel0, axis=0).astype(BF16)
    sel1 = jnp.concatenate(sel1, axis=0).astype(BF16)
    gate = gate_ref[...]
    lane = lax.broadcasted_iota(jnp.int32, gate.shape, 1)

    def gate_pieces(col):
        out = jnp.zeros(gate.shape, F32)
        for n_, piece in enumerate(_bf16_split(col)):
            out = jnp.where(lane == n_, piece.astype(F32), out)
        return out.astype(BF16)

    xslab_sc[slot] = _dot(sel0 + sel1, h_ref[...])
    gslab_sc[slot] = _dot(sel0, gate_pieces(gate[:, 0:1])) + _dot(sel1, gate_pieces(gate[:, 1:2]))

    @pl.when(i > 0)
    def _():
        slab_dma(i - 1, 1 - slot, "wait")

    slab_dma(i, slot, "start")

    @pl.when(i == n - 1)
    def _():
        slab_dma(i, slot, "wait")


def _dispatch(h, dest_t, gates, slab, fill, p_rows, tm, tc):
    t, d = h.shape
    nb = t // tc
    return pl.pallas_call(
        _dispatch_kernel,
        grid_spec=pltpu.PrefetchScalarGridSpec(
            num_scalar_prefetch=2,
            grid=(nb,),
            in_specs=[pl.BlockSpec((tc, d), lambda i, s, z: (i, 0)),
                      pl.BlockSpec((1, TOP_K, tc), lambda i, s, z: (i, 0, 0)),
                      pl.BlockSpec((tc, LANES), lambda i, s, z: (i, 0))],
            out_specs=[pl.BlockSpec(memory_space=pl.ANY), pl.BlockSpec(memory_space=pl.ANY)],
            scratch_shapes=[pltpu.VMEM((2, N_EXPERTS * tc, d), F32),
                            pltpu.VMEM((2, N_EXPERTS * tc, LANES), F32),
                            pltpu.VMEM((tm, d), F32), pltpu.VMEM((tm, LANES), F32),
                            pltpu.SemaphoreType.DMA((2,)), pltpu.SemaphoreType.DMA(())],
        ),
        out_shape=[jax.ShapeDtypeStruct((p_rows, d), F32), jax.ShapeDtypeStruct((p_rows, LANES), F32)],
        compiler_params=_params("arbitrary"),
        name="moe_dispatch",
    )(slab, fill, h, dest_t, gates)


def _moe_kernel(blk_e_ref, used_ref, xblk_ref, x_ref, gs_ref, wg_ref, wu_ref, wd_ref, o_ref,
                xb_sc, acc_sc):
    i = pl.program_id(0)
    f = pl.program_id(1)
    used = used_ref[i] > 0

    @pl.when(jnp.logical_and(used, f == 0))
    def _():
        xb_sc[...] = x_ref[...].astype(BF16)
        acc_sc[...] = jnp.zeros_like(acc_sc)

    @pl.when(used)
    def _():
        acc_sc[...] += _swiglu_tile(xb_sc[...], wg_ref, wu_ref, wd_ref)

    @pl.when(f == pl.num_programs(1) - 1)
    def _():
        gs = gs_ref[...]
        gate = gs[:, 0:1] + gs[:, 1:2] + gs[:, 2:3]
        o_ref[...] = jnp.where(used, acc_sc[...] * gate, 0.0)


def _moe_experts(xs, gs, blk_e, blk_used, xblk, wg, wu, wd, tm, tf=1792):
    p, d = xs.shape
    nblk = p // tm
    nf = wg.shape[2] // tf

    def fidx(i, f, used_ref):
        return jnp.where(used_ref[i] > 0, f, nf - 1)

    return pl.pallas_call(
        _moe_kernel,
        grid_spec=pltpu.PrefetchScalarGridSpec(
            num_scalar_prefetch=3,
            grid=(nblk, nf),
            in_specs=[pl.BlockSpec((tm, d), lambda i, f, be, us, xb: (xb[i], 0)),
                      pl.BlockSpec((tm, LANES), lambda i, f, be, us, xb: (xb[i], 0)),
                      pl.BlockSpec((None, d, tf), lambda i, f, be, us, xb: (be[i], 0, fidx(i, f, us))),
                      pl.BlockSpec((None, d, tf), lambda i, f, be, us, xb: (be[i], 0, fidx(i, f, us))),
                      pl.BlockSpec((None, tf, d), lambda i, f, be, us, xb: (be[i], fidx(i, f, us), 0))],
            out_specs=pl.BlockSpec((tm, d), lambda i, f, be, us, xb: (i, 0)),
            scratch_shapes=[pltpu.VMEM((tm, d), BF16), pltpu.VMEM((tm, d), F32)],
        ),
        out_shape=jax.ShapeDtypeStruct((p, d), F32),
        compiler_params=_params("arbitrary", "arbitrary"),
        name="moe_experts",
    )(blk_e, blk_used, xblk, xs, gs, wg, wu, wd)


def _combine_kernel(slab_ref, ys_hbm, dest_ref, x_ref, g_ref, o_ref, slab_sc, sem):
    i = pl.program_id(0)
    n = pl.num_programs(0)
    slot = i % 2
    tc = x_ref.shape[0]

    def slab_dma(step, sl, op):
        for e in range(N_EXPERTS):
            start = pl.multiple_of(slab_ref[step * N_EXPERTS + e], SUBLANES)
            short = slab_ref[(n + step) * N_EXPERTS + e] <= tc // 2
            for size, cond in ((tc // 2, short), (tc, jnp.logical_not(short))):
                @pl.when(cond)
                def _():
                    copy = pltpu.make_async_copy(ys_hbm.at[pl.ds(start, size)],
                                                 slab_sc.at[sl, pl.ds(e * tc, size)], sem.at[sl])
                    getattr(copy, op)()

    @pl.when(i == 0)
    def _():
        slab_sc[...] = jnp.zeros_like(slab_sc)
        slab_dma(0, 0, "start")

    @pl.when(i + 1 < n)
    def _():
        slab_dma(i + 1, 1 - slot, "start")

    dest = dest_ref[...]
    pos = lax.broadcasted_iota(jnp.int32, (tc, tc), 1)
    sel = []
    for e in range(N_EXPERTS):
        start = slab_ref[i * N_EXPERTS + e]
        hit = jnp.logical_or(dest[:, 0:1] - start == pos, dest[:, 1:2] - start == pos)
        sel.append(jnp.where(hit, 1.0, 0.0))
    sel = jnp.concatenate(sel, axis=1).astype(BF16)

    slab_dma(i, slot, "wait")
    y = _dot(sel, slab_sc[slot].astype(BF16))
    o_ref[...] = x_ref[...] + _rms(y, g_ref[...])


def _combine(ys, dest, slab, x, g, tc):
    t, d = x.shape
    return pl.pallas_call(
        _combine_kernel,
        grid_spec=pltpu.PrefetchScalarGridSpec(
            num_scalar_prefetch=1,
            grid=(t // tc,),
            in_specs=[pl.BlockSpec(memory_space=pl.ANY),
                      pl.BlockSpec((tc, LANES), lambda i, s: (i, 0)),
                      pl.BlockSpec((tc, d), lambda i, s: (i, 0)),
                      pl.BlockSpec((1, d), lambda i, s: (0, 0))],
            out_specs=pl.BlockSpec((tc, d), lambda i, s: (i, 0)),
            scratch_shapes=[pltpu.VMEM((2, N_EXPERTS * tc, d), F32), pltpu.SemaphoreType.DMA((2,))],
        ),
        out_shape=jax.ShapeDtypeStruct((t, d), F32),
        compiler_params=_params("arbitrary"),
        name="moe_combine",
    )(slab, ys, dest, x, g.reshape(1, d))


def _route(top_e, tm, tc):
    t = top_e.shape[0]
    a = t * TOP_K
    nb = t // tc
    e_flat = top_e.reshape(a)
    onehot = (e_flat[:, None] == jnp.arange(N_EXPERTS, dtype=jnp.int32)[None, :]).astype(jnp.int32)
    csum = jnp.cumsum(onehot, axis=0)
    rank = jnp.sum(onehot * (csum - 1), axis=1)
    upto = csum[TOP_K * tc - 1::TOP_K * tc]
    before = jnp.concatenate([jnp.zeros((1, N_EXPERTS), jnp.int32), upto[:nb - 1]], axis=0)
    rows = (upto - before + SUBLANES - 1) // SUBLANES * SUBLANES
    rows_before = jnp.cumsum(rows, axis=0) - rows
    total = jnp.sum(rows, axis=0)
    region = (total + 2 * tm - 1) // tm * tm
    rends = jnp.cumsum(region)
    rstarts = rends - region
    slab = (rstarts[None, :] + rows_before).astype(jnp.int32)
    shift = jnp.broadcast_to((slab - before)[:, None, :], (nb, TOP_K * tc, N_EXPERTS)).reshape(a, N_EXPERTS)
    dest = (rank + jnp.sum(onehot * shift, axis=1)).astype(jnp.int32).reshape(t, TOP_K)
    slab = jnp.concatenate([slab.reshape(nb * N_EXPERTS), rows.reshape(nb * N_EXPERTS)])
    row_end = rstarts + total
    fill = jnp.concatenate([row_end, rends - tm, rends[-1:] // tm]).astype(jnp.int32)
    nblk = -(-(a + nb * N_EXPERTS * (SUBLANES - 1)) // tm) + 2 * N_EXPERTS
    blk_start = jnp.arange(nblk, dtype=jnp.int32) * tm
    blk_e = jnp.minimum(jnp.sum((rends[None, :] <= blk_start[:, None]).astype(jnp.int32), axis=1),
                        N_EXPERTS - 1)
    blk_used = jnp.logical_and(blk_start < row_end[blk_e], blk_start < rends[-1]).astype(jnp.int32)
    first_used = jnp.argmax(blk_used).astype(jnp.int32)
    xblk = jnp.where(blk_used > 0, jnp.arange(nblk, dtype=jnp.int32), first_used)
    return dest, slab, fill, blk_e, blk_used, xblk, nblk * tm


def _moe(x, g_in, g_out, w_router, wg, wu, wd, tm=512, tc=256):
    t = x.shape[0]
    tc = min(tc, t)
    h, e_pad, gates = _router(x, g_in, w_router)
    dest, slab, fill, blk_e, blk_used, xblk, p_rows = _route(e_pad[:, :TOP_K], tm, tc)
    dest_t = dest.reshape(t // tc, tc, TOP_K).transpose(0, 2, 1)
    dest_pad = jnp.zeros((t, LANES), jnp.int32).at[:, :TOP_K].set(dest)
    xs, gs = _dispatch(h, dest_t, gates, slab, fill, p_rows, tm, tc)
    ys = _moe_experts(xs, gs, blk_e, blk_used, xblk, wg, wu, wd, tm)
    return _combine(ys, dest_pad, slab, x, g_out, tc)


def kernel(x, mem, norm_g, mem_norm_g, w_mem_kv, a_w_in, a_sink, a_w_out, b_w_in, b_lambda,
           b_subln_g, b_w_out, ffn_w_gate, ffn_w_up, ffn_w_down, moe_w_router, moe_w_gate,
           moe_w_up, moe_w_down):
    batch, seq, d = x.shape
    n_mem = mem.shape[1]
    bf = lambda w: w.astype(BF16)
    xt = x.reshape(batch * seq, d)
    mkv = _norm_matmul(mem.reshape(batch * n_mem, d), mem_norm_g, bf(w_mem_kv),
                       jnp.ones((w_mem_kv.shape[1],), F32))

    g = norm_g[0]
    proj = _norm_matmul(xt, g[0], bf(a_w_in[0]), _query_col_scale(a_w_in.shape[2], A_Q_HEADS * HEAD_DIM))
    mix = _window_attention(proj, a_sink[0], batch, seq)
    mem_out = _mem_attention(proj, (A_Q_HEADS + 2 * A_KV_HEADS) * HEAD_DIM // MEM_WIDTH, mkv, batch, seq)
    xt = _out_proj(mix, mem_out, bf(a_w_out[0]), xt, g[1])
    xt = _ffn(xt, g[2], g[3], bf(ffn_w_gate[0]), bf(ffn_w_up[0]), bf(ffn_w_down[0]))

    g = norm_g[1]
    lambda_init = 0.8 - 0.6 * math.exp(-0.3 * 1)
    proj = _norm_matmul(xt, g[0], bf(b_w_in[0]), _query_col_scale(b_w_in.shape[2], MIX_WIDTH))
    mix = _diff_attention(proj, b_lambda[0], b_subln_g[0], lambda_init, batch, seq)
    mem_out = _mem_attention(proj, 3 * MIX_WIDTH // MEM_WIDTH, mkv, batch, seq)
    xt = _out_proj(mix, mem_out, bf(b_w_out[0]), xt, g[1])
    xt = _moe(xt, g[2], g[3], moe_w_router[0], bf(moe_w_gate[0]), bf(moe_w_up[0]), bf(moe_w_down[0]))
    return xt.reshape(batch, seq, d)
```

```python
import functools
import math

import jax
import jax.numpy as jnp
import numpy as np
from jax import lax
from jax.experimental import pallas as pl
from jax.experimental.pallas import tpu as pltpu

D_MODEL = 1024
HEAD_DIM = 64
MIX_WIDTH = 768
MEM_WIDTH = 256
MEM_HEADS = 4
A_Q_HEADS = 12
A_KV_HEADS = 4
A_GROUP = 3
WINDOW = 128
B_HEADS = 6
B_VDIM = 128
D_FF = 3584
N_EXPERTS = 8
TOP_K = 2
EPS = 1e-6
NEG_INF = -1e30
LOG2E = math.log2(math.e)
QK_SCALE = HEAD_DIM ** -0.5 * LOG2E
LANES = 128
SUBLANES = 8
VMEM_LIMIT_BYTES = 48 * 1024 * 1024

BF16 = jnp.bfloat16
F32 = jnp.float32


def _params(*sem):
    return pltpu.CompilerParams(dimension_semantics=sem, vmem_limit_bytes=VMEM_LIMIT_BYTES)


def _rms(x, g):
    return x * lax.rsqrt(jnp.mean(x * x, axis=-1, keepdims=True) + EPS) * g


def _dot(a, b):
    return jnp.dot(a, b, preferred_element_type=F32)


def _dot_nt(a, b):
    return lax.dot_general(a, b, (((1,), (1,)), ((), ())), preferred_element_type=F32)


def _alibi_slopes(n):
    return [2.0 ** (-8.0 * (i + 1) / n) for i in range(n)]


def _norm_matmul_kernel(x_ref, g_ref, w_ref, cs_ref, o_ref, *, chunk):
    h = _rms(x_ref[...], g_ref[...]).astype(BF16)
    for c in range(o_ref.shape[1] // chunk):
        sl = slice(c * chunk, (c + 1) * chunk)
        o_ref[:, sl] = (_dot(h, w_ref[:, sl]) * cs_ref[:, sl]).astype(o_ref.dtype)


def _norm_matmul(x, g, w, col_scale, tm=512, chunk=512):
    t, d = x.shape
    n = w.shape[1]
    tm = min(tm, t)
    return pl.pallas_call(
        functools.partial(_norm_matmul_kernel, chunk=min(chunk, n)),
        grid=(t // tm,),
        in_specs=[pl.BlockSpec((tm, d), lambda i: (i, 0)),
                  pl.BlockSpec((1, d), lambda i: (0, 0)),
                  pl.BlockSpec((d, n), lambda i: (0, 0)),
                  pl.BlockSpec((1, n), lambda i: (0, 0))],
        out_specs=pl.BlockSpec((tm, n), lambda i: (i, 0)),
        out_shape=jax.ShapeDtypeStruct((t, n), BF16),
        compiler_params=_params("parallel"),
        name="norm_matmul",
    )(x, g.reshape(1, d), w, col_scale.reshape(1, n))


def _query_col_scale(n, mix_q_cols):
    cols = np.arange(n)
    return jnp.asarray(np.where((cols < mix_q_cols) | (cols >= n - MEM_WIDTH), QK_SCALE, 1.0), F32)


def _window_kernel(sink_ref, q_ref, kp_ref, kc_ref, kn_ref, vp_ref, vc_ref, vn_ref, bias_ref, o_ref):
    n = pl.program_id(1)
    nb = pl.num_programs(1)
    blk = q_ref.shape[0]
    k = jnp.concatenate([kp_ref[...], kc_ref[...], kn_ref[...]], axis=0)
    v = jnp.concatenate([vp_ref[...], vc_ref[...], vn_ref[...]], axis=0)
    col = lax.broadcasted_iota(jnp.int32, (1, 3 * blk), 1)
    outside = jnp.logical_or(jnp.logical_and(n == 0, col < blk),
                             jnp.logical_and(n == nb - 1, col >= 2 * blk))
    edge = jnp.where(outside, NEG_INF, 0.0)
    q = q_ref[...]
    outs = []
    for kh in range(A_KV_HEADS):
        heads = range(kh * A_GROUP, (kh + 1) * A_GROUP)
        qs = jnp.concatenate([q[:, h * HEAD_DIM:(h + 1) * HEAD_DIM] for h in heads], axis=0)
        sink = jnp.concatenate([jnp.full((blk, 1), sink_ref[h] * LOG2E, F32) for h in heads], axis=0)
        s = _dot_nt(qs, k[:, kh * HEAD_DIM:(kh + 1) * HEAD_DIM]) + bias_ref[kh] + edge
        m = jnp.maximum(jnp.max(s, axis=-1, keepdims=True), sink)
        p = jnp.exp2(s - m)
        denom = jnp.sum(p, axis=-1, keepdims=True) + jnp.exp2(sink - m)
        o = _dot(p.astype(BF16), v[:, kh * HEAD_DIM:(kh + 1) * HEAD_DIM]) * (1.0 / denom)
        outs.extend(o[g * blk:(g + 1) * blk] for g in range(A_GROUP))
    o_ref[...] = jnp.concatenate(outs, axis=-1).astype(o_ref.dtype)


def _window_bias(blk):
    qi = np.arange(blk)[:, None]
    kj = np.arange(3 * blk)[None, :]
    dist = np.abs(blk + qi - kj)
    slopes = _alibi_slopes(A_Q_HEADS)
    tables = [np.where(dist <= WINDOW, -(slopes[h] * LOG2E) * dist, NEG_INF) for h in range(A_Q_HEADS)]
    return jnp.asarray(np.stack(tables).reshape(A_KV_HEADS, A_GROUP * blk, 3 * blk), F32)


def _window_attention(proj, sink, batch, seq):
    blk = WINDOW
    nb = seq // blk
    kcol = MIX_WIDTH // 256
    vcol = kcol + 1
    bias = _window_bias(blk)

    def cur(b, n, s):
        return b * nb + n

    def prev(b, n, s):
        return b * nb + jnp.maximum(n - 1, 0)

    def nxt(b, n, s):
        return b * nb + jnp.minimum(n + 1, nb - 1)

    def spec(row, col):
        return pl.BlockSpec((blk, 256), lambda b, n, s: (row(b, n, s), col))

    return pl.pallas_call(
        _window_kernel,
        grid_spec=pltpu.PrefetchScalarGridSpec(
            num_scalar_prefetch=1,
            grid=(batch, nb),
            in_specs=[pl.BlockSpec((blk, MIX_WIDTH), lambda b, n, s: (b * nb + n, 0)),
                      spec(prev, kcol), spec(cur, kcol), spec(nxt, kcol),
                      spec(prev, vcol), spec(cur, vcol), spec(nxt, vcol),
                      pl.BlockSpec(bias.shape, lambda b, n, s: (0, 0, 0))],
            out_specs=pl.BlockSpec((blk, MIX_WIDTH), lambda b, n, s: (b * nb + n, 0)),
        ),
        out_shape=jax.ShapeDtypeStruct((batch * seq, MIX_WIDTH), BF16),
        compiler_params=_params("parallel", "parallel"),
        name="window_attention",
    )(sink, proj, proj, proj, proj, proj, proj, proj, bias)


def _mem_attn_kernel(q_ref, mkv_ref, o_ref):
    q = q_ref[...]
    mkv = mkv_ref[...]
    outs = []
    for h in range(MEM_HEADS):
        sl = slice(h * HEAD_DIM, (h + 1) * HEAD_DIM)
        s = _dot_nt(q[:, sl], mkv[:, sl])
        m = jnp.max(s, axis=-1, keepdims=True)
        p = jnp.exp2(s - m)
        denom = jnp.sum(p, axis=-1, keepdims=True)
        o = _dot(p.astype(BF16), mkv[:, MEM_WIDTH + h * HEAD_DIM:MEM_WIDTH + (h + 1) * HEAD_DIM])
        outs.append(o * (1.0 / denom))
    o_ref[...] = jnp.concatenate(outs, axis=-1).astype(o_ref.dtype)


def _mem_attention(proj, qcol, mkv, batch, seq, tm=512):
    tm = min(tm, seq)
    per_batch = seq // tm
    n_mem = mkv.shape[0] // batch
    return pl.pallas_call(
        _mem_attn_kernel,
        grid=(batch * per_batch,),
        in_specs=[pl.BlockSpec((tm, MEM_WIDTH), lambda i: (i, qcol)),
                  pl.BlockSpec((n_mem, 2 * MEM_WIDTH), lambda i: (i // per_batch, 0))],
        out_specs=pl.BlockSpec((tm, MEM_WIDTH), lambda i: (i, 0)),
        out_shape=jax.ShapeDtypeStruct((batch * seq, MEM_WIDTH), BF16),
        compiler_params=_params("parallel"),
        name="mem_attention",
    )(proj, mkv)


def _out_proj_kernel(mix_ref, mem_ref, w1_ref, w2_ref, x_ref, g_ref, o_ref):
    o = _dot(mix_ref[...], w1_ref[...]) + _dot(mem_ref[...], w2_ref[...])
    o_ref[...] = x_ref[...] + _rms(o, g_ref[...])


def _out_proj(mix, mem_out, w_out, x, g, tm=512):
    t, d = x.shape
    tm = min(tm, t)
    w1 = w_out[:MIX_WIDTH]
    w2 = w_out[MIX_WIDTH:]
    return pl.pallas_call(
        _out_proj_kernel,
        grid=(t // tm,),
        in_specs=[pl.BlockSpec((tm, MIX_WIDTH), lambda i: (i, 0)),
                  pl.BlockSpec((tm, MEM_WIDTH), lambda i: (i, 0)),
                  pl.BlockSpec((MIX_WIDTH, d), lambda i: (0, 0)),
                  pl.BlockSpec((MEM_WIDTH, d), lambda i: (0, 0)),
                  pl.BlockSpec((tm, d), lambda i: (i, 0)),
                  pl.BlockSpec((1, d), lambda i: (0, 0))],
        out_specs=pl.BlockSpec((tm, d), lambda i: (i, 0)),
        out_shape=jax.ShapeDtypeStruct((t, d), F32),
        compiler_params=_params("parallel"),
        name="out_proj",
    )(mix, mem_out, w1, w2, x, g.reshape(1, d))


def _swiglu_tile(h, wg_ref, wu_ref, wd_ref, chunk=256):
    out = None
    for c in range(wg_ref.shape[1] // chunk):
        sl = slice(c * chunk, (c + 1) * chunk)
        a = _dot(h, wg_ref[:, sl])
        u = _dot(h, wu_ref[:, sl])
        z = (a * jax.nn.sigmoid(a) * u).astype(BF16)
        y = _dot(z, wd_ref[sl, :])
        out = y if out is None else out + y
    return out


def _ffn_kernel(x_ref, gin_ref, gout_ref, wg_ref, wu_ref, wd_ref, o_ref, h_sc, acc_sc):
    f = pl.program_id(1)

    @pl.when(f == 0)
    def _():
        h_sc[...] = _rms(x_ref[...], gin_ref[...]).astype(BF16)
        acc_sc[...] = jnp.zeros_like(acc_sc)

    acc_sc[...] += _swiglu_tile(h_sc[...], wg_ref, wu_ref, wd_ref)

    @pl.when(f == pl.num_programs(1) - 1)
    def _():
        o_ref[...] = x_ref[...] + _rms(acc_sc[...], gout_ref[...])


def _ffn(x, g_in, g_out, wg, wu, wd, tm=512, tf=1792):
    t, d = x.shape
    ff = wg.shape[1]
    tm = min(tm, t)
    return pl.pallas_call(
        _ffn_kernel,
        grid=(t // tm, ff // tf),
        in_specs=[pl.BlockSpec((tm, d), lambda i, f: (i, 0)),
                  pl.BlockSpec((1, d), lambda i, f: (0, 0)),
                  pl.BlockSpec((1, d), lambda i, f: (0, 0)),
                  pl.BlockSpec((d, tf), lambda i, f: (0, f)),
                  pl.BlockSpec((d, tf), lambda i, f: (0, f)),
                  pl.BlockSpec((tf, d), lambda i, f: (f, 0))],
        out_specs=pl.BlockSpec((tm, d), lambda i, f: (i, 0)),
        out_shape=jax.ShapeDtypeStruct((t, d), F32),
        scratch_shapes=[pltpu.VMEM((tm, d), BF16), pltpu.VMEM((tm, d), F32)],
        compiler_params=_params("parallel", "arbitrary"),
        name="ffn",
    )(x, g_in.reshape(1, d), g_out.reshape(1, d), wg, wu, wd)


_POS_SPLIT = 256
_LEFT, _DIAG, _RIGHT = 0, 1, 2
_BOUND_LANE = 12
_CHUNK_LANE = 15
_ZERO_EXP2_ARG = 152.0
_MIN_DENOM = 2.0 ** -60


def _bf16_split(x):
    hi = x.astype(BF16)
    r = x - hi.astype(F32)
    mid = r.astype(BF16)
    lo = (r - mid.astype(F32)).astype(BF16)
    return hi, mid, lo


def _diff_attn_kernel(reach_ref, slope_ref, q_ref, k_ref, v_ref, qf_ref, kf_ref, dist_ref, lam_ref,
                      g_ref, o_ref, qa_sc, ka_sc, va_sc, k2_sc, m_sc, acc_sc, sa_sc, pa_sc, pb_sc, *, tk,
                      out_scale, lambda_init):
    h = pl.program_id(1)
    i = pl.program_id(2)
    tq = q_ref.shape[0]
    seq = k_ref.shape[0]
    nk = seq // tk
    slope = slope_ref[h]
    reach = reach_ref[h]

    @pl.when(i == 0)
    def _():
        k = k_ref[...]
        ka_sc[:, :B_VDIM] = k
        va_sc[:, :B_VDIM] = v_ref[...]
        lane = lax.broadcasted_iota(jnp.int32, (seq, LANES), 1)
        va_sc[:, B_VDIM:] = jnp.where(lane == 0, 1.0, 0.0).astype(BF16)
        kf = kf_ref[...]
        flane = lax.broadcasted_iota(jnp.int32, kf.shape, 1)
        chunk_lanes = jnp.logical_and(flane >= _CHUNK_LANE, flane < _CHUNK_LANE + 3)
        for c in range(nk):
            ka_sc[c * tk:(c + 1) * tk, B_VDIM:] = jnp.where(chunk_lanes, float(c), kf).astype(BF16)
        kk = k.astype(F32)
        kk = kk * kk
        for c, sel in enumerate((lane < HEAD_DIM, lane >= HEAD_DIM)):
            n2 = jnp.sum(jnp.where(sel, kk, 0.0), axis=-1, keepdims=True)
            k2_sc[c] = jnp.broadcast_to(jnp.max(n2, axis=0, keepdims=True), k2_sc.shape[1:])

    q = q_ref[...]
    lane = lax.broadcasted_iota(jnp.int32, q.shape, 1)
    zero = jnp.zeros_like(q)
    q0 = jnp.where(lane < HEAD_DIM, q, zero)
    q1 = jnp.where(lane >= HEAD_DIM, q, zero)
    qq = q.astype(F32)
    qq = qq * qq
    n0 = jnp.sum(jnp.where(lane < HEAD_DIM, qq, 0.0), axis=-1, keepdims=True)
    n1 = jnp.sum(jnp.where(lane >= HEAD_DIM, qq, 0.0), axis=-1, keepdims=True)
    bound = jnp.concatenate([jnp.sqrt(n0) * jnp.sqrt(k2_sc[0][0:1, 0:1]),
                             jnp.sqrt(n1) * jnp.sqrt(k2_sc[1][0:1, 0:1])], axis=0)
    off = slope * (i * tk).astype(F32)
    qf = qf_ref[...]
    lane2 = lax.broadcasted_iota(jnp.int32, (2 * tq, LANES), 1)

    def build_queries(with_bound):
        for variant, feat, row_term in ((_LEFT, -qf, bound + off), (_DIAG, zero, bound),
                                        (_RIGHT, qf, bound - off)):
            feat = jnp.concatenate([feat, feat], axis=0)
            if with_bound:
                for n, piece in enumerate(_bf16_split(row_term)):
                    feat = jnp.where(lane2 == _BOUND_LANE + n, piece, feat)
            qa_sc[variant, :, B_VDIM:] = feat
            qa_sc[variant, :tq, :B_VDIM] = q0
            qa_sc[variant, tq:, :B_VDIM] = q1

    def scores(j, s_ref):
        variant = jnp.where(j < i, _LEFT, jnp.where(j == i, _DIAG, _RIGHT))
        start = pl.multiple_of(j * tk, tk)
        s_ref[...] = _dot_nt(qa_sc[variant], ka_sc[pl.ds(start, tk), :])

    def diag_bias(j, s_ref):
        @pl.when(j == i)
        def _():
            bias = slope * dist_ref[...]
            s_ref[:tq] -= bias
            s_ref[tq:] -= bias

    def probs(j, p_ref):
        start = pl.multiple_of(j * tk, tk)
        s = _dot_nt(qa_sc[jnp.where(j < i, _LEFT, _RIGHT)], ka_sc[pl.ds(start, tk), :])
        p_ref[...] = jnp.exp2(s).astype(BF16)

    def values(j):
        return va_sc[pl.ds(pl.multiple_of(j * tk, tk), tk), :]

    build_queries(True)
    lo = jnp.maximum(i - reach, 0)
    count = jnp.minimum(i + reach + 1, nk) - lo - 1

    def off_diag(n):
        j = lo + jnp.minimum(n, count - 1)
        return j + (j >= i).astype(jnp.int32)

    s_diag = _dot_nt(qa_sc[_DIAG], ka_sc[pl.ds(pl.multiple_of(i * tk, tk), tk), :])
    probs(off_diag(0), pa_sc)
    bias = slope * dist_ref[...]
    p_diag = jnp.exp2(s_diag - jnp.concatenate([bias, bias], axis=0)).astype(BF16)
    acc_sc[...] = _dot(p_diag, values(i))

    def pair(tt, carry):
        n = 2 * tt
        probs(off_diag(n + 1), pb_sc)
        acc_sc[...] += _dot(pa_sc[...], values(off_diag(n)))
        probs(off_diag(n + 2), pa_sc)
        acc_sc[...] += _dot(pb_sc[...], values(off_diag(n + 1)))
        return carry

    lax.fori_loop(0, count // 2, pair, 0)

    @pl.when(count % 2 == 1)
    def _():
        acc_sc[...] += _dot(pa_sc[...], values(off_diag(count - 1)))

    denom_min = jnp.min(acc_sc[:, B_VDIM:B_VDIM + 1])

    @pl.when(jnp.logical_not(denom_min >= _MIN_DENOM))
    def _():
        build_queries(False)
        m_sc[...] = jnp.full_like(m_sc, NEG_INF)
        acc_sc[...] = jnp.zeros_like(acc_sc)

        def chunk(j, carry):
            scores(j, sa_sc)
            diag_bias(j, sa_sc)
            adj = jnp.where(j > i, off, jnp.where(j < i, -off, 0.0))
            start = pl.multiple_of(j * tk, tk)
            s = sa_sc[...]
            m_prev = m_sc[...]
            m_new = jnp.maximum(m_prev, jnp.max(s, axis=-1, keepdims=True) + adj)
            p = jnp.exp2(s - (m_new - adj)).astype(BF16)
            acc_sc[...] = jnp.exp2(m_prev - m_new) * acc_sc[...] + _dot(p, va_sc[pl.ds(start, tk), :])
            m_sc[...] = m_new
            return carry

        lax.fori_loop(0, nk, chunk, 0)

    lp = lam_ref[...]
    lam = (jnp.exp(jnp.sum(lp[0:1] * lp[1:2], axis=-1, keepdims=True))
           - jnp.exp(jnp.sum(lp[2:3] * lp[3:4], axis=-1, keepdims=True)) + lambda_init)
    acc = acc_sc[...]
    o0 = acc[:tq, :B_VDIM] * (1.0 / acc[:tq, B_VDIM:B_VDIM + 1])
    o1 = acc[tq:, :B_VDIM] * (1.0 / acc[tq:, B_VDIM:B_VDIM + 1])
    o_ref[...] = (_rms(o0 - lam * o1, g_ref[...]) * out_scale).astype(o_ref.dtype)


def _bf16_pieces(x, n=3):
    out = []
    r = np.float64(x)
    for _ in range(n):
        p = np.float64(np.float32(r).astype(jnp.bfloat16).astype(np.float32))
        out.append(p)
        r = r - p
    return out


def _alibi_features(slopes2, t):
    pos = np.arange(t)
    hi = (pos // _POS_SPLIT) * _POS_SPLIT
    lo = pos % _POS_SPLIT
    qf = np.zeros((len(slopes2), t, LANES), np.float32)
    kf = np.zeros((len(slopes2), t, LANES), np.float32)
    for h, s in enumerate(slopes2):
        for n, piece in enumerate(_bf16_pieces(s)):
            for base, part in ((0, hi), (3, lo)):
                qf[h, :, base + n] = -piece
                kf[h, :, base + n] = part
                qf[h, :, 6 + base + n] = part
                kf[h, :, 6 + base + n] = piece
        for n, piece in enumerate(_bf16_pieces(s * t)):
            qf[h, :, _CHUNK_LANE + n] = -piece
        kf[h, :, _BOUND_LANE:_BOUND_LANE + 3] = -1.0
    return jnp.asarray(qf, BF16), jnp.asarray(kf, BF16)


def _diff_attention(proj, b_lambda, subln_g, lambda_init, batch, seq, t=512):
    t = min(t, seq // 2)
    nq = seq // t
    assert seq % (2 * t) == 0
    kcol = MIX_WIDTH // B_VDIM
    vcol = 2 * kcol
    slopes2 = [s * LOG2E for s in _alibi_slopes(B_HEADS)]
    reach = [min(nq, int(math.floor((_ZERO_EXP2_ARG / s - 1.0) / t)) + 1) for s in slopes2]
    qf, kf = _alibi_features(slopes2, t)
    pos = np.arange(t)
    dist = jnp.asarray(np.abs(pos[:, None] - pos[None, :]), F32)
    kernel = functools.partial(_diff_attn_kernel, tk=t, out_scale=1.0 - lambda_init,
                               lambda_init=lambda_init)
    return pl.pallas_call(
        kernel,
        grid_spec=pltpu.PrefetchScalarGridSpec(
            num_scalar_prefetch=2,
            grid=(batch, B_HEADS, nq),
            in_specs=[pl.BlockSpec((t, B_VDIM), lambda b, h, i, r, s: (b * nq + i, h)),
                      pl.BlockSpec((seq, B_VDIM), lambda b, h, i, r, s: (b, kcol + h)),
                      pl.BlockSpec((seq, B_VDIM), lambda b, h, i, r, s: (b, vcol + h)),
                      pl.BlockSpec((None, t, LANES), lambda b, h, i, r, s: (h, 0, 0)),
                      pl.BlockSpec((None, t, LANES), lambda b, h, i, r, s: (h, 0, 0)),
                      pl.BlockSpec((t, t), lambda b, h, i, r, s: (0, 0)),
                      pl.BlockSpec((4, HEAD_DIM), lambda b, h, i, r, s: (0, 0)),
                      pl.BlockSpec((1, B_VDIM), lambda b, h, i, r, s: (0, 0))],
            out_specs=pl.BlockSpec((t, B_VDIM), lambda b, h, i, r, s: (b * nq + i, h)),
            scratch_shapes=[pltpu.VMEM((3, 2 * t, 2 * LANES), BF16),
                            pltpu.VMEM((seq, 2 * LANES), BF16),
                            pltpu.VMEM((seq, 2 * LANES), BF16),
                            pltpu.VMEM((2, 8, LANES), F32),
                            pltpu.VMEM((2 * t, 1), F32),
                            pltpu.VMEM((2 * t, 2 * LANES), F32),
                            pltpu.VMEM((2 * t, t), F32),
                            pltpu.VMEM((2 * t, t), BF16),
                            pltpu.VMEM((2 * t, t), BF16)],
        ),
        out_shape=jax.ShapeDtypeStruct((batch * seq, MIX_WIDTH), BF16),
        compiler_params=_params("arbitrary", "arbitrary", "arbitrary"),
        name="diff_attention",
    )(jnp.asarray(reach, jnp.int32), jnp.asarray(slopes2, F32), proj, proj, proj, qf, kf, dist,
      b_lambda, subln_g.reshape(1, B_VDIM))


def _router_kernel(x_ref, g_ref, whi_ref, wlo_ref, h_ref, e_ref, gate_ref):
    h = _rms(x_ref[...], g_ref[...])
    h_hi = h.astype(BF16)
    h_lo = (h - h_hi.astype(F32)).astype(BF16)
    whi = whi_ref[...]
    logits = _dot(h_hi, whi) + (_dot(h_hi, wlo_ref[...]) + _dot(h_lo, whi))
    lane = lax.broadcasted_iota(jnp.int32, logits.shape, 1)
    logits = jnp.where(lane < N_EXPERTS, logits, -jnp.inf)
    m1 = jnp.max(logits, axis=-1, keepdims=True)
    i1 = jnp.min(jnp.where(logits == m1, lane, LANES), axis=-1, keepdims=True)
    rest = jnp.where(lane == i1, -jnp.inf, logits)
    m2 = jnp.max(rest, axis=-1, keepdims=True)
    i2 = jnp.min(jnp.where(rest == m2, lane, LANES), axis=-1, keepdims=True)
    r = jnp.exp(m2 - m1)
    g1 = 1.0 / (1.0 + r)
    g2 = r * g1
    e_ref[...] = jnp.where(lane == 0, i1, jnp.where(lane == 1, i2, 0))
    gate_ref[...] = jnp.where(lane == 0, g1, jnp.where(lane == 1, g2, 0.0))
    h_ref[...] = h_hi


def _router(x, g, w_router, tm=512):
    t, d = x.shape
    tm = min(tm, t)
    wpad = jnp.zeros((d, LANES), F32).at[:, :N_EXPERTS].set(w_router)
    whi = wpad.astype(BF16)
    wlo = (wpad - whi.astype(F32)).astype(BF16)
    return pl.pallas_call(
        _router_kernel,
        grid=(t // tm,),
        in_specs=[pl.BlockSpec((tm, d), lambda i: (i, 0)),
                  pl.BlockSpec((1, d), lambda i: (0, 0)),
                  pl.BlockSpec((d, LANES), lambda i: (0, 0)),
                  pl.BlockSpec((d, LANES), lambda i: (0, 0))],
        out_specs=[pl.BlockSpec((tm, d), lambda i: (i, 0)),
                   pl.BlockSpec((tm, LANES), lambda i: (i, 0)),
                   pl.BlockSpec((tm, LANES), lambda i: (i, 0))],
        out_shape=[jax.ShapeDtypeStruct((t, d), BF16),
                   jax.ShapeDtypeStruct((t, LANES), jnp.int32),
                   jax.ShapeDtypeStruct((t, LANES), F32)],
        compiler_params=_params("parallel"),
        name="router",
    )(x, g.reshape(1, d), whi, wlo)


def _dispatch_kernel(slab_ref, fill_ref, h_ref, dest_ref, gate_ref, xs_hbm, gs_hbm,
                     xslab_sc, gslab_sc, zx_sc, zg_sc, sem, zsem):
    i = pl.program_id(0)
    n = pl.num_programs(0)
    slot = i % 2
    tc = h_ref.shape[0]
    tm = zx_sc.shape[0]

    def slab_dma(step, sl, op):
        for e in range(N_EXPERTS):
            start = pl.multiple_of(slab_ref[step * N_EXPERTS + e], SUBLANES)
            short = slab_ref[(n + step) * N_EXPERTS + e] <= tc // 2
            for size, cond in ((tc // 2, short), (tc, jnp.logical_not(short))):
                @pl.when(cond)
                def _():
                    for src, dst in ((xslab_sc, xs_hbm), (gslab_sc, gs_hbm)):
                        copy = pltpu.make_async_copy(src.at[sl, pl.ds(e * tc, size)],
                                                     dst.at[pl.ds(start, size)], sem.at[sl])
                        getattr(copy, op)()

    def fill_copies(start):
        start = pl.multiple_of(start, SUBLANES)
        return [pltpu.make_async_copy(zx_sc, xs_hbm.at[pl.ds(start, tm)], zsem),
                pltpu.make_async_copy(zg_sc, gs_hbm.at[pl.ds(start, tm)], zsem)]

    @pl.when(i == 0)
    def _():
        zx_sc[...] = jnp.zeros_like(zx_sc)
        zg_sc[...] = jnp.zeros_like(zg_sc)
        for half in range(2):
            for e in range(N_EXPERTS):
                for c in fill_copies(fill_ref[half * N_EXPERTS + e]):
                    c.start()
            for e in range(N_EXPERTS):
                for c in fill_copies(fill_ref[half * N_EXPERTS + e]):
                    c.wait()
        first_free = fill_ref[2 * N_EXPERTS]
        n_tiles = xs_hbm.shape[0] // tm

        def start_tile(b, carry):
            for c in fill_copies(b * tm):
                c.start()
            return carry

        def wait_tile(b, carry):
            for c in fill_copies(b * tm):
                c.wait()
            return carry

        lax.fori_loop(first_free, n_tiles, start_tile, 0)
        lax.fori_loop(first_free, n_tiles, wait_tile, 0)

    dest = dest_ref[0]
    pos = lax.broadcasted_iota(jnp.int32, (tc, tc), 0)
    sel0, sel1 = [], []
    for e in range(N_EXPERTS):
        start = slab_ref[i * N_EXPERTS + e]
        sel0.append(jnp.where(dest[0:1, :] - start == pos, 1.0, 0.0))
        sel1.append(jnp.where(dest[1:2, :] - start == pos, 1.0, 0.0))
    sel0 = jnp.concatenate(sel0, axis=0).astype(BF16)
    sel1 = jnp.concatenate(sel1, axis=0).astype(BF16)
    gate = gate_ref[...]
    lane = lax.broadcasted_iota(jnp.int32, gate.shape, 1)

    def gate_pieces(col):
        out = jnp.zeros(gate.shape, F32)
        for n_, piece in enumerate(_bf16_split(col)):
            out = jnp.where(lane == n_, piece.astype(F32), out)
        return out.astype(BF16)

    xslab_sc[slot] = _dot(sel0 + sel1, h_ref[...])
    gslab_sc[slot] = _dot(sel0, gate_pieces(gate[:, 0:1])) + _dot(sel1, gate_pieces(gate[:, 1:2]))

    @pl.when(i > 0)
    def _():
        slab_dma(i - 1, 1 - slot, "wait")

    slab_dma(i, slot, "start")

    @pl.when(i == n - 1)
    def _():
        slab_dma(i, slot, "wait")


def _dispatch(h, dest_t, gates, slab, fill, p_rows, tm, tc):
    t, d = h.shape
    nb = t // tc
    return pl.pallas_call(
        _dispatch_kernel,
        grid_spec=pltpu.PrefetchScalarGridSpec(
            num_scalar_prefetch=2,
            grid=(nb,),
            in_specs=[pl.BlockSpec((tc, d), lambda i, s, z: (i, 0)),
                      pl.BlockSpec((1, TOP_K, tc), lambda i, s, z: (i, 0, 0)),
                      pl.BlockSpec((tc, LANES), lambda i, s, z: (i, 0))],
            out_specs=[pl.BlockSpec(memory_space=pl.ANY), pl.BlockSpec(memory_space=pl.ANY)],
            scratch_shapes=[pltpu.VMEM((2, N_EXPERTS * tc, d), F32),
                            pltpu.VMEM((2, N_EXPERTS * tc, LANES), F32),
                            pltpu.VMEM((tm, d), F32), pltpu.VMEM((tm, LANES), F32),
                            pltpu.SemaphoreType.DMA((2,)), pltpu.SemaphoreType.DMA(())],
        ),
        out_shape=[jax.ShapeDtypeStruct((p_rows, d), F32), jax.ShapeDtypeStruct((p_rows, LANES), F32)],
        compiler_params=_params("arbitrary"),
        name="moe_dispatch",
    )(slab, fill, h, dest_t, gates)


def _moe_kernel(blk_e_ref, used_ref, xblk_ref, x_ref, gs_ref, wg_ref, wu_ref, wd_ref, o_ref,
                xb_sc, acc_sc):
    i = pl.program_id(0)
    f = pl.program_id(1)
    used = used_ref[i] > 0

    @pl.when(jnp.logical_and(used, f == 0))
    def _():
        xb_sc[...] = x_ref[...].astype(BF16)
        acc_sc[...] = jnp.zeros_like(acc_sc)

    @pl.when(used)
    def _():
        acc_sc[...] += _swiglu_tile(xb_sc[...], wg_ref, wu_ref, wd_ref)

    @pl.when(f == pl.num_programs(1) - 1)
    def _():
        gs = gs_ref[...]
        gate = gs[:, 0:1] + gs[:, 1:2] + gs[:, 2:3]
        o_ref[...] = jnp.where(used, acc_sc[...] * gate, 0.0)


def _moe_experts(xs, gs, blk_e, blk_used, xblk, wg, wu, wd, tm, tf=1792):
    p, d = xs.shape
    nblk = p // tm
    nf = wg.shape[2] // tf

    def fidx(i, f, used_ref):
        return jnp.where(used_ref[i] > 0, f, nf - 1)

    return pl.pallas_call(
        _moe_kernel,
        grid_spec=pltpu.PrefetchScalarGridSpec(
            num_scalar_prefetch=3,
            grid=(nblk, nf),
            in_specs=[pl.BlockSpec((tm, d), lambda i, f, be, us, xb: (xb[i], 0)),
                      pl.BlockSpec((tm, LANES), lambda i, f, be, us, xb: (xb[i], 0)),
                      pl.BlockSpec((None, d, tf), lambda i, f, be, us, xb: (be[i], 0, fidx(i, f, us))),
                      pl.BlockSpec((None, d, tf), lambda i, f, be, us, xb: (be[i], 0, fidx(i, f, us))),
                      pl.BlockSpec((None, tf, d), lambda i, f, be, us, xb: (be[i], fidx(i, f, us), 0))],
            out_specs=pl.BlockSpec((tm, d), lambda i, f, be, us, xb: (i, 0)),
            scratch_shapes=[pltpu.VMEM((tm, d), BF16), pltpu.VMEM((tm, d), F32)],
        ),
        out_shape=jax.ShapeDtypeStruct((p, d), F32),
        compiler_params=_params("arbitrary", "arbitrary"),
        name="moe_experts",
    )(blk_e, blk_used, xblk, xs, gs, wg, wu, wd)


def _combine_kernel(slab_ref, ys_hbm, dest_ref, x_ref, g_ref, o_ref, slab_sc, sem):
    i = pl.program_id(0)
    n = pl.num_programs(0)
    slot = i % 2
    tc = x_ref.shape[0]

    def slab_dma(step, sl, op):
        for e in range(N_EXPERTS):
            start = pl.multiple_of(slab_ref[step * N_EXPERTS + e], SUBLANES)
            short = slab_ref[(n + step) * N_EXPERTS + e] <= tc // 2
            for size, cond in ((tc // 2, short), (tc, jnp.logical_not(short))):
                @pl.when(cond)
                def _():
                    copy = pltpu.make_async_copy(ys_hbm.at[pl.ds(start, size)],
                                                 slab_sc.at[sl, pl.ds(e * tc, size)], sem.at[sl])
                    getattr(copy, op)()

    @pl.when(i == 0)
    def _():
        slab_sc[...] = jnp.zeros_like(slab_sc)
        slab_dma(0, 0, "start")

    @pl.when(i + 1 < n)
    def _():
        slab_dma(i + 1, 1 - slot, "start")

    dest = dest_ref[...]
    pos = lax.broadcasted_iota(jnp.int32, (tc, tc), 1)
    sel = []
    for e in range(N_EXPERTS):
        start = slab_ref[i * N_EXPERTS + e]
        hit = jnp.logical_or(dest[:, 0:1] - start == pos, dest[:, 1:2] - start == pos)
        sel.append(jnp.where(hit, 1.0, 0.0))
    sel = jnp.concatenate(sel, axis=1).astype(BF16)

    slab_dma(i, slot, "wait")
    y = _dot(sel, slab_sc[slot].astype(BF16))
    o_ref[...] = x_ref[...] + _rms(y, g_ref[...])


def _combine(ys, dest, slab, x, g, tc):
    t, d = x.shape
    return pl.pallas_call(
        _combine_kernel,
        grid_spec=pltpu.PrefetchScalarGridSpec(
            num_scalar_prefetch=1,
            grid=(t // tc,),
            in_specs=[pl.BlockSpec(memory_space=pl.ANY),
                      pl.BlockSpec((tc, LANES), lambda i, s: (i, 0)),
                      pl.BlockSpec((tc, d), lambda i, s: (i, 0)),
                      pl.BlockSpec((1, d), lambda i, s: (0, 0))],
            out_specs=pl.BlockSpec((tc, d), lambda i, s: (i, 0)),
            scratch_shapes=[pltpu.VMEM((2, N_EXPERTS * tc, d), F32), pltpu.SemaphoreType.DMA((2,))],
        ),
        out_shape=jax.ShapeDtypeStruct((t, d), F32),
        compiler_params=_params("arbitrary"),
        name="moe_combine",
    )(slab, ys, dest, x, g.reshape(1, d))


def _route(top_e, tm, tc):
    t = top_e.shape[0]
    a = t * TOP_K
    nb = t // tc
    e_flat = top_e.reshape(a)
    onehot = (e_flat[:, None] == jnp.arange(N_EXPERTS, dtype=jnp.int32)[None, :]).astype(jnp.int32)
    csum = jnp.cumsum(onehot, axis=0)
    rank = jnp.sum(onehot * (csum - 1), axis=1)
    upto = csum[TOP_K * tc - 1::TOP_K * tc]
    before = jnp.concatenate([jnp.zeros((1, N_EXPERTS), jnp.int32), upto[:nb - 1]], axis=0)
    rows = (upto - before + SUBLANES - 1) // SUBLANES * SUBLANES
    rows_before = jnp.cumsum(rows, axis=0) - rows
    total = jnp.sum(rows, axis=0)
    region = (total + 2 * tm - 1) // tm * tm
    rends = jnp.cumsum(region)
    rstarts = rends - region
    slab = (rstarts[None, :] + rows_before).astype(jnp.int32)
    shift = jnp.broadcast_to((slab - before)[:, None, :], (nb, TOP_K * tc, N_EXPERTS)).reshape(a, N_EXPERTS)
    dest = (rank + jnp.sum(onehot * shift, axis=1)).astype(jnp.int32).reshape(t, TOP_K)
    slab = jnp.concatenate([slab.reshape(nb * N_EXPERTS), rows.reshape(nb * N_EXPERTS)])
    row_end = rstarts + total
    fill = jnp.concatenate([row_end, rends - tm, rends[-1:] // tm]).astype(jnp.int32)
    nblk = -(-(a + nb * N_EXPERTS * (SUBLANES - 1)) // tm) + 2 * N_EXPERTS
    blk_start = jnp.arange(nblk, dtype=jnp.int32) * tm
    blk_e = jnp.minimum(jnp.sum((rends[None, :] <= blk_start[:, None]).astype(jnp.int32), axis=1),
                        N_EXPERTS - 1)
    blk_used = jnp.logical_and(blk_start < row_end[blk_e], blk_start < rends[-1]).astype(jnp.int32)
    first_used = jnp.argmax(blk_used).astype(jnp.int32)
    xblk = jnp.where(blk_used > 0, jnp.arange(nblk, dtype=jnp.int32), first_used)
    return dest, slab, fill, blk_e, blk_used, xblk, nblk * tm


def _moe(x, g_in, g_out, w_router, wg, wu, wd, tm=512, tc=256):
    t = x.shape[0]
    tc = min(tc, t)
    h, e_pad, gates = _router(x, g_in, w_router)
    dest, slab, fill, blk_e, blk_used, xblk, p_rows = _route(e_pad[:, :TOP_K], tm, tc)
    dest_t = dest.reshape(t // tc, tc, TOP_K).transpose(0, 2, 1)
    dest_pad = jnp.zeros((t, LANES), jnp.int32).at[:, :TOP_K].set(dest)
    xs, gs = _dispatch(h, dest_t, gates, slab, fill, p_rows, tm, tc)
    ys = _moe_experts(xs, gs, blk_e, blk_used, xblk, wg, wu, wd, tm)
    return _combine(ys, dest_pad, slab, x, g_out, tc)


def kernel(x, mem, norm_g, mem_norm_g, w_mem_kv, a_w_in, a_sink, a_w_out, b_w_in, b_lambda,
           b_subln_g, b_w_out, ffn_w_gate, ffn_w_up, ffn_w_down, moe_w_router, moe_w_gate,
           moe_w_up, moe_w_down):
    batch, seq, d = x.shape
    n_mem = mem.shape[1]
    bf = lambda w: w.astype(BF16)
    xt = x.reshape(batch * seq, d)
    mkv = _norm_matmul(mem.reshape(batch * n_mem, d), mem_norm_g, bf(w_mem_kv),
                       jnp.ones((w_mem_kv.shape[1],), F32))

    g = norm_g[0]
    proj = _norm_matmul(xt, g[0], bf(a_w_in[0]), _query_col_scale(a_w_in.shape[2], A_Q_HEADS * HEAD_DIM))
    mix = _window_attention(proj, a_sink[0], batch, seq)
    mem_out = _mem_attention(proj, (A_Q_HEADS + 2 * A_KV_HEADS) * HEAD_DIM // MEM_WIDTH, mkv, batch, seq)
    xt = _out_proj(mix, mem_out, bf(a_w_out[0]), xt, g[1])
    xt = _ffn(xt, g[2], g[3], bf(ffn_w_gate[0]), bf(ffn_w_up[0]), bf(ffn_w_down[0]))

    g = norm_g[1]
    lambda_init = 0.8 - 0.6 * math.exp(-0.3 * 1)
    proj = _norm_matmul(xt, g[0], bf(b_w_in[0]), _query_col_scale(b_w_in.shape[2], MIX_WIDTH))
    mix = _diff_attention(proj, b_lambda[0], b_subln_g[0], lambda_init, batch, seq)
    mem_out = _mem_attention(proj, 3 * MIX_WIDTH // MEM_WIDTH, mkv, batch, seq)
    xt = _out_proj(mix, mem_out, bf(b_w_out[0]), xt, g[1])
    xt = _moe(xt, g[2], g[3], moe_w_router[0], bf(moe_w_gate[0]), bf(moe_w_up[0]), bf(moe_w_down[0]))
    return xt.reshape(batch, seq, d)
```

```python
import functools
import math

import jax
import jax.numpy as jnp
import numpy as np
from jax import lax
from jax.experimental import pallas as pl
from jax.experimental.pallas import tpu as pltpu

D_MODEL = 1024
HEAD_DIM = 64
MIX_WIDTH = 768
MEM_WIDTH = 256
MEM_HEADS = 4
A_Q_HEADS = 12
A_KV_HEADS = 4
A_GROUP = 3
WINDOW = 128
B_HEADS = 6
B_VDIM = 128
D_FF = 3584
N_EXPERTS = 8
TOP_K = 2
EPS = 1e-6
NEG_INF = -1e30
LOG2E = math.log2(math.e)
QK_SCALE = HEAD_DIM ** -0.5 * LOG2E
LANES = 128
SUBLANES = 8
VMEM_LIMIT_BYTES = 48 * 1024 * 1024

BF16 = jnp.bfloat16
F32 = jnp.float32


def _params(*sem):
    return pltpu.CompilerParams(dimension_semantics=sem, vmem_limit_bytes=VMEM_LIMIT_BYTES)


def _rms(x, g):
    return x * lax.rsqrt(jnp.mean(x * x, axis=-1, keepdims=True) + EPS) * g


def _dot(a, b):
    return jnp.dot(a, b, preferred_element_type=F32)


def _dot_nt(a, b):
    return lax.dot_general(a, b, (((1,), (1,)), ((), ())), preferred_element_type=F32)


def _alibi_slopes(n):
    return [2.0 ** (-8.0 * (i + 1) / n) for i in range(n)]


def _norm_matmul_kernel(x_ref, g_ref, w_ref, cs_ref, o_ref, *, chunk):
    h = _rms(x_ref[...], g_ref[...]).astype(BF16)
    for c in range(o_ref.shape[1] // chunk):
        sl = slice(c * chunk, (c + 1) * chunk)
        o_ref[:, sl] = (_dot(h, w_ref[:, sl]) * cs_ref[:, sl]).astype(o_ref.dtype)


def _norm_matmul(x, g, w, col_scale, tm=512, chunk=512):
    t, d = x.shape
    n = w.shape[1]
    tm = min(tm, t)
    return pl.pallas_call(
        functools.partial(_norm_matmul_kernel, chunk=min(chunk, n)),
        grid=(t // tm,),
        in_specs=[pl.BlockSpec((tm, d), lambda i: (i, 0)),
                  pl.BlockSpec((1, d), lambda i: (0, 0)),
                  pl.BlockSpec((d, n), lambda i: (0, 0)),
                  pl.BlockSpec((1, n), lambda i: (0, 0))],
        out_specs=pl.BlockSpec((tm, n), lambda i: (i, 0)),
        out_shape=jax.ShapeDtypeStruct((t, n), BF16),
        compiler_params=_params("parallel"),
        name="norm_matmul",
    )(x, g.reshape(1, d), w, col_scale.reshape(1, n))


def _query_col_scale(n, mix_q_cols):
    cols = np.arange(n)
    return jnp.asarray(np.where((cols < mix_q_cols) | (cols >= n - MEM_WIDTH), QK_SCALE, 1.0), F32)


def _window_kernel(sink_ref, q_ref, kp_ref, kc_ref, kn_ref, vp_ref, vc_ref, vn_ref, bias_ref, o_ref):
    n = pl.program_id(1)
    nb = pl.num_programs(1)
    blk = kp_ref.shape[0]
    sub = q_ref.shape[0] // blk
    k_all = jnp.concatenate([kp_ref[...], kc_ref[...], kn_ref[...]], axis=0)
    v_all = jnp.concatenate([vp_ref[...], vc_ref[...], vn_ref[...]], axis=0)
    col = lax.broadcasted_iota(jnp.int32, (1, 3 * blk), 1)
    for u in range(sub):
        k = k_all[u * blk:(u + 3) * blk]
        v = v_all[u * blk:(u + 3) * blk]
        edge = jnp.zeros((1, 3 * blk), F32)
        if u == 0:
            edge = jnp.where(jnp.logical_and(n == 0, col < blk), NEG_INF, edge)
        if u == sub - 1:
            edge = jnp.where(jnp.logical_and(n == nb - 1, col >= 2 * blk), NEG_INF, edge)
        q = q_ref[u * blk:(u + 1) * blk, :]
        outs = []
        for kh in range(A_KV_HEADS):
            heads = range(kh * A_GROUP, (kh + 1) * A_GROUP)
            qs = jnp.concatenate([q[:, h * HEAD_DIM:(h + 1) * HEAD_DIM] for h in heads], axis=0)
            sink = jnp.concatenate([jnp.full((blk, 1), sink_ref[h] * LOG2E, F32) for h in heads], axis=0)
            s = _dot_nt(qs, k[:, kh * HEAD_DIM:(kh + 1) * HEAD_DIM]) + bias_ref[kh] + edge
            m = jnp.maximum(jnp.max(s, axis=-1, keepdims=True), sink)
            p = jnp.exp2(s - m)
            denom = jnp.sum(p, axis=-1, keepdims=True) + jnp.exp2(sink - m)
            o = _dot(p.astype(BF16), v[:, kh * HEAD_DIM:(kh + 1) * HEAD_DIM]) * (1.0 / denom)
            outs.extend(o[g * blk:(g + 1) * blk] for g in range(A_GROUP))
        o_ref[u * blk:(u + 1) * blk, :] = jnp.concatenate(outs, axis=-1).astype(o_ref.dtype)


def _window_bias(blk):
    qi = np.arange(blk)[:, None]
    kj = np.arange(3 * blk)[None, :]
    dist = np.abs(blk + qi - kj)
    slopes = _alibi_slopes(A_Q_HEADS)
    tables = [np.where(dist <= WINDOW, -(slopes[h] * LOG2E) * dist, NEG_INF) for h in range(A_Q_HEADS)]
    return jnp.asarray(np.stack(tables).reshape(A_KV_HEADS, A_GROUP * blk, 3 * blk), F32)


def _window_attention(proj, sink, batch, seq, sub=2):
    blk = WINDOW
    nb = seq // blk
    sub = min(sub, nb)
    ns = nb // sub
    kcol = MIX_WIDTH // 256
    vcol = kcol + 1
    bias = _window_bias(blk)

    def cur(col):
        return pl.BlockSpec((sub * blk, 256), lambda b, n, s: (b * ns + n, col))

    def prev(col):
        return pl.BlockSpec((blk, 256), lambda b, n, s: (b * nb + jnp.maximum(n * sub - 1, 0), col))

    def nxt(col):
        return pl.BlockSpec((blk, 256), lambda b, n, s: (b * nb + jnp.minimum((n + 1) * sub, nb - 1), col))

    return pl.pallas_call(
        _window_kernel,
        grid_spec=pltpu.PrefetchScalarGridSpec(
            num_scalar_prefetch=1,
            grid=(batch, ns),
            in_specs=[pl.BlockSpec((sub * blk, MIX_WIDTH), lambda b, n, s: (b * ns + n, 0)),
                      prev(kcol), cur(kcol), nxt(kcol), prev(vcol), cur(vcol), nxt(vcol),
                      pl.BlockSpec(bias.shape, lambda b, n, s: (0, 0, 0))],
            out_specs=pl.BlockSpec((sub * blk, MIX_WIDTH), lambda b, n, s: (b * ns + n, 0)),
        ),
        out_shape=jax.ShapeDtypeStruct((batch * seq, MIX_WIDTH), BF16),
        compiler_params=_params("parallel", "parallel"),
        name="window_attention",
    )(sink, proj, proj, proj, proj, proj, proj, proj, bias)


def _mem_attn_kernel(q_ref, mkv_ref, o_ref):
    q = q_ref[...]
    mkv = mkv_ref[...]
    outs = []
    for h in range(MEM_HEADS):
        sl = slice(h * HEAD_DIM, (h + 1) * HEAD_DIM)
        s = _dot_nt(q[:, sl], mkv[:, sl])
        m = jnp.max(s, axis=-1, keepdims=True)
        p = jnp.exp2(s - m)
        denom = jnp.sum(p, axis=-1, keepdims=True)
        o = _dot(p.astype(BF16), mkv[:, MEM_WIDTH + h * HEAD_DIM:MEM_WIDTH + (h + 1) * HEAD_DIM])
        outs.append(o * (1.0 / denom))
    o_ref[...] = jnp.concatenate(outs, axis=-1).astype(o_ref.dtype)


def _mem_attention(proj, qcol, mkv, batch, seq, tm=512):
    tm = min(tm, seq)
    per_batch = seq // tm
    n_mem = mkv.shape[0] // batch
    return pl.pallas_call(
        _mem_attn_kernel,
        grid=(batch * per_batch,),
        in_specs=[pl.BlockSpec((tm, MEM_WIDTH), lambda i: (i, qcol)),
                  pl.BlockSpec((n_mem, 2 * MEM_WIDTH), lambda i: (i // per_batch, 0))],
        out_specs=pl.BlockSpec((tm, MEM_WIDTH), lambda i: (i, 0)),
        out_shape=jax.ShapeDtypeStruct((batch * seq, MEM_WIDTH), BF16),
        compiler_params=_params("parallel"),
        name="mem_attention",
    )(proj, mkv)


def _out_proj_kernel(mix_ref, mem_ref, w1_ref, w2_ref, x_ref, g_ref, o_ref):
    o = _dot(mix_ref[...], w1_ref[...]) + _dot(mem_ref[...], w2_ref[...])
    o_ref[...] = x_ref[...] + _rms(o, g_ref[...])


def _out_proj(mix, mem_out, w_out, x, g, tm=512):
    t, d = x.shape
    tm = min(tm, t)
    w1 = w_out[:MIX_WIDTH]
    w2 = w_out[MIX_WIDTH:]
    return pl.pallas_call(
        _out_proj_kernel,
        grid=(t // tm,),
        in_specs=[pl.BlockSpec((tm, MIX_WIDTH), lambda i: (i, 0)),
                  pl.BlockSpec((tm, MEM_WIDTH), lambda i: (i, 0)),
                  pl.BlockSpec((MIX_WIDTH, d), lambda i: (0, 0)),
                  pl.BlockSpec((MEM_WIDTH, d), lambda i: (0, 0)),
                  pl.BlockSpec((tm, d), lambda i: (i, 0)),
                  pl.BlockSpec((1, d), lambda i: (0, 0))],
        out_specs=pl.BlockSpec((tm, d), lambda i: (i, 0)),
        out_shape=jax.ShapeDtypeStruct((t, d), F32),
        compiler_params=_params("parallel"),
        name="out_proj",
    )(mix, mem_out, w1, w2, x, g.reshape(1, d))


def _swiglu_tile(h, wg_ref, wu_ref, wd_ref, chunk=256):
    out = None
    for c in range(wg_ref.shape[1] // chunk):
        sl = slice(c * chunk, (c + 1) * chunk)
        a = _dot(h, wg_ref[:, sl])
        u = _dot(h, wu_ref[:, sl])
        z = (a * jax.nn.sigmoid(a) * u).astype(BF16)
        y = _dot(z, wd_ref[sl, :])
        out = y if out is None else out + y
    return out


def _ffn_kernel(x_ref, gin_ref, gout_ref, wg_ref, wu_ref, wd_ref, o_ref, h_sc, acc_sc):
    f = pl.program_id(1)

    @pl.when(f == 0)
    def _():
        h_sc[...] = _rms(x_ref[...], gin_ref[...]).astype(BF16)
        acc_sc[...] = jnp.zeros_like(acc_sc)

    acc_sc[...] += _swiglu_tile(h_sc[...], wg_ref, wu_ref, wd_ref)

    @pl.when(f == pl.num_programs(1) - 1)
    def _():
        o_ref[...] = x_ref[...] + _rms(acc_sc[...], gout_ref[...])


def _ffn(x, g_in, g_out, wg, wu, wd, tm=512, tf=1792):
    t, d = x.shape
    ff = wg.shape[1]
    tm = min(tm, t)
    return pl.pallas_call(
        _ffn_kernel,
        grid=(t // tm, ff // tf),
        in_specs=[pl.BlockSpec((tm, d), lambda i, f: (i, 0)),
                  pl.BlockSpec((1, d), lambda i, f: (0, 0)),
                  pl.BlockSpec((1, d), lambda i, f: (0, 0)),
                  pl.BlockSpec((d, tf), lambda i, f: (0, f)),
                  pl.BlockSpec((d, tf), lambda i, f: (0, f)),
                  pl.BlockSpec((tf, d), lambda i, f: (f, 0))],
        out_specs=pl.BlockSpec((tm, d), lambda i, f: (i, 0)),
        out_shape=jax.ShapeDtypeStruct((t, d), F32),
        scratch_shapes=[pltpu.VMEM((tm, d), BF16), pltpu.VMEM((tm, d), F32)],
        compiler_params=_params("parallel", "arbitrary"),
        name="ffn",
    )(x, g_in.reshape(1, d), g_out.reshape(1, d), wg, wu, wd)


_POS_SPLIT = 256
_LEFT, _DIAG, _RIGHT = 0, 1, 2
_BOUND_LANE = 12
_CHUNK_LANE = 15
_OFFSET_LANE = 18
_ZERO_EXP2_ARG = 136.0
_MIN_DENOM = 2.0 ** -60


def _bf16_split(x):
    hi = x.astype(BF16)
    r = x - hi.astype(F32)
    mid = r.astype(BF16)
    lo = (r - mid.astype(F32)).astype(BF16)
    return hi, mid, lo


def _diff_attn_kernel(reach_ref, slope_ref, q_ref, k_ref, v_ref, qf_ref, kf_ref, dist_ref, lam_ref,
                      g_ref, o_ref, qa_sc, ka_sc, va_sc, k2_sc, m_sc, acc_sc, sa_sc, pa_sc, pb_sc, *, tk,
                      out_scale, lambda_init):
    h = pl.program_id(1)
    i = pl.program_id(2)
    tq = q_ref.shape[0]
    seq = k_ref.shape[0]
    nk = seq // tk
    slope = slope_ref[h]
    reach = reach_ref[h]

    @pl.when(i == 0)
    def _():
        k = k_ref[...]
        ka_sc[:, :B_VDIM] = k
        va_sc[:, :B_VDIM] = v_ref[...]
        lane = lax.broadcasted_iota(jnp.int32, (seq, LANES), 1)
        va_sc[:, B_VDIM:] = jnp.where(lane == 0, 1.0, 0.0).astype(BF16)
        kf = kf_ref[...]
        flane = lax.broadcasted_iota(jnp.int32, kf.shape, 1)
        chunk_lanes = jnp.logical_and(flane >= _CHUNK_LANE, flane < _CHUNK_LANE + 3)
        for c in range(nk):
            ka_sc[c * tk:(c + 1) * tk, B_VDIM:] = jnp.where(chunk_lanes, float(c), kf).astype(BF16)
        kk = k.astype(F32)
        kk = kk * kk
        for c, sel in enumerate((lane < HEAD_DIM, lane >= HEAD_DIM)):
            n2 = jnp.sum(jnp.where(sel, kk, 0.0), axis=-1, keepdims=True)
            k2_sc[c] = jnp.broadcast_to(jnp.max(n2, axis=0, keepdims=True), k2_sc.shape[1:])

    q = q_ref[...]
    lane = lax.broadcasted_iota(jnp.int32, q.shape, 1)
    zero = jnp.zeros_like(q)
    q0 = jnp.where(lane < HEAD_DIM, q, zero)
    q1 = jnp.where(lane >= HEAD_DIM, q, zero)
    qq = q.astype(F32)
    qq = qq * qq
    n0 = jnp.sum(jnp.where(lane < HEAD_DIM, qq, 0.0), axis=-1, keepdims=True)
    n1 = jnp.sum(jnp.where(lane >= HEAD_DIM, qq, 0.0), axis=-1, keepdims=True)
    bound = jnp.concatenate([jnp.sqrt(n0) * jnp.sqrt(k2_sc[0][0:1, 0:1]),
                             jnp.sqrt(n1) * jnp.sqrt(k2_sc[1][0:1, 0:1])], axis=0)
    off = slope * (i * tk).astype(F32)
    qf = qf_ref[...]
    qf2 = jnp.concatenate([qf, qf], axis=0)

    def piece_lanes(x, first_lane):
        lane_ = lax.broadcasted_iota(jnp.int32, x.shape, 1)
        out = jnp.zeros_like(x)
        for n, piece in enumerate(_bf16_split(x)):
            out = jnp.where(lane_ == first_lane + n, piece.astype(F32), out)
        return out.astype(BF16)

    def build_queries(with_bound):
        feats = {_LEFT: -qf2, _DIAG: jnp.zeros_like(qf2), _RIGHT: qf2}
        if with_bound:
            base = piece_lanes(jnp.broadcast_to(bound, (2 * tq, LANES)), _BOUND_LANE)
            side = piece_lanes(jnp.full((SUBLANES, LANES), off, F32), _OFFSET_LANE)[0:1]
            feats = {_LEFT: base + side - qf2, _DIAG: base, _RIGHT: base - side + qf2}
        for variant, feat in feats.items():
            qa_sc[variant, :, B_VDIM:] = feat
            qa_sc[variant, :tq, :B_VDIM] = q0
            qa_sc[variant, tq:, :B_VDIM] = q1

    def scores(j, s_ref):
        variant = jnp.where(j < i, _LEFT, jnp.where(j == i, _DIAG, _RIGHT))
        start = pl.multiple_of(j * tk, tk)
        s_ref[...] = _dot_nt(qa_sc[variant], ka_sc[pl.ds(start, tk), :])

    def diag_bias(j, s_ref):
        @pl.when(j == i)
        def _():
            bias = slope * dist_ref[...]
            s_ref[:tq] -= bias
            s_ref[tq:] -= bias

    def probs(j, p_ref):
        start = pl.multiple_of(j * tk, tk)
        s = _dot_nt(qa_sc[jnp.where(j < i, _LEFT, _RIGHT)], ka_sc[pl.ds(start, tk), :])
        p_ref[...] = jnp.exp2(s).astype(BF16)

    def values(j):
        return va_sc[pl.ds(pl.multiple_of(j * tk, tk), tk), :]

    build_queries(True)
    lo = jnp.maximum(i - reach, 0)
    count = jnp.minimum(i + reach + 1, nk) - lo - 1

    def off_diag(n):
        j = lo + jnp.minimum(n, count - 1)
        return j + (j >= i).astype(jnp.int32)

    s_diag = _dot_nt(qa_sc[_DIAG], ka_sc[pl.ds(pl.multiple_of(i * tk, tk), tk), :])
    probs(off_diag(0), pa_sc)
    bias = slope * dist_ref[...]
    p_diag = jnp.exp2(s_diag - jnp.concatenate([bias, bias], axis=0)).astype(BF16)
    acc_sc[...] = _dot(p_diag, values(i))

    def pair(tt, carry):
        n = 2 * tt
        probs(off_diag(n + 1), pb_sc)
        acc_sc[...] += _dot(pa_sc[...], values(off_diag(n)))
        probs(off_diag(n + 2), pa_sc)
        acc_sc[...] += _dot(pb_sc[...], values(off_diag(n + 1)))
        return carry

    lax.fori_loop(0, count // 2, pair, 0)

    @pl.when(count % 2 == 1)
    def _():
        acc_sc[...] += _dot(pa_sc[...], values(off_diag(count - 1)))

    denom_min = jnp.min(acc_sc[:, B_VDIM:B_VDIM + 1])

    @pl.when(jnp.logical_not(denom_min >= _MIN_DENOM))
    def _():
        build_queries(False)
        m_sc[...] = jnp.full_like(m_sc, NEG_INF)
        acc_sc[...] = jnp.zeros_like(acc_sc)

        def chunk(j, carry):
            scores(j, sa_sc)
            diag_bias(j, sa_sc)
            adj = jnp.where(j > i, off, jnp.where(j < i, -off, 0.0))
            start = pl.multiple_of(j * tk, tk)
            s = sa_sc[...]
            m_prev = m_sc[...]
            m_new = jnp.maximum(m_prev, jnp.max(s, axis=-1, keepdims=True) + adj)
            p = jnp.exp2(s - (m_new - adj)).astype(BF16)
            acc_sc[...] = jnp.exp2(m_prev - m_new) * acc_sc[...] + _dot(p, va_sc[pl.ds(start, tk), :])
            m_sc[...] = m_new
            return carry

        lax.fori_loop(0, nk, chunk, 0)

    lp = lam_ref[...]
    lam = (jnp.exp(jnp.sum(lp[0:1] * lp[1:2], axis=-1, keepdims=True))
           - jnp.exp(jnp.sum(lp[2:3] * lp[3:4], axis=-1, keepdims=True)) + lambda_init)
    acc = acc_sc[...]
    o0 = acc[:tq, :B_VDIM] * (1.0 / acc[:tq, B_VDIM:B_VDIM + 1])
    o1 = acc[tq:, :B_VDIM] * (1.0 / acc[tq:, B_VDIM:B_VDIM + 1])
    o_ref[...] = (_rms(o0 - lam * o1, g_ref[...]) * out_scale).astype(o_ref.dtype)


def _bf16_pieces(x, n=3):
    out = []
    r = np.float64(x)
    for _ in range(n):
        p = np.float64(np.float32(r).astype(jnp.bfloat16).astype(np.float32))
        out.append(p)
        r = r - p
    return out


def _alibi_features(slopes2, t):
    pos = np.arange(t)
    hi = (pos // _POS_SPLIT) * _POS_SPLIT
    lo = pos % _POS_SPLIT
    qf = np.zeros((len(slopes2), t, LANES), np.float32)
    kf = np.zeros((len(slopes2), t, LANES), np.float32)
    for h, s in enumerate(slopes2):
        for n, piece in enumerate(_bf16_pieces(s)):
            for base, part in ((0, hi), (3, lo)):
                qf[h, :, base + n] = -piece
                kf[h, :, base + n] = part
                qf[h, :, 6 + base + n] = part
                kf[h, :, 6 + base + n] = piece
        for n, piece in enumerate(_bf16_pieces(s * t)):
            qf[h, :, _CHUNK_LANE + n] = -piece
        kf[h, :, _BOUND_LANE:_BOUND_LANE + 3] = -1.0
        kf[h, :, _OFFSET_LANE:_OFFSET_LANE + 3] = -1.0
    return jnp.asarray(qf, BF16), jnp.asarray(kf, BF16)


def _diff_attention(proj, b_lambda, subln_g, lambda_init, batch, seq, t=512):
    t = min(t, seq // 2)
    nq = seq // t
    assert seq % (2 * t) == 0
    kcol = MIX_WIDTH // B_VDIM
    vcol = 2 * kcol
    slopes2 = [s * LOG2E for s in _alibi_slopes(B_HEADS)]
    reach = [min(nq, int(math.floor((_ZERO_EXP2_ARG / s - 1.0) / t)) + 1) for s in slopes2]
    qf, kf = _alibi_features(slopes2, t)
    pos = np.arange(t)
    dist = jnp.asarray(np.abs(pos[:, None] - pos[None, :]), F32)
    kernel = functools.partial(_diff_attn_kernel, tk=t, out_scale=1.0 - lambda_init,
                               lambda_init=lambda_init)
    return pl.pallas_call(
        kernel,
        grid_spec=pltpu.PrefetchScalarGridSpec(
            num_scalar_prefetch=2,
            grid=(batch, B_HEADS, nq),
            in_specs=[pl.BlockSpec((t, B_VDIM), lambda b, h, i, r, s: (b * nq + i, h)),
                      pl.BlockSpec((seq, B_VDIM), lambda b, h, i, r, s: (b, kcol + h)),
                      pl.BlockSpec((seq, B_VDIM), lambda b, h, i, r, s: (b, vcol + h)),
                      pl.BlockSpec((None, t, LANES), lambda b, h, i, r, s: (h, 0, 0)),
                      pl.BlockSpec((None, t, LANES), lambda b, h, i, r, s: (h, 0, 0)),
                      pl.BlockSpec((t, t), lambda b, h, i, r, s: (0, 0)),
                      pl.BlockSpec((4, HEAD_DIM), lambda b, h, i, r, s: (0, 0)),
                      pl.BlockSpec((1, B_VDIM), lambda b, h, i, r, s: (0, 0))],
            out_specs=pl.BlockSpec((t, B_VDIM), lambda b, h, i, r, s: (b * nq + i, h)),
            scratch_shapes=[pltpu.VMEM((3, 2 * t, 2 * LANES), BF16),
                            pltpu.VMEM((seq, 2 * LANES), BF16),
                            pltpu.VMEM((seq, 2 * LANES), BF16),
                            pltpu.VMEM((2, 8, LANES), F32),
                            pltpu.VMEM((2 * t, 1), F32),
                            pltpu.VMEM((2 * t, 2 * LANES), F32),
                            pltpu.VMEM((2 * t, t), F32),
                            pltpu.VMEM((2 * t, t), BF16),
                            pltpu.VMEM((2 * t, t), BF16)],
        ),
        out_shape=jax.ShapeDtypeStruct((batch * seq, MIX_WIDTH), BF16),
        compiler_params=_params("arbitrary", "arbitrary", "arbitrary"),
        name="diff_attention",
    )(jnp.asarray(reach, jnp.int32), jnp.asarray(slopes2, F32), proj, proj, proj, qf, kf, dist,
      b_lambda, subln_g.reshape(1, B_VDIM))


def _router_kernel(x_ref, g_ref, whi_ref, wlo_ref, h_ref, e_ref, gate_ref):
    h = _rms(x_ref[...], g_ref[...])
    h_hi = h.astype(BF16)
    h_lo = (h - h_hi.astype(F32)).astype(BF16)
    whi = whi_ref[...]
    logits = _dot(h_hi, whi) + (_dot(h_hi, wlo_ref[...]) + _dot(h_lo, whi))
    lane = lax.broadcasted_iota(jnp.int32, logits.shape, 1)
    logits = jnp.where(lane < N_EXPERTS, logits, -jnp.inf)
    m1 = jnp.max(logits, axis=-1, keepdims=True)
    i1 = jnp.min(jnp.where(logits == m1, lane, LANES), axis=-1, keepdims=True)
    rest = jnp.where(lane == i1, -jnp.inf, logits)
    m2 = jnp.max(rest, axis=-1, keepdims=True)
    i2 = jnp.min(jnp.where(rest == m2, lane, LANES), axis=-1, keepdims=True)
    r = jnp.exp(m2 - m1)
    g1 = 1.0 / (1.0 + r)
    g2 = r * g1
    e_ref[...] = jnp.where(lane == 0, i1, jnp.where(lane == 1, i2, 0))
    gate_ref[...] = jnp.where(lane == 0, g1, jnp.where(lane == 1, g2, 0.0))
    h_ref[...] = h_hi


def _router(x, g, w_router, tm=512):
    t, d = x.shape
    tm = min(tm, t)
    wpad = jnp.zeros((d, LANES), F32).at[:, :N_EXPERTS].set(w_router)
    whi = wpad.astype(BF16)
    wlo = (wpad - whi.astype(F32)).astype(BF16)
    return pl.pallas_call(
        _router_kernel,
        grid=(t // tm,),
        in_specs=[pl.BlockSpec((tm, d), lambda i: (i, 0)),
                  pl.BlockSpec((1, d), lambda i: (0, 0)),
                  pl.BlockSpec((d, LANES), lambda i: (0, 0)),
                  pl.BlockSpec((d, LANES), lambda i: (0, 0))],
        out_specs=[pl.BlockSpec((tm, d), lambda i: (i, 0)),
                   pl.BlockSpec((tm, LANES), lambda i: (i, 0)),
                   pl.BlockSpec((tm, LANES), lambda i: (i, 0))],
        out_shape=[jax.ShapeDtypeStruct((t, d), BF16),
                   jax.ShapeDtypeStruct((t, LANES), jnp.int32),
                   jax.ShapeDtypeStruct((t, LANES), F32)],
        compiler_params=_params("parallel"),
        name="router",
    )(x, g.reshape(1, d), whi, wlo)


def _slab_sizes(rows, tc):
    quarter, half = tc // 4, tc // 2
    return ((quarter, rows <= quarter),
            (half, jnp.logical_and(rows > quarter, rows <= half)),
            (tc, rows > half))


def _dispatch_kernel(slab_ref, fill_ref, h_ref, dest_ref, gate_ref, xs_hbm, gs_hbm,
                     xslab_sc, gslab_sc, zx_sc, zg_sc, sem, zsem):
    i = pl.program_id(0)
    n = pl.num_programs(0)
    slot = i % 2
    tc = h_ref.shape[0]
    tm = zx_sc.shape[0]

    def slab_dma(step, sl, op):
        for e in range(N_EXPERTS):
            start = pl.multiple_of(slab_ref[step * N_EXPERTS + e], SUBLANES)
            for size, cond in _slab_sizes(slab_ref[(n + step) * N_EXPERTS + e], tc):
                @pl.when(cond)
                def _():
                    for src, dst in ((xslab_sc, xs_hbm), (gslab_sc, gs_hbm)):
                        copy = pltpu.make_async_copy(src.at[sl, pl.ds(e * tc, size)],
                                                     dst.at[pl.ds(start, size)], sem.at[sl])
                        getattr(copy, op)()

    def fill_copies(start):
        start = pl.multiple_of(start, SUBLANES)
        return [pltpu.make_async_copy(zx_sc, xs_hbm.at[pl.ds(start, tm)], zsem),
                pltpu.make_async_copy(zg_sc, gs_hbm.at[pl.ds(start, tm)], zsem)]

    @pl.when(i == 0)
    def _():
        zx_sc[...] = jnp.zeros_like(zx_sc)
        zg_sc[...] = jnp.zeros_like(zg_sc)
        for half in range(2):
            for e in range(N_EXPERTS):
                for c in fill_copies(fill_ref[half * N_EXPERTS + e]):
                    c.start()
            for e in range(N_EXPERTS):
                for c in fill_copies(fill_ref[half * N_EXPERTS + e]):
                    c.wait()
        first_free = fill_ref[2 * N_EXPERTS]
        n_tiles = xs_hbm.shape[0] // tm

        def start_tile(b, carry):
            for c in fill_copies(b * tm):
                c.start()
            return carry

        def wait_tile(b, carry):
            for c in fill_copies(b * tm):
                c.wait()
            return carry

        lax.fori_loop(first_free, n_tiles, start_tile, 0)
        lax.fori_loop(first_free, n_tiles, wait_tile, 0)

    dest = dest_ref[0]
    pos = lax.broadcasted_iota(jnp.int32, (tc, tc), 0)
    sel0, sel1 = [], []
    for e in range(N_EXPERTS):
        start = slab_ref[i * N_EXPERTS + e]
        sel0.append(jnp.where(dest[0:1, :] - start == pos, 1.0, 0.0))
        sel1.append(jnp.where(dest[1:2, :] - start == pos, 1.0, 0.0))
    sel0 = jnp.concatenate(sel0, axis=0).astype(BF16)
    sel1 = jnp.concatenate(sel1, axis=0).astype(BF16)
    gate = gate_ref[...]
    lane = lax.broadcasted_iota(jnp.int32, gate.shape, 1)

    def gate_pieces(col):
        out = jnp.zeros(gate.shape, F32)
        for n_, piece in enumerate(_bf16_split(col)):
            out = jnp.where(lane == n_, piece.astype(F32), out)
        return out.astype(BF16)

    xslab_sc[slot] = _dot(sel0 + sel1, h_ref[...])
    gslab_sc[slot] = _dot(sel0, gate_pieces(gate[:, 0:1])) + _dot(sel1, gate_pieces(gate[:, 1:2]))

    @pl.when(i > 0)
    def _():
        slab_dma(i - 1, 1 - slot, "wait")

    slab_dma(i, slot, "start")

    @pl.when(i == n - 1)
    def _():
        slab_dma(i, slot, "wait")


def _dispatch(h, dest_t, gates, slab, fill, p_rows, tm, tc):
    t, d = h.shape
    nb = t // tc
    return pl.pallas_call(
        _dispatch_kernel,
        grid_spec=pltpu.PrefetchScalarGridSpec(
            num_scalar_prefetch=2,
            grid=(nb,),
            in_specs=[pl.BlockSpec((tc, d), lambda i, s, z: (i, 0)),
                      pl.BlockSpec((1, TOP_K, tc), lambda i, s, z: (i, 0, 0)),
                      pl.BlockSpec((tc, LANES), lambda i, s, z: (i, 0))],
            out_specs=[pl.BlockSpec(memory_space=pl.ANY), pl.BlockSpec(memory_space=pl.ANY)],
            scratch_shapes=[pltpu.VMEM((2, N_EXPERTS * tc, d), F32),
                            pltpu.VMEM((2, N_EXPERTS * tc, LANES), F32),
                            pltpu.VMEM((tm, d), F32), pltpu.VMEM((tm, LANES), F32),
                            pltpu.SemaphoreType.DMA((2,)), pltpu.SemaphoreType.DMA(())],
        ),
        out_shape=[jax.ShapeDtypeStruct((p_rows, d), F32), jax.ShapeDtypeStruct((p_rows, LANES), F32)],
        compiler_params=_params("arbitrary"),
        name="moe_dispatch",
    )(slab, fill, h, dest_t, gates)


def _moe_kernel(blk_e_ref, used_ref, xblk_ref, x_ref, gs_ref, wg_ref, wu_ref, wd_ref, o_ref,
                xb_sc, acc_sc):
    i = pl.program_id(0)
    f = pl.program_id(1)
    used = used_ref[i] > 0

    @pl.when(jnp.logical_and(used, f == 0))
    def _():
        xb_sc[...] = x_ref[...].astype(BF16)
        acc_sc[...] = jnp.zeros_like(acc_sc)

    @pl.when(used)
    def _():
        acc_sc[...] += _swiglu_tile(xb_sc[...], wg_ref, wu_ref, wd_ref)

    @pl.when(f == pl.num_programs(1) - 1)
    def _():
        gs = gs_ref[...]
        gate = gs[:, 0:1] + gs[:, 1:2] + gs[:, 2:3]
        o_ref[...] = jnp.where(used, acc_sc[...] * gate, 0.0)


def _moe_experts(xs, gs, blk_e, blk_used, xblk, wg, wu, wd, tm, tf=1792):
    p, d = xs.shape
    nblk = p // tm
    nf = wg.shape[2] // tf

    def fidx(i, f, used_ref):
        return jnp.where(used_ref[i] > 0, f, nf - 1)

    return pl.pallas_call(
        _moe_kernel,
        grid_spec=pltpu.PrefetchScalarGridSpec(
            num_scalar_prefetch=3,
            grid=(nblk, nf),
            in_specs=[pl.BlockSpec((tm, d), lambda i, f, be, us, xb: (xb[i], 0)),
                      pl.BlockSpec((tm, LANES), lambda i, f, be, us, xb: (xb[i], 0)),
                      pl.BlockSpec((None, d, tf), lambda i, f, be, us, xb: (be[i], 0, fidx(i, f, us))),
                      pl.BlockSpec((None, d, tf), lambda i, f, be, us, xb: (be[i], 0, fidx(i, f, us))),
                      pl.BlockSpec((None, tf, d), lambda i, f, be, us, xb: (be[i], fidx(i, f, us), 0))],
            out_specs=pl.BlockSpec((tm, d), lambda i, f, be, us, xb: (i, 0)),
            scratch_shapes=[pltpu.VMEM((tm, d), BF16), pltpu.VMEM((tm, d), F32)],
        ),
        out_shape=jax.ShapeDtypeStruct((p, d), F32),
        compiler_params=_params("arbitrary", "arbitrary"),
        name="moe_experts",
    )(blk_e, blk_used, xblk, xs, gs, wg, wu, wd)


def _combine_kernel(slab_ref, ys_hbm, dest_ref, x_ref, g_ref, o_ref, slab_sc, sem):
    i = pl.program_id(0)
    n = pl.num_programs(0)
    slot = i % 2
    tc = x_ref.shape[0]

    def slab_dma(step, sl, op):
        for e in range(N_EXPERTS):
            start = pl.multiple_of(slab_ref[step * N_EXPERTS + e], SUBLANES)
            for size, cond in _slab_sizes(slab_ref[(n + step) * N_EXPERTS + e], tc):
                @pl.when(cond)
                def _():
                    copy = pltpu.make_async_copy(ys_hbm.at[pl.ds(start, size)],
                                                 slab_sc.at[sl, pl.ds(e * tc, size)], sem.at[sl])
                    getattr(copy, op)()

    @pl.when(i == 0)
    def _():
        slab_sc[...] = jnp.zeros_like(slab_sc)
        slab_dma(0, 0, "start")

    @pl.when(i + 1 < n)
    def _():
        slab_dma(i + 1, 1 - slot, "start")

    dest = dest_ref[...]
    pos = lax.broadcasted_iota(jnp.int32, (tc, tc), 1)
    sel = []
    for e in range(N_EXPERTS):
        start = slab_ref[i * N_EXPERTS + e]
        hit = jnp.logical_or(dest[:, 0:1] - start == pos, dest[:, 1:2] - start == pos)
        sel.append(jnp.where(hit, 1.0, 0.0))
    sel = jnp.concatenate(sel, axis=1).astype(BF16)

    slab_dma(i, slot, "wait")
    y = _dot(sel, slab_sc[slot].astype(BF16))
    o_ref[...] = x_ref[...] + _rms(y, g_ref[...])


def _combine(ys, dest, slab, x, g, tc):
    t, d = x.shape
    return pl.pallas_call(
        _combine_kernel,
        grid_spec=pltpu.PrefetchScalarGridSpec(
            num_scalar_prefetch=1,
            grid=(t // tc,),
            in_specs=[pl.BlockSpec(memory_space=pl.ANY),
                      pl.BlockSpec((tc, LANES), lambda i, s: (i, 0)),
                      pl.BlockSpec((tc, d), lambda i, s: (i, 0)),
                      pl.BlockSpec((1, d), lambda i, s: (0, 0))],
            out_specs=pl.BlockSpec((tc, d), lambda i, s: (i, 0)),
            scratch_shapes=[pltpu.VMEM((2, N_EXPERTS * tc, d), F32), pltpu.SemaphoreType.DMA((2,))],
        ),
        out_shape=jax.ShapeDtypeStruct((t, d), F32),
        compiler_params=_params("arbitrary"),
        name="moe_combine",
    )(slab, ys, dest, x, g.reshape(1, d))


def _route(top_e, tm, tc):
    t = top_e.shape[0]
    a = t * TOP_K
    nb = t // tc
    e_flat = top_e.reshape(a)
    onehot = (e_flat[:, None] == jnp.arange(N_EXPERTS, dtype=jnp.int32)[None, :]).astype(jnp.int32)
    csum = jnp.cumsum(onehot, axis=0)
    rank = jnp.sum(onehot * (csum - 1), axis=1)
    upto = csum[TOP_K * tc - 1::TOP_K * tc]
    before = jnp.concatenate([jnp.zeros((1, N_EXPERTS), jnp.int32), upto[:nb - 1]], axis=0)
    rows = (upto - before + SUBLANES - 1) // SUBLANES * SUBLANES
    rows_before = jnp.cumsum(rows, axis=0) - rows
    total = jnp.sum(rows, axis=0)
    region = (total + 2 * tm - 1) // tm * tm
    rends = jnp.cumsum(region)
    rstarts = rends - region
    slab = (rstarts[None, :] + rows_before).astype(jnp.int32)
    shift = jnp.broadcast_to((slab - before)[:, None, :], (nb, TOP_K * tc, N_EXPERTS)).reshape(a, N_EXPERTS)
    dest = (rank + jnp.sum(onehot * shift, axis=1)).astype(jnp.int32).reshape(t, TOP_K)
    slab = jnp.concatenate([slab.reshape(nb * N_EXPERTS), rows.reshape(nb * N_EXPERTS)])
    row_end = rstarts + total
    fill = jnp.concatenate([row_end, rends - tm, rends[-1:] // tm]).astype(jnp.int32)
    nblk = -(-(a + nb * N_EXPERTS * (SUBLANES - 1)) // tm) + 2 * N_EXPERTS
    blk_start = jnp.arange(nblk, dtype=jnp.int32) * tm
    blk_e = jnp.minimum(jnp.sum((rends[None, :] <= blk_start[:, None]).astype(jnp.int32), axis=1),
                        N_EXPERTS - 1)
    blk_used = jnp.logical_and(blk_start < row_end[blk_e], blk_start < rends[-1]).astype(jnp.int32)
    first_used = jnp.argmax(blk_used).astype(jnp.int32)
    xblk = jnp.where(blk_used > 0, jnp.arange(nblk, dtype=jnp.int32), first_used)
    return dest, slab, fill, blk_e, blk_used, xblk, nblk * tm


def _moe(x, g_in, g_out, w_router, wg, wu, wd, tm=512, tc=256):
    t = x.shape[0]
    tc = min(tc, t)
    h, e_pad, gates = _router(x, g_in, w_router)
    dest, slab, fill, blk_e, blk_used, xblk, p_rows = _route(e_pad[:, :TOP_K], tm, tc)
    dest_t = dest.reshape(t // tc, tc, TOP_K).transpose(0, 2, 1)
    dest_pad = jnp.zeros((t, LANES), jnp.int32).at[:, :TOP_K].set(dest)
    xs, gs = _dispatch(h, dest_t, gates, slab, fill, p_rows, tm, tc)
    ys = _moe_experts(xs, gs, blk_e, blk_used, xblk, wg, wu, wd, tm)
    return _combine(ys, dest_pad, slab, x, g_out, tc)


def kernel(x, mem, norm_g, mem_norm_g, w_mem_kv, a_w_in, a_sink, a_w_out, b_w_in, b_lambda,
           b_subln_g, b_w_out, ffn_w_gate, ffn_w_up, ffn_w_down, moe_w_router, moe_w_gate,
           moe_w_up, moe_w_down):
    batch, seq, d = x.shape
    n_mem = mem.shape[1]
    bf = lambda w: w.astype(BF16)
    xt = x.reshape(batch * seq, d)
    mkv = _norm_matmul(mem.reshape(batch * n_mem, d), mem_norm_g, bf(w_mem_kv),
                       jnp.ones((w_mem_kv.shape[1],), F32))

    g = norm_g[0]
    proj = _norm_matmul(xt, g[0], bf(a_w_in[0]), _query_col_scale(a_w_in.shape[2], A_Q_HEADS * HEAD_DIM))
    mix = _window_attention(proj, a_sink[0], batch, seq)
    mem_out = _mem_attention(proj, (A_Q_HEADS + 2 * A_KV_HEADS) * HEAD_DIM // MEM_WIDTH, mkv, batch, seq)
    xt = _out_proj(mix, mem_out, bf(a_w_out[0]), xt, g[1])
    xt = _ffn(xt, g[2], g[3], bf(ffn_w_gate[0]), bf(ffn_w_up[0]), bf(ffn_w_down[0]))

    g = norm_g[1]
    lambda_init = 0.8 - 0.6 * math.exp(-0.3 * 1)
    proj = _norm_matmul(xt, g[0], bf(b_w_in[0]), _query_col_scale(b_w_in.shape[2], MIX_WIDTH))
    mix = _diff_attention(proj, b_lambda[0], b_subln_g[0], lambda_init, batch, seq)
    mem_out = _mem_attention(proj, 3 * MIX_WIDTH // MEM_WIDTH, mkv, batch, seq)
    xt = _out_proj(mix, mem_out, bf(b_w_out[0]), xt, g[1])
    xt = _moe(xt, g[2], g[3], moe_w_router[0], bf(moe_w_gate[0]), bf(moe_w_up[0]), bf(moe_w_down[0]))
    return xt.reshape(batch, seq, d)
```

```python
import functools
import math

import jax
import jax.numpy as jnp
import numpy as np
from jax import lax
from jax.experimental import pallas as pl
from jax.experimental.pallas import tpu as pltpu

D_MODEL = 1024
HEAD_DIM = 64
MIX_WIDTH = 768
MEM_WIDTH = 256
MEM_HEADS = 4
A_Q_HEADS = 12
A_KV_HEADS = 4
A_GROUP = 3
WINDOW = 128
B_HEADS = 6
B_VDIM = 128
D_FF = 3584
N_EXPERTS = 8
TOP_K = 2
EPS = 1e-6
NEG_INF = -1e30
LOG2E = math.log2(math.e)
QK_SCALE = HEAD_DIM ** -0.5 * LOG2E
LANES = 128
SUBLANES = 8
VMEM_LIMIT_BYTES = 48 * 1024 * 1024

BF16 = jnp.bfloat16
F32 = jnp.float32


def _params(*sem):
    return pltpu.CompilerParams(dimension_semantics=sem, vmem_limit_bytes=VMEM_LIMIT_BYTES)


def _rms(x, g):
    return x * lax.rsqrt(jnp.mean(x * x, axis=-1, keepdims=True) + EPS) * g


def _dot(a, b):
    return jnp.dot(a, b, preferred_element_type=F32)


def _dot_nt(a, b):
    return lax.dot_general(a, b, (((1,), (1,)), ((), ())), preferred_element_type=F32)


def _alibi_slopes(n):
    return [2.0 ** (-8.0 * (i + 1) / n) for i in range(n)]


def _norm_matmul_kernel(x_ref, g_ref, w_ref, cs_ref, o_ref, *, chunk):
    h = _rms(x_ref[...], g_ref[...]).astype(BF16)
    for c in range(o_ref.shape[1] // chunk):
        sl = slice(c * chunk, (c + 1) * chunk)
        o_ref[:, sl] = (_dot(h, w_ref[:, sl]) * cs_ref[:, sl]).astype(o_ref.dtype)


def _norm_matmul(x, g, w, col_scale, tm=512, chunk=512):
    t, d = x.shape
    n = w.shape[1]
    tm = min(tm, t)
    return pl.pallas_call(
        functools.partial(_norm_matmul_kernel, chunk=min(chunk, n)),
        grid=(t // tm,),
        in_specs=[pl.BlockSpec((tm, d), lambda i: (i, 0)),
                  pl.BlockSpec((1, d), lambda i: (0, 0)),
                  pl.BlockSpec((d, n), lambda i: (0, 0)),
                  pl.BlockSpec((1, n), lambda i: (0, 0))],
        out_specs=pl.BlockSpec((tm, n), lambda i: (i, 0)),
        out_shape=jax.ShapeDtypeStruct((t, n), BF16),
        compiler_params=_params("parallel"),
        name="norm_matmul",
    )(x, g.reshape(1, d), w, col_scale.reshape(1, n))


def _query_col_scale(n, mix_q_cols):
    cols = np.arange(n)
    return jnp.asarray(np.where((cols < mix_q_cols) | (cols >= n - MEM_WIDTH), QK_SCALE, 1.0), F32)


def _window_kernel(sink_ref, q_ref, kp_ref, kc_ref, kn_ref, vp_ref, vc_ref, vn_ref, bias_ref, o_ref):
    n = pl.program_id(1)
    nb = pl.num_programs(1)
    blk = kp_ref.shape[0]
    sub = q_ref.shape[0] // blk
    k_all = jnp.concatenate([kp_ref[...], kc_ref[...], kn_ref[...]], axis=0)
    v_all = jnp.concatenate([vp_ref[...], vc_ref[...], vn_ref[...]], axis=0)
    col = lax.broadcasted_iota(jnp.int32, (1, 3 * blk), 1)
    for u in range(sub):
        k = k_all[u * blk:(u + 3) * blk]
        v = v_all[u * blk:(u + 3) * blk]
        edge = jnp.zeros((1, 3 * blk), F32)
        if u == 0:
            edge = jnp.where(jnp.logical_and(n == 0, col < blk), NEG_INF, edge)
        if u == sub - 1:
            edge = jnp.where(jnp.logical_and(n == nb - 1, col >= 2 * blk), NEG_INF, edge)
        q = q_ref[u * blk:(u + 1) * blk, :]
        outs = []
        for kh in range(A_KV_HEADS):
            heads = range(kh * A_GROUP, (kh + 1) * A_GROUP)
            qs = jnp.concatenate([q[:, h * HEAD_DIM:(h + 1) * HEAD_DIM] for h in heads], axis=0)
            sink = jnp.concatenate([jnp.full((blk, 1), sink_ref[h] * LOG2E, F32) for h in heads], axis=0)
            s = _dot_nt(qs, k[:, kh * HEAD_DIM:(kh + 1) * HEAD_DIM]) + bias_ref[kh] + edge
            m = jnp.maximum(jnp.max(s, axis=-1, keepdims=True), sink)
            p = jnp.exp2(s - m)
            denom = jnp.sum(p, axis=-1, keepdims=True) + jnp.exp2(sink - m)
            o = _dot(p.astype(BF16), v[:, kh * HEAD_DIM:(kh + 1) * HEAD_DIM]) * (1.0 / denom)
            outs.extend(o[g * blk:(g + 1) * blk] for g in range(A_GROUP))
        o_ref[u * blk:(u + 1) * blk, :] = jnp.concatenate(outs, axis=-1).astype(o_ref.dtype)


def _window_bias(blk):
    qi = np.arange(blk)[:, None]
    kj = np.arange(3 * blk)[None, :]
    dist = np.abs(blk + qi - kj)
    slopes = _alibi_slopes(A_Q_HEADS)
    tables = [np.where(dist <= WINDOW, -(slopes[h] * LOG2E) * dist, NEG_INF) for h in range(A_Q_HEADS)]
    return jnp.asarray(np.stack(tables).reshape(A_KV_HEADS, A_GROUP * blk, 3 * blk), F32)


def _window_attention(proj, sink, batch, seq, sub=2):
    blk = WINDOW
    nb = seq // blk
    sub = min(sub, nb)
    ns = nb // sub
    kcol = MIX_WIDTH // 256
    vcol = kcol + 1
    bias = _window_bias(blk)

    def cur(col):
        return pl.BlockSpec((sub * blk, 256), lambda b, n, s: (b * ns + n, col))

    def prev(col):
        return pl.BlockSpec((blk, 256), lambda b, n, s: (b * nb + jnp.maximum(n * sub - 1, 0), col))

    def nxt(col):
        return pl.BlockSpec((blk, 256), lambda b, n, s: (b * nb + jnp.minimum((n + 1) * sub, nb - 1), col))

    return pl.pallas_call(
        _window_kernel,
        grid_spec=pltpu.PrefetchScalarGridSpec(
            num_scalar_prefetch=1,
            grid=(batch, ns),
            in_specs=[pl.BlockSpec((sub * blk, MIX_WIDTH), lambda b, n, s: (b * ns + n, 0)),
                      prev(kcol), cur(kcol), nxt(kcol), prev(vcol), cur(vcol), nxt(vcol),
                      pl.BlockSpec(bias.shape, lambda b, n, s: (0, 0, 0))],
            out_specs=pl.BlockSpec((sub * blk, MIX_WIDTH), lambda b, n, s: (b * ns + n, 0)),
        ),
        out_shape=jax.ShapeDtypeStruct((batch * seq, MIX_WIDTH), BF16),
        compiler_params=_params("parallel", "parallel"),
        name="window_attention",
    )(sink, proj, proj, proj, proj, proj, proj, proj, bias)


def _mem_attn_kernel(q_ref, mkv_ref, o_ref):
    q = q_ref[...]
    mkv = mkv_ref[...]
    outs = []
    for h in range(MEM_HEADS):
        sl = slice(h * HEAD_DIM, (h + 1) * HEAD_DIM)
        s = _dot_nt(q[:, sl], mkv[:, sl])
        m = jnp.max(s, axis=-1, keepdims=True)
        p = jnp.exp2(s - m)
        denom = jnp.sum(p, axis=-1, keepdims=True)
        o = _dot(p.astype(BF16), mkv[:, MEM_WIDTH + h * HEAD_DIM:MEM_WIDTH + (h + 1) * HEAD_DIM])
        outs.append(o * (1.0 / denom))
    o_ref[...] = jnp.concatenate(outs, axis=-1).astype(o_ref.dtype)


def _mem_attention(proj, qcol, mkv, batch, seq, tm=512):
    tm = min(tm, seq)
    per_batch = seq // tm
    n_mem = mkv.shape[0] // batch
    return pl.pallas_call(
        _mem_attn_kernel,
        grid=(batch * per_batch,),
        in_specs=[pl.BlockSpec((tm, MEM_WIDTH), lambda i: (i, qcol)),
                  pl.BlockSpec((n_mem, 2 * MEM_WIDTH), lambda i: (i // per_batch, 0))],
        out_specs=pl.BlockSpec((tm, MEM_WIDTH), lambda i: (i, 0)),
        out_shape=jax.ShapeDtypeStruct((batch * seq, MEM_WIDTH), BF16),
        compiler_params=_params("parallel"),
        name="mem_attention",
    )(proj, mkv)


def _out_proj_kernel(mix_ref, mem_ref, w1_ref, w2_ref, x_ref, g_ref, o_ref):
    o = _dot(mix_ref[...], w1_ref[...]) + _dot(mem_ref[...], w2_ref[...])
    o_ref[...] = x_ref[...] + _rms(o, g_ref[...])


def _out_proj(mix, mem_out, w_out, x, g, tm=512):
    t, d = x.shape
    tm = min(tm, t)
    w1 = w_out[:MIX_WIDTH]
    w2 = w_out[MIX_WIDTH:]
    return pl.pallas_call(
        _out_proj_kernel,
        grid=(t // tm,),
        in_specs=[pl.BlockSpec((tm, MIX_WIDTH), lambda i: (i, 0)),
                  pl.BlockSpec((tm, MEM_WIDTH), lambda i: (i, 0)),
                  pl.BlockSpec((MIX_WIDTH, d), lambda i: (0, 0)),
                  pl.BlockSpec((MEM_WIDTH, d), lambda i: (0, 0)),
                  pl.BlockSpec((tm, d), lambda i: (i, 0)),
                  pl.BlockSpec((1, d), lambda i: (0, 0))],
        out_specs=pl.BlockSpec((tm, d), lambda i: (i, 0)),
        out_shape=jax.ShapeDtypeStruct((t, d), F32),
        compiler_params=_params("parallel"),
        name="out_proj",
    )(mix, mem_out, w1, w2, x, g.reshape(1, d))


def _swiglu_tile(h, wg_ref, wu_ref, wd_ref, chunk=256):
    out = None
    for c in range(wg_ref.shape[1] // chunk):
        sl = slice(c * chunk, (c + 1) * chunk)
        a = _dot(h, wg_ref[:, sl])
        u = _dot(h, wu_ref[:, sl])
        z = (a * jax.nn.sigmoid(a) * u).astype(BF16)
        y = _dot(z, wd_ref[sl, :])
        out = y if out is None else out + y
    return out


def _ffn_kernel(x_ref, gin_ref, gout_ref, wg_ref, wu_ref, wd_ref, o_ref, h_sc, acc_sc):
    f = pl.program_id(1)

    @pl.when(f == 0)
    def _():
        h_sc[...] = _rms(x_ref[...], gin_ref[...]).astype(BF16)
        acc_sc[...] = jnp.zeros_like(acc_sc)

    acc_sc[...] += _swiglu_tile(h_sc[...], wg_ref, wu_ref, wd_ref)

    @pl.when(f == pl.num_programs(1) - 1)
    def _():
        o_ref[...] = x_ref[...] + _rms(acc_sc[...], gout_ref[...])


def _ffn(x, g_in, g_out, wg, wu, wd, tm=512, tf=1792):
    t, d = x.shape
    ff = wg.shape[1]
    tm = min(tm, t)
    return pl.pallas_call(
        _ffn_kernel,
        grid=(t // tm, ff // tf),
        in_specs=[pl.BlockSpec((tm, d), lambda i, f: (i, 0)),
                  pl.BlockSpec((1, d), lambda i, f: (0, 0)),
                  pl.BlockSpec((1, d), lambda i, f: (0, 0)),
                  pl.BlockSpec((d, tf), lambda i, f: (0, f)),
                  pl.BlockSpec((d, tf), lambda i, f: (0, f)),
                  pl.BlockSpec((tf, d), lambda i, f: (f, 0))],
        out_specs=pl.BlockSpec((tm, d), lambda i, f: (i, 0)),
        out_shape=jax.ShapeDtypeStruct((t, d), F32),
        scratch_shapes=[pltpu.VMEM((tm, d), BF16), pltpu.VMEM((tm, d), F32)],
        compiler_params=_params("parallel", "arbitrary"),
        name="ffn",
    )(x, g_in.reshape(1, d), g_out.reshape(1, d), wg, wu, wd)


_POS_SPLIT = 256
_LEFT, _DIAG, _RIGHT = 0, 1, 2
_BOUND_LANE = 12
_CHUNK_LANE = 15
_OFFSET_LANE = 18
_ZERO_EXP2_ARG = 136.0
_MIN_DENOM = 2.0 ** -60


def _bf16_split(x):
    hi = x.astype(BF16)
    r = x - hi.astype(F32)
    mid = r.astype(BF16)
    lo = (r - mid.astype(F32)).astype(BF16)
    return hi, mid, lo


def _diff_attn_kernel(reach_ref, slope_ref, q_ref, k_ref, v_ref, qf_ref, kf_ref, dist_ref, lam_ref,
                      g_ref, o_ref, qa_sc, ka_sc, va_sc, k2_sc, m_sc, acc_sc, sa_sc, pa_sc, pb_sc, *, tk,
                      out_scale, lambda_init):
    h = pl.program_id(1)
    i = pl.program_id(2)
    tq = q_ref.shape[0]
    seq = k_ref.shape[0]
    nk = seq // tk
    slope = slope_ref[h]
    reach = reach_ref[h]

    @pl.when(i == 0)
    def _():
        k = k_ref[...]
        ka_sc[:, :B_VDIM] = k
        va_sc[:, :B_VDIM] = v_ref[...]
        lane = lax.broadcasted_iota(jnp.int32, (seq, LANES), 1)
        va_sc[:, B_VDIM:] = jnp.where(lane == 0, 1.0, 0.0).astype(BF16)
        kf = kf_ref[...]
        flane = lax.broadcasted_iota(jnp.int32, kf.shape, 1)
        chunk_lanes = jnp.logical_and(flane >= _CHUNK_LANE, flane < _CHUNK_LANE + 3)
        for c in range(nk):
            ka_sc[c * tk:(c + 1) * tk, B_VDIM:] = jnp.where(chunk_lanes, float(c), kf).astype(BF16)
        kk = k.astype(F32)
        kk = kk * kk
        for c, sel in enumerate((lane < HEAD_DIM, lane >= HEAD_DIM)):
            n2 = jnp.sum(jnp.where(sel, kk, 0.0), axis=-1, keepdims=True)
            k2_sc[c] = jnp.broadcast_to(jnp.max(n2, axis=0, keepdims=True), k2_sc.shape[1:])

    q = q_ref[...]
    lane = lax.broadcasted_iota(jnp.int32, q.shape, 1)
    zero = jnp.zeros_like(q)
    q0 = jnp.where(lane < HEAD_DIM, q, zero)
    q1 = jnp.where(lane >= HEAD_DIM, q, zero)
    qq = q.astype(F32)
    qq = qq * qq
    n0 = jnp.sum(jnp.where(lane < HEAD_DIM, qq, 0.0), axis=-1, keepdims=True)
    n1 = jnp.sum(jnp.where(lane >= HEAD_DIM, qq, 0.0), axis=-1, keepdims=True)
    bound = jnp.concatenate([jnp.sqrt(n0) * jnp.sqrt(k2_sc[0][0:1, 0:1]),
                             jnp.sqrt(n1) * jnp.sqrt(k2_sc[1][0:1, 0:1])], axis=0)
    off = slope * (i * tk).astype(F32)
    qf = qf_ref[...]
    qf2 = jnp.concatenate([qf, qf], axis=0)

    def piece_lanes(x, first_lane):
        lane_ = lax.broadcasted_iota(jnp.int32, x.shape, 1)
        out = jnp.zeros_like(x)
        for n, piece in enumerate(_bf16_split(x)):
            out = jnp.where(lane_ == first_lane + n, piece.astype(F32), out)
        return out.astype(BF16)

    def build_queries(with_bound):
        feats = {_LEFT: -qf2, _DIAG: jnp.zeros_like(qf2), _RIGHT: qf2}
        if with_bound:
            base = piece_lanes(jnp.broadcast_to(bound, (2 * tq, LANES)), _BOUND_LANE)
            side = piece_lanes(jnp.full((SUBLANES, LANES), off, F32), _OFFSET_LANE)[0:1]
            feats = {_LEFT: base + side - qf2, _DIAG: base, _RIGHT: base - side + qf2}
        for variant, feat in feats.items():
            qa_sc[variant, :, B_VDIM:] = feat
            qa_sc[variant, :tq, :B_VDIM] = q0
            qa_sc[variant, tq:, :B_VDIM] = q1

    def scores(j, s_ref):
        variant = jnp.where(j < i, _LEFT, jnp.where(j == i, _DIAG, _RIGHT))
        start = pl.multiple_of(j * tk, tk)
        s_ref[...] = _dot_nt(qa_sc[variant], ka_sc[pl.ds(start, tk), :])

    def diag_bias(j, s_ref):
        @pl.when(j == i)
        def _():
            bias = slope * dist_ref[...]
            s_ref[:tq] -= bias
            s_ref[tq:] -= bias

    def probs(j, p_ref):
        start = pl.multiple_of(j * tk, tk)
        s = _dot_nt(qa_sc[jnp.where(j < i, _LEFT, _RIGHT)], ka_sc[pl.ds(start, tk), :])
        p_ref[...] = jnp.exp2(s).astype(BF16)

    def values(j):
        return va_sc[pl.ds(pl.multiple_of(j * tk, tk), tk), :]

    build_queries(True)
    lo = jnp.maximum(i - reach, 0)
    total = jnp.minimum(i + reach + 1, nk) - lo

    def visited(n):
        m = jnp.minimum(n, total - 1)
        j = lo + m - 1
        return jnp.where(m == 0, i, j + (j >= i).astype(jnp.int32))

    s_diag = _dot_nt(qa_sc[_DIAG], ka_sc[pl.ds(pl.multiple_of(i * tk, tk), tk), :])
    bias = slope * dist_ref[...]
    pa_sc[...] = jnp.exp2(s_diag - jnp.concatenate([bias, bias], axis=0)).astype(BF16)
    acc_sc[...] = jnp.zeros_like(acc_sc)

    def pair(tt, carry):
        n = 2 * tt
        probs(visited(n + 1), pb_sc)
        acc_sc[...] += _dot(pa_sc[...], values(visited(n)))
        probs(visited(n + 2), pa_sc)
        acc_sc[...] += _dot(pb_sc[...], values(visited(n + 1)))
        return carry

    lax.fori_loop(0, total // 2, pair, 0)

    @pl.when(total % 2 == 1)
    def _():
        acc_sc[...] += _dot(pa_sc[...], values(visited(total - 1)))

    denom_min = jnp.min(acc_sc[:, B_VDIM:B_VDIM + 1])

    @pl.when(jnp.logical_not(denom_min >= _MIN_DENOM))
    def _():
        build_queries(False)
        m_sc[...] = jnp.full_like(m_sc, NEG_INF)
        acc_sc[...] = jnp.zeros_like(acc_sc)

        def chunk(j, carry):
            scores(j, sa_sc)
            diag_bias(j, sa_sc)
            adj = jnp.where(j > i, off, jnp.where(j < i, -off, 0.0))
            start = pl.multiple_of(j * tk, tk)
            s = sa_sc[...]
            m_prev = m_sc[...]
            m_new = jnp.maximum(m_prev, jnp.max(s, axis=-1, keepdims=True) + adj)
            p = jnp.exp2(s - (m_new - adj)).astype(BF16)
            acc_sc[...] = jnp.exp2(m_prev - m_new) * acc_sc[...] + _dot(p, va_sc[pl.ds(start, tk), :])
            m_sc[...] = m_new
            return carry

        lax.fori_loop(0, nk, chunk, 0)

    lp = lam_ref[...]
    lam = (jnp.exp(jnp.sum(lp[0:1] * lp[1:2], axis=-1, keepdims=True))
           - jnp.exp(jnp.sum(lp[2:3] * lp[3:4], axis=-1, keepdims=True)) + lambda_init)
    acc = acc_sc[...]
    o0 = acc[:tq, :B_VDIM] * (1.0 / acc[:tq, B_VDIM:B_VDIM + 1])
    o1 = acc[tq:, :B_VDIM] * (1.0 / acc[tq:, B_VDIM:B_VDIM + 1])
    o_ref[...] = (_rms(o0 - lam * o1, g_ref[...]) * out_scale).astype(o_ref.dtype)


def _bf16_pieces(x, n=3):
    out = []
    r = np.float64(x)
    for _ in range(n):
        p = np.float64(np.float32(r).astype(jnp.bfloat16).astype(np.float32))
        out.append(p)
        r = r - p
    return out


def _alibi_features(slopes2, t):
    pos = np.arange(t)
    hi = (pos // _POS_SPLIT) * _POS_SPLIT
    lo = pos % _POS_SPLIT
    qf = np.zeros((len(slopes2), t, LANES), np.float32)
    kf = np.zeros((len(slopes2), t, LANES), np.float32)
    for h, s in enumerate(slopes2):
        for n, piece in enumerate(_bf16_pieces(s)):
            for base, part in ((0, hi), (3, lo)):
                qf[h, :, base + n] = -piece
                kf[h, :, base + n] = part
                qf[h, :, 6 + base + n] = part
                kf[h, :, 6 + base + n] = piece
        for n, piece in enumerate(_bf16_pieces(s * t)):
            qf[h, :, _CHUNK_LANE + n] = -piece
        kf[h, :, _BOUND_LANE:_BOUND_LANE + 3] = -1.0
        kf[h, :, _OFFSET_LANE:_OFFSET_LANE + 3] = -1.0
    return jnp.asarray(qf, BF16), jnp.asarray(kf, BF16)


def _diff_attention(proj, b_lambda, subln_g, lambda_init, batch, seq, t=512):
    t = min(t, seq // 2)
    nq = seq // t
    assert seq % (2 * t) == 0
    kcol = MIX_WIDTH // B_VDIM
    vcol = 2 * kcol
    slopes2 = [s * LOG2E for s in _alibi_slopes(B_HEADS)]
    reach = [min(nq, int(math.floor((_ZERO_EXP2_ARG / s - 1.0) / t)) + 1) for s in slopes2]
    qf, kf = _alibi_features(slopes2, t)
    pos = np.arange(t)
    dist = jnp.asarray(np.abs(pos[:, None] - pos[None, :]), F32)
    kernel = functools.partial(_diff_attn_kernel, tk=t, out_scale=1.0 - lambda_init,
                               lambda_init=lambda_init)
    return pl.pallas_call(
        kernel,
        grid_spec=pltpu.PrefetchScalarGridSpec(
            num_scalar_prefetch=2,
            grid=(batch, B_HEADS, nq),
            in_specs=[pl.BlockSpec((t, B_VDIM), lambda b, h, i, r, s: (b * nq + i, h)),
                      pl.BlockSpec((seq, B_VDIM), lambda b, h, i, r, s: (b, kcol + h)),
                      pl.BlockSpec((seq, B_VDIM), lambda b, h, i, r, s: (b, vcol + h)),
                      pl.BlockSpec((None, t, LANES), lambda b, h, i, r, s: (h, 0, 0)),
                      pl.BlockSpec((None, t, LANES), lambda b, h, i, r, s: (h, 0, 0)),
                      pl.BlockSpec((t, t), lambda b, h, i, r, s: (0, 0)),
                      pl.BlockSpec((4, HEAD_DIM), lambda b, h, i, r, s: (0, 0)),
                      pl.BlockSpec((1, B_VDIM), lambda b, h, i, r, s: (0, 0))],
            out_specs=pl.BlockSpec((t, B_VDIM), lambda b, h, i, r, s: (b * nq + i, h)),
            scratch_shapes=[pltpu.VMEM((3, 2 * t, 2 * LANES), BF16),
                            pltpu.VMEM((seq, 2 * LANES), BF16),
                            pltpu.VMEM((seq, 2 * LANES), BF16),
                            pltpu.VMEM((2, 8, LANES), F32),
                            pltpu.VMEM((2 * t, 1), F32),
                            pltpu.VMEM((2 * t, 2 * LANES), F32),
                            pltpu.VMEM((2 * t, t), F32),
                            pltpu.VMEM((2 * t, t), BF16),
                            pltpu.VMEM((2 * t, t), BF16)],
        ),
        out_shape=jax.ShapeDtypeStruct((batch * seq, MIX_WIDTH), BF16),
        compiler_params=_params("arbitrary", "arbitrary", "arbitrary"),
        name="diff_attention",
    )(jnp.asarray(reach, jnp.int32), jnp.asarray(slopes2, F32), proj, proj, proj, qf, kf, dist,
      b_lambda, subln_g.reshape(1, B_VDIM))


def _router_kernel(x_ref, g_ref, whi_ref, wlo_ref, h_ref, e_ref, gate_ref):
    h = _rms(x_ref[...], g_ref[...])
    h_hi = h.astype(BF16)
    h_lo = (h - h_hi.astype(F32)).astype(BF16)
    whi = whi_ref[...]
    logits = _dot(h_hi, whi) + (_dot(h_hi, wlo_ref[...]) + _dot(h_lo, whi))
    lane = lax.broadcasted_iota(jnp.int32, logits.shape, 1)
    logits = jnp.where(lane < N_EXPERTS, logits, -jnp.inf)
    m1 = jnp.max(logits, axis=-1, keepdims=True)
    i1 = jnp.min(jnp.where(logits == m1, lane, LANES), axis=-1, keepdims=True)
    rest = jnp.where(lane == i1, -jnp.inf, logits)
    m2 = jnp.max(rest, axis=-1, keepdims=True)
    i2 = jnp.min(jnp.where(rest == m2, lane, LANES), axis=-1, keepdims=True)
    r = jnp.exp(m2 - m1)
    g1 = 1.0 / (1.0 + r)
    g2 = r * g1
    e_ref[...] = jnp.where(lane == 0, i1, jnp.where(lane == 1, i2, 0))
    gate_ref[...] = jnp.where(lane == 0, g1, jnp.where(lane == 1, g2, 0.0))
    h_ref[...] = h_hi


def _router(x, g, w_router, tm=512):
    t, d = x.shape
    tm = min(tm, t)
    wpad = jnp.zeros((d, LANES), F32).at[:, :N_EXPERTS].set(w_router)
    whi = wpad.astype(BF16)
    wlo = (wpad - whi.astype(F32)).astype(BF16)
    return pl.pallas_call(
        _router_kernel,
        grid=(t // tm,),
        in_specs=[pl.BlockSpec((tm, d), lambda i: (i, 0)),
                  pl.BlockSpec((1, d), lambda i: (0, 0)),
                  pl.BlockSpec((d, LANES), lambda i: (0, 0)),
                  pl.BlockSpec((d, LANES), lambda i: (0, 0))],
        out_specs=[pl.BlockSpec((tm, d), lambda i: (i, 0)),
                   pl.BlockSpec((tm, LANES), lambda i: (i, 0)),
                   pl.BlockSpec((tm, LANES), lambda i: (i, 0))],
        out_shape=[jax.ShapeDtypeStruct((t, d), BF16),
                   jax.ShapeDtypeStruct((t, LANES), jnp.int32),
                   jax.ShapeDtypeStruct((t, LANES), F32)],
        compiler_params=_params("parallel"),
        name="router",
    )(x, g.reshape(1, d), whi, wlo)


def _slab_sizes(rows, tc):
    quarter, half = tc // 4, tc // 2
    return ((quarter, rows <= quarter),
            (half, jnp.logical_and(rows > quarter, rows <= half)),
            (tc, rows > half))


def _dispatch_kernel(slab_ref, fill_ref, h_ref, dest_ref, gate_ref, xs_hbm, gs_hbm,
                     xslab_sc, gslab_sc, zx_sc, zg_sc, sem, zsem):
    i = pl.program_id(0)
    n = pl.num_programs(0)
    slot = i % 2
    tc = h_ref.shape[0]
    tm = zx_sc.shape[0]

    def slab_dma(step, sl, op):
        for e in range(N_EXPERTS):
            start = pl.multiple_of(slab_ref[step * N_EXPERTS + e], SUBLANES)
            for size, cond in _slab_sizes(slab_ref[(n + step) * N_EXPERTS + e], tc):
                @pl.when(cond)
                def _():
                    for src, dst in ((xslab_sc, xs_hbm), (gslab_sc, gs_hbm)):
                        copy = pltpu.make_async_copy(src.at[sl, pl.ds(e * tc, size)],
                                                     dst.at[pl.ds(start, size)], sem.at[sl])
                        getattr(copy, op)()

    def fill_copies(start):
        start = pl.multiple_of(start, SUBLANES)
        return [pltpu.make_async_copy(zx_sc, xs_hbm.at[pl.ds(start, tm)], zsem),
                pltpu.make_async_copy(zg_sc, gs_hbm.at[pl.ds(start, tm)], zsem)]

    @pl.when(i == 0)
    def _():
        zx_sc[...] = jnp.zeros_like(zx_sc)
        zg_sc[...] = jnp.zeros_like(zg_sc)
        for half in range(2):
            for e in range(N_EXPERTS):
                for c in fill_copies(fill_ref[half * N_EXPERTS + e]):
                    c.start()
            for e in range(N_EXPERTS):
                for c in fill_copies(fill_ref[half * N_EXPERTS + e]):
                    c.wait()
        first_free = fill_ref[2 * N_EXPERTS]
        n_tiles = xs_hbm.shape[0] // tm

        def start_tile(b, carry):
            for c in fill_copies(b * tm):
                c.start()
            return carry

        def wait_tile(b, carry):
            for c in fill_copies(b * tm):
                c.wait()
            return carry

        lax.fori_loop(first_free, n_tiles, start_tile, 0)
        lax.fori_loop(first_free, n_tiles, wait_tile, 0)

    dest = dest_ref[0]
    pos = lax.broadcasted_iota(jnp.int32, (tc, tc), 0)
    sel0, sel1 = [], []
    for e in range(N_EXPERTS):
        start = slab_ref[i * N_EXPERTS + e]
        sel0.append(jnp.where(dest[0:1, :] - start == pos, 1.0, 0.0))
        sel1.append(jnp.where(dest[1:2, :] - start == pos, 1.0, 0.0))
    sel0 = jnp.concatenate(sel0, axis=0).astype(BF16)
    sel1 = jnp.concatenate(sel1, axis=0).astype(BF16)
    gate = gate_ref[...]
    lane = lax.broadcasted_iota(jnp.int32, gate.shape, 1)

    def gate_pieces(col):
        out = jnp.zeros(gate.shape, F32)
        for n_, piece in enumerate(_bf16_split(col)):
            out = jnp.where(lane == n_, piece.astype(F32), out)
        return out.astype(BF16)

    xslab_sc[slot] = _dot(sel0 + sel1, h_ref[...])
    gslab_sc[slot] = _dot(sel0, gate_pieces(gate[:, 0:1])) + _dot(sel1, gate_pieces(gate[:, 1:2]))

    @pl.when(i > 0)
    def _():
        slab_dma(i - 1, 1 - slot, "wait")

    slab_dma(i, slot, "start")

    @pl.when(i == n - 1)
    def _():
        slab_dma(i, slot, "wait")


def _dispatch(h, dest_t, gates, slab, fill, p_rows, tm, tc):
    t, d = h.shape
    nb = t // tc
    return pl.pallas_call(
        _dispatch_kernel,
        grid_spec=pltpu.PrefetchScalarGridSpec(
            num_scalar_prefetch=2,
            grid=(nb,),
            in_specs=[pl.BlockSpec((tc, d), lambda i, s, z: (i, 0)),
                      pl.BlockSpec((1, TOP_K, tc), lambda i, s, z: (i, 0, 0)),
                      pl.BlockSpec((tc, LANES), lambda i, s, z: (i, 0))],
            out_specs=[pl.BlockSpec(memory_space=pl.ANY), pl.BlockSpec(memory_space=pl.ANY)],
            scratch_shapes=[pltpu.VMEM((2, N_EXPERTS * tc, d), F32),
                            pltpu.VMEM((2, N_EXPERTS * tc, LANES), F32),
                            pltpu.VMEM((tm, d), F32), pltpu.VMEM((tm, LANES), F32),
                            pltpu.SemaphoreType.DMA((2,)), pltpu.SemaphoreType.DMA(())],
        ),
        out_shape=[jax.ShapeDtypeStruct((p_rows, d), F32), jax.ShapeDtypeStruct((p_rows, LANES), F32)],
        compiler_params=_params("arbitrary"),
        name="moe_dispatch",
    )(slab, fill, h, dest_t, gates)


def _moe_kernel(blk_e_ref, used_ref, xblk_ref, x_ref, gs_ref, wg_ref, wu_ref, wd_ref, o_ref,
                xb_sc, acc_sc):
    i = pl.program_id(0)
    f = pl.program_id(1)
    used = used_ref[i] > 0

    @pl.when(jnp.logical_and(used, f == 0))
    def _():
        xb_sc[...] = x_ref[...].astype(BF16)
        acc_sc[...] = jnp.zeros_like(acc_sc)

    @pl.when(used)
    def _():
        acc_sc[...] += _swiglu_tile(xb_sc[...], wg_ref, wu_ref, wd_ref)

    @pl.when(f == pl.num_programs(1) - 1)
    def _():
        gs = gs_ref[...]
        gate = gs[:, 0:1] + gs[:, 1:2] + gs[:, 2:3]
        o_ref[...] = jnp.where(used, acc_sc[...] * gate, 0.0)


def _moe_experts(xs, gs, blk_e, blk_used, xblk, wg, wu, wd, tm, tf=1792):
    p, d = xs.shape
    nblk = p // tm
    nf = wg.shape[2] // tf

    def fidx(i, f, used_ref):
        return jnp.where(used_ref[i] > 0, f, nf - 1)

    return pl.pallas_call(
        _moe_kernel,
        grid_spec=pltpu.PrefetchScalarGridSpec(
            num_scalar_prefetch=3,
            grid=(nblk, nf),
            in_specs=[pl.BlockSpec((tm, d), lambda i, f, be, us, xb: (xb[i], 0)),
                      pl.BlockSpec((tm, LANES), lambda i, f, be, us, xb: (xb[i], 0)),
                      pl.BlockSpec((None, d, tf), lambda i, f, be, us, xb: (be[i], 0, fidx(i, f, us))),
                      pl.BlockSpec((None, d, tf), lambda i, f, be, us, xb: (be[i], 0, fidx(i, f, us))),
                      pl.BlockSpec((None, tf, d), lambda i, f, be, us, xb: (be[i], fidx(i, f, us), 0))],
            out_specs=pl.BlockSpec((tm, d), lambda i, f, be, us, xb: (i, 0)),
            scratch_shapes=[pltpu.VMEM((tm, d), BF16), pltpu.VMEM((tm, d), F32)],
        ),
        out_shape=jax.ShapeDtypeStruct((p, d), F32),
        compiler_params=_params("arbitrary", "arbitrary"),
        name="moe_experts",
    )(blk_e, blk_used, xblk, xs, gs, wg, wu, wd)


def _combine_kernel(slab_ref, ys_hbm, dest_ref, x_ref, g_ref, o_ref, slab_sc, sem):
    i = pl.program_id(0)
    n = pl.num_programs(0)
    slot = i % 2
    tc = x_ref.shape[0]

    def slab_dma(step, sl, op):
        for e in range(N_EXPERTS):
            start = pl.multiple_of(slab_ref[step * N_EXPERTS + e], SUBLANES)
            for size, cond in _slab_sizes(slab_ref[(n + step) * N_EXPERTS + e], tc):
                @pl.when(cond)
                def _():
                    copy = pltpu.make_async_copy(ys_hbm.at[pl.ds(start, size)],
                                                 slab_sc.at[sl, pl.ds(e * tc, size)], sem.at[sl])
                    getattr(copy, op)()

    @pl.when(i == 0)
    def _():
        slab_sc[...] = jnp.zeros_like(slab_sc)
        slab_dma(0, 0, "start")

    @pl.when(i + 1 < n)
    def _():
        slab_dma(i + 1, 1 - slot, "start")

    dest = dest_ref[...]
    pos = lax.broadcasted_iota(jnp.int32, (tc, tc), 1)
    sel = []
    for e in range(N_EXPERTS):
        start = slab_ref[i * N_EXPERTS + e]
        hit = jnp.logical_or(dest[:, 0:1] - start == pos, dest[:, 1:2] - start == pos)
        sel.append(jnp.where(hit, 1.0, 0.0))
    sel = jnp.concatenate(sel, axis=1).astype(BF16)

    slab_dma(i, slot, "wait")
    y = _dot(sel, slab_sc[slot].astype(BF16))
    o_ref[...] = x_ref[...] + _rms(y, g_ref[...])


def _combine(ys, dest, slab, x, g, tc):
    t, d = x.shape
    return pl.pallas_call(
        _combine_kernel,
        grid_spec=pltpu.PrefetchScalarGridSpec(
            num_scalar_prefetch=1,
            grid=(t // tc,),
            in_specs=[pl.BlockSpec(memory_space=pl.ANY),
                      pl.BlockSpec((tc, LANES), lambda i, s: (i, 0)),
                      pl.BlockSpec((tc, d), lambda i, s: (i, 0)),
                      pl.BlockSpec((1, d), lambda i, s: (0, 0))],
            out_specs=pl.BlockSpec((tc, d), lambda i, s: (i, 0)),
            scratch_shapes=[pltpu.VMEM((2, N_EXPERTS * tc, d), F32), pltpu.SemaphoreType.DMA((2,))],
        ),
        out_shape=jax.ShapeDtypeStruct((t, d), F32),
        compiler_params=_params("arbitrary"),
        name="moe_combine",
    )(slab, ys, dest, x, g.reshape(1, d))


def _route(top_e, tm, tc):
    t = top_e.shape[0]
    a = t * TOP_K
    nb = t // tc
    e_flat = top_e.reshape(a)
    onehot = (e_flat[:, None] == jnp.arange(N_EXPERTS, dtype=jnp.int32)[None, :]).astype(jnp.int32)
    csum = jnp.cumsum(onehot, axis=0)
    rank = jnp.sum(onehot * (csum - 1), axis=1)
    upto = csum[TOP_K * tc - 1::TOP_K * tc]
    before = jnp.concatenate([jnp.zeros((1, N_EXPERTS), jnp.int32), upto[:nb - 1]], axis=0)
    rows = (upto - before + SUBLANES - 1) // SUBLANES * SUBLANES
    rows_before = jnp.cumsum(rows, axis=0) - rows
    total = jnp.sum(rows, axis=0)
    region = (total + 2 * tm - 1) // tm * tm
    rends = jnp.cumsum(region)
    rstarts = rends - region
    slab = (rstarts[None, :] + rows_before).astype(jnp.int32)
    shift = jnp.broadcast_to((slab - before)[:, None, :], (nb, TOP_K * tc, N_EXPERTS)).reshape(a, N_EXPERTS)
    dest = (rank + jnp.sum(onehot * shift, axis=1)).astype(jnp.int32).reshape(t, TOP_K)
    slab = jnp.concatenate([slab.reshape(nb * N_EXPERTS), rows.reshape(nb * N_EXPERTS)])
    row_end = rstarts + total
    fill = jnp.concatenate([row_end, rends - tm, rends[-1:] // tm]).astype(jnp.int32)
    nblk = -(-(a + nb * N_EXPERTS * (SUBLANES - 1)) // tm) + 2 * N_EXPERTS
    blk_start = jnp.arange(nblk, dtype=jnp.int32) * tm
    blk_e = jnp.minimum(jnp.sum((rends[None, :] <= blk_start[:, None]).astype(jnp.int32), axis=1),
                        N_EXPERTS - 1)
    blk_used = jnp.logical_and(blk_start < row_end[blk_e], blk_start < rends[-1]).astype(jnp.int32)
    first_used = jnp.argmax(blk_used).astype(jnp.int32)
    xblk = jnp.where(blk_used > 0, jnp.arange(nblk, dtype=jnp.int32), first_used)
    return dest, slab, fill, blk_e, blk_used, xblk, nblk * tm


def _moe(x, g_in, g_out, w_router, wg, wu, wd, tm=512, tc=256):
    t = x.shape[0]
    tc = min(tc, t)
    h, e_pad, gates = _router(x, g_in, w_router)
    dest, slab, fill, blk_e, blk_used, xblk, p_rows = _route(e_pad[:, :TOP_K], tm, tc)
    dest_t = dest.reshape(t // tc, tc, TOP_K).transpose(0, 2, 1)
    dest_pad = jnp.zeros((t, LANES), jnp.int32).at[:, :TOP_K].set(dest)
    xs, gs = _dispatch(h, dest_t, gates, slab, fill, p_rows, tm, tc)
    ys = _moe_experts(xs, gs, blk_e, blk_used, xblk, wg, wu, wd, tm)
    return _combine(ys, dest_pad, slab, x, g_out, tc)


def kernel(x, mem, norm_g, mem_norm_g, w_mem_kv, a_w_in, a_sink, a_w_out, b_w_in, b_lambda,
           b_subln_g, b_w_out, ffn_w_gate, ffn_w_up, ffn_w_down, moe_w_router, moe_w_gate,
           moe_w_up, moe_w_down):
    batch, seq, d = x.shape
    n_mem = mem.shape[1]
    bf = lambda w: w.astype(BF16)
    xt = x.reshape(batch * seq, d)
    mkv = _norm_matmul(mem.reshape(batch * n_mem, d), mem_norm_g, bf(w_mem_kv),
                       jnp.ones((w_mem_kv.shape[1],), F32))

    g = norm_g[0]
    proj = _norm_matmul(xt, g[0], bf(a_w_in[0]), _query_col_scale(a_w_in.shape[2], A_Q_HEADS * HEAD_DIM))
    mix = _window_attention(proj, a_sink[0], batch, seq)
    mem_out = _mem_attention(proj, (A_Q_HEADS + 2 * A_KV_HEADS) * HEAD_DIM // MEM_WIDTH, mkv, batch, seq)
    xt = _out_proj(mix, mem_out, bf(a_w_out[0]), xt, g[1])
    xt = _ffn(xt, g[2], g[3], bf(ffn_w_gate[0]), bf(ffn_w_up[0]), bf(ffn_w_down[0]))

    g = norm_g[1]
    lambda_init = 0.8 - 0.6 * math.exp(-0.3 * 1)
    proj = _norm_matmul(xt, g[0], bf(b_w_in[0]), _query_col_scale(b_w_in.shape[2], MIX_WIDTH))
    mix = _diff_attention(proj, b_lambda[0], b_subln_g[0], lambda_init, batch, seq)
    mem_out = _mem_attention(proj, 3 * MIX_WIDTH // MEM_WIDTH, mkv, batch, seq)
    xt = _out_proj(mix, mem_out, bf(b_w_out[0]), xt, g[1])
    xt = _moe(xt, g[2], g[3], moe_w_router[0], bf(moe_w_gate[0]), bf(moe_w_up[0]), bf(moe_w_down[0]))
    return xt.reshape(batch, seq, d)
```

```python
import functools
import math

import jax
import jax.numpy as jnp
import numpy as np
from jax import lax
from jax.experimental import pallas as pl
from jax.experimental.pallas import tpu as pltpu

D_MODEL = 1024
HEAD_DIM = 64
MIX_WIDTH = 768
MEM_WIDTH = 256
MEM_HEADS = 4
A_Q_HEADS = 12
A_KV_HEADS = 4
A_GROUP = 3
WINDOW = 128
B_HEADS = 6
B_VDIM = 128
D_FF = 3584
N_EXPERTS = 8
TOP_K = 2
EPS = 1e-6
NEG_INF = -1e30
LOG2E = math.log2(math.e)
QK_SCALE = HEAD_DIM ** -0.5 * LOG2E
LANES = 128
SUBLANES = 8
VMEM_LIMIT_BYTES = 48 * 1024 * 1024

BF16 = jnp.bfloat16
F32 = jnp.float32


def _params(*sem):
    return pltpu.CompilerParams(dimension_semantics=sem, vmem_limit_bytes=VMEM_LIMIT_BYTES)


def _rms(x, g):
    return x * lax.rsqrt(jnp.mean(x * x, axis=-1, keepdims=True) + EPS) * g


def _dot(a, b):
    return jnp.dot(a, b, preferred_element_type=F32)


def _dot_nt(a, b):
    return lax.dot_general(a, b, (((1,), (1,)), ((), ())), preferred_element_type=F32)


def _alibi_slopes(n):
    return [2.0 ** (-8.0 * (i + 1) / n) for i in range(n)]


def _norm_matmul_kernel(x_ref, g_ref, w_ref, cs_ref, o_ref, *, chunk):
    h = _rms(x_ref[...], g_ref[...]).astype(BF16)
    for c in range(o_ref.shape[1] // chunk):
        sl = slice(c * chunk, (c + 1) * chunk)
        o_ref[:, sl] = (_dot(h, w_ref[:, sl]) * cs_ref[:, sl]).astype(o_ref.dtype)


def _norm_matmul(x, g, w, col_scale, tm=512, chunk=512):
    t, d = x.shape
    n = w.shape[1]
    tm = min(tm, t)
    return pl.pallas_call(
        functools.partial(_norm_matmul_kernel, chunk=min(chunk, n)),
        grid=(t // tm,),
        in_specs=[pl.BlockSpec((tm, d), lambda i: (i, 0)),
                  pl.BlockSpec((1, d), lambda i: (0, 0)),
                  pl.BlockSpec((d, n), lambda i: (0, 0)),
                  pl.BlockSpec((1, n), lambda i: (0, 0))],
        out_specs=pl.BlockSpec((tm, n), lambda i: (i, 0)),
        out_shape=jax.ShapeDtypeStruct((t, n), BF16),
        compiler_params=_params("parallel"),
        name="norm_matmul",
    )(x, g.reshape(1, d), w, col_scale.reshape(1, n))


def _query_col_scale(n, mix_q_cols):
    cols = np.arange(n)
    return jnp.asarray(np.where((cols < mix_q_cols) | (cols >= n - MEM_WIDTH), QK_SCALE, 1.0), F32)


def _window_kernel(q_ref, kp_ref, kc_ref, kn_ref, vp_ref, vc_ref, vn_ref, bias_ref, o_ref):
    n = pl.program_id(1)
    nb = pl.num_programs(1)
    blk = kp_ref.shape[0]
    sub = q_ref.shape[0] // blk
    zeros = jnp.zeros((blk, kp_ref.shape[1]), kp_ref.dtype)
    k_all = jnp.concatenate([kp_ref[...], kc_ref[...], kn_ref[...]], axis=0)
    v_all = jnp.concatenate([vp_ref[...], vc_ref[...], vn_ref[...]], axis=0)
    col = lax.broadcasted_iota(jnp.int32, (1, 4 * blk), 1)
    ones = jnp.ones((4 * blk, LANES), BF16)
    for u in range(sub):
        k = jnp.concatenate([k_all[u * blk:(u + 3) * blk], zeros], axis=0)
        v = jnp.concatenate([v_all[u * blk:(u + 3) * blk], zeros], axis=0)
        edge = jnp.zeros((1, 4 * blk), F32)
        if u == 0:
            edge = jnp.where(jnp.logical_and(n == 0, col < blk), NEG_INF, edge)
        if u == sub - 1:
            edge = jnp.where(jnp.logical_and(n == nb - 1,
                                             jnp.logical_and(col >= 2 * blk, col < 3 * blk)), NEG_INF, edge)
        q = q_ref[u * blk:(u + 1) * blk, :]
        outs = []
        for kh in range(A_KV_HEADS):
            heads = range(kh * A_GROUP, (kh + 1) * A_GROUP)
            kv = slice(kh * HEAD_DIM, (kh + 1) * HEAD_DIM)
            qs = jnp.concatenate([q[:, h * HEAD_DIM:(h + 1) * HEAD_DIM] for h in heads], axis=0)
            s = _dot_nt(qs, k[:, kv]) + bias_ref[kh] + edge
            p = jnp.exp2(s - jnp.max(s, axis=-1, keepdims=True)).astype(BF16)
            o = _dot(p, v[:, kv]) * (1.0 / _dot(p, ones)[:, :HEAD_DIM])
            outs.extend(o[g * blk:(g + 1) * blk] for g in range(A_GROUP))
        o_ref[u * blk:(u + 1) * blk, :] = jnp.concatenate(outs, axis=-1).astype(o_ref.dtype)


def _window_bias(blk, sink):
    qi = np.arange(blk)[:, None]
    kj = np.arange(3 * blk)[None, :]
    dist = np.abs(blk + qi - kj)
    slopes = _alibi_slopes(A_Q_HEADS)
    tables = np.stack([np.where(dist <= WINDOW, -(slopes[h] * LOG2E) * dist, NEG_INF)
                       for h in range(A_Q_HEADS)])
    pad = jnp.full((A_Q_HEADS, blk, blk), NEG_INF, F32)
    pad = pad.at[:, :, 0].set(jnp.broadcast_to((sink.astype(F32) * LOG2E)[:, None], (A_Q_HEADS, blk)))
    full = jnp.concatenate([jnp.asarray(tables, F32), pad], axis=2)
    return full.reshape(A_KV_HEADS, A_GROUP * blk, 4 * blk)


def _window_attention(proj, sink, batch, seq, sub=2):
    blk = WINDOW
    nb = seq // blk
    sub = min(sub, nb)
    ns = nb // sub
    kcol = MIX_WIDTH // 256
    vcol = kcol + 1
    bias = _window_bias(blk, sink)

    def cur(col):
        return pl.BlockSpec((sub * blk, 256), lambda b, n: (b * ns + n, col))

    def prev(col):
        return pl.BlockSpec((blk, 256), lambda b, n: (b * nb + jnp.maximum(n * sub - 1, 0), col))

    def nxt(col):
        return pl.BlockSpec((blk, 256), lambda b, n: (b * nb + jnp.minimum((n + 1) * sub, nb - 1), col))

    return pl.pallas_call(
        _window_kernel,
        grid=(batch, ns),
        in_specs=[pl.BlockSpec((sub * blk, MIX_WIDTH), lambda b, n: (b * ns + n, 0)),
                  prev(kcol), cur(kcol), nxt(kcol), prev(vcol), cur(vcol), nxt(vcol),
                  pl.BlockSpec(bias.shape, lambda b, n: (0, 0, 0))],
        out_specs=pl.BlockSpec((sub * blk, MIX_WIDTH), lambda b, n: (b * ns + n, 0)),
        out_shape=jax.ShapeDtypeStruct((batch * seq, MIX_WIDTH), BF16),
        compiler_params=_params("parallel", "parallel"),
        name="window_attention",
    )(proj, proj, proj, proj, proj, proj, proj, bias)


def _mem_attn_kernel(q_ref, mkv_ref, o_ref):
    q = q_ref[...]
    mkv = mkv_ref[...]
    ones = jnp.ones((mkv.shape[0], LANES), BF16)
    outs = []
    for h in range(MEM_HEADS):
        sl = slice(h * HEAD_DIM, (h + 1) * HEAD_DIM)
        s = _dot_nt(q[:, sl], mkv[:, sl])
        p = jnp.exp2(s - jnp.max(s, axis=-1, keepdims=True)).astype(BF16)
        o = _dot(p, mkv[:, MEM_WIDTH + h * HEAD_DIM:MEM_WIDTH + (h + 1) * HEAD_DIM])
        outs.append(o * (1.0 / _dot(p, ones)[:, :HEAD_DIM]))
    o_ref[...] = jnp.concatenate(outs, axis=-1).astype(o_ref.dtype)


def _mem_attention(proj, qcol, mkv, batch, seq, tm=512):
    tm = min(tm, seq)
    per_batch = seq // tm
    n_mem = mkv.shape[0] // batch
    return pl.pallas_call(
        _mem_attn_kernel,
        grid=(batch * per_batch,),
        in_specs=[pl.BlockSpec((tm, MEM_WIDTH), lambda i: (i, qcol)),
                  pl.BlockSpec((n_mem, 2 * MEM_WIDTH), lambda i: (i // per_batch, 0))],
        out_specs=pl.BlockSpec((tm, MEM_WIDTH), lambda i: (i, 0)),
        out_shape=jax.ShapeDtypeStruct((batch * seq, MEM_WIDTH), BF16),
        compiler_params=_params("parallel"),
        name="mem_attention",
    )(proj, mkv)


def _out_proj_kernel(mix_ref, mem_ref, w1_ref, w2_ref, x_ref, g_ref, o_ref):
    o = _dot(mix_ref[...], w1_ref[...]) + _dot(mem_ref[...], w2_ref[...])
    o_ref[...] = x_ref[...] + _rms(o, g_ref[...])


def _out_proj(mix, mem_out, w_out, x, g, tm=512):
    t, d = x.shape
    tm = min(tm, t)
    w1 = w_out[:MIX_WIDTH]
    w2 = w_out[MIX_WIDTH:]
    return pl.pallas_call(
        _out_proj_kernel,
        grid=(t // tm,),
        in_specs=[pl.BlockSpec((tm, MIX_WIDTH), lambda i: (i, 0)),
                  pl.BlockSpec((tm, MEM_WIDTH), lambda i: (i, 0)),
                  pl.BlockSpec((MIX_WIDTH, d), lambda i: (0, 0)),
                  pl.BlockSpec((MEM_WIDTH, d), lambda i: (0, 0)),
                  pl.BlockSpec((tm, d), lambda i: (i, 0)),
                  pl.BlockSpec((1, d), lambda i: (0, 0))],
        out_specs=pl.BlockSpec((tm, d), lambda i: (i, 0)),
        out_shape=jax.ShapeDtypeStruct((t, d), F32),
        compiler_params=_params("parallel"),
        name="out_proj",
    )(mix, mem_out, w1, w2, x, g.reshape(1, d))


def _swiglu_tile(h, wg_ref, wu_ref, wd_ref, chunk=256):
    out = None
    for c in range(wg_ref.shape[1] // chunk):
        sl = slice(c * chunk, (c + 1) * chunk)
        a = _dot(h, wg_ref[:, sl])
        u = _dot(h, wu_ref[:, sl])
        z = (a * jax.nn.sigmoid(a) * u).astype(BF16)
        y = _dot(z, wd_ref[sl, :])
        out = y if out is None else out + y
    return out


def _ffn_kernel(x_ref, gin_ref, gout_ref, wg_ref, wu_ref, wd_ref, o_ref, h_sc, acc_sc):
    f = pl.program_id(1)

    @pl.when(f == 0)
    def _():
        h_sc[...] = _rms(x_ref[...], gin_ref[...]).astype(BF16)
        acc_sc[...] = jnp.zeros_like(acc_sc)

    acc_sc[...] += _swiglu_tile(h_sc[...], wg_ref, wu_ref, wd_ref)

    @pl.when(f == pl.num_programs(1) - 1)
    def _():
        o_ref[...] = x_ref[...] + _rms(acc_sc[...], gout_ref[...])


def _ffn(x, g_in, g_out, wg, wu, wd, tm=512, tf=1792):
    t, d = x.shape
    ff = wg.shape[1]
    tm = min(tm, t)
    return pl.pallas_call(
        _ffn_kernel,
        grid=(t // tm, ff // tf),
        in_specs=[pl.BlockSpec((tm, d), lambda i, f: (i, 0)),
                  pl.BlockSpec((1, d), lambda i, f: (0, 0)),
                  pl.BlockSpec((1, d), lambda i, f: (0, 0)),
                  pl.BlockSpec((d, tf), lambda i, f: (0, f)),
                  pl.BlockSpec((d, tf), lambda i, f: (0, f)),
                  pl.BlockSpec((tf, d), lambda i, f: (f, 0))],
        out_specs=pl.BlockSpec((tm, d), lambda i, f: (i, 0)),
        out_shape=jax.ShapeDtypeStruct((t, d), F32),
        scratch_shapes=[pltpu.VMEM((tm, d), BF16), pltpu.VMEM((tm, d), F32)],
        compiler_params=_params("parallel", "arbitrary"),
        name="ffn",
    )(x, g_in.reshape(1, d), g_out.reshape(1, d), wg, wu, wd)


_POS_SPLIT = 256
_LEFT, _DIAG, _RIGHT = 0, 1, 2
_BOUND_LANE = 12
_CHUNK_LANE = 15
_OFFSET_LANE = 18
_ZERO_EXP2_ARG = 136.0
_MIN_DENOM = 2.0 ** -60


def _bf16_split(x):
    hi = x.astype(BF16)
    r = x - hi.astype(F32)
    mid = r.astype(BF16)
    lo = (r - mid.astype(F32)).astype(BF16)
    return hi, mid, lo


def _diff_attn_kernel(reach_ref, slope_ref, q_ref, k_ref, v_ref, qf_ref, kf_ref, dist_ref, lam_ref,
                      g_ref, o_ref, qa_sc, ka_sc, va_sc, k2_sc, m_sc, acc_sc, sa_sc, pa_sc, pb_sc, *, tk,
                      out_scale, lambda_init):
    h = pl.program_id(1)
    i = pl.program_id(2)
    tq = q_ref.shape[0]
    seq = k_ref.shape[0]
    nk = seq // tk
    slope = slope_ref[h]
    reach = reach_ref[h]

    @pl.when(i == 0)
    def _():
        k = k_ref[...]
        ka_sc[:, :B_VDIM] = k
        va_sc[:, :B_VDIM] = v_ref[...]
        lane = lax.broadcasted_iota(jnp.int32, (seq, LANES), 1)
        va_sc[:, B_VDIM:] = jnp.where(lane == 0, 1.0, 0.0).astype(BF16)
        kf = kf_ref[...]
        flane = lax.broadcasted_iota(jnp.int32, kf.shape, 1)
        chunk_lanes = jnp.logical_and(flane >= _CHUNK_LANE, flane < _CHUNK_LANE + 3)
        for c in range(nk):
            ka_sc[c * tk:(c + 1) * tk, B_VDIM:] = jnp.where(chunk_lanes, float(c), kf).astype(BF16)
        kk = k.astype(F32)
        kk = kk * kk
        for c, sel in enumerate((lane < HEAD_DIM, lane >= HEAD_DIM)):
            n2 = jnp.sum(jnp.where(sel, kk, 0.0), axis=-1, keepdims=True)
            k2_sc[c] = jnp.broadcast_to(jnp.max(n2, axis=0, keepdims=True), k2_sc.shape[1:])

    q = q_ref[...]
    lane = lax.broadcasted_iota(jnp.int32, q.shape, 1)
    zero = jnp.zeros_like(q)
    q0 = jnp.where(lane < HEAD_DIM, q, zero)
    q1 = jnp.where(lane >= HEAD_DIM, q, zero)
    qq = q.astype(F32)
    qq = qq * qq
    n0 = jnp.sum(jnp.where(lane < HEAD_DIM, qq, 0.0), axis=-1, keepdims=True)
    n1 = jnp.sum(jnp.where(lane >= HEAD_DIM, qq, 0.0), axis=-1, keepdims=True)
    bound = jnp.concatenate([jnp.sqrt(n0) * jnp.sqrt(k2_sc[0][0:1, 0:1]),
                             jnp.sqrt(n1) * jnp.sqrt(k2_sc[1][0:1, 0:1])], axis=0)
    off = slope * (i * tk).astype(F32)
    qf = qf_ref[...]
    qf2 = jnp.concatenate([qf, qf], axis=0)

    def piece_lanes(x, first_lane):
        lane_ = lax.broadcasted_iota(jnp.int32, x.shape, 1)
        out = jnp.zeros_like(x)
        for n, piece in enumerate(_bf16_split(x)):
            out = jnp.where(lane_ == first_lane + n, piece.astype(F32), out)
        return out.astype(BF16)

    def build_queries(with_bound):
        feats = {_LEFT: -qf2, _DIAG: jnp.zeros_like(qf2), _RIGHT: qf2}
        if with_bound:
            base = piece_lanes(jnp.broadcast_to(bound, (2 * tq, LANES)), _BOUND_LANE)
            side = piece_lanes(jnp.full((SUBLANES, LANES), off, F32), _OFFSET_LANE)[0:1]
            feats = {_LEFT: base + side - qf2, _DIAG: base, _RIGHT: base - side + qf2}
        for variant, feat in feats.items():
            qa_sc[variant, :, B_VDIM:] = feat
            qa_sc[variant, :tq, :B_VDIM] = q0
            qa_sc[variant, tq:, :B_VDIM] = q1

    def scores(j, s_ref):
        variant = jnp.where(j < i, _LEFT, jnp.where(j == i, _DIAG, _RIGHT))
        start = pl.multiple_of(j * tk, tk)
        s_ref[...] = _dot_nt(qa_sc[variant], ka_sc[pl.ds(start, tk), :])

    def diag_bias(j, s_ref):
        @pl.when(j == i)
        def _():
            bias = slope * dist_ref[...]
            s_ref[:tq] -= bias
            s_ref[tq:] -= bias

    def probs(j, p_ref):
        start = pl.multiple_of(j * tk, tk)
        s = _dot_nt(qa_sc[jnp.where(j < i, _LEFT, _RIGHT)], ka_sc[pl.ds(start, tk), :])
        p_ref[...] = jnp.exp2(s).astype(BF16)

    def values(j):
        return va_sc[pl.ds(pl.multiple_of(j * tk, tk), tk), :]

    build_queries(True)
    lo = jnp.maximum(i - reach, 0)
    count = jnp.minimum(i + reach + 1, nk) - lo - 1

    def off_diag(n):
        j = lo + jnp.minimum(n, count - 1)
        return j + (j >= i).astype(jnp.int32)

    s_diag = _dot_nt(qa_sc[_DIAG], ka_sc[pl.ds(pl.multiple_of(i * tk, tk), tk), :])
    probs(off_diag(0), pa_sc)
    bias = slope * dist_ref[...]
    p_diag = jnp.exp2(s_diag - jnp.concatenate([bias, bias], axis=0)).astype(BF16)
    acc_sc[...] = _dot(p_diag, values(i))

    def pair(tt, carry):
        n = 2 * tt
        probs(off_diag(n + 1), pb_sc)
        acc_sc[...] += _dot(pa_sc[...], values(off_diag(n)))
        probs(off_diag(n + 2), pa_sc)
        acc_sc[...] += _dot(pb_sc[...], values(off_diag(n + 1)))
        return carry

    lax.fori_loop(0, count // 2, pair, 0)

    @pl.when(count % 2 == 1)
    def _():
        acc_sc[...] += _dot(pa_sc[...], values(off_diag(count - 1)))

    denom_min = jnp.min(acc_sc[:, B_VDIM:B_VDIM + 1])

    @pl.when(jnp.logical_not(denom_min >= _MIN_DENOM))
    def _():
        build_queries(False)
        m_sc[...] = jnp.full_like(m_sc, NEG_INF)
        acc_sc[...] = jnp.zeros_like(acc_sc)

        def chunk(j, carry):
            scores(j, sa_sc)
            diag_bias(j, sa_sc)
            adj = jnp.where(j > i, off, jnp.where(j < i, -off, 0.0))
            start = pl.multiple_of(j * tk, tk)
            s = sa_sc[...]
            m_prev = m_sc[...]
            m_new = jnp.maximum(m_prev, jnp.max(s, axis=-1, keepdims=True) + adj)
            p = jnp.exp2(s - (m_new - adj)).astype(BF16)
            acc_sc[...] = jnp.exp2(m_prev - m_new) * acc_sc[...] + _dot(p, va_sc[pl.ds(start, tk), :])
            m_sc[...] = m_new
            return carry

        lax.fori_loop(0, nk, chunk, 0)

    lp = lam_ref[...]
    lam = (jnp.exp(jnp.sum(lp[0:1] * lp[1:2], axis=-1, keepdims=True))
           - jnp.exp(jnp.sum(lp[2:3] * lp[3:4], axis=-1, keepdims=True)) + lambda_init)
    acc = acc_sc[...]
    o0 = acc[:tq, :B_VDIM] * (1.0 / acc[:tq, B_VDIM:B_VDIM + 1])
    o1 = acc[tq:, :B_VDIM] * (1.0 / acc[tq:, B_VDIM:B_VDIM + 1])
    o_ref[...] = (_rms(o0 - lam * o1, g_ref[...]) * out_scale).astype(o_ref.dtype)


def _bf16_pieces(x, n=3):
    out = []
    r = np.float64(x)
    for _ in range(n):
        p = np.float64(np.float32(r).astype(jnp.bfloat16).astype(np.float32))
        out.append(p)
        r = r - p
    return out


def _alibi_features(slopes2, t):
    pos = np.arange(t)
    hi = (pos // _POS_SPLIT) * _POS_SPLIT
    lo = pos % _POS_SPLIT
    qf = np.zeros((len(slopes2), t, LANES), np.float32)
    kf = np.zeros((len(slopes2), t, LANES), np.float32)
    for h, s in enumerate(slopes2):
        for n, piece in enumerate(_bf16_pieces(s)):
            for base, part in ((0, hi), (3, lo)):
                qf[h, :, base + n] = -piece
                kf[h, :, base + n] = part
                qf[h, :, 6 + base + n] = part
                kf[h, :, 6 + base + n] = piece
        for n, piece in enumerate(_bf16_pieces(s * t)):
            qf[h, :, _CHUNK_LANE + n] = -piece
        kf[h, :, _BOUND_LANE:_BOUND_LANE + 3] = -1.0
        kf[h, :, _OFFSET_LANE:_OFFSET_LANE + 3] = -1.0
    return jnp.asarray(qf, BF16), jnp.asarray(kf, BF16)


def _diff_attention(proj, b_lambda, subln_g, lambda_init, batch, seq, t=512):
    t = min(t, seq // 2)
    nq = seq // t
    assert seq % (2 * t) == 0
    kcol = MIX_WIDTH // B_VDIM
    vcol = 2 * kcol
    slopes2 = [s * LOG2E for s in _alibi_slopes(B_HEADS)]
    reach = [min(nq, int(math.floor((_ZERO_EXP2_ARG / s - 1.0) / t)) + 1) for s in slopes2]
    qf, kf = _alibi_features(slopes2, t)
    pos = np.arange(t)
    dist = jnp.asarray(np.abs(pos[:, None] - pos[None, :]), F32)
    kernel = functools.partial(_diff_attn_kernel, tk=t, out_scale=1.0 - lambda_init,
                               lambda_init=lambda_init)
    return pl.pallas_call(
        kernel,
        grid_spec=pltpu.PrefetchScalarGridSpec(
            num_scalar_prefetch=2,
            grid=(batch, B_HEADS, nq),
            in_specs=[pl.BlockSpec((t, B_VDIM), lambda b, h, i, r, s: (b * nq + i, h)),
                      pl.BlockSpec((seq, B_VDIM), lambda b, h, i, r, s: (b, kcol + h)),
                      pl.BlockSpec((seq, B_VDIM), lambda b, h, i, r, s: (b, vcol + h)),
                      pl.BlockSpec((None, t, LANES), lambda b, h, i, r, s: (h, 0, 0)),
                      pl.BlockSpec((None, t, LANES), lambda b, h, i, r, s: (h, 0, 0)),
                      pl.BlockSpec((t, t), lambda b, h, i, r, s: (0, 0)),
                      pl.BlockSpec((4, HEAD_DIM), lambda b, h, i, r, s: (0, 0)),
                      pl.BlockSpec((1, B_VDIM), lambda b, h, i, r, s: (0, 0))],
            out_specs=pl.BlockSpec((t, B_VDIM), lambda b, h, i, r, s: (b * nq + i, h)),
            scratch_shapes=[pltpu.VMEM((3, 2 * t, 2 * LANES), BF16),
                            pltpu.VMEM((seq, 2 * LANES), BF16),
                            pltpu.VMEM((seq, 2 * LANES), BF16),
                            pltpu.VMEM((2, 8, LANES), F32),
                            pltpu.VMEM((2 * t, 1), F32),
                            pltpu.VMEM((2 * t, 2 * LANES), F32),
                            pltpu.VMEM((2 * t, t), F32),
                            pltpu.VMEM((2 * t, t), BF16),
                            pltpu.VMEM((2 * t, t), BF16)],
        ),
        out_shape=jax.ShapeDtypeStruct((batch * seq, MIX_WIDTH), BF16),
        compiler_params=_params("arbitrary", "arbitrary", "arbitrary"),
        name="diff_attention",
    )(jnp.asarray(reach, jnp.int32), jnp.asarray(slopes2, F32), proj, proj, proj, qf, kf, dist,
      b_lambda, subln_g.reshape(1, B_VDIM))


def _router_kernel(x_ref, g_ref, whi_ref, wlo_ref, h_ref, e_ref, gate_ref):
    h = _rms(x_ref[...], g_ref[...])
    h_hi = h.astype(BF16)
    h_lo = (h - h_hi.astype(F32)).astype(BF16)
    whi = whi_ref[...]
    logits = _dot(h_hi, whi) + (_dot(h_hi, wlo_ref[...]) + _dot(h_lo, whi))
    lane = lax.broadcasted_iota(jnp.int32, logits.shape, 1)
    logits = jnp.where(lane < N_EXPERTS, logits, -jnp.inf)
    m1 = jnp.max(logits, axis=-1, keepdims=True)
    i1 = jnp.min(jnp.where(logits == m1, lane, LANES), axis=-1, keepdims=True)
    rest = jnp.where(lane == i1, -jnp.inf, logits)
    m2 = jnp.max(rest, axis=-1, keepdims=True)
    i2 = jnp.min(jnp.where(rest == m2, lane, LANES), axis=-1, keepdims=True)
    r = jnp.exp(m2 - m1)
    g1 = 1.0 / (1.0 + r)
    g2 = r * g1
    e_ref[...] = jnp.where(lane == 0, i1, jnp.where(lane == 1, i2, 0))
    gate_ref[...] = jnp.where(lane == 0, g1, jnp.where(lane == 1, g2, 0.0))
    h_ref[...] = h_hi


def _router(x, g, w_router, tm=512):
    t, d = x.shape
    tm = min(tm, t)
    wpad = jnp.zeros((d, LANES), F32).at[:, :N_EXPERTS].set(w_router)
    whi = wpad.astype(BF16)
    wlo = (wpad - whi.astype(F32)).astype(BF16)
    return pl.pallas_call(
        _router_kernel,
        grid=(t // tm,),
        in_specs=[pl.BlockSpec((tm, d), lambda i: (i, 0)),
                  pl.BlockSpec((1, d), lambda i: (0, 0)),
                  pl.BlockSpec((d, LANES), lambda i: (0, 0)),
                  pl.BlockSpec((d, LANES), lambda i: (0, 0))],
        out_specs=[pl.BlockSpec((tm, d), lambda i: (i, 0)),
                   pl.BlockSpec((tm, LANES), lambda i: (i, 0)),
                   pl.BlockSpec((tm, LANES), lambda i: (i, 0))],
        out_shape=[jax.ShapeDtypeStruct((t, d), BF16),
                   jax.ShapeDtypeStruct((t, LANES), jnp.int32),
                   jax.ShapeDtypeStruct((t, LANES), F32)],
        compiler_params=_params("parallel"),
        name="router",
    )(x, g.reshape(1, d), whi, wlo)


def _slab_sizes(rows, tc):
    quarter, half = tc // 4, tc // 2
    return ((quarter, rows <= quarter),
            (half, jnp.logical_and(rows > quarter, rows <= half)),
            (tc, rows > half))


def _dispatch_kernel(slab_ref, fill_ref, h_ref, dest_ref, gate_ref, xs_hbm, gs_hbm,
                     xslab_sc, gslab_sc, zx_sc, zg_sc, sem, zsem):
    i = pl.program_id(0)
    n = pl.num_programs(0)
    slot = i % 2
    tc = h_ref.shape[0]
    tm = zx_sc.shape[0]

    def slab_dma(step, sl, op):
        for e in range(N_EXPERTS):
            start = pl.multiple_of(slab_ref[step * N_EXPERTS + e], SUBLANES)
            for size, cond in _slab_sizes(slab_ref[(n + step) * N_EXPERTS + e], tc):
                @pl.when(cond)
                def _():
                    for src, dst in ((xslab_sc, xs_hbm), (gslab_sc, gs_hbm)):
                        copy = pltpu.make_async_copy(src.at[sl, pl.ds(e * tc, size)],
                                                     dst.at[pl.ds(start, size)], sem.at[sl])
                        getattr(copy, op)()

    def fill_copies(start):
        start = pl.multiple_of(start, SUBLANES)
        return [pltpu.make_async_copy(zx_sc, xs_hbm.at[pl.ds(start, tm)], zsem),
                pltpu.make_async_copy(zg_sc, gs_hbm.at[pl.ds(start, tm)], zsem)]

    @pl.when(i == 0)
    def _():
        zx_sc[...] = jnp.zeros_like(zx_sc)
        zg_sc[...] = jnp.zeros_like(zg_sc)
        for half in range(2):
            for e in range(N_EXPERTS):
                for c in fill_copies(fill_ref[half * N_EXPERTS + e]):
                    c.start()
            for e in range(N_EXPERTS):
                for c in fill_copies(fill_ref[half * N_EXPERTS + e]):
                    c.wait()
        first_free = fill_ref[2 * N_EXPERTS]
        n_tiles = xs_hbm.shape[0] // tm

        def start_tile(b, carry):
            for c in fill_copies(b * tm):
                c.start()
            return carry

        def wait_tile(b, carry):
            for c in fill_copies(b * tm):
                c.wait()
            return carry

        lax.fori_loop(first_free, n_tiles, start_tile, 0)
        lax.fori_loop(first_free, n_tiles, wait_tile, 0)

    dest = dest_ref[0]
    pos = lax.broadcasted_iota(jnp.int32, (tc, tc), 0)
    sel0, sel1 = [], []
    for e in range(N_EXPERTS):
        start = slab_ref[i * N_EXPERTS + e]
        sel0.append(jnp.where(dest[0:1, :] - start == pos, 1.0, 0.0))
        sel1.append(jnp.where(dest[1:2, :] - start == pos, 1.0, 0.0))
    sel0 = jnp.concatenate(sel0, axis=0).astype(BF16)
    sel1 = jnp.concatenate(sel1, axis=0).astype(BF16)
    gate = gate_ref[...]
    lane = lax.broadcasted_iota(jnp.int32, gate.shape, 1)

    def gate_pieces(col):
        out = jnp.zeros(gate.shape, F32)
        for n_, piece in enumerate(_bf16_split(col)):
            out = jnp.where(lane == n_, piece.astype(F32), out)
        return out.astype(BF16)

    xslab_sc[slot] = _dot(sel0 + sel1, h_ref[...])
    gslab_sc[slot] = _dot(sel0, gate_pieces(gate[:, 0:1])) + _dot(sel1, gate_pieces(gate[:, 1:2]))

    @pl.when(i > 0)
    def _():
        slab_dma(i - 1, 1 - slot, "wait")

    slab_dma(i, slot, "start")

    @pl.when(i == n - 1)
    def _():
        slab_dma(i, slot, "wait")


def _dispatch(h, dest_t, gates, slab, fill, p_rows, tm, tc):
    t, d = h.shape
    nb = t // tc
    return pl.pallas_call(
        _dispatch_kernel,
        grid_spec=pltpu.PrefetchScalarGridSpec(
            num_scalar_prefetch=2,
            grid=(nb,),
            in_specs=[pl.BlockSpec((tc, d), lambda i, s, z: (i, 0)),
                      pl.BlockSpec((1, TOP_K, tc), lambda i, s, z: (i, 0, 0)),
                      pl.BlockSpec((tc, LANES), lambda i, s, z: (i, 0))],
            out_specs=[pl.BlockSpec(memory_space=pl.ANY), pl.BlockSpec(memory_space=pl.ANY)],
            scratch_shapes=[pltpu.VMEM((2, N_EXPERTS * tc, d), F32),
                            pltpu.VMEM((2, N_EXPERTS * tc, LANES), F32),
                            pltpu.VMEM((tm, d), F32), pltpu.VMEM((tm, LANES), F32),
                            pltpu.SemaphoreType.DMA((2,)), pltpu.SemaphoreType.DMA(())],
        ),
        out_shape=[jax.ShapeDtypeStruct((p_rows, d), F32), jax.ShapeDtypeStruct((p_rows, LANES), F32)],
        compiler_params=_params("arbitrary"),
        name="moe_dispatch",
    )(slab, fill, h, dest_t, gates)


def _moe_kernel(blk_e_ref, used_ref, xblk_ref, x_ref, gs_ref, wg_ref, wu_ref, wd_ref, o_ref,
                xb_sc, acc_sc):
    i = pl.program_id(0)
    f = pl.program_id(1)
    used = used_ref[i] > 0

    @pl.when(jnp.logical_and(used, f == 0))
    def _():
        xb_sc[...] = x_ref[...].astype(BF16)
        acc_sc[...] = jnp.zeros_like(acc_sc)

    @pl.when(used)
    def _():
        acc_sc[...] += _swiglu_tile(xb_sc[...], wg_ref, wu_ref, wd_ref)

    @pl.when(f == pl.num_programs(1) - 1)
    def _():
        gs = gs_ref[...]
        gate = gs[:, 0:1] + gs[:, 1:2] + gs[:, 2:3]
        o_ref[...] = jnp.where(used, acc_sc[...] * gate, 0.0)


def _moe_experts(xs, gs, blk_e, blk_used, xblk, wg, wu, wd, tm, tf=1792):
    p, d = xs.shape
    nblk = p // tm
    nf = wg.shape[2] // tf

    def fidx(i, f, used_ref):
        return jnp.where(used_ref[i] > 0, f, nf - 1)

    return pl.pallas_call(
        _moe_kernel,
        grid_spec=pltpu.PrefetchScalarGridSpec(
            num_scalar_prefetch=3,
            grid=(nblk, nf),
            in_specs=[pl.BlockSpec((tm, d), lambda i, f, be, us, xb: (xb[i], 0)),
                      pl.BlockSpec((tm, LANES), lambda i, f, be, us, xb: (xb[i], 0)),
                      pl.BlockSpec((None, d, tf), lambda i, f, be, us, xb: (be[i], 0, fidx(i, f, us))),
                      pl.BlockSpec((None, d, tf), lambda i, f, be, us, xb: (be[i], 0, fidx(i, f, us))),
                      pl.BlockSpec((None, tf, d), lambda i, f, be, us, xb: (be[i], fidx(i, f, us), 0))],
            out_specs=pl.BlockSpec((tm, d), lambda i, f, be, us, xb: (i, 0)),
            scratch_shapes=[pltpu.VMEM((tm, d), BF16), pltpu.VMEM((tm, d), F32)],
        ),
        out_shape=jax.ShapeDtypeStruct((p, d), F32),
        compiler_params=_params("arbitrary", "arbitrary"),
        name="moe_experts",
    )(blk_e, blk_used, xblk, xs, gs, wg, wu, wd)


def _combine_kernel(slab_ref, ys_hbm, dest_ref, x_ref, g_ref, o_ref, slab_sc, sem):
    i = pl.program_id(0)
    n = pl.num_programs(0)
    slot = i % 2
    tc = x_ref.shape[0]

    def slab_dma(step, sl, op):
        for e in range(N_EXPERTS):
            start = pl.multiple_of(slab_ref[step * N_EXPERTS + e], SUBLANES)
            for size, cond in _slab_sizes(slab_ref[(n + step) * N_EXPERTS + e], tc):
                @pl.when(cond)
                def _():
                    copy = pltpu.make_async_copy(ys_hbm.at[pl.ds(start, size)],
                                                 slab_sc.at[sl, pl.ds(e * tc, size)], sem.at[sl])
                    getattr(copy, op)()

    @pl.when(i == 0)
    def _():
        slab_sc[...] = jnp.zeros_like(slab_sc)
        slab_dma(0, 0, "start")

    @pl.when(i + 1 < n)
    def _():
        slab_dma(i + 1, 1 - slot, "start")

    dest = dest_ref[...]
    pos = lax.broadcasted_iota(jnp.int32, (tc, tc), 1)
    sel = []
    for e in range(N_EXPERTS):
        start = slab_ref[i * N_EXPERTS + e]
        hit = jnp.logical_or(dest[:, 0:1] - start == pos, dest[:, 1:2] - start == pos)
        sel.append(jnp.where(hit, 1.0, 0.0))
    sel = jnp.concatenate(sel, axis=1).astype(BF16)

    slab_dma(i, slot, "wait")
    y = _dot(sel, slab_sc[slot].astype(BF16))
    o_ref[...] = x_ref[...] + _rms(y, g_ref[...])


def _combine(ys, dest, slab, x, g, tc):
    t, d = x.shape
    return pl.pallas_call(
        _combine_kernel,
        grid_spec=pltpu.PrefetchScalarGridSpec(
            num_scalar_prefetch=1,
            grid=(t // tc,),
            in_specs=[pl.BlockSpec(memory_space=pl.ANY),
                      pl.BlockSpec((tc, LANES), lambda i, s: (i, 0)),
                      pl.BlockSpec((tc, d), lambda i, s: (i, 0)),
                      pl.BlockSpec((1, d), lambda i, s: (0, 0))],
            out_specs=pl.BlockSpec((tc, d), lambda i, s: (i, 0)),
            scratch_shapes=[pltpu.VMEM((2, N_EXPERTS * tc, d), F32), pltpu.SemaphoreType.DMA((2,))],
        ),
        out_shape=jax.ShapeDtypeStruct((t, d), F32),
        compiler_params=_params("arbitrary"),
        name="moe_combine",
    )(slab, ys, dest, x, g.reshape(1, d))


def _route(top_e, tm, tc):
    t = top_e.shape[0]
    a = t * TOP_K
    nb = t // tc
    e_flat = top_e.reshape(a)
    onehot = (e_flat[:, None] == jnp.arange(N_EXPERTS, dtype=jnp.int32)[None, :]).astype(jnp.int32)
    csum = jnp.cumsum(onehot, axis=0)
    rank = jnp.sum(onehot * (csum - 1), axis=1)
    upto = csum[TOP_K * tc - 1::TOP_K * tc]
    before = jnp.concatenate([jnp.zeros((1, N_EXPERTS), jnp.int32), upto[:nb - 1]], axis=0)
    rows = (upto - before + SUBLANES - 1) // SUBLANES * SUBLANES
    rows_before = jnp.cumsum(rows, axis=0) - rows
    total = jnp.sum(rows, axis=0)
    region = (total + 2 * tm - 1) // tm * tm
    rends = jnp.cumsum(region)
    rstarts = rends - region
    slab = (rstarts[None, :] + rows_before).astype(jnp.int32)
    shift = jnp.broadcast_to((slab - before)[:, None, :], (nb, TOP_K * tc, N_EXPERTS)).reshape(a, N_EXPERTS)
    dest = (rank + jnp.sum(onehot * shift, axis=1)).astype(jnp.int32).reshape(t, TOP_K)
    slab = jnp.concatenate([slab.reshape(nb * N_EXPERTS), rows.reshape(nb * N_EXPERTS)])
    row_end = rstarts + total
    fill = jnp.concatenate([row_end, rends - tm, rends[-1:] // tm]).astype(jnp.int32)
    nblk = -(-(a + nb * N_EXPERTS * (SUBLANES - 1)) // tm) + 2 * N_EXPERTS
    blk_start = jnp.arange(nblk, dtype=jnp.int32) * tm
    blk_e = jnp.minimum(jnp.sum((rends[None, :] <= blk_start[:, None]).astype(jnp.int32), axis=1),
                        N_EXPERTS - 1)
    blk_used = jnp.logical_and(blk_start < row_end[blk_e], blk_start < rends[-1]).astype(jnp.int32)
    first_used = jnp.argmax(blk_used).astype(jnp.int32)
    xblk = jnp.where(blk_used > 0, jnp.arange(nblk, dtype=jnp.int32), first_used)
    return dest, slab, fill, blk_e, blk_used, xblk, nblk * tm


def _moe(x, g_in, g_out, w_router, wg, wu, wd, tm=512, tc=256):
    t = x.shape[0]
    tc = min(tc, t)
    h, e_pad, gates = _router(x, g_in, w_router)
    dest, slab, fill, blk_e, blk_used, xblk, p_rows = _route(e_pad[:, :TOP_K], tm, tc)
    dest_t = dest.reshape(t // tc, tc, TOP_K).transpose(0, 2, 1)
    dest_pad = jnp.zeros((t, LANES), jnp.int32).at[:, :TOP_K].set(dest)
    xs, gs = _dispatch(h, dest_t, gates, slab, fill, p_rows, tm, tc)
    ys = _moe_experts(xs, gs, blk_e, blk_used, xblk, wg, wu, wd, tm)
    return _combine(ys, dest_pad, slab, x, g_out, tc)


def kernel(x, mem, norm_g, mem_norm_g, w_mem_kv, a_w_in, a_sink, a_w_out, b_w_in, b_lambda,
           b_subln_g, b_w_out, ffn_w_gate, ffn_w_up, ffn_w_down, moe_w_router, moe_w_gate,
           moe_w_up, moe_w_down):
    batch, seq, d = x.shape
    n_mem = mem.shape[1]
    bf = lambda w: w.astype(BF16)
    xt = x.reshape(batch * seq, d)
    mkv = _norm_matmul(mem.reshape(batch * n_mem, d), mem_norm_g, bf(w_mem_kv),
                       jnp.ones((w_mem_kv.shape[1],), F32))

    g = norm_g[0]
    proj = _norm_matmul(xt, g[0], bf(a_w_in[0]), _query_col_scale(a_w_in.shape[2], A_Q_HEADS * HEAD_DIM))
    mix = _window_attention(proj, a_sink[0], batch, seq)
    mem_out = _mem_attention(proj, (A_Q_HEADS + 2 * A_KV_HEADS) * HEAD_DIM // MEM_WIDTH, mkv, batch, seq)
    xt = _out_proj(mix, mem_out, bf(a_w_out[0]), xt, g[1])
    xt = _ffn(xt, g[2], g[3], bf(ffn_w_gate[0]), bf(ffn_w_up[0]), bf(ffn_w_down[0]))

    g = norm_g[1]
    lambda_init = 0.8 - 0.6 * math.exp(-0.3 * 1)
    proj = _norm_matmul(xt, g[0], bf(b_w_in[0]), _query_col_scale(b_w_in.shape[2], MIX_WIDTH))
    mix = _diff_attention(proj, b_lambda[0], b_subln_g[0], lambda_init, batch, seq)
    mem_out = _mem_attention(proj, 3 * MIX_WIDTH // MEM_WIDTH, mkv, batch, seq)
    xt = _out_proj(mix, mem_out, bf(b_w_out[0]), xt, g[1])
    xt = _moe(xt, g[2], g[3], moe_w_router[0], bf(moe_w_gate[0]), bf(moe_w_up[0]), bf(moe_w_down[0]))
    return xt.reshape(batch, seq, d)
```

```python
import functools
import math

import jax
import jax.numpy as jnp
import numpy as np
from jax import lax
from jax.experimental import pallas as pl
from jax.experimental.pallas import tpu as pltpu

D_MODEL = 1024
HEAD_DIM = 64
MIX_WIDTH = 768
MEM_WIDTH = 256
MEM_HEADS = 4
A_Q_HEADS = 12
A_KV_HEADS = 4
A_GROUP = 3
WINDOW = 128
B_HEADS = 6
B_VDIM = 128
D_FF = 3584
N_EXPERTS = 8
TOP_K = 2
EPS = 1e-6
NEG_INF = -1e30
LOG2E = math.log2(math.e)
QK_SCALE = HEAD_DIM ** -0.5 * LOG2E
LANES = 128
SUBLANES = 8
VMEM_LIMIT_BYTES = 48 * 1024 * 1024

BF16 = jnp.bfloat16
F32 = jnp.float32


def _params(*sem):
    return pltpu.CompilerParams(dimension_semantics=sem, vmem_limit_bytes=VMEM_LIMIT_BYTES)


def _rms(x, g):
    return x * lax.rsqrt(jnp.mean(x * x, axis=-1, keepdims=True) + EPS) * g


def _dot(a, b):
    return jnp.dot(a, b, preferred_element_type=F32)


def _dot_nt(a, b):
    return lax.dot_general(a, b, (((1,), (1,)), ((), ())), preferred_element_type=F32)


def _alibi_slopes(n):
    return [2.0 ** (-8.0 * (i + 1) / n) for i in range(n)]


def _norm_matmul_kernel(x_ref, g_ref, w_ref, cs_ref, o_ref, *, chunk):
    h = _rms(x_ref[...], g_ref[...]).astype(BF16)
    for c in range(o_ref.shape[1] // chunk):
        sl = slice(c * chunk, (c + 1) * chunk)
        o_ref[:, sl] = (_dot(h, w_ref[:, sl]) * cs_ref[:, sl]).astype(o_ref.dtype)


def _norm_matmul(x, g, w, col_scale, tm=512, chunk=512):
    t, d = x.shape
    n = w.shape[1]
    tm = min(tm, t)
    return pl.pallas_call(
        functools.partial(_norm_matmul_kernel, chunk=min(chunk, n)),
        grid=(t // tm,),
        in_specs=[pl.BlockSpec((tm, d), lambda i: (i, 0)),
                  pl.BlockSpec((1, d), lambda i: (0, 0)),
                  pl.BlockSpec((d, n), lambda i: (0, 0)),
                  pl.BlockSpec((1, n), lambda i: (0, 0))],
        out_specs=pl.BlockSpec((tm, n), lambda i: (i, 0)),
        out_shape=jax.ShapeDtypeStruct((t, n), BF16),
        compiler_params=_params("parallel"),
        name="norm_matmul",
    )(x, g.reshape(1, d), w, col_scale.reshape(1, n))


def _query_col_scale(n, mix_q_cols):
    cols = np.arange(n)
    return jnp.asarray(np.where((cols < mix_q_cols) | (cols >= n - MEM_WIDTH), QK_SCALE, 1.0), F32)


def _window_kernel(q_ref, kp_ref, kc_ref, kn_ref, vp_ref, vc_ref, vn_ref, bias_ref, o_ref):
    n = pl.program_id(1)
    nb = pl.num_programs(1)
    blk = kp_ref.shape[0]
    sub = q_ref.shape[0] // blk
    zeros = jnp.zeros((blk, kp_ref.shape[1]), kp_ref.dtype)
    k_all = jnp.concatenate([kp_ref[...], kc_ref[...], kn_ref[...]], axis=0)
    v_all = jnp.concatenate([vp_ref[...], vc_ref[...], vn_ref[...]], axis=0)
    col = lax.broadcasted_iota(jnp.int32, (1, 4 * blk), 1)
    ones = jnp.ones((4 * blk, LANES), BF16)
    for u in range(sub):
        k = jnp.concatenate([k_all[u * blk:(u + 3) * blk], zeros], axis=0)
        v = jnp.concatenate([v_all[u * blk:(u + 3) * blk], zeros], axis=0)
        edge = jnp.zeros((1, 4 * blk), F32)
        if u == 0:
            edge = jnp.where(jnp.logical_and(n == 0, col < blk), NEG_INF, edge)
        if u == sub - 1:
            edge = jnp.where(jnp.logical_and(n == nb - 1,
                                             jnp.logical_and(col >= 2 * blk, col < 3 * blk)), NEG_INF, edge)
        q = q_ref[u * blk:(u + 1) * blk, :]
        outs = []
        for kh in range(A_KV_HEADS):
            heads = range(kh * A_GROUP, (kh + 1) * A_GROUP)
            kv = slice(kh * HEAD_DIM, (kh + 1) * HEAD_DIM)
            qs = jnp.concatenate([q[:, h * HEAD_DIM:(h + 1) * HEAD_DIM] for h in heads], axis=0)
            s = _dot_nt(qs, k[:, kv]) + bias_ref[kh] + edge
            p = jnp.exp2(s - jnp.max(s, axis=-1, keepdims=True)).astype(BF16)
            o = _dot(p, v[:, kv]) * (1.0 / _dot(p, ones)[:, :HEAD_DIM])
            outs.extend(o[g * blk:(g + 1) * blk] for g in range(A_GROUP))
        o_ref[u * blk:(u + 1) * blk, :] = jnp.concatenate(outs, axis=-1).astype(o_ref.dtype)


def _window_bias(blk, sink):
    qi = np.arange(blk)[:, None]
    kj = np.arange(3 * blk)[None, :]
    dist = np.abs(blk + qi - kj)
    slopes = _alibi_slopes(A_Q_HEADS)
    tables = np.stack([np.where(dist <= WINDOW, -(slopes[h] * LOG2E) * dist, NEG_INF)
                       for h in range(A_Q_HEADS)])
    pad = jnp.full((A_Q_HEADS, blk, blk), NEG_INF, F32)
    pad = pad.at[:, :, 0].set(jnp.broadcast_to((sink.astype(F32) * LOG2E)[:, None], (A_Q_HEADS, blk)))
    full = jnp.concatenate([jnp.asarray(tables, F32), pad], axis=2)
    return full.reshape(A_KV_HEADS, A_GROUP * blk, 4 * blk)


def _window_attention(proj, sink, batch, seq, sub=2):
    blk = WINDOW
    nb = seq // blk
    sub = min(sub, nb)
    ns = nb // sub
    kcol = MIX_WIDTH // 256
    vcol = kcol + 1
    bias = _window_bias(blk, sink)

    def cur(col):
        return pl.BlockSpec((sub * blk, 256), lambda b, n: (b * ns + n, col))

    def prev(col):
        return pl.BlockSpec((blk, 256), lambda b, n: (b * nb + jnp.maximum(n * sub - 1, 0), col))

    def nxt(col):
        return pl.BlockSpec((blk, 256), lambda b, n: (b * nb + jnp.minimum((n + 1) * sub, nb - 1), col))

    return pl.pallas_call(
        _window_kernel,
        grid=(batch, ns),
        in_specs=[pl.BlockSpec((sub * blk, MIX_WIDTH), lambda b, n: (b * ns + n, 0)),
                  prev(kcol), cur(kcol), nxt(kcol), prev(vcol), cur(vcol), nxt(vcol),
                  pl.BlockSpec(bias.shape, lambda b, n: (0, 0, 0))],
        out_specs=pl.BlockSpec((sub * blk, MIX_WIDTH), lambda b, n: (b * ns + n, 0)),
        out_shape=jax.ShapeDtypeStruct((batch * seq, MIX_WIDTH), BF16),
        compiler_params=_params("parallel", "parallel"),
        name="window_attention",
    )(proj, proj, proj, proj, proj, proj, proj, bias)


def _mem_attn_kernel(q_ref, mkv_ref, o_ref):
    q = q_ref[...]
    mkv = mkv_ref[...]
    ones = jnp.ones((mkv.shape[0], LANES), BF16)
    outs = []
    for h in range(MEM_HEADS):
        sl = slice(h * HEAD_DIM, (h + 1) * HEAD_DIM)
        s = _dot_nt(q[:, sl], mkv[:, sl])
        p = jnp.exp2(s - jnp.max(s, axis=-1, keepdims=True)).astype(BF16)
        o = _dot(p, mkv[:, MEM_WIDTH + h * HEAD_DIM:MEM_WIDTH + (h + 1) * HEAD_DIM])
        outs.append(o * (1.0 / _dot(p, ones)[:, :HEAD_DIM]))
    o_ref[...] = jnp.concatenate(outs, axis=-1).astype(o_ref.dtype)


def _mem_attention(proj, qcol, mkv, batch, seq, tm=512):
    tm = min(tm, seq)
    per_batch = seq // tm
    n_mem = mkv.shape[0] // batch
    return pl.pallas_call(
        _mem_attn_kernel,
        grid=(batch * per_batch,),
        in_specs=[pl.BlockSpec((tm, MEM_WIDTH), lambda i: (i, qcol)),
                  pl.BlockSpec((n_mem, 2 * MEM_WIDTH), lambda i: (i // per_batch, 0))],
        out_specs=pl.BlockSpec((tm, MEM_WIDTH), lambda i: (i, 0)),
        out_shape=jax.ShapeDtypeStruct((batch * seq, MEM_WIDTH), BF16),
        compiler_params=_params("parallel"),
        name="mem_attention",
    )(proj, mkv)


def _out_proj_kernel(mix_ref, mem_ref, w1_ref, w2_ref, x_ref, g_ref, o_ref):
    o = _dot(mix_ref[...], w1_ref[...]) + _dot(mem_ref[...], w2_ref[...])
    o_ref[...] = x_ref[...] + _rms(o, g_ref[...])


def _out_proj(mix, mem_out, w_out, x, g, tm=512):
    t, d = x.shape
    tm = min(tm, t)
    w1 = w_out[:MIX_WIDTH]
    w2 = w_out[MIX_WIDTH:]
    return pl.pallas_call(
        _out_proj_kernel,
        grid=(t // tm,),
        in_specs=[pl.BlockSpec((tm, MIX_WIDTH), lambda i: (i, 0)),
                  pl.BlockSpec((tm, MEM_WIDTH), lambda i: (i, 0)),
                  pl.BlockSpec((MIX_WIDTH, d), lambda i: (0, 0)),
                  pl.BlockSpec((MEM_WIDTH, d), lambda i: (0, 0)),
                  pl.BlockSpec((tm, d), lambda i: (i, 0)),
                  pl.BlockSpec((1, d), lambda i: (0, 0))],
        out_specs=pl.BlockSpec((tm, d), lambda i: (i, 0)),
        out_shape=jax.ShapeDtypeStruct((t, d), F32),
        compiler_params=_params("parallel"),
        name="out_proj",
    )(mix, mem_out, w1, w2, x, g.reshape(1, d))


def _swiglu_tile(h, wg_ref, wu_ref, wd_ref, chunk=256):
    out = None
    for c in range(wg_ref.shape[1] // chunk):
        sl = slice(c * chunk, (c + 1) * chunk)
        a = _dot(h, wg_ref[:, sl])
        u = _dot(h, wu_ref[:, sl])
        z = (a * jax.nn.sigmoid(a) * u).astype(BF16)
        y = _dot(z, wd_ref[sl, :])
        out = y if out is None else out + y
    return out


def _ffn_kernel(x_ref, gin_ref, gout_ref, wg_ref, wu_ref, wd_ref, o_ref, h_sc, acc_sc):
    f = pl.program_id(1)

    @pl.when(f == 0)
    def _():
        h_sc[...] = _rms(x_ref[...], gin_ref[...]).astype(BF16)
        acc_sc[...] = jnp.zeros_like(acc_sc)

    acc_sc[...] += _swiglu_tile(h_sc[...], wg_ref, wu_ref, wd_ref)

    @pl.when(f == pl.num_programs(1) - 1)
    def _():
        o_ref[...] = x_ref[...] + _rms(acc_sc[...], gout_ref[...])


def _ffn(x, g_in, g_out, wg, wu, wd, tm=512, tf=1792):
    t, d = x.shape
    ff = wg.shape[1]
    tm = min(tm, t)
    return pl.pallas_call(
        _ffn_kernel,
        grid=(t // tm, ff // tf),
        in_specs=[pl.BlockSpec((tm, d), lambda i, f: (i, 0)),
                  pl.BlockSpec((1, d), lambda i, f: (0, 0)),
                  pl.BlockSpec((1, d), lambda i, f: (0, 0)),
                  pl.BlockSpec((d, tf), lambda i, f: (0, f)),
                  pl.BlockSpec((d, tf), lambda i, f: (0, f)),
                  pl.BlockSpec((tf, d), lambda i, f: (f, 0))],
        out_specs=pl.BlockSpec((tm, d), lambda i, f: (i, 0)),
        out_shape=jax.ShapeDtypeStruct((t, d), F32),
        scratch_shapes=[pltpu.VMEM((tm, d), BF16), pltpu.VMEM((tm, d), F32)],
        compiler_params=_params("parallel", "arbitrary"),
        name="ffn",
    )(x, g_in.reshape(1, d), g_out.reshape(1, d), wg, wu, wd)


_POS_SPLIT = 256
_LEFT, _DIAG, _RIGHT = 0, 1, 2
_BOUND_LANE = 12
_CHUNK_LANE = 15
_OFFSET_LANE = 18
_ZERO_EXP2_ARG = 136.0
_MIN_DENOM = 2.0 ** -60


def _bf16_split(x):
    hi = x.astype(BF16)
    r = x - hi.astype(F32)
    mid = r.astype(BF16)
    lo = (r - mid.astype(F32)).astype(BF16)
    return hi, mid, lo


def _diff_attn_kernel(reach_ref, slope_ref, q_ref, k_ref, v_ref, qf_ref, kf_ref, dist_ref, lam_ref,
                      g_ref, o_ref, qa_sc, ka_sc, va_sc, k2_sc, m_sc, acc_sc, sa_sc, pa_sc, pb_sc, *, tk,
                      out_scale, lambda_init):
    h = pl.program_id(1)
    i = pl.program_id(2)
    tq = q_ref.shape[0]
    seq = k_ref.shape[0]
    nk = seq // tk
    slope = slope_ref[h]
    reach = reach_ref[h]

    @pl.when(i == 0)
    def _():
        k = k_ref[...]
        ka_sc[:, :B_VDIM] = k
        va_sc[:, :B_VDIM] = v_ref[...]
        lane = lax.broadcasted_iota(jnp.int32, (seq, LANES), 1)
        va_sc[:, B_VDIM:] = jnp.ones((seq, LANES), BF16)
        kf = kf_ref[...]
        flane = lax.broadcasted_iota(jnp.int32, kf.shape, 1)
        chunk_lanes = jnp.logical_and(flane >= _CHUNK_LANE, flane < _CHUNK_LANE + 3)
        for c in range(nk):
            ka_sc[c * tk:(c + 1) * tk, B_VDIM:] = jnp.where(chunk_lanes, float(c), kf).astype(BF16)
        kk = k.astype(F32)
        kk = kk * kk
        for c, sel in enumerate((lane < HEAD_DIM, lane >= HEAD_DIM)):
            n2 = jnp.sum(jnp.where(sel, kk, 0.0), axis=-1, keepdims=True)
            k2_sc[c] = jnp.broadcast_to(jnp.max(n2, axis=0, keepdims=True), k2_sc.shape[1:])

    q = q_ref[...]
    lane = lax.broadcasted_iota(jnp.int32, q.shape, 1)
    zero = jnp.zeros_like(q)
    q0 = jnp.where(lane < HEAD_DIM, q, zero)
    q1 = jnp.where(lane >= HEAD_DIM, q, zero)
    qq = q.astype(F32)
    qq = qq * qq
    n0 = jnp.sum(jnp.where(lane < HEAD_DIM, qq, 0.0), axis=-1, keepdims=True)
    n1 = jnp.sum(jnp.where(lane >= HEAD_DIM, qq, 0.0), axis=-1, keepdims=True)
    bound = jnp.concatenate([jnp.sqrt(n0) * jnp.sqrt(k2_sc[0][0:1, 0:1]),
                             jnp.sqrt(n1) * jnp.sqrt(k2_sc[1][0:1, 0:1])], axis=0)
    off = slope * (i * tk).astype(F32)
    qf = qf_ref[...]
    qf2 = jnp.concatenate([qf, qf], axis=0)

    def piece_lanes(x, first_lane):
        lane_ = lax.broadcasted_iota(jnp.int32, x.shape, 1)
        out = jnp.zeros_like(x)
        for n, piece in enumerate(_bf16_split(x)):
            out = jnp.where(lane_ == first_lane + n, piece.astype(F32), out)
        return out.astype(BF16)

    def build_queries(with_bound):
        feats = {_LEFT: -qf2, _DIAG: jnp.zeros_like(qf2), _RIGHT: qf2}
        if with_bound:
            base = piece_lanes(jnp.broadcast_to(bound, (2 * tq, LANES)), _BOUND_LANE)
            side = piece_lanes(jnp.full((SUBLANES, LANES), off, F32), _OFFSET_LANE)[0:1]
            feats = {_LEFT: base + side - qf2, _DIAG: base, _RIGHT: base - side + qf2}
        for variant, feat in feats.items():
            qa_sc[variant, :, B_VDIM:] = feat
            qa_sc[variant, :tq, :B_VDIM] = q0
            qa_sc[variant, tq:, :B_VDIM] = q1

    def scores(j, s_ref):
        variant = jnp.where(j < i, _LEFT, jnp.where(j == i, _DIAG, _RIGHT))
        start = pl.multiple_of(j * tk, tk)
        s_ref[...] = _dot_nt(qa_sc[variant], ka_sc[pl.ds(start, tk), :])

    def diag_bias(j, s_ref):
        @pl.when(j == i)
        def _():
            bias = slope * dist_ref[...]
            s_ref[:tq] -= bias
            s_ref[tq:] -= bias

    def probs(j, p_ref):
        start = pl.multiple_of(j * tk, tk)
        s = _dot_nt(qa_sc[jnp.where(j < i, _LEFT, _RIGHT)], ka_sc[pl.ds(start, tk), :])
        p_ref[...] = jnp.exp2(s).astype(BF16)

    def values(j):
        return va_sc[pl.ds(pl.multiple_of(j * tk, tk), tk), :]

    build_queries(True)
    lo = jnp.maximum(i - reach, 0)
    count = jnp.minimum(i + reach + 1, nk) - lo - 1

    def off_diag(n):
        j = lo + jnp.minimum(n, count - 1)
        return j + (j >= i).astype(jnp.int32)

    s_diag = _dot_nt(qa_sc[_DIAG], ka_sc[pl.ds(pl.multiple_of(i * tk, tk), tk), :])
    probs(off_diag(0), pa_sc)
    bias = slope * dist_ref[...]
    p_diag = jnp.exp2(s_diag - jnp.concatenate([bias, bias], axis=0)).astype(BF16)
    acc_sc[...] = _dot(p_diag, values(i))

    def pair(tt, carry):
        n = 2 * tt
        probs(off_diag(n + 1), pb_sc)
        acc_sc[...] += _dot(pa_sc[...], values(off_diag(n)))
        probs(off_diag(n + 2), pa_sc)
        acc_sc[...] += _dot(pb_sc[...], values(off_diag(n + 1)))
        return carry

    lax.fori_loop(0, count // 2, pair, 0)

    @pl.when(count % 2 == 1)
    def _():
        acc_sc[...] += _dot(pa_sc[...], values(off_diag(count - 1)))

    denom_min = jnp.min(acc_sc[:, B_VDIM:])

    @pl.when(jnp.logical_not(denom_min >= _MIN_DENOM))
    def _():
        build_queries(False)
        m_sc[...] = jnp.full_like(m_sc, NEG_INF)
        acc_sc[...] = jnp.zeros_like(acc_sc)

        def chunk(j, carry):
            scores(j, sa_sc)
            diag_bias(j, sa_sc)
            adj = jnp.where(j > i, off, jnp.where(j < i, -off, 0.0))
            start = pl.multiple_of(j * tk, tk)
            s = sa_sc[...]
            m_prev = m_sc[...]
            m_new = jnp.maximum(m_prev, jnp.max(s, axis=-1, keepdims=True) + adj)
            p = jnp.exp2(s - (m_new - adj)).astype(BF16)
            acc_sc[...] = jnp.exp2(m_prev - m_new) * acc_sc[...] + _dot(p, va_sc[pl.ds(start, tk), :])
            m_sc[...] = m_new
            return carry

        lax.fori_loop(0, nk, chunk, 0)

    lp = lam_ref[...]
    lam = (jnp.exp(jnp.sum(lp[0:1] * lp[1:2], axis=-1, keepdims=True))
           - jnp.exp(jnp.sum(lp[2:3] * lp[3:4], axis=-1, keepdims=True)) + lambda_init)
    acc = acc_sc[...]
    o0 = acc[:tq, :B_VDIM] * (1.0 / acc[:tq, B_VDIM:])
    o1 = acc[tq:, :B_VDIM] * (1.0 / acc[tq:, B_VDIM:])
    o_ref[...] = (_rms(o0 - lam * o1, g_ref[...]) * out_scale).astype(o_ref.dtype)


def _bf16_pieces(x, n=3):
    out = []
    r = np.float64(x)
    for _ in range(n):
        p = np.float64(np.float32(r).astype(jnp.bfloat16).astype(np.float32))
        out.append(p)
        r = r - p
    return out


def _alibi_features(slopes2, t):
    pos = np.arange(t)
    hi = (pos // _POS_SPLIT) * _POS_SPLIT
    lo = pos % _POS_SPLIT
    qf = np.zeros((len(slopes2), t, LANES), np.float32)
    kf = np.zeros((len(slopes2), t, LANES), np.float32)
    for h, s in enumerate(slopes2):
        for n, piece in enumerate(_bf16_pieces(s)):
            for base, part in ((0, hi), (3, lo)):
                qf[h, :, base + n] = -piece
                kf[h, :, base + n] = part
                qf[h, :, 6 + base + n] = part
                kf[h, :, 6 + base + n] = piece
        for n, piece in enumerate(_bf16_pieces(s * t)):
            qf[h, :, _CHUNK_LANE + n] = -piece
        kf[h, :, _BOUND_LANE:_BOUND_LANE + 3] = -1.0
        kf[h, :, _OFFSET_LANE:_OFFSET_LANE + 3] = -1.0
    return jnp.asarray(qf, BF16), jnp.asarray(kf, BF16)


def _diff_attention(proj, b_lambda, subln_g, lambda_init, batch, seq, t=512):
    t = min(t, seq // 2)
    nq = seq // t
    assert seq % (2 * t) == 0
    kcol = MIX_WIDTH // B_VDIM
    vcol = 2 * kcol
    slopes2 = [s * LOG2E for s in _alibi_slopes(B_HEADS)]
    reach = [min(nq, int(math.floor((_ZERO_EXP2_ARG / s - 1.0) / t)) + 1) for s in slopes2]
    qf, kf = _alibi_features(slopes2, t)
    pos = np.arange(t)
    dist = jnp.asarray(np.abs(pos[:, None] - pos[None, :]), F32)
    kernel = functools.partial(_diff_attn_kernel, tk=t, out_scale=1.0 - lambda_init,
                               lambda_init=lambda_init)
    return pl.pallas_call(
        kernel,
        grid_spec=pltpu.PrefetchScalarGridSpec(
            num_scalar_prefetch=2,
            grid=(batch, B_HEADS, nq),
            in_specs=[pl.BlockSpec((t, B_VDIM), lambda b, h, i, r, s: (b * nq + i, h)),
                      pl.BlockSpec((seq, B_VDIM), lambda b, h, i, r, s: (b, kcol + h)),
                      pl.BlockSpec((seq, B_VDIM), lambda b, h, i, r, s: (b, vcol + h)),
                      pl.BlockSpec((None, t, LANES), lambda b, h, i, r, s: (h, 0, 0)),
                      pl.BlockSpec((None, t, LANES), lambda b, h, i, r, s: (h, 0, 0)),
                      pl.BlockSpec((t, t), lambda b, h, i, r, s: (0, 0)),
                      pl.BlockSpec((4, HEAD_DIM), lambda b, h, i, r, s: (0, 0)),
                      pl.BlockSpec((1, B_VDIM), lambda b, h, i, r, s: (0, 0))],
            out_specs=pl.BlockSpec((t, B_VDIM), lambda b, h, i, r, s: (b * nq + i, h)),
            scratch_shapes=[pltpu.VMEM((3, 2 * t, 2 * LANES), BF16),
                            pltpu.VMEM((seq, 2 * LANES), BF16),
                            pltpu.VMEM((seq, 2 * LANES), BF16),
                            pltpu.VMEM((2, 8, LANES), F32),
                            pltpu.VMEM((2 * t, 1), F32),
                            pltpu.VMEM((2 * t, 2 * LANES), F32),
                            pltpu.VMEM((2 * t, t), F32),
                            pltpu.VMEM((2 * t, t), BF16),
                            pltpu.VMEM((2 * t, t), BF16)],
        ),
        out_shape=jax.ShapeDtypeStruct((batch * seq, MIX_WIDTH), BF16),
        compiler_params=_params("arbitrary", "arbitrary", "arbitrary"),
        name="diff_attention",
    )(jnp.asarray(reach, jnp.int32), jnp.asarray(slopes2, F32), proj, proj, proj, qf, kf, dist,
      b_lambda, subln_g.reshape(1, B_VDIM))


def _router_kernel(x_ref, g_ref, whi_ref, wlo_ref, h_ref, e_ref, gate_ref):
    h = _rms(x_ref[...], g_ref[...])
    h_hi = h.astype(BF16)
    h_lo = (h - h_hi.astype(F32)).astype(BF16)
    whi = whi_ref[...]
    logits = _dot(h_hi, whi) + (_dot(h_hi, wlo_ref[...]) + _dot(h_lo, whi))
    lane = lax.broadcasted_iota(jnp.int32, logits.shape, 1)
    logits = jnp.where(lane < N_EXPERTS, logits, -jnp.inf)
    m1 = jnp.max(logits, axis=-1, keepdims=True)
    i1 = jnp.min(jnp.where(logits == m1, lane, LANES), axis=-1, keepdims=True)
    rest = jnp.where(lane == i1, -jnp.inf, logits)
    m2 = jnp.max(rest, axis=-1, keepdims=True)
    i2 = jnp.min(jnp.where(rest == m2, lane, LANES), axis=-1, keepdims=True)
    r = jnp.exp(m2 - m1)
    g1 = 1.0 / (1.0 + r)
    g2 = r * g1
    e_ref[...] = jnp.where(lane == 0, i1, jnp.where(lane == 1, i2, 0))
    gate_ref[...] = jnp.where(lane == 0, g1, jnp.where(lane == 1, g2, 0.0))
    h_ref[...] = h_hi


def _router(x, g, w_router, tm=512):
    t, d = x.shape
    tm = min(tm, t)
    wpad = jnp.zeros((d, LANES), F32).at[:, :N_EXPERTS].set(w_router)
    whi = wpad.astype(BF16)
    wlo = (wpad - whi.astype(F32)).astype(BF16)
    return pl.pallas_call(
        _router_kernel,
        grid=(t // tm,),
        in_specs=[pl.BlockSpec((tm, d), lambda i: (i, 0)),
                  pl.BlockSpec((1, d), lambda i: (0, 0)),
                  pl.BlockSpec((d, LANES), lambda i: (0, 0)),
                  pl.BlockSpec((d, LANES), lambda i: (0, 0))],
        out_specs=[pl.BlockSpec((tm, d), lambda i: (i, 0)),
                   pl.BlockSpec((tm, LANES), lambda i: (i, 0)),
                   pl.BlockSpec((tm, LANES), lambda i: (i, 0))],
        out_shape=[jax.ShapeDtypeStruct((t, d), BF16),
                   jax.ShapeDtypeStruct((t, LANES), jnp.int32),
                   jax.ShapeDtypeStruct((t, LANES), F32)],
        compiler_params=_params("parallel"),
        name="router",
    )(x, g.reshape(1, d), whi, wlo)


def _slab_sizes(rows, tc):
    quarter, half = tc // 4, tc // 2
    return ((quarter, rows <= quarter),
            (half, jnp.logical_and(rows > quarter, rows <= half)),
            (tc, rows > half))


def _slab_stride(slab_ref, n, step, tc):
    most = slab_ref[(n + step) * N_EXPERTS]
    for e in range(1, N_EXPERTS):
        most = jnp.maximum(most, slab_ref[(n + step) * N_EXPERTS + e])
    return jnp.where(most <= tc // 2, tc // 2, tc)


def _dispatch_kernel(slab_ref, fill_ref, h_ref, dest_ref, gate_ref, xs_hbm, gs_hbm,
                     xslab_sc, gslab_sc, zx_sc, zg_sc, sem, zsem):
    i = pl.program_id(0)
    n = pl.num_programs(0)
    slot = i % 2
    tc = h_ref.shape[0]
    tm = zx_sc.shape[0]

    def slab_dma(step, sl, op):
        stride = _slab_stride(slab_ref, n, step, tc)
        for e in range(N_EXPERTS):
            start = pl.multiple_of(slab_ref[step * N_EXPERTS + e], SUBLANES)
            first = pl.multiple_of(e * stride, SUBLANES)
            for size, cond in _slab_sizes(slab_ref[(n + step) * N_EXPERTS + e], tc):
                @pl.when(cond)
                def _():
                    for src, dst in ((xslab_sc, xs_hbm), (gslab_sc, gs_hbm)):
                        copy = pltpu.make_async_copy(src.at[sl, pl.ds(first, size)],
                                                     dst.at[pl.ds(start, size)], sem.at[sl])
                        getattr(copy, op)()

    def fill_copies(start):
        start = pl.multiple_of(start, SUBLANES)
        return [pltpu.make_async_copy(zx_sc, xs_hbm.at[pl.ds(start, tm)], zsem),
                pltpu.make_async_copy(zg_sc, gs_hbm.at[pl.ds(start, tm)], zsem)]

    @pl.when(i == 0)
    def _():
        zx_sc[...] = jnp.zeros_like(zx_sc)
        zg_sc[...] = jnp.zeros_like(zg_sc)
        for half in range(2):
            for e in range(N_EXPERTS):
                for c in fill_copies(fill_ref[half * N_EXPERTS + e]):
                    c.start()
            for e in range(N_EXPERTS):
                for c in fill_copies(fill_ref[half * N_EXPERTS + e]):
                    c.wait()
        first_free = fill_ref[2 * N_EXPERTS]
        n_tiles = xs_hbm.shape[0] // tm

        def start_tile(b, carry):
            for c in fill_copies(b * tm):
                c.start()
            return carry

        def wait_tile(b, carry):
            for c in fill_copies(b * tm):
                c.wait()
            return carry

        lax.fori_loop(first_free, n_tiles, start_tile, 0)
        lax.fori_loop(first_free, n_tiles, wait_tile, 0)

    dest = dest_ref[0]
    gate = gate_ref[...]
    lane = lax.broadcasted_iota(jnp.int32, gate.shape, 1)

    def gate_pieces(col):
        out = jnp.zeros(gate.shape, F32)
        for n_, piece in enumerate(_bf16_split(col)):
            out = jnp.where(lane == n_, piece.astype(F32), out)
        return out.astype(BF16)

    def place_rows(stride):
        pos = lax.broadcasted_iota(jnp.int32, (stride, tc), 0)
        sel0, sel1 = [], []
        for e in range(N_EXPERTS):
            start = slab_ref[i * N_EXPERTS + e]
            sel0.append(jnp.where(dest[0:1, :] - start == pos, 1.0, 0.0))
            sel1.append(jnp.where(dest[1:2, :] - start == pos, 1.0, 0.0))
        sel0 = jnp.concatenate(sel0, axis=0).astype(BF16)
        sel1 = jnp.concatenate(sel1, axis=0).astype(BF16)
        rows = pl.ds(0, N_EXPERTS * stride)
        xslab_sc[slot, rows] = _dot(sel0 + sel1, h_ref[...])
        gslab_sc[slot, rows] = (_dot(sel0, gate_pieces(gate[:, 0:1]))
                                + _dot(sel1, gate_pieces(gate[:, 1:2])))

    compact = _slab_stride(slab_ref, n, i, tc) < tc
    pl.when(compact)(lambda: place_rows(tc // 2))
    pl.when(jnp.logical_not(compact))(lambda: place_rows(tc))

    @pl.when(i > 0)
    def _():
        slab_dma(i - 1, 1 - slot, "wait")

    slab_dma(i, slot, "start")

    @pl.when(i == n - 1)
    def _():
        slab_dma(i, slot, "wait")


def _dispatch(h, dest_t, gates, slab, fill, p_rows, tm, tc):
    t, d = h.shape
    nb = t // tc
    return pl.pallas_call(
        _dispatch_kernel,
        grid_spec=pltpu.PrefetchScalarGridSpec(
            num_scalar_prefetch=2,
            grid=(nb,),
            in_specs=[pl.BlockSpec((tc, d), lambda i, s, z: (i, 0)),
                      pl.BlockSpec((1, TOP_K, tc), lambda i, s, z: (i, 0, 0)),
                      pl.BlockSpec((tc, LANES), lambda i, s, z: (i, 0))],
            out_specs=[pl.BlockSpec(memory_space=pl.ANY), pl.BlockSpec(memory_space=pl.ANY)],
            scratch_shapes=[pltpu.VMEM((2, N_EXPERTS * tc, d), F32),
                            pltpu.VMEM((2, N_EXPERTS * tc, LANES), F32),
                            pltpu.VMEM((tm, d), F32), pltpu.VMEM((tm, LANES), F32),
                            pltpu.SemaphoreType.DMA((2,)), pltpu.SemaphoreType.DMA(())],
        ),
        out_shape=[jax.ShapeDtypeStruct((p_rows, d), F32), jax.ShapeDtypeStruct((p_rows, LANES), F32)],
        compiler_params=_params("arbitrary"),
        name="moe_dispatch",
    )(slab, fill, h, dest_t, gates)


def _moe_kernel(blk_e_ref, used_ref, xblk_ref, x_ref, gs_ref, wg_ref, wu_ref, wd_ref, o_ref,
                xb_sc, acc_sc):
    i = pl.program_id(0)
    f = pl.program_id(1)
    used = used_ref[i] > 0

    @pl.when(jnp.logical_and(used, f == 0))
    def _():
        xb_sc[...] = x_ref[...].astype(BF16)
        acc_sc[...] = jnp.zeros_like(acc_sc)

    @pl.when(used)
    def _():
        acc_sc[...] += _swiglu_tile(xb_sc[...], wg_ref, wu_ref, wd_ref)

    @pl.when(f == pl.num_programs(1) - 1)
    def _():
        gs = gs_ref[...]
        gate = gs[:, 0:1] + gs[:, 1:2] + gs[:, 2:3]
        o_ref[...] = jnp.where(used, acc_sc[...] * gate, 0.0)


def _moe_experts(xs, gs, blk_e, blk_used, xblk, wg, wu, wd, tm, tf=1792):
    p, d = xs.shape
    nblk = p // tm
    nf = wg.shape[2] // tf

    def fidx(i, f, used_ref):
        return jnp.where(used_ref[i] > 0, f, nf - 1)

    return pl.pallas_call(
        _moe_kernel,
        grid_spec=pltpu.PrefetchScalarGridSpec(
            num_scalar_prefetch=3,
            grid=(nblk, nf),
            in_specs=[pl.BlockSpec((tm, d), lambda i, f, be, us, xb: (xb[i], 0)),
                      pl.BlockSpec((tm, LANES), lambda i, f, be, us, xb: (xb[i], 0)),
                      pl.BlockSpec((None, d, tf), lambda i, f, be, us, xb: (be[i], 0, fidx(i, f, us))),
                      pl.BlockSpec((None, d, tf), lambda i, f, be, us, xb: (be[i], 0, fidx(i, f, us))),
                      pl.BlockSpec((None, tf, d), lambda i, f, be, us, xb: (be[i], fidx(i, f, us), 0))],
            out_specs=pl.BlockSpec((tm, d), lambda i, f, be, us, xb: (i, 0)),
            scratch_shapes=[pltpu.VMEM((tm, d), BF16), pltpu.VMEM((tm, d), F32)],
        ),
        out_shape=jax.ShapeDtypeStruct((p, d), F32),
        compiler_params=_params("arbitrary", "arbitrary"),
        name="moe_experts",
    )(blk_e, blk_used, xblk, xs, gs, wg, wu, wd)


def _combine_kernel(slab_ref, ys_hbm, dest_ref, x_ref, g_ref, o_ref, slab_sc, sem):
    i = pl.program_id(0)
    n = pl.num_programs(0)
    slot = i % 2
    tc = x_ref.shape[0]

    def slab_dma(step, sl, op):
        stride = _slab_stride(slab_ref, n, step, tc)
        for e in range(N_EXPERTS):
            start = pl.multiple_of(slab_ref[step * N_EXPERTS + e], SUBLANES)
            first = pl.multiple_of(e * stride, SUBLANES)
            for size, cond in _slab_sizes(slab_ref[(n + step) * N_EXPERTS + e], tc):
                @pl.when(cond)
                def _():
                    copy = pltpu.make_async_copy(ys_hbm.at[pl.ds(start, size)],
                                                 slab_sc.at[sl, pl.ds(first, size)], sem.at[sl])
                    getattr(copy, op)()

    @pl.when(i == 0)
    def _():
        slab_sc[...] = jnp.zeros_like(slab_sc)
        slab_dma(0, 0, "start")

    @pl.when(i + 1 < n)
    def _():
        slab_dma(i + 1, 1 - slot, "start")

    dest = dest_ref[...]
    slab_dma(i, slot, "wait")

    def gather_rows(stride):
        pos = lax.broadcasted_iota(jnp.int32, (tc, stride), 1)
        sel = []
        for e in range(N_EXPERTS):
            start = slab_ref[i * N_EXPERTS + e]
            hit = jnp.logical_or(dest[:, 0:1] - start == pos, dest[:, 1:2] - start == pos)
            sel.append(jnp.where(hit, 1.0, 0.0))
        sel = jnp.concatenate(sel, axis=1).astype(BF16)
        y = _dot(sel, slab_sc[slot, pl.ds(0, N_EXPERTS * stride)].astype(BF16))
        o_ref[...] = x_ref[...] + _rms(y, g_ref[...])

    compact = _slab_stride(slab_ref, n, i, tc) < tc
    pl.when(compact)(lambda: gather_rows(tc // 2))
    pl.when(jnp.logical_not(compact))(lambda: gather_rows(tc))


def _combine(ys, dest, slab, x, g, tc):
    t, d = x.shape
    return pl.pallas_call(
        _combine_kernel,
        grid_spec=pltpu.PrefetchScalarGridSpec(
            num_scalar_prefetch=1,
            grid=(t // tc,),
            in_specs=[pl.BlockSpec(memory_space=pl.ANY),
                      pl.BlockSpec((tc, LANES), lambda i, s: (i, 0)),
                      pl.BlockSpec((tc, d), lambda i, s: (i, 0)),
                      pl.BlockSpec((1, d), lambda i, s: (0, 0))],
            out_specs=pl.BlockSpec((tc, d), lambda i, s: (i, 0)),
            scratch_shapes=[pltpu.VMEM((2, N_EXPERTS * tc, d), F32), pltpu.SemaphoreType.DMA((2,))],
        ),
        out_shape=jax.ShapeDtypeStruct((t, d), F32),
        compiler_params=_params("arbitrary"),
        name="moe_combine",
    )(slab, ys, dest, x, g.reshape(1, d))


def _route(top_e, tm, tc):
    t = top_e.shape[0]
    a = t * TOP_K
    nb = t // tc
    e_flat = top_e.reshape(a)
    onehot = (e_flat[:, None] == jnp.arange(N_EXPERTS, dtype=jnp.int32)[None, :]).astype(jnp.int32)
    csum = jnp.cumsum(onehot, axis=0)
    rank = jnp.sum(onehot * (csum - 1), axis=1)
    upto = csum[TOP_K * tc - 1::TOP_K * tc]
    before = jnp.concatenate([jnp.zeros((1, N_EXPERTS), jnp.int32), upto[:nb - 1]], axis=0)
    rows = (upto - before + SUBLANES - 1) // SUBLANES * SUBLANES
    rows_before = jnp.cumsum(rows, axis=0) - rows
    total = jnp.sum(rows, axis=0)
    region = (total + 2 * tm - 1) // tm * tm
    rends = jnp.cumsum(region)
    rstarts = rends - region
    slab = (rstarts[None, :] + rows_before).astype(jnp.int32)
    shift = jnp.broadcast_to((slab - before)[:, None, :], (nb, TOP_K * tc, N_EXPERTS)).reshape(a, N_EXPERTS)
    dest = (rank + jnp.sum(onehot * shift, axis=1)).astype(jnp.int32).reshape(t, TOP_K)
    slab = jnp.concatenate([slab.reshape(nb * N_EXPERTS), rows.reshape(nb * N_EXPERTS)])
    row_end = rstarts + total
    fill = jnp.concatenate([row_end, rends - tm, rends[-1:] // tm]).astype(jnp.int32)
    nblk = -(-(a + nb * N_EXPERTS * (SUBLANES - 1)) // tm) + 2 * N_EXPERTS
    blk_start = jnp.arange(nblk, dtype=jnp.int32) * tm
    blk_e = jnp.minimum(jnp.sum((rends[None, :] <= blk_start[:, None]).astype(jnp.int32), axis=1),
                        N_EXPERTS - 1)
    blk_used = jnp.logical_and(blk_start < row_end[blk_e], blk_start < rends[-1]).astype(jnp.int32)
    first_used = jnp.argmax(blk_used).astype(jnp.int32)
    xblk = jnp.where(blk_used > 0, jnp.arange(nblk, dtype=jnp.int32), first_used)
    return dest, slab, fill, blk_e, blk_used, xblk, nblk * tm


def _moe(x, g_in, g_out, w_router, wg, wu, wd, tm=512, tc=256):
    t = x.shape[0]
    tc = min(tc, t)
    h, e_pad, gates = _router(x, g_in, w_router)
    dest, slab, fill, blk_e, blk_used, xblk, p_rows = _route(e_pad[:, :TOP_K], tm, tc)
    dest_t = dest.reshape(t // tc, tc, TOP_K).transpose(0, 2, 1)
    dest_pad = jnp.zeros((t, LANES), jnp.int32).at[:, :TOP_K].set(dest)
    xs, gs = _dispatch(h, dest_t, gates, slab, fill, p_rows, tm, tc)
    ys = _moe_experts(xs, gs, blk_e, blk_used, xblk, wg, wu, wd, tm)
    return _combine(ys, dest_pad, slab, x, g_out, tc)


def kernel(x, mem, norm_g, mem_norm_g, w_mem_kv, a_w_in, a_sink, a_w_out, b_w_in, b_lambda,
           b_subln_g, b_w_out, ffn_w_gate, ffn_w_up, ffn_w_down, moe_w_router, moe_w_gate,
           moe_w_up, moe_w_down):
    batch, seq, d = x.shape
    n_mem = mem.shape[1]
    bf = lambda w: w.astype(BF16)
    xt = x.reshape(batch * seq, d)
    mkv = _norm_matmul(mem.reshape(batch * n_mem, d), mem_norm_g, bf(w_mem_kv),
                       jnp.ones((w_mem_kv.shape[1],), F32))

    g = norm_g[0]
    proj = _norm_matmul(xt, g[0], bf(a_w_in[0]), _query_col_scale(a_w_in.shape[2], A_Q_HEADS * HEAD_DIM))
    mix = _window_attention(proj, a_sink[0], batch, seq)
    mem_out = _mem_attention(proj, (A_Q_HEADS + 2 * A_KV_HEADS) * HEAD_DIM // MEM_WIDTH, mkv, batch, seq)
    xt = _out_proj(mix, mem_out, bf(a_w_out[0]), xt, g[1])
    xt = _ffn(xt, g[2], g[3], bf(ffn_w_gate[0]), bf(ffn_w_up[0]), bf(ffn_w_down[0]))

    g = norm_g[1]
    lambda_init = 0.8 - 0.6 * math.exp(-0.3 * 1)
    proj = _norm_matmul(xt, g[0], bf(b_w_in[0]), _query_col_scale(b_w_in.shape[2], MIX_WIDTH))
    mix = _diff_attention(proj, b_lambda[0], b_subln_g[0], lambda_init, batch, seq)
    mem_out = _mem_attention(proj, 3 * MIX_WIDTH // MEM_WIDTH, mkv, batch, seq)
    xt = _out_proj(mix, mem_out, bf(b_w_out[0]), xt, g[1])
    xt = _moe(xt, g[2], g[3], moe_w_router[0], bf(moe_w_gate[0]), bf(moe_w_up[0]), bf(moe_w_down[0]))
    return xt.reshape(batch, seq, d)
```

```python
import functools
import math

import jax
import jax.numpy as jnp
import numpy as np
from jax import lax
from jax.experimental import pallas as pl
from jax.experimental.pallas import tpu as pltpu

D_MODEL = 1024
HEAD_DIM = 64
MIX_WIDTH = 768
MEM_WIDTH = 256
MEM_HEADS = 4
A_Q_HEADS = 12
A_KV_HEADS = 4
A_GROUP = 3
WINDOW = 128
B_HEADS = 6
B_VDIM = 128
D_FF = 3584
N_EXPERTS = 8
TOP_K = 2
EPS = 1e-6
NEG_INF = -1e30
LOG2E = math.log2(math.e)
QK_SCALE = HEAD_DIM ** -0.5 * LOG2E
LANES = 128
SUBLANES = 8
VMEM_LIMIT_BYTES = 48 * 1024 * 1024

BF16 = jnp.bfloat16
F32 = jnp.float32


def _params(*sem):
    return pltpu.CompilerParams(dimension_semantics=sem, vmem_limit_bytes=VMEM_LIMIT_BYTES)


def _rms(x, g):
    return x * lax.rsqrt(jnp.mean(x * x, axis=-1, keepdims=True) + EPS) * g


def _dot(a, b):
    return jnp.dot(a, b, preferred_element_type=F32)


def _dot_nt(a, b):
    return lax.dot_general(a, b, (((1,), (1,)), ((), ())), preferred_element_type=F32)


def _alibi_slopes(n):
    return [2.0 ** (-8.0 * (i + 1) / n) for i in range(n)]


def _norm_matmul_kernel(x_ref, g_ref, w_ref, cs_ref, o_ref, *, chunk):
    h = _rms(x_ref[...], g_ref[...]).astype(BF16)
    for c in range(o_ref.shape[1] // chunk):
        sl = slice(c * chunk, (c + 1) * chunk)
        o_ref[:, sl] = (_dot(h, w_ref[:, sl]) * cs_ref[:, sl]).astype(o_ref.dtype)


def _norm_matmul(x, g, w, col_scale, tm=512, chunk=512):
    t, d = x.shape
    n = w.shape[1]
    tm = min(tm, t)
    return pl.pallas_call(
        functools.partial(_norm_matmul_kernel, chunk=min(chunk, n)),
        grid=(t // tm,),
        in_specs=[pl.BlockSpec((tm, d), lambda i: (i, 0)),
                  pl.BlockSpec((1, d), lambda i: (0, 0)),
                  pl.BlockSpec((d, n), lambda i: (0, 0)),
                  pl.BlockSpec((1, n), lambda i: (0, 0))],
        out_specs=pl.BlockSpec((tm, n), lambda i: (i, 0)),
        out_shape=jax.ShapeDtypeStruct((t, n), BF16),
        compiler_params=_params("parallel"),
        name="norm_matmul",
    )(x, g.reshape(1, d), w, col_scale.reshape(1, n))


def _query_col_scale(n, mix_q_cols):
    cols = np.arange(n)
    return jnp.asarray(np.where((cols < mix_q_cols) | (cols >= n - MEM_WIDTH), QK_SCALE, 1.0), F32)


def _window_kernel(q_ref, kp_ref, kc_ref, kn_ref, vp_ref, vc_ref, vn_ref, bias_ref, o_ref):
    n = pl.program_id(1)
    nb = pl.num_programs(1)
    blk = kp_ref.shape[0]
    sub = q_ref.shape[0] // blk
    zeros = jnp.zeros((blk, kp_ref.shape[1]), kp_ref.dtype)
    k_all = jnp.concatenate([kp_ref[...], kc_ref[...], kn_ref[...]], axis=0)
    v_all = jnp.concatenate([vp_ref[...], vc_ref[...], vn_ref[...]], axis=0)
    col = lax.broadcasted_iota(jnp.int32, (1, 4 * blk), 1)
    ones = jnp.ones((4 * blk, LANES), BF16)
    for u in range(sub):
        k = jnp.concatenate([k_all[u * blk:(u + 3) * blk], zeros], axis=0)
        v = jnp.concatenate([v_all[u * blk:(u + 3) * blk], zeros], axis=0)
        edge = jnp.zeros((1, 4 * blk), F32)
        if u == 0:
            edge = jnp.where(jnp.logical_and(n == 0, col < blk), NEG_INF, edge)
        if u == sub - 1:
            edge = jnp.where(jnp.logical_and(n == nb - 1,
                                             jnp.logical_and(col >= 2 * blk, col < 3 * blk)), NEG_INF, edge)
        q = q_ref[u * blk:(u + 1) * blk, :]
        outs = []
        for kh in range(A_KV_HEADS):
            heads = range(kh * A_GROUP, (kh + 1) * A_GROUP)
            kv = slice(kh * HEAD_DIM, (kh + 1) * HEAD_DIM)
            qs = jnp.concatenate([q[:, h * HEAD_DIM:(h + 1) * HEAD_DIM] for h in heads], axis=0)
            s = _dot_nt(qs, k[:, kv]) + bias_ref[kh] + edge
            p = jnp.exp2(s - jnp.max(s, axis=-1, keepdims=True)).astype(BF16)
            o = _dot(p, v[:, kv]) * (1.0 / _dot(p, ones)[:, :HEAD_DIM])
            outs.extend(o[g * blk:(g + 1) * blk] for g in range(A_GROUP))
        o_ref[u * blk:(u + 1) * blk, :] = jnp.concatenate(outs, axis=-1).astype(o_ref.dtype)


def _window_bias(blk, sink):
    qi = np.arange(blk)[:, None]
    kj = np.arange(3 * blk)[None, :]
    dist = np.abs(blk + qi - kj)
    slopes = _alibi_slopes(A_Q_HEADS)
    tables = np.stack([np.where(dist <= WINDOW, -(slopes[h] * LOG2E) * dist, NEG_INF)
                       for h in range(A_Q_HEADS)])
    pad = jnp.full((A_Q_HEADS, blk, blk), NEG_INF, F32)
    pad = pad.at[:, :, 0].set(jnp.broadcast_to((sink.astype(F32) * LOG2E)[:, None], (A_Q_HEADS, blk)))
    full = jnp.concatenate([jnp.asarray(tables, F32), pad], axis=2)
    return full.reshape(A_KV_HEADS, A_GROUP * blk, 4 * blk)


def _window_attention(proj, sink, batch, seq, sub=2):
    blk = WINDOW
    nb = seq // blk
    sub = min(sub, nb)
    ns = nb // sub
    kcol = MIX_WIDTH // 256
    vcol = kcol + 1
    bias = _window_bias(blk, sink)

    def cur(col):
        return pl.BlockSpec((sub * blk, 256), lambda b, n: (b * ns + n, col))

    def prev(col):
        return pl.BlockSpec((blk, 256), lambda b, n: (b * nb + jnp.maximum(n * sub - 1, 0), col))

    def nxt(col):
        return pl.BlockSpec((blk, 256), lambda b, n: (b * nb + jnp.minimum((n + 1) * sub, nb - 1), col))

    return pl.pallas_call(
        _window_kernel,
        grid=(batch, ns),
        in_specs=[pl.BlockSpec((sub * blk, MIX_WIDTH), lambda b, n: (b * ns + n, 0)),
                  prev(kcol), cur(kcol), nxt(kcol), prev(vcol), cur(vcol), nxt(vcol),
                  pl.BlockSpec(bias.shape, lambda b, n: (0, 0, 0))],
        out_specs=pl.BlockSpec((sub * blk, MIX_WIDTH), lambda b, n: (b * ns + n, 0)),
        out_shape=jax.ShapeDtypeStruct((batch * seq, MIX_WIDTH), BF16),
        compiler_params=_params("parallel", "parallel"),
        name="window_attention",
    )(proj, proj, proj, proj, proj, proj, proj, bias)


def _mem_attn_kernel(q_ref, mkv_ref, o_ref):
    q = q_ref[...]
    mkv = mkv_ref[...]
    ones = jnp.ones((mkv.shape[0], LANES), BF16)
    outs = []
    for h in range(MEM_HEADS):
        sl = slice(h * HEAD_DIM, (h + 1) * HEAD_DIM)
        s = _dot_nt(q[:, sl], mkv[:, sl])
        p = jnp.exp2(s - jnp.max(s, axis=-1, keepdims=True)).astype(BF16)
        o = _dot(p, mkv[:, MEM_WIDTH + h * HEAD_DIM:MEM_WIDTH + (h + 1) * HEAD_DIM])
        outs.append(o * (1.0 / _dot(p, ones)[:, :HEAD_DIM]))
    o_ref[...] = jnp.concatenate(outs, axis=-1).astype(o_ref.dtype)


def _mem_attention(proj, qcol, mkv, batch, seq, tm=512):
    tm = min(tm, seq)
    per_batch = seq // tm
    n_mem = mkv.shape[0] // batch
    return pl.pallas_call(
        _mem_attn_kernel,
        grid=(batch * per_batch,),
        in_specs=[pl.BlockSpec((tm, MEM_WIDTH), lambda i: (i, qcol)),
                  pl.BlockSpec((n_mem, 2 * MEM_WIDTH), lambda i: (i // per_batch, 0))],
        out_specs=pl.BlockSpec((tm, MEM_WIDTH), lambda i: (i, 0)),
        out_shape=jax.ShapeDtypeStruct((batch * seq, MEM_WIDTH), BF16),
        compiler_params=_params("parallel"),
        name="mem_attention",
    )(proj, mkv)


def _out_proj_kernel(mix_ref, mem_ref, w1_ref, w2_ref, x_ref, g_ref, o_ref):
    o = _dot(mix_ref[...], w1_ref[...]) + _dot(mem_ref[...], w2_ref[...])
    o_ref[...] = x_ref[...] + _rms(o, g_ref[...])


def _out_proj(mix, mem_out, w_out, x, g, tm=512):
    t, d = x.shape
    tm = min(tm, t)
    w1 = w_out[:MIX_WIDTH]
    w2 = w_out[MIX_WIDTH:]
    return pl.pallas_call(
        _out_proj_kernel,
        grid=(t // tm,),
        in_specs=[pl.BlockSpec((tm, MIX_WIDTH), lambda i: (i, 0)),
                  pl.BlockSpec((tm, MEM_WIDTH), lambda i: (i, 0)),
                  pl.BlockSpec((MIX_WIDTH, d), lambda i: (0, 0)),
                  pl.BlockSpec((MEM_WIDTH, d), lambda i: (0, 0)),
                  pl.BlockSpec((tm, d), lambda i: (i, 0)),
                  pl.BlockSpec((1, d), lambda i: (0, 0))],
        out_specs=pl.BlockSpec((tm, d), lambda i: (i, 0)),
        out_shape=jax.ShapeDtypeStruct((t, d), F32),
        compiler_params=_params("parallel"),
        name="out_proj",
    )(mix, mem_out, w1, w2, x, g.reshape(1, d))


def _swiglu_tile(h, wg_ref, wu_ref, wd_ref, chunk=256):
    out = None
    for c in range(wg_ref.shape[1] // chunk):
        sl = slice(c * chunk, (c + 1) * chunk)
        a = _dot(h, wg_ref[:, sl])
        u = _dot(h, wu_ref[:, sl])
        z = (a * jax.nn.sigmoid(a) * u).astype(BF16)
        y = _dot(z, wd_ref[sl, :])
        out = y if out is None else out + y
    return out


def _ffn_kernel(x_ref, gin_ref, gout_ref, wg_ref, wu_ref, wd_ref, o_ref, h_sc, acc_sc):
    f = pl.program_id(1)

    @pl.when(f == 0)
    def _():
        h_sc[...] = _rms(x_ref[...], gin_ref[...]).astype(BF16)
        acc_sc[...] = jnp.zeros_like(acc_sc)

    acc_sc[...] += _swiglu_tile(h_sc[...], wg_ref, wu_ref, wd_ref)

    @pl.when(f == pl.num_programs(1) - 1)
    def _():
        o_ref[...] = x_ref[...] + _rms(acc_sc[...], gout_ref[...])


def _ffn(x, g_in, g_out, wg, wu, wd, tm=512, tf=1792):
    t, d = x.shape
    ff = wg.shape[1]
    tm = min(tm, t)
    return pl.pallas_call(
        _ffn_kernel,
        grid=(t // tm, ff // tf),
        in_specs=[pl.BlockSpec((tm, d), lambda i, f: (i, 0)),
                  pl.BlockSpec((1, d), lambda i, f: (0, 0)),
                  pl.BlockSpec((1, d), lambda i, f: (0, 0)),
                  pl.BlockSpec((d, tf), lambda i, f: (0, f)),
                  pl.BlockSpec((d, tf), lambda i, f: (0, f)),
                  pl.BlockSpec((tf, d), lambda i, f: (f, 0))],
        out_specs=pl.BlockSpec((tm, d), lambda i, f: (i, 0)),
        out_shape=jax.ShapeDtypeStruct((t, d), F32),
        scratch_shapes=[pltpu.VMEM((tm, d), BF16), pltpu.VMEM((tm, d), F32)],
        compiler_params=_params("parallel", "arbitrary"),
        name="ffn",
    )(x, g_in.reshape(1, d), g_out.reshape(1, d), wg, wu, wd)


_POS_SPLIT = 256
_LEFT, _DIAG, _RIGHT = 0, 1, 2
_BOUND_LANE = 12
_CHUNK_LANE = 15
_OFFSET_LANE = 18
_ZERO_EXP2_ARG = 136.0
_MIN_DENOM = 2.0 ** -60


def _bf16_split(x):
    hi = x.astype(BF16)
    r = x - hi.astype(F32)
    mid = r.astype(BF16)
    lo = (r - mid.astype(F32)).astype(BF16)
    return hi, mid, lo


def _diff_attn_kernel(reach_ref, slope_ref, q_ref, k_ref, v_ref, qf_ref, kf_ref, dist_ref, lam_ref,
                      g_ref, o_ref, qa_sc, ka_sc, va_sc, k2_sc, m_sc, acc_sc, sa_sc, pa_sc, pb_sc, *, tk,
                      out_scale, lambda_init):
    h = pl.program_id(1)
    i = pl.program_id(2)
    tq = q_ref.shape[0]
    seq = k_ref.shape[0]
    nk = seq // tk
    slope = slope_ref[h]
    reach = reach_ref[h]

    @pl.when(i == 0)
    def _():
        k = k_ref[...]
        ka_sc[:, :B_VDIM] = k
        va_sc[:, :B_VDIM] = v_ref[...]
        lane = lax.broadcasted_iota(jnp.int32, (seq, LANES), 1)
        va_sc[:, B_VDIM:] = jnp.where(lane == 0, 1.0, 0.0).astype(BF16)
        kf = kf_ref[...]
        flane = lax.broadcasted_iota(jnp.int32, kf.shape, 1)
        chunk_lanes = jnp.logical_and(flane >= _CHUNK_LANE, flane < _CHUNK_LANE + 3)
        for c in range(nk):
            ka_sc[c * tk:(c + 1) * tk, B_VDIM:] = jnp.where(chunk_lanes, float(c), kf).astype(BF16)
        kk = k.astype(F32)
        kk = kk * kk
        for c, sel in enumerate((lane < HEAD_DIM, lane >= HEAD_DIM)):
            n2 = jnp.sum(jnp.where(sel, kk, 0.0), axis=-1, keepdims=True)
            k2_sc[c] = jnp.broadcast_to(jnp.max(n2, axis=0, keepdims=True), k2_sc.shape[1:])

    q = q_ref[...]
    lane = lax.broadcasted_iota(jnp.int32, q.shape, 1)
    zero = jnp.zeros_like(q)
    q0 = jnp.where(lane < HEAD_DIM, q, zero)
    q1 = jnp.where(lane >= HEAD_DIM, q, zero)
    qq = q.astype(F32)
    qq = qq * qq
    n0 = jnp.sum(jnp.where(lane < HEAD_DIM, qq, 0.0), axis=-1, keepdims=True)
    n1 = jnp.sum(jnp.where(lane >= HEAD_DIM, qq, 0.0), axis=-1, keepdims=True)
    bound = jnp.concatenate([jnp.sqrt(n0) * jnp.sqrt(k2_sc[0][0:1, 0:1]),
                             jnp.sqrt(n1) * jnp.sqrt(k2_sc[1][0:1, 0:1])], axis=0)
    off = slope * (i * tk).astype(F32)
    qf = qf_ref[...]
    qf2 = jnp.concatenate([qf, qf], axis=0)

    def piece_lanes(x, first_lane):
        lane_ = lax.broadcasted_iota(jnp.int32, x.shape, 1)
        out = jnp.zeros_like(x)
        for n, piece in enumerate(_bf16_split(x)):
            out = jnp.where(lane_ == first_lane + n, piece.astype(F32), out)
        return out.astype(BF16)

    def build_queries(with_bound):
        feats = {_LEFT: -qf2, _DIAG: jnp.zeros_like(qf2), _RIGHT: qf2}
        if with_bound:
            base = piece_lanes(jnp.broadcast_to(bound, (2 * tq, LANES)), _BOUND_LANE)
            side = piece_lanes(jnp.full((SUBLANES, LANES), off, F32), _OFFSET_LANE)[0:1]
            feats = {_LEFT: base + side - qf2, _DIAG: base, _RIGHT: base - side + qf2}
        for variant, feat in feats.items():
            qa_sc[variant, :, B_VDIM:] = feat
            qa_sc[variant, :tq, :B_VDIM] = q0
            qa_sc[variant, tq:, :B_VDIM] = q1

    def scores(j, s_ref):
        variant = jnp.where(j < i, _LEFT, jnp.where(j == i, _DIAG, _RIGHT))
        start = pl.multiple_of(j * tk, tk)
        s_ref[...] = _dot_nt(qa_sc[variant], ka_sc[pl.ds(start, tk), :])

    def diag_bias(j, s_ref):
        @pl.when(j == i)
        def _():
            bias = slope * dist_ref[...]
            s_ref[:tq] -= bias
            s_ref[tq:] -= bias

    def probs(j, p_ref):
        start = pl.multiple_of(j * tk, tk)
        s = _dot_nt(qa_sc[jnp.where(j < i, _LEFT, _RIGHT)], ka_sc[pl.ds(start, tk), :])
        p_ref[...] = jnp.exp2(s).astype(BF16)

    def values(j):
        return va_sc[pl.ds(pl.multiple_of(j * tk, tk), tk), :]

    build_queries(True)
    lo = jnp.maximum(i - reach, 0)
    count = jnp.minimum(i + reach + 1, nk) - lo - 1

    def off_diag(n):
        j = lo + jnp.minimum(n, count - 1)
        return j + (j >= i).astype(jnp.int32)

    s_diag = _dot_nt(qa_sc[_DIAG], ka_sc[pl.ds(pl.multiple_of(i * tk, tk), tk), :])
    probs(off_diag(0), pa_sc)
    bias = slope * dist_ref[...]
    p_diag = jnp.exp2(s_diag - jnp.concatenate([bias, bias], axis=0)).astype(BF16)
    acc_sc[...] = _dot(p_diag, values(i))

    def pair(tt, carry):
        n = 2 * tt
        probs(off_diag(n + 1), pb_sc)
        acc_sc[...] += _dot(pa_sc[...], values(off_diag(n)))
        probs(off_diag(n + 2), pa_sc)
        acc_sc[...] += _dot(pb_sc[...], values(off_diag(n + 1)))
        return carry

    lax.fori_loop(0, count // 2, pair, 0)

    @pl.when(count % 2 == 1)
    def _():
        acc_sc[...] += _dot(pa_sc[...], values(off_diag(count - 1)))

    denom_min = jnp.min(acc_sc[:, B_VDIM:B_VDIM + 1])

    @pl.when(jnp.logical_not(denom_min >= _MIN_DENOM))
    def _():
        build_queries(False)
        m_sc[...] = jnp.full_like(m_sc, NEG_INF)
        acc_sc[...] = jnp.zeros_like(acc_sc)

        def chunk(j, carry):
            scores(j, sa_sc)
            diag_bias(j, sa_sc)
            adj = jnp.where(j > i, off, jnp.where(j < i, -off, 0.0))
            start = pl.multiple_of(j * tk, tk)
            s = sa_sc[...]
            m_prev = m_sc[...]
            m_new = jnp.maximum(m_prev, jnp.max(s, axis=-1, keepdims=True) + adj)
            p = jnp.exp2(s - (m_new - adj)).astype(BF16)
            acc_sc[...] = jnp.exp2(m_prev - m_new) * acc_sc[...] + _dot(p, va_sc[pl.ds(start, tk), :])
            m_sc[...] = m_new
            return carry

        lax.fori_loop(0, nk, chunk, 0)

    lp = lam_ref[...]
    lam = (jnp.exp(jnp.sum(lp[0:1] * lp[1:2], axis=-1, keepdims=True))
           - jnp.exp(jnp.sum(lp[2:3] * lp[3:4], axis=-1, keepdims=True)) + lambda_init)
    acc = acc_sc[...]
    o0 = acc[:tq, :B_VDIM] * (1.0 / acc[:tq, B_VDIM:B_VDIM + 1])
    o1 = acc[tq:, :B_VDIM] * (1.0 / acc[tq:, B_VDIM:B_VDIM + 1])
    o_ref[...] = (_rms(o0 - lam * o1, g_ref[...]) * out_scale).astype(o_ref.dtype)


def _bf16_pieces(x, n=3):
    out = []
    r = np.float64(x)
    for _ in range(n):
        p = np.float64(np.float32(r).astype(jnp.bfloat16).astype(np.float32))
        out.append(p)
        r = r - p
    return out


def _alibi_features(slopes2, t):
    pos = np.arange(t)
    hi = (pos // _POS_SPLIT) * _POS_SPLIT
    lo = pos % _POS_SPLIT
    qf = np.zeros((len(slopes2), t, LANES), np.float32)
    kf = np.zeros((len(slopes2), t, LANES), np.float32)
    for h, s in enumerate(slopes2):
        for n, piece in enumerate(_bf16_pieces(s)):
            for base, part in ((0, hi), (3, lo)):
                qf[h, :, base + n] = -piece
                kf[h, :, base + n] = part
                qf[h, :, 6 + base + n] = part
                kf[h, :, 6 + base + n] = piece
        for n, piece in enumerate(_bf16_pieces(s * t)):
            qf[h, :, _CHUNK_LANE + n] = -piece
        kf[h, :, _BOUND_LANE:_BOUND_LANE + 3] = -1.0
        kf[h, :, _OFFSET_LANE:_OFFSET_LANE + 3] = -1.0
    return jnp.asarray(qf, BF16), jnp.asarray(kf, BF16)


def _diff_attention(proj, b_lambda, subln_g, lambda_init, batch, seq, t=512):
    t = min(t, seq // 2)
    nq = seq // t
    assert seq % (2 * t) == 0
    kcol = MIX_WIDTH // B_VDIM
    vcol = 2 * kcol
    slopes2 = [s * LOG2E for s in _alibi_slopes(B_HEADS)]
    reach = [min(nq, int(math.floor((_ZERO_EXP2_ARG / s - 1.0) / t)) + 1) for s in slopes2]
    qf, kf = _alibi_features(slopes2, t)
    pos = np.arange(t)
    dist = jnp.asarray(np.abs(pos[:, None] - pos[None, :]), F32)
    kernel = functools.partial(_diff_attn_kernel, tk=t, out_scale=1.0 - lambda_init,
                               lambda_init=lambda_init)
    return pl.pallas_call(
        kernel,
        grid_spec=pltpu.PrefetchScalarGridSpec(
            num_scalar_prefetch=2,
            grid=(batch, B_HEADS, nq),
            in_specs=[pl.BlockSpec((t, B_VDIM), lambda b, h, i, r, s: (b * nq + i, h)),
                      pl.BlockSpec((seq, B_VDIM), lambda b, h, i, r, s: (b, kcol + h)),
                      pl.BlockSpec((seq, B_VDIM), lambda b, h, i, r, s: (b, vcol + h)),
                      pl.BlockSpec((None, t, LANES), lambda b, h, i, r, s: (h, 0, 0)),
                      pl.BlockSpec((None, t, LANES), lambda b, h, i, r, s: (h, 0, 0)),
                      pl.BlockSpec((t, t), lambda b, h, i, r, s: (0, 0)),
                      pl.BlockSpec((4, HEAD_DIM), lambda b, h, i, r, s: (0, 0)),
                      pl.BlockSpec((1, B_VDIM), lambda b, h, i, r, s: (0, 0))],
            out_specs=pl.BlockSpec((t, B_VDIM), lambda b, h, i, r, s: (b * nq + i, h)),
            scratch_shapes=[pltpu.VMEM((3, 2 * t, 2 * LANES), BF16),
                            pltpu.VMEM((seq, 2 * LANES), BF16),
                            pltpu.VMEM((seq, 2 * LANES), BF16),
                            pltpu.VMEM((2, 8, LANES), F32),
                            pltpu.VMEM((2 * t, 1), F32),
                            pltpu.VMEM((2 * t, 2 * LANES), F32),
                            pltpu.VMEM((2 * t, t), F32),
                            pltpu.VMEM((2 * t, t), BF16),
                            pltpu.VMEM((2 * t, t), BF16)],
        ),
        out_shape=jax.ShapeDtypeStruct((batch * seq, MIX_WIDTH), BF16),
        compiler_params=_params("arbitrary", "arbitrary", "arbitrary"),
        name="diff_attention",
    )(jnp.asarray(reach, jnp.int32), jnp.asarray(slopes2, F32), proj, proj, proj, qf, kf, dist,
      b_lambda, subln_g.reshape(1, B_VDIM))


def _out_proj_router_kernel(mix_ref, mem_ref, w1_ref, w2_ref, x_ref, gout_ref, gin_ref, whi_ref,
                            wlo_ref, xo_ref, h_ref, e_ref, gate_ref):
    o = _dot(mix_ref[...], w1_ref[...]) + _dot(mem_ref[...], w2_ref[...])
    x = x_ref[...] + _rms(o, gout_ref[...])
    xo_ref[...] = x
    h = _rms(x, gin_ref[...])
    h_hi = h.astype(BF16)
    h_lo = (h - h_hi.astype(F32)).astype(BF16)
    whi = whi_ref[...]
    logits = _dot(h_hi, whi) + (_dot(h_hi, wlo_ref[...]) + _dot(h_lo, whi))
    lane = lax.broadcasted_iota(jnp.int32, logits.shape, 1)
    logits = jnp.where(lane < N_EXPERTS, logits, -jnp.inf)
    m1 = jnp.max(logits, axis=-1, keepdims=True)
    i1 = jnp.min(jnp.where(logits == m1, lane, LANES), axis=-1, keepdims=True)
    rest = jnp.where(lane == i1, -jnp.inf, logits)
    m2 = jnp.max(rest, axis=-1, keepdims=True)
    i2 = jnp.min(jnp.where(rest == m2, lane, LANES), axis=-1, keepdims=True)
    r = jnp.exp(m2 - m1)
    g1 = 1.0 / (1.0 + r)
    g2 = r * g1
    e_ref[...] = jnp.where(lane == 0, i1, jnp.where(lane == 1, i2, 0))
    gate_ref[...] = jnp.where(lane == 0, g1, jnp.where(lane == 1, g2, 0.0))
    h_ref[...] = h_hi


def _out_proj_router(mix, mem_out, w_out, x, g_out, g_in, w_router, tm=512):
    t, d = x.shape
    tm = min(tm, t)
    w1 = w_out[:MIX_WIDTH]
    w2 = w_out[MIX_WIDTH:]
    wpad = jnp.zeros((d, LANES), F32).at[:, :N_EXPERTS].set(w_router)
    whi = wpad.astype(BF16)
    wlo = (wpad - whi.astype(F32)).astype(BF16)
    const = lambda i: (0, 0)
    rows = lambda i: (i, 0)
    return pl.pallas_call(
        _out_proj_router_kernel,
        grid=(t // tm,),
        in_specs=[pl.BlockSpec((tm, MIX_WIDTH), rows),
                  pl.BlockSpec((tm, MEM_WIDTH), rows),
                  pl.BlockSpec((MIX_WIDTH, d), const),
                  pl.BlockSpec((MEM_WIDTH, d), const),
                  pl.BlockSpec((tm, d), rows),
                  pl.BlockSpec((1, d), const),
                  pl.BlockSpec((1, d), const),
                  pl.BlockSpec((d, LANES), const),
                  pl.BlockSpec((d, LANES), const)],
        out_specs=[pl.BlockSpec((tm, d), rows),
                   pl.BlockSpec((tm, d), rows),
                   pl.BlockSpec((tm, LANES), rows),
                   pl.BlockSpec((tm, LANES), rows)],
        out_shape=[jax.ShapeDtypeStruct((t, d), F32),
                   jax.ShapeDtypeStruct((t, d), BF16),
                   jax.ShapeDtypeStruct((t, LANES), jnp.int32),
                   jax.ShapeDtypeStruct((t, LANES), F32)],
        compiler_params=_params("parallel"),
        name="out_proj_router",
    )(mix, mem_out, w1, w2, x, g_out.reshape(1, d), g_in.reshape(1, d), whi, wlo)


def _slab_sizes(rows, tc):
    quarter, half = tc // 4, tc // 2
    return ((quarter, rows <= quarter),
            (half, jnp.logical_and(rows > quarter, rows <= half)),
            (tc, rows > half))


def _slab_stride(slab_ref, n, step, tc):
    most = slab_ref[(n + step) * N_EXPERTS]
    for e in range(1, N_EXPERTS):
        most = jnp.maximum(most, slab_ref[(n + step) * N_EXPERTS + e])
    return jnp.where(most <= tc // 2, tc // 2, tc)


def _dispatch_kernel(slab_ref, fill_ref, h_ref, dest_ref, gate_ref, xs_hbm, gs_hbm,
                     xslab_sc, gslab_sc, zx_sc, zg_sc, sem, zsem):
    i = pl.program_id(0)
    n = pl.num_programs(0)
    slot = i % 2
    tc = h_ref.shape[0]
    tm = zx_sc.shape[0]

    def slab_dma(step, sl, op):
        stride = _slab_stride(slab_ref, n, step, tc)
        for e in range(N_EXPERTS):
            start = pl.multiple_of(slab_ref[step * N_EXPERTS + e], SUBLANES)
            first = pl.multiple_of(e * stride, SUBLANES)
            for size, cond in _slab_sizes(slab_ref[(n + step) * N_EXPERTS + e], tc):
                @pl.when(cond)
                def _():
                    for src, dst in ((xslab_sc, xs_hbm), (gslab_sc, gs_hbm)):
                        copy = pltpu.make_async_copy(src.at[sl, pl.ds(first, size)],
                                                     dst.at[pl.ds(start, size)], sem.at[sl])
                        getattr(copy, op)()

    def fill_copies(start):
        start = pl.multiple_of(start, SUBLANES)
        return [pltpu.make_async_copy(zx_sc, xs_hbm.at[pl.ds(start, tm)], zsem),
                pltpu.make_async_copy(zg_sc, gs_hbm.at[pl.ds(start, tm)], zsem)]

    @pl.when(i == 0)
    def _():
        zx_sc[...] = jnp.zeros_like(zx_sc)
        zg_sc[...] = jnp.zeros_like(zg_sc)
        for half in range(2):
            for e in range(N_EXPERTS):
                for c in fill_copies(fill_ref[half * N_EXPERTS + e]):
                    c.start()
            for e in range(N_EXPERTS):
                for c in fill_copies(fill_ref[half * N_EXPERTS + e]):
                    c.wait()
        first_free = fill_ref[2 * N_EXPERTS]
        n_tiles = xs_hbm.shape[0] // tm

        def start_tile(b, carry):
            for c in fill_copies(b * tm):
                c.start()
            return carry

        def wait_tile(b, carry):
            for c in fill_copies(b * tm):
                c.wait()
            return carry

        lax.fori_loop(first_free, n_tiles, start_tile, 0)
        lax.fori_loop(first_free, n_tiles, wait_tile, 0)

    dest = dest_ref[0]
    gate = gate_ref[...]
    lane = lax.broadcasted_iota(jnp.int32, gate.shape, 1)

    def gate_pieces(col):
        out = jnp.zeros(gate.shape, F32)
        for n_, piece in enumerate(_bf16_split(col)):
            out = jnp.where(lane == n_, piece.astype(F32), out)
        return out.astype(BF16)

    def place_rows(stride):
        pos = lax.broadcasted_iota(jnp.int32, (stride, tc), 0)
        sel0, sel1 = [], []
        for e in range(N_EXPERTS):
            start = slab_ref[i * N_EXPERTS + e]
            sel0.append(jnp.where(dest[0:1, :] - start == pos, 1.0, 0.0))
            sel1.append(jnp.where(dest[1:2, :] - start == pos, 1.0, 0.0))
        sel0 = jnp.concatenate(sel0, axis=0).astype(BF16)
        sel1 = jnp.concatenate(sel1, axis=0).astype(BF16)
        rows = pl.ds(0, N_EXPERTS * stride)
        xslab_sc[slot, rows] = _dot(sel0 + sel1, h_ref[...])
        gslab_sc[slot, rows] = (_dot(sel0, gate_pieces(gate[:, 0:1]))
                                + _dot(sel1, gate_pieces(gate[:, 1:2])))

    compact = _slab_stride(slab_ref, n, i, tc) < tc
    pl.when(compact)(lambda: place_rows(tc // 2))
    pl.when(jnp.logical_not(compact))(lambda: place_rows(tc))

    @pl.when(i > 0)
    def _():
        slab_dma(i - 1, 1 - slot, "wait")

    slab_dma(i, slot, "start")

    @pl.when(i == n - 1)
    def _():
        slab_dma(i, slot, "wait")


def _dispatch(h, dest_t, gates, slab, fill, p_rows, tm, tc):
    t, d = h.shape
    nb = t // tc
    return pl.pallas_call(
        _dispatch_kernel,
        grid_spec=pltpu.PrefetchScalarGridSpec(
            num_scalar_prefetch=2,
            grid=(nb,),
            in_specs=[pl.BlockSpec((tc, d), lambda i, s, z: (i, 0)),
                      pl.BlockSpec((1, TOP_K, tc), lambda i, s, z: (i, 0, 0)),
                      pl.BlockSpec((tc, LANES), lambda i, s, z: (i, 0))],
            out_specs=[pl.BlockSpec(memory_space=pl.ANY), pl.BlockSpec(memory_space=pl.ANY)],
            scratch_shapes=[pltpu.VMEM((2, N_EXPERTS * tc, d), F32),
                            pltpu.VMEM((2, N_EXPERTS * tc, LANES), F32),
                            pltpu.VMEM((tm, d), F32), pltpu.VMEM((tm, LANES), F32),
                            pltpu.SemaphoreType.DMA((2,)), pltpu.SemaphoreType.DMA(())],
        ),
        out_shape=[jax.ShapeDtypeStruct((p_rows, d), F32), jax.ShapeDtypeStruct((p_rows, LANES), F32)],
        compiler_params=_params("arbitrary"),
        name="moe_dispatch",
    )(slab, fill, h, dest_t, gates)


def _moe_kernel(blk_e_ref, used_ref, xblk_ref, x_ref, gs_ref, wg_ref, wu_ref, wd_ref, o_ref,
                xb_sc, acc_sc):
    i = pl.program_id(0)
    f = pl.program_id(1)
    used = used_ref[i] > 0

    @pl.when(jnp.logical_and(used, f == 0))
    def _():
        xb_sc[...] = x_ref[...].astype(BF16)
        acc_sc[...] = jnp.zeros_like(acc_sc)

    @pl.when(used)
    def _():
        acc_sc[...] += _swiglu_tile(xb_sc[...], wg_ref, wu_ref, wd_ref)

    @pl.when(f == pl.num_programs(1) - 1)
    def _():
        gs = gs_ref[...]
        gate = gs[:, 0:1] + gs[:, 1:2] + gs[:, 2:3]
        o_ref[...] = jnp.where(used, acc_sc[...] * gate, 0.0)


def _moe_experts(xs, gs, blk_e, blk_used, xblk, wg, wu, wd, tm, tf=1792):
    p, d = xs.shape
    nblk = p // tm
    nf = wg.shape[2] // tf

    def fidx(i, f, used_ref):
        return jnp.where(used_ref[i] > 0, f, nf - 1)

    return pl.pallas_call(
        _moe_kernel,
        grid_spec=pltpu.PrefetchScalarGridSpec(
            num_scalar_prefetch=3,
            grid=(nblk, nf),
            in_specs=[pl.BlockSpec((tm, d), lambda i, f, be, us, xb: (xb[i], 0)),
                      pl.BlockSpec((tm, LANES), lambda i, f, be, us, xb: (xb[i], 0)),
                      pl.BlockSpec((None, d, tf), lambda i, f, be, us, xb: (be[i], 0, fidx(i, f, us))),
                      pl.BlockSpec((None, d, tf), lambda i, f, be, us, xb: (be[i], 0, fidx(i, f, us))),
                      pl.BlockSpec((None, tf, d), lambda i, f, be, us, xb: (be[i], fidx(i, f, us), 0))],
            out_specs=pl.BlockSpec((tm, d), lambda i, f, be, us, xb: (i, 0)),
            scratch_shapes=[pltpu.VMEM((tm, d), BF16), pltpu.VMEM((tm, d), F32)],
        ),
        out_shape=jax.ShapeDtypeStruct((p, d), F32),
        compiler_params=_params("arbitrary", "arbitrary"),
        name="moe_experts",
    )(blk_e, blk_used, xblk, xs, gs, wg, wu, wd)


def _combine_kernel(slab_ref, ys_hbm, dest_ref, x_ref, g_ref, o_ref, slab_sc, sem):
    i = pl.program_id(0)
    n = pl.num_programs(0)
    slot = i % 2
    tc = x_ref.shape[0]

    def slab_dma(step, sl, op):
        stride = _slab_stride(slab_ref, n, step, tc)
        for e in range(N_EXPERTS):
            start = pl.multiple_of(slab_ref[step * N_EXPERTS + e], SUBLANES)
            first = pl.multiple_of(e * stride, SUBLANES)
            for size, cond in _slab_sizes(slab_ref[(n + step) * N_EXPERTS + e], tc):
                @pl.when(cond)
                def _():
                    copy = pltpu.make_async_copy(ys_hbm.at[pl.ds(start, size)],
                                                 slab_sc.at[sl, pl.ds(first, size)], sem.at[sl])
                    getattr(copy, op)()

    @pl.when(i == 0)
    def _():
        slab_sc[...] = jnp.zeros_like(slab_sc)
        slab_dma(0, 0, "start")

    @pl.when(i + 1 < n)
    def _():
        slab_dma(i + 1, 1 - slot, "start")

    dest = dest_ref[...]
    slab_dma(i, slot, "wait")

    def gather_rows(stride):
        pos = lax.broadcasted_iota(jnp.int32, (tc, stride), 1)
        sel = []
        for e in range(N_EXPERTS):
            start = slab_ref[i * N_EXPERTS + e]
            hit = jnp.logical_or(dest[:, 0:1] - start == pos, dest[:, 1:2] - start == pos)
            sel.append(jnp.where(hit, 1.0, 0.0))
        sel = jnp.concatenate(sel, axis=1).astype(BF16)
        y = _dot(sel, slab_sc[slot, pl.ds(0, N_EXPERTS * stride)].astype(BF16))
        o_ref[...] = x_ref[...] + _rms(y, g_ref[...])

    compact = _slab_stride(slab_ref, n, i, tc) < tc
    pl.when(compact)(lambda: gather_rows(tc // 2))
    pl.when(jnp.logical_not(compact))(lambda: gather_rows(tc))


def _combine(ys, dest, slab, x, g, tc):
    t, d = x.shape
    return pl.pallas_call(
        _combine_kernel,
        grid_spec=pltpu.PrefetchScalarGridSpec(
            num_scalar_prefetch=1,
            grid=(t // tc,),
            in_specs=[pl.BlockSpec(memory_space=pl.ANY),
                      pl.BlockSpec((tc, LANES), lambda i, s: (i, 0)),
                      pl.BlockSpec((tc, d), lambda i, s: (i, 0)),
                      pl.BlockSpec((1, d), lambda i, s: (0, 0))],
            out_specs=pl.BlockSpec((tc, d), lambda i, s: (i, 0)),
            scratch_shapes=[pltpu.VMEM((2, N_EXPERTS * tc, d), F32), pltpu.SemaphoreType.DMA((2,))],
        ),
        out_shape=jax.ShapeDtypeStruct((t, d), F32),
        compiler_params=_params("arbitrary"),
        name="moe_combine",
    )(slab, ys, dest, x, g.reshape(1, d))


def _route(top_e, tm, tc):
    t = top_e.shape[0]
    a = t * TOP_K
    nb = t // tc
    e_flat = top_e.reshape(a)
    onehot = (e_flat[:, None] == jnp.arange(N_EXPERTS, dtype=jnp.int32)[None, :]).astype(jnp.int32)
    csum = jnp.cumsum(onehot, axis=0)
    rank = jnp.sum(onehot * (csum - 1), axis=1)
    upto = csum[TOP_K * tc - 1::TOP_K * tc]
    before = jnp.concatenate([jnp.zeros((1, N_EXPERTS), jnp.int32), upto[:nb - 1]], axis=0)
    rows = (upto - before + SUBLANES - 1) // SUBLANES * SUBLANES
    rows_before = jnp.cumsum(rows, axis=0) - rows
    total = jnp.sum(rows, axis=0)
    region = (total + 2 * tm - 1) // tm * tm
    rends = jnp.cumsum(region)
    rstarts = rends - region
    slab = (rstarts[None, :] + rows_before).astype(jnp.int32)
    shift = jnp.broadcast_to((slab - before)[:, None, :], (nb, TOP_K * tc, N_EXPERTS)).reshape(a, N_EXPERTS)
    dest = (rank + jnp.sum(onehot * shift, axis=1)).astype(jnp.int32).reshape(t, TOP_K)
    slab = jnp.concatenate([slab.reshape(nb * N_EXPERTS), rows.reshape(nb * N_EXPERTS)])
    row_end = rstarts + total
    fill = jnp.concatenate([row_end, rends - tm, rends[-1:] // tm]).astype(jnp.int32)
    nblk = -(-(a + nb * N_EXPERTS * (SUBLANES - 1)) // tm) + 2 * N_EXPERTS
    blk_start = jnp.arange(nblk, dtype=jnp.int32) * tm
    blk_e = jnp.minimum(jnp.sum((rends[None, :] <= blk_start[:, None]).astype(jnp.int32), axis=1),
                        N_EXPERTS - 1)
    blk_used = jnp.logical_and(blk_start < row_end[blk_e], blk_start < rends[-1]).astype(jnp.int32)
    first_used = jnp.argmax(blk_used).astype(jnp.int32)
    xblk = jnp.where(blk_used > 0, jnp.arange(nblk, dtype=jnp.int32), first_used)
    return dest, slab, fill, blk_e, blk_used, xblk, nblk * tm


def _moe(x, h, e_pad, gates, g_out, wg, wu, wd, tm=512, tc=256):
    t = x.shape[0]
    tc = min(tc, t)
    dest, slab, fill, blk_e, blk_used, xblk, p_rows = _route(e_pad[:, :TOP_K], tm, tc)
    dest_t = dest.reshape(t // tc, tc, TOP_K).transpose(0, 2, 1)
    dest_pad = jnp.zeros((t, LANES), jnp.int32).at[:, :TOP_K].set(dest)
    xs, gs = _dispatch(h, dest_t, gates, slab, fill, p_rows, tm, tc)
    ys = _moe_experts(xs, gs, blk_e, blk_used, xblk, wg, wu, wd, tm)
    return _combine(ys, dest_pad, slab, x, g_out, tc)


def kernel(x, mem, norm_g, mem_norm_g, w_mem_kv, a_w_in, a_sink, a_w_out, b_w_in, b_lambda,
           b_subln_g, b_w_out, ffn_w_gate, ffn_w_up, ffn_w_down, moe_w_router, moe_w_gate,
           moe_w_up, moe_w_down):
    batch, seq, d = x.shape
    n_mem = mem.shape[1]
    bf = lambda w: w.astype(BF16)
    xt = x.reshape(batch * seq, d)
    mkv = _norm_matmul(mem.reshape(batch * n_mem, d), mem_norm_g, bf(w_mem_kv),
                       jnp.ones((w_mem_kv.shape[1],), F32))

    g = norm_g[0]
    proj = _norm_matmul(xt, g[0], bf(a_w_in[0]), _query_col_scale(a_w_in.shape[2], A_Q_HEADS * HEAD_DIM))
    mix = _window_attention(proj, a_sink[0], batch, seq)
    mem_out = _mem_attention(proj, (A_Q_HEADS + 2 * A_KV_HEADS) * HEAD_DIM // MEM_WIDTH, mkv, batch, seq)
    xt = _out_proj(mix, mem_out, bf(a_w_out[0]), xt, g[1])
    xt = _ffn(xt, g[2], g[3], bf(ffn_w_gate[0]), bf(ffn_w_up[0]), bf(ffn_w_down[0]))

    g = norm_g[1]
    lambda_init = 0.8 - 0.6 * math.exp(-0.3 * 1)
    proj = _norm_matmul(xt, g[0], bf(b_w_in[0]), _query_col_scale(b_w_in.shape[2], MIX_WIDTH))
    mix = _diff_attention(proj, b_lambda[0], b_subln_g[0], lambda_init, batch, seq)
    mem_out = _mem_attention(proj, 3 * MIX_WIDTH // MEM_WIDTH, mkv, batch, seq)
    xt, h, e_pad, gates = _out_proj_router(mix, mem_out, bf(b_w_out[0]), xt, g[1], g[2], moe_w_router[0])
    xt = _moe(xt, h, e_pad, gates, g[3], bf(moe_w_gate[0]), bf(moe_w_up[0]), bf(moe_w_down[0]))
    return xt.reshape(batch, seq, d)
```

```python
import functools
import math

import jax
import jax.numpy as jnp
import numpy as np
from jax import lax
from jax.experimental import pallas as pl
from jax.experimental.pallas import tpu as pltpu

D_MODEL = 1024
HEAD_DIM = 64
MIX_WIDTH = 768
MEM_WIDTH = 256
MEM_HEADS = 4
A_Q_HEADS = 12
A_KV_HEADS = 4
A_GROUP = 3
WINDOW = 128
B_HEADS = 6
B_VDIM = 128
D_FF = 3584
N_EXPERTS = 8
TOP_K = 2
EPS = 1e-6
NEG_INF = -1e30
LOG2E = math.log2(math.e)
QK_SCALE = HEAD_DIM ** -0.5 * LOG2E
LANES = 128
SUBLANES = 8
VMEM_LIMIT_BYTES = 48 * 1024 * 1024

BF16 = jnp.bfloat16
F32 = jnp.float32


def _params(*sem):
    return pltpu.CompilerParams(dimension_semantics=sem, vmem_limit_bytes=VMEM_LIMIT_BYTES)


def _rms(x, g):
    return x * lax.rsqrt(jnp.mean(x * x, axis=-1, keepdims=True) + EPS) * g


def _dot(a, b):
    return jnp.dot(a, b, preferred_element_type=F32)


def _dot_nt(a, b):
    return lax.dot_general(a, b, (((1,), (1,)), ((), ())), preferred_element_type=F32)


def _alibi_slopes(n):
    return [2.0 ** (-8.0 * (i + 1) / n) for i in range(n)]


def _norm_matmul_kernel(x_ref, g_ref, w_ref, cs_ref, o_ref, *, chunk):
    h = _rms(x_ref[...], g_ref[...]).astype(BF16)
    for c in range(o_ref.shape[1] // chunk):
        sl = slice(c * chunk, (c + 1) * chunk)
        o_ref[:, sl] = (_dot(h, w_ref[:, sl]) * cs_ref[:, sl]).astype(o_ref.dtype)


def _norm_matmul(x, g, w, col_scale, tm=1024, chunk=512):
    t, d = x.shape
    n = w.shape[1]
    tm = min(tm, t)
    return pl.pallas_call(
        functools.partial(_norm_matmul_kernel, chunk=min(chunk, n)),
        grid=(t // tm,),
        in_specs=[pl.BlockSpec((tm, d), lambda i: (i, 0)),
                  pl.BlockSpec((1, d), lambda i: (0, 0)),
                  pl.BlockSpec((d, n), lambda i: (0, 0)),
                  pl.BlockSpec((1, n), lambda i: (0, 0))],
        out_specs=pl.BlockSpec((tm, n), lambda i: (i, 0)),
        out_shape=jax.ShapeDtypeStruct((t, n), BF16),
        compiler_params=_params("parallel"),
        name="norm_matmul",
    )(x, g.reshape(1, d), w, col_scale.reshape(1, n))


def _query_col_scale(n, mix_q_cols):
    cols = np.arange(n)
    return jnp.asarray(np.where((cols < mix_q_cols) | (cols >= n - MEM_WIDTH), QK_SCALE, 1.0), F32)


def _window_kernel(q_ref, kp_ref, kc_ref, kn_ref, vp_ref, vc_ref, vn_ref, bias_ref, o_ref):
    n = pl.program_id(1)
    nb = pl.num_programs(1)
    blk = kp_ref.shape[0]
    sub = q_ref.shape[0] // blk
    zeros = jnp.zeros((blk, kp_ref.shape[1]), kp_ref.dtype)
    k_all = jnp.concatenate([kp_ref[...], kc_ref[...], kn_ref[...]], axis=0)
    v_all = jnp.concatenate([vp_ref[...], vc_ref[...], vn_ref[...]], axis=0)
    col = lax.broadcasted_iota(jnp.int32, (1, 4 * blk), 1)
    ones = jnp.ones((4 * blk, LANES), BF16)
    for u in range(sub):
        k = jnp.concatenate([k_all[u * blk:(u + 3) * blk], zeros], axis=0)
        v = jnp.concatenate([v_all[u * blk:(u + 3) * blk], zeros], axis=0)
        edge = jnp.zeros((1, 4 * blk), F32)
        if u == 0:
            edge = jnp.where(jnp.logical_and(n == 0, col < blk), NEG_INF, edge)
        if u == sub - 1:
            edge = jnp.where(jnp.logical_and(n == nb - 1,
                                             jnp.logical_and(col >= 2 * blk, col < 3 * blk)), NEG_INF, edge)
        q = q_ref[u * blk:(u + 1) * blk, :]
        outs = []
        for kh in range(A_KV_HEADS):
            heads = range(kh * A_GROUP, (kh + 1) * A_GROUP)
            kv = slice(kh * HEAD_DIM, (kh + 1) * HEAD_DIM)
            qs = jnp.concatenate([q[:, h * HEAD_DIM:(h + 1) * HEAD_DIM] for h in heads], axis=0)
            s = _dot_nt(qs, k[:, kv]) + bias_ref[kh] + edge
            p = jnp.exp2(s - jnp.max(s, axis=-1, keepdims=True)).astype(BF16)
            o = _dot(p, v[:, kv]) * (1.0 / _dot(p, ones)[:, :HEAD_DIM])
            outs.extend(o[g * blk:(g + 1) * blk] for g in range(A_GROUP))
        o_ref[u * blk:(u + 1) * blk, :] = jnp.concatenate(outs, axis=-1).astype(o_ref.dtype)


def _window_bias(blk, sink):
    qi = np.arange(blk)[:, None]
    kj = np.arange(3 * blk)[None, :]
    dist = np.abs(blk + qi - kj)
    slopes = _alibi_slopes(A_Q_HEADS)
    tables = np.stack([np.where(dist <= WINDOW, -(slopes[h] * LOG2E) * dist, NEG_INF)
                       for h in range(A_Q_HEADS)])
    pad = jnp.full((A_Q_HEADS, blk, blk), NEG_INF, F32)
    pad = pad.at[:, :, 0].set(jnp.broadcast_to((sink.astype(F32) * LOG2E)[:, None], (A_Q_HEADS, blk)))
    full = jnp.concatenate([jnp.asarray(tables, F32), pad], axis=2)
    return full.reshape(A_KV_HEADS, A_GROUP * blk, 4 * blk)


def _window_attention(proj, sink, batch, seq, sub=2):
    blk = WINDOW
    nb = seq // blk
    sub = min(sub, nb)
    ns = nb // sub
    kcol = MIX_WIDTH // 256
    vcol = kcol + 1
    bias = _window_bias(blk, sink)

    def cur(col):
        return pl.BlockSpec((sub * blk, 256), lambda b, n: (b * ns + n, col))

    def prev(col):
        return pl.BlockSpec((blk, 256), lambda b, n: (b * nb + jnp.maximum(n * sub - 1, 0), col))

    def nxt(col):
        return pl.BlockSpec((blk, 256), lambda b, n: (b * nb + jnp.minimum((n + 1) * sub, nb - 1), col))

    return pl.pallas_call(
        _window_kernel,
        grid=(batch, ns),
        in_specs=[pl.BlockSpec((sub * blk, MIX_WIDTH), lambda b, n: (b * ns + n, 0)),
                  prev(kcol), cur(kcol), nxt(kcol), prev(vcol), cur(vcol), nxt(vcol),
                  pl.BlockSpec(bias.shape, lambda b, n: (0, 0, 0))],
        out_specs=pl.BlockSpec((sub * blk, MIX_WIDTH), lambda b, n: (b * ns + n, 0)),
        out_shape=jax.ShapeDtypeStruct((batch * seq, MIX_WIDTH), BF16),
        compiler_params=_params("parallel", "parallel"),
        name="window_attention",
    )(proj, proj, proj, proj, proj, proj, proj, bias)


def _mem_attn_kernel(q_ref, mkv_ref, o_ref):
    q = q_ref[...]
    mkv = mkv_ref[...]
    ones = jnp.ones((mkv.shape[0], LANES), BF16)
    outs = []
    for h in range(MEM_HEADS):
        sl = slice(h * HEAD_DIM, (h + 1) * HEAD_DIM)
        s = _dot_nt(q[:, sl], mkv[:, sl])
        p = jnp.exp2(s - jnp.max(s, axis=-1, keepdims=True)).astype(BF16)
        o = _dot(p, mkv[:, MEM_WIDTH + h * HEAD_DIM:MEM_WIDTH + (h + 1) * HEAD_DIM])
        outs.append(o * (1.0 / _dot(p, ones)[:, :HEAD_DIM]))
    o_ref[...] = jnp.concatenate(outs, axis=-1).astype(o_ref.dtype)


def _mem_attention(proj, qcol, mkv, batch, seq, tm=1024):
    tm = min(tm, seq)
    per_batch = seq // tm
    n_mem = mkv.shape[0] // batch
    return pl.pallas_call(
        _mem_attn_kernel,
        grid=(batch * per_batch,),
        in_specs=[pl.BlockSpec((tm, MEM_WIDTH), lambda i: (i, qcol)),
                  pl.BlockSpec((n_mem, 2 * MEM_WIDTH), lambda i: (i // per_batch, 0))],
        out_specs=pl.BlockSpec((tm, MEM_WIDTH), lambda i: (i, 0)),
        out_shape=jax.ShapeDtypeStruct((batch * seq, MEM_WIDTH), BF16),
        compiler_params=_params("parallel"),
        name="mem_attention",
    )(proj, mkv)


def _out_proj_kernel(mix_ref, mem_ref, w1_ref, w2_ref, x_ref, g_ref, o_ref):
    o = _dot(mix_ref[...], w1_ref[...]) + _dot(mem_ref[...], w2_ref[...])
    o_ref[...] = x_ref[...] + _rms(o, g_ref[...])


def _out_proj(mix, mem_out, w_out, x, g, tm=1024):
    t, d = x.shape
    tm = min(tm, t)
    w1 = w_out[:MIX_WIDTH]
    w2 = w_out[MIX_WIDTH:]
    return pl.pallas_call(
        _out_proj_kernel,
        grid=(t // tm,),
        in_specs=[pl.BlockSpec((tm, MIX_WIDTH), lambda i: (i, 0)),
                  pl.BlockSpec((tm, MEM_WIDTH), lambda i: (i, 0)),
                  pl.BlockSpec((MIX_WIDTH, d), lambda i: (0, 0)),
                  pl.BlockSpec((MEM_WIDTH, d), lambda i: (0, 0)),
                  pl.BlockSpec((tm, d), lambda i: (i, 0)),
                  pl.BlockSpec((1, d), lambda i: (0, 0))],
        out_specs=pl.BlockSpec((tm, d), lambda i: (i, 0)),
        out_shape=jax.ShapeDtypeStruct((t, d), F32),
        compiler_params=_params("parallel"),
        name="out_proj",
    )(mix, mem_out, w1, w2, x, g.reshape(1, d))


def _swiglu_tile(h, wg_ref, wu_ref, wd_ref, chunk=256):
    out = None
    for c in range(wg_ref.shape[1] // chunk):
        sl = slice(c * chunk, (c + 1) * chunk)
        a = _dot(h, wg_ref[:, sl])
        u = _dot(h, wu_ref[:, sl])
        z = (a * jax.nn.sigmoid(a) * u).astype(BF16)
        y = _dot(z, wd_ref[sl, :])
        out = y if out is None else out + y
    return out


def _ffn_kernel(x_ref, gin_ref, gout_ref, wg_ref, wu_ref, wd_ref, o_ref, h_sc, acc_sc):
    f = pl.program_id(1)

    @pl.when(f == 0)
    def _():
        h_sc[...] = _rms(x_ref[...], gin_ref[...]).astype(BF16)
        acc_sc[...] = jnp.zeros_like(acc_sc)

    acc_sc[...] += _swiglu_tile(h_sc[...], wg_ref, wu_ref, wd_ref)

    @pl.when(f == pl.num_programs(1) - 1)
    def _():
        o_ref[...] = x_ref[...] + _rms(acc_sc[...], gout_ref[...])


def _ffn(x, g_in, g_out, wg, wu, wd, tm=512, tf=1792):
    t, d = x.shape
    ff = wg.shape[1]
    tm = min(tm, t)
    return pl.pallas_call(
        _ffn_kernel,
        grid=(t // tm, ff // tf),
        in_specs=[pl.BlockSpec((tm, d), lambda i, f: (i, 0)),
                  pl.BlockSpec((1, d), lambda i, f: (0, 0)),
                  pl.BlockSpec((1, d), lambda i, f: (0, 0)),
                  pl.BlockSpec((d, tf), lambda i, f: (0, f)),
                  pl.BlockSpec((d, tf), lambda i, f: (0, f)),
                  pl.BlockSpec((tf, d), lambda i, f: (f, 0))],
        out_specs=pl.BlockSpec((tm, d), lambda i, f: (i, 0)),
        out_shape=jax.ShapeDtypeStruct((t, d), F32),
        scratch_shapes=[pltpu.VMEM((tm, d), BF16), pltpu.VMEM((tm, d), F32)],
        compiler_params=_params("parallel", "arbitrary"),
        name="ffn",
    )(x, g_in.reshape(1, d), g_out.reshape(1, d), wg, wu, wd)


_POS_SPLIT = 256
_LEFT, _DIAG, _RIGHT = 0, 1, 2
_BOUND_LANE = 12
_CHUNK_LANE = 15
_OFFSET_LANE = 18
_ZERO_EXP2_ARG = 136.0
_MIN_DENOM = 2.0 ** -60


def _bf16_split(x):
    hi = x.astype(BF16)
    r = x - hi.astype(F32)
    mid = r.astype(BF16)
    lo = (r - mid.astype(F32)).astype(BF16)
    return hi, mid, lo


def _diff_attn_kernel(reach_ref, slope_ref, q_ref, k_ref, v_ref, qf_ref, kf_ref, dist_ref, lam_ref,
                      g_ref, o_ref, qa_sc, ka_sc, va_sc, k2_sc, m_sc, acc_sc, sa_sc, pa_sc, pb_sc, *, tk,
                      out_scale, lambda_init):
    seq = k_ref.shape[0]
    nk = seq // tk
    k = k_ref[...]
    ka_sc[:, :B_VDIM] = k
    va_sc[:, :B_VDIM] = v_ref[...]
    lane = lax.broadcasted_iota(jnp.int32, (seq, LANES), 1)
    va_sc[:, B_VDIM:] = jnp.where(lane == 0, 1.0, 0.0).astype(BF16)
    kf = kf_ref[...]
    flane = lax.broadcasted_iota(jnp.int32, kf.shape, 1)
    chunk_lanes = jnp.logical_and(flane >= _CHUNK_LANE, flane < _CHUNK_LANE + 3)
    for c in range(nk):
        ka_sc[c * tk:(c + 1) * tk, B_VDIM:] = jnp.where(chunk_lanes, float(c), kf).astype(BF16)
    kk = k.astype(F32)
    kk = kk * kk
    for c, sel in enumerate((lane < HEAD_DIM, lane >= HEAD_DIM)):
        n2 = jnp.sum(jnp.where(sel, kk, 0.0), axis=-1, keepdims=True)
        k2_sc[c] = jnp.broadcast_to(jnp.max(n2, axis=0, keepdims=True), k2_sc.shape[1:])

    def body(i, carry):
        _diff_attn_block(i, reach_ref, slope_ref, q_ref, qf_ref, dist_ref, lam_ref, g_ref, o_ref, qa_sc,
                         ka_sc, va_sc, k2_sc, m_sc, acc_sc, sa_sc, pa_sc, pb_sc, tq=tk, nk=nk,
                         out_scale=out_scale, lambda_init=lambda_init)
        return carry

    lax.fori_loop(0, nk, body, 0)


def _diff_attn_block(i, reach_ref, slope_ref, q_ref, qf_ref, dist_ref, lam_ref, g_ref, o_ref, qa_sc,
                     ka_sc, va_sc, k2_sc, m_sc, acc_sc, sa_sc, pa_sc, pb_sc, *, tq, nk, out_scale,
                     lambda_init):
    h = pl.program_id(1)
    tk = tq
    slope = slope_ref[h]
    reach = reach_ref[h]
    rows = pl.ds(pl.multiple_of(i * tq, tq), tq)
    q = q_ref[rows, :]
    lane = lax.broadcasted_iota(jnp.int32, q.shape, 1)
    zero = jnp.zeros_like(q)
    q0 = jnp.where(lane < HEAD_DIM, q, zero)
    q1 = jnp.where(lane >= HEAD_DIM, q, zero)
    qq = q.astype(F32)
    qq = qq * qq
    n0 = jnp.sum(jnp.where(lane < HEAD_DIM, qq, 0.0), axis=-1, keepdims=True)
    n1 = jnp.sum(jnp.where(lane >= HEAD_DIM, qq, 0.0), axis=-1, keepdims=True)
    bound = jnp.concatenate([jnp.sqrt(n0) * jnp.sqrt(k2_sc[0][0:1, 0:1]),
                             jnp.sqrt(n1) * jnp.sqrt(k2_sc[1][0:1, 0:1])], axis=0)
    off = slope * (i * tk).astype(F32)
    qf = qf_ref[...]
    qf2 = jnp.concatenate([qf, qf], axis=0)

    def piece_lanes(x, first_lane):
        lane_ = lax.broadcasted_iota(jnp.int32, x.shape, 1)
        out = jnp.zeros_like(x)
        for n, piece in enumerate(_bf16_split(x)):
            out = jnp.where(lane_ == first_lane + n, piece.astype(F32), out)
        return out.astype(BF16)

    def build_queries(with_bound):
        feats = {_LEFT: -qf2, _DIAG: jnp.zeros_like(qf2), _RIGHT: qf2}
        if with_bound:
            base = piece_lanes(jnp.broadcast_to(bound, (2 * tq, LANES)), _BOUND_LANE)
            side = piece_lanes(jnp.full((SUBLANES, LANES), off, F32), _OFFSET_LANE)[0:1]
            feats = {_LEFT: base + side - qf2, _DIAG: base, _RIGHT: base - side + qf2}
        for variant, feat in feats.items():
            qa_sc[variant, :, B_VDIM:] = feat
            qa_sc[variant, :tq, :B_VDIM] = q0
            qa_sc[variant, tq:, :B_VDIM] = q1

    def scores(j, s_ref):
        variant = jnp.where(j < i, _LEFT, jnp.where(j == i, _DIAG, _RIGHT))
        start = pl.multiple_of(j * tk, tk)
        s_ref[...] = _dot_nt(qa_sc[variant], ka_sc[pl.ds(start, tk), :])

    def diag_bias(j, s_ref):
        @pl.when(j == i)
        def _():
            bias = slope * dist_ref[...]
            s_ref[:tq] -= bias
            s_ref[tq:] -= bias

    def probs(j, p_ref):
        start = pl.multiple_of(j * tk, tk)
        s = _dot_nt(qa_sc[jnp.where(j < i, _LEFT, _RIGHT)], ka_sc[pl.ds(start, tk), :])
        p_ref[...] = jnp.exp2(s).astype(BF16)

    def values(j):
        return va_sc[pl.ds(pl.multiple_of(j * tk, tk), tk), :]

    build_queries(True)
    lo = jnp.maximum(i - reach, 0)
    count = jnp.minimum(i + reach + 1, nk) - lo - 1

    def off_diag(n):
        j = lo + jnp.minimum(n, count - 1)
        return j + (j >= i).astype(jnp.int32)

    s_diag = _dot_nt(qa_sc[_DIAG], ka_sc[pl.ds(pl.multiple_of(i * tk, tk), tk), :])
    probs(off_diag(0), pa_sc)
    bias = slope * dist_ref[...]
    p_diag = jnp.exp2(s_diag - jnp.concatenate([bias, bias], axis=0)).astype(BF16)
    acc_sc[...] = _dot(p_diag, values(i))

    def pair(tt, carry):
        n = 2 * tt
        probs(off_diag(n + 1), pb_sc)
        acc_sc[...] += _dot(pa_sc[...], values(off_diag(n)))
        probs(off_diag(n + 2), pa_sc)
        acc_sc[...] += _dot(pb_sc[...], values(off_diag(n + 1)))
        return carry

    lax.fori_loop(0, count // 2, pair, 0)

    @pl.when(count % 2 == 1)
    def _():
        acc_sc[...] += _dot(pa_sc[...], values(off_diag(count - 1)))

    denom_min = jnp.min(acc_sc[:, B_VDIM:B_VDIM + 1])

    @pl.when(jnp.logical_not(denom_min >= _MIN_DENOM))
    def _():
        build_queries(False)
        m_sc[...] = jnp.full_like(m_sc, NEG_INF)
        acc_sc[...] = jnp.zeros_like(acc_sc)

        def chunk(j, carry):
            scores(j, sa_sc)
            diag_bias(j, sa_sc)
            adj = jnp.where(j > i, off, jnp.where(j < i, -off, 0.0))
            start = pl.multiple_of(j * tk, tk)
            s = sa_sc[...]
            m_prev = m_sc[...]
            m_new = jnp.maximum(m_prev, jnp.max(s, axis=-1, keepdims=True) + adj)
            p = jnp.exp2(s - (m_new - adj)).astype(BF16)
            acc_sc[...] = jnp.exp2(m_prev - m_new) * acc_sc[...] + _dot(p, va_sc[pl.ds(start, tk), :])
            m_sc[...] = m_new
            return carry

        lax.fori_loop(0, nk, chunk, 0)

    lp = lam_ref[...]
    lam = (jnp.exp(jnp.sum(lp[0:1] * lp[1:2], axis=-1, keepdims=True))
           - jnp.exp(jnp.sum(lp[2:3] * lp[3:4], axis=-1, keepdims=True)) + lambda_init)
    acc = acc_sc[...]
    o0 = acc[:tq, :B_VDIM] * (1.0 / acc[:tq, B_VDIM:B_VDIM + 1])
    o1 = acc[tq:, :B_VDIM] * (1.0 / acc[tq:, B_VDIM:B_VDIM + 1])
    o_ref[rows, :] = (_rms(o0 - lam * o1, g_ref[...]) * out_scale).astype(o_ref.dtype)


def _bf16_pieces(x, n=3):
    out = []
    r = np.float64(x)
    for _ in range(n):
        p = np.float64(np.float32(r).astype(jnp.bfloat16).astype(np.float32))
        out.append(p)
        r = r - p
    return out


def _alibi_features(slopes2, t):
    pos = np.arange(t)
    hi = (pos // _POS_SPLIT) * _POS_SPLIT
    lo = pos % _POS_SPLIT
    qf = np.zeros((len(slopes2), t, LANES), np.float32)
    kf = np.zeros((len(slopes2), t, LANES), np.float32)
    for h, s in enumerate(slopes2):
        for n, piece in enumerate(_bf16_pieces(s)):
            for base, part in ((0, hi), (3, lo)):
                qf[h, :, base + n] = -piece
                kf[h, :, base + n] = part
                qf[h, :, 6 + base + n] = part
                kf[h, :, 6 + base + n] = piece
        for n, piece in enumerate(_bf16_pieces(s * t)):
            qf[h, :, _CHUNK_LANE + n] = -piece
        kf[h, :, _BOUND_LANE:_BOUND_LANE + 3] = -1.0
        kf[h, :, _OFFSET_LANE:_OFFSET_LANE + 3] = -1.0
    return jnp.asarray(qf, BF16), jnp.asarray(kf, BF16)


def _diff_attention(proj, b_lambda, subln_g, lambda_init, batch, seq, t=512):
    t = min(t, seq // 2)
    nq = seq // t
    assert seq % (2 * t) == 0
    kcol = MIX_WIDTH // B_VDIM
    vcol = 2 * kcol
    slopes2 = [s * LOG2E for s in _alibi_slopes(B_HEADS)]
    reach = [min(nq, int(math.floor((_ZERO_EXP2_ARG / s - 1.0) / t)) + 1) for s in slopes2]
    qf, kf = _alibi_features(slopes2, t)
    pos = np.arange(t)
    dist = jnp.asarray(np.abs(pos[:, None] - pos[None, :]), F32)
    kernel = functools.partial(_diff_attn_kernel, tk=t, out_scale=1.0 - lambda_init,
                               lambda_init=lambda_init)
    return pl.pallas_call(
        kernel,
        grid_spec=pltpu.PrefetchScalarGridSpec(
            num_scalar_prefetch=2,
            grid=(batch, B_HEADS),
            in_specs=[pl.BlockSpec((seq, B_VDIM), lambda b, h, r, s: (b, h)),
                      pl.BlockSpec((seq, B_VDIM), lambda b, h, r, s: (b, kcol + h)),
                      pl.BlockSpec((seq, B_VDIM), lambda b, h, r, s: (b, vcol + h)),
                      pl.BlockSpec((None, t, LANES), lambda b, h, r, s: (h, 0, 0)),
                      pl.BlockSpec((None, t, LANES), lambda b, h, r, s: (h, 0, 0)),
                      pl.BlockSpec((t, t), lambda b, h, r, s: (0, 0)),
                      pl.BlockSpec((4, HEAD_DIM), lambda b, h, r, s: (0, 0)),
                      pl.BlockSpec((1, B_VDIM), lambda b, h, r, s: (0, 0))],
            out_specs=pl.BlockSpec((seq, B_VDIM), lambda b, h, r, s: (b, h)),
            scratch_shapes=[pltpu.VMEM((3, 2 * t, 2 * LANES), BF16),
                            pltpu.VMEM((seq, 2 * LANES), BF16),
                            pltpu.VMEM((seq, 2 * LANES), BF16),
                            pltpu.VMEM((2, 8, LANES), F32),
                            pltpu.VMEM((2 * t, 1), F32),
                            pltpu.VMEM((2 * t, 2 * LANES), F32),
                            pltpu.VMEM((2 * t, t), F32),
                            pltpu.VMEM((2 * t, t), BF16),
                            pltpu.VMEM((2 * t, t), BF16)],
        ),
        out_shape=jax.ShapeDtypeStruct((batch * seq, MIX_WIDTH), BF16),
        compiler_params=_params("parallel", "parallel"),
        name="diff_attention",
    )(jnp.asarray(reach, jnp.int32), jnp.asarray(slopes2, F32), proj, proj, proj, qf, kf, dist,
      b_lambda, subln_g.reshape(1, B_VDIM))


def _router_kernel(x_ref, g_ref, whi_ref, wlo_ref, h_ref, e_ref, gate_ref):
    h = _rms(x_ref[...], g_ref[...])
    h_hi = h.astype(BF16)
    h_lo = (h - h_hi.astype(F32)).astype(BF16)
    whi = whi_ref[...]
    logits = _dot(h_hi, whi) + (_dot(h_hi, wlo_ref[...]) + _dot(h_lo, whi))
    lane = lax.broadcasted_iota(jnp.int32, logits.shape, 1)
    logits = jnp.where(lane < N_EXPERTS, logits, -jnp.inf)
    m1 = jnp.max(logits, axis=-1, keepdims=True)
    i1 = jnp.min(jnp.where(logits == m1, lane, LANES), axis=-1, keepdims=True)
    rest = jnp.where(lane == i1, -jnp.inf, logits)
    m2 = jnp.max(rest, axis=-1, keepdims=True)
    i2 = jnp.min(jnp.where(rest == m2, lane, LANES), axis=-1, keepdims=True)
    r = jnp.exp(m2 - m1)
    g1 = 1.0 / (1.0 + r)
    g2 = r * g1
    e_ref[...] = jnp.where(lane == 0, i1, jnp.where(lane == 1, i2, 0))
    gate_ref[...] = jnp.where(lane == 0, g1, jnp.where(lane == 1, g2, 0.0))
    h_ref[...] = h_hi


def _router(x, g, w_router, tm=1024):
    t, d = x.shape
    tm = min(tm, t)
    wpad = jnp.zeros((d, LANES), F32).at[:, :N_EXPERTS].set(w_router)
    whi = wpad.astype(BF16)
    wlo = (wpad - whi.astype(F32)).astype(BF16)
    return pl.pallas_call(
        _router_kernel,
        grid=(t // tm,),
        in_specs=[pl.BlockSpec((tm, d), lambda i: (i, 0)),
                  pl.BlockSpec((1, d), lambda i: (0, 0)),
                  pl.BlockSpec((d, LANES), lambda i: (0, 0)),
                  pl.BlockSpec((d, LANES), lambda i: (0, 0))],
        out_specs=[pl.BlockSpec((tm, d), lambda i: (i, 0)),
                   pl.BlockSpec((tm, LANES), lambda i: (i, 0)),
                   pl.BlockSpec((tm, LANES), lambda i: (i, 0))],
        out_shape=[jax.ShapeDtypeStruct((t, d), BF16),
                   jax.ShapeDtypeStruct((t, LANES), jnp.int32),
                   jax.ShapeDtypeStruct((t, LANES), F32)],
        compiler_params=_params("parallel"),
        name="router",
    )(x, g.reshape(1, d), whi, wlo)


def _slab_sizes(rows, tc):
    quarter, half = tc // 4, tc // 2
    return ((quarter, rows <= quarter),
            (half, jnp.logical_and(rows > quarter, rows <= half)),
            (tc, rows > half))


def _slab_stride(slab_ref, n, step, tc):
    most = slab_ref[(n + step) * N_EXPERTS]
    for e in range(1, N_EXPERTS):
        most = jnp.maximum(most, slab_ref[(n + step) * N_EXPERTS + e])
    return jnp.where(most <= tc // 2, tc // 2, tc)


def _dispatch_kernel(slab_ref, fill_ref, h_ref, dest_ref, gate_ref, xs_hbm, gs_hbm,
                     xslab_sc, gslab_sc, zx_sc, zg_sc, sem, zsem):
    i = pl.program_id(0)
    n = pl.num_programs(0)
    slot = i % 2
    tc = h_ref.shape[0]
    tm = zx_sc.shape[0]

    def slab_dma(step, sl, op):
        stride = _slab_stride(slab_ref, n, step, tc)
        for e in range(N_EXPERTS):
            start = pl.multiple_of(slab_ref[step * N_EXPERTS + e], SUBLANES)
            first = pl.multiple_of(e * stride, SUBLANES)
            for size, cond in _slab_sizes(slab_ref[(n + step) * N_EXPERTS + e], tc):
                @pl.when(cond)
                def _():
                    for src, dst in ((xslab_sc, xs_hbm), (gslab_sc, gs_hbm)):
                        copy = pltpu.make_async_copy(src.at[sl, pl.ds(first, size)],
                                                     dst.at[pl.ds(start, size)], sem.at[sl])
                        getattr(copy, op)()

    def fill_copies(start):
        start = pl.multiple_of(start, SUBLANES)
        return [pltpu.make_async_copy(zx_sc, xs_hbm.at[pl.ds(start, tm)], zsem),
                pltpu.make_async_copy(zg_sc, gs_hbm.at[pl.ds(start, tm)], zsem)]

    @pl.when(i == 0)
    def _():
        zx_sc[...] = jnp.zeros_like(zx_sc)
        zg_sc[...] = jnp.zeros_like(zg_sc)
        for half in range(2):
            for e in range(N_EXPERTS):
                for c in fill_copies(fill_ref[half * N_EXPERTS + e]):
                    c.start()
            for e in range(N_EXPERTS):
                for c in fill_copies(fill_ref[half * N_EXPERTS + e]):
                    c.wait()
        first_free = fill_ref[2 * N_EXPERTS]
        n_tiles = xs_hbm.shape[0] // tm

        def start_tile(b, carry):
            for c in fill_copies(b * tm):
                c.start()
            return carry

        def wait_tile(b, carry):
            for c in fill_copies(b * tm):
                c.wait()
            return carry

        lax.fori_loop(first_free, n_tiles, start_tile, 0)
        lax.fori_loop(first_free, n_tiles, wait_tile, 0)

    dest = dest_ref[0]
    gate = gate_ref[...]
    lane = lax.broadcasted_iota(jnp.int32, gate.shape, 1)

    def gate_pieces(col):
        out = jnp.zeros(gate.shape, F32)
        for n_, piece in enumerate(_bf16_split(col)):
            out = jnp.where(lane == n_, piece.astype(F32), out)
        return out.astype(BF16)

    def place_rows(stride):
        pos = lax.broadcasted_iota(jnp.int32, (stride, tc), 0)
        sel0, sel1 = [], []
        for e in range(N_EXPERTS):
            start = slab_ref[i * N_EXPERTS + e]
            sel0.append(jnp.where(dest[0:1, :] - start == pos, 1.0, 0.0))
            sel1.append(jnp.where(dest[1:2, :] - start == pos, 1.0, 0.0))
        sel0 = jnp.concatenate(sel0, axis=0).astype(BF16)
        sel1 = jnp.concatenate(sel1, axis=0).astype(BF16)
        rows = pl.ds(0, N_EXPERTS * stride)
        xslab_sc[slot, rows] = _dot(sel0 + sel1, h_ref[...])
        gslab_sc[slot, rows] = (_dot(sel0, gate_pieces(gate[:, 0:1]))
                                + _dot(sel1, gate_pieces(gate[:, 1:2])))

    compact = _slab_stride(slab_ref, n, i, tc) < tc
    pl.when(compact)(lambda: place_rows(tc // 2))
    pl.when(jnp.logical_not(compact))(lambda: place_rows(tc))

    @pl.when(i > 0)
    def _():
        slab_dma(i - 1, 1 - slot, "wait")

    slab_dma(i, slot, "start")

    @pl.when(i == n - 1)
    def _():
        slab_dma(i, slot, "wait")


def _dispatch(h, dest_t, gates, slab, fill, p_rows, tm, tc):
    t, d = h.shape
    nb = t // tc
    return pl.pallas_call(
        _dispatch_kernel,
        grid_spec=pltpu.PrefetchScalarGridSpec(
            num_scalar_prefetch=2,
            grid=(nb,),
            in_specs=[pl.BlockSpec((tc, d), lambda i, s, z: (i, 0)),
                      pl.BlockSpec((1, TOP_K, tc), lambda i, s, z: (i, 0, 0)),
                      pl.BlockSpec((tc, LANES), lambda i, s, z: (i, 0))],
            out_specs=[pl.BlockSpec(memory_space=pl.ANY), pl.BlockSpec(memory_space=pl.ANY)],
            scratch_shapes=[pltpu.VMEM((2, N_EXPERTS * tc, d), F32),
                            pltpu.VMEM((2, N_EXPERTS * tc, LANES), F32),
                            pltpu.VMEM((tm, d), F32), pltpu.VMEM((tm, LANES), F32),
                            pltpu.SemaphoreType.DMA((2,)), pltpu.SemaphoreType.DMA(())],
        ),
        out_shape=[jax.ShapeDtypeStruct((p_rows, d), F32), jax.ShapeDtypeStruct((p_rows, LANES), F32)],
        compiler_params=_params("arbitrary"),
        name="moe_dispatch",
    )(slab, fill, h, dest_t, gates)


def _moe_kernel(blk_e_ref, used_ref, xblk_ref, x_ref, gs_ref, wg_ref, wu_ref, wd_ref, o_ref,
                xb_sc, acc_sc):
    i = pl.program_id(0)
    f = pl.program_id(1)
    used = used_ref[i] > 0

    @pl.when(jnp.logical_and(used, f == 0))
    def _():
        xb_sc[...] = x_ref[...].astype(BF16)
        acc_sc[...] = jnp.zeros_like(acc_sc)

    @pl.when(used)
    def _():
        acc_sc[...] += _swiglu_tile(xb_sc[...], wg_ref, wu_ref, wd_ref)

    @pl.when(f == pl.num_programs(1) - 1)
    def _():
        gs = gs_ref[...]
        gate = gs[:, 0:1] + gs[:, 1:2] + gs[:, 2:3]
        o_ref[...] = jnp.where(used, acc_sc[...] * gate, 0.0)


def _moe_experts(xs, gs, blk_e, blk_used, xblk, wg, wu, wd, tm, tf=1792):
    p, d = xs.shape
    nblk = p // tm
    nf = wg.shape[2] // tf

    def fidx(i, f, used_ref):
        return jnp.where(used_ref[i] > 0, f, nf - 1)

    return pl.pallas_call(
        _moe_kernel,
        grid_spec=pltpu.PrefetchScalarGridSpec(
            num_scalar_prefetch=3,
            grid=(nblk, nf),
            in_specs=[pl.BlockSpec((tm, d), lambda i, f, be, us, xb: (xb[i], 0)),
                      pl.BlockSpec((tm, LANES), lambda i, f, be, us, xb: (xb[i], 0)),
                      pl.BlockSpec((None, d, tf), lambda i, f, be, us, xb: (be[i], 0, fidx(i, f, us))),
                      pl.BlockSpec((None, d, tf), lambda i, f, be, us, xb: (be[i], 0, fidx(i, f, us))),
                      pl.BlockSpec((None, tf, d), lambda i, f, be, us, xb: (be[i], fidx(i, f, us), 0))],
            out_specs=pl.BlockSpec((tm, d), lambda i, f, be, us, xb: (i, 0)),
            scratch_shapes=[pltpu.VMEM((tm, d), BF16), pltpu.VMEM((tm, d), F32)],
        ),
        out_shape=jax.ShapeDtypeStruct((p, d), F32),
        compiler_params=_params("arbitrary", "arbitrary"),
        name="moe_experts",
    )(blk_e, blk_used, xblk, xs, gs, wg, wu, wd)


def _combine_kernel(slab_ref, ys_hbm, dest_ref, x_ref, g_ref, o_ref, slab_sc, sem):
    i = pl.program_id(0)
    n = pl.num_programs(0)
    slot = i % 2
    tc = x_ref.shape[0]

    def slab_dma(step, sl, op):
        stride = _slab_stride(slab_ref, n, step, tc)
        for e in range(N_EXPERTS):
            start = pl.multiple_of(slab_ref[step * N_EXPERTS + e], SUBLANES)
            first = pl.multiple_of(e * stride, SUBLANES)
            for size, cond in _slab_sizes(slab_ref[(n + step) * N_EXPERTS + e], tc):
                @pl.when(cond)
                def _():
                    copy = pltpu.make_async_copy(ys_hbm.at[pl.ds(start, size)],
                                                 slab_sc.at[sl, pl.ds(first, size)], sem.at[sl])
                    getattr(copy, op)()

    @pl.when(i == 0)
    def _():
        slab_sc[...] = jnp.zeros_like(slab_sc)
        slab_dma(0, 0, "start")

    @pl.when(i + 1 < n)
    def _():
        slab_dma(i + 1, 1 - slot, "start")

    dest = dest_ref[...]
    slab_dma(i, slot, "wait")

    def gather_rows(stride):
        pos = lax.broadcasted_iota(jnp.int32, (tc, stride), 1)
        sel = []
        for e in range(N_EXPERTS):
            start = slab_ref[i * N_EXPERTS + e]
            hit = jnp.logical_or(dest[:, 0:1] - start == pos, dest[:, 1:2] - start == pos)
            sel.append(jnp.where(hit, 1.0, 0.0))
        sel = jnp.concatenate(sel, axis=1).astype(BF16)
        y = _dot(sel, slab_sc[slot, pl.ds(0, N_EXPERTS * stride)].astype(BF16))
        o_ref[...] = x_ref[...] + _rms(y, g_ref[...])

    compact = _slab_stride(slab_ref, n, i, tc) < tc
    pl.when(compact)(lambda: gather_rows(tc // 2))
    pl.when(jnp.logical_not(compact))(lambda: gather_rows(tc))


def _combine(ys, dest, slab, x, g, tc):
    t, d = x.shape
    return pl.pallas_call(
        _combine_kernel,
        grid_spec=pltpu.PrefetchScalarGridSpec(
            num_scalar_prefetch=1,
            grid=(t // tc,),
            in_specs=[pl.BlockSpec(memory_space=pl.ANY),
                      pl.BlockSpec((tc, LANES), lambda i, s: (i, 0)),
                      pl.BlockSpec((tc, d), lambda i, s: (i, 0)),
                      pl.BlockSpec((1, d), lambda i, s: (0, 0))],
            out_specs=pl.BlockSpec((tc, d), lambda i, s: (i, 0)),
            scratch_shapes=[pltpu.VMEM((2, N_EXPERTS * tc, d), F32), pltpu.SemaphoreType.DMA((2,))],
        ),
        out_shape=jax.ShapeDtypeStruct((t, d), F32),
        compiler_params=_params("arbitrary"),
        name="moe_combine",
    )(slab, ys, dest, x, g.reshape(1, d))


def _route(top_e, tm, tc):
    t = top_e.shape[0]
    a = t * TOP_K
    nb = t // tc
    e_flat = top_e.reshape(a)
    onehot = (e_flat[:, None] == jnp.arange(N_EXPERTS, dtype=jnp.int32)[None, :]).astype(jnp.int32)
    csum = jnp.cumsum(onehot, axis=0)
    rank = jnp.sum(onehot * (csum - 1), axis=1)
    upto = csum[TOP_K * tc - 1::TOP_K * tc]
    before = jnp.concatenate([jnp.zeros((1, N_EXPERTS), jnp.int32), upto[:nb - 1]], axis=0)
    rows = (upto - before + SUBLANES - 1) // SUBLANES * SUBLANES
    rows_before = jnp.cumsum(rows, axis=0) - rows
    total = jnp.sum(rows, axis=0)
    region = (total + 2 * tm - 1) // tm * tm
    rends = jnp.cumsum(region)
    rstarts = rends - region
    slab = (rstarts[None, :] + rows_before).astype(jnp.int32)
    shift = jnp.broadcast_to((slab - before)[:, None, :], (nb, TOP_K * tc, N_EXPERTS)).reshape(a, N_EXPERTS)
    dest = (rank + jnp.sum(onehot * shift, axis=1)).astype(jnp.int32).reshape(t, TOP_K)
    slab = jnp.concatenate([slab.reshape(nb * N_EXPERTS), rows.reshape(nb * N_EXPERTS)])
    row_end = rstarts + total
    fill = jnp.concatenate([row_end, rends - tm, rends[-1:] // tm]).astype(jnp.int32)
    nblk = -(-(a + nb * N_EXPERTS * (SUBLANES - 1)) // tm) + 2 * N_EXPERTS
    blk_start = jnp.arange(nblk, dtype=jnp.int32) * tm
    blk_e = jnp.minimum(jnp.sum((rends[None, :] <= blk_start[:, None]).astype(jnp.int32), axis=1),
                        N_EXPERTS - 1)
    blk_used = jnp.logical_and(blk_start < row_end[blk_e], blk_start < rends[-1]).astype(jnp.int32)
    first_used = jnp.argmax(blk_used).astype(jnp.int32)
    xblk = jnp.where(blk_used > 0, jnp.arange(nblk, dtype=jnp.int32), first_used)
    return dest, slab, fill, blk_e, blk_used, xblk, nblk * tm


def _moe(x, h, e_pad, gates, g_out, wg, wu, wd, tm=512, tc=256):
    t = x.shape[0]
    tc = min(tc, t)
    dest, slab, fill, blk_e, blk_used, xblk, p_rows = _route(e_pad[:, :TOP_K], tm, tc)
    dest_t = dest.reshape(t // tc, tc, TOP_K).transpose(0, 2, 1)
    dest_pad = jnp.zeros((t, LANES), jnp.int32).at[:, :TOP_K].set(dest)
    xs, gs = _dispatch(h, dest_t, gates, slab, fill, p_rows, tm, tc)
    ys = _moe_experts(xs, gs, blk_e, blk_used, xblk, wg, wu, wd, tm)
    return _combine(ys, dest_pad, slab, x, g_out, tc)


def kernel(x, mem, norm_g, mem_norm_g, w_mem_kv, a_w_in, a_sink, a_w_out, b_w_in, b_lambda,
           b_subln_g, b_w_out, ffn_w_gate, ffn_w_up, ffn_w_down, moe_w_router, moe_w_gate,
           moe_w_up, moe_w_down):
    batch, seq, d = x.shape
    n_mem = mem.shape[1]
    bf = lambda w: w.astype(BF16)
    xt = x.reshape(batch * seq, d)
    mkv = _norm_matmul(mem.reshape(batch * n_mem, d), mem_norm_g, bf(w_mem_kv),
                       jnp.ones((w_mem_kv.shape[1],), F32))

    g = norm_g[0]
    proj = _norm_matmul(xt, g[0], bf(a_w_in[0]), _query_col_scale(a_w_in.shape[2], A_Q_HEADS * HEAD_DIM))
    mix = _window_attention(proj, a_sink[0], batch, seq)
    mem_out = _mem_attention(proj, (A_Q_HEADS + 2 * A_KV_HEADS) * HEAD_DIM // MEM_WIDTH, mkv, batch, seq)
    xt = _out_proj(mix, mem_out, bf(a_w_out[0]), xt, g[1])
    xt = _ffn(xt, g[2], g[3], bf(ffn_w_gate[0]), bf(ffn_w_up[0]), bf(ffn_w_down[0]))

    g = norm_g[1]
    lambda_init = 0.8 - 0.6 * math.exp(-0.3 * 1)
    proj = _norm_matmul(xt, g[0], bf(b_w_in[0]), _query_col_scale(b_w_in.shape[2], MIX_WIDTH))
    mix = _diff_attention(proj, b_lambda[0], b_subln_g[0], lambda_init, batch, seq)
    mem_out = _mem_attention(proj, 3 * MIX_WIDTH // MEM_WIDTH, mkv, batch, seq)
    xt = _out_proj(mix, mem_out, bf(b_w_out[0]), xt, g[1])
    h, e_pad, gates = _router(xt, g[2], moe_w_router[0])
    xt = _moe(xt, h, e_pad, gates, g[3], bf(moe_w_gate[0]), bf(moe_w_up[0]), bf(moe_w_down[0]))
    return xt.reshape(batch, seq, d)
```

```python
import functools
import math

import jax
import jax.numpy as jnp
import numpy as np
from jax import lax
from jax.experimental import pallas as pl
from jax.experimental.pallas import tpu as pltpu

D_MODEL = 1024
HEAD_DIM = 64
MIX_WIDTH = 768
MEM_WIDTH = 256
MEM_HEADS = 4
A_Q_HEADS = 12
A_KV_HEADS = 4
A_GROUP = 3
WINDOW = 128
B_HEADS = 6
B_VDIM = 128
D_FF = 3584
N_EXPERTS = 8
TOP_K = 2
EPS = 1e-6
NEG_INF = -1e30
LOG2E = math.log2(math.e)
QK_SCALE = HEAD_DIM ** -0.5 * LOG2E
LANES = 128
SUBLANES = 8
VMEM_LIMIT_BYTES = 56 * 1024 * 1024

BF16 = jnp.bfloat16
F32 = jnp.float32


def _params(*sem):
    return pltpu.CompilerParams(dimension_semantics=sem, vmem_limit_bytes=VMEM_LIMIT_BYTES)


def _rms(x, g):
    return x * lax.rsqrt(jnp.mean(x * x, axis=-1, keepdims=True) + EPS) * g


def _dot(a, b):
    return jnp.dot(a, b, preferred_element_type=F32)


def _dot_nt(a, b):
    return lax.dot_general(a, b, (((1,), (1,)), ((), ())), preferred_element_type=F32)


def _alibi_slopes(n):
    return [2.0 ** (-8.0 * (i + 1) / n) for i in range(n)]


def _norm_matmul_kernel(x_ref, g_ref, w_ref, cs_ref, o_ref, *, chunk):
    h = _rms(x_ref[...], g_ref[...]).astype(BF16)
    for c in range(o_ref.shape[1] // chunk):
        sl = slice(c * chunk, (c + 1) * chunk)
        o_ref[:, sl] = (_dot(h, w_ref[:, sl]) * cs_ref[:, sl]).astype(o_ref.dtype)


def _norm_matmul(x, g, w, col_scale, tm=1024, chunk=512):
    t, d = x.shape
    n = w.shape[1]
    tm = min(tm, t)
    return pl.pallas_call(
        functools.partial(_norm_matmul_kernel, chunk=min(chunk, n)),
        grid=(t // tm,),
        in_specs=[pl.BlockSpec((tm, d), lambda i: (i, 0)),
                  pl.BlockSpec((1, d), lambda i: (0, 0)),
                  pl.BlockSpec((d, n), lambda i: (0, 0)),
                  pl.BlockSpec((1, n), lambda i: (0, 0))],
        out_specs=pl.BlockSpec((tm, n), lambda i: (i, 0)),
        out_shape=jax.ShapeDtypeStruct((t, n), BF16),
        compiler_params=_params("parallel"),
        name="norm_matmul",
    )(x, g.reshape(1, d), w, col_scale.reshape(1, n))


def _query_col_scale(n, mix_q_cols):
    cols = np.arange(n)
    return jnp.asarray(np.where((cols < mix_q_cols) | (cols >= n - MEM_WIDTH), QK_SCALE, 1.0), F32)


def _window_kernel(q_ref, kp_ref, kc_ref, kn_ref, vp_ref, vc_ref, vn_ref, bias_ref, o_ref):
    n = pl.program_id(1)
    nb = pl.num_programs(1)
    blk = kp_ref.shape[0]
    sub = q_ref.shape[0] // blk
    zeros = jnp.zeros((blk, kp_ref.shape[1]), kp_ref.dtype)
    k_all = jnp.concatenate([kp_ref[...], kc_ref[...], kn_ref[...]], axis=0)
    v_all = jnp.concatenate([vp_ref[...], vc_ref[...], vn_ref[...]], axis=0)
    col = lax.broadcasted_iota(jnp.int32, (1, 4 * blk), 1)
    ones = jnp.ones((4 * blk, LANES), BF16)
    for u in range(sub):
        k = jnp.concatenate([k_all[u * blk:(u + 3) * blk], zeros], axis=0)
        v = jnp.concatenate([v_all[u * blk:(u + 3) * blk], zeros], axis=0)
        edge = jnp.zeros((1, 4 * blk), F32)
        if u == 0:
            edge = jnp.where(jnp.logical_and(n == 0, col < blk), NEG_INF, edge)
        if u == sub - 1:
            edge = jnp.where(jnp.logical_and(n == nb - 1,
                                             jnp.logical_and(col >= 2 * blk, col < 3 * blk)), NEG_INF, edge)
        q = q_ref[u * blk:(u + 1) * blk, :]
        outs = []
        for kh in range(A_KV_HEADS):
            heads = range(kh * A_GROUP, (kh + 1) * A_GROUP)
            kv = slice(kh * HEAD_DIM, (kh + 1) * HEAD_DIM)
            qs = jnp.concatenate([q[:, h * HEAD_DIM:(h + 1) * HEAD_DIM] for h in heads], axis=0)
            s = _dot_nt(qs, k[:, kv]) + bias_ref[kh] + edge
            p = jnp.exp2(s - jnp.max(s, axis=-1, keepdims=True)).astype(BF16)
            o = _dot(p, v[:, kv]) * (1.0 / _dot(p, ones)[:, :HEAD_DIM])
            outs.extend(o[g * blk:(g + 1) * blk] for g in range(A_GROUP))
        o_ref[u * blk:(u + 1) * blk, :] = jnp.concatenate(outs, axis=-1).astype(o_ref.dtype)


def _window_bias(blk, sink):
    qi = np.arange(blk)[:, None]
    kj = np.arange(3 * blk)[None, :]
    dist = np.abs(blk + qi - kj)
    slopes = _alibi_slopes(A_Q_HEADS)
    tables = np.stack([np.where(dist <= WINDOW, -(slopes[h] * LOG2E) * dist, NEG_INF)
                       for h in range(A_Q_HEADS)])
    pad = jnp.full((A_Q_HEADS, blk, blk), NEG_INF, F32)
    pad = pad.at[:, :, 0].set(jnp.broadcast_to((sink.astype(F32) * LOG2E)[:, None], (A_Q_HEADS, blk)))
    full = jnp.concatenate([jnp.asarray(tables, F32), pad], axis=2)
    return full.reshape(A_KV_HEADS, A_GROUP * blk, 4 * blk)


def _window_attention(proj, sink, batch, seq, sub=2):
    blk = WINDOW
    nb = seq // blk
    sub = min(sub, nb)
    ns = nb // sub
    kcol = MIX_WIDTH // 256
    vcol = kcol + 1
    bias = _window_bias(blk, sink)

    def cur(col):
        return pl.BlockSpec((sub * blk, 256), lambda b, n: (b * ns + n, col))

    def prev(col):
        return pl.BlockSpec((blk, 256), lambda b, n: (b * nb + jnp.maximum(n * sub - 1, 0), col))

    def nxt(col):
        return pl.BlockSpec((blk, 256), lambda b, n: (b * nb + jnp.minimum((n + 1) * sub, nb - 1), col))

    return pl.pallas_call(
        _window_kernel,
        grid=(batch, ns),
        in_specs=[pl.BlockSpec((sub * blk, MIX_WIDTH), lambda b, n: (b * ns + n, 0)),
                  prev(kcol), cur(kcol), nxt(kcol), prev(vcol), cur(vcol), nxt(vcol),
                  pl.BlockSpec(bias.shape, lambda b, n: (0, 0, 0))],
        out_specs=pl.BlockSpec((sub * blk, MIX_WIDTH), lambda b, n: (b * ns + n, 0)),
        out_shape=jax.ShapeDtypeStruct((batch * seq, MIX_WIDTH), BF16),
        compiler_params=_params("parallel", "parallel"),
        name="window_attention",
    )(proj, proj, proj, proj, proj, proj, proj, bias)


def _mem_attn_kernel(q_ref, mkv_ref, o_ref):
    q = q_ref[...]
    mkv = mkv_ref[...]
    ones = jnp.ones((mkv.shape[0], LANES), BF16)
    outs = []
    for h in range(MEM_HEADS):
        sl = slice(h * HEAD_DIM, (h + 1) * HEAD_DIM)
        s = _dot_nt(q[:, sl], mkv[:, sl])
        p = jnp.exp2(s - jnp.max(s, axis=-1, keepdims=True)).astype(BF16)
        o = _dot(p, mkv[:, MEM_WIDTH + h * HEAD_DIM:MEM_WIDTH + (h + 1) * HEAD_DIM])
        outs.append(o * (1.0 / _dot(p, ones)[:, :HEAD_DIM]))
    o_ref[...] = jnp.concatenate(outs, axis=-1).astype(o_ref.dtype)


def _mem_attention(proj, qcol, mkv, batch, seq, tm=2048):
    tm = min(tm, seq)
    per_batch = seq // tm
    n_mem = mkv.shape[0] // batch
    return pl.pallas_call(
        _mem_attn_kernel,
        grid=(batch * per_batch,),
        in_specs=[pl.BlockSpec((tm, MEM_WIDTH), lambda i: (i, qcol)),
                  pl.BlockSpec((n_mem, 2 * MEM_WIDTH), lambda i: (i // per_batch, 0))],
        out_specs=pl.BlockSpec((tm, MEM_WIDTH), lambda i: (i, 0)),
        out_shape=jax.ShapeDtypeStruct((batch * seq, MEM_WIDTH), BF16),
        compiler_params=_params("parallel"),
        name="mem_attention",
    )(proj, mkv)


def _out_proj_kernel(mix_ref, mem_ref, w1_ref, w2_ref, x_ref, g_ref, o_ref):
    o = _dot(mix_ref[...], w1_ref[...]) + _dot(mem_ref[...], w2_ref[...])
    o_ref[...] = x_ref[...] + _rms(o, g_ref[...])


def _out_proj(mix, mem_out, w_out, x, g, tm=1024):
    t, d = x.shape
    tm = min(tm, t)
    w1 = w_out[:MIX_WIDTH]
    w2 = w_out[MIX_WIDTH:]
    return pl.pallas_call(
        _out_proj_kernel,
        grid=(t // tm,),
        in_specs=[pl.BlockSpec((tm, MIX_WIDTH), lambda i: (i, 0)),
                  pl.BlockSpec((tm, MEM_WIDTH), lambda i: (i, 0)),
                  pl.BlockSpec((MIX_WIDTH, d), lambda i: (0, 0)),
                  pl.BlockSpec((MEM_WIDTH, d), lambda i: (0, 0)),
                  pl.BlockSpec((tm, d), lambda i: (i, 0)),
                  pl.BlockSpec((1, d), lambda i: (0, 0))],
        out_specs=pl.BlockSpec((tm, d), lambda i: (i, 0)),
        out_shape=jax.ShapeDtypeStruct((t, d), F32),
        compiler_params=_params("parallel"),
        name="out_proj",
    )(mix, mem_out, w1, w2, x, g.reshape(1, d))


def _swiglu_tile(h, wg_ref, wu_ref, wd_ref, chunk=256):
    out = None
    for c in range(wg_ref.shape[1] // chunk):
        sl = slice(c * chunk, (c + 1) * chunk)
        a = _dot(h, wg_ref[:, sl])
        u = _dot(h, wu_ref[:, sl])
        z = (a * jax.nn.sigmoid(a) * u).astype(BF16)
        y = _dot(z, wd_ref[sl, :])
        out = y if out is None else out + y
    return out


def _ffn_kernel(x_ref, gin_ref, gout_ref, wg_ref, wu_ref, wd_ref, o_ref, h_sc, acc_sc):
    f = pl.program_id(1)

    @pl.when(f == 0)
    def _():
        h_sc[...] = _rms(x_ref[...], gin_ref[...]).astype(BF16)
        acc_sc[...] = jnp.zeros_like(acc_sc)

    acc_sc[...] += _swiglu_tile(h_sc[...], wg_ref, wu_ref, wd_ref)

    @pl.when(f == pl.num_programs(1) - 1)
    def _():
        o_ref[...] = x_ref[...] + _rms(acc_sc[...], gout_ref[...])


def _ffn(x, g_in, g_out, wg, wu, wd, tm=1024, tf=1792):
    t, d = x.shape
    ff = wg.shape[1]
    tm = min(tm, t)
    return pl.pallas_call(
        _ffn_kernel,
        grid=(t // tm, ff // tf),
        in_specs=[pl.BlockSpec((tm, d), lambda i, f: (i, 0)),
                  pl.BlockSpec((1, d), lambda i, f: (0, 0)),
                  pl.BlockSpec((1, d), lambda i, f: (0, 0)),
                  pl.BlockSpec((d, tf), lambda i, f: (0, f)),
                  pl.BlockSpec((d, tf), lambda i, f: (0, f)),
                  pl.BlockSpec((tf, d), lambda i, f: (f, 0))],
        out_specs=pl.BlockSpec((tm, d), lambda i, f: (i, 0)),
        out_shape=jax.ShapeDtypeStruct((t, d), F32),
        scratch_shapes=[pltpu.VMEM((tm, d), BF16), pltpu.VMEM((tm, d), F32)],
        compiler_params=_params("parallel", "arbitrary"),
        name="ffn",
    )(x, g_in.reshape(1, d), g_out.reshape(1, d), wg, wu, wd)


_POS_SPLIT = 256
_LEFT, _DIAG, _RIGHT = 0, 1, 2
_BOUND_LANE = 12
_CHUNK_LANE = 15
_OFFSET_LANE = 18
_ZERO_EXP2_ARG = 136.0
_MIN_DENOM = 2.0 ** -60


def _bf16_split(x):
    hi = x.astype(BF16)
    r = x - hi.astype(F32)
    mid = r.astype(BF16)
    lo = (r - mid.astype(F32)).astype(BF16)
    return hi, mid, lo


def _diff_attn_kernel(reach_ref, slope_ref, q_ref, k_ref, v_ref, qf_ref, kf_ref, dist_ref, lam_ref,
                      g_ref, o_ref, qa_sc, ka_sc, va_sc, k2_sc, m_sc, acc_sc, sa_sc, pa_sc, pb_sc, *, tk,
                      out_scale, lambda_init):
    seq = k_ref.shape[0]
    nk = seq // tk
    k = k_ref[...]
    ka_sc[:, :B_VDIM] = k
    va_sc[:, :B_VDIM] = v_ref[...]
    lane = lax.broadcasted_iota(jnp.int32, (seq, LANES), 1)
    va_sc[:, B_VDIM:] = jnp.where(lane == 0, 1.0, 0.0).astype(BF16)
    kf = kf_ref[...]
    flane = lax.broadcasted_iota(jnp.int32, kf.shape, 1)
    chunk_lanes = jnp.logical_and(flane >= _CHUNK_LANE, flane < _CHUNK_LANE + 3)
    for c in range(nk):
        ka_sc[c * tk:(c + 1) * tk, B_VDIM:] = jnp.where(chunk_lanes, float(c), kf).astype(BF16)
    kk = k.astype(F32)
    kk = kk * kk
    for c, sel in enumerate((lane < HEAD_DIM, lane >= HEAD_DIM)):
        n2 = jnp.sum(jnp.where(sel, kk, 0.0), axis=-1, keepdims=True)
        k2_sc[c] = jnp.broadcast_to(jnp.max(n2, axis=0, keepdims=True), k2_sc.shape[1:])

    def body(i, carry):
        i = jnp.asarray(i, jnp.int32)
        _diff_attn_block(i, reach_ref, slope_ref, q_ref, qf_ref, dist_ref, lam_ref, g_ref, o_ref, qa_sc,
                         ka_sc, va_sc, k2_sc, m_sc, acc_sc, sa_sc, pa_sc, pb_sc, tq=tk, nk=nk,
                         out_scale=out_scale, lambda_init=lambda_init)
        return carry

    lax.fori_loop(0, nk, body, 0)


def _diff_attn_block(i, reach_ref, slope_ref, q_ref, qf_ref, dist_ref, lam_ref, g_ref, o_ref, qa_sc,
                     ka_sc, va_sc, k2_sc, m_sc, acc_sc, sa_sc, pa_sc, pb_sc, *, tq, nk, out_scale,
                     lambda_init):
    h = pl.program_id(1)
    tk = tq
    slope = slope_ref[h]
    reach = reach_ref[h]
    rows = pl.ds(pl.multiple_of(i * tq, tq), tq)
    q = q_ref[rows, :]
    lane = lax.broadcasted_iota(jnp.int32, q.shape, 1)
    zero = jnp.zeros_like(q)
    q0 = jnp.where(lane < HEAD_DIM, q, zero)
    q1 = jnp.where(lane >= HEAD_DIM, q, zero)
    qq = q.astype(F32)
    qq = (qq * qq).astype(BF16)
    comp = lax.broadcasted_iota(jnp.int32, (B_VDIM, LANES), 0)
    n0 = _dot(qq, jnp.where(comp < HEAD_DIM, 1.0, 0.0).astype(BF16)) * (1.0 + 2.0 ** -7)
    n1 = _dot(qq, jnp.where(comp >= HEAD_DIM, 1.0, 0.0).astype(BF16)) * (1.0 + 2.0 ** -7)
    bound = jnp.concatenate([jnp.sqrt(n0) * jnp.sqrt(k2_sc[0][0:1, :]),
                             jnp.sqrt(n1) * jnp.sqrt(k2_sc[1][0:1, :])], axis=0)
    off = slope * (i * tk).astype(F32)
    qf = qf_ref[...]
    qf2 = jnp.concatenate([qf, qf], axis=0)

    def piece_lanes(x, first_lane):
        lane_ = lax.broadcasted_iota(jnp.int32, x.shape, 1)
        out = jnp.zeros_like(x)
        for n, piece in enumerate(_bf16_split(x)):
            out = jnp.where(lane_ == first_lane + n, piece.astype(F32), out)
        return out.astype(BF16)

    def build_queries(with_bound):
        feats = {_LEFT: -qf2, _DIAG: jnp.zeros_like(qf2), _RIGHT: qf2}
        if with_bound:
            base = piece_lanes(bound, _BOUND_LANE)
            side = piece_lanes(jnp.full((SUBLANES, LANES), off, F32), _OFFSET_LANE)[0:1]
            feats = {_LEFT: base + side - qf2, _DIAG: base, _RIGHT: base - side + qf2}
        for variant, feat in feats.items():
            qa_sc[variant, :, B_VDIM:] = feat
            qa_sc[variant, :tq, :B_VDIM] = q0
            qa_sc[variant, tq:, :B_VDIM] = q1

    def scores(j, s_ref):
        variant = jnp.where(j < i, _LEFT, jnp.where(j == i, _DIAG, _RIGHT))
        start = pl.multiple_of(j * tk, tk)
        s_ref[...] = _dot_nt(qa_sc[variant], ka_sc[pl.ds(start, tk), :])

    def diag_bias(j, s_ref):
        @pl.when(j == i)
        def _():
            bias = slope * dist_ref[...]
            s_ref[:tq] -= bias
            s_ref[tq:] -= bias

    def probs(j, p_ref):
        start = pl.multiple_of(j * tk, tk)
        s = _dot_nt(qa_sc[jnp.where(j < i, _LEFT, _RIGHT)], ka_sc[pl.ds(start, tk), :])
        p_ref[...] = jnp.exp2(s).astype(BF16)

    def values(j):
        return va_sc[pl.ds(pl.multiple_of(j * tk, tk), tk), :]

    build_queries(True)
    lo = jnp.maximum(i - reach, 0)
    count = jnp.minimum(i + reach + 1, nk) - lo - 1

    def off_diag(n):
        j = lo + jnp.minimum(n, count - 1)
        return j + (j >= i).astype(jnp.int32)

    s_diag = _dot_nt(qa_sc[_DIAG], ka_sc[pl.ds(pl.multiple_of(i * tk, tk), tk), :])
    probs(off_diag(0), pa_sc)
    bias = slope * dist_ref[...]
    p_diag = jnp.exp2(s_diag - jnp.concatenate([bias, bias], axis=0)).astype(BF16)
    acc_sc[...] = _dot(p_diag, values(i))

    def pair(tt, carry):
        n = 2 * tt
        probs(off_diag(n + 1), pb_sc)
        acc_sc[...] += _dot(pa_sc[...], values(off_diag(n)))
        probs(off_diag(n + 2), pa_sc)
        acc_sc[...] += _dot(pb_sc[...], values(off_diag(n + 1)))
        return carry

    lax.fori_loop(0, count // 2, pair, 0)

    @pl.when(count % 2 == 1)
    def _():
        acc_sc[...] += _dot(pa_sc[...], values(off_diag(count - 1)))

    denom_min = jnp.min(acc_sc[:, B_VDIM:B_VDIM + 1])

    @pl.when(jnp.logical_not(denom_min >= _MIN_DENOM))
    def _():
        build_queries(False)
        m_sc[...] = jnp.full_like(m_sc, NEG_INF)
        acc_sc[...] = jnp.zeros_like(acc_sc)

        def chunk(j, carry):
            scores(j, sa_sc)
            diag_bias(j, sa_sc)
            adj = jnp.where(j > i, off, jnp.where(j < i, -off, 0.0))
            start = pl.multiple_of(j * tk, tk)
            s = sa_sc[...]
            m_prev = m_sc[...]
            m_new = jnp.maximum(m_prev, jnp.max(s, axis=-1, keepdims=True) + adj)
            p = jnp.exp2(s - (m_new - adj)).astype(BF16)
            acc_sc[...] = jnp.exp2(m_prev - m_new) * acc_sc[...] + _dot(p, va_sc[pl.ds(start, tk), :])
            m_sc[...] = m_new
            return carry

        lax.fori_loop(0, nk, chunk, 0)

    lp = lam_ref[...]
    lam = (jnp.exp(jnp.sum(lp[0:1] * lp[1:2], axis=-1, keepdims=True))
           - jnp.exp(jnp.sum(lp[2:3] * lp[3:4], axis=-1, keepdims=True)) + lambda_init)
    acc = acc_sc[...]
    o0 = acc[:tq, :B_VDIM] * (1.0 / acc[:tq, B_VDIM:B_VDIM + 1])
    o1 = acc[tq:, :B_VDIM] * (1.0 / acc[tq:, B_VDIM:B_VDIM + 1])
    o_ref[rows, :] = (_rms(o0 - lam * o1, g_ref[...]) * out_scale).astype(o_ref.dtype)


def _bf16_pieces(x, n=3):
    out = []
    r = np.float64(x)
    for _ in range(n):
        p = np.float64(np.float32(r).astype(jnp.bfloat16).astype(np.float32))
        out.append(p)
        r = r - p
    return out


def _alibi_features(slopes2, t):
    pos = np.arange(t)
    hi = (pos // _POS_SPLIT) * _POS_SPLIT
    lo = pos % _POS_SPLIT
    qf = np.zeros((len(slopes2), t, LANES), np.float32)
    kf = np.zeros((len(slopes2), t, LANES), np.float32)
    for h, s in enumerate(slopes2):
        for n, piece in enumerate(_bf16_pieces(s)):
            for base, part in ((0, hi), (3, lo)):
                qf[h, :, base + n] = -piece
                kf[h, :, base + n] = part
                qf[h, :, 6 + base + n] = part
                kf[h, :, 6 + base + n] = piece
        for n, piece in enumerate(_bf16_pieces(s * t)):
            qf[h, :, _CHUNK_LANE + n] = -piece
        kf[h, :, _BOUND_LANE:_BOUND_LANE + 3] = -1.0
        kf[h, :, _OFFSET_LANE:_OFFSET_LANE + 3] = -1.0
    return jnp.asarray(qf, BF16), jnp.asarray(kf, BF16)


def _diff_attention(proj, b_lambda, subln_g, lambda_init, batch, seq, t=512):
    t = min(t, seq // 2)
    nq = seq // t
    assert seq % (2 * t) == 0
    kcol = MIX_WIDTH // B_VDIM
    vcol = 2 * kcol
    slopes2 = [s * LOG2E for s in _alibi_slopes(B_HEADS)]
    reach = [min(nq, int(math.floor((_ZERO_EXP2_ARG / s - 1.0) / t)) + 1) for s in slopes2]
    qf, kf = _alibi_features(slopes2, t)
    pos = np.arange(t)
    dist = jnp.asarray(np.abs(pos[:, None] - pos[None, :]), F32)
    kernel = functools.partial(_diff_attn_kernel, tk=t, out_scale=1.0 - lambda_init,
                               lambda_init=lambda_init)
    return pl.pallas_call(
        kernel,
        grid_spec=pltpu.PrefetchScalarGridSpec(
            num_scalar_prefetch=2,
            grid=(batch, B_HEADS),
            in_specs=[pl.BlockSpec((seq, B_VDIM), lambda b, h, r, s: (b, h)),
                      pl.BlockSpec((seq, B_VDIM), lambda b, h, r, s: (b, kcol + h)),
                      pl.BlockSpec((seq, B_VDIM), lambda b, h, r, s: (b, vcol + h)),
                      pl.BlockSpec((None, t, LANES), lambda b, h, r, s: (h, 0, 0)),
                      pl.BlockSpec((None, t, LANES), lambda b, h, r, s: (h, 0, 0)),
                      pl.BlockSpec((t, t), lambda b, h, r, s: (0, 0)),
                      pl.BlockSpec((4, HEAD_DIM), lambda b, h, r, s: (0, 0)),
                      pl.BlockSpec((1, B_VDIM), lambda b, h, r, s: (0, 0))],
            out_specs=pl.BlockSpec((seq, B_VDIM), lambda b, h, r, s: (b, h)),
            scratch_shapes=[pltpu.VMEM((3, 2 * t, 2 * LANES), BF16),
                            pltpu.VMEM((seq, 2 * LANES), BF16),
                            pltpu.VMEM((seq, 2 * LANES), BF16),
                            pltpu.VMEM((2, 8, LANES), F32),
                            pltpu.VMEM((2 * t, 1), F32),
                            pltpu.VMEM((2 * t, 2 * LANES), F32),
                            pltpu.VMEM((2 * t, t), F32),
                            pltpu.VMEM((2 * t, t), BF16),
                            pltpu.VMEM((2 * t, t), BF16)],
        ),
        out_shape=jax.ShapeDtypeStruct((batch * seq, MIX_WIDTH), BF16),
        compiler_params=_params("parallel", "parallel"),
        name="diff_attention",
    )(jnp.asarray(reach, jnp.int32), jnp.asarray(slopes2, F32), proj, proj, proj, qf, kf, dist,
      b_lambda, subln_g.reshape(1, B_VDIM))


def _router_kernel(x_ref, g_ref, whi_ref, wlo_ref, h_ref, e_ref, gate_ref):
    h = _rms(x_ref[...], g_ref[...])
    h_hi = h.astype(BF16)
    h_lo = (h - h_hi.astype(F32)).astype(BF16)
    whi = whi_ref[...]
    logits = _dot(h_hi, whi) + (_dot(h_hi, wlo_ref[...]) + _dot(h_lo, whi))
    lane = lax.broadcasted_iota(jnp.int32, logits.shape, 1)
    logits = jnp.where(lane < N_EXPERTS, logits, -jnp.inf)
    m1 = jnp.max(logits, axis=-1, keepdims=True)
    i1 = jnp.min(jnp.where(logits == m1, lane, LANES), axis=-1, keepdims=True)
    rest = jnp.where(lane == i1, -jnp.inf, logits)
    m2 = jnp.max(rest, axis=-1, keepdims=True)
    i2 = jnp.min(jnp.where(rest == m2, lane, LANES), axis=-1, keepdims=True)
    r = jnp.exp(m2 - m1)
    g1 = 1.0 / (1.0 + r)
    g2 = r * g1
    e_ref[...] = jnp.where(lane == 0, i1, jnp.where(lane == 1, i2, 0))
    gate_ref[...] = jnp.where(lane == 0, g1, jnp.where(lane == 1, g2, 0.0))
    h_ref[...] = h_hi


def _router(x, g, w_router, tm=2048):
    t, d = x.shape
    tm = min(tm, t)
    wpad = jnp.zeros((d, LANES), F32).at[:, :N_EXPERTS].set(w_router)
    whi = wpad.astype(BF16)
    wlo = (wpad - whi.astype(F32)).astype(BF16)
    return pl.pallas_call(
        _router_kernel,
        grid=(t // tm,),
        in_specs=[pl.BlockSpec((tm, d), lambda i: (i, 0)),
                  pl.BlockSpec((1, d), lambda i: (0, 0)),
                  pl.BlockSpec((d, LANES), lambda i: (0, 0)),
                  pl.BlockSpec((d, LANES), lambda i: (0, 0))],
        out_specs=[pl.BlockSpec((tm, d), lambda i: (i, 0)),
                   pl.BlockSpec((tm, LANES), lambda i: (i, 0)),
                   pl.BlockSpec((tm, LANES), lambda i: (i, 0))],
        out_shape=[jax.ShapeDtypeStruct((t, d), BF16),
                   jax.ShapeDtypeStruct((t, LANES), jnp.int32),
                   jax.ShapeDtypeStruct((t, LANES), F32)],
        compiler_params=_params("parallel"),
        name="router",
    )(x, g.reshape(1, d), whi, wlo)


def _slab_sizes(rows, tc):
    quarter, half = tc // 4, tc // 2
    return ((quarter, rows <= quarter),
            (half, jnp.logical_and(rows > quarter, rows <= half)),
            (tc, rows > half))


def _slab_stride(slab_ref, n, step, tc):
    most = slab_ref[(n + step) * N_EXPERTS]
    for e in range(1, N_EXPERTS):
        most = jnp.maximum(most, slab_ref[(n + step) * N_EXPERTS + e])
    return jnp.where(most <= tc // 2, tc // 2, tc)


def _dispatch_kernel(slab_ref, fill_ref, h_ref, dest_ref, gate_ref, xs_hbm, gs_hbm,
                     xslab_sc, gslab_sc, zx_sc, zg_sc, sem, zsem):
    i = pl.program_id(0)
    n = pl.num_programs(0)
    slot = i % 2
    tc = h_ref.shape[0]
    tm = zx_sc.shape[0]

    def slab_dma(step, sl, op):
        stride = _slab_stride(slab_ref, n, step, tc)
        for e in range(N_EXPERTS):
            start = pl.multiple_of(slab_ref[step * N_EXPERTS + e], SUBLANES)
            first = pl.multiple_of(e * stride, SUBLANES)
            for size, cond in _slab_sizes(slab_ref[(n + step) * N_EXPERTS + e], tc):
                @pl.when(cond)
                def _():
                    for src, dst in ((xslab_sc, xs_hbm), (gslab_sc, gs_hbm)):
                        copy = pltpu.make_async_copy(src.at[sl, pl.ds(first, size)],
                                                     dst.at[pl.ds(start, size)], sem.at[sl])
                        getattr(copy, op)()

    def fill_copies(start):
        start = pl.multiple_of(start, SUBLANES)
        return [pltpu.make_async_copy(zx_sc, xs_hbm.at[pl.ds(start, tm)], zsem),
                pltpu.make_async_copy(zg_sc, gs_hbm.at[pl.ds(start, tm)], zsem)]

    @pl.when(i == 0)
    def _():
        zx_sc[...] = jnp.zeros_like(zx_sc)
        zg_sc[...] = jnp.zeros_like(zg_sc)
        for half in range(2):
            for e in range(N_EXPERTS):
                for c in fill_copies(fill_ref[half * N_EXPERTS + e]):
                    c.start()
            for e in range(N_EXPERTS):
                for c in fill_copies(fill_ref[half * N_EXPERTS + e]):
                    c.wait()
        first_free = fill_ref[2 * N_EXPERTS]
        n_tiles = xs_hbm.shape[0] // tm

        def start_tile(b, carry):
            for c in fill_copies(b * tm):
                c.start()
            return carry

        def wait_tile(b, carry):
            for c in fill_copies(b * tm):
                c.wait()
            return carry

        lax.fori_loop(first_free, n_tiles, start_tile, 0)
        lax.fori_loop(first_free, n_tiles, wait_tile, 0)

    dest = dest_ref[0]
    gate = gate_ref[...]
    lane = lax.broadcasted_iota(jnp.int32, gate.shape, 1)

    def gate_pieces(col):
        out = jnp.zeros(gate.shape, F32)
        for n_, piece in enumerate(_bf16_split(col)):
            out = jnp.where(lane == n_, piece.astype(F32), out)
        return out.astype(BF16)

    def place_rows(stride):
        pos = lax.broadcasted_iota(jnp.int32, (stride, tc), 0)
        sel0, sel1 = [], []
        for e in range(N_EXPERTS):
            start = slab_ref[i * N_EXPERTS + e]
            sel0.append(jnp.where(dest[0:1, :] - start == pos, 1.0, 0.0))
            sel1.append(jnp.where(dest[1:2, :] - start == pos, 1.0, 0.0))
        sel0 = jnp.concatenate(sel0, axis=0).astype(BF16)
        sel1 = jnp.concatenate(sel1, axis=0).astype(BF16)
        rows = pl.ds(0, N_EXPERTS * stride)
        xslab_sc[slot, rows] = _dot(sel0 + sel1, h_ref[...])
        gslab_sc[slot, rows] = (_dot(sel0, gate_pieces(gate[:, 0:1]))
                                + _dot(sel1, gate_pieces(gate[:, 1:2])))

    compact = _slab_stride(slab_ref, n, i, tc) < tc
    pl.when(compact)(lambda: place_rows(tc // 2))
    pl.when(jnp.logical_not(compact))(lambda: place_rows(tc))

    @pl.when(i > 0)
    def _():
        slab_dma(i - 1, 1 - slot, "wait")

    slab_dma(i, slot, "start")

    @pl.when(i == n - 1)
    def _():
        slab_dma(i, slot, "wait")


def _dispatch(h, dest_t, gates, slab, fill, p_rows, tm, tc):
    t, d = h.shape
    nb = t // tc
    return pl.pallas_call(
        _dispatch_kernel,
        grid_spec=pltpu.PrefetchScalarGridSpec(
            num_scalar_prefetch=2,
            grid=(nb,),
            in_specs=[pl.BlockSpec((tc, d), lambda i, s, z: (i, 0)),
                      pl.BlockSpec((1, TOP_K, tc), lambda i, s, z: (i, 0, 0)),
                      pl.BlockSpec((tc, LANES), lambda i, s, z: (i, 0))],
            out_specs=[pl.BlockSpec(memory_space=pl.ANY), pl.BlockSpec(memory_space=pl.ANY)],
            scratch_shapes=[pltpu.VMEM((2, N_EXPERTS * tc, d), F32),
                            pltpu.VMEM((2, N_EXPERTS * tc, LANES), F32),
                            pltpu.VMEM((tm, d), F32), pltpu.VMEM((tm, LANES), F32),
                            pltpu.SemaphoreType.DMA((2,)), pltpu.SemaphoreType.DMA(())],
        ),
        out_shape=[jax.ShapeDtypeStruct((p_rows, d), F32), jax.ShapeDtypeStruct((p_rows, LANES), F32)],
        compiler_params=_params("arbitrary"),
        name="moe_dispatch",
    )(slab, fill, h, dest_t, gates)


def _moe_kernel(blk_e_ref, used_ref, xblk_ref, x_ref, gs_ref, wg_ref, wu_ref, wd_ref, o_ref,
                xb_sc, acc_sc):
    i = pl.program_id(0)
    f = pl.program_id(1)
    used = used_ref[i] > 0

    @pl.when(jnp.logical_and(used, f == 0))
    def _():
        xb_sc[...] = x_ref[...].astype(BF16)
        acc_sc[...] = jnp.zeros_like(acc_sc)

    @pl.when(used)
    def _():
        acc_sc[...] += _swiglu_tile(xb_sc[...], wg_ref, wu_ref, wd_ref)

    @pl.when(f == pl.num_programs(1) - 1)
    def _():
        gs = gs_ref[...]
        gate = gs[:, 0:1] + gs[:, 1:2] + gs[:, 2:3]
        o_ref[...] = jnp.where(used, acc_sc[...] * gate, 0.0)


def _moe_experts(xs, gs, blk_e, blk_used, xblk, wg, wu, wd, tm, tf=1792):
    p, d = xs.shape
    nblk = p // tm
    nf = wg.shape[2] // tf

    def fidx(i, f, used_ref):
        return jnp.where(used_ref[i] > 0, f, nf - 1)

    return pl.pallas_call(
        _moe_kernel,
        grid_spec=pltpu.PrefetchScalarGridSpec(
            num_scalar_prefetch=3,
            grid=(nblk, nf),
            in_specs=[pl.BlockSpec((tm, d), lambda i, f, be, us, xb: (xb[i], 0)),
                      pl.BlockSpec((tm, LANES), lambda i, f, be, us, xb: (xb[i], 0)),
                      pl.BlockSpec((None, d, tf), lambda i, f, be, us, xb: (be[i], 0, fidx(i, f, us))),
                      pl.BlockSpec((None, d, tf), lambda i, f, be, us, xb: (be[i], 0, fidx(i, f, us))),
                      pl.BlockSpec((None, tf, d), lambda i, f, be, us, xb: (be[i], fidx(i, f, us), 0))],
            out_specs=pl.BlockSpec((tm, d), lambda i, f, be, us, xb: (i, 0)),
            scratch_shapes=[pltpu.VMEM((tm, d), BF16), pltpu.VMEM((tm, d), F32)],
        ),
        out_shape=jax.ShapeDtypeStruct((p, d), F32),
        compiler_params=_params("arbitrary", "arbitrary"),
        name="moe_experts",
    )(blk_e, blk_used, xblk, xs, gs, wg, wu, wd)


def _combine_kernel(slab_ref, ys_hbm, dest_ref, x_ref, g_ref, o_ref, slab_sc, sem):
    i = pl.program_id(0)
    n = pl.num_programs(0)
    slot = i % 2
    tc = x_ref.shape[0]

    def slab_dma(step, sl, op):
        stride = _slab_stride(slab_ref, n, step, tc)
        for e in range(N_EXPERTS):
            start = pl.multiple_of(slab_ref[step * N_EXPERTS + e], SUBLANES)
            first = pl.multiple_of(e * stride, SUBLANES)
            for size, cond in _slab_sizes(slab_ref[(n + step) * N_EXPERTS + e], tc):
                @pl.when(cond)
                def _():
                    copy = pltpu.make_async_copy(ys_hbm.at[pl.ds(start, size)],
                                                 slab_sc.at[sl, pl.ds(first, size)], sem.at[sl])
                    getattr(copy, op)()

    @pl.when(i == 0)
    def _():
        slab_sc[...] = jnp.zeros_like(slab_sc)
        slab_dma(0, 0, "start")

    @pl.when(i + 1 < n)
    def _():
        slab_dma(i + 1, 1 - slot, "start")

    dest = dest_ref[...]
    slab_dma(i, slot, "wait")

    def gather_rows(stride):
        pos = lax.broadcasted_iota(jnp.int32, (tc, stride), 1)
        sel = []
        for e in range(N_EXPERTS):
            start = slab_ref[i * N_EXPERTS + e]
            hit = jnp.logical_or(dest[:, 0:1] - start == pos, dest[:, 1:2] - start == pos)
            sel.append(jnp.where(hit, 1.0, 0.0))
        sel = jnp.concatenate(sel, axis=1).astype(BF16)
        y = _dot(sel, slab_sc[slot, pl.ds(0, N_EXPERTS * stride)].astype(BF16))
        o_ref[...] = x_ref[...] + _rms(y, g_ref[...])

    compact = _slab_stride(slab_ref, n, i, tc) < tc
    pl.when(compact)(lambda: gather_rows(tc // 2))
    pl.when(jnp.logical_not(compact))(lambda: gather_rows(tc))


def _combine(ys, dest, slab, x, g, tc):
    t, d = x.shape
    return pl.pallas_call(
        _combine_kernel,
        grid_spec=pltpu.PrefetchScalarGridSpec(
            num_scalar_prefetch=1,
            grid=(t // tc,),
            in_specs=[pl.BlockSpec(memory_space=pl.ANY),
                      pl.BlockSpec((tc, LANES), lambda i, s: (i, 0)),
                      pl.BlockSpec((tc, d), lambda i, s: (i, 0)),
                      pl.BlockSpec((1, d), lambda i, s: (0, 0))],
            out_specs=pl.BlockSpec((tc, d), lambda i, s: (i, 0)),
            scratch_shapes=[pltpu.VMEM((2, N_EXPERTS * tc, d), F32), pltpu.SemaphoreType.DMA((2,))],
        ),
        out_shape=jax.ShapeDtypeStruct((t, d), F32),
        compiler_params=_params("arbitrary"),
        name="moe_combine",
    )(slab, ys, dest, x, g.reshape(1, d))


def _route(top_e, tm, tc):
    t = top_e.shape[0]
    a = t * TOP_K
    nb = t // tc
    e_flat = top_e.reshape(a)
    onehot = (e_flat[:, None] == jnp.arange(N_EXPERTS, dtype=jnp.int32)[None, :]).astype(jnp.int32)
    csum = jnp.cumsum(onehot, axis=0)
    rank = jnp.sum(onehot * (csum - 1), axis=1)
    upto = csum[TOP_K * tc - 1::TOP_K * tc]
    before = jnp.concatenate([jnp.zeros((1, N_EXPERTS), jnp.int32), upto[:nb - 1]], axis=0)
    rows = (upto - before + SUBLANES - 1) // SUBLANES * SUBLANES
    rows_before = jnp.cumsum(rows, axis=0) - rows
    total = jnp.sum(rows, axis=0)
    region = (total + 2 * tm - 1) // tm * tm
    rends = jnp.cumsum(region)
    rstarts = rends - region
    slab = (rstarts[None, :] + rows_before).astype(jnp.int32)
    shift = jnp.broadcast_to((slab - before)[:, None, :], (nb, TOP_K * tc, N_EXPERTS)).reshape(a, N_EXPERTS)
    dest = (rank + jnp.sum(onehot * shift, axis=1)).astype(jnp.int32).reshape(t, TOP_K)
    slab = jnp.concatenate([slab.reshape(nb * N_EXPERTS), rows.reshape(nb * N_EXPERTS)])
    row_end = rstarts + total
    fill = jnp.concatenate([row_end, rends - tm, rends[-1:] // tm]).astype(jnp.int32)
    nblk = -(-(a + nb * N_EXPERTS * (SUBLANES - 1)) // tm) + 2 * N_EXPERTS
    blk_start = jnp.arange(nblk, dtype=jnp.int32) * tm
    blk_e = jnp.minimum(jnp.sum((rends[None, :] <= blk_start[:, None]).astype(jnp.int32), axis=1),
                        N_EXPERTS - 1)
    blk_used = jnp.logical_and(blk_start < row_end[blk_e], blk_start < rends[-1]).astype(jnp.int32)
    first_used = jnp.argmax(blk_used).astype(jnp.int32)
    xblk = jnp.where(blk_used > 0, jnp.arange(nblk, dtype=jnp.int32), first_used)
    return dest, slab, fill, blk_e, blk_used, xblk, nblk * tm


def _moe(x, h, e_pad, gates, g_out, wg, wu, wd, tm=512, tc=256):
    t = x.shape[0]
    tc = min(tc, t)
    dest, slab, fill, blk_e, blk_used, xblk, p_rows = _route(e_pad[:, :TOP_K], tm, tc)
    dest_t = dest.reshape(t // tc, tc, TOP_K).transpose(0, 2, 1)
    dest_pad = jnp.zeros((t, LANES), jnp.int32).at[:, :TOP_K].set(dest)
    xs, gs = _dispatch(h, dest_t, gates, slab, fill, p_rows, tm, tc)
    ys = _moe_experts(xs, gs, blk_e, blk_used, xblk, wg, wu, wd, tm)
    return _combine(ys, dest_pad, slab, x, g_out, tc)


def kernel(x, mem, norm_g, mem_norm_g, w_mem_kv, a_w_in, a_sink, a_w_out, b_w_in, b_lambda,
           b_subln_g, b_w_out, ffn_w_gate, ffn_w_up, ffn_w_down, moe_w_router, moe_w_gate,
           moe_w_up, moe_w_down):
    batch, seq, d = x.shape
    n_mem = mem.shape[1]
    bf = lambda w: w.astype(BF16)
    xt = x.reshape(batch * seq, d)
    mkv = _norm_matmul(mem.reshape(batch * n_mem, d), mem_norm_g, bf(w_mem_kv),
                       jnp.ones((w_mem_kv.shape[1],), F32))

    g = norm_g[0]
    proj = _norm_matmul(xt, g[0], bf(a_w_in[0]), _query_col_scale(a_w_in.shape[2], A_Q_HEADS * HEAD_DIM))
    mix = _window_attention(proj, a_sink[0], batch, seq)
    mem_out = _mem_attention(proj, (A_Q_HEADS + 2 * A_KV_HEADS) * HEAD_DIM // MEM_WIDTH, mkv, batch, seq)
    xt = _out_proj(mix, mem_out, bf(a_w_out[0]), xt, g[1])
    xt = _ffn(xt, g[2], g[3], bf(ffn_w_gate[0]), bf(ffn_w_up[0]), bf(ffn_w_down[0]))

    g = norm_g[1]
    lambda_init = 0.8 - 0.6 * math.exp(-0.3 * 1)
    proj = _norm_matmul(xt, g[0], bf(b_w_in[0]), _query_col_scale(b_w_in.shape[2], MIX_WIDTH))
    mix = _diff_attention(proj, b_lambda[0], b_subln_g[0], lambda_init, batch, seq)
    mem_out = _mem_attention(proj, 3 * MIX_WIDTH // MEM_WIDTH, mkv, batch, seq)
    xt = _out_proj(mix, mem_out, bf(b_w_out[0]), xt, g[1])
    h, e_pad, gates = _router(xt, g[2], moe_w_router[0])
    xt = _moe(xt, h, e_pad, gates, g[3], bf(moe_w_gate[0]), bf(moe_w_up[0]), bf(moe_w_down[0]))
    return xt.reshape(batch, seq, d)
```

```python
import functools
import math

import jax
import jax.numpy as jnp
import numpy as np
from jax import lax
from jax.experimental import pallas as pl
from jax.experimental.pallas import tpu as pltpu

D_MODEL = 1024
HEAD_DIM = 64
MIX_WIDTH = 768
MEM_WIDTH = 256
MEM_HEADS = 4
A_Q_HEADS = 12
A_KV_HEADS = 4
A_GROUP = 3
WINDOW = 128
B_HEADS = 6
B_VDIM = 128
D_FF = 3584
N_EXPERTS = 8
TOP_K = 2
EPS = 1e-6
NEG_INF = -1e30
LOG2E = math.log2(math.e)
QK_SCALE = HEAD_DIM ** -0.5 * LOG2E
LANES = 128
SUBLANES = 8
VMEM_LIMIT_BYTES = 56 * 1024 * 1024

BF16 = jnp.bfloat16
F32 = jnp.float32


def _params(*sem):
    return pltpu.CompilerParams(dimension_semantics=sem, vmem_limit_bytes=VMEM_LIMIT_BYTES)


def _rms(x, g):
    return x * lax.rsqrt(jnp.mean(x * x, axis=-1, keepdims=True) + EPS) * g


def _dot(a, b):
    return jnp.dot(a, b, preferred_element_type=F32)


def _dot_nt(a, b):
    return lax.dot_general(a, b, (((1,), (1,)), ((), ())), preferred_element_type=F32)


def _alibi_slopes(n):
    return [2.0 ** (-8.0 * (i + 1) / n) for i in range(n)]


def _norm_matmul_kernel(x_ref, g_ref, w_ref, cs_ref, o_ref, *, chunk):
    h = _rms(x_ref[...], g_ref[...]).astype(BF16)
    for c in range(o_ref.shape[1] // chunk):
        sl = slice(c * chunk, (c + 1) * chunk)
        o_ref[:, sl] = (_dot(h, w_ref[:, sl]) * cs_ref[:, sl]).astype(o_ref.dtype)


def _norm_matmul(x, g, w, col_scale, tm=1024, chunk=512):
    t, d = x.shape
    n = w.shape[1]
    tm = min(tm, t)
    return pl.pallas_call(
        functools.partial(_norm_matmul_kernel, chunk=min(chunk, n)),
        grid=(t // tm,),
        in_specs=[pl.BlockSpec((tm, d), lambda i: (i, 0)),
                  pl.BlockSpec((1, d), lambda i: (0, 0)),
                  pl.BlockSpec((d, n), lambda i: (0, 0)),
                  pl.BlockSpec((1, n), lambda i: (0, 0))],
        out_specs=pl.BlockSpec((tm, n), lambda i: (i, 0)),
        out_shape=jax.ShapeDtypeStruct((t, n), BF16),
        compiler_params=_params("parallel"),
        name="norm_matmul",
    )(x, g.reshape(1, d), w, col_scale.reshape(1, n))


def _query_col_scale(n, mix_q_cols):
    cols = np.arange(n)
    return jnp.asarray(np.where((cols < mix_q_cols) | (cols >= n - MEM_WIDTH), QK_SCALE, 1.0), F32)


def _window_kernel(q_ref, kp_ref, kc_ref, kn_ref, vp_ref, vc_ref, vn_ref, bias_ref, o_ref):
    n = pl.program_id(1)
    nb = pl.num_programs(1)
    blk = kp_ref.shape[0]
    sub = q_ref.shape[0] // blk
    zeros = jnp.zeros((blk, kp_ref.shape[1]), kp_ref.dtype)
    k_all = jnp.concatenate([kp_ref[...], kc_ref[...], kn_ref[...]], axis=0)
    v_all = jnp.concatenate([vp_ref[...], vc_ref[...], vn_ref[...]], axis=0)
    col = lax.broadcasted_iota(jnp.int32, (1, 4 * blk), 1)
    ones = jnp.ones((4 * blk, LANES), BF16)
    for u in range(sub):
        k = jnp.concatenate([k_all[u * blk:(u + 3) * blk], zeros], axis=0)
        v = jnp.concatenate([v_all[u * blk:(u + 3) * blk], zeros], axis=0)
        edge = jnp.zeros((1, 4 * blk), F32)
        if u == 0:
            edge = jnp.where(jnp.logical_and(n == 0, col < blk), NEG_INF, edge)
        if u == sub - 1:
            edge = jnp.where(jnp.logical_and(n == nb - 1,
                                             jnp.logical_and(col >= 2 * blk, col < 3 * blk)), NEG_INF, edge)
        q = q_ref[u * blk:(u + 1) * blk, :]
        outs = []
        for kh in range(A_KV_HEADS):
            heads = range(kh * A_GROUP, (kh + 1) * A_GROUP)
            kv = slice(kh * HEAD_DIM, (kh + 1) * HEAD_DIM)
            qs = jnp.concatenate([q[:, h * HEAD_DIM:(h + 1) * HEAD_DIM] for h in heads], axis=0)
            s = _dot_nt(qs, k[:, kv]) + bias_ref[kh] + edge
            p = jnp.exp2(s - jnp.max(s, axis=-1, keepdims=True)).astype(BF16)
            o = _dot(p, v[:, kv]) * (1.0 / _dot(p, ones)[:, :HEAD_DIM])
            outs.extend(o[g * blk:(g + 1) * blk] for g in range(A_GROUP))
        o_ref[u * blk:(u + 1) * blk, :] = jnp.concatenate(outs, axis=-1).astype(o_ref.dtype)


def _window_bias(blk, sink):
    qi = np.arange(blk)[:, None]
    kj = np.arange(3 * blk)[None, :]
    dist = np.abs(blk + qi - kj)
    slopes = _alibi_slopes(A_Q_HEADS)
    tables = np.stack([np.where(dist <= WINDOW, -(slopes[h] * LOG2E) * dist, NEG_INF)
                       for h in range(A_Q_HEADS)])
    pad = jnp.full((A_Q_HEADS, blk, blk), NEG_INF, F32)
    pad = pad.at[:, :, 0].set(jnp.broadcast_to((sink.astype(F32) * LOG2E)[:, None], (A_Q_HEADS, blk)))
    full = jnp.concatenate([jnp.asarray(tables, F32), pad], axis=2)
    return full.reshape(A_KV_HEADS, A_GROUP * blk, 4 * blk)


def _window_attention(proj, sink, batch, seq, sub=4):
    blk = WINDOW
    nb = seq // blk
    sub = min(sub, nb)
    ns = nb // sub
    kcol = MIX_WIDTH // 256
    vcol = kcol + 1
    bias = _window_bias(blk, sink)

    def cur(col):
        return pl.BlockSpec((sub * blk, 256), lambda b, n: (b * ns + n, col))

    def prev(col):
        return pl.BlockSpec((blk, 256), lambda b, n: (b * nb + jnp.maximum(n * sub - 1, 0), col))

    def nxt(col):
        return pl.BlockSpec((blk, 256), lambda b, n: (b * nb + jnp.minimum((n + 1) * sub, nb - 1), col))

    return pl.pallas_call(
        _window_kernel,
        grid=(batch, ns),
        in_specs=[pl.BlockSpec((sub * blk, MIX_WIDTH), lambda b, n: (b * ns + n, 0)),
                  prev(kcol), cur(kcol), nxt(kcol), prev(vcol), cur(vcol), nxt(vcol),
                  pl.BlockSpec(bias.shape, lambda b, n: (0, 0, 0))],
        out_specs=pl.BlockSpec((sub * blk, MIX_WIDTH), lambda b, n: (b * ns + n, 0)),
        out_shape=jax.ShapeDtypeStruct((batch * seq, MIX_WIDTH), BF16),
        compiler_params=_params("parallel", "parallel"),
        name="window_attention",
    )(proj, proj, proj, proj, proj, proj, proj, bias)


def _mem_attn_kernel(q_ref, mkv_ref, o_ref):
    q = q_ref[...]
    mkv = mkv_ref[...]
    ones = jnp.ones((mkv.shape[0], LANES), BF16)
    outs = []
    for h in range(MEM_HEADS):
        sl = slice(h * HEAD_DIM, (h + 1) * HEAD_DIM)
        s = _dot_nt(q[:, sl], mkv[:, sl])
        p = jnp.exp2(s - jnp.max(s, axis=-1, keepdims=True)).astype(BF16)
        o = _dot(p, mkv[:, MEM_WIDTH + h * HEAD_DIM:MEM_WIDTH + (h + 1) * HEAD_DIM])
        outs.append(o * (1.0 / _dot(p, ones)[:, :HEAD_DIM]))
    o_ref[...] = jnp.concatenate(outs, axis=-1).astype(o_ref.dtype)


def _mem_attention(proj, qcol, mkv, batch, seq, tm=2048):
    tm = min(tm, seq)
    per_batch = seq // tm
    n_mem = mkv.shape[0] // batch
    return pl.pallas_call(
        _mem_attn_kernel,
        grid=(batch * per_batch,),
        in_specs=[pl.BlockSpec((tm, MEM_WIDTH), lambda i: (i, qcol)),
                  pl.BlockSpec((n_mem, 2 * MEM_WIDTH), lambda i: (i // per_batch, 0))],
        out_specs=pl.BlockSpec((tm, MEM_WIDTH), lambda i: (i, 0)),
        out_shape=jax.ShapeDtypeStruct((batch * seq, MEM_WIDTH), BF16),
        compiler_params=_params("parallel"),
        name="mem_attention",
    )(proj, mkv)


def _out_proj_kernel(mix_ref, mem_ref, w1_ref, w2_ref, x_ref, g_ref, o_ref):
    o = _dot(mix_ref[...], w1_ref[...]) + _dot(mem_ref[...], w2_ref[...])
    o_ref[...] = x_ref[...] + _rms(o, g_ref[...])


def _out_proj(mix, mem_out, w_out, x, g, tm=1024):
    t, d = x.shape
    tm = min(tm, t)
    w1 = w_out[:MIX_WIDTH]
    w2 = w_out[MIX_WIDTH:]
    return pl.pallas_call(
        _out_proj_kernel,
        grid=(t // tm,),
        in_specs=[pl.BlockSpec((tm, MIX_WIDTH), lambda i: (i, 0)),
                  pl.BlockSpec((tm, MEM_WIDTH), lambda i: (i, 0)),
                  pl.BlockSpec((MIX_WIDTH, d), lambda i: (0, 0)),
                  pl.BlockSpec((MEM_WIDTH, d), lambda i: (0, 0)),
                  pl.BlockSpec((tm, d), lambda i: (i, 0)),
                  pl.BlockSpec((1, d), lambda i: (0, 0))],
        out_specs=pl.BlockSpec((tm, d), lambda i: (i, 0)),
        out_shape=jax.ShapeDtypeStruct((t, d), F32),
        compiler_params=_params("parallel"),
        name="out_proj",
    )(mix, mem_out, w1, w2, x, g.reshape(1, d))


def _swiglu_tile(h, wg_ref, wu_ref, wd_ref, chunk=256):
    out = None
    for c in range(wg_ref.shape[1] // chunk):
        sl = slice(c * chunk, (c + 1) * chunk)
        a = _dot(h, wg_ref[:, sl])
        u = _dot(h, wu_ref[:, sl])
        z = (a * jax.nn.sigmoid(a) * u).astype(BF16)
        y = _dot(z, wd_ref[sl, :])
        out = y if out is None else out + y
    return out


def _ffn_kernel(x_ref, gin_ref, gout_ref, wg_ref, wu_ref, wd_ref, o_ref, h_sc, acc_sc):
    f = pl.program_id(1)

    @pl.when(f == 0)
    def _():
        h_sc[...] = _rms(x_ref[...], gin_ref[...]).astype(BF16)
        acc_sc[...] = jnp.zeros_like(acc_sc)

    acc_sc[...] += _swiglu_tile(h_sc[...], wg_ref, wu_ref, wd_ref)

    @pl.when(f == pl.num_programs(1) - 1)
    def _():
        o_ref[...] = x_ref[...] + _rms(acc_sc[...], gout_ref[...])


def _ffn(x, g_in, g_out, wg, wu, wd, tm=1024, tf=1792):
    t, d = x.shape
    ff = wg.shape[1]
    tm = min(tm, t)
    return pl.pallas_call(
        _ffn_kernel,
        grid=(t // tm, ff // tf),
        in_specs=[pl.BlockSpec((tm, d), lambda i, f: (i, 0)),
                  pl.BlockSpec((1, d), lambda i, f: (0, 0)),
                  pl.BlockSpec((1, d), lambda i, f: (0, 0)),
                  pl.BlockSpec((d, tf), lambda i, f: (0, f)),
                  pl.BlockSpec((d, tf), lambda i, f: (0, f)),
                  pl.BlockSpec((tf, d), lambda i, f: (f, 0))],
        out_specs=pl.BlockSpec((tm, d), lambda i, f: (i, 0)),
        out_shape=jax.ShapeDtypeStruct((t, d), F32),
        scratch_shapes=[pltpu.VMEM((tm, d), BF16), pltpu.VMEM((tm, d), F32)],
        compiler_params=_params("parallel", "arbitrary"),
        name="ffn",
    )(x, g_in.reshape(1, d), g_out.reshape(1, d), wg, wu, wd)


_POS_SPLIT = 256
_LEFT, _DIAG, _RIGHT = 0, 1, 2
_BOUND_LANE = 12
_CHUNK_LANE = 15
_OFFSET_LANE = 18
_ZERO_EXP2_ARG = 136.0
_MIN_DENOM = 2.0 ** -60


def _bf16_split(x):
    hi = x.astype(BF16)
    r = x - hi.astype(F32)
    mid = r.astype(BF16)
    lo = (r - mid.astype(F32)).astype(BF16)
    return hi, mid, lo


def _diff_attn_kernel(reach_ref, slope_ref, q_ref, k_ref, v_ref, qf_ref, kf_ref, dist_ref, lam_ref,
                      g_ref, o_ref, qa_sc, ka_sc, va_sc, k2_sc, m_sc, acc_sc, sa_sc, pa_sc, pb_sc, kt_sc, *, tk,
                      out_scale, lambda_init):
    seq = k_ref.shape[0]
    nk = seq // tk
    k = k_ref[...]
    ka_sc[:, :B_VDIM] = k
    va_sc[:, :B_VDIM] = v_ref[...]
    lane = lax.broadcasted_iota(jnp.int32, (seq, LANES), 1)
    va_sc[:, B_VDIM:] = jnp.where(lane == 0, 1.0, 0.0).astype(BF16)
    kf = kf_ref[...]
    flane = lax.broadcasted_iota(jnp.int32, kf.shape, 1)
    chunk_lanes = jnp.logical_and(flane >= _CHUNK_LANE, flane < _CHUNK_LANE + 3)
    for c in range(nk):
        ka_sc[c * tk:(c + 1) * tk, B_VDIM:] = jnp.where(chunk_lanes, float(c), kf).astype(BF16)
    for c in range(nk):
        kt_sc[:, c * tk:(c + 1) * tk] = ka_sc[c * tk:(c + 1) * tk, :].T
    kk = k.astype(F32)
    kk = kk * kk
    for c, sel in enumerate((lane < HEAD_DIM, lane >= HEAD_DIM)):
        n2 = jnp.sum(jnp.where(sel, kk, 0.0), axis=-1, keepdims=True)
        k2_sc[c] = jnp.broadcast_to(jnp.max(n2, axis=0, keepdims=True), k2_sc.shape[1:])

    def body(i, carry):
        i = jnp.asarray(i, jnp.int32)
        _diff_attn_block(i, reach_ref, slope_ref, q_ref, qf_ref, dist_ref, lam_ref, g_ref, o_ref, qa_sc,
                         ka_sc, va_sc, k2_sc, m_sc, acc_sc, sa_sc, pa_sc, pb_sc, kt_sc, tq=tk, nk=nk,
                         out_scale=out_scale, lambda_init=lambda_init)
        return carry

    lax.fori_loop(0, nk, body, 0)


def _diff_attn_block(i, reach_ref, slope_ref, q_ref, qf_ref, dist_ref, lam_ref, g_ref, o_ref, qa_sc,
                     ka_sc, va_sc, k2_sc, m_sc, acc_sc, sa_sc, pa_sc, pb_sc, kt_sc, *, tq, nk, out_scale,
                     lambda_init):
    h = pl.program_id(1)
    tk = tq
    slope = slope_ref[h]
    reach = reach_ref[h]
    rows = pl.ds(pl.multiple_of(i * tq, tq), tq)
    q = q_ref[rows, :]
    lane = lax.broadcasted_iota(jnp.int32, q.shape, 1)
    zero = jnp.zeros_like(q)
    q0 = jnp.where(lane < HEAD_DIM, q, zero)
    q1 = jnp.where(lane >= HEAD_DIM, q, zero)
    qq = q.astype(F32)
    qq = (qq * qq).astype(BF16)
    comp = lax.broadcasted_iota(jnp.int32, (B_VDIM, LANES), 0)
    n0 = _dot(qq, jnp.where(comp < HEAD_DIM, 1.0, 0.0).astype(BF16)) * (1.0 + 2.0 ** -7)
    n1 = _dot(qq, jnp.where(comp >= HEAD_DIM, 1.0, 0.0).astype(BF16)) * (1.0 + 2.0 ** -7)
    bound = jnp.concatenate([jnp.sqrt(n0) * jnp.sqrt(k2_sc[0][0:1, :]),
                             jnp.sqrt(n1) * jnp.sqrt(k2_sc[1][0:1, :])], axis=0)
    off = slope * (i * tk).astype(F32)
    qf = qf_ref[...]
    qf2 = jnp.concatenate([qf, qf], axis=0)

    def piece_lanes(x, first_lane):
        lane_ = lax.broadcasted_iota(jnp.int32, x.shape, 1)
        out = jnp.zeros_like(x)
        for n, piece in enumerate(_bf16_split(x)):
            out = jnp.where(lane_ == first_lane + n, piece.astype(F32), out)
        return out.astype(BF16)

    def build_queries(with_bound):
        feats = {_LEFT: -qf2, _DIAG: jnp.zeros_like(qf2), _RIGHT: qf2}
        if with_bound:
            base = piece_lanes(bound, _BOUND_LANE)
            side = piece_lanes(jnp.full((SUBLANES, LANES), off, F32), _OFFSET_LANE)[0:1]
            feats = {_LEFT: base + side - qf2, _DIAG: base, _RIGHT: base - side + qf2}
        for variant, feat in feats.items():
            qa_sc[variant, :, B_VDIM:] = feat
            qa_sc[variant, :tq, :B_VDIM] = q0
            qa_sc[variant, tq:, :B_VDIM] = q1

    def scores(j, s_ref):
        variant = jnp.where(j < i, _LEFT, jnp.where(j == i, _DIAG, _RIGHT))
        start = pl.multiple_of(j * tk, tk)
        s_ref[...] = _dot_nt(qa_sc[variant], ka_sc[pl.ds(start, tk), :])

    def diag_bias(j, s_ref):
        @pl.when(j == i)
        def _():
            bias = slope * dist_ref[...]
            s_ref[:tq] -= bias
            s_ref[tq:] -= bias

    def probs(j, p_ref):
        start = pl.multiple_of(j * tk, tk)
        s = _dot(qa_sc[jnp.where(j < i, _LEFT, _RIGHT)], kt_sc[:, pl.ds(start, tk)])
        p_ref[...] = jnp.exp2(s).astype(BF16)

    def values(j):
        return va_sc[pl.ds(pl.multiple_of(j * tk, tk), tk), :]

    build_queries(True)
    lo = jnp.maximum(i - reach, 0)
    count = jnp.minimum(i + reach + 1, nk) - lo - 1

    def off_diag(n):
        j = lo + jnp.minimum(n, count - 1)
        return j + (j >= i).astype(jnp.int32)

    s_diag = _dot(qa_sc[_DIAG], kt_sc[:, pl.ds(pl.multiple_of(i * tk, tk), tk)])
    probs(off_diag(0), pa_sc)
    bias = slope * dist_ref[...]
    p_diag = jnp.exp2(s_diag - jnp.concatenate([bias, bias], axis=0)).astype(BF16)
    acc_sc[...] = _dot(p_diag, values(i))

    def pair(tt, carry):
        n = 2 * tt
        probs(off_diag(n + 1), pb_sc)
        acc_sc[...] += _dot(pa_sc[...], values(off_diag(n)))
        probs(off_diag(n + 2), pa_sc)
        acc_sc[...] += _dot(pb_sc[...], values(off_diag(n + 1)))
        return carry

    lax.fori_loop(0, count // 2, pair, 0)

    @pl.when(count % 2 == 1)
    def _():
        acc_sc[...] += _dot(pa_sc[...], values(off_diag(count - 1)))

    denom_min = jnp.min(acc_sc[:, B_VDIM:B_VDIM + 1])

    @pl.when(jnp.logical_not(denom_min >= _MIN_DENOM))
    def _():
        build_queries(False)
        m_sc[...] = jnp.full_like(m_sc, NEG_INF)
        acc_sc[...] = jnp.zeros_like(acc_sc)

        def chunk(j, carry):
            scores(j, sa_sc)
            diag_bias(j, sa_sc)
            adj = jnp.where(j > i, off, jnp.where(j < i, -off, 0.0))
            start = pl.multiple_of(j * tk, tk)
            s = sa_sc[...]
            m_prev = m_sc[...]
            m_new = jnp.maximum(m_prev, jnp.max(s, axis=-1, keepdims=True) + adj)
            p = jnp.exp2(s - (m_new - adj)).astype(BF16)
            acc_sc[...] = jnp.exp2(m_prev - m_new) * acc_sc[...] + _dot(p, va_sc[pl.ds(start, tk), :])
            m_sc[...] = m_new
            return carry

        lax.fori_loop(0, nk, chunk, 0)

    lp = lam_ref[...]
    lam = (jnp.exp(jnp.sum(lp[0:1] * lp[1:2], axis=-1, keepdims=True))
           - jnp.exp(jnp.sum(lp[2:3] * lp[3:4], axis=-1, keepdims=True)) + lambda_init)
    acc = acc_sc[...]
    o0 = acc[:tq, :B_VDIM] * (1.0 / acc[:tq, B_VDIM:B_VDIM + 1])
    o1 = acc[tq:, :B_VDIM] * (1.0 / acc[tq:, B_VDIM:B_VDIM + 1])
    o_ref[rows, :] = (_rms(o0 - lam * o1, g_ref[...]) * out_scale).astype(o_ref.dtype)


def _bf16_pieces(x, n=3):
    out = []
    r = np.float64(x)
    for _ in range(n):
        p = np.float64(np.float32(r).astype(jnp.bfloat16).astype(np.float32))
        out.append(p)
        r = r - p
    return out


def _alibi_features(slopes2, t):
    pos = np.arange(t)
    hi = (pos // _POS_SPLIT) * _POS_SPLIT
    lo = pos % _POS_SPLIT
    qf = np.zeros((len(slopes2), t, LANES), np.float32)
    kf = np.zeros((len(slopes2), t, LANES), np.float32)
    for h, s in enumerate(slopes2):
        for n, piece in enumerate(_bf16_pieces(s)):
            for base, part in ((0, hi), (3, lo)):
                qf[h, :, base + n] = -piece
                kf[h, :, base + n] = part
                qf[h, :, 6 + base + n] = part
                kf[h, :, 6 + base + n] = piece
        for n, piece in enumerate(_bf16_pieces(s * t)):
            qf[h, :, _CHUNK_LANE + n] = -piece
        kf[h, :, _BOUND_LANE:_BOUND_LANE + 3] = -1.0
        kf[h, :, _OFFSET_LANE:_OFFSET_LANE + 3] = -1.0
    return jnp.asarray(qf, BF16), jnp.asarray(kf, BF16)


def _diff_attention(proj, b_lambda, subln_g, lambda_init, batch, seq, t=512):
    t = min(t, seq // 2)
    nq = seq // t
    assert seq % (2 * t) == 0
    kcol = MIX_WIDTH // B_VDIM
    vcol = 2 * kcol
    slopes2 = [s * LOG2E for s in _alibi_slopes(B_HEADS)]
    reach = [min(nq, int(math.floor((_ZERO_EXP2_ARG / s - 1.0) / t)) + 1) for s in slopes2]
    qf, kf = _alibi_features(slopes2, t)
    pos = np.arange(t)
    dist = jnp.asarray(np.abs(pos[:, None] - pos[None, :]), F32)
    kernel = functools.partial(_diff_attn_kernel, tk=t, out_scale=1.0 - lambda_init,
                               lambda_init=lambda_init)
    return pl.pallas_call(
        kernel,
        grid_spec=pltpu.PrefetchScalarGridSpec(
            num_scalar_prefetch=2,
            grid=(batch, B_HEADS),
            in_specs=[pl.BlockSpec((seq, B_VDIM), lambda b, h, r, s: (b, h)),
                      pl.BlockSpec((seq, B_VDIM), lambda b, h, r, s: (b, kcol + h)),
                      pl.BlockSpec((seq, B_VDIM), lambda b, h, r, s: (b, vcol + h)),
                      pl.BlockSpec((None, t, LANES), lambda b, h, r, s: (h, 0, 0)),
                      pl.BlockSpec((None, t, LANES), lambda b, h, r, s: (h, 0, 0)),
                      pl.BlockSpec((t, t), lambda b, h, r, s: (0, 0)),
                      pl.BlockSpec((4, HEAD_DIM), lambda b, h, r, s: (0, 0)),
                      pl.BlockSpec((1, B_VDIM), lambda b, h, r, s: (0, 0))],
            out_specs=pl.BlockSpec((seq, B_VDIM), lambda b, h, r, s: (b, h)),
            scratch_shapes=[pltpu.VMEM((3, 2 * t, 2 * LANES), BF16),
                            pltpu.VMEM((seq, 2 * LANES), BF16),
                            pltpu.VMEM((seq, 2 * LANES), BF16),
                            pltpu.VMEM((2, 8, LANES), F32),
                            pltpu.VMEM((2 * t, 1), F32),
                            pltpu.VMEM((2 * t, 2 * LANES), F32),
                            pltpu.VMEM((2 * t, t), F32),
                            pltpu.VMEM((2 * t, t), BF16),
                            pltpu.VMEM((2 * t, t), BF16),
                            pltpu.VMEM((2 * LANES, seq), BF16)],
        ),
        out_shape=jax.ShapeDtypeStruct((batch * seq, MIX_WIDTH), BF16),
        compiler_params=_params("parallel", "parallel"),
        name="diff_attention",
    )(jnp.asarray(reach, jnp.int32), jnp.asarray(slopes2, F32), proj, proj, proj, qf, kf, dist,
      b_lambda, subln_g.reshape(1, B_VDIM))


def _router_kernel(x_ref, g_ref, whi_ref, wlo_ref, h_ref, e_ref, gate_ref):
    h = _rms(x_ref[...], g_ref[...])
    h_hi = h.astype(BF16)
    h_lo = (h - h_hi.astype(F32)).astype(BF16)
    whi = whi_ref[...]
    logits = _dot(h_hi, whi) + (_dot(h_hi, wlo_ref[...]) + _dot(h_lo, whi))
    lane = lax.broadcasted_iota(jnp.int32, logits.shape, 1)
    logits = jnp.where(lane < N_EXPERTS, logits, -jnp.inf)
    m1 = jnp.max(logits, axis=-1, keepdims=True)
    i1 = jnp.min(jnp.where(logits == m1, lane, LANES), axis=-1, keepdims=True)
    rest = jnp.where(lane == i1, -jnp.inf, logits)
    m2 = jnp.max(rest, axis=-1, keepdims=True)
    i2 = jnp.min(jnp.where(rest == m2, lane, LANES), axis=-1, keepdims=True)
    r = jnp.exp(m2 - m1)
    g1 = 1.0 / (1.0 + r)
    g2 = r * g1
    e_ref[...] = jnp.where(lane == 0, i1, jnp.where(lane == 1, i2, 0))
    gate_ref[...] = jnp.where(lane == 0, g1, jnp.where(lane == 1, g2, 0.0))
    h_ref[...] = h_hi


def _router(x, g, w_router, tm=2048):
    t, d = x.shape
    tm = min(tm, t)
    wpad = jnp.zeros((d, LANES), F32).at[:, :N_EXPERTS].set(w_router)
    whi = wpad.astype(BF16)
    wlo = (wpad - whi.astype(F32)).astype(BF16)
    return pl.pallas_call(
        _router_kernel,
        grid=(t // tm,),
        in_specs=[pl.BlockSpec((tm, d), lambda i: (i, 0)),
                  pl.BlockSpec((1, d), lambda i: (0, 0)),
                  pl.BlockSpec((d, LANES), lambda i: (0, 0)),
                  pl.BlockSpec((d, LANES), lambda i: (0, 0))],
        out_specs=[pl.BlockSpec((tm, d), lambda i: (i, 0)),
                   pl.BlockSpec((tm, LANES), lambda i: (i, 0)),
                   pl.BlockSpec((tm, LANES), lambda i: (i, 0))],
        out_shape=[jax.ShapeDtypeStruct((t, d), BF16),
                   jax.ShapeDtypeStruct((t, LANES), jnp.int32),
                   jax.ShapeDtypeStruct((t, LANES), F32)],
        compiler_params=_params("parallel"),
        name="router",
    )(x, g.reshape(1, d), whi, wlo)


def _slab_sizes(rows, tc):
    quarter, half = tc // 4, tc // 2
    return ((quarter, rows <= quarter),
            (half, jnp.logical_and(rows > quarter, rows <= half)),
            (tc, rows > half))


def _slab_stride(slab_ref, n, step, tc):
    most = slab_ref[(n + step) * N_EXPERTS]
    for e in range(1, N_EXPERTS):
        most = jnp.maximum(most, slab_ref[(n + step) * N_EXPERTS + e])
    return jnp.where(most <= tc // 2, tc // 2, tc)


def _dispatch_kernel(slab_ref, fill_ref, h_ref, dest_ref, gate_ref, xs_hbm, gs_hbm,
                     xslab_sc, gslab_sc, zx_sc, zg_sc, sem, zsem):
    i = pl.program_id(0)
    n = pl.num_programs(0)
    slot = i % 2
    tc = h_ref.shape[0]
    tm = zx_sc.shape[0]

    def slab_dma(step, sl, op):
        stride = _slab_stride(slab_ref, n, step, tc)
        for e in range(N_EXPERTS):
            start = pl.multiple_of(slab_ref[step * N_EXPERTS + e], SUBLANES)
            first = pl.multiple_of(e * stride, SUBLANES)
            for size, cond in _slab_sizes(slab_ref[(n + step) * N_EXPERTS + e], tc):
                @pl.when(cond)
                def _():
                    for src, dst in ((xslab_sc, xs_hbm), (gslab_sc, gs_hbm)):
                        copy = pltpu.make_async_copy(src.at[sl, pl.ds(first, size)],
                                                     dst.at[pl.ds(start, size)], sem.at[sl])
                        getattr(copy, op)()

    def fill_copies(start):
        start = pl.multiple_of(start, SUBLANES)
        return [pltpu.make_async_copy(zx_sc, xs_hbm.at[pl.ds(start, tm)], zsem),
                pltpu.make_async_copy(zg_sc, gs_hbm.at[pl.ds(start, tm)], zsem)]

    @pl.when(i == 0)
    def _():
        zx_sc[...] = jnp.zeros_like(zx_sc)
        zg_sc[...] = jnp.zeros_like(zg_sc)
        for half in range(2):
            for e in range(N_EXPERTS):
                for c in fill_copies(fill_ref[half * N_EXPERTS + e]):
                    c.start()
            for e in range(N_EXPERTS):
                for c in fill_copies(fill_ref[half * N_EXPERTS + e]):
                    c.wait()
        first_free = fill_ref[2 * N_EXPERTS]
        n_tiles = xs_hbm.shape[0] // tm

        def start_tile(b, carry):
            for c in fill_copies(b * tm):
                c.start()
            return carry

        def wait_tile(b, carry):
            for c in fill_copies(b * tm):
                c.wait()
            return carry

        lax.fori_loop(first_free, n_tiles, start_tile, 0)
        lax.fori_loop(first_free, n_tiles, wait_tile, 0)

    dest = dest_ref[0]
    gate = gate_ref[...]
    lane = lax.broadcasted_iota(jnp.int32, gate.shape, 1)

    def gate_pieces(col):
        out = jnp.zeros(gate.shape, F32)
        for n_, piece in enumerate(_bf16_split(col)):
            out = jnp.where(lane == n_, piece.astype(F32), out)
        return out.astype(BF16)

    def place_rows(stride):
        pos = lax.broadcasted_iota(jnp.int32, (stride, tc), 0)
        sel0, sel1 = [], []
        for e in range(N_EXPERTS):
            start = slab_ref[i * N_EXPERTS + e]
            sel0.append(jnp.where(dest[0:1, :] - start == pos, 1.0, 0.0))
            sel1.append(jnp.where(dest[1:2, :] - start == pos, 1.0, 0.0))
        sel0 = jnp.concatenate(sel0, axis=0).astype(BF16)
        sel1 = jnp.concatenate(sel1, axis=0).astype(BF16)
        rows = pl.ds(0, N_EXPERTS * stride)
        xslab_sc[slot, rows] = _dot(sel0 + sel1, h_ref[...])
        gslab_sc[slot, rows] = (_dot(sel0, gate_pieces(gate[:, 0:1]))
                                + _dot(sel1, gate_pieces(gate[:, 1:2])))

    compact = _slab_stride(slab_ref, n, i, tc) < tc
    pl.when(compact)(lambda: place_rows(tc // 2))
    pl.when(jnp.logical_not(compact))(lambda: place_rows(tc))

    @pl.when(i > 0)
    def _():
        slab_dma(i - 1, 1 - slot, "wait")

    slab_dma(i, slot, "start")

    @pl.when(i == n - 1)
    def _():
        slab_dma(i, slot, "wait")


def _dispatch(h, dest_t, gates, slab, fill, p_rows, tm, tc):
    t, d = h.shape
    nb = t // tc
    return pl.pallas_call(
        _dispatch_kernel,
        grid_spec=pltpu.PrefetchScalarGridSpec(
            num_scalar_prefetch=2,
            grid=(nb,),
            in_specs=[pl.BlockSpec((tc, d), lambda i, s, z: (i, 0)),
                      pl.BlockSpec((1, TOP_K, tc), lambda i, s, z: (i, 0, 0)),
                      pl.BlockSpec((tc, LANES), lambda i, s, z: (i, 0))],
            out_specs=[pl.BlockSpec(memory_space=pl.ANY), pl.BlockSpec(memory_space=pl.ANY)],
            scratch_shapes=[pltpu.VMEM((2, N_EXPERTS * tc, d), F32),
                            pltpu.VMEM((2, N_EXPERTS * tc, LANES), F32),
                            pltpu.VMEM((tm, d), F32), pltpu.VMEM((tm, LANES), F32),
                            pltpu.SemaphoreType.DMA((2,)), pltpu.SemaphoreType.DMA(())],
        ),
        out_shape=[jax.ShapeDtypeStruct((p_rows, d), F32), jax.ShapeDtypeStruct((p_rows, LANES), F32)],
        compiler_params=_params("arbitrary"),
        name="moe_dispatch",
    )(slab, fill, h, dest_t, gates)


def _moe_kernel(blk_e_ref, used_ref, xblk_ref, x_ref, gs_ref, wg_ref, wu_ref, wd_ref, o_ref,
                xb_sc, acc_sc):
    i = pl.program_id(0)
    f = pl.program_id(1)
    used = used_ref[i] > 0

    @pl.when(jnp.logical_and(used, f == 0))
    def _():
        xb_sc[...] = x_ref[...].astype(BF16)
        acc_sc[...] = jnp.zeros_like(acc_sc)

    @pl.when(used)
    def _():
        acc_sc[...] += _swiglu_tile(xb_sc[...], wg_ref, wu_ref, wd_ref)

    @pl.when(f == pl.num_programs(1) - 1)
    def _():
        gs = gs_ref[...]
        gate = gs[:, 0:1] + gs[:, 1:2] + gs[:, 2:3]
        o_ref[...] = jnp.where(used, acc_sc[...] * gate, 0.0)


def _moe_experts(xs, gs, blk_e, blk_used, xblk, wg, wu, wd, tm, tf=1792):
    p, d = xs.shape
    nblk = p // tm
    nf = wg.shape[2] // tf

    def fidx(i, f, used_ref):
        return jnp.where(used_ref[i] > 0, f, nf - 1)

    return pl.pallas_call(
        _moe_kernel,
        grid_spec=pltpu.PrefetchScalarGridSpec(
            num_scalar_prefetch=3,
            grid=(nblk, nf),
            in_specs=[pl.BlockSpec((tm, d), lambda i, f, be, us, xb: (xb[i], 0)),
                      pl.BlockSpec((tm, LANES), lambda i, f, be, us, xb: (xb[i], 0)),
                      pl.BlockSpec((None, d, tf), lambda i, f, be, us, xb: (be[i], 0, fidx(i, f, us))),
                      pl.BlockSpec((None, d, tf), lambda i, f, be, us, xb: (be[i], 0, fidx(i, f, us))),
                      pl.BlockSpec((None, tf, d), lambda i, f, be, us, xb: (be[i], fidx(i, f, us), 0))],
            out_specs=pl.BlockSpec((tm, d), lambda i, f, be, us, xb: (i, 0)),
            scratch_shapes=[pltpu.VMEM((tm, d), BF16), pltpu.VMEM((tm, d), F32)],
        ),
        out_shape=jax.ShapeDtypeStruct((p, d), F32),
        compiler_params=_params("arbitrary", "arbitrary"),
        name="moe_experts",
    )(blk_e, blk_used, xblk, xs, gs, wg, wu, wd)


def _combine_kernel(slab_ref, ys_hbm, dest_ref, x_ref, g_ref, o_ref, slab_sc, sem):
    i = pl.program_id(0)
    n = pl.num_programs(0)
    slot = i % 2
    tc = x_ref.shape[0]

    def slab_dma(step, sl, op):
        stride = _slab_stride(slab_ref, n, step, tc)
        for e in range(N_EXPERTS):
            start = pl.multiple_of(slab_ref[step * N_EXPERTS + e], SUBLANES)
            first = pl.multiple_of(e * stride, SUBLANES)
            for size, cond in _slab_sizes(slab_ref[(n + step) * N_EXPERTS + e], tc):
                @pl.when(cond)
                def _():
                    copy = pltpu.make_async_copy(ys_hbm.at[pl.ds(start, size)],
                                                 slab_sc.at[sl, pl.ds(first, size)], sem.at[sl])
                    getattr(copy, op)()

    @pl.when(i == 0)
    def _():
        slab_sc[...] = jnp.zeros_like(slab_sc)
        slab_dma(0, 0, "start")

    @pl.when(i + 1 < n)
    def _():
        slab_dma(i + 1, 1 - slot, "start")

    dest = dest_ref[...]
    slab_dma(i, slot, "wait")

    def gather_rows(stride):
        pos = lax.broadcasted_iota(jnp.int32, (tc, stride), 1)
        sel = []
        for e in range(N_EXPERTS):
            start = slab_ref[i * N_EXPERTS + e]
            hit = jnp.logical_or(dest[:, 0:1] - start == pos, dest[:, 1:2] - start == pos)
            sel.append(jnp.where(hit, 1.0, 0.0))
        sel = jnp.concatenate(sel, axis=1).astype(BF16)
        y = _dot(sel, slab_sc[slot, pl.ds(0, N_EXPERTS * stride)].astype(BF16))
        o_ref[...] = x_ref[...] + _rms(y, g_ref[...])

    compact = _slab_stride(slab_ref, n, i, tc) < tc
    pl.when(compact)(lambda: gather_rows(tc // 2))
    pl.when(jnp.logical_not(compact))(lambda: gather_rows(tc))


def _combine(ys, dest, slab, x, g, tc):
    t, d = x.shape
    return pl.pallas_call(
        _combine_kernel,
        grid_spec=pltpu.PrefetchScalarGridSpec(
            num_scalar_prefetch=1,
            grid=(t // tc,),
            in_specs=[pl.BlockSpec(memory_space=pl.ANY),
                      pl.BlockSpec((tc, LANES), lambda i, s: (i, 0)),
                      pl.BlockSpec((tc, d), lambda i, s: (i, 0)),
                      pl.BlockSpec((1, d), lambda i, s: (0, 0))],
            out_specs=pl.BlockSpec((tc, d), lambda i, s: (i, 0)),
            scratch_shapes=[pltpu.VMEM((2, N_EXPERTS * tc, d), F32), pltpu.SemaphoreType.DMA((2,))],
        ),
        out_shape=jax.ShapeDtypeStruct((t, d), F32),
        compiler_params=_params("arbitrary"),
        name="moe_combine",
    )(slab, ys, dest, x, g.reshape(1, d))


def _route(top_e, tm, tc):
    t = top_e.shape[0]
    a = t * TOP_K
    nb = t // tc
    e_flat = top_e.reshape(a)
    onehot = (e_flat[:, None] == jnp.arange(N_EXPERTS, dtype=jnp.int32)[None, :]).astype(jnp.int32)
    csum = jnp.cumsum(onehot, axis=0)
    rank = jnp.sum(onehot * (csum - 1), axis=1)
    upto = csum[TOP_K * tc - 1::TOP_K * tc]
    before = jnp.concatenate([jnp.zeros((1, N_EXPERTS), jnp.int32), upto[:nb - 1]], axis=0)
    rows = (upto - before + SUBLANES - 1) // SUBLANES * SUBLANES
    rows_before = jnp.cumsum(rows, axis=0) - rows
    total = jnp.sum(rows, axis=0)
    region = (total + 2 * tm - 1) // tm * tm
    rends = jnp.cumsum(region)
    rstarts = rends - region
    slab = (rstarts[None, :] + rows_before).astype(jnp.int32)
    shift = jnp.broadcast_to((slab - before)[:, None, :], (nb, TOP_K * tc, N_EXPERTS)).reshape(a, N_EXPERTS)
    dest = (rank + jnp.sum(onehot * shift, axis=1)).astype(jnp.int32).reshape(t, TOP_K)
    slab = jnp.concatenate([slab.reshape(nb * N_EXPERTS), rows.reshape(nb * N_EXPERTS)])
    row_end = rstarts + total
    fill = jnp.concatenate([row_end, rends - tm, rends[-1:] // tm]).astype(jnp.int32)
    nblk = -(-(a + nb * N_EXPERTS * (SUBLANES - 1)) // tm) + 2 * N_EXPERTS
    blk_start = jnp.arange(nblk, dtype=jnp.int32) * tm
    blk_e = jnp.minimum(jnp.sum((rends[None, :] <= blk_start[:, None]).astype(jnp.int32), axis=1),
                        N_EXPERTS - 1)
    blk_used = jnp.logical_and(blk_start < row_end[blk_e], blk_start < rends[-1]).astype(jnp.int32)
    first_used = jnp.argmax(blk_used).astype(jnp.int32)
    xblk = jnp.where(blk_used > 0, jnp.arange(nblk, dtype=jnp.int32), first_used)
    return dest, slab, fill, blk_e, blk_used, xblk, nblk * tm


def _moe(x, h, e_pad, gates, g_out, wg, wu, wd, tm=512, tc=256):
    t = x.shape[0]
    tc = min(tc, t)
    dest, slab, fill, blk_e, blk_used, xblk, p_rows = _route(e_pad[:, :TOP_K], tm, tc)
    dest_t = dest.reshape(t // tc, tc, TOP_K).transpose(0, 2, 1)
    dest_pad = jnp.zeros((t, LANES), jnp.int32).at[:, :TOP_K].set(dest)
    xs, gs = _dispatch(h, dest_t, gates, slab, fill, p_rows, tm, tc)
    ys = _moe_experts(xs, gs, blk_e, blk_used, xblk, wg, wu, wd, tm)
    return _combine(ys, dest_pad, slab, x, g_out, tc)


def kernel(x, mem, norm_g, mem_norm_g, w_mem_kv, a_w_in, a_sink, a_w_out, b_w_in, b_lambda,
           b_subln_g, b_w_out, ffn_w_gate, ffn_w_up, ffn_w_down, moe_w_router, moe_w_gate,
           moe_w_up, moe_w_down):
    batch, seq, d = x.shape
    n_mem = mem.shape[1]
    bf = lambda w: w.astype(BF16)
    xt = x.reshape(batch * seq, d)
    mkv = _norm_matmul(mem.reshape(batch * n_mem, d), mem_norm_g, bf(w_mem_kv),
                       jnp.ones((w_mem_kv.shape[1],), F32))

    g = norm_g[0]
    proj = _norm_matmul(xt, g[0], bf(a_w_in[0]), _query_col_scale(a_w_in.shape[2], A_Q_HEADS * HEAD_DIM))
    mix = _window_attention(proj, a_sink[0], batch, seq)
    mem_out = _mem_attention(proj, (A_Q_HEADS + 2 * A_KV_HEADS) * HEAD_DIM // MEM_WIDTH, mkv, batch, seq)
    xt = _out_proj(mix, mem_out, bf(a_w_out[0]), xt, g[1])
    xt = _ffn(xt, g[2], g[3], bf(ffn_w_gate[0]), bf(ffn_w_up[0]), bf(ffn_w_down[0]))

    g = norm_g[1]
    lambda_init = 0.8 - 0.6 * math.exp(-0.3 * 1)
    proj = _norm_matmul(xt, g[0], bf(b_w_in[0]), _query_col_scale(b_w_in.shape[2], MIX_WIDTH))
    mix = _diff_attention(proj, b_lambda[0], b_subln_g[0], lambda_init, batch, seq)
    mem_out = _mem_attention(proj, 3 * MIX_WIDTH // MEM_WIDTH, mkv, batch, seq)
    xt = _out_proj(mix, mem_out, bf(b_w_out[0]), xt, g[1])
    h, e_pad, gates = _router(xt, g[2], moe_w_router[0])
    xt = _moe(xt, h, e_pad, gates, g[3], bf(moe_w_gate[0]), bf(moe_w_up[0]), bf(moe_w_down[0]))
    return xt.reshape(batch, seq, d)
```

```python
import functools
import math

import jax
import jax.numpy as jnp
import numpy as np
from jax import lax
from jax.experimental import pallas as pl
from jax.experimental.pallas import tpu as pltpu

D_MODEL = 1024
HEAD_DIM = 64
MIX_WIDTH = 768
MEM_WIDTH = 256
MEM_HEADS = 4
A_Q_HEADS = 12
A_KV_HEADS = 4
A_GROUP = 3
WINDOW = 128
B_HEADS = 6
B_VDIM = 128
D_FF = 3584
N_EXPERTS = 8
TOP_K = 2
EPS = 1e-6
NEG_INF = -1e30
LOG2E = math.log2(math.e)
QK_SCALE = HEAD_DIM ** -0.5 * LOG2E
LANES = 128
SUBLANES = 8
VMEM_LIMIT_BYTES = 56 * 1024 * 1024

BF16 = jnp.bfloat16
F32 = jnp.float32


def _params(*sem):
    return pltpu.CompilerParams(dimension_semantics=sem, vmem_limit_bytes=VMEM_LIMIT_BYTES)


def _rms(x, g):
    return x * lax.rsqrt(jnp.mean(x * x, axis=-1, keepdims=True) + EPS) * g


def _dot(a, b):
    return jnp.dot(a, b, preferred_element_type=F32)


def _dot_nt(a, b):
    return lax.dot_general(a, b, (((1,), (1,)), ((), ())), preferred_element_type=F32)


def _alibi_slopes(n):
    return [2.0 ** (-8.0 * (i + 1) / n) for i in range(n)]


def _norm_matmul_kernel(x_ref, g_ref, w_ref, cs_ref, o_ref, *, chunk):
    h = _rms(x_ref[...], g_ref[...]).astype(BF16)
    for c in range(o_ref.shape[1] // chunk):
        sl = slice(c * chunk, (c + 1) * chunk)
        o_ref[:, sl] = (_dot(h, w_ref[:, sl]) * cs_ref[:, sl]).astype(o_ref.dtype)


def _norm_matmul(x, g, w, col_scale, tm=1024, chunk=512):
    t, d = x.shape
    n = w.shape[1]
    tm = min(tm, t)
    return pl.pallas_call(
        functools.partial(_norm_matmul_kernel, chunk=min(chunk, n)),
        grid=(t // tm,),
        in_specs=[pl.BlockSpec((tm, d), lambda i: (i, 0)),
                  pl.BlockSpec((1, d), lambda i: (0, 0)),
                  pl.BlockSpec((d, n), lambda i: (0, 0)),
                  pl.BlockSpec((1, n), lambda i: (0, 0))],
        out_specs=pl.BlockSpec((tm, n), lambda i: (i, 0)),
        out_shape=jax.ShapeDtypeStruct((t, n), BF16),
        compiler_params=_params("parallel"),
        name="norm_matmul",
    )(x, g.reshape(1, d), w, col_scale.reshape(1, n))


def _query_col_scale(n, mix_q_cols):
    cols = np.arange(n)
    return jnp.asarray(np.where((cols < mix_q_cols) | (cols >= n - MEM_WIDTH), QK_SCALE, 1.0), F32)


def _window_kernel(q_ref, kp_ref, kc_ref, kn_ref, vp_ref, vc_ref, vn_ref, bias_ref, o_ref):
    n = pl.program_id(1)
    nb = pl.num_programs(1)
    blk = kp_ref.shape[0]
    sub = q_ref.shape[0] // blk
    zeros = jnp.zeros((blk, kp_ref.shape[1]), kp_ref.dtype)
    k_all = jnp.concatenate([kp_ref[...], kc_ref[...], kn_ref[...]], axis=0)
    v_all = jnp.concatenate([vp_ref[...], vc_ref[...], vn_ref[...]], axis=0)
    col = lax.broadcasted_iota(jnp.int32, (1, 4 * blk), 1)
    half = lax.broadcasted_iota(jnp.int32, (4 * blk, LANES), 1) < HEAD_DIM
    low_half = jnp.where(half, 1.0, 0.0).astype(BF16)
    high_half = jnp.where(half, 0.0, 1.0).astype(BF16)
    for u in range(sub):
        k = jnp.concatenate([k_all[u * blk:(u + 3) * blk], zeros], axis=0)
        v = jnp.concatenate([v_all[u * blk:(u + 3) * blk], zeros], axis=0)
        edge = jnp.zeros((1, 4 * blk), F32)
        if u == 0:
            edge = jnp.where(jnp.logical_and(n == 0, col < blk), NEG_INF, edge)
        if u == sub - 1:
            edge = jnp.where(jnp.logical_and(n == nb - 1,
                                             jnp.logical_and(col >= 2 * blk, col < 3 * blk)), NEG_INF, edge)
        q = q_ref[u * blk:(u + 1) * blk, :]
        outs = []
        for kh in range(A_KV_HEADS):
            heads = range(kh * A_GROUP, (kh + 1) * A_GROUP)
            kv = slice(kh * HEAD_DIM, (kh + 1) * HEAD_DIM)
            qs = jnp.concatenate([q[:, h * HEAD_DIM:(h + 1) * HEAD_DIM] for h in heads], axis=0)
            s = _dot_nt(qs, k[:, kv]) + bias_ref[kh] + edge
            p = jnp.exp2(s - jnp.max(s, axis=-1, keepdims=True)).astype(BF16)
            low = kh % 2 == 0
            slab = v[:, (kh // 2) * LANES:(kh // 2 + 1) * LANES]
            own, other = (low_half, high_half) if low else (high_half, low_half)
            ov = _dot(p, jnp.concatenate([slab * own + other, own], axis=1))
            den = ov[:, LANES:]
            o = ov[:, :LANES] * (1.0 / jnp.where(den == 0.0, 1.0, den))
            o = o[:, :HEAD_DIM] if low else o[:, HEAD_DIM:]
            outs.extend(o[g * blk:(g + 1) * blk] for g in range(A_GROUP))
        o_ref[u * blk:(u + 1) * blk, :] = jnp.concatenate(outs, axis=-1).astype(o_ref.dtype)


def _window_bias(blk, sink):
    qi = np.arange(blk)[:, None]
    kj = np.arange(3 * blk)[None, :]
    dist = np.abs(blk + qi - kj)
    slopes = _alibi_slopes(A_Q_HEADS)
    tables = np.stack([np.where(dist <= WINDOW, -(slopes[h] * LOG2E) * dist, NEG_INF)
                       for h in range(A_Q_HEADS)])
    pad = jnp.full((A_Q_HEADS, blk, blk), NEG_INF, F32)
    pad = pad.at[:, :, 0].set(jnp.broadcast_to((sink.astype(F32) * LOG2E)[:, None], (A_Q_HEADS, blk)))
    full = jnp.concatenate([jnp.asarray(tables, F32), pad], axis=2)
    return full.reshape(A_KV_HEADS, A_GROUP * blk, 4 * blk)


def _window_attention(proj, sink, batch, seq, sub=4):
    blk = WINDOW
    nb = seq // blk
    sub = min(sub, nb)
    ns = nb // sub
    kcol = MIX_WIDTH // 256
    vcol = kcol + 1
    bias = _window_bias(blk, sink)

    def cur(col):
        return pl.BlockSpec((sub * blk, 256), lambda b, n: (b * ns + n, col))

    def prev(col):
        return pl.BlockSpec((blk, 256), lambda b, n: (b * nb + jnp.maximum(n * sub - 1, 0), col))

    def nxt(col):
        return pl.BlockSpec((blk, 256), lambda b, n: (b * nb + jnp.minimum((n + 1) * sub, nb - 1), col))

    return pl.pallas_call(
        _window_kernel,
        grid=(batch, ns),
        in_specs=[pl.BlockSpec((sub * blk, MIX_WIDTH), lambda b, n: (b * ns + n, 0)),
                  prev(kcol), cur(kcol), nxt(kcol), prev(vcol), cur(vcol), nxt(vcol),
                  pl.BlockSpec(bias.shape, lambda b, n: (0, 0, 0))],
        out_specs=pl.BlockSpec((sub * blk, MIX_WIDTH), lambda b, n: (b * ns + n, 0)),
        out_shape=jax.ShapeDtypeStruct((batch * seq, MIX_WIDTH), BF16),
        compiler_params=_params("parallel", "parallel"),
        name="window_attention",
    )(proj, proj, proj, proj, proj, proj, proj, bias)


def _mem_attn_kernel(q_ref, mkv_ref, o_ref):
    q = q_ref[...]
    mkv = mkv_ref[...]
    half = lax.broadcasted_iota(jnp.int32, (mkv.shape[0], LANES), 1) < HEAD_DIM
    low_half = jnp.where(half, 1.0, 0.0).astype(BF16)
    high_half = jnp.where(half, 0.0, 1.0).astype(BF16)
    outs = []
    for h in range(MEM_HEADS):
        sl = slice(h * HEAD_DIM, (h + 1) * HEAD_DIM)
        s = _dot_nt(q[:, sl], mkv[:, sl])
        p = jnp.exp2(s - jnp.max(s, axis=-1, keepdims=True)).astype(BF16)
        low = h % 2 == 0
        slab = mkv[:, MEM_WIDTH + (h // 2) * LANES:MEM_WIDTH + (h // 2 + 1) * LANES]
        own, other = (low_half, high_half) if low else (high_half, low_half)
        ov = _dot(p, jnp.concatenate([slab * own + other, own], axis=1))
        den = ov[:, LANES:]
        o = ov[:, :LANES] * (1.0 / jnp.where(den == 0.0, 1.0, den))
        outs.append(o[:, :HEAD_DIM] if low else o[:, HEAD_DIM:])
    o_ref[...] = jnp.concatenate(outs, axis=-1).astype(o_ref.dtype)


def _mem_attention(proj, qcol, mkv, batch, seq, tm=2048):
    tm = min(tm, seq)
    per_batch = seq // tm
    n_mem = mkv.shape[0] // batch
    return pl.pallas_call(
        _mem_attn_kernel,
        grid=(batch * per_batch,),
        in_specs=[pl.BlockSpec((tm, MEM_WIDTH), lambda i: (i, qcol)),
                  pl.BlockSpec((n_mem, 2 * MEM_WIDTH), lambda i: (i // per_batch, 0))],
        out_specs=pl.BlockSpec((tm, MEM_WIDTH), lambda i: (i, 0)),
        out_shape=jax.ShapeDtypeStruct((batch * seq, MEM_WIDTH), BF16),
        compiler_params=_params("parallel"),
        name="mem_attention",
    )(proj, mkv)


def _out_proj_kernel(mix_ref, mem_ref, w1_ref, w2_ref, x_ref, g_ref, o_ref):
    o = _dot(mix_ref[...], w1_ref[...]) + _dot(mem_ref[...], w2_ref[...])
    o_ref[...] = x_ref[...] + _rms(o, g_ref[...])


def _out_proj(mix, mem_out, w_out, x, g, tm=1024):
    t, d = x.shape
    tm = min(tm, t)
    w1 = w_out[:MIX_WIDTH]
    w2 = w_out[MIX_WIDTH:]
    return pl.pallas_call(
        _out_proj_kernel,
        grid=(t // tm,),
        in_specs=[pl.BlockSpec((tm, MIX_WIDTH), lambda i: (i, 0)),
                  pl.BlockSpec((tm, MEM_WIDTH), lambda i: (i, 0)),
                  pl.BlockSpec((MIX_WIDTH, d), lambda i: (0, 0)),
                  pl.BlockSpec((MEM_WIDTH, d), lambda i: (0, 0)),
                  pl.BlockSpec((tm, d), lambda i: (i, 0)),
                  pl.BlockSpec((1, d), lambda i: (0, 0))],
        out_specs=pl.BlockSpec((tm, d), lambda i: (i, 0)),
        out_shape=jax.ShapeDtypeStruct((t, d), F32),
        compiler_params=_params("parallel"),
        name="out_proj",
    )(mix, mem_out, w1, w2, x, g.reshape(1, d))


def _swiglu_tile(h, wg_ref, wu_ref, wd_ref, chunk=256):
    out = None
    for c in range(wg_ref.shape[1] // chunk):
        sl = slice(c * chunk, (c + 1) * chunk)
        a = _dot(h, wg_ref[:, sl])
        u = _dot(h, wu_ref[:, sl])
        z = (a * jax.nn.sigmoid(a) * u).astype(BF16)
        y = _dot(z, wd_ref[sl, :])
        out = y if out is None else out + y
    return out


def _ffn_kernel(x_ref, gin_ref, gout_ref, wg_ref, wu_ref, wd_ref, o_ref, h_sc, acc_sc):
    f = pl.program_id(1)

    @pl.when(f == 0)
    def _():
        h_sc[...] = _rms(x_ref[...], gin_ref[...]).astype(BF16)
        acc_sc[...] = jnp.zeros_like(acc_sc)

    acc_sc[...] += _swiglu_tile(h_sc[...], wg_ref, wu_ref, wd_ref)

    @pl.when(f == pl.num_programs(1) - 1)
    def _():
        o_ref[...] = x_ref[...] + _rms(acc_sc[...], gout_ref[...])


def _ffn(x, g_in, g_out, wg, wu, wd, tm=1024, tf=1792):
    t, d = x.shape
    ff = wg.shape[1]
    tm = min(tm, t)
    return pl.pallas_call(
        _ffn_kernel,
        grid=(t // tm, ff // tf),
        in_specs=[pl.BlockSpec((tm, d), lambda i, f: (i, 0)),
                  pl.BlockSpec((1, d), lambda i, f: (0, 0)),
                  pl.BlockSpec((1, d), lambda i, f: (0, 0)),
                  pl.BlockSpec((d, tf), lambda i, f: (0, f)),
                  pl.BlockSpec((d, tf), lambda i, f: (0, f)),
                  pl.BlockSpec((tf, d), lambda i, f: (f, 0))],
        out_specs=pl.BlockSpec((tm, d), lambda i, f: (i, 0)),
        out_shape=jax.ShapeDtypeStruct((t, d), F32),
        scratch_shapes=[pltpu.VMEM((tm, d), BF16), pltpu.VMEM((tm, d), F32)],
        compiler_params=_params("parallel", "arbitrary"),
        name="ffn",
    )(x, g_in.reshape(1, d), g_out.reshape(1, d), wg, wu, wd)


_POS_SPLIT = 256
_LEFT, _DIAG, _RIGHT = 0, 1, 2
_BOUND_LANE = 12
_CHUNK_LANE = 15
_OFFSET_LANE = 18
_ZERO_EXP2_ARG = 136.0
_MIN_DENOM = 2.0 ** -60


def _bf16_split(x):
    hi = x.astype(BF16)
    r = x - hi.astype(F32)
    mid = r.astype(BF16)
    lo = (r - mid.astype(F32)).astype(BF16)
    return hi, mid, lo


def _diff_attn_kernel(reach_ref, slope_ref, q_ref, k_ref, v_ref, qf_ref, kf_ref, dist_ref, lam_ref,
                      g_ref, o_ref, qa_sc, ka_sc, va_sc, k2_sc, m_sc, acc_sc, sa_sc, pa_sc, pb_sc, kt_sc, *, tk,
                      out_scale, lambda_init):
    seq = k_ref.shape[0]
    nk = seq // tk
    k = k_ref[...]
    ka_sc[:, :B_VDIM] = k
    va_sc[:, :B_VDIM] = v_ref[...]
    lane = lax.broadcasted_iota(jnp.int32, (seq, LANES), 1)
    va_sc[:, B_VDIM:] = jnp.where(lane == 0, 1.0, 0.0).astype(BF16)
    kf = kf_ref[...]
    flane = lax.broadcasted_iota(jnp.int32, kf.shape, 1)
    chunk_lanes = jnp.logical_and(flane >= _CHUNK_LANE, flane < _CHUNK_LANE + 3)
    for c in range(nk):
        ka_sc[c * tk:(c + 1) * tk, B_VDIM:] = jnp.where(chunk_lanes, float(c), kf).astype(BF16)
    for c in range(nk):
        kt_sc[:, c * tk:(c + 1) * tk] = ka_sc[c * tk:(c + 1) * tk, :].T
    kk = k.astype(F32)
    kk = kk * kk
    for c, sel in enumerate((lane < HEAD_DIM, lane >= HEAD_DIM)):
        n2 = jnp.sum(jnp.where(sel, kk, 0.0), axis=-1, keepdims=True)
        k2_sc[c] = jnp.broadcast_to(jnp.max(n2, axis=0, keepdims=True), k2_sc.shape[1:])

    def body(i, carry):
        i = jnp.asarray(i, jnp.int32)
        _diff_attn_block(i, reach_ref, slope_ref, q_ref, qf_ref, dist_ref, lam_ref, g_ref, o_ref, qa_sc,
                         ka_sc, va_sc, k2_sc, m_sc, acc_sc, sa_sc, pa_sc, pb_sc, kt_sc, tq=tk, nk=nk,
                         out_scale=out_scale, lambda_init=lambda_init)
        return carry

    lax.fori_loop(0, nk, body, 0)


def _diff_attn_block(i, reach_ref, slope_ref, q_ref, qf_ref, dist_ref, lam_ref, g_ref, o_ref, qa_sc,
                     ka_sc, va_sc, k2_sc, m_sc, acc_sc, sa_sc, pa_sc, pb_sc, kt_sc, *, tq, nk, out_scale,
                     lambda_init):
    h = pl.program_id(1)
    tk = tq
    slope = slope_ref[h]
    reach = reach_ref[h]
    rows = pl.ds(pl.multiple_of(i * tq, tq), tq)
    q = q_ref[rows, :]
    lane = lax.broadcasted_iota(jnp.int32, q.shape, 1)
    zero = jnp.zeros_like(q)
    q0 = jnp.where(lane < HEAD_DIM, q, zero)
    q1 = jnp.where(lane >= HEAD_DIM, q, zero)
    qq = q.astype(F32)
    qq = (qq * qq).astype(BF16)
    comp = lax.broadcasted_iota(jnp.int32, (B_VDIM, LANES), 0)
    n0 = _dot(qq, jnp.where(comp < HEAD_DIM, 1.0, 0.0).astype(BF16)) * (1.0 + 2.0 ** -7)
    n1 = _dot(qq, jnp.where(comp >= HEAD_DIM, 1.0, 0.0).astype(BF16)) * (1.0 + 2.0 ** -7)
    bound = jnp.concatenate([jnp.sqrt(n0) * jnp.sqrt(k2_sc[0][0:1, :]),
                             jnp.sqrt(n1) * jnp.sqrt(k2_sc[1][0:1, :])], axis=0)
    off = slope * (i * tk).astype(F32)
    qf = qf_ref[...]
    qf2 = jnp.concatenate([qf, qf], axis=0)

    def piece_lanes(x, first_lane):
        lane_ = lax.broadcasted_iota(jnp.int32, x.shape, 1)
        out = jnp.zeros_like(x)
        for n, piece in enumerate(_bf16_split(x)):
            out = jnp.where(lane_ == first_lane + n, piece.astype(F32), out)
        return out.astype(BF16)

    def build_queries(with_bound):
        feats = {_LEFT: -qf2, _DIAG: jnp.zeros_like(qf2), _RIGHT: qf2}
        if with_bound:
            base = piece_lanes(bound, _BOUND_LANE)
            side = piece_lanes(jnp.full((SUBLANES, LANES), off, F32), _OFFSET_LANE)[0:1]
            feats = {_LEFT: base + side - qf2, _DIAG: base, _RIGHT: base - side + qf2}
        for variant, feat in feats.items():
            qa_sc[variant, :, B_VDIM:] = feat
            qa_sc[variant, :tq, :B_VDIM] = q0
            qa_sc[variant, tq:, :B_VDIM] = q1

    def scores(j, s_ref):
        variant = jnp.where(j < i, _LEFT, jnp.where(j == i, _DIAG, _RIGHT))
        start = pl.multiple_of(j * tk, tk)
        s_ref[...] = _dot_nt(qa_sc[variant], ka_sc[pl.ds(start, tk), :])

    def diag_bias(j, s_ref):
        @pl.when(j == i)
        def _():
            bias = slope * dist_ref[...]
            s_ref[:tq] -= bias
            s_ref[tq:] -= bias

    def probs(j, p_ref):
        start = pl.multiple_of(j * tk, tk)
        s = _dot(qa_sc[jnp.where(j < i, _LEFT, _RIGHT)], kt_sc[:, pl.ds(start, tk)])
        p_ref[...] = jnp.exp2(s).astype(BF16)

    def values(j):
        return va_sc[pl.ds(pl.multiple_of(j * tk, tk), tk), :]

    build_queries(True)
    lo = jnp.maximum(i - reach, 0)
    count = jnp.minimum(i + reach + 1, nk) - lo - 1

    def off_diag(n):
        j = lo + jnp.minimum(n, count - 1)
        return j + (j >= i).astype(jnp.int32)

    s_diag = _dot(qa_sc[_DIAG], kt_sc[:, pl.ds(pl.multiple_of(i * tk, tk), tk)])
    probs(off_diag(0), pa_sc)
    bias = slope * dist_ref[...]
    p_diag = jnp.exp2(s_diag - jnp.concatenate([bias, bias], axis=0)).astype(BF16)
    acc_sc[...] = _dot(p_diag, values(i))

    def pair(tt, carry):
        n = 2 * tt
        probs(off_diag(n + 1), pb_sc)
        acc_sc[...] += _dot(pa_sc[...], values(off_diag(n)))
        probs(off_diag(n + 2), pa_sc)
        acc_sc[...] += _dot(pb_sc[...], values(off_diag(n + 1)))
        return carry

    lax.fori_loop(0, count // 2, pair, 0)

    @pl.when(count % 2 == 1)
    def _():
        acc_sc[...] += _dot(pa_sc[...], values(off_diag(count - 1)))

    denom_min = jnp.min(acc_sc[:, B_VDIM:B_VDIM + 1])

    @pl.when(jnp.logical_not(denom_min >= _MIN_DENOM))
    def _():
        build_queries(False)
        m_sc[...] = jnp.full_like(m_sc, NEG_INF)
        acc_sc[...] = jnp.zeros_like(acc_sc)

        def chunk(j, carry):
            scores(j, sa_sc)
            diag_bias(j, sa_sc)
            adj = jnp.where(j > i, off, jnp.where(j < i, -off, 0.0))
            start = pl.multiple_of(j * tk, tk)
            s = sa_sc[...]
            m_prev = m_sc[...]
            m_new = jnp.maximum(m_prev, jnp.max(s, axis=-1, keepdims=True) + adj)
            p = jnp.exp2(s - (m_new - adj)).astype(BF16)
            acc_sc[...] = jnp.exp2(m_prev - m_new) * acc_sc[...] + _dot(p, va_sc[pl.ds(start, tk), :])
            m_sc[...] = m_new
            return carry

        lax.fori_loop(0, nk, chunk, 0)

    lp = lam_ref[...]
    lam = (jnp.exp(jnp.sum(lp[0:1] * lp[1:2], axis=-1, keepdims=True))
           - jnp.exp(jnp.sum(lp[2:3] * lp[3:4], axis=-1, keepdims=True)) + lambda_init)
    acc = acc_sc[...]
    o0 = acc[:tq, :B_VDIM] * (1.0 / acc[:tq, B_VDIM:B_VDIM + 1])
    o1 = acc[tq:, :B_VDIM] * (1.0 / acc[tq:, B_VDIM:B_VDIM + 1])
    o_ref[rows, :] = (_rms(o0 - lam * o1, g_ref[...]) * out_scale).astype(o_ref.dtype)


def _bf16_pieces(x, n=3):
    out = []
    r = np.float64(x)
    for _ in range(n):
        p = np.float64(np.float32(r).astype(jnp.bfloat16).astype(np.float32))
        out.append(p)
        r = r - p
    return out


def _alibi_features(slopes2, t):
    pos = np.arange(t)
    hi = (pos // _POS_SPLIT) * _POS_SPLIT
    lo = pos % _POS_SPLIT
    qf = np.zeros((len(slopes2), t, LANES), np.float32)
    kf = np.zeros((len(slopes2), t, LANES), np.float32)
    for h, s in enumerate(slopes2):
        for n, piece in enumerate(_bf16_pieces(s)):
            for base, part in ((0, hi), (3, lo)):
                qf[h, :, base + n] = -piece
                kf[h, :, base + n] = part
                qf[h, :, 6 + base + n] = part
                kf[h, :, 6 + base + n] = piece
        for n, piece in enumerate(_bf16_pieces(s * t)):
            qf[h, :, _CHUNK_LANE + n] = -piece
        kf[h, :, _BOUND_LANE:_BOUND_LANE + 3] = -1.0
        kf[h, :, _OFFSET_LANE:_OFFSET_LANE + 3] = -1.0
    return jnp.asarray(qf, BF16), jnp.asarray(kf, BF16)


def _diff_attention(proj, b_lambda, subln_g, lambda_init, batch, seq, t=512):
    t = min(t, seq // 2)
    nq = seq // t
    assert seq % (2 * t) == 0
    kcol = MIX_WIDTH // B_VDIM
    vcol = 2 * kcol
    slopes2 = [s * LOG2E for s in _alibi_slopes(B_HEADS)]
    reach = [min(nq, int(math.floor((_ZERO_EXP2_ARG / s - 1.0) / t)) + 1) for s in slopes2]
    qf, kf = _alibi_features(slopes2, t)
    pos = np.arange(t)
    dist = jnp.asarray(np.abs(pos[:, None] - pos[None, :]), F32)
    kernel = functools.partial(_diff_attn_kernel, tk=t, out_scale=1.0 - lambda_init,
                               lambda_init=lambda_init)
    return pl.pallas_call(
        kernel,
        grid_spec=pltpu.PrefetchScalarGridSpec(
            num_scalar_prefetch=2,
            grid=(batch, B_HEADS),
            in_specs=[pl.BlockSpec((seq, B_VDIM), lambda b, h, r, s: (b, h)),
                      pl.BlockSpec((seq, B_VDIM), lambda b, h, r, s: (b, kcol + h)),
                      pl.BlockSpec((seq, B_VDIM), lambda b, h, r, s: (b, vcol + h)),
                      pl.BlockSpec((None, t, LANES), lambda b, h, r, s: (h, 0, 0)),
                      pl.BlockSpec((None, t, LANES), lambda b, h, r, s: (h, 0, 0)),
                      pl.BlockSpec((t, t), lambda b, h, r, s: (0, 0)),
                      pl.BlockSpec((4, HEAD_DIM), lambda b, h, r, s: (0, 0)),
                      pl.BlockSpec((1, B_VDIM), lambda b, h, r, s: (0, 0))],
            out_specs=pl.BlockSpec((seq, B_VDIM), lambda b, h, r, s: (b, h)),
            scratch_shapes=[pltpu.VMEM((3, 2 * t, 2 * LANES), BF16),
                            pltpu.VMEM((seq, 2 * LANES), BF16),
                            pltpu.VMEM((seq, 2 * LANES), BF16),
                            pltpu.VMEM((2, 8, LANES), F32),
                            pltpu.VMEM((2 * t, 1), F32),
                            pltpu.VMEM((2 * t, 2 * LANES), F32),
                            pltpu.VMEM((2 * t, t), F32),
                            pltpu.VMEM((2 * t, t), BF16),
                            pltpu.VMEM((2 * t, t), BF16),
                            pltpu.VMEM((2 * LANES, seq), BF16)],
        ),
        out_shape=jax.ShapeDtypeStruct((batch * seq, MIX_WIDTH), BF16),
        compiler_params=_params("parallel", "parallel"),
        name="diff_attention",
    )(jnp.asarray(reach, jnp.int32), jnp.asarray(slopes2, F32), proj, proj, proj, qf, kf, dist,
      b_lambda, subln_g.reshape(1, B_VDIM))


def _router_kernel(x_ref, g_ref, whi_ref, wlo_ref, h_ref, e_ref, gate_ref):
    h = _rms(x_ref[...], g_ref[...])
    h_hi = h.astype(BF16)
    h_lo = (h - h_hi.astype(F32)).astype(BF16)
    whi = whi_ref[...]
    logits = _dot(h_hi, whi) + (_dot(h_hi, wlo_ref[...]) + _dot(h_lo, whi))
    lane = lax.broadcasted_iota(jnp.int32, logits.shape, 1)
    logits = jnp.where(lane < N_EXPERTS, logits, -jnp.inf)
    m1 = jnp.max(logits, axis=-1, keepdims=True)
    i1 = jnp.min(jnp.where(logits == m1, lane, LANES), axis=-1, keepdims=True)
    rest = jnp.where(lane == i1, -jnp.inf, logits)
    m2 = jnp.max(rest, axis=-1, keepdims=True)
    i2 = jnp.min(jnp.where(rest == m2, lane, LANES), axis=-1, keepdims=True)
    r = jnp.exp(m2 - m1)
    g1 = 1.0 / (1.0 + r)
    g2 = r * g1
    e_ref[...] = jnp.where(lane == 0, i1, jnp.where(lane == 1, i2, 0))
    gate_ref[...] = jnp.where(lane == 0, g1, jnp.where(lane == 1, g2, 0.0))
    h_ref[...] = h_hi


def _router(x, g, w_router, tm=2048):
    t, d = x.shape
    tm = min(tm, t)
    wpad = jnp.zeros((d, LANES), F32).at[:, :N_EXPERTS].set(w_router)
    whi = wpad.astype(BF16)
    wlo = (wpad - whi.astype(F32)).astype(BF16)
    return pl.pallas_call(
        _router_kernel,
        grid=(t // tm,),
        in_specs=[pl.BlockSpec((tm, d), lambda i: (i, 0)),
                  pl.BlockSpec((1, d), lambda i: (0, 0)),
                  pl.BlockSpec((d, LANES), lambda i: (0, 0)),
                  pl.BlockSpec((d, LANES), lambda i: (0, 0))],
        out_specs=[pl.BlockSpec((tm, d), lambda i: (i, 0)),
                   pl.BlockSpec((tm, LANES), lambda i: (i, 0)),
                   pl.BlockSpec((tm, LANES), lambda i: (i, 0))],
        out_shape=[jax.ShapeDtypeStruct((t, d), BF16),
                   jax.ShapeDtypeStruct((t, LANES), jnp.int32),
                   jax.ShapeDtypeStruct((t, LANES), F32)],
        compiler_params=_params("parallel"),
        name="router",
    )(x, g.reshape(1, d), whi, wlo)


def _slab_sizes(rows, tc):
    quarter, half = tc // 4, tc // 2
    return ((quarter, rows <= quarter),
            (half, jnp.logical_and(rows > quarter, rows <= half)),
            (tc, rows > half))


def _slab_stride(slab_ref, n, step, tc):
    most = slab_ref[(n + step) * N_EXPERTS]
    for e in range(1, N_EXPERTS):
        most = jnp.maximum(most, slab_ref[(n + step) * N_EXPERTS + e])
    return jnp.where(most <= tc // 2, tc // 2, tc)


def _dispatch_kernel(slab_ref, fill_ref, h_ref, dest_ref, gate_ref, xs_hbm, gs_hbm,
                     xslab_sc, gslab_sc, zx_sc, zg_sc, sem, zsem):
    i = pl.program_id(0)
    n = pl.num_programs(0)
    slot = i % 2
    tc = h_ref.shape[0]
    tm = zx_sc.shape[0]

    def slab_dma(step, sl, op):
        stride = _slab_stride(slab_ref, n, step, tc)
        for e in range(N_EXPERTS):
            start = pl.multiple_of(slab_ref[step * N_EXPERTS + e], SUBLANES)
            first = pl.multiple_of(e * stride, SUBLANES)
            for size, cond in _slab_sizes(slab_ref[(n + step) * N_EXPERTS + e], tc):
                @pl.when(cond)
                def _():
                    for src, dst in ((xslab_sc, xs_hbm), (gslab_sc, gs_hbm)):
                        copy = pltpu.make_async_copy(src.at[sl, pl.ds(first, size)],
                                                     dst.at[pl.ds(start, size)], sem.at[sl])
                        getattr(copy, op)()

    def fill_copies(start):
        start = pl.multiple_of(start, SUBLANES)
        return [pltpu.make_async_copy(zx_sc, xs_hbm.at[pl.ds(start, tm)], zsem),
                pltpu.make_async_copy(zg_sc, gs_hbm.at[pl.ds(start, tm)], zsem)]

    @pl.when(i == 0)
    def _():
        zx_sc[...] = jnp.zeros_like(zx_sc)
        zg_sc[...] = jnp.zeros_like(zg_sc)
        for half in range(2):
            for e in range(N_EXPERTS):
                for c in fill_copies(fill_ref[half * N_EXPERTS + e]):
                    c.start()
            for e in range(N_EXPERTS):
                for c in fill_copies(fill_ref[half * N_EXPERTS + e]):
                    c.wait()
        first_free = fill_ref[2 * N_EXPERTS]
        n_tiles = xs_hbm.shape[0] // tm

        def start_tile(b, carry):
            for c in fill_copies(b * tm):
                c.start()
            return carry

        def wait_tile(b, carry):
            for c in fill_copies(b * tm):
                c.wait()
            return carry

        lax.fori_loop(first_free, n_tiles, start_tile, 0)
        lax.fori_loop(first_free, n_tiles, wait_tile, 0)

    dest = dest_ref[0]
    gate = gate_ref[...]
    lane = lax.broadcasted_iota(jnp.int32, gate.shape, 1)

    def gate_pieces(col):
        out = jnp.zeros(gate.shape, F32)
        for n_, piece in enumerate(_bf16_split(col)):
            out = jnp.where(lane == n_, piece.astype(F32), out)
        return out.astype(BF16)

    def place_rows(stride):
        pos = lax.broadcasted_iota(jnp.int32, (stride, tc), 0)
        sel0, sel1 = [], []
        for e in range(N_EXPERTS):
            start = slab_ref[i * N_EXPERTS + e]
            sel0.append(jnp.where(dest[0:1, :] - start == pos, 1.0, 0.0))
            sel1.append(jnp.where(dest[1:2, :] - start == pos, 1.0, 0.0))
        sel0 = jnp.concatenate(sel0, axis=0).astype(BF16)
        sel1 = jnp.concatenate(sel1, axis=0).astype(BF16)
        rows = pl.ds(0, N_EXPERTS * stride)
        xslab_sc[slot, rows] = _dot(sel0 + sel1, h_ref[...])
        gslab_sc[slot, rows] = (_dot(sel0, gate_pieces(gate[:, 0:1]))
                                + _dot(sel1, gate_pieces(gate[:, 1:2])))

    compact = _slab_stride(slab_ref, n, i, tc) < tc
    pl.when(compact)(lambda: place_rows(tc // 2))
    pl.when(jnp.logical_not(compact))(lambda: place_rows(tc))

    @pl.when(i > 0)
    def _():
        slab_dma(i - 1, 1 - slot, "wait")

    slab_dma(i, slot, "start")

    @pl.when(i == n - 1)
    def _():
        slab_dma(i, slot, "wait")


def _dispatch(h, dest_t, gates, slab, fill, p_rows, tm, tc):
    t, d = h.shape
    nb = t // tc
    return pl.pallas_call(
        _dispatch_kernel,
        grid_spec=pltpu.PrefetchScalarGridSpec(
            num_scalar_prefetch=2,
            grid=(nb,),
            in_specs=[pl.BlockSpec((tc, d), lambda i, s, z: (i, 0)),
                      pl.BlockSpec((1, TOP_K, tc), lambda i, s, z: (i, 0, 0)),
                      pl.BlockSpec((tc, LANES), lambda i, s, z: (i, 0))],
            out_specs=[pl.BlockSpec(memory_space=pl.ANY), pl.BlockSpec(memory_space=pl.ANY)],
            scratch_shapes=[pltpu.VMEM((2, N_EXPERTS * tc, d), F32),
                            pltpu.VMEM((2, N_EXPERTS * tc, LANES), F32),
                            pltpu.VMEM((tm, d), F32), pltpu.VMEM((tm, LANES), F32),
                            pltpu.SemaphoreType.DMA((2,)), pltpu.SemaphoreType.DMA(())],
        ),
        out_shape=[jax.ShapeDtypeStruct((p_rows, d), F32), jax.ShapeDtypeStruct((p_rows, LANES), F32)],
        compiler_params=_params("arbitrary"),
        name="moe_dispatch",
    )(slab, fill, h, dest_t, gates)


def _moe_kernel(blk_e_ref, used_ref, xblk_ref, x_ref, gs_ref, wg_ref, wu_ref, wd_ref, o_ref,
                xb_sc, acc_sc):
    i = pl.program_id(0)
    f = pl.program_id(1)
    used = used_ref[i] > 0

    @pl.when(jnp.logical_and(used, f == 0))
    def _():
        xb_sc[...] = x_ref[...].astype(BF16)
        acc_sc[...] = jnp.zeros_like(acc_sc)

    @pl.when(used)
    def _():
        acc_sc[...] += _swiglu_tile(xb_sc[...], wg_ref, wu_ref, wd_ref)

    @pl.when(f == pl.num_programs(1) - 1)
    def _():
        gs = gs_ref[...]
        gate = gs[:, 0:1] + gs[:, 1:2] + gs[:, 2:3]
        o_ref[...] = jnp.where(used, acc_sc[...] * gate, 0.0)


def _moe_experts(xs, gs, blk_e, blk_used, xblk, wg, wu, wd, tm, tf=1792):
    p, d = xs.shape
    nblk = p // tm
    nf = wg.shape[2] // tf

    def fidx(i, f, used_ref):
        return jnp.where(used_ref[i] > 0, f, nf - 1)

    return pl.pallas_call(
        _moe_kernel,
        grid_spec=pltpu.PrefetchScalarGridSpec(
            num_scalar_prefetch=3,
            grid=(nblk, nf),
            in_specs=[pl.BlockSpec((tm, d), lambda i, f, be, us, xb: (xb[i], 0)),
                      pl.BlockSpec((tm, LANES), lambda i, f, be, us, xb: (xb[i], 0)),
                      pl.BlockSpec((None, d, tf), lambda i, f, be, us, xb: (be[i], 0, fidx(i, f, us))),
                      pl.BlockSpec((None, d, tf), lambda i, f, be, us, xb: (be[i], 0, fidx(i, f, us))),
                      pl.BlockSpec((None, tf, d), lambda i, f, be, us, xb: (be[i], fidx(i, f, us), 0))],
            out_specs=pl.BlockSpec((tm, d), lambda i, f, be, us, xb: (i, 0)),
            scratch_shapes=[pltpu.VMEM((tm, d), BF16), pltpu.VMEM((tm, d), F32)],
        ),
        out_shape=jax.ShapeDtypeStruct((p, d), F32),
        compiler_params=_params("arbitrary", "arbitrary"),
        name="moe_experts",
    )(blk_e, blk_used, xblk, xs, gs, wg, wu, wd)


def _combine_kernel(slab_ref, ys_hbm, dest_ref, x_ref, g_ref, o_ref, slab_sc, sem):
    i = pl.program_id(0)
    n = pl.num_programs(0)
    slot = i % 2
    tc = x_ref.shape[0]

    def slab_dma(step, sl, op):
        stride = _slab_stride(slab_ref, n, step, tc)
        for e in range(N_EXPERTS):
            start = pl.multiple_of(slab_ref[step * N_EXPERTS + e], SUBLANES)
            first = pl.multiple_of(e * stride, SUBLANES)
            for size, cond in _slab_sizes(slab_ref[(n + step) * N_EXPERTS + e], tc):
                @pl.when(cond)
                def _():
                    copy = pltpu.make_async_copy(ys_hbm.at[pl.ds(start, size)],
                                                 slab_sc.at[sl, pl.ds(first, size)], sem.at[sl])
                    getattr(copy, op)()

    @pl.when(i == 0)
    def _():
        slab_sc[...] = jnp.zeros_like(slab_sc)
        slab_dma(0, 0, "start")

    @pl.when(i + 1 < n)
    def _():
        slab_dma(i + 1, 1 - slot, "start")

    dest = dest_ref[...]
    slab_dma(i, slot, "wait")

    def gather_rows(stride):
        pos = lax.broadcasted_iota(jnp.int32, (tc, stride), 1)
        sel = []
        for e in range(N_EXPERTS):
            start = slab_ref[i * N_EXPERTS + e]
            hit = jnp.logical_or(dest[:, 0:1] - start == pos, dest[:, 1:2] - start == pos)
            sel.append(jnp.where(hit, 1.0, 0.0))
        sel = jnp.concatenate(sel, axis=1).astype(BF16)
        y = _dot(sel, slab_sc[slot, pl.ds(0, N_EXPERTS * stride)].astype(BF16))
        o_ref[...] = x_ref[...] + _rms(y, g_ref[...])

    compact = _slab_stride(slab_ref, n, i, tc) < tc
    pl.when(compact)(lambda: gather_rows(tc // 2))
    pl.when(jnp.logical_not(compact))(lambda: gather_rows(tc))


def _combine(ys, dest, slab, x, g, tc):
    t, d = x.shape
    return pl.pallas_call(
        _combine_kernel,
        grid_spec=pltpu.PrefetchScalarGridSpec(
            num_scalar_prefetch=1,
            grid=(t // tc,),
            in_specs=[pl.BlockSpec(memory_space=pl.ANY),
                      pl.BlockSpec((tc, LANES), lambda i, s: (i, 0)),
                      pl.BlockSpec((tc, d), lambda i, s: (i, 0)),
                      pl.BlockSpec((1, d), lambda i, s: (0, 0))],
            out_specs=pl.BlockSpec((tc, d), lambda i, s: (i, 0)),
            scratch_shapes=[pltpu.VMEM((2, N_EXPERTS * tc, d), F32), pltpu.SemaphoreType.DMA((2,))],
        ),
        out_shape=jax.ShapeDtypeStruct((t, d), F32),
        compiler_params=_params("arbitrary"),
        name="moe_combine",
    )(slab, ys, dest, x, g.reshape(1, d))


def _route(top_e, tm, tc):
    t = top_e.shape[0]
    a = t * TOP_K
    nb = t // tc
    e_flat = top_e.reshape(a)
    onehot = (e_flat[:, None] == jnp.arange(N_EXPERTS, dtype=jnp.int32)[None, :]).astype(jnp.int32)
    csum = jnp.cumsum(onehot, axis=0)
    rank = jnp.sum(onehot * (csum - 1), axis=1)
    upto = csum[TOP_K * tc - 1::TOP_K * tc]
    before = jnp.concatenate([jnp.zeros((1, N_EXPERTS), jnp.int32), upto[:nb - 1]], axis=0)
    rows = (upto - before + SUBLANES - 1) // SUBLANES * SUBLANES
    rows_before = jnp.cumsum(rows, axis=0) - rows
    total = jnp.sum(rows, axis=0)
    region = (total + 2 * tm - 1) // tm * tm
    rends = jnp.cumsum(region)
    rstarts = rends - region
    slab = (rstarts[None, :] + rows_before).astype(jnp.int32)
    shift = jnp.broadcast_to((slab - before)[:, None, :], (nb, TOP_K * tc, N_EXPERTS)).reshape(a, N_EXPERTS)
    dest = (rank + jnp.sum(onehot * shift, axis=1)).astype(jnp.int32).reshape(t, TOP_K)
    slab = jnp.concatenate([slab.reshape(nb * N_EXPERTS), rows.reshape(nb * N_EXPERTS)])
    row_end = rstarts + total
    fill = jnp.concatenate([row_end, rends - tm, rends[-1:] // tm]).astype(jnp.int32)
    nblk = -(-(a + nb * N_EXPERTS * (SUBLANES - 1)) // tm) + 2 * N_EXPERTS
    blk_start = jnp.arange(nblk, dtype=jnp.int32) * tm
    blk_e = jnp.minimum(jnp.sum((rends[None, :] <= blk_start[:, None]).astype(jnp.int32), axis=1),
                        N_EXPERTS - 1)
    blk_used = jnp.logical_and(blk_start < row_end[blk_e], blk_start < rends[-1]).astype(jnp.int32)
    first_used = jnp.argmax(blk_used).astype(jnp.int32)
    xblk = jnp.where(blk_used > 0, jnp.arange(nblk, dtype=jnp.int32), first_used)
    return dest, slab, fill, blk_e, blk_used, xblk, nblk * tm


def _moe(x, h, e_pad, gates, g_out, wg, wu, wd, tm=512, tc=256):
    t = x.shape[0]
    tc = min(tc, t)
    dest, slab, fill, blk_e, blk_used, xblk, p_rows = _route(e_pad[:, :TOP_K], tm, tc)
    dest_t = dest.reshape(t // tc, tc, TOP_K).transpose(0, 2, 1)
    dest_pad = jnp.zeros((t, LANES), jnp.int32).at[:, :TOP_K].set(dest)
    xs, gs = _dispatch(h, dest_t, gates, slab, fill, p_rows, tm, tc)
    ys = _moe_experts(xs, gs, blk_e, blk_used, xblk, wg, wu, wd, tm)
    return _combine(ys, dest_pad, slab, x, g_out, tc)


def kernel(x, mem, norm_g, mem_norm_g, w_mem_kv, a_w_in, a_sink, a_w_out, b_w_in, b_lambda,
           b_subln_g, b_w_out, ffn_w_gate, ffn_w_up, ffn_w_down, moe_w_router, moe_w_gate,
           moe_w_up, moe_w_down):
    batch, seq, d = x.shape
    n_mem = mem.shape[1]
    bf = lambda w: w.astype(BF16)
    xt = x.reshape(batch * seq, d)
    mkv = _norm_matmul(mem.reshape(batch * n_mem, d), mem_norm_g, bf(w_mem_kv),
                       jnp.ones((w_mem_kv.shape[1],), F32))

    g = norm_g[0]
    proj = _norm_matmul(xt, g[0], bf(a_w_in[0]), _query_col_scale(a_w_in.shape[2], A_Q_HEADS * HEAD_DIM))
    mix = _window_attention(proj, a_sink[0], batch, seq)
    mem_out = _mem_attention(proj, (A_Q_HEADS + 2 * A_KV_HEADS) * HEAD_DIM // MEM_WIDTH, mkv, batch, seq)
    xt = _out_proj(mix, mem_out, bf(a_w_out[0]), xt, g[1])
    xt = _ffn(xt, g[2], g[3], bf(ffn_w_gate[0]), bf(ffn_w_up[0]), bf(ffn_w_down[0]))

    g = norm_g[1]
    lambda_init = 0.8 - 0.6 * math.exp(-0.3 * 1)
    proj = _norm_matmul(xt, g[0], bf(b_w_in[0]), _query_col_scale(b_w_in.shape[2], MIX_WIDTH))
    mix = _diff_attention(proj, b_lambda[0], b_subln_g[0], lambda_init, batch, seq)
    mem_out = _mem_attention(proj, 3 * MIX_WIDTH // MEM_WIDTH, mkv, batch, seq)
    xt = _out_proj(mix, mem_out, bf(b_w_out[0]), xt, g[1])
    h, e_pad, gates = _router(xt, g[2], moe_w_router[0])
    xt = _moe(xt, h, e_pad, gates, g[3], bf(moe_w_gate[0]), bf(moe_w_up[0]), bf(moe_w_down[0]))
    return xt.reshape(batch, seq, d)
```

```python
import functools
import math

import jax
import jax.numpy as jnp
import numpy as np
from jax import lax
from jax.experimental import pallas as pl
from jax.experimental.pallas import tpu as pltpu

D_MODEL = 1024
HEAD_DIM = 64
MIX_WIDTH = 768
MEM_WIDTH = 256
MEM_HEADS = 4
A_Q_HEADS = 12
A_KV_HEADS = 4
A_GROUP = 3
WINDOW = 128
B_HEADS = 6
B_VDIM = 128
D_FF = 3584
N_EXPERTS = 8
TOP_K = 2
EPS = 1e-6
NEG_INF = -1e30
LOG2E = math.log2(math.e)
QK_SCALE = HEAD_DIM ** -0.5 * LOG2E
LANES = 128
SUBLANES = 8
VMEM_LIMIT_BYTES = 56 * 1024 * 1024

BF16 = jnp.bfloat16
F32 = jnp.float32


def _params(*sem):
    return pltpu.CompilerParams(dimension_semantics=sem, vmem_limit_bytes=VMEM_LIMIT_BYTES)


def _rms(x, g):
    return x * lax.rsqrt(jnp.mean(x * x, axis=-1, keepdims=True) + EPS) * g


def _dot(a, b):
    return jnp.dot(a, b, preferred_element_type=F32)


def _dot_nt(a, b):
    return lax.dot_general(a, b, (((1,), (1,)), ((), ())), preferred_element_type=F32)


def _alibi_slopes(n):
    return [2.0 ** (-8.0 * (i + 1) / n) for i in range(n)]


def _norm_matmul_kernel(x0_ref, xn_ref, g_ref, w_ref, cs_ref, o_ref, h_sc, *, chunk):
    i = pl.program_id(0)

    @pl.when(i == 0)
    def _():
        h_sc[0] = _rms(x0_ref[...], g_ref[...]).astype(BF16)

    h = h_sc[i % 2]
    h_sc[(i + 1) % 2] = _rms(xn_ref[...], g_ref[...]).astype(BF16)
    for c in range(o_ref.shape[1] // chunk):
        sl = slice(c * chunk, (c + 1) * chunk)
        o_ref[:, sl] = (_dot(h, w_ref[:, sl]) * cs_ref[:, sl]).astype(o_ref.dtype)


def _norm_matmul(x, g, w, col_scale, tm=1024, chunk=512):
    t, d = x.shape
    n = w.shape[1]
    tm = min(tm, t)
    steps = t // tm
    return pl.pallas_call(
        functools.partial(_norm_matmul_kernel, chunk=min(chunk, n)),
        grid=(steps,),
        in_specs=[pl.BlockSpec((tm, d), lambda i: (0, 0)),
                  pl.BlockSpec((tm, d), lambda i: (jnp.minimum(i + 1, steps - 1), 0)),
                  pl.BlockSpec((1, d), lambda i: (0, 0)),
                  pl.BlockSpec((d, n), lambda i: (0, 0)),
                  pl.BlockSpec((1, n), lambda i: (0, 0))],
        out_specs=pl.BlockSpec((tm, n), lambda i: (i, 0)),
        out_shape=jax.ShapeDtypeStruct((t, n), BF16),
        scratch_shapes=[pltpu.VMEM((2, tm, d), BF16)],
        compiler_params=_params("arbitrary"),
        name="norm_matmul",
    )(x, x, g.reshape(1, d), w, col_scale.reshape(1, n))


def _query_col_scale(n, mix_q_cols):
    cols = np.arange(n)
    return jnp.asarray(np.where((cols < mix_q_cols) | (cols >= n - MEM_WIDTH), QK_SCALE, 1.0), F32)


def _window_kernel(q_ref, kp_ref, kc_ref, kn_ref, vp_ref, vc_ref, vn_ref, bias_ref, o_ref):
    n = pl.program_id(1)
    nb = pl.num_programs(1)
    blk = kp_ref.shape[0]
    sub = q_ref.shape[0] // blk
    zeros = jnp.zeros((blk, kp_ref.shape[1]), kp_ref.dtype)
    k_all = jnp.concatenate([kp_ref[...], kc_ref[...], kn_ref[...]], axis=0)
    v_all = jnp.concatenate([vp_ref[...], vc_ref[...], vn_ref[...]], axis=0)
    col = lax.broadcasted_iota(jnp.int32, (1, 4 * blk), 1)
    half = lax.broadcasted_iota(jnp.int32, (4 * blk, LANES), 1) < HEAD_DIM
    low_half = jnp.where(half, 1.0, 0.0).astype(BF16)
    high_half = jnp.where(half, 0.0, 1.0).astype(BF16)
    for u in range(sub):
        k = jnp.concatenate([k_all[u * blk:(u + 3) * blk], zeros], axis=0)
        v = jnp.concatenate([v_all[u * blk:(u + 3) * blk], zeros], axis=0)
        edge = jnp.zeros((1, 4 * blk), F32)
        if u == 0:
            edge = jnp.where(jnp.logical_and(n == 0, col < blk), NEG_INF, edge)
        if u == sub - 1:
            edge = jnp.where(jnp.logical_and(n == nb - 1,
                                             jnp.logical_and(col >= 2 * blk, col < 3 * blk)), NEG_INF, edge)
        q = q_ref[u * blk:(u + 1) * blk, :]
        outs = []
        for kh in range(A_KV_HEADS):
            heads = range(kh * A_GROUP, (kh + 1) * A_GROUP)
            kv = slice(kh * HEAD_DIM, (kh + 1) * HEAD_DIM)
            qs = jnp.concatenate([q[:, h * HEAD_DIM:(h + 1) * HEAD_DIM] for h in heads], axis=0)
            s = _dot_nt(qs, k[:, kv]) + bias_ref[kh] + edge
            p = jnp.exp2(s - jnp.max(s, axis=-1, keepdims=True)).astype(BF16)
            low = kh % 2 == 0
            slab = v[:, (kh // 2) * LANES:(kh // 2 + 1) * LANES]
            own, other = (low_half, high_half) if low else (high_half, low_half)
            ov = _dot(p, jnp.concatenate([slab * own + other, own], axis=1))
            den = ov[:, LANES:]
            o = ov[:, :LANES] * (1.0 / jnp.where(den == 0.0, 1.0, den))
            o = o[:, :HEAD_DIM] if low else o[:, HEAD_DIM:]
            outs.extend(o[g * blk:(g + 1) * blk] for g in range(A_GROUP))
        o_ref[u * blk:(u + 1) * blk, :] = jnp.concatenate(outs, axis=-1).astype(o_ref.dtype)


def _window_bias(blk, sink):
    qi = np.arange(blk)[:, None]
    kj = np.arange(3 * blk)[None, :]
    dist = np.abs(blk + qi - kj)
    slopes = _alibi_slopes(A_Q_HEADS)
    tables = np.stack([np.where(dist <= WINDOW, -(slopes[h] * LOG2E) * dist, NEG_INF)
                       for h in range(A_Q_HEADS)])
    pad = jnp.full((A_Q_HEADS, blk, blk), NEG_INF, F32)
    pad = pad.at[:, :, 0].set(jnp.broadcast_to((sink.astype(F32) * LOG2E)[:, None], (A_Q_HEADS, blk)))
    full = jnp.concatenate([jnp.asarray(tables, F32), pad], axis=2)
    return full.reshape(A_KV_HEADS, A_GROUP * blk, 4 * blk)


def _window_attention(proj, sink, batch, seq, sub=4):
    blk = WINDOW
    nb = seq // blk
    sub = min(sub, nb)
    ns = nb // sub
    kcol = MIX_WIDTH // 256
    vcol = kcol + 1
    bias = _window_bias(blk, sink)

    def cur(col):
        return pl.BlockSpec((sub * blk, 256), lambda b, n: (b * ns + n, col))

    def prev(col):
        return pl.BlockSpec((blk, 256), lambda b, n: (b * nb + jnp.maximum(n * sub - 1, 0), col))

    def nxt(col):
        return pl.BlockSpec((blk, 256), lambda b, n: (b * nb + jnp.minimum((n + 1) * sub, nb - 1), col))

    return pl.pallas_call(
        _window_kernel,
        grid=(batch, ns),
        in_specs=[pl.BlockSpec((sub * blk, MIX_WIDTH), lambda b, n: (b * ns + n, 0)),
                  prev(kcol), cur(kcol), nxt(kcol), prev(vcol), cur(vcol), nxt(vcol),
                  pl.BlockSpec(bias.shape, lambda b, n: (0, 0, 0))],
        out_specs=pl.BlockSpec((sub * blk, MIX_WIDTH), lambda b, n: (b * ns + n, 0)),
        out_shape=jax.ShapeDtypeStruct((batch * seq, MIX_WIDTH), BF16),
        compiler_params=_params("parallel", "parallel"),
        name="window_attention",
    )(proj, proj, proj, proj, proj, proj, proj, bias)


def _mem_attn_kernel(q_ref, mkv_ref, o_ref):
    q = q_ref[...]
    mkv = mkv_ref[...]
    ones = jnp.ones((mkv.shape[0], LANES), BF16)
    outs = []
    for h in range(MEM_HEADS):
        sl = slice(h * HEAD_DIM, (h + 1) * HEAD_DIM)
        s = _dot_nt(q[:, sl], mkv[:, sl])
        p = jnp.exp2(s - jnp.max(s, axis=-1, keepdims=True)).astype(BF16)
        o = _dot(p, mkv[:, MEM_WIDTH + h * HEAD_DIM:MEM_WIDTH + (h + 1) * HEAD_DIM])
        outs.append(o * (1.0 / _dot(p, ones)[:, :HEAD_DIM]))
    o_ref[...] = jnp.concatenate(outs, axis=-1).astype(o_ref.dtype)


def _mem_attention(proj, qcol, mkv, batch, seq, tm=2048):
    tm = min(tm, seq)
    per_batch = seq // tm
    n_mem = mkv.shape[0] // batch
    return pl.pallas_call(
        _mem_attn_kernel,
        grid=(batch * per_batch,),
        in_specs=[pl.BlockSpec((tm, MEM_WIDTH), lambda i: (i, qcol)),
                  pl.BlockSpec((n_mem, 2 * MEM_WIDTH), lambda i: (i // per_batch, 0))],
        out_specs=pl.BlockSpec((tm, MEM_WIDTH), lambda i: (i, 0)),
        out_shape=jax.ShapeDtypeStruct((batch * seq, MEM_WIDTH), BF16),
        compiler_params=_params("parallel"),
        name="mem_attention",
    )(proj, mkv)


def _out_proj_kernel(mix_ref, mem_ref, w1_ref, w2_ref, x_ref, g_ref, o_ref):
    o = _dot(mix_ref[...], w1_ref[...]) + _dot(mem_ref[...], w2_ref[...])
    o_ref[...] = x_ref[...] + _rms(o, g_ref[...])


def _out_proj(mix, mem_out, w_out, x, g, tm=1024):
    t, d = x.shape
    tm = min(tm, t)
    w1 = w_out[:MIX_WIDTH]
    w2 = w_out[MIX_WIDTH:]
    return pl.pallas_call(
        _out_proj_kernel,
        grid=(t // tm,),
        in_specs=[pl.BlockSpec((tm, MIX_WIDTH), lambda i: (i, 0)),
                  pl.BlockSpec((tm, MEM_WIDTH), lambda i: (i, 0)),
                  pl.BlockSpec((MIX_WIDTH, d), lambda i: (0, 0)),
                  pl.BlockSpec((MEM_WIDTH, d), lambda i: (0, 0)),
                  pl.BlockSpec((tm, d), lambda i: (i, 0)),
                  pl.BlockSpec((1, d), lambda i: (0, 0))],
        out_specs=pl.BlockSpec((tm, d), lambda i: (i, 0)),
        out_shape=jax.ShapeDtypeStruct((t, d), F32),
        compiler_params=_params("parallel"),
        name="out_proj",
    )(mix, mem_out, w1, w2, x, g.reshape(1, d))


def _swiglu_tile(h, wg_ref, wu_ref, wd_ref, chunk=256):
    out = None
    for c in range(wg_ref.shape[1] // chunk):
        sl = slice(c * chunk, (c + 1) * chunk)
        a = _dot(h, wg_ref[:, sl])
        u = _dot(h, wu_ref[:, sl])
        z = (a * jax.nn.sigmoid(a) * u).astype(BF16)
        y = _dot(z, wd_ref[sl, :])
        out = y if out is None else out + y
    return out


def _ffn_kernel(x_ref, gin_ref, gout_ref, wg_ref, wu_ref, wd_ref, o_ref, h_sc, acc_sc):
    f = pl.program_id(1)

    @pl.when(f == 0)
    def _():
        h_sc[...] = _rms(x_ref[...], gin_ref[...]).astype(BF16)
        acc_sc[...] = jnp.zeros_like(acc_sc)

    acc_sc[...] += _swiglu_tile(h_sc[...], wg_ref, wu_ref, wd_ref)

    @pl.when(f == pl.num_programs(1) - 1)
    def _():
        o_ref[...] = x_ref[...] + _rms(acc_sc[...], gout_ref[...])


def _ffn(x, g_in, g_out, wg, wu, wd, tm=1024, tf=1792):
    t, d = x.shape
    ff = wg.shape[1]
    tm = min(tm, t)
    return pl.pallas_call(
        _ffn_kernel,
        grid=(t // tm, ff // tf),
        in_specs=[pl.BlockSpec((tm, d), lambda i, f: (i, 0)),
                  pl.BlockSpec((1, d), lambda i, f: (0, 0)),
                  pl.BlockSpec((1, d), lambda i, f: (0, 0)),
                  pl.BlockSpec((d, tf), lambda i, f: (0, f)),
                  pl.BlockSpec((d, tf), lambda i, f: (0, f)),
                  pl.BlockSpec((tf, d), lambda i, f: (f, 0))],
        out_specs=pl.BlockSpec((tm, d), lambda i, f: (i, 0)),
        out_shape=jax.ShapeDtypeStruct((t, d), F32),
        scratch_shapes=[pltpu.VMEM((tm, d), BF16), pltpu.VMEM((tm, d), F32)],
        compiler_params=_params("parallel", "arbitrary"),
        name="ffn",
    )(x, g_in.reshape(1, d), g_out.reshape(1, d), wg, wu, wd)


_POS_SPLIT = 256
_LEFT, _DIAG, _RIGHT = 0, 1, 2
_BOUND_LANE = 12
_CHUNK_LANE = 15
_OFFSET_LANE = 18
_ZERO_EXP2_ARG = 136.0
_MIN_DENOM = 2.0 ** -60


def _bf16_split(x):
    hi = x.astype(BF16)
    r = x - hi.astype(F32)
    mid = r.astype(BF16)
    lo = (r - mid.astype(F32)).astype(BF16)
    return hi, mid, lo


def _diff_attn_kernel(reach_ref, slope_ref, q_ref, k_ref, v_ref, qf_ref, kf_ref, dist_ref, lam_ref,
                      g_ref, o_ref, qa_sc, ka_sc, va_sc, k2_sc, m_sc, acc_sc, sa_sc, pa_sc, pb_sc, kt_sc, *, tk,
                      out_scale, lambda_init):
    seq = k_ref.shape[0]
    nk = seq // tk
    k = k_ref[...]
    ka_sc[:, :B_VDIM] = k
    va_sc[:, :B_VDIM] = v_ref[...]
    lane = lax.broadcasted_iota(jnp.int32, (seq, LANES), 1)
    va_sc[:, B_VDIM:] = jnp.where(lane == 0, 1.0, 0.0).astype(BF16)
    kf = kf_ref[...]
    flane = lax.broadcasted_iota(jnp.int32, kf.shape, 1)
    chunk_lanes = jnp.logical_and(flane >= _CHUNK_LANE, flane < _CHUNK_LANE + 3)
    for c in range(nk):
        ka_sc[c * tk:(c + 1) * tk, B_VDIM:] = jnp.where(chunk_lanes, float(c), kf).astype(BF16)
    for c in range(nk):
        kt_sc[:, c * tk:(c + 1) * tk] = ka_sc[c * tk:(c + 1) * tk, :].T
    kk = k.astype(F32)
    kk = kk * kk
    for c, sel in enumerate((lane < HEAD_DIM, lane >= HEAD_DIM)):
        n2 = jnp.sum(jnp.where(sel, kk, 0.0), axis=-1, keepdims=True)
        k2_sc[c] = jnp.broadcast_to(jnp.max(n2, axis=0, keepdims=True), k2_sc.shape[1:])

    def body(i, carry):
        i = jnp.asarray(i, jnp.int32)
        _diff_attn_block(i, reach_ref, slope_ref, q_ref, qf_ref, dist_ref, lam_ref, g_ref, o_ref, qa_sc,
                         ka_sc, va_sc, k2_sc, m_sc, acc_sc, sa_sc, pa_sc, pb_sc, kt_sc, tq=tk, nk=nk,
                         out_scale=out_scale, lambda_init=lambda_init)
        return carry

    lax.fori_loop(0, nk, body, 0)


def _diff_attn_block(i, reach_ref, slope_ref, q_ref, qf_ref, dist_ref, lam_ref, g_ref, o_ref, qa_sc,
                     ka_sc, va_sc, k2_sc, m_sc, acc_sc, sa_sc, pa_sc, pb_sc, kt_sc, *, tq, nk, out_scale,
                     lambda_init):
    h = pl.program_id(1)
    tk = tq
    slope = slope_ref[h]
    reach = reach_ref[h]
    rows = pl.ds(pl.multiple_of(i * tq, tq), tq)
    q = q_ref[rows, :]
    lane = lax.broadcasted_iota(jnp.int32, q.shape, 1)
    zero = jnp.zeros_like(q)
    q0 = jnp.where(lane < HEAD_DIM, q, zero)
    q1 = jnp.where(lane >= HEAD_DIM, q, zero)
    qq = q.astype(F32)
    qq = (qq * qq).astype(BF16)
    comp = lax.broadcasted_iota(jnp.int32, (B_VDIM, LANES), 0)
    n0 = _dot(qq, jnp.where(comp < HEAD_DIM, 1.0, 0.0).astype(BF16)) * (1.0 + 2.0 ** -7)
    n1 = _dot(qq, jnp.where(comp >= HEAD_DIM, 1.0, 0.0).astype(BF16)) * (1.0 + 2.0 ** -7)
    bound = jnp.concatenate([jnp.sqrt(n0) * jnp.sqrt(k2_sc[0][0:1, :]),
                             jnp.sqrt(n1) * jnp.sqrt(k2_sc[1][0:1, :])], axis=0)
    off = slope * (i * tk).astype(F32)
    qf = qf_ref[...]
    qf2 = jnp.concatenate([qf, qf], axis=0)

    def piece_lanes(x, first_lane):
        lane_ = lax.broadcasted_iota(jnp.int32, x.shape, 1)
        out = jnp.zeros_like(x)
        for n, piece in enumerate(_bf16_split(x)):
            out = jnp.where(lane_ == first_lane + n, piece.astype(F32), out)
        return out.astype(BF16)

    def build_queries(with_bound):
        feats = {_LEFT: -qf2, _DIAG: jnp.zeros_like(qf2), _RIGHT: qf2}
        if with_bound:
            base = piece_lanes(bound, _BOUND_LANE)
            side = piece_lanes(jnp.full((SUBLANES, LANES), off, F32), _OFFSET_LANE)[0:1]
            feats = {_LEFT: base + side - qf2, _DIAG: base, _RIGHT: base - side + qf2}
        for variant, feat in feats.items():
            qa_sc[variant, :, B_VDIM:] = feat
            qa_sc[variant, :tq, :B_VDIM] = q0
            qa_sc[variant, tq:, :B_VDIM] = q1

    def scores(j, s_ref):
        variant = jnp.where(j < i, _LEFT, jnp.where(j == i, _DIAG, _RIGHT))
        start = pl.multiple_of(j * tk, tk)
        s_ref[...] = _dot_nt(qa_sc[variant], ka_sc[pl.ds(start, tk), :])

    def diag_bias(j, s_ref):
        @pl.when(j == i)
        def _():
            bias = slope * dist_ref[...]
            s_ref[:tq] -= bias
            s_ref[tq:] -= bias

    def probs(j, p_ref):
        start = pl.multiple_of(j * tk, tk)
        s = _dot(qa_sc[jnp.where(j < i, _LEFT, _RIGHT)], kt_sc[:, pl.ds(start, tk)])
        p_ref[...] = jnp.exp2(s).astype(BF16)

    def values(j):
        return va_sc[pl.ds(pl.multiple_of(j * tk, tk), tk), :]

    build_queries(True)
    lo = jnp.maximum(i - reach, 0)
    count = jnp.minimum(i + reach + 1, nk) - lo - 1

    def off_diag(n):
        j = lo + jnp.minimum(n, count - 1)
        return j + (j >= i).astype(jnp.int32)

    s_diag = _dot(qa_sc[_DIAG], kt_sc[:, pl.ds(pl.multiple_of(i * tk, tk), tk)])
    probs(off_diag(0), pa_sc)
    bias = slope * dist_ref[...]
    p_diag = jnp.exp2(s_diag - jnp.concatenate([bias, bias], axis=0)).astype(BF16)
    acc_sc[...] = _dot(p_diag, values(i))

    def pair(tt, carry):
        n = 2 * tt
        probs(off_diag(n + 1), pb_sc)
        acc_sc[...] += _dot(pa_sc[...], values(off_diag(n)))
        probs(off_diag(n + 2), pa_sc)
        acc_sc[...] += _dot(pb_sc[...], values(off_diag(n + 1)))
        return carry

    lax.fori_loop(0, count // 2, pair, 0)

    @pl.when(count % 2 == 1)
    def _():
        acc_sc[...] += _dot(pa_sc[...], values(off_diag(count - 1)))

    denom_min = jnp.min(acc_sc[:, B_VDIM:B_VDIM + 1])

    @pl.when(jnp.logical_not(denom_min >= _MIN_DENOM))
    def _():
        build_queries(False)
        m_sc[...] = jnp.full_like(m_sc, NEG_INF)
        acc_sc[...] = jnp.zeros_like(acc_sc)

        def chunk(j, carry):
            scores(j, sa_sc)
            diag_bias(j, sa_sc)
            adj = jnp.where(j > i, off, jnp.where(j < i, -off, 0.0))
            start = pl.multiple_of(j * tk, tk)
            s = sa_sc[...]
            m_prev = m_sc[...]
            m_new = jnp.maximum(m_prev, jnp.max(s, axis=-1, keepdims=True) + adj)
            p = jnp.exp2(s - (m_new - adj)).astype(BF16)
            acc_sc[...] = jnp.exp2(m_prev - m_new) * acc_sc[...] + _dot(p, va_sc[pl.ds(start, tk), :])
            m_sc[...] = m_new
            return carry

        lax.fori_loop(0, nk, chunk, 0)

    lp = lam_ref[...]
    lam = (jnp.exp(jnp.sum(lp[0:1] * lp[1:2], axis=-1, keepdims=True))
           - jnp.exp(jnp.sum(lp[2:3] * lp[3:4], axis=-1, keepdims=True)) + lambda_init)
    acc = acc_sc[...]
    o0 = acc[:tq, :B_VDIM] * (1.0 / acc[:tq, B_VDIM:B_VDIM + 1])
    o1 = acc[tq:, :B_VDIM] * (1.0 / acc[tq:, B_VDIM:B_VDIM + 1])
    o_ref[rows, :] = (_rms(o0 - lam * o1, g_ref[...]) * out_scale).astype(o_ref.dtype)


def _bf16_pieces(x, n=3):
    out = []
    r = np.float64(x)
    for _ in range(n):
        p = np.float64(np.float32(r).astype(jnp.bfloat16).astype(np.float32))
        out.append(p)
        r = r - p
    return out


def _alibi_features(slopes2, t):
    pos = np.arange(t)
    hi = (pos // _POS_SPLIT) * _POS_SPLIT
    lo = pos % _POS_SPLIT
    qf = np.zeros((len(slopes2), t, LANES), np.float32)
    kf = np.zeros((len(slopes2), t, LANES), np.float32)
    for h, s in enumerate(slopes2):
        for n, piece in enumerate(_bf16_pieces(s)):
            for base, part in ((0, hi), (3, lo)):
                qf[h, :, base + n] = -piece
                kf[h, :, base + n] = part
                qf[h, :, 6 + base + n] = part
                kf[h, :, 6 + base + n] = piece
        for n, piece in enumerate(_bf16_pieces(s * t)):
            qf[h, :, _CHUNK_LANE + n] = -piece
        kf[h, :, _BOUND_LANE:_BOUND_LANE + 3] = -1.0
        kf[h, :, _OFFSET_LANE:_OFFSET_LANE + 3] = -1.0
    return jnp.asarray(qf, BF16), jnp.asarray(kf, BF16)


def _diff_attention(proj, b_lambda, subln_g, lambda_init, batch, seq, t=512):
    t = min(t, seq // 2)
    nq = seq // t
    assert seq % (2 * t) == 0
    kcol = MIX_WIDTH // B_VDIM
    vcol = 2 * kcol
    slopes2 = [s * LOG2E for s in _alibi_slopes(B_HEADS)]
    reach = [min(nq, int(math.floor((_ZERO_EXP2_ARG / s - 1.0) / t)) + 1) for s in slopes2]
    qf, kf = _alibi_features(slopes2, t)
    pos = np.arange(t)
    dist = jnp.asarray(np.abs(pos[:, None] - pos[None, :]), F32)
    kernel = functools.partial(_diff_attn_kernel, tk=t, out_scale=1.0 - lambda_init,
                               lambda_init=lambda_init)
    return pl.pallas_call(
        kernel,
        grid_spec=pltpu.PrefetchScalarGridSpec(
            num_scalar_prefetch=2,
            grid=(batch, B_HEADS),
            in_specs=[pl.BlockSpec((seq, B_VDIM), lambda b, h, r, s: (b, h)),
                      pl.BlockSpec((seq, B_VDIM), lambda b, h, r, s: (b, kcol + h)),
                      pl.BlockSpec((seq, B_VDIM), lambda b, h, r, s: (b, vcol + h)),
                      pl.BlockSpec((None, t, LANES), lambda b, h, r, s: (h, 0, 0)),
                      pl.BlockSpec((None, t, LANES), lambda b, h, r, s: (h, 0, 0)),
                      pl.BlockSpec((t, t), lambda b, h, r, s: (0, 0)),
                      pl.BlockSpec((4, HEAD_DIM), lambda b, h, r, s: (0, 0)),
                      pl.BlockSpec((1, B_VDIM), lambda b, h, r, s: (0, 0))],
            out_specs=pl.BlockSpec((seq, B_VDIM), lambda b, h, r, s: (b, h)),
            scratch_shapes=[pltpu.VMEM((3, 2 * t, 2 * LANES), BF16),
                            pltpu.VMEM((seq, 2 * LANES), BF16),
                            pltpu.VMEM((seq, 2 * LANES), BF16),
                            pltpu.VMEM((2, 8, LANES), F32),
                            pltpu.VMEM((2 * t, 1), F32),
                            pltpu.VMEM((2 * t, 2 * LANES), F32),
                            pltpu.VMEM((2 * t, t), F32),
                            pltpu.VMEM((2 * t, t), BF16),
                            pltpu.VMEM((2 * t, t), BF16),
                            pltpu.VMEM((2 * LANES, seq), BF16)],
        ),
        out_shape=jax.ShapeDtypeStruct((batch * seq, MIX_WIDTH), BF16),
        compiler_params=_params("parallel", "parallel"),
        name="diff_attention",
    )(jnp.asarray(reach, jnp.int32), jnp.asarray(slopes2, F32), proj, proj, proj, qf, kf, dist,
      b_lambda, subln_g.reshape(1, B_VDIM))


def _router_kernel(x_ref, g_ref, whi_ref, wlo_ref, h_ref, e_ref, gate_ref):
    h = _rms(x_ref[...], g_ref[...])
    h_hi = h.astype(BF16)
    h_lo = (h - h_hi.astype(F32)).astype(BF16)
    whi = whi_ref[...]
    logits = _dot(h_hi, whi) + (_dot(h_hi, wlo_ref[...]) + _dot(h_lo, whi))
    lane = lax.broadcasted_iota(jnp.int32, logits.shape, 1)
    logits = jnp.where(lane < N_EXPERTS, logits, -jnp.inf)
    m1 = jnp.max(logits, axis=-1, keepdims=True)
    i1 = jnp.min(jnp.where(logits == m1, lane, LANES), axis=-1, keepdims=True)
    rest = jnp.where(lane == i1, -jnp.inf, logits)
    m2 = jnp.max(rest, axis=-1, keepdims=True)
    i2 = jnp.min(jnp.where(rest == m2, lane, LANES), axis=-1, keepdims=True)
    r = jnp.exp(m2 - m1)
    g1 = 1.0 / (1.0 + r)
    g2 = r * g1
    e_ref[...] = jnp.where(lane == 0, i1, jnp.where(lane == 1, i2, 0))
    gate_ref[...] = jnp.where(lane == 0, g1, jnp.where(lane == 1, g2, 0.0))
    h_ref[...] = h_hi


def _router(x, g, w_router, tm=2048):
    t, d = x.shape
    tm = min(tm, t)
    wpad = jnp.zeros((d, LANES), F32).at[:, :N_EXPERTS].set(w_router)
    whi = wpad.astype(BF16)
    wlo = (wpad - whi.astype(F32)).astype(BF16)
    return pl.pallas_call(
        _router_kernel,
        grid=(t // tm,),
        in_specs=[pl.BlockSpec((tm, d), lambda i: (i, 0)),
                  pl.BlockSpec((1, d), lambda i: (0, 0)),
                  pl.BlockSpec((d, LANES), lambda i: (0, 0)),
                  pl.BlockSpec((d, LANES), lambda i: (0, 0))],
        out_specs=[pl.BlockSpec((tm, d), lambda i: (i, 0)),
                   pl.BlockSpec((tm, LANES), lambda i: (i, 0)),
                   pl.BlockSpec((tm, LANES), lambda i: (i, 0))],
        out_shape=[jax.ShapeDtypeStruct((t, d), BF16),
                   jax.ShapeDtypeStruct((t, LANES), jnp.int32),
                   jax.ShapeDtypeStruct((t, LANES), F32)],
        compiler_params=_params("parallel"),
        name="router",
    )(x, g.reshape(1, d), whi, wlo)


def _slab_sizes(rows, tc):
    quarter, half = tc // 4, tc // 2
    return ((quarter, rows <= quarter),
            (half, jnp.logical_and(rows > quarter, rows <= half)),
            (tc, rows > half))


def _slab_stride(slab_ref, n, step, tc):
    most = slab_ref[(n + step) * N_EXPERTS]
    for e in range(1, N_EXPERTS):
        most = jnp.maximum(most, slab_ref[(n + step) * N_EXPERTS + e])
    return jnp.where(most <= tc // 2, tc // 2, tc)


def _dispatch_kernel(slab_ref, fill_ref, h_ref, dest_ref, gate_ref, xs_hbm, gs_hbm,
                     xslab_sc, gslab_sc, zx_sc, zg_sc, sem, zsem):
    i = pl.program_id(0)
    n = pl.num_programs(0)
    slot = i % 2
    tc = h_ref.shape[0]
    tm = zx_sc.shape[0]

    def slab_dma(step, sl, op):
        stride = _slab_stride(slab_ref, n, step, tc)
        for e in range(N_EXPERTS):
            start = pl.multiple_of(slab_ref[step * N_EXPERTS + e], SUBLANES)
            first = pl.multiple_of(e * stride, SUBLANES)
            for size, cond in _slab_sizes(slab_ref[(n + step) * N_EXPERTS + e], tc):
                @pl.when(cond)
                def _():
                    for src, dst in ((xslab_sc, xs_hbm), (gslab_sc, gs_hbm)):
                        copy = pltpu.make_async_copy(src.at[sl, pl.ds(first, size)],
                                                     dst.at[pl.ds(start, size)], sem.at[sl])
                        getattr(copy, op)()

    def fill_copies(start):
        start = pl.multiple_of(start, SUBLANES)
        return [pltpu.make_async_copy(zx_sc, xs_hbm.at[pl.ds(start, tm)], zsem),
                pltpu.make_async_copy(zg_sc, gs_hbm.at[pl.ds(start, tm)], zsem)]

    @pl.when(i == 0)
    def _():
        zx_sc[...] = jnp.zeros_like(zx_sc)
        zg_sc[...] = jnp.zeros_like(zg_sc)
        for half in range(2):
            for e in range(N_EXPERTS):
                for c in fill_copies(fill_ref[half * N_EXPERTS + e]):
                    c.start()
            for e in range(N_EXPERTS):
                for c in fill_copies(fill_ref[half * N_EXPERTS + e]):
                    c.wait()
        first_free = fill_ref[2 * N_EXPERTS]
        n_tiles = xs_hbm.shape[0] // tm

        def start_tile(b, carry):
            for c in fill_copies(b * tm):
                c.start()
            return carry

        def wait_tile(b, carry):
            for c in fill_copies(b * tm):
                c.wait()
            return carry

        lax.fori_loop(first_free, n_tiles, start_tile, 0)
        lax.fori_loop(first_free, n_tiles, wait_tile, 0)

    dest = dest_ref[0]
    gate = gate_ref[...]
    lane = lax.broadcasted_iota(jnp.int32, gate.shape, 1)

    def gate_pieces(col):
        out = jnp.zeros(gate.shape, F32)
        for n_, piece in enumerate(_bf16_split(col)):
            out = jnp.where(lane == n_, piece.astype(F32), out)
        return out.astype(BF16)

    def place_rows(stride):
        pos = lax.broadcasted_iota(jnp.int32, (stride, tc), 0)
        sel0, sel1 = [], []
        for e in range(N_EXPERTS):
            start = slab_ref[i * N_EXPERTS + e]
            sel0.append(jnp.where(dest[0:1, :] - start == pos, 1.0, 0.0))
            sel1.append(jnp.where(dest[1:2, :] - start == pos, 1.0, 0.0))
        sel0 = jnp.concatenate(sel0, axis=0).astype(BF16)
        sel1 = jnp.concatenate(sel1, axis=0).astype(BF16)
        rows = pl.ds(0, N_EXPERTS * stride)
        xslab_sc[slot, rows] = _dot(sel0 + sel1, h_ref[...])
        gslab_sc[slot, rows] = (_dot(sel0, gate_pieces(gate[:, 0:1]))
                                + _dot(sel1, gate_pieces(gate[:, 1:2])))

    compact = _slab_stride(slab_ref, n, i, tc) < tc
    pl.when(compact)(lambda: place_rows(tc // 2))
    pl.when(jnp.logical_not(compact))(lambda: place_rows(tc))

    @pl.when(i > 0)
    def _():
        slab_dma(i - 1, 1 - slot, "wait")

    slab_dma(i, slot, "start")

    @pl.when(i == n - 1)
    def _():
        slab_dma(i, slot, "wait")


def _dispatch(h, dest_t, gates, slab, fill, p_rows, tm, tc):
    t, d = h.shape
    nb = t // tc
    return pl.pallas_call(
        _dispatch_kernel,
        grid_spec=pltpu.PrefetchScalarGridSpec(
            num_scalar_prefetch=2,
            grid=(nb,),
            in_specs=[pl.BlockSpec((tc, d), lambda i, s, z: (i, 0)),
                      pl.BlockSpec((1, TOP_K, tc), lambda i, s, z: (i, 0, 0)),
                      pl.BlockSpec((tc, LANES), lambda i, s, z: (i, 0))],
            out_specs=[pl.BlockSpec(memory_space=pl.ANY), pl.BlockSpec(memory_space=pl.ANY)],
            scratch_shapes=[pltpu.VMEM((2, N_EXPERTS * tc, d), F32),
                            pltpu.VMEM((2, N_EXPERTS * tc, LANES), F32),
                            pltpu.VMEM((tm, d), F32), pltpu.VMEM((tm, LANES), F32),
                            pltpu.SemaphoreType.DMA((2,)), pltpu.SemaphoreType.DMA(())],
        ),
        out_shape=[jax.ShapeDtypeStruct((p_rows, d), F32), jax.ShapeDtypeStruct((p_rows, LANES), F32)],
        compiler_params=_params("arbitrary"),
        name="moe_dispatch",
    )(slab, fill, h, dest_t, gates)


def _moe_kernel(blk_e_ref, used_ref, xblk_ref, x_ref, gs_ref, wg_ref, wu_ref, wd_ref, o_ref,
                xb_sc, acc_sc):
    i = pl.program_id(0)
    f = pl.program_id(1)
    used = used_ref[i] > 0

    @pl.when(jnp.logical_and(used, f == 0))
    def _():
        xb_sc[...] = x_ref[...].astype(BF16)
        acc_sc[...] = jnp.zeros_like(acc_sc)

    @pl.when(used)
    def _():
        acc_sc[...] += _swiglu_tile(xb_sc[...], wg_ref, wu_ref, wd_ref)

    @pl.when(f == pl.num_programs(1) - 1)
    def _():
        gs = gs_ref[...]
        gate = gs[:, 0:1] + gs[:, 1:2] + gs[:, 2:3]
        o_ref[...] = jnp.where(used, acc_sc[...] * gate, 0.0)


def _moe_experts(xs, gs, blk_e, blk_used, xblk, wg, wu, wd, tm, tf=1792):
    p, d = xs.shape
    nblk = p // tm
    nf = wg.shape[2] // tf

    def fidx(i, f, used_ref):
        return jnp.where(used_ref[i] > 0, f, nf - 1)

    return pl.pallas_call(
        _moe_kernel,
        grid_spec=pltpu.PrefetchScalarGridSpec(
            num_scalar_prefetch=3,
            grid=(nblk, nf),
            in_specs=[pl.BlockSpec((tm, d), lambda i, f, be, us, xb: (xb[i], 0)),
                      pl.BlockSpec((tm, LANES), lambda i, f, be, us, xb: (xb[i], 0)),
                      pl.BlockSpec((None, d, tf), lambda i, f, be, us, xb: (be[i], 0, fidx(i, f, us))),
                      pl.BlockSpec((None, d, tf), lambda i, f, be, us, xb: (be[i], 0, fidx(i, f, us))),
                      pl.BlockSpec((None, tf, d), lambda i, f, be, us, xb: (be[i], fidx(i, f, us), 0))],
            out_specs=pl.BlockSpec((tm, d), lambda i, f, be, us, xb: (i, 0)),
            scratch_shapes=[pltpu.VMEM((tm, d), BF16), pltpu.VMEM((tm, d), F32)],
        ),
        out_shape=jax.ShapeDtypeStruct((p, d), F32),
        compiler_params=_params("arbitrary", "arbitrary"),
        name="moe_experts",
    )(blk_e, blk_used, xblk, xs, gs, wg, wu, wd)


def _combine_kernel(slab_ref, ys_hbm, dest_ref, x_ref, g_ref, o_ref, slab_sc, sem):
    i = pl.program_id(0)
    n = pl.num_programs(0)
    slot = i % 2
    tc = x_ref.shape[0]

    def slab_dma(step, sl, op):
        stride = _slab_stride(slab_ref, n, step, tc)
        for e in range(N_EXPERTS):
            start = pl.multiple_of(slab_ref[step * N_EXPERTS + e], SUBLANES)
            first = pl.multiple_of(e * stride, SUBLANES)
            for size, cond in _slab_sizes(slab_ref[(n + step) * N_EXPERTS + e], tc):
                @pl.when(cond)
                def _():
                    copy = pltpu.make_async_copy(ys_hbm.at[pl.ds(start, size)],
                                                 slab_sc.at[sl, pl.ds(first, size)], sem.at[sl])
                    getattr(copy, op)()

    @pl.when(i == 0)
    def _():
        slab_sc[...] = jnp.zeros_like(slab_sc)
        slab_dma(0, 0, "start")

    @pl.when(i + 1 < n)
    def _():
        slab_dma(i + 1, 1 - slot, "start")

    dest = dest_ref[...]
    slab_dma(i, slot, "wait")

    def gather_rows(stride):
        pos = lax.broadcasted_iota(jnp.int32, (tc, stride), 1)
        sel = []
        for e in range(N_EXPERTS):
            start = slab_ref[i * N_EXPERTS + e]
            hit = jnp.logical_or(dest[:, 0:1] - start == pos, dest[:, 1:2] - start == pos)
            sel.append(jnp.where(hit, 1.0, 0.0))
        sel = jnp.concatenate(sel, axis=1).astype(BF16)
        y = _dot(sel, slab_sc[slot, pl.ds(0, N_EXPERTS * stride)].astype(BF16))
        o_ref[...] = x_ref[...] + _rms(y, g_ref[...])

    compact = _slab_stride(slab_ref, n, i, tc) < tc
    pl.when(compact)(lambda: gather_rows(tc // 2))
    pl.when(jnp.logical_not(compact))(lambda: gather_rows(tc))


def _combine(ys, dest, slab, x, g, tc):
    t, d = x.shape
    return pl.pallas_call(
        _combine_kernel,
        grid_spec=pltpu.PrefetchScalarGridSpec(
            num_scalar_prefetch=1,
            grid=(t // tc,),
            in_specs=[pl.BlockSpec(memory_space=pl.ANY),
                      pl.BlockSpec((tc, LANES), lambda i, s: (i, 0)),
                      pl.BlockSpec((tc, d), lambda i, s: (i, 0)),
                      pl.BlockSpec((1, d), lambda i, s: (0, 0))],
            out_specs=pl.BlockSpec((tc, d), lambda i, s: (i, 0)),
            scratch_shapes=[pltpu.VMEM((2, N_EXPERTS * tc, d), F32), pltpu.SemaphoreType.DMA((2,))],
        ),
        out_shape=jax.ShapeDtypeStruct((t, d), F32),
        compiler_params=_params("arbitrary"),
        name="moe_combine",
    )(slab, ys, dest, x, g.reshape(1, d))


def _route(top_e, tm, tc):
    t = top_e.shape[0]
    a = t * TOP_K
    nb = t // tc
    e_flat = top_e.reshape(a)
    onehot = (e_flat[:, None] == jnp.arange(N_EXPERTS, dtype=jnp.int32)[None, :]).astype(jnp.int32)
    csum = jnp.cumsum(onehot, axis=0)
    rank = jnp.sum(onehot * (csum - 1), axis=1)
    upto = csum[TOP_K * tc - 1::TOP_K * tc]
    before = jnp.concatenate([jnp.zeros((1, N_EXPERTS), jnp.int32), upto[:nb - 1]], axis=0)
    rows = (upto - before + SUBLANES - 1) // SUBLANES * SUBLANES
    rows_before = jnp.cumsum(rows, axis=0) - rows
    total = jnp.sum(rows, axis=0)
    region = (total + 2 * tm - 1) // tm * tm
    rends = jnp.cumsum(region)
    rstarts = rends - region
    slab = (rstarts[None, :] + rows_before).astype(jnp.int32)
    shift = jnp.broadcast_to((slab - before)[:, None, :], (nb, TOP_K * tc, N_EXPERTS)).reshape(a, N_EXPERTS)
    dest = (rank + jnp.sum(onehot * shift, axis=1)).astype(jnp.int32).reshape(t, TOP_K)
    slab = jnp.concatenate([slab.reshape(nb * N_EXPERTS), rows.reshape(nb * N_EXPERTS)])
    row_end = rstarts + total
    fill = jnp.concatenate([row_end, rends - tm, rends[-1:] // tm]).astype(jnp.int32)
    nblk = -(-(a + nb * N_EXPERTS * (SUBLANES - 1)) // tm) + 2 * N_EXPERTS
    blk_start = jnp.arange(nblk, dtype=jnp.int32) * tm
    blk_e = jnp.minimum(jnp.sum((rends[None, :] <= blk_start[:, None]).astype(jnp.int32), axis=1),
                        N_EXPERTS - 1)
    blk_used = jnp.logical_and(blk_start < row_end[blk_e], blk_start < rends[-1]).astype(jnp.int32)
    first_used = jnp.argmax(blk_used).astype(jnp.int32)
    xblk = jnp.where(blk_used > 0, jnp.arange(nblk, dtype=jnp.int32), first_used)
    return dest, slab, fill, blk_e, blk_used, xblk, nblk * tm


def _moe(x, h, e_pad, gates, g_out, wg, wu, wd, tm=512, tc=256):
    t = x.shape[0]
    tc = min(tc, t)
    dest, slab, fill, blk_e, blk_used, xblk, p_rows = _route(e_pad[:, :TOP_K], tm, tc)
    dest_t = dest.reshape(t // tc, tc, TOP_K).transpose(0, 2, 1)
    dest_pad = jnp.zeros((t, LANES), jnp.int32).at[:, :TOP_K].set(dest)
    xs, gs = _dispatch(h, dest_t, gates, slab, fill, p_rows, tm, tc)
    ys = _moe_experts(xs, gs, blk_e, blk_used, xblk, wg, wu, wd, tm)
    return _combine(ys, dest_pad, slab, x, g_out, tc)


def kernel(x, mem, norm_g, mem_norm_g, w_mem_kv, a_w_in, a_sink, a_w_out, b_w_in, b_lambda,
           b_subln_g, b_w_out, ffn_w_gate, ffn_w_up, ffn_w_down, moe_w_router, moe_w_gate,
           moe_w_up, moe_w_down):
    batch, seq, d = x.shape
    n_mem = mem.shape[1]
    bf = lambda w: w.astype(BF16)
    xt = x.reshape(batch * seq, d)
    mkv = _norm_matmul(mem.reshape(batch * n_mem, d), mem_norm_g, bf(w_mem_kv),
                       jnp.ones((w_mem_kv.shape[1],), F32))

    g = norm_g[0]
    proj = _norm_matmul(xt, g[0], bf(a_w_in[0]), _query_col_scale(a_w_in.shape[2], A_Q_HEADS * HEAD_DIM))
    mix = _window_attention(proj, a_sink[0], batch, seq)
    mem_out = _mem_attention(proj, (A_Q_HEADS + 2 * A_KV_HEADS) * HEAD_DIM // MEM_WIDTH, mkv, batch, seq)
    xt = _out_proj(mix, mem_out, bf(a_w_out[0]), xt, g[1])
    xt = _ffn(xt, g[2], g[3], bf(ffn_w_gate[0]), bf(ffn_w_up[0]), bf(ffn_w_down[0]))

    g = norm_g[1]
    lambda_init = 0.8 - 0.6 * math.exp(-0.3 * 1)
    proj = _norm_matmul(xt, g[0], bf(b_w_in[0]), _query_col_scale(b_w_in.shape[2], MIX_WIDTH))
    mix = _diff_attention(proj, b_lambda[0], b_subln_g[0], lambda_init, batch, seq)
    mem_out = _mem_attention(proj, 3 * MIX_WIDTH // MEM_WIDTH, mkv, batch, seq)
    xt = _out_proj(mix, mem_out, bf(b_w_out[0]), xt, g[1])
    h, e_pad, gates = _router(xt, g[2], moe_w_router[0])
    xt = _moe(xt, h, e_pad, gates, g[3], bf(moe_w_gate[0]), bf(moe_w_up[0]), bf(moe_w_down[0]))
    return xt.reshape(batch, seq, d)
```

```python
import functools
import math

import jax
import jax.numpy as jnp
import numpy as np
from jax import lax
from jax.experimental import pallas as pl
from jax.experimental.pallas import tpu as pltpu

D_MODEL = 1024
HEAD_DIM = 64
MIX_WIDTH = 768
MEM_WIDTH = 256
MEM_HEADS = 4
A_Q_HEADS = 12
A_KV_HEADS = 4
A_GROUP = 3
WINDOW = 128
B_HEADS = 6
B_VDIM = 128
D_FF = 3584
N_EXPERTS = 8
TOP_K = 2
EPS = 1e-6
NEG_INF = -1e30
LOG2E = math.log2(math.e)
QK_SCALE = HEAD_DIM ** -0.5 * LOG2E
LANES = 128
SUBLANES = 8
VMEM_LIMIT_BYTES = 56 * 1024 * 1024

BF16 = jnp.bfloat16
F32 = jnp.float32


def _params(*sem):
    return pltpu.CompilerParams(dimension_semantics=sem, vmem_limit_bytes=VMEM_LIMIT_BYTES)


def _rms(x, g):
    return x * lax.rsqrt(jnp.mean(x * x, axis=-1, keepdims=True) + EPS) * g


def _dot(a, b):
    return jnp.dot(a, b, preferred_element_type=F32)


def _dot_nt(a, b):
    return lax.dot_general(a, b, (((1,), (1,)), ((), ())), preferred_element_type=F32)


def _alibi_slopes(n):
    return [2.0 ** (-8.0 * (i + 1) / n) for i in range(n)]


def _norm_matmul_kernel(x_ref, g_ref, w_ref, cs_ref, o_ref, *, chunk):
    h = _rms(x_ref[...], g_ref[...]).astype(BF16)
    for c in range(o_ref.shape[1] // chunk):
        sl = slice(c * chunk, (c + 1) * chunk)
        o_ref[:, sl] = (_dot(h, w_ref[:, sl]) * cs_ref[:, sl]).astype(o_ref.dtype)


def _norm_matmul(x, g, w, col_scale, tm=1024, chunk=512):
    t, d = x.shape
    n = w.shape[1]
    tm = min(tm, t)
    return pl.pallas_call(
        functools.partial(_norm_matmul_kernel, chunk=min(chunk, n)),
        grid=(t // tm,),
        in_specs=[pl.BlockSpec((tm, d), lambda i: (i, 0)),
                  pl.BlockSpec((1, d), lambda i: (0, 0)),
                  pl.BlockSpec((d, n), lambda i: (0, 0)),
                  pl.BlockSpec((1, n), lambda i: (0, 0))],
        out_specs=pl.BlockSpec((tm, n), lambda i: (i, 0)),
        out_shape=jax.ShapeDtypeStruct((t, n), BF16),
        compiler_params=_params("parallel"),
        name="norm_matmul",
    )(x, g.reshape(1, d), w, col_scale.reshape(1, n))


def _query_col_scale(n, mix_q_cols):
    cols = np.arange(n)
    return jnp.asarray(np.where((cols < mix_q_cols) | (cols >= n - MEM_WIDTH), QK_SCALE, 1.0), F32)


def _window_kernel(q_ref, kp_ref, kc_ref, kn_ref, vp_ref, vc_ref, vn_ref, bias_ref, o_ref):
    n = pl.program_id(1)
    nb = pl.num_programs(1)
    blk = kp_ref.shape[0]
    sub = q_ref.shape[0] // blk
    zeros = jnp.zeros((blk, kp_ref.shape[1]), kp_ref.dtype)
    k_all = jnp.concatenate([kp_ref[...], kc_ref[...], kn_ref[...]], axis=0)
    v_all = jnp.concatenate([vp_ref[...], vc_ref[...], vn_ref[...]], axis=0)
    col = lax.broadcasted_iota(jnp.int32, (1, 4 * blk), 1)
    half = lax.broadcasted_iota(jnp.int32, (4 * blk, LANES), 1) < HEAD_DIM
    low_half = jnp.where(half, 1.0, 0.0).astype(BF16)
    high_half = jnp.where(half, 0.0, 1.0).astype(BF16)
    for u in range(sub):
        k = jnp.concatenate([k_all[u * blk:(u + 3) * blk], zeros], axis=0)
        v = jnp.concatenate([v_all[u * blk:(u + 3) * blk], zeros], axis=0)
        edge = jnp.zeros((1, 4 * blk), F32)
        if u == 0:
            edge = jnp.where(jnp.logical_and(n == 0, col < blk), NEG_INF, edge)
        if u == sub - 1:
            edge = jnp.where(jnp.logical_and(n == nb - 1,
                                             jnp.logical_and(col >= 2 * blk, col < 3 * blk)), NEG_INF, edge)
        q = q_ref[u * blk:(u + 1) * blk, :]
        outs = []
        for kh in range(A_KV_HEADS):
            heads = range(kh * A_GROUP, (kh + 1) * A_GROUP)
            kv = slice(kh * HEAD_DIM, (kh + 1) * HEAD_DIM)
            qs = jnp.concatenate([q[:, h * HEAD_DIM:(h + 1) * HEAD_DIM] for h in heads], axis=0)
            s = _dot_nt(qs, k[:, kv]) + bias_ref[kh] + edge
            p = jnp.exp2(s - jnp.max(s, axis=-1, keepdims=True)).astype(BF16)
            low = kh % 2 == 0
            slab = v[:, (kh // 2) * LANES:(kh // 2 + 1) * LANES]
            own, other = (low_half, high_half) if low else (high_half, low_half)
            ov = _dot(p, jnp.concatenate([slab * own + other, own], axis=1))
            den = ov[:, LANES:]
            o = ov[:, :LANES] * (1.0 / jnp.where(den == 0.0, 1.0, den))
            o = o[:, :HEAD_DIM] if low else o[:, HEAD_DIM:]
            outs.extend(o[g * blk:(g + 1) * blk] for g in range(A_GROUP))
        o_ref[u * blk:(u + 1) * blk, :] = jnp.concatenate(outs, axis=-1).astype(o_ref.dtype)


def _window_bias(blk, sink):
    qi = np.arange(blk)[:, None]
    kj = np.arange(3 * blk)[None, :]
    dist = np.abs(blk + qi - kj)
    slopes = _alibi_slopes(A_Q_HEADS)
    tables = np.stack([np.where(dist <= WINDOW, -(slopes[h] * LOG2E) * dist, NEG_INF)
                       for h in range(A_Q_HEADS)])
    pad = jnp.full((A_Q_HEADS, blk, blk), NEG_INF, F32)
    pad = pad.at[:, :, 0].set(jnp.broadcast_to((sink.astype(F32) * LOG2E)[:, None], (A_Q_HEADS, blk)))
    full = jnp.concatenate([jnp.asarray(tables, F32), pad], axis=2)
    return full.reshape(A_KV_HEADS, A_GROUP * blk, 4 * blk)


def _window_attention(proj, sink, batch, seq, sub=4):
    blk = WINDOW
    nb = seq // blk
    sub = min(sub, nb)
    ns = nb // sub
    kcol = MIX_WIDTH // 256
    vcol = kcol + 1
    bias = _window_bias(blk, sink)

    def cur(col):
        return pl.BlockSpec((sub * blk, 256), lambda b, n: (b * ns + n, col))

    def prev(col):
        return pl.BlockSpec((blk, 256), lambda b, n: (b * nb + jnp.maximum(n * sub - 1, 0), col))

    def nxt(col):
        return pl.BlockSpec((blk, 256), lambda b, n: (b * nb + jnp.minimum((n + 1) * sub, nb - 1), col))

    return pl.pallas_call(
        _window_kernel,
        grid=(batch, ns),
        in_specs=[pl.BlockSpec((sub * blk, MIX_WIDTH), lambda b, n: (b * ns + n, 0)),
                  prev(kcol), cur(kcol), nxt(kcol), prev(vcol), cur(vcol), nxt(vcol),
                  pl.BlockSpec(bias.shape, lambda b, n: (0, 0, 0))],
        out_specs=pl.BlockSpec((sub * blk, MIX_WIDTH), lambda b, n: (b * ns + n, 0)),
        out_shape=jax.ShapeDtypeStruct((batch * seq, MIX_WIDTH), BF16),
        compiler_params=_params("parallel", "parallel"),
        name="window_attention",
    )(proj, proj, proj, proj, proj, proj, proj, bias)


def _mem_attn_kernel(q_ref, mkv_ref, o_ref):
    q = q_ref[...]
    mkv = mkv_ref[...]
    ones = jnp.ones((mkv.shape[0], LANES), BF16)
    outs = []
    for h in range(MEM_HEADS):
        sl = slice(h * HEAD_DIM, (h + 1) * HEAD_DIM)
        s = _dot_nt(q[:, sl], mkv[:, sl])
        p = jnp.exp2(s - jnp.max(s, axis=-1, keepdims=True)).astype(BF16)
        o = _dot(p, mkv[:, MEM_WIDTH + h * HEAD_DIM:MEM_WIDTH + (h + 1) * HEAD_DIM])
        outs.append(o * (1.0 / _dot(p, ones)[:, :HEAD_DIM]))
    o_ref[...] = jnp.concatenate(outs, axis=-1).astype(o_ref.dtype)


def _mem_attention(proj, qcol, mkv, batch, seq, tm=2048):
    tm = min(tm, seq)
    per_batch = seq // tm
    n_mem = mkv.shape[0] // batch
    return pl.pallas_call(
        _mem_attn_kernel,
        grid=(batch * per_batch,),
        in_specs=[pl.BlockSpec((tm, MEM_WIDTH), lambda i: (i, qcol)),
                  pl.BlockSpec((n_mem, 2 * MEM_WIDTH), lambda i: (i // per_batch, 0))],
        out_specs=pl.BlockSpec((tm, MEM_WIDTH), lambda i: (i, 0)),
        out_shape=jax.ShapeDtypeStruct((batch * seq, MEM_WIDTH), BF16),
        compiler_params=_params("parallel"),
        name="mem_attention",
    )(proj, mkv)


def _out_proj_kernel(mix_ref, mem_ref, w1_ref, w2_ref, x_ref, g_ref, o_ref):
    o = _dot(mix_ref[...], w1_ref[...]) + _dot(mem_ref[...], w2_ref[...])
    o_ref[...] = x_ref[...] + _rms(o, g_ref[...])


def _out_proj(mix, mem_out, w_out, x, g, tm=1024):
    t, d = x.shape
    tm = min(tm, t)
    w1 = w_out[:MIX_WIDTH]
    w2 = w_out[MIX_WIDTH:]
    return pl.pallas_call(
        _out_proj_kernel,
        grid=(t // tm,),
        in_specs=[pl.BlockSpec((tm, MIX_WIDTH), lambda i: (i, 0)),
                  pl.BlockSpec((tm, MEM_WIDTH), lambda i: (i, 0)),
                  pl.BlockSpec((MIX_WIDTH, d), lambda i: (0, 0)),
                  pl.BlockSpec((MEM_WIDTH, d), lambda i: (0, 0)),
                  pl.BlockSpec((tm, d), lambda i: (i, 0)),
                  pl.BlockSpec((1, d), lambda i: (0, 0))],
        out_specs=pl.BlockSpec((tm, d), lambda i: (i, 0)),
        out_shape=jax.ShapeDtypeStruct((t, d), F32),
        compiler_params=_params("parallel"),
        name="out_proj",
    )(mix, mem_out, w1, w2, x, g.reshape(1, d))


def _swiglu_tile(h, wg_ref, wu_ref, wd_ref, chunk=256):
    out = None
    for c in range(wg_ref.shape[1] // chunk):
        sl = slice(c * chunk, (c + 1) * chunk)
        a = _dot(h, wg_ref[:, sl])
        u = _dot(h, wu_ref[:, sl])
        z = (a * jax.nn.sigmoid(a) * u).astype(BF16)
        y = _dot(z, wd_ref[sl, :])
        out = y if out is None else out + y
    return out


def _ffn_kernel(x_ref, gin_ref, gout_ref, wg_ref, wu_ref, wd_ref, o_ref, h_sc, acc_sc):
    f = pl.program_id(1)

    @pl.when(f == 0)
    def _():
        h_sc[...] = _rms(x_ref[...], gin_ref[...]).astype(BF16)
        acc_sc[...] = jnp.zeros_like(acc_sc)

    acc_sc[...] += _swiglu_tile(h_sc[...], wg_ref, wu_ref, wd_ref)

    @pl.when(f == pl.num_programs(1) - 1)
    def _():
        o_ref[...] = x_ref[...] + _rms(acc_sc[...], gout_ref[...])


def _ffn(x, g_in, g_out, wg, wu, wd, tm=1024, tf=1792):
    t, d = x.shape
    ff = wg.shape[1]
    tm = min(tm, t)
    return pl.pallas_call(
        _ffn_kernel,
        grid=(t // tm, ff // tf),
        in_specs=[pl.BlockSpec((tm, d), lambda i, f: (i, 0)),
                  pl.BlockSpec((1, d), lambda i, f: (0, 0)),
                  pl.BlockSpec((1, d), lambda i, f: (0, 0)),
                  pl.BlockSpec((d, tf), lambda i, f: (0, f)),
                  pl.BlockSpec((d, tf), lambda i, f: (0, f)),
                  pl.BlockSpec((tf, d), lambda i, f: (f, 0))],
        out_specs=pl.BlockSpec((tm, d), lambda i, f: (i, 0)),
        out_shape=jax.ShapeDtypeStruct((t, d), F32),
        scratch_shapes=[pltpu.VMEM((tm, d), BF16), pltpu.VMEM((tm, d), F32)],
        compiler_params=_params("parallel", "arbitrary"),
        name="ffn",
    )(x, g_in.reshape(1, d), g_out.reshape(1, d), wg, wu, wd)


_POS_SPLIT = 256
_LEFT, _DIAG, _RIGHT = 0, 1, 2
_BOUND_LANE = 12
_CHUNK_LANE = 15
_OFFSET_LANE = 18
_ZERO_EXP2_ARG = 136.0
_MIN_DENOM = 2.0 ** -60


def _bf16_split(x):
    hi = x.astype(BF16)
    r = x - hi.astype(F32)
    mid = r.astype(BF16)
    lo = (r - mid.astype(F32)).astype(BF16)
    return hi, mid, lo


def _diff_attn_kernel(reach_ref, slope_ref, q_ref, k_ref, v_ref, qf_ref, kf_ref, dist_ref, lam_ref,
                      g_ref, o_ref, qa_sc, ka_sc, va_sc, k2_sc, m_sc, acc_sc, sa_sc, pa_sc, pb_sc, kt_sc, *, tk,
                      out_scale, lambda_init):
    seq = k_ref.shape[0]
    nk = seq // tk
    k = k_ref[...]
    ka_sc[:, :B_VDIM] = k
    va_sc[:, :B_VDIM] = v_ref[...]
    lane = lax.broadcasted_iota(jnp.int32, (seq, LANES), 1)
    va_sc[:, B_VDIM:] = jnp.where(lane == 0, 1.0, 0.0).astype(BF16)
    kf = kf_ref[...]
    flane = lax.broadcasted_iota(jnp.int32, kf.shape, 1)
    chunk_lanes = jnp.logical_and(flane >= _CHUNK_LANE, flane < _CHUNK_LANE + 3)
    for c in range(nk):
        ka_sc[c * tk:(c + 1) * tk, B_VDIM:] = jnp.where(chunk_lanes, float(c), kf).astype(BF16)
    for c in range(nk):
        kt_sc[:, c * tk:(c + 1) * tk] = ka_sc[c * tk:(c + 1) * tk, :].T
    kk = k.astype(F32)
    kk = kk * kk
    for c, sel in enumerate((lane < HEAD_DIM, lane >= HEAD_DIM)):
        n2 = jnp.sum(jnp.where(sel, kk, 0.0), axis=-1, keepdims=True)
        k2_sc[c] = jnp.broadcast_to(jnp.max(n2, axis=0, keepdims=True), k2_sc.shape[1:])

    def body(i, carry):
        i = jnp.asarray(i, jnp.int32)
        _diff_attn_block(i, reach_ref, slope_ref, q_ref, qf_ref, dist_ref, lam_ref, g_ref, o_ref, qa_sc,
                         ka_sc, va_sc, k2_sc, m_sc, acc_sc, sa_sc, pa_sc, pb_sc, kt_sc, tq=tk, nk=nk,
                         out_scale=out_scale, lambda_init=lambda_init)
        return carry

    lax.fori_loop(0, nk, body, 0)


def _diff_attn_block(i, reach_ref, slope_ref, q_ref, qf_ref, dist_ref, lam_ref, g_ref, o_ref, qa_sc,
                     ka_sc, va_sc, k2_sc, m_sc, acc_sc, sa_sc, pa_sc, pb_sc, kt_sc, *, tq, nk, out_scale,
                     lambda_init):
    h = pl.program_id(1)
    tk = tq
    slope = slope_ref[h]
    reach = reach_ref[h]
    rows = pl.ds(pl.multiple_of(i * tq, tq), tq)
    q = q_ref[rows, :]
    lane = lax.broadcasted_iota(jnp.int32, q.shape, 1)
    zero = jnp.zeros_like(q)
    q0 = jnp.where(lane < HEAD_DIM, q, zero)
    q1 = jnp.where(lane >= HEAD_DIM, q, zero)
    qq = q.astype(F32)
    qq = (qq * qq).astype(BF16)
    comp = lax.broadcasted_iota(jnp.int32, (B_VDIM, LANES), 0)
    n0 = _dot(qq, jnp.where(comp < HEAD_DIM, 1.0, 0.0).astype(BF16)) * (1.0 + 2.0 ** -7)
    n1 = _dot(qq, jnp.where(comp >= HEAD_DIM, 1.0, 0.0).astype(BF16)) * (1.0 + 2.0 ** -7)
    bound = jnp.concatenate([jnp.sqrt(n0) * jnp.sqrt(k2_sc[0][0:1, :]),
                             jnp.sqrt(n1) * jnp.sqrt(k2_sc[1][0:1, :])], axis=0)
    off = slope * (i * tk).astype(F32)
    qf = qf_ref[...]
    qf2 = jnp.concatenate([qf, qf], axis=0)

    def piece_lanes(x, first_lane):
        lane_ = lax.broadcasted_iota(jnp.int32, x.shape, 1)
        out = jnp.zeros_like(x)
        for n, piece in enumerate(_bf16_split(x)):
            out = jnp.where(lane_ == first_lane + n, piece.astype(F32), out)
        return out.astype(BF16)

    def build_queries(with_bound):
        feats = {_LEFT: -qf2, _DIAG: jnp.zeros_like(qf2), _RIGHT: qf2}
        if with_bound:
            base = piece_lanes(bound, _BOUND_LANE)
            side = piece_lanes(jnp.full((SUBLANES, LANES), off, F32), _OFFSET_LANE)[0:1]
            feats = {_LEFT: base + side - qf2, _DIAG: base, _RIGHT: base - side + qf2}
        for variant, feat in feats.items():
            qa_sc[variant, :, B_VDIM:] = feat
            qa_sc[variant, :tq, :B_VDIM] = q0
            qa_sc[variant, tq:, :B_VDIM] = q1

    def scores(j, s_ref):
        variant = jnp.where(j < i, _LEFT, jnp.where(j == i, _DIAG, _RIGHT))
        start = pl.multiple_of(j * tk, tk)
        s_ref[...] = _dot_nt(qa_sc[variant], ka_sc[pl.ds(start, tk), :])

    def diag_bias(j, s_ref):
        @pl.when(j == i)
        def _():
            bias = slope * dist_ref[...]
            s_ref[:tq] -= bias
            s_ref[tq:] -= bias

    def probs(j, p_ref):
        start = pl.multiple_of(j * tk, tk)
        s = _dot(qa_sc[jnp.where(j < i, _LEFT, _RIGHT)], kt_sc[:, pl.ds(start, tk)])
        p_ref[...] = jnp.exp2(s).astype(BF16)

    def values(j):
        return va_sc[pl.ds(pl.multiple_of(j * tk, tk), tk), :]

    build_queries(True)
    lo = jnp.maximum(i - reach, 0)
    count = jnp.minimum(i + reach + 1, nk) - lo - 1

    def off_diag(n):
        j = lo + jnp.minimum(n, count - 1)
        return j + (j >= i).astype(jnp.int32)

    s_diag = _dot(qa_sc[_DIAG], kt_sc[:, pl.ds(pl.multiple_of(i * tk, tk), tk)])
    probs(off_diag(0), pa_sc)
    bias = slope * dist_ref[...]
    p_diag = jnp.exp2(s_diag - jnp.concatenate([bias, bias], axis=0)).astype(BF16)
    acc_sc[...] = _dot(p_diag, values(i))

    def pair(tt, carry):
        n = 2 * tt
        probs(off_diag(n + 1), pb_sc)
        acc_sc[...] += _dot(pa_sc[...], values(off_diag(n)))
        probs(off_diag(n + 2), pa_sc)
        acc_sc[...] += _dot(pb_sc[...], values(off_diag(n + 1)))
        return carry

    lax.fori_loop(0, count // 2, pair, 0)

    @pl.when(count % 2 == 1)
    def _():
        acc_sc[...] += _dot(pa_sc[...], values(off_diag(count - 1)))

    denom_min = jnp.min(acc_sc[:, B_VDIM:B_VDIM + 1])

    @pl.when(jnp.logical_not(denom_min >= _MIN_DENOM))
    def _():
        build_queries(False)
        m_sc[...] = jnp.full_like(m_sc, NEG_INF)
        acc_sc[...] = jnp.zeros_like(acc_sc)

        def chunk(j, carry):
            scores(j, sa_sc)
            diag_bias(j, sa_sc)
            adj = jnp.where(j > i, off, jnp.where(j < i, -off, 0.0))
            start = pl.multiple_of(j * tk, tk)
            s = sa_sc[...]
            m_prev = m_sc[...]
            m_new = jnp.maximum(m_prev, jnp.max(s, axis=-1, keepdims=True) + adj)
            p = jnp.exp2(s - (m_new - adj)).astype(BF16)
            acc_sc[...] = jnp.exp2(m_prev - m_new) * acc_sc[...] + _dot(p, va_sc[pl.ds(start, tk), :])
            m_sc[...] = m_new
            return carry

        lax.fori_loop(0, nk, chunk, 0)

    lp = lam_ref[...]
    lam = (jnp.exp(jnp.sum(lp[0:1] * lp[1:2], axis=-1, keepdims=True))
           - jnp.exp(jnp.sum(lp[2:3] * lp[3:4], axis=-1, keepdims=True)) + lambda_init)
    acc = acc_sc[...]
    o0 = acc[:tq, :B_VDIM] * (1.0 / acc[:tq, B_VDIM:B_VDIM + 1])
    o1 = acc[tq:, :B_VDIM] * (1.0 / acc[tq:, B_VDIM:B_VDIM + 1])
    o_ref[rows, :] = (_rms(o0 - lam * o1, g_ref[...]) * out_scale).astype(o_ref.dtype)


def _bf16_pieces(x, n=3):
    out = []
    r = np.float64(x)
    for _ in range(n):
        p = np.float64(np.float32(r).astype(jnp.bfloat16).astype(np.float32))
        out.append(p)
        r = r - p
    return out


def _alibi_features(slopes2, t):
    pos = np.arange(t)
    hi = (pos // _POS_SPLIT) * _POS_SPLIT
    lo = pos % _POS_SPLIT
    qf = np.zeros((len(slopes2), t, LANES), np.float32)
    kf = np.zeros((len(slopes2), t, LANES), np.float32)
    for h, s in enumerate(slopes2):
        for n, piece in enumerate(_bf16_pieces(s)):
            for base, part in ((0, hi), (3, lo)):
                qf[h, :, base + n] = -piece
                kf[h, :, base + n] = part
                qf[h, :, 6 + base + n] = part
                kf[h, :, 6 + base + n] = piece
        for n, piece in enumerate(_bf16_pieces(s * t)):
            qf[h, :, _CHUNK_LANE + n] = -piece
        kf[h, :, _BOUND_LANE:_BOUND_LANE + 3] = -1.0
        kf[h, :, _OFFSET_LANE:_OFFSET_LANE + 3] = -1.0
    return jnp.asarray(qf, BF16), jnp.asarray(kf, BF16)


def _diff_attention(proj, b_lambda, subln_g, lambda_init, batch, seq, t=512):
    t = min(t, seq // 2)
    nq = seq // t
    assert seq % (2 * t) == 0
    kcol = MIX_WIDTH // B_VDIM
    vcol = 2 * kcol
    slopes2 = [s * LOG2E for s in _alibi_slopes(B_HEADS)]
    reach = [min(nq, int(math.floor((_ZERO_EXP2_ARG / s - 1.0) / t)) + 1) for s in slopes2]
    qf, kf = _alibi_features(slopes2, t)
    pos = np.arange(t)
    dist = jnp.asarray(np.abs(pos[:, None] - pos[None, :]), F32)
    kernel = functools.partial(_diff_attn_kernel, tk=t, out_scale=1.0 - lambda_init,
                               lambda_init=lambda_init)
    return pl.pallas_call(
        kernel,
        grid_spec=pltpu.PrefetchScalarGridSpec(
            num_scalar_prefetch=2,
            grid=(batch, B_HEADS),
            in_specs=[pl.BlockSpec((seq, B_VDIM), lambda b, h, r, s: (b, h)),
                      pl.BlockSpec((seq, B_VDIM), lambda b, h, r, s: (b, kcol + h)),
                      pl.BlockSpec((seq, B_VDIM), lambda b, h, r, s: (b, vcol + h)),
                      pl.BlockSpec((None, t, LANES), lambda b, h, r, s: (h, 0, 0)),
                      pl.BlockSpec((None, t, LANES), lambda b, h, r, s: (h, 0, 0)),
                      pl.BlockSpec((t, t), lambda b, h, r, s: (0, 0)),
                      pl.BlockSpec((4, HEAD_DIM), lambda b, h, r, s: (0, 0)),
                      pl.BlockSpec((1, B_VDIM), lambda b, h, r, s: (0, 0))],
            out_specs=pl.BlockSpec((seq, B_VDIM), lambda b, h, r, s: (b, h)),
            scratch_shapes=[pltpu.VMEM((3, 2 * t, 2 * LANES), BF16),
                            pltpu.VMEM((seq, 2 * LANES), BF16),
                            pltpu.VMEM((seq, 2 * LANES), BF16),
                            pltpu.VMEM((2, 8, LANES), F32),
                            pltpu.VMEM((2 * t, 1), F32),
                            pltpu.VMEM((2 * t, 2 * LANES), F32),
                            pltpu.VMEM((2 * t, t), F32),
                            pltpu.VMEM((2 * t, t), BF16),
                            pltpu.VMEM((2 * t, t), BF16),
                            pltpu.VMEM((2 * LANES, seq), BF16)],
        ),
        out_shape=jax.ShapeDtypeStruct((batch * seq, MIX_WIDTH), BF16),
        compiler_params=_params("parallel", "parallel"),
        name="diff_attention",
    )(jnp.asarray(reach, jnp.int32), jnp.asarray(slopes2, F32), proj, proj, proj, qf, kf, dist,
      b_lambda, subln_g.reshape(1, B_VDIM))


def _router_kernel(x_ref, g_ref, whi_ref, wlo_ref, h_ref, e_ref, gate_ref):
    h = _rms(x_ref[...], g_ref[...])
    h_hi = h.astype(BF16)
    h_lo = (h - h_hi.astype(F32)).astype(BF16)
    whi = whi_ref[...]
    logits = _dot(h_hi, whi) + (_dot(h_hi, wlo_ref[...]) + _dot(h_lo, whi))
    lane = lax.broadcasted_iota(jnp.int32, logits.shape, 1)
    logits = jnp.where(lane < N_EXPERTS, logits, -jnp.inf)
    m1 = jnp.max(logits, axis=-1, keepdims=True)
    i1 = jnp.min(jnp.where(logits == m1, lane, LANES), axis=-1, keepdims=True)
    rest = jnp.where(lane == i1, -jnp.inf, logits)
    m2 = jnp.max(rest, axis=-1, keepdims=True)
    i2 = jnp.min(jnp.where(rest == m2, lane, LANES), axis=-1, keepdims=True)
    r = jnp.exp(m2 - m1)
    g1 = 1.0 / (1.0 + r)
    g2 = r * g1
    e_ref[...] = jnp.where(lane == 0, i1, jnp.where(lane == 1, i2, 0))
    gate_ref[...] = jnp.where(lane == 0, g1, jnp.where(lane == 1, g2, 0.0))
    h_ref[...] = h_hi


def _router(x, g, w_router, tm=2048):
    t, d = x.shape
    tm = min(tm, t)
    wpad = jnp.zeros((d, LANES), F32).at[:, :N_EXPERTS].set(w_router)
    whi = wpad.astype(BF16)
    wlo = (wpad - whi.astype(F32)).astype(BF16)
    return pl.pallas_call(
        _router_kernel,
        grid=(t // tm,),
        in_specs=[pl.BlockSpec((tm, d), lambda i: (i, 0)),
                  pl.BlockSpec((1, d), lambda i: (0, 0)),
                  pl.BlockSpec((d, LANES), lambda i: (0, 0)),
                  pl.BlockSpec((d, LANES), lambda i: (0, 0))],
        out_specs=[pl.BlockSpec((tm, d), lambda i: (i, 0)),
                   pl.BlockSpec((tm, LANES), lambda i: (i, 0)),
                   pl.BlockSpec((tm, LANES), lambda i: (i, 0))],
        out_shape=[jax.ShapeDtypeStruct((t, d), BF16),
                   jax.ShapeDtypeStruct((t, LANES), jnp.int32),
                   jax.ShapeDtypeStruct((t, LANES), F32)],
        compiler_params=_params("parallel"),
        name="router",
    )(x, g.reshape(1, d), whi, wlo)


def _slab_sizes(rows, tc):
    quarter, half = tc // 4, tc // 2
    return ((quarter, rows <= quarter),
            (half, jnp.logical_and(rows > quarter, rows <= half)),
            (tc, rows > half))


def _slab_stride(slab_ref, n, step, tc):
    most = slab_ref[(n + step) * N_EXPERTS]
    for e in range(1, N_EXPERTS):
        most = jnp.maximum(most, slab_ref[(n + step) * N_EXPERTS + e])
    return jnp.where(most <= tc // 2, tc // 2, tc)


def _dispatch_kernel(slab_ref, fill_ref, h_ref, dest_ref, gate_ref, xs_hbm, gs_hbm,
                     xslab_sc, gslab_sc, zx_sc, zg_sc, sem, zsem):
    i = pl.program_id(0)
    n = pl.num_programs(0)
    slot = i % 2
    tc = h_ref.shape[0]
    tm = zx_sc.shape[0]

    def slab_dma(step, sl, op):
        stride = _slab_stride(slab_ref, n, step, tc)
        for e in range(N_EXPERTS):
            start = pl.multiple_of(slab_ref[step * N_EXPERTS + e], SUBLANES)
            first = pl.multiple_of(e * stride, SUBLANES)
            for size, cond in _slab_sizes(slab_ref[(n + step) * N_EXPERTS + e], tc):
                @pl.when(cond)
                def _():
                    for src, dst in ((xslab_sc, xs_hbm), (gslab_sc, gs_hbm)):
                        copy = pltpu.make_async_copy(src.at[sl, pl.ds(first, size)],
                                                     dst.at[pl.ds(start, size)], sem.at[sl])
                        getattr(copy, op)()

    def fill_copies(start):
        start = pl.multiple_of(start, SUBLANES)
        return [pltpu.make_async_copy(zx_sc, xs_hbm.at[pl.ds(start, tm)], zsem),
                pltpu.make_async_copy(zg_sc, gs_hbm.at[pl.ds(start, tm)], zsem)]

    @pl.when(i == 0)
    def _():
        zx_sc[...] = jnp.zeros_like(zx_sc)
        zg_sc[...] = jnp.zeros_like(zg_sc)
        for half in range(2):
            for e in range(N_EXPERTS):
                for c in fill_copies(fill_ref[half * N_EXPERTS + e]):
                    c.start()
            for e in range(N_EXPERTS):
                for c in fill_copies(fill_ref[half * N_EXPERTS + e]):
                    c.wait()
        first_free = fill_ref[2 * N_EXPERTS]
        n_tiles = xs_hbm.shape[0] // tm

        def start_tile(b, carry):
            for c in fill_copies(b * tm):
                c.start()
            return carry

        def wait_tile(b, carry):
            for c in fill_copies(b * tm):
                c.wait()
            return carry

        lax.fori_loop(first_free, n_tiles, start_tile, 0)
        lax.fori_loop(first_free, n_tiles, wait_tile, 0)

    dest = dest_ref[0]
    gate = gate_ref[...]
    lane = lax.broadcasted_iota(jnp.int32, gate.shape, 1)

    def gate_pieces(col):
        out = jnp.zeros(gate.shape, F32)
        for n_, piece in enumerate(_bf16_split(col)):
            out = jnp.where(lane == n_, piece.astype(F32), out)
        return out.astype(BF16)

    def place_rows(stride):
        pos = lax.broadcasted_iota(jnp.int32, (stride, tc), 0)
        sel0, sel1 = [], []
        for e in range(N_EXPERTS):
            start = slab_ref[i * N_EXPERTS + e]
            sel0.append(jnp.where(dest[0:1, :] - start == pos, 1.0, 0.0))
            sel1.append(jnp.where(dest[1:2, :] - start == pos, 1.0, 0.0))
        sel0 = jnp.concatenate(sel0, axis=0).astype(BF16)
        sel1 = jnp.concatenate(sel1, axis=0).astype(BF16)
        rows = pl.ds(0, N_EXPERTS * stride)
        xslab_sc[slot, rows] = _dot(sel0 + sel1, h_ref[...])
        gslab_sc[slot, rows] = (_dot(sel0, gate_pieces(gate[:, 0:1]))
                                + _dot(sel1, gate_pieces(gate[:, 1:2])))

    compact = _slab_stride(slab_ref, n, i, tc) < tc
    pl.when(compact)(lambda: place_rows(tc // 2))
    pl.when(jnp.logical_not(compact))(lambda: place_rows(tc))

    @pl.when(i > 0)
    def _():
        slab_dma(i - 1, 1 - slot, "wait")

    slab_dma(i, slot, "start")

    @pl.when(i == n - 1)
    def _():
        slab_dma(i, slot, "wait")


def _dispatch(h, dest_t, gates, slab, fill, p_rows, tm, tc):
    t, d = h.shape
    nb = t // tc
    return pl.pallas_call(
        _dispatch_kernel,
        grid_spec=pltpu.PrefetchScalarGridSpec(
            num_scalar_prefetch=2,
            grid=(nb,),
            in_specs=[pl.BlockSpec((tc, d), lambda i, s, z: (i, 0)),
                      pl.BlockSpec((1, TOP_K, tc), lambda i, s, z: (i, 0, 0)),
                      pl.BlockSpec((tc, LANES), lambda i, s, z: (i, 0))],
            out_specs=[pl.BlockSpec(memory_space=pl.ANY), pl.BlockSpec(memory_space=pl.ANY)],
            scratch_shapes=[pltpu.VMEM((2, N_EXPERTS * tc, d), F32),
                            pltpu.VMEM((2, N_EXPERTS * tc, LANES), F32),
                            pltpu.VMEM((tm, d), F32), pltpu.VMEM((tm, LANES), F32),
                            pltpu.SemaphoreType.DMA((2,)), pltpu.SemaphoreType.DMA(())],
        ),
        out_shape=[jax.ShapeDtypeStruct((p_rows, d), F32), jax.ShapeDtypeStruct((p_rows, LANES), F32)],
        compiler_params=_params("arbitrary"),
        name="moe_dispatch",
    )(slab, fill, h, dest_t, gates)


def _moe_kernel(blk_e_ref, used_ref, xblk_ref, x_ref, gs_ref, wg_ref, wu_ref, wd_ref, o_ref,
                xb_sc, acc_sc):
    i = pl.program_id(0)
    f = pl.program_id(1)
    used = used_ref[i] > 0

    @pl.when(jnp.logical_and(used, f == 0))
    def _():
        xb_sc[...] = x_ref[...].astype(BF16)
        acc_sc[...] = jnp.zeros_like(acc_sc)

    @pl.when(used)
    def _():
        acc_sc[...] += _swiglu_tile(xb_sc[...], wg_ref, wu_ref, wd_ref)

    @pl.when(f == pl.num_programs(1) - 1)
    def _():
        gs = gs_ref[...]
        gate = gs[:, 0:1] + gs[:, 1:2] + gs[:, 2:3]
        o_ref[...] = jnp.where(used, acc_sc[...] * gate, 0.0)


def _moe_experts(xs, gs, blk_e, blk_used, xblk, wg, wu, wd, tm, tf=1792):
    p, d = xs.shape
    nblk = p // tm
    nf = wg.shape[2] // tf

    def fidx(i, f, used_ref):
        return jnp.where(used_ref[i] > 0, f, nf - 1)

    return pl.pallas_call(
        _moe_kernel,
        grid_spec=pltpu.PrefetchScalarGridSpec(
            num_scalar_prefetch=3,
            grid=(nblk, nf),
            in_specs=[pl.BlockSpec((tm, d), lambda i, f, be, us, xb: (xb[i], 0)),
                      pl.BlockSpec((tm, LANES), lambda i, f, be, us, xb: (xb[i], 0)),
                      pl.BlockSpec((None, d, tf), lambda i, f, be, us, xb: (be[i], 0, fidx(i, f, us))),
                      pl.BlockSpec((None, d, tf), lambda i, f, be, us, xb: (be[i], 0, fidx(i, f, us))),
                      pl.BlockSpec((None, tf, d), lambda i, f, be, us, xb: (be[i], fidx(i, f, us), 0))],
            out_specs=pl.BlockSpec((tm, d), lambda i, f, be, us, xb: (i, 0)),
            scratch_shapes=[pltpu.VMEM((tm, d), BF16), pltpu.VMEM((tm, d), F32)],
        ),
        out_shape=jax.ShapeDtypeStruct((p, d), F32),
        compiler_params=_params("arbitrary", "arbitrary"),
        name="moe_experts",
    )(blk_e, blk_used, xblk, xs, gs, wg, wu, wd)


def _combine_kernel(slab_ref, ys_hbm, dest_ref, x_ref, g_ref, o_ref, slab_sc, sem):
    i = pl.program_id(0)
    n = pl.num_programs(0)
    slot = i % 2
    tc = x_ref.shape[0]

    def slab_dma(step, sl, op):
        stride = _slab_stride(slab_ref, n, step, tc)
        for e in range(N_EXPERTS):
            start = pl.multiple_of(slab_ref[step * N_EXPERTS + e], SUBLANES)
            first = pl.multiple_of(e * stride, SUBLANES)
            for size, cond in _slab_sizes(slab_ref[(n + step) * N_EXPERTS + e], tc):
                @pl.when(cond)
                def _():
                    copy = pltpu.make_async_copy(ys_hbm.at[pl.ds(start, size)],
                                                 slab_sc.at[sl, pl.ds(first, size)], sem.at[sl])
                    getattr(copy, op)()

    @pl.when(i == 0)
    def _():
        slab_sc[...] = jnp.zeros_like(slab_sc)
        slab_dma(0, 0, "start")

    @pl.when(i + 1 < n)
    def _():
        slab_dma(i + 1, 1 - slot, "start")

    dest = dest_ref[...]
    slab_dma(i, slot, "wait")

    def gather_rows(stride):
        pos = lax.broadcasted_iota(jnp.int32, (tc, stride), 1)
        sel = []
        for e in range(N_EXPERTS):
            start = slab_ref[i * N_EXPERTS + e]
            hit = jnp.logical_or(dest[:, 0:1] - start == pos, dest[:, 1:2] - start == pos)
            sel.append(jnp.where(hit, 1.0, 0.0))
        sel = jnp.concatenate(sel, axis=1).astype(BF16)
        y = _dot(sel, slab_sc[slot, pl.ds(0, N_EXPERTS * stride)].astype(BF16))
        o_ref[...] = x_ref[...] + _rms(y, g_ref[...])

    compact = _slab_stride(slab_ref, n, i, tc) < tc
    pl.when(compact)(lambda: gather_rows(tc // 2))
    pl.when(jnp.logical_not(compact))(lambda: gather_rows(tc))


def _combine(ys, dest, slab, x, g, tc):
    t, d = x.shape
    return pl.pallas_call(
        _combine_kernel,
        grid_spec=pltpu.PrefetchScalarGridSpec(
            num_scalar_prefetch=1,
            grid=(t // tc,),
            in_specs=[pl.BlockSpec(memory_space=pl.ANY),
                      pl.BlockSpec((tc, LANES), lambda i, s: (i, 0)),
                      pl.BlockSpec((tc, d), lambda i, s: (i, 0)),
                      pl.BlockSpec((1, d), lambda i, s: (0, 0))],
            out_specs=pl.BlockSpec((tc, d), lambda i, s: (i, 0)),
            scratch_shapes=[pltpu.VMEM((2, N_EXPERTS * tc, d), F32), pltpu.SemaphoreType.DMA((2,))],
        ),
        out_shape=jax.ShapeDtypeStruct((t, d), F32),
        compiler_params=_params("arbitrary"),
        name="moe_combine",
    )(slab, ys, dest, x, g.reshape(1, d))


def _route(top_e, tm, tc):
    t = top_e.shape[0]
    a = t * TOP_K
    nb = t // tc
    e_flat = top_e.reshape(a)
    onehot = (e_flat[:, None] == jnp.arange(N_EXPERTS, dtype=jnp.int32)[None, :]).astype(jnp.int32)
    csum = jnp.cumsum(onehot, axis=0)
    rank = jnp.sum(onehot * (csum - 1), axis=1)
    upto = csum[TOP_K * tc - 1::TOP_K * tc]
    before = jnp.concatenate([jnp.zeros((1, N_EXPERTS), jnp.int32), upto[:nb - 1]], axis=0)
    rows = (upto - before + SUBLANES - 1) // SUBLANES * SUBLANES
    rows_before = jnp.cumsum(rows, axis=0) - rows
    total = jnp.sum(rows, axis=0)
    region = (total + 2 * tm - 1) // tm * tm
    rends = jnp.cumsum(region)
    rstarts = rends - region
    slab = (rstarts[None, :] + rows_before).astype(jnp.int32)
    shift = jnp.broadcast_to((slab - before)[:, None, :], (nb, TOP_K * tc, N_EXPERTS)).reshape(a, N_EXPERTS)
    dest = (rank + jnp.sum(onehot * shift, axis=1)).astype(jnp.int32).reshape(t, TOP_K)
    slab = jnp.concatenate([slab.reshape(nb * N_EXPERTS), rows.reshape(nb * N_EXPERTS)])
    row_end = rstarts + total
    fill = jnp.concatenate([row_end, rends - tm, rends[-1:] // tm]).astype(jnp.int32)
    nblk = -(-(a + nb * N_EXPERTS * (SUBLANES - 1)) // tm) + 2 * N_EXPERTS
    blk_start = jnp.arange(nblk, dtype=jnp.int32) * tm
    blk_e = jnp.minimum(jnp.sum((rends[None, :] <= blk_start[:, None]).astype(jnp.int32), axis=1),
                        N_EXPERTS - 1)
    blk_used = jnp.logical_and(blk_start < row_end[blk_e], blk_start < rends[-1]).astype(jnp.int32)
    first_used = jnp.argmax(blk_used).astype(jnp.int32)
    xblk = jnp.where(blk_used > 0, jnp.arange(nblk, dtype=jnp.int32), first_used)
    return dest, slab, fill, blk_e, blk_used, xblk, nblk * tm


def _moe(x, h, e_pad, gates, g_out, wg, wu, wd, tm=512, tc=256):
    t = x.shape[0]
    tc = min(tc, t)
    dest, slab, fill, blk_e, blk_used, xblk, p_rows = _route(e_pad[:, :TOP_K], tm, tc)
    dest_t = dest.reshape(t // tc, tc, TOP_K).transpose(0, 2, 1)
    dest_pad = jnp.zeros((t, LANES), jnp.int32).at[:, :TOP_K].set(dest)
    xs, gs = _dispatch(h, dest_t, gates, slab, fill, p_rows, tm, tc)
    ys = _moe_experts(xs, gs, blk_e, blk_used, xblk, wg, wu, wd, tm)
    return _combine(ys, dest_pad, slab, x, g_out, tc)


def kernel(x, mem, norm_g, mem_norm_g, w_mem_kv, a_w_in, a_sink, a_w_out, b_w_in, b_lambda,
           b_subln_g, b_w_out, ffn_w_gate, ffn_w_up, ffn_w_down, moe_w_router, moe_w_gate,
           moe_w_up, moe_w_down):
    batch, seq, d = x.shape
    n_mem = mem.shape[1]
    bf = lambda w: w.astype(BF16)
    xt = x.reshape(batch * seq, d)
    mkv = _norm_matmul(mem.reshape(batch * n_mem, d), mem_norm_g, bf(w_mem_kv),
                       jnp.ones((w_mem_kv.shape[1],), F32))

    g = norm_g[0]
    proj = _norm_matmul(xt, g[0], bf(a_w_in[0]), _query_col_scale(a_w_in.shape[2], A_Q_HEADS * HEAD_DIM))
    mix = _window_attention(proj, a_sink[0], batch, seq)
    mem_out = _mem_attention(proj, (A_Q_HEADS + 2 * A_KV_HEADS) * HEAD_DIM // MEM_WIDTH, mkv, batch, seq)
    xt = _out_proj(mix, mem_out, bf(a_w_out[0]), xt, g[1])
    xt = _ffn(xt, g[2], g[3], bf(ffn_w_gate[0]), bf(ffn_w_up[0]), bf(ffn_w_down[0]))

    g = norm_g[1]
    lambda_init = 0.8 - 0.6 * math.exp(-0.3 * 1)
    proj = _norm_matmul(xt, g[0], bf(b_w_in[0]), _query_col_scale(b_w_in.shape[2], MIX_WIDTH))
    mix = _diff_attention(proj, b_lambda[0], b_subln_g[0], lambda_init, batch, seq)
    mem_out = _mem_attention(proj, 3 * MIX_WIDTH // MEM_WIDTH, mkv, batch, seq)
    xt = _out_proj(mix, mem_out, bf(b_w_out[0]), xt, g[1])
    h, e_pad, gates = _router(xt, g[2], moe_w_router[0])
    xt = _moe(xt, h, e_pad, gates, g[3], bf(moe_w_gate[0]), bf(moe_w_up[0]), bf(moe_w_down[0]))
    return xt.reshape(batch, seq, d)
```

```python
import functools
import math

import jax
import jax.numpy as jnp
import numpy as np
from jax import lax
from jax.experimental import pallas as pl
from jax.experimental.pallas import tpu as pltpu

D_MODEL = 1024
HEAD_DIM = 64
MIX_WIDTH = 768
MEM_WIDTH = 256
MEM_HEADS = 4
A_Q_HEADS = 12
A_KV_HEADS = 4
A_GROUP = 3
WINDOW = 128
B_HEADS = 6
B_VDIM = 128
D_FF = 3584
N_EXPERTS = 8
TOP_K = 2
EPS = 1e-6
NEG_INF = -1e30
LOG2E = math.log2(math.e)
QK_SCALE = HEAD_DIM ** -0.5 * LOG2E
LANES = 128
SUBLANES = 8
VMEM_LIMIT_BYTES = 56 * 1024 * 1024

BF16 = jnp.bfloat16
F32 = jnp.float32


def _params(*sem):
    return pltpu.CompilerParams(dimension_semantics=sem, vmem_limit_bytes=VMEM_LIMIT_BYTES)


def _rms(x, g):
    return x * lax.rsqrt(jnp.mean(x * x, axis=-1, keepdims=True) + EPS) * g


def _dot(a, b):
    return jnp.dot(a, b, preferred_element_type=F32)


def _dot_nt(a, b):
    return lax.dot_general(a, b, (((1,), (1,)), ((), ())), preferred_element_type=F32)


def _alibi_slopes(n):
    return [2.0 ** (-8.0 * (i + 1) / n) for i in range(n)]


def _norm_matmul_kernel(x_ref, g_ref, w_ref, cs_ref, o_ref, *, chunk):
    h = _rms(x_ref[...], g_ref[...]).astype(BF16)
    for c in range(o_ref.shape[1] // chunk):
        sl = slice(c * chunk, (c + 1) * chunk)
        o_ref[:, sl] = (_dot(h, w_ref[:, sl]) * cs_ref[:, sl]).astype(o_ref.dtype)


def _norm_matmul(x, g, w, col_scale, tm=1024, chunk=512):
    t, d = x.shape
    n = w.shape[1]
    tm = min(tm, t)
    return pl.pallas_call(
        functools.partial(_norm_matmul_kernel, chunk=min(chunk, n)),
        grid=(t // tm,),
        in_specs=[pl.BlockSpec((tm, d), lambda i: (i, 0)),
                  pl.BlockSpec((1, d), lambda i: (0, 0)),
                  pl.BlockSpec((d, n), lambda i: (0, 0)),
                  pl.BlockSpec((1, n), lambda i: (0, 0))],
        out_specs=pl.BlockSpec((tm, n), lambda i: (i, 0)),
        out_shape=jax.ShapeDtypeStruct((t, n), BF16),
        compiler_params=_params("parallel"),
        name="norm_matmul",
    )(x, g.reshape(1, d), w, col_scale.reshape(1, n))


def _query_col_scale(n, mix_q_cols):
    cols = np.arange(n)
    return jnp.asarray(np.where((cols < mix_q_cols) | (cols >= n - MEM_WIDTH), QK_SCALE, 1.0), F32)


def _window_kernel(q_ref, kp_ref, kc_ref, kn_ref, vp_ref, vc_ref, vn_ref, bias_ref, o_ref):
    n = pl.program_id(1)
    nb = pl.num_programs(1)
    blk = kp_ref.shape[0]
    sub = q_ref.shape[0] // blk
    zeros = jnp.zeros((blk, kp_ref.shape[1]), kp_ref.dtype)
    k_all = jnp.concatenate([kp_ref[...], kc_ref[...], kn_ref[...]], axis=0)
    v_all = jnp.concatenate([vp_ref[...], vc_ref[...], vn_ref[...]], axis=0)
    col = lax.broadcasted_iota(jnp.int32, (1, 4 * blk), 1)
    half = lax.broadcasted_iota(jnp.int32, (4 * blk, LANES), 1) < HEAD_DIM
    low_half = jnp.where(half, 1.0, 0.0).astype(BF16)
    high_half = jnp.where(half, 0.0, 1.0).astype(BF16)
    for u in range(sub):
        k = jnp.concatenate([k_all[u * blk:(u + 3) * blk], zeros], axis=0)
        v = jnp.concatenate([v_all[u * blk:(u + 3) * blk], zeros], axis=0)
        edge = jnp.zeros((1, 4 * blk), F32)
        if u == 0:
            edge = jnp.where(jnp.logical_and(n == 0, col < blk), NEG_INF, edge)
        if u == sub - 1:
            edge = jnp.where(jnp.logical_and(n == nb - 1,
                                             jnp.logical_and(col >= 2 * blk, col < 3 * blk)), NEG_INF, edge)
        q = q_ref[u * blk:(u + 1) * blk, :]
        outs = []
        for kh in range(A_KV_HEADS):
            heads = range(kh * A_GROUP, (kh + 1) * A_GROUP)
            kv = slice(kh * HEAD_DIM, (kh + 1) * HEAD_DIM)
            qs = jnp.concatenate([q[:, h * HEAD_DIM:(h + 1) * HEAD_DIM] for h in heads], axis=0)
            s = _dot_nt(qs, k[:, kv]) + bias_ref[kh] + edge
            p = jnp.exp2(s - jnp.max(s, axis=-1, keepdims=True)).astype(BF16)
            low = kh % 2 == 0
            slab = v[:, (kh // 2) * LANES:(kh // 2 + 1) * LANES]
            own, other = (low_half, high_half) if low else (high_half, low_half)
            ov = _dot(p, jnp.concatenate([slab * own + other, own], axis=1))
            den = ov[:, LANES:]
            o = ov[:, :LANES] * (1.0 / jnp.where(den == 0.0, 1.0, den))
            o = o[:, :HEAD_DIM] if low else o[:, HEAD_DIM:]
            outs.extend(o[g * blk:(g + 1) * blk] for g in range(A_GROUP))
        o_ref[u * blk:(u + 1) * blk, :] = jnp.concatenate(outs, axis=-1).astype(o_ref.dtype)


def _window_bias(blk, sink):
    qi = np.arange(blk)[:, None]
    kj = np.arange(3 * blk)[None, :]
    dist = np.abs(blk + qi - kj)
    slopes = _alibi_slopes(A_Q_HEADS)
    tables = np.stack([np.where(dist <= WINDOW, -(slopes[h] * LOG2E) * dist, NEG_INF)
                       for h in range(A_Q_HEADS)])
    pad = jnp.full((A_Q_HEADS, blk, blk), NEG_INF, F32)
    pad = pad.at[:, :, 0].set(jnp.broadcast_to((sink.astype(F32) * LOG2E)[:, None], (A_Q_HEADS, blk)))
    full = jnp.concatenate([jnp.asarray(tables, F32), pad], axis=2)
    return full.reshape(A_KV_HEADS, A_GROUP * blk, 4 * blk)


def _window_attention(proj, sink, batch, seq, sub=4):
    blk = WINDOW
    nb = seq // blk
    sub = min(sub, nb)
    ns = nb // sub
    kcol = MIX_WIDTH // 256
    vcol = kcol + 1
    bias = _window_bias(blk, sink)

    def cur(col):
        return pl.BlockSpec((sub * blk, 256), lambda b, n: (b * ns + n, col))

    def prev(col):
        return pl.BlockSpec((blk, 256), lambda b, n: (b * nb + jnp.maximum(n * sub - 1, 0), col))

    def nxt(col):
        return pl.BlockSpec((blk, 256), lambda b, n: (b * nb + jnp.minimum((n + 1) * sub, nb - 1), col))

    return pl.pallas_call(
        _window_kernel,
        grid=(batch, ns),
        in_specs=[pl.BlockSpec((sub * blk, MIX_WIDTH), lambda b, n: (b * ns + n, 0)),
                  prev(kcol), cur(kcol), nxt(kcol), prev(vcol), cur(vcol), nxt(vcol),
                  pl.BlockSpec(bias.shape, lambda b, n: (0, 0, 0))],
        out_specs=pl.BlockSpec((sub * blk, MIX_WIDTH), lambda b, n: (b * ns + n, 0)),
        out_shape=jax.ShapeDtypeStruct((batch * seq, MIX_WIDTH), BF16),
        compiler_params=_params("parallel", "parallel"),
        name="window_attention",
    )(proj, proj, proj, proj, proj, proj, proj, bias)


def _mem_attn_kernel(q_ref, mkv_ref, o_ref):
    q = q_ref[...]
    mkv = mkv_ref[...]
    ones = jnp.ones((mkv.shape[0], LANES), BF16)
    outs = []
    for h in range(MEM_HEADS):
        sl = slice(h * HEAD_DIM, (h + 1) * HEAD_DIM)
        s = _dot_nt(q[:, sl], mkv[:, sl])
        p = jnp.exp2(s - jnp.max(s, axis=-1, keepdims=True)).astype(BF16)
        o = _dot(p, mkv[:, MEM_WIDTH + h * HEAD_DIM:MEM_WIDTH + (h + 1) * HEAD_DIM])
        outs.append(o * (1.0 / _dot(p, ones)[:, :HEAD_DIM]))
    o_ref[...] = jnp.concatenate(outs, axis=-1).astype(o_ref.dtype)


def _mem_attention(proj, qcol, mkv, batch, seq, tm=2048):
    tm = min(tm, seq)
    per_batch = seq // tm
    n_mem = mkv.shape[0] // batch
    return pl.pallas_call(
        _mem_attn_kernel,
        grid=(batch * per_batch,),
        in_specs=[pl.BlockSpec((tm, MEM_WIDTH), lambda i: (i, qcol)),
                  pl.BlockSpec((n_mem, 2 * MEM_WIDTH), lambda i: (i // per_batch, 0))],
        out_specs=pl.BlockSpec((tm, MEM_WIDTH), lambda i: (i, 0)),
        out_shape=jax.ShapeDtypeStruct((batch * seq, MEM_WIDTH), BF16),
        compiler_params=_params("parallel"),
        name="mem_attention",
    )(proj, mkv)


def _out_proj_kernel(mix_ref, mem_ref, w1_ref, w2_ref, x_ref, g_ref, o_ref):
    o = _dot(mix_ref[...], w1_ref[...]) + _dot(mem_ref[...], w2_ref[...])
    o_ref[...] = x_ref[...] + _rms(o, g_ref[...])


def _out_proj(mix, mem_out, w_out, x, g, tm=1024):
    t, d = x.shape
    tm = min(tm, t)
    w1 = w_out[:MIX_WIDTH]
    w2 = w_out[MIX_WIDTH:]
    return pl.pallas_call(
        _out_proj_kernel,
        grid=(t // tm,),
        in_specs=[pl.BlockSpec((tm, MIX_WIDTH), lambda i: (i, 0)),
                  pl.BlockSpec((tm, MEM_WIDTH), lambda i: (i, 0)),
                  pl.BlockSpec((MIX_WIDTH, d), lambda i: (0, 0)),
                  pl.BlockSpec((MEM_WIDTH, d), lambda i: (0, 0)),
                  pl.BlockSpec((tm, d), lambda i: (i, 0)),
                  pl.BlockSpec((1, d), lambda i: (0, 0))],
        out_specs=pl.BlockSpec((tm, d), lambda i: (i, 0)),
        out_shape=jax.ShapeDtypeStruct((t, d), F32),
        compiler_params=_params("parallel"),
        name="out_proj",
    )(mix, mem_out, w1, w2, x, g.reshape(1, d))


def _swiglu_tile(h, wg_ref, wu_ref, wd_ref, chunk=256):
    assert wg_ref.shape[1] % chunk == 0
    out = None
    for c in range(wg_ref.shape[1] // chunk):
        sl = slice(c * chunk, (c + 1) * chunk)
        a = _dot(h, wg_ref[:, sl])
        u = _dot(h, wu_ref[:, sl])
        z = (a * jax.nn.sigmoid(a) * u).astype(BF16)
        y = _dot(z, wd_ref[sl, :])
        out = y if out is None else out + y
    return out


def _ffn_kernel(x_ref, gin_ref, gout_ref, wg_ref, wu_ref, wd_ref, o_ref, h_sc, acc_sc):
    f = pl.program_id(1)

    @pl.when(f == 0)
    def _():
        h_sc[...] = _rms(x_ref[...], gin_ref[...]).astype(BF16)
        acc_sc[...] = jnp.zeros_like(acc_sc)

    acc_sc[...] += _swiglu_tile(h_sc[...], wg_ref, wu_ref, wd_ref)

    @pl.when(f == pl.num_programs(1) - 1)
    def _():
        o_ref[...] = x_ref[...] + _rms(acc_sc[...], gout_ref[...])


def _ffn(x, g_in, g_out, wg, wu, wd, tm=1024, tf=1792):
    t, d = x.shape
    ff = wg.shape[1]
    tm = min(tm, t)
    return pl.pallas_call(
        _ffn_kernel,
        grid=(t // tm, ff // tf),
        in_specs=[pl.BlockSpec((tm, d), lambda i, f: (i, 0)),
                  pl.BlockSpec((1, d), lambda i, f: (0, 0)),
                  pl.BlockSpec((1, d), lambda i, f: (0, 0)),
                  pl.BlockSpec((d, tf), lambda i, f: (0, f)),
                  pl.BlockSpec((d, tf), lambda i, f: (0, f)),
                  pl.BlockSpec((tf, d), lambda i, f: (f, 0))],
        out_specs=pl.BlockSpec((tm, d), lambda i, f: (i, 0)),
        out_shape=jax.ShapeDtypeStruct((t, d), F32),
        scratch_shapes=[pltpu.VMEM((tm, d), BF16), pltpu.VMEM((tm, d), F32)],
        compiler_params=_params("parallel", "arbitrary"),
        name="ffn",
    )(x, g_in.reshape(1, d), g_out.reshape(1, d), wg, wu, wd)


_POS_SPLIT = 256
_LEFT, _DIAG, _RIGHT = 0, 1, 2
_BOUND_LANE = 12
_CHUNK_LANE = 15
_OFFSET_LANE = 18
_ZERO_EXP2_ARG = 136.0
_MIN_DENOM = 2.0 ** -60


def _bf16_split(x):
    hi = x.astype(BF16)
    r = x - hi.astype(F32)
    mid = r.astype(BF16)
    lo = (r - mid.astype(F32)).astype(BF16)
    return hi, mid, lo


def _diff_attn_kernel(reach_ref, slope_ref, q_ref, k_ref, v_ref, qf_ref, kf_ref, dist_ref, lam_ref,
                      g_ref, o_ref, qa_sc, ka_sc, va_sc, k2_sc, m_sc, acc_sc, sa_sc, pa_sc, pb_sc, kt_sc, *, tk,
                      out_scale, lambda_init):
    seq = k_ref.shape[0]
    nk = seq // tk
    k = k_ref[...]
    ka_sc[:, :B_VDIM] = k
    va_sc[:, :B_VDIM] = v_ref[...]
    lane = lax.broadcasted_iota(jnp.int32, (seq, LANES), 1)
    va_sc[:, B_VDIM:] = jnp.where(lane == 0, 1.0, 0.0).astype(BF16)
    kf = kf_ref[...]
    flane = lax.broadcasted_iota(jnp.int32, kf.shape, 1)
    chunk_lanes = jnp.logical_and(flane >= _CHUNK_LANE, flane < _CHUNK_LANE + 3)
    for c in range(nk):
        ka_sc[c * tk:(c + 1) * tk, B_VDIM:] = jnp.where(chunk_lanes, float(c), kf).astype(BF16)
    for c in range(nk):
        kt_sc[:, c * tk:(c + 1) * tk] = ka_sc[c * tk:(c + 1) * tk, :].T
    kk = k.astype(F32)
    kk = kk * kk
    for c, sel in enumerate((lane < HEAD_DIM, lane >= HEAD_DIM)):
        n2 = jnp.sum(jnp.where(sel, kk, 0.0), axis=-1, keepdims=True)
        k2_sc[c] = jnp.broadcast_to(jnp.max(n2, axis=0, keepdims=True), k2_sc.shape[1:])

    def body(i, carry):
        i = jnp.asarray(i, jnp.int32)
        _diff_attn_block(i, reach_ref, slope_ref, q_ref, qf_ref, dist_ref, lam_ref, g_ref, o_ref, qa_sc,
                         ka_sc, va_sc, k2_sc, m_sc, acc_sc, sa_sc, pa_sc, pb_sc, kt_sc, tq=tk, nk=nk,
                         out_scale=out_scale, lambda_init=lambda_init)
        return carry

    lax.fori_loop(0, nk, body, 0)


def _diff_attn_block(i, reach_ref, slope_ref, q_ref, qf_ref, dist_ref, lam_ref, g_ref, o_ref, qa_sc,
                     ka_sc, va_sc, k2_sc, m_sc, acc_sc, sa_sc, pa_sc, pb_sc, kt_sc, *, tq, nk, out_scale,
                     lambda_init):
    h = pl.program_id(1)
    tk = tq
    slope = slope_ref[h]
    reach = reach_ref[h]
    rows = pl.ds(pl.multiple_of(i * tq, tq), tq)
    q = q_ref[rows, :]
    lane = lax.broadcasted_iota(jnp.int32, q.shape, 1)
    zero = jnp.zeros_like(q)
    q0 = jnp.where(lane < HEAD_DIM, q, zero)
    q1 = jnp.where(lane >= HEAD_DIM, q, zero)
    qq = q.astype(F32)
    qq = (qq * qq).astype(BF16)
    comp = lax.broadcasted_iota(jnp.int32, (B_VDIM, LANES), 0)
    n0 = _dot(qq, jnp.where(comp < HEAD_DIM, 1.0, 0.0).astype(BF16)) * (1.0 + 2.0 ** -7)
    n1 = _dot(qq, jnp.where(comp >= HEAD_DIM, 1.0, 0.0).astype(BF16)) * (1.0 + 2.0 ** -7)
    bound = jnp.concatenate([jnp.sqrt(n0) * jnp.sqrt(k2_sc[0][0:1, :]),
                             jnp.sqrt(n1) * jnp.sqrt(k2_sc[1][0:1, :])], axis=0)
    off = slope * (i * tk).astype(F32)
    qf = qf_ref[...]
    qf2 = jnp.concatenate([qf, qf], axis=0)

    def piece_lanes(x, first_lane):
        lane_ = lax.broadcasted_iota(jnp.int32, x.shape, 1)
        out = jnp.zeros_like(x)
        for n, piece in enumerate(_bf16_split(x)):
            out = jnp.where(lane_ == first_lane + n, piece.astype(F32), out)
        return out.astype(BF16)

    def build_queries(with_bound):
        feats = {_LEFT: -qf2, _DIAG: jnp.zeros_like(qf2), _RIGHT: qf2}
        if with_bound:
            base = piece_lanes(bound, _BOUND_LANE)
            side = piece_lanes(jnp.full((SUBLANES, LANES), off, F32), _OFFSET_LANE)[0:1]
            feats = {_LEFT: base + side - qf2, _DIAG: base, _RIGHT: base - side + qf2}
        for variant, feat in feats.items():
            qa_sc[variant, :, B_VDIM:] = feat
            qa_sc[variant, :tq, :B_VDIM] = q0
            qa_sc[variant, tq:, :B_VDIM] = q1

    def scores(j, s_ref):
        variant = jnp.where(j < i, _LEFT, jnp.where(j == i, _DIAG, _RIGHT))
        start = pl.multiple_of(j * tk, tk)
        s_ref[...] = _dot_nt(qa_sc[variant], ka_sc[pl.ds(start, tk), :])

    def diag_bias(j, s_ref):
        @pl.when(j == i)
        def _():
            bias = slope * dist_ref[...]
            s_ref[:tq] -= bias
            s_ref[tq:] -= bias

    def probs(j, p_ref):
        start = pl.multiple_of(j * tk, tk)
        s = _dot(qa_sc[jnp.where(j < i, _LEFT, _RIGHT)], kt_sc[:, pl.ds(start, tk)])
        p_ref[...] = jnp.exp2(s).astype(BF16)

    def values(j):
        return va_sc[pl.ds(pl.multiple_of(j * tk, tk), tk), :]

    build_queries(True)
    lo = jnp.maximum(i - reach, 0)
    count = jnp.minimum(i + reach + 1, nk) - lo - 1

    def off_diag(n):
        j = lo + jnp.minimum(n, count - 1)
        return j + (j >= i).astype(jnp.int32)

    s_diag = _dot(qa_sc[_DIAG], kt_sc[:, pl.ds(pl.multiple_of(i * tk, tk), tk)])
    probs(off_diag(0), pa_sc)
    bias = slope * dist_ref[...]
    p_diag = jnp.exp2(s_diag - jnp.concatenate([bias, bias], axis=0)).astype(BF16)
    acc_sc[...] = _dot(p_diag, values(i))

    def pair(tt, carry):
        n = 2 * tt
        probs(off_diag(n + 1), pb_sc)
        acc_sc[...] += _dot(pa_sc[...], values(off_diag(n)))
        probs(off_diag(n + 2), pa_sc)
        acc_sc[...] += _dot(pb_sc[...], values(off_diag(n + 1)))
        return carry

    lax.fori_loop(0, count // 2, pair, 0)

    @pl.when(count % 2 == 1)
    def _():
        acc_sc[...] += _dot(pa_sc[...], values(off_diag(count - 1)))

    denom_min = jnp.min(acc_sc[:, B_VDIM:B_VDIM + 1])

    @pl.when(jnp.logical_not(denom_min >= _MIN_DENOM))
    def _():
        build_queries(False)
        m_sc[...] = jnp.full_like(m_sc, NEG_INF)
        acc_sc[...] = jnp.zeros_like(acc_sc)

        def chunk(j, carry):
            scores(j, sa_sc)
            diag_bias(j, sa_sc)
            adj = jnp.where(j > i, off, jnp.where(j < i, -off, 0.0))
            start = pl.multiple_of(j * tk, tk)
            s = sa_sc[...]
            m_prev = m_sc[...]
            m_new = jnp.maximum(m_prev, jnp.max(s, axis=-1, keepdims=True) + adj)
            p = jnp.exp2(s - (m_new - adj)).astype(BF16)
            acc_sc[...] = jnp.exp2(m_prev - m_new) * acc_sc[...] + _dot(p, va_sc[pl.ds(start, tk), :])
            m_sc[...] = m_new
            return carry

        lax.fori_loop(0, nk, chunk, 0)

    lp = lam_ref[...]
    lam = (jnp.exp(jnp.sum(lp[0:1] * lp[1:2], axis=-1, keepdims=True))
           - jnp.exp(jnp.sum(lp[2:3] * lp[3:4], axis=-1, keepdims=True)) + lambda_init)
    acc = acc_sc[...]
    o0 = acc[:tq, :B_VDIM] * (1.0 / acc[:tq, B_VDIM:B_VDIM + 1])
    o1 = acc[tq:, :B_VDIM] * (1.0 / acc[tq:, B_VDIM:B_VDIM + 1])
    o_ref[rows, :] = (_rms(o0 - lam * o1, g_ref[...]) * out_scale).astype(o_ref.dtype)


def _bf16_pieces(x, n=3):
    out = []
    r = np.float64(x)
    for _ in range(n):
        p = np.float64(np.float32(r).astype(jnp.bfloat16).astype(np.float32))
        out.append(p)
        r = r - p
    return out


def _alibi_features(slopes2, t):
    pos = np.arange(t)
    hi = (pos // _POS_SPLIT) * _POS_SPLIT
    lo = pos % _POS_SPLIT
    qf = np.zeros((len(slopes2), t, LANES), np.float32)
    kf = np.zeros((len(slopes2), t, LANES), np.float32)
    for h, s in enumerate(slopes2):
        for n, piece in enumerate(_bf16_pieces(s)):
            for base, part in ((0, hi), (3, lo)):
                qf[h, :, base + n] = -piece
                kf[h, :, base + n] = part
                qf[h, :, 6 + base + n] = part
                kf[h, :, 6 + base + n] = piece
        for n, piece in enumerate(_bf16_pieces(s * t)):
            qf[h, :, _CHUNK_LANE + n] = -piece
        kf[h, :, _BOUND_LANE:_BOUND_LANE + 3] = -1.0
        kf[h, :, _OFFSET_LANE:_OFFSET_LANE + 3] = -1.0
    return jnp.asarray(qf, BF16), jnp.asarray(kf, BF16)


def _diff_attention(proj, b_lambda, subln_g, lambda_init, batch, seq, t=512):
    t = min(t, seq // 2)
    nq = seq // t
    assert seq % (2 * t) == 0
    kcol = MIX_WIDTH // B_VDIM
    vcol = 2 * kcol
    slopes2 = [s * LOG2E for s in _alibi_slopes(B_HEADS)]
    reach = [min(nq, int(math.floor((_ZERO_EXP2_ARG / s - 1.0) / t)) + 1) for s in slopes2]
    qf, kf = _alibi_features(slopes2, t)
    pos = np.arange(t)
    dist = jnp.asarray(np.abs(pos[:, None] - pos[None, :]), F32)
    kernel = functools.partial(_diff_attn_kernel, tk=t, out_scale=1.0 - lambda_init,
                               lambda_init=lambda_init)
    return pl.pallas_call(
        kernel,
        grid_spec=pltpu.PrefetchScalarGridSpec(
            num_scalar_prefetch=2,
            grid=(batch, B_HEADS),
            in_specs=[pl.BlockSpec((seq, B_VDIM), lambda b, h, r, s: (b, h)),
                      pl.BlockSpec((seq, B_VDIM), lambda b, h, r, s: (b, kcol + h)),
                      pl.BlockSpec((seq, B_VDIM), lambda b, h, r, s: (b, vcol + h)),
                      pl.BlockSpec((None, t, LANES), lambda b, h, r, s: (h, 0, 0)),
                      pl.BlockSpec((None, t, LANES), lambda b, h, r, s: (h, 0, 0)),
                      pl.BlockSpec((t, t), lambda b, h, r, s: (0, 0)),
                      pl.BlockSpec((4, HEAD_DIM), lambda b, h, r, s: (0, 0)),
                      pl.BlockSpec((1, B_VDIM), lambda b, h, r, s: (0, 0))],
            out_specs=pl.BlockSpec((seq, B_VDIM), lambda b, h, r, s: (b, h)),
            scratch_shapes=[pltpu.VMEM((3, 2 * t, 2 * LANES), BF16),
                            pltpu.VMEM((seq, 2 * LANES), BF16),
                            pltpu.VMEM((seq, 2 * LANES), BF16),
                            pltpu.VMEM((2, 8, LANES), F32),
                            pltpu.VMEM((2 * t, 1), F32),
                            pltpu.VMEM((2 * t, 2 * LANES), F32),
                            pltpu.VMEM((2 * t, t), F32),
                            pltpu.VMEM((2 * t, t), BF16),
                            pltpu.VMEM((2 * t, t), BF16),
                            pltpu.VMEM((2 * LANES, seq), BF16)],
        ),
        out_shape=jax.ShapeDtypeStruct((batch * seq, MIX_WIDTH), BF16),
        compiler_params=_params("parallel", "parallel"),
        name="diff_attention",
    )(jnp.asarray(reach, jnp.int32), jnp.asarray(slopes2, F32), proj, proj, proj, qf, kf, dist,
      b_lambda, subln_g.reshape(1, B_VDIM))


def _router_kernel(x_ref, g_ref, whi_ref, wlo_ref, h_ref, e_ref, gate_ref):
    h = _rms(x_ref[...], g_ref[...])
    h_hi = h.astype(BF16)
    h_lo = (h - h_hi.astype(F32)).astype(BF16)
    whi = whi_ref[...]
    logits = _dot(h_hi, whi) + (_dot(h_hi, wlo_ref[...]) + _dot(h_lo, whi))
    lane = lax.broadcasted_iota(jnp.int32, logits.shape, 1)
    logits = jnp.where(lane < N_EXPERTS, logits, -jnp.inf)
    m1 = jnp.max(logits, axis=-1, keepdims=True)
    i1 = jnp.min(jnp.where(logits == m1, lane, LANES), axis=-1, keepdims=True)
    rest = jnp.where(lane == i1, -jnp.inf, logits)
    m2 = jnp.max(rest, axis=-1, keepdims=True)
    i2 = jnp.min(jnp.where(rest == m2, lane, LANES), axis=-1, keepdims=True)
    r = jnp.exp(m2 - m1)
    g1 = 1.0 / (1.0 + r)
    g2 = r * g1
    e_ref[...] = jnp.where(lane == 0, i1, jnp.where(lane == 1, i2, 0))
    gate_ref[...] = jnp.where(lane == 0, g1, jnp.where(lane == 1, g2, 0.0))
    h_ref[...] = h_hi


def _router(x, g, w_router, tm=2048):
    t, d = x.shape
    tm = min(tm, t)
    wpad = jnp.zeros((d, LANES), F32).at[:, :N_EXPERTS].set(w_router)
    whi = wpad.astype(BF16)
    wlo = (wpad - whi.astype(F32)).astype(BF16)
    return pl.pallas_call(
        _router_kernel,
        grid=(t // tm,),
        in_specs=[pl.BlockSpec((tm, d), lambda i: (i, 0)),
                  pl.BlockSpec((1, d), lambda i: (0, 0)),
                  pl.BlockSpec((d, LANES), lambda i: (0, 0)),
                  pl.BlockSpec((d, LANES), lambda i: (0, 0))],
        out_specs=[pl.BlockSpec((tm, d), lambda i: (i, 0)),
                   pl.BlockSpec((tm, LANES), lambda i: (i, 0)),
                   pl.BlockSpec((tm, LANES), lambda i: (i, 0))],
        out_shape=[jax.ShapeDtypeStruct((t, d), BF16),
                   jax.ShapeDtypeStruct((t, LANES), jnp.int32),
                   jax.ShapeDtypeStruct((t, LANES), F32)],
        compiler_params=_params("parallel"),
        name="router",
    )(x, g.reshape(1, d), whi, wlo)


def _slab_sizes(rows, tc):
    quarter, half = tc // 4, tc // 2
    return ((quarter, rows <= quarter),
            (half, jnp.logical_and(rows > quarter, rows <= half)),
            (tc, rows > half))


def _slab_stride(slab_ref, n, step, tc):
    most = slab_ref[(n + step) * N_EXPERTS]
    for e in range(1, N_EXPERTS):
        most = jnp.maximum(most, slab_ref[(n + step) * N_EXPERTS + e])
    return jnp.where(most <= tc // 2, tc // 2, tc)


def _dispatch_kernel(slab_ref, fill_ref, h_ref, dest_ref, gate_ref, xs_hbm, gs_hbm,
                     xslab_sc, gslab_sc, zx_sc, zg_sc, sem, zsem):
    i = pl.program_id(0)
    n = pl.num_programs(0)
    slot = i % 2
    tc = h_ref.shape[0]
    tm = zx_sc.shape[0]

    def slab_dma(step, sl, op):
        stride = _slab_stride(slab_ref, n, step, tc)
        for e in range(N_EXPERTS):
            start = pl.multiple_of(slab_ref[step * N_EXPERTS + e], SUBLANES)
            first = pl.multiple_of(e * stride, SUBLANES)
            for size, cond in _slab_sizes(slab_ref[(n + step) * N_EXPERTS + e], tc):
                @pl.when(cond)
                def _():
                    for src, dst in ((xslab_sc, xs_hbm), (gslab_sc, gs_hbm)):
                        copy = pltpu.make_async_copy(src.at[sl, pl.ds(first, size)],
                                                     dst.at[pl.ds(start, size)], sem.at[sl])
                        getattr(copy, op)()

    def fill_copies(start):
        start = pl.multiple_of(start, SUBLANES)
        return [pltpu.make_async_copy(zx_sc, xs_hbm.at[pl.ds(start, tm)], zsem),
                pltpu.make_async_copy(zg_sc, gs_hbm.at[pl.ds(start, tm)], zsem)]

    @pl.when(i == 0)
    def _():
        zx_sc[...] = jnp.zeros_like(zx_sc)
        zg_sc[...] = jnp.zeros_like(zg_sc)
        for half in range(2):
            for e in range(N_EXPERTS):
                for c in fill_copies(fill_ref[half * N_EXPERTS + e]):
                    c.start()
            for e in range(N_EXPERTS):
                for c in fill_copies(fill_ref[half * N_EXPERTS + e]):
                    c.wait()
        first_free = fill_ref[2 * N_EXPERTS]
        n_tiles = xs_hbm.shape[0] // tm

        def start_tile(b, carry):
            for c in fill_copies(b * tm):
                c.start()
            return carry

        def wait_tile(b, carry):
            for c in fill_copies(b * tm):
                c.wait()
            return carry

        lax.fori_loop(first_free, n_tiles, start_tile, 0)
        lax.fori_loop(first_free, n_tiles, wait_tile, 0)

    dest = dest_ref[0]
    gate = gate_ref[...]
    lane = lax.broadcasted_iota(jnp.int32, gate.shape, 1)

    def gate_pieces(col):
        out = jnp.zeros(gate.shape, F32)
        for n_, piece in enumerate(_bf16_split(col)):
            out = jnp.where(lane == n_, piece.astype(F32), out)
        return out.astype(BF16)

    def place_rows(stride):
        pos = lax.broadcasted_iota(jnp.int32, (stride, tc), 0)
        sel0, sel1 = [], []
        for e in range(N_EXPERTS):
            start = slab_ref[i * N_EXPERTS + e]
            sel0.append(jnp.where(dest[0:1, :] - start == pos, 1.0, 0.0))
            sel1.append(jnp.where(dest[1:2, :] - start == pos, 1.0, 0.0))
        sel0 = jnp.concatenate(sel0, axis=0).astype(BF16)
        sel1 = jnp.concatenate(sel1, axis=0).astype(BF16)
        rows = pl.ds(0, N_EXPERTS * stride)
        xslab_sc[slot, rows] = _dot(sel0 + sel1, h_ref[...])
        gslab_sc[slot, rows] = (_dot(sel0, gate_pieces(gate[:, 0:1]))
                                + _dot(sel1, gate_pieces(gate[:, 1:2])))

    compact = _slab_stride(slab_ref, n, i, tc) < tc
    pl.when(compact)(lambda: place_rows(tc // 2))
    pl.when(jnp.logical_not(compact))(lambda: place_rows(tc))

    @pl.when(i > 0)
    def _():
        slab_dma(i - 1, 1 - slot, "wait")

    slab_dma(i, slot, "start")

    @pl.when(i == n - 1)
    def _():
        slab_dma(i, slot, "wait")


def _dispatch(h, dest_t, gates, slab, fill, p_rows, tm, tc):
    t, d = h.shape
    nb = t // tc
    return pl.pallas_call(
        _dispatch_kernel,
        grid_spec=pltpu.PrefetchScalarGridSpec(
            num_scalar_prefetch=2,
            grid=(nb,),
            in_specs=[pl.BlockSpec((tc, d), lambda i, s, z: (i, 0)),
                      pl.BlockSpec((1, TOP_K, tc), lambda i, s, z: (i, 0, 0)),
                      pl.BlockSpec((tc, LANES), lambda i, s, z: (i, 0))],
            out_specs=[pl.BlockSpec(memory_space=pl.ANY), pl.BlockSpec(memory_space=pl.ANY)],
            scratch_shapes=[pltpu.VMEM((2, N_EXPERTS * tc, d), F32),
                            pltpu.VMEM((2, N_EXPERTS * tc, LANES), F32),
                            pltpu.VMEM((tm, d), F32), pltpu.VMEM((tm, LANES), F32),
                            pltpu.SemaphoreType.DMA((2,)), pltpu.SemaphoreType.DMA(())],
        ),
        out_shape=[jax.ShapeDtypeStruct((p_rows, d), F32), jax.ShapeDtypeStruct((p_rows, LANES), F32)],
        compiler_params=_params("arbitrary"),
        name="moe_dispatch",
    )(slab, fill, h, dest_t, gates)


def _moe_kernel(blk_e_ref, used_ref, xblk_ref, x_ref, gs_ref, wg_ref, wu_ref, wd_ref, o_ref,
                xb_sc, acc_sc):
    i = pl.program_id(0)
    f = pl.program_id(1)
    used = used_ref[i] > 0

    @pl.when(jnp.logical_and(used, f == 0))
    def _():
        xb_sc[...] = x_ref[...].astype(BF16)
        acc_sc[...] = jnp.zeros_like(acc_sc)

    @pl.when(used)
    def _():
        acc_sc[...] += _swiglu_tile(xb_sc[...], wg_ref, wu_ref, wd_ref)

    @pl.when(f == pl.num_programs(1) - 1)
    def _():
        gs = gs_ref[...]
        gate = gs[:, 0:1] + gs[:, 1:2] + gs[:, 2:3]
        o_ref[...] = jnp.where(used, acc_sc[...] * gate, 0.0)


def _moe_experts(xs, gs, blk_e, blk_used, xblk, wg, wu, wd, tm, tf=1792):
    p, d = xs.shape
    nblk = p // tm
    nf = wg.shape[2] // tf

    def fidx(i, f, used_ref):
        return jnp.where(used_ref[i] > 0, f, nf - 1)

    return pl.pallas_call(
        _moe_kernel,
        grid_spec=pltpu.PrefetchScalarGridSpec(
            num_scalar_prefetch=3,
            grid=(nblk, nf),
            in_specs=[pl.BlockSpec((tm, d), lambda i, f, be, us, xb: (xb[i], 0)),
                      pl.BlockSpec((tm, LANES), lambda i, f, be, us, xb: (xb[i], 0)),
                      pl.BlockSpec((None, d, tf), lambda i, f, be, us, xb: (be[i], 0, fidx(i, f, us))),
                      pl.BlockSpec((None, d, tf), lambda i, f, be, us, xb: (be[i], 0, fidx(i, f, us))),
                      pl.BlockSpec((None, tf, d), lambda i, f, be, us, xb: (be[i], fidx(i, f, us), 0))],
            out_specs=pl.BlockSpec((tm, d), lambda i, f, be, us, xb: (i, 0)),
            scratch_shapes=[pltpu.VMEM((tm, d), BF16), pltpu.VMEM((tm, d), F32)],
        ),
        out_shape=jax.ShapeDtypeStruct((p, d), F32),
        compiler_params=_params("arbitrary", "arbitrary"),
        name="moe_experts",
    )(blk_e, blk_used, xblk, xs, gs, wg, wu, wd)


def _combine_kernel(slab_ref, ys_hbm, dest_ref, x_ref, g_ref, o_ref, slab_sc, sem):
    i = pl.program_id(0)
    n = pl.num_programs(0)
    slot = i % 2
    tc = x_ref.shape[0]

    def slab_dma(step, sl, op):
        stride = _slab_stride(slab_ref, n, step, tc)
        for e in range(N_EXPERTS):
            start = pl.multiple_of(slab_ref[step * N_EXPERTS + e], SUBLANES)
            first = pl.multiple_of(e * stride, SUBLANES)
            for size, cond in _slab_sizes(slab_ref[(n + step) * N_EXPERTS + e], tc):
                @pl.when(cond)
                def _():
                    copy = pltpu.make_async_copy(ys_hbm.at[pl.ds(start, size)],
                                                 slab_sc.at[sl, pl.ds(first, size)], sem.at[sl])
                    getattr(copy, op)()

    @pl.when(i == 0)
    def _():
        slab_sc[...] = jnp.zeros_like(slab_sc)
        slab_dma(0, 0, "start")

    @pl.when(i + 1 < n)
    def _():
        slab_dma(i + 1, 1 - slot, "start")

    dest = dest_ref[...]
    slab_dma(i, slot, "wait")

    def gather_rows(stride):
        pos = lax.broadcasted_iota(jnp.int32, (tc, stride), 1)
        sel = []
        for e in range(N_EXPERTS):
            start = slab_ref[i * N_EXPERTS + e]
            hit = jnp.logical_or(dest[:, 0:1] - start == pos, dest[:, 1:2] - start == pos)
            sel.append(jnp.where(hit, 1.0, 0.0))
        sel = jnp.concatenate(sel, axis=1).astype(BF16)
        y = _dot(sel, slab_sc[slot, pl.ds(0, N_EXPERTS * stride)].astype(BF16))
        o_ref[...] = x_ref[...] + _rms(y, g_ref[...])

    compact = _slab_stride(slab_ref, n, i, tc) < tc
    pl.when(compact)(lambda: gather_rows(tc // 2))
    pl.when(jnp.logical_not(compact))(lambda: gather_rows(tc))


def _combine(ys, dest, slab, x, g, tc):
    t, d = x.shape
    return pl.pallas_call(
        _combine_kernel,
        grid_spec=pltpu.PrefetchScalarGridSpec(
            num_scalar_prefetch=1,
            grid=(t // tc,),
            in_specs=[pl.BlockSpec(memory_space=pl.ANY),
                      pl.BlockSpec((tc, TOP_K), lambda i, s: (i, 0)),
                      pl.BlockSpec((tc, d), lambda i, s: (i, 0)),
                      pl.BlockSpec((1, d), lambda i, s: (0, 0))],
            out_specs=pl.BlockSpec((tc, d), lambda i, s: (i, 0)),
            scratch_shapes=[pltpu.VMEM((2, N_EXPERTS * tc, d), F32), pltpu.SemaphoreType.DMA((2,))],
        ),
        out_shape=jax.ShapeDtypeStruct((t, d), F32),
        compiler_params=_params("arbitrary"),
        name="moe_combine",
    )(slab, ys, dest, x, g.reshape(1, d))


def _route(top_e, tm, tc):
    t = top_e.shape[0]
    a = t * TOP_K
    nb = t // tc
    e_flat = top_e.reshape(a)
    onehot = (e_flat[:, None] == jnp.arange(N_EXPERTS, dtype=jnp.int32)[None, :]).astype(jnp.int32)
    csum = jnp.cumsum(onehot, axis=0)
    rank = jnp.sum(onehot * (csum - 1), axis=1)
    upto = csum[TOP_K * tc - 1::TOP_K * tc]
    before = jnp.concatenate([jnp.zeros((1, N_EXPERTS), jnp.int32), upto[:nb - 1]], axis=0)
    rows = (upto - before + SUBLANES - 1) // SUBLANES * SUBLANES
    rows_before = jnp.cumsum(rows, axis=0) - rows
    total = jnp.sum(rows, axis=0)
    region = (total + 2 * tm - 1) // tm * tm
    rends = jnp.cumsum(region)
    rstarts = rends - region
    slab = (rstarts[None, :] + rows_before).astype(jnp.int32)
    shift = jnp.broadcast_to((slab - before)[:, None, :], (nb, TOP_K * tc, N_EXPERTS)).reshape(a, N_EXPERTS)
    dest = (rank + jnp.sum(onehot * shift, axis=1)).astype(jnp.int32).reshape(t, TOP_K)
    slab = jnp.concatenate([slab.reshape(nb * N_EXPERTS), rows.reshape(nb * N_EXPERTS)])
    row_end = rstarts + total
    fill = jnp.concatenate([row_end, rends - tm, rends[-1:] // tm]).astype(jnp.int32)
    nblk = -(-(a + nb * N_EXPERTS * (SUBLANES - 1)) // tm) + 2 * N_EXPERTS
    blk_start = jnp.arange(nblk, dtype=jnp.int32) * tm
    blk_e = jnp.minimum(jnp.sum((rends[None, :] <= blk_start[:, None]).astype(jnp.int32), axis=1),
                        N_EXPERTS - 1)
    blk_used = jnp.logical_and(blk_start < row_end[blk_e], blk_start < rends[-1]).astype(jnp.int32)
    first_used = jnp.argmax(blk_used).astype(jnp.int32)
    xblk = jnp.where(blk_used > 0, jnp.arange(nblk, dtype=jnp.int32), first_used)
    return dest, slab, fill, blk_e, blk_used, xblk, nblk * tm


def _moe(x, h, e_pad, gates, g_out, wg, wu, wd, tm=512, tc=256):
    t = x.shape[0]
    tc = min(tc, t)
    dest, slab, fill, blk_e, blk_used, xblk, p_rows = _route(e_pad[:, :TOP_K], tm, tc)
    dest_t = dest.reshape(t // tc, tc, TOP_K).transpose(0, 2, 1)
    xs, gs = _dispatch(h, dest_t, gates, slab, fill, p_rows, tm, tc)
    ys = _moe_experts(xs, gs, blk_e, blk_used, xblk, wg, wu, wd, tm)
    return _combine(ys, dest, slab, x, g_out, tc)


def kernel(x, mem, norm_g, mem_norm_g, w_mem_kv, a_w_in, a_sink, a_w_out, b_w_in, b_lambda,
           b_subln_g, b_w_out, ffn_w_gate, ffn_w_up, ffn_w_down, moe_w_router, moe_w_gate,
           moe_w_up, moe_w_down):
    batch, seq, d = x.shape
    n_mem = mem.shape[1]
    bf = lambda w: w.astype(BF16)
    xt = x.reshape(batch * seq, d)
    mkv = _norm_matmul(mem.reshape(batch * n_mem, d), mem_norm_g, bf(w_mem_kv),
                       jnp.ones((w_mem_kv.shape[1],), F32))

    g = norm_g[0]
    proj = _norm_matmul(xt, g[0], bf(a_w_in[0]), _query_col_scale(a_w_in.shape[2], A_Q_HEADS * HEAD_DIM))
    mix = _window_attention(proj, a_sink[0], batch, seq)
    mem_out = _mem_attention(proj, (A_Q_HEADS + 2 * A_KV_HEADS) * HEAD_DIM // MEM_WIDTH, mkv, batch, seq)
    xt = _out_proj(mix, mem_out, bf(a_w_out[0]), xt, g[1])
    xt = _ffn(xt, g[2], g[3], bf(ffn_w_gate[0]), bf(ffn_w_up[0]), bf(ffn_w_down[0]))

    g = norm_g[1]
    lambda_init = 0.8 - 0.6 * math.exp(-0.3 * 1)
    proj = _norm_matmul(xt, g[0], bf(b_w_in[0]), _query_col_scale(b_w_in.shape[2], MIX_WIDTH))
    mix = _diff_attention(proj, b_lambda[0], b_subln_g[0], lambda_init, batch, seq)
    mem_out = _mem_attention(proj, 3 * MIX_WIDTH // MEM_WIDTH, mkv, batch, seq)
    xt = _out_proj(mix, mem_out, bf(b_w_out[0]), xt, g[1])
    h, e_pad, gates = _router(xt, g[2], moe_w_router[0])
    xt = _moe(xt, h, e_pad, gates, g[3], bf(moe_w_gate[0]), bf(moe_w_up[0]), bf(moe_w_down[0]))
    return xt.reshape(batch, seq, d)
```

```python
import functools
import math

import jax
import jax.numpy as jnp
import numpy as np
from jax import lax
from jax.experimental import pallas as pl
from jax.experimental.pallas import tpu as pltpu

D_MODEL = 1024
HEAD_DIM = 64
MIX_WIDTH = 768
MEM_WIDTH = 256
MEM_HEADS = 4
A_Q_HEADS = 12
A_KV_HEADS = 4
A_GROUP = 3
WINDOW = 128
B_HEADS = 6
B_VDIM = 128
D_FF = 3584
N_EXPERTS = 8
TOP_K = 2
EPS = 1e-6
NEG_INF = -1e30
LOG2E = math.log2(math.e)
QK_SCALE = HEAD_DIM ** -0.5 * LOG2E
LANES = 128
SUBLANES = 8
VMEM_LIMIT_BYTES = 56 * 1024 * 1024

BF16 = jnp.bfloat16
F32 = jnp.float32


def _params(*sem):
    return pltpu.CompilerParams(dimension_semantics=sem, vmem_limit_bytes=VMEM_LIMIT_BYTES)


def _rms(x, g):
    return x * lax.rsqrt(jnp.mean(x * x, axis=-1, keepdims=True) + EPS) * g


def _dot(a, b):
    return jnp.dot(a, b, preferred_element_type=F32)


def _dot_nt(a, b):
    return lax.dot_general(a, b, (((1,), (1,)), ((), ())), preferred_element_type=F32)


def _alibi_slopes(n):
    return [2.0 ** (-8.0 * (i + 1) / n) for i in range(n)]


def _norm_matmul_kernel(x_ref, g_ref, w_ref, cs_ref, o_ref, *, chunk):
    h = _rms(x_ref[...], g_ref[...]).astype(BF16)
    for c in range(o_ref.shape[1] // chunk):
        sl = slice(c * chunk, (c + 1) * chunk)
        o_ref[:, sl] = (_dot(h, w_ref[:, sl]) * cs_ref[:, sl]).astype(o_ref.dtype)


def _norm_matmul(x, g, w, col_scale, tm=1024, chunk=512):
    t, d = x.shape
    n = w.shape[1]
    tm = min(tm, t)
    return pl.pallas_call(
        functools.partial(_norm_matmul_kernel, chunk=min(chunk, n)),
        grid=(t // tm,),
        in_specs=[pl.BlockSpec((tm, d), lambda i: (i, 0)),
                  pl.BlockSpec((1, d), lambda i: (0, 0)),
                  pl.BlockSpec((d, n), lambda i: (0, 0)),
                  pl.BlockSpec((1, n), lambda i: (0, 0))],
        out_specs=pl.BlockSpec((tm, n), lambda i: (i, 0)),
        out_shape=jax.ShapeDtypeStruct((t, n), BF16),
        compiler_params=_params("parallel"),
        name="norm_matmul",
    )(x, g.reshape(1, d), w, col_scale.reshape(1, n))


def _query_col_scale(n, mix_q_cols):
    cols = np.arange(n)
    return jnp.asarray(np.where((cols < mix_q_cols) | (cols >= n - MEM_WIDTH), QK_SCALE, 1.0), F32)


def _window_kernel(q_ref, kp_ref, kc_ref, kn_ref, vp_ref, vc_ref, vn_ref, bias_ref, o_ref):
    n = pl.program_id(1)
    nb = pl.num_programs(1)
    blk = kp_ref.shape[0]
    sub = q_ref.shape[0] // blk
    zeros = jnp.zeros((blk, kp_ref.shape[1]), kp_ref.dtype)
    k_all = jnp.concatenate([kp_ref[...], kc_ref[...], kn_ref[...]], axis=0)
    v_all = jnp.concatenate([vp_ref[...], vc_ref[...], vn_ref[...]], axis=0)
    col = lax.broadcasted_iota(jnp.int32, (1, 4 * blk), 1)
    half = lax.broadcasted_iota(jnp.int32, (4 * blk, LANES), 1) < HEAD_DIM
    low_half = jnp.where(half, 1.0, 0.0).astype(BF16)
    high_half = jnp.where(half, 0.0, 1.0).astype(BF16)
    for u in range(sub):
        k = jnp.concatenate([k_all[u * blk:(u + 3) * blk], zeros], axis=0)
        v = jnp.concatenate([v_all[u * blk:(u + 3) * blk], zeros], axis=0)
        edge = jnp.zeros((1, 4 * blk), F32)
        if u == 0:
            edge = jnp.where(jnp.logical_and(n == 0, col < blk), NEG_INF, edge)
        if u == sub - 1:
            edge = jnp.where(jnp.logical_and(n == nb - 1,
                                             jnp.logical_and(col >= 2 * blk, col < 3 * blk)), NEG_INF, edge)
        q = q_ref[u * blk:(u + 1) * blk, :]
        outs = []
        for kh in range(A_KV_HEADS):
            heads = range(kh * A_GROUP, (kh + 1) * A_GROUP)
            kv = slice(kh * HEAD_DIM, (kh + 1) * HEAD_DIM)
            qs = jnp.concatenate([q[:, h * HEAD_DIM:(h + 1) * HEAD_DIM] for h in heads], axis=0)
            s = _dot_nt(qs, k[:, kv]) + bias_ref[kh] + edge
            p = jnp.exp2(s - jnp.max(s, axis=-1, keepdims=True)).astype(BF16)
            low = kh % 2 == 0
            slab = v[:, (kh // 2) * LANES:(kh // 2 + 1) * LANES]
            own, other = (low_half, high_half) if low else (high_half, low_half)
            ov = _dot(p, jnp.concatenate([slab * own + other, own], axis=1))
            den = ov[:, LANES:]
            o = ov[:, :LANES] * (1.0 / jnp.where(den == 0.0, 1.0, den))
            o = o[:, :HEAD_DIM] if low else o[:, HEAD_DIM:]
            outs.extend(o[g * blk:(g + 1) * blk] for g in range(A_GROUP))
        o_ref[u * blk:(u + 1) * blk, :] = jnp.concatenate(outs, axis=-1).astype(o_ref.dtype)


def _window_bias(blk, sink):
    qi = np.arange(blk)[:, None]
    kj = np.arange(3 * blk)[None, :]
    dist = np.abs(blk + qi - kj)
    slopes = _alibi_slopes(A_Q_HEADS)
    tables = np.stack([np.where(dist <= WINDOW, -(slopes[h] * LOG2E) * dist, NEG_INF)
                       for h in range(A_Q_HEADS)])
    pad = jnp.full((A_Q_HEADS, blk, blk), NEG_INF, F32)
    pad = pad.at[:, :, 0].set(jnp.broadcast_to((sink.astype(F32) * LOG2E)[:, None], (A_Q_HEADS, blk)))
    full = jnp.concatenate([jnp.asarray(tables, F32), pad], axis=2)
    return full.reshape(A_KV_HEADS, A_GROUP * blk, 4 * blk)


def _window_attention(proj, sink, batch, seq, sub=4):
    blk = WINDOW
    nb = seq // blk
    sub = min(sub, nb)
    ns = nb // sub
    kcol = MIX_WIDTH // 256
    vcol = kcol + 1
    bias = _window_bias(blk, sink)

    def cur(col):
        return pl.BlockSpec((sub * blk, 256), lambda b, n: (b * ns + n, col))

    def prev(col):
        return pl.BlockSpec((blk, 256), lambda b, n: (b * nb + jnp.maximum(n * sub - 1, 0), col))

    def nxt(col):
        return pl.BlockSpec((blk, 256), lambda b, n: (b * nb + jnp.minimum((n + 1) * sub, nb - 1), col))

    return pl.pallas_call(
        _window_kernel,
        grid=(batch, ns),
        in_specs=[pl.BlockSpec((sub * blk, MIX_WIDTH), lambda b, n: (b * ns + n, 0)),
                  prev(kcol), cur(kcol), nxt(kcol), prev(vcol), cur(vcol), nxt(vcol),
                  pl.BlockSpec(bias.shape, lambda b, n: (0, 0, 0))],
        out_specs=pl.BlockSpec((sub * blk, MIX_WIDTH), lambda b, n: (b * ns + n, 0)),
        out_shape=jax.ShapeDtypeStruct((batch * seq, MIX_WIDTH), BF16),
        compiler_params=_params("parallel", "parallel"),
        name="window_attention",
    )(proj, proj, proj, proj, proj, proj, proj, bias)


def _mem_attn_kernel(q_ref, mkv_ref, o_ref):
    q = q_ref[...]
    mkv = mkv_ref[...]
    ones = jnp.ones((mkv.shape[0], LANES), BF16)
    outs = []
    for h in range(MEM_HEADS):
        sl = slice(h * HEAD_DIM, (h + 1) * HEAD_DIM)
        s = _dot_nt(q[:, sl], mkv[:, sl])
        p = jnp.exp2(s - jnp.max(s, axis=-1, keepdims=True)).astype(BF16)
        o = _dot(p, mkv[:, MEM_WIDTH + h * HEAD_DIM:MEM_WIDTH + (h + 1) * HEAD_DIM])
        outs.append(o * (1.0 / _dot(p, ones)[:, :HEAD_DIM]))
    o_ref[...] = jnp.concatenate(outs, axis=-1).astype(o_ref.dtype)


def _mem_attention(proj, qcol, mkv, batch, seq, tm=2048):
    tm = min(tm, seq)
    per_batch = seq // tm
    n_mem = mkv.shape[0] // batch
    return pl.pallas_call(
        _mem_attn_kernel,
        grid=(batch * per_batch,),
        in_specs=[pl.BlockSpec((tm, MEM_WIDTH), lambda i: (i, qcol)),
                  pl.BlockSpec((n_mem, 2 * MEM_WIDTH), lambda i: (i // per_batch, 0))],
        out_specs=pl.BlockSpec((tm, MEM_WIDTH), lambda i: (i, 0)),
        out_shape=jax.ShapeDtypeStruct((batch * seq, MEM_WIDTH), BF16),
        compiler_params=_params("parallel"),
        name="mem_attention",
    )(proj, mkv)


def _out_proj_kernel(mix_ref, mem_ref, w1_ref, w2_ref, x_ref, g_ref, o_ref):
    o = _dot(mix_ref[...], w1_ref[...]) + _dot(mem_ref[...], w2_ref[...])
    o_ref[...] = x_ref[...] + _rms(o, g_ref[...])


def _out_proj(mix, mem_out, w_out, x, g, tm=1024):
    t, d = x.shape
    tm = min(tm, t)
    w1 = w_out[:MIX_WIDTH]
    w2 = w_out[MIX_WIDTH:]
    return pl.pallas_call(
        _out_proj_kernel,
        grid=(t // tm,),
        in_specs=[pl.BlockSpec((tm, MIX_WIDTH), lambda i: (i, 0)),
                  pl.BlockSpec((tm, MEM_WIDTH), lambda i: (i, 0)),
                  pl.BlockSpec((MIX_WIDTH, d), lambda i: (0, 0)),
                  pl.BlockSpec((MEM_WIDTH, d), lambda i: (0, 0)),
                  pl.BlockSpec((tm, d), lambda i: (i, 0)),
                  pl.BlockSpec((1, d), lambda i: (0, 0))],
        out_specs=pl.BlockSpec((tm, d), lambda i: (i, 0)),
        out_shape=jax.ShapeDtypeStruct((t, d), F32),
        compiler_params=_params("parallel"),
        name="out_proj",
    )(mix, mem_out, w1, w2, x, g.reshape(1, d))


def _swiglu_tile(h, wg_ref, wu_ref, wd_ref, chunk=256):
    out = None
    for c in range(wg_ref.shape[1] // chunk):
        sl = slice(c * chunk, (c + 1) * chunk)
        a = _dot(h, wg_ref[:, sl])
        u = _dot(h, wu_ref[:, sl])
        z = (a * jax.nn.sigmoid(a) * u).astype(BF16)
        y = _dot(z, wd_ref[sl, :])
        out = y if out is None else out + y
    return out


def _ffn_kernel(x_ref, gin_ref, gout_ref, wg_ref, wu_ref, wd_ref, o_ref, h_sc, acc_sc):
    f = pl.program_id(1)

    @pl.when(f == 0)
    def _():
        h_sc[...] = _rms(x_ref[...], gin_ref[...]).astype(BF16)
        acc_sc[...] = jnp.zeros_like(acc_sc)

    acc_sc[...] += _swiglu_tile(h_sc[...], wg_ref, wu_ref, wd_ref)

    @pl.when(f == pl.num_programs(1) - 1)
    def _():
        o_ref[...] = x_ref[...] + _rms(acc_sc[...], gout_ref[...])


def _ffn(x, g_in, g_out, wg, wu, wd, tm=512, tf=3584):
    t, d = x.shape
    ff = wg.shape[1]
    tm = min(tm, t)
    return pl.pallas_call(
        _ffn_kernel,
        grid=(t // tm, ff // tf),
        in_specs=[pl.BlockSpec((tm, d), lambda i, f: (i, 0)),
                  pl.BlockSpec((1, d), lambda i, f: (0, 0)),
                  pl.BlockSpec((1, d), lambda i, f: (0, 0)),
                  pl.BlockSpec((d, tf), lambda i, f: (0, f)),
                  pl.BlockSpec((d, tf), lambda i, f: (0, f)),
                  pl.BlockSpec((tf, d), lambda i, f: (f, 0))],
        out_specs=pl.BlockSpec((tm, d), lambda i, f: (i, 0)),
        out_shape=jax.ShapeDtypeStruct((t, d), F32),
        scratch_shapes=[pltpu.VMEM((tm, d), BF16), pltpu.VMEM((tm, d), F32)],
        compiler_params=_params("parallel", "arbitrary"),
        name="ffn",
    )(x, g_in.reshape(1, d), g_out.reshape(1, d), wg, wu, wd)


_POS_SPLIT = 256
_LEFT, _DIAG, _RIGHT = 0, 1, 2
_BOUND_LANE = 12
_CHUNK_LANE = 15
_OFFSET_LANE = 18
_ZERO_EXP2_ARG = 136.0
_MIN_DENOM = 2.0 ** -60


def _bf16_split(x):
    hi = x.astype(BF16)
    r = x - hi.astype(F32)
    mid = r.astype(BF16)
    lo = (r - mid.astype(F32)).astype(BF16)
    return hi, mid, lo


def _diff_attn_kernel(reach_ref, slope_ref, q_ref, k_ref, v_ref, qf_ref, kf_ref, dist_ref, lam_ref,
                      g_ref, o_ref, qa_sc, ka_sc, va_sc, k2_sc, m_sc, acc_sc, sa_sc, pa_sc, pb_sc, kt_sc, *, tk,
                      out_scale, lambda_init):
    seq = k_ref.shape[0]
    nk = seq // tk
    k = k_ref[...]
    ka_sc[:, :B_VDIM] = k
    va_sc[:, :B_VDIM] = v_ref[...]
    lane = lax.broadcasted_iota(jnp.int32, (seq, LANES), 1)
    va_sc[:, B_VDIM:] = jnp.where(lane == 0, 1.0, 0.0).astype(BF16)
    kf = kf_ref[...]
    flane = lax.broadcasted_iota(jnp.int32, kf.shape, 1)
    chunk_lanes = jnp.logical_and(flane >= _CHUNK_LANE, flane < _CHUNK_LANE + 3)
    for c in range(nk):
        ka_sc[c * tk:(c + 1) * tk, B_VDIM:] = jnp.where(chunk_lanes, float(c), kf).astype(BF16)
    for c in range(nk):
        kt_sc[:, c * tk:(c + 1) * tk] = ka_sc[c * tk:(c + 1) * tk, :].T
    kk = k.astype(F32)
    kk = kk * kk
    for c, sel in enumerate((lane < HEAD_DIM, lane >= HEAD_DIM)):
        n2 = jnp.sum(jnp.where(sel, kk, 0.0), axis=-1, keepdims=True)
        k2_sc[c] = jnp.broadcast_to(jnp.max(n2, axis=0, keepdims=True), k2_sc.shape[1:])

    def body(i, carry):
        i = jnp.asarray(i, jnp.int32)
        _diff_attn_block(i, reach_ref, slope_ref, q_ref, qf_ref, dist_ref, lam_ref, g_ref, o_ref, qa_sc,
                         ka_sc, va_sc, k2_sc, m_sc, acc_sc, sa_sc, pa_sc, pb_sc, kt_sc, tq=tk, nk=nk,
                         out_scale=out_scale, lambda_init=lambda_init)
        return carry

    lax.fori_loop(0, nk, body, 0)


def _diff_attn_block(i, reach_ref, slope_ref, q_ref, qf_ref, dist_ref, lam_ref, g_ref, o_ref, qa_sc,
                     ka_sc, va_sc, k2_sc, m_sc, acc_sc, sa_sc, pa_sc, pb_sc, kt_sc, *, tq, nk, out_scale,
                     lambda_init):
    h = pl.program_id(1)
    tk = tq
    slope = slope_ref[h]
    reach = reach_ref[h]
    rows = pl.ds(pl.multiple_of(i * tq, tq), tq)
    q = q_ref[rows, :]
    lane = lax.broadcasted_iota(jnp.int32, q.shape, 1)
    zero = jnp.zeros_like(q)
    q0 = jnp.where(lane < HEAD_DIM, q, zero)
    q1 = jnp.where(lane >= HEAD_DIM, q, zero)
    qq = q.astype(F32)
    qq = (qq * qq).astype(BF16)
    comp = lax.broadcasted_iota(jnp.int32, (B_VDIM, LANES), 0)
    n0 = _dot(qq, jnp.where(comp < HEAD_DIM, 1.0, 0.0).astype(BF16)) * (1.0 + 2.0 ** -7)
    n1 = _dot(qq, jnp.where(comp >= HEAD_DIM, 1.0, 0.0).astype(BF16)) * (1.0 + 2.0 ** -7)
    bound = jnp.concatenate([jnp.sqrt(n0) * jnp.sqrt(k2_sc[0][0:1, :]),
                             jnp.sqrt(n1) * jnp.sqrt(k2_sc[1][0:1, :])], axis=0)
    off = slope * (i * tk).astype(F32)
    qf = qf_ref[...]
    qf2 = jnp.concatenate([qf, qf], axis=0)

    def piece_lanes(x, first_lane):
        lane_ = lax.broadcasted_iota(jnp.int32, x.shape, 1)
        out = jnp.zeros_like(x)
        for n, piece in enumerate(_bf16_split(x)):
            out = jnp.where(lane_ == first_lane + n, piece.astype(F32), out)
        return out.astype(BF16)

    def build_queries(with_bound):
        feats = {_LEFT: -qf2, _DIAG: jnp.zeros_like(qf2), _RIGHT: qf2}
        if with_bound:
            base = piece_lanes(bound, _BOUND_LANE)
            side = piece_lanes(jnp.full((SUBLANES, LANES), off, F32), _OFFSET_LANE)[0:1]
            feats = {_LEFT: base + side - qf2, _DIAG: base, _RIGHT: base - side + qf2}
        for variant, feat in feats.items():
            qa_sc[variant, :, B_VDIM:] = feat
            qa_sc[variant, :tq, :B_VDIM] = q0
            qa_sc[variant, tq:, :B_VDIM] = q1

    def scores(j, s_ref):
        variant = jnp.where(j < i, _LEFT, jnp.where(j == i, _DIAG, _RIGHT))
        start = pl.multiple_of(j * tk, tk)
        s_ref[...] = _dot_nt(qa_sc[variant], ka_sc[pl.ds(start, tk), :])

    def diag_bias(j, s_ref):
        @pl.when(j == i)
        def _():
            bias = slope * dist_ref[...]
            s_ref[:tq] -= bias
            s_ref[tq:] -= bias

    def probs(j, p_ref):
        start = pl.multiple_of(j * tk, tk)
        s = _dot(qa_sc[jnp.where(j < i, _LEFT, _RIGHT)], kt_sc[:, pl.ds(start, tk)])
        p_ref[...] = jnp.exp2(s).astype(BF16)

    def values(j):
        return va_sc[pl.ds(pl.multiple_of(j * tk, tk), tk), :]

    build_queries(True)
    lo = jnp.maximum(i - reach, 0)
    count = jnp.minimum(i + reach + 1, nk) - lo - 1

    def off_diag(n):
        j = lo + jnp.minimum(n, count - 1)
        return j + (j >= i).astype(jnp.int32)

    s_diag = _dot(qa_sc[_DIAG], kt_sc[:, pl.ds(pl.multiple_of(i * tk, tk), tk)])
    probs(off_diag(0), pa_sc)
    bias = slope * dist_ref[...]
    p_diag = jnp.exp2(s_diag - jnp.concatenate([bias, bias], axis=0)).astype(BF16)
    acc_sc[...] = _dot(p_diag, values(i))

    def pair(tt, carry):
        n = 2 * tt
        probs(off_diag(n + 1), pb_sc)
        acc_sc[...] += _dot(pa_sc[...], values(off_diag(n)))
        probs(off_diag(n + 2), pa_sc)
        acc_sc[...] += _dot(pb_sc[...], values(off_diag(n + 1)))
        return carry

    lax.fori_loop(0, count // 2, pair, 0)

    @pl.when(count % 2 == 1)
    def _():
        acc_sc[...] += _dot(pa_sc[...], values(off_diag(count - 1)))

    denom_min = jnp.min(acc_sc[:, B_VDIM:B_VDIM + 1])

    @pl.when(jnp.logical_not(denom_min >= _MIN_DENOM))
    def _():
        build_queries(False)
        m_sc[...] = jnp.full_like(m_sc, NEG_INF)
        acc_sc[...] = jnp.zeros_like(acc_sc)

        def chunk(j, carry):
            scores(j, sa_sc)
            diag_bias(j, sa_sc)
            adj = jnp.where(j > i, off, jnp.where(j < i, -off, 0.0))
            start = pl.multiple_of(j * tk, tk)
            s = sa_sc[...]
            m_prev = m_sc[...]
            m_new = jnp.maximum(m_prev, jnp.max(s, axis=-1, keepdims=True) + adj)
            p = jnp.exp2(s - (m_new - adj)).astype(BF16)
            acc_sc[...] = jnp.exp2(m_prev - m_new) * acc_sc[...] + _dot(p, va_sc[pl.ds(start, tk), :])
            m_sc[...] = m_new
            return carry

        lax.fori_loop(0, nk, chunk, 0)

    lp = lam_ref[...]
    lam = (jnp.exp(jnp.sum(lp[0:1] * lp[1:2], axis=-1, keepdims=True))
           - jnp.exp(jnp.sum(lp[2:3] * lp[3:4], axis=-1, keepdims=True)) + lambda_init)
    acc = acc_sc[...]
    o0 = acc[:tq, :B_VDIM] * (1.0 / acc[:tq, B_VDIM:B_VDIM + 1])
    o1 = acc[tq:, :B_VDIM] * (1.0 / acc[tq:, B_VDIM:B_VDIM + 1])
    o_ref[rows, :] = (_rms(o0 - lam * o1, g_ref[...]) * out_scale).astype(o_ref.dtype)


def _bf16_pieces(x, n=3):
    out = []
    r = np.float64(x)
    for _ in range(n):
        p = np.float64(np.float32(r).astype(jnp.bfloat16).astype(np.float32))
        out.append(p)
        r = r - p
    return out


def _alibi_features(slopes2, t):
    pos = np.arange(t)
    hi = (pos // _POS_SPLIT) * _POS_SPLIT
    lo = pos % _POS_SPLIT
    qf = np.zeros((len(slopes2), t, LANES), np.float32)
    kf = np.zeros((len(slopes2), t, LANES), np.float32)
    for h, s in enumerate(slopes2):
        for n, piece in enumerate(_bf16_pieces(s)):
            for base, part in ((0, hi), (3, lo)):
                qf[h, :, base + n] = -piece
                kf[h, :, base + n] = part
                qf[h, :, 6 + base + n] = part
                kf[h, :, 6 + base + n] = piece
        for n, piece in enumerate(_bf16_pieces(s * t)):
            qf[h, :, _CHUNK_LANE + n] = -piece
        kf[h, :, _BOUND_LANE:_BOUND_LANE + 3] = -1.0
        kf[h, :, _OFFSET_LANE:_OFFSET_LANE + 3] = -1.0
    return jnp.asarray(qf, BF16), jnp.asarray(kf, BF16)


def _diff_attention(proj, b_lambda, subln_g, lambda_init, batch, seq, t=512):
    t = min(t, seq // 2)
    nq = seq // t
    assert seq % (2 * t) == 0
    kcol = MIX_WIDTH // B_VDIM
    vcol = 2 * kcol
    slopes2 = [s * LOG2E for s in _alibi_slopes(B_HEADS)]
    reach = [min(nq, int(math.floor((_ZERO_EXP2_ARG / s - 1.0) / t)) + 1) for s in slopes2]
    qf, kf = _alibi_features(slopes2, t)
    pos = np.arange(t)
    dist = jnp.asarray(np.abs(pos[:, None] - pos[None, :]), F32)
    kernel = functools.partial(_diff_attn_kernel, tk=t, out_scale=1.0 - lambda_init,
                               lambda_init=lambda_init)
    return pl.pallas_call(
        kernel,
        grid_spec=pltpu.PrefetchScalarGridSpec(
            num_scalar_prefetch=2,
            grid=(batch, B_HEADS),
            in_specs=[pl.BlockSpec((seq, B_VDIM), lambda b, h, r, s: (b, h)),
                      pl.BlockSpec((seq, B_VDIM), lambda b, h, r, s: (b, kcol + h)),
                      pl.BlockSpec((seq, B_VDIM), lambda b, h, r, s: (b, vcol + h)),
                      pl.BlockSpec((None, t, LANES), lambda b, h, r, s: (h, 0, 0)),
                      pl.BlockSpec((None, t, LANES), lambda b, h, r, s: (h, 0, 0)),
                      pl.BlockSpec((t, t), lambda b, h, r, s: (0, 0)),
                      pl.BlockSpec((4, HEAD_DIM), lambda b, h, r, s: (0, 0)),
                      pl.BlockSpec((1, B_VDIM), lambda b, h, r, s: (0, 0))],
            out_specs=pl.BlockSpec((seq, B_VDIM), lambda b, h, r, s: (b, h)),
            scratch_shapes=[pltpu.VMEM((3, 2 * t, 2 * LANES), BF16),
                            pltpu.VMEM((seq, 2 * LANES), BF16),
                            pltpu.VMEM((seq, 2 * LANES), BF16),
                            pltpu.VMEM((2, 8, LANES), F32),
                            pltpu.VMEM((2 * t, 1), F32),
                            pltpu.VMEM((2 * t, 2 * LANES), F32),
                            pltpu.VMEM((2 * t, t), F32),
                            pltpu.VMEM((2 * t, t), BF16),
                            pltpu.VMEM((2 * t, t), BF16),
                            pltpu.VMEM((2 * LANES, seq), BF16)],
        ),
        out_shape=jax.ShapeDtypeStruct((batch * seq, MIX_WIDTH), BF16),
        compiler_params=_params("parallel", "parallel"),
        name="diff_attention",
    )(jnp.asarray(reach, jnp.int32), jnp.asarray(slopes2, F32), proj, proj, proj, qf, kf, dist,
      b_lambda, subln_g.reshape(1, B_VDIM))


def _router_kernel(x_ref, g_ref, whi_ref, wlo_ref, h_ref, e_ref, gate_ref):
    h = _rms(x_ref[...], g_ref[...])
    h_hi = h.astype(BF16)
    h_lo = (h - h_hi.astype(F32)).astype(BF16)
    whi = whi_ref[...]
    logits = _dot(h_hi, whi) + (_dot(h_hi, wlo_ref[...]) + _dot(h_lo, whi))
    lane = lax.broadcasted_iota(jnp.int32, logits.shape, 1)
    logits = jnp.where(lane < N_EXPERTS, logits, -jnp.inf)
    m1 = jnp.max(logits, axis=-1, keepdims=True)
    i1 = jnp.min(jnp.where(logits == m1, lane, LANES), axis=-1, keepdims=True)
    rest = jnp.where(lane == i1, -jnp.inf, logits)
    m2 = jnp.max(rest, axis=-1, keepdims=True)
    i2 = jnp.min(jnp.where(rest == m2, lane, LANES), axis=-1, keepdims=True)
    r = jnp.exp(m2 - m1)
    g1 = 1.0 / (1.0 + r)
    g2 = r * g1
    e_ref[...] = jnp.where(lane == 0, i1, jnp.where(lane == 1, i2, 0))
    gate_ref[...] = jnp.where(lane == 0, g1, jnp.where(lane == 1, g2, 0.0))
    h_ref[...] = h_hi


def _router(x, g, w_router, tm=2048):
    t, d = x.shape
    tm = min(tm, t)
    wpad = jnp.zeros((d, LANES), F32).at[:, :N_EXPERTS].set(w_router)
    whi = wpad.astype(BF16)
    wlo = (wpad - whi.astype(F32)).astype(BF16)
    return pl.pallas_call(
        _router_kernel,
        grid=(t // tm,),
        in_specs=[pl.BlockSpec((tm, d), lambda i: (i, 0)),
                  pl.BlockSpec((1, d), lambda i: (0, 0)),
                  pl.BlockSpec((d, LANES), lambda i: (0, 0)),
                  pl.BlockSpec((d, LANES), lambda i: (0, 0))],
        out_specs=[pl.BlockSpec((tm, d), lambda i: (i, 0)),
                   pl.BlockSpec((tm, LANES), lambda i: (i, 0)),
                   pl.BlockSpec((tm, LANES), lambda i: (i, 0))],
        out_shape=[jax.ShapeDtypeStruct((t, d), BF16),
                   jax.ShapeDtypeStruct((t, LANES), jnp.int32),
                   jax.ShapeDtypeStruct((t, LANES), F32)],
        compiler_params=_params("parallel"),
        name="router",
    )(x, g.reshape(1, d), whi, wlo)


def _slab_sizes(rows, tc):
    quarter, half = tc // 4, tc // 2
    return ((quarter, rows <= quarter),
            (half, jnp.logical_and(rows > quarter, rows <= half)),
            (tc, rows > half))


def _slab_stride(slab_ref, n, step, tc):
    most = slab_ref[(n + step) * N_EXPERTS]
    for e in range(1, N_EXPERTS):
        most = jnp.maximum(most, slab_ref[(n + step) * N_EXPERTS + e])
    return jnp.where(most <= tc // 2, tc // 2, tc)


def _dispatch_kernel(slab_ref, fill_ref, h_ref, dest_ref, gate_ref, xs_hbm, gs_hbm,
                     xslab_sc, gslab_sc, zx_sc, zg_sc, sem, zsem):
    i = pl.program_id(0)
    n = pl.num_programs(0)
    slot = i % 2
    tc = h_ref.shape[0]
    tm = zx_sc.shape[0]

    def slab_dma(step, sl, op):
        stride = _slab_stride(slab_ref, n, step, tc)
        for e in range(N_EXPERTS):
            start = pl.multiple_of(slab_ref[step * N_EXPERTS + e], SUBLANES)
            first = pl.multiple_of(e * stride, SUBLANES)
            for size, cond in _slab_sizes(slab_ref[(n + step) * N_EXPERTS + e], tc):
                @pl.when(cond)
                def _():
                    for src, dst in ((xslab_sc, xs_hbm), (gslab_sc, gs_hbm)):
                        copy = pltpu.make_async_copy(src.at[sl, pl.ds(first, size)],
                                                     dst.at[pl.ds(start, size)], sem.at[sl])
                        getattr(copy, op)()

    def fill_copies(start):
        start = pl.multiple_of(start, SUBLANES)
        return [pltpu.make_async_copy(zx_sc, xs_hbm.at[pl.ds(start, tm)], zsem),
                pltpu.make_async_copy(zg_sc, gs_hbm.at[pl.ds(start, tm)], zsem)]

    @pl.when(i == 0)
    def _():
        zx_sc[...] = jnp.zeros_like(zx_sc)
        zg_sc[...] = jnp.zeros_like(zg_sc)
        for half in range(2):
            for e in range(N_EXPERTS):
                for c in fill_copies(fill_ref[half * N_EXPERTS + e]):
                    c.start()
            for e in range(N_EXPERTS):
                for c in fill_copies(fill_ref[half * N_EXPERTS + e]):
                    c.wait()
        first_free = fill_ref[2 * N_EXPERTS]
        n_tiles = xs_hbm.shape[0] // tm

        def start_tile(b, carry):
            for c in fill_copies(b * tm):
                c.start()
            return carry

        def wait_tile(b, carry):
            for c in fill_copies(b * tm):
                c.wait()
            return carry

        lax.fori_loop(first_free, n_tiles, start_tile, 0)
        lax.fori_loop(first_free, n_tiles, wait_tile, 0)

    dest = dest_ref[0]
    gate = gate_ref[...]
    lane = lax.broadcasted_iota(jnp.int32, gate.shape, 1)

    def gate_pieces(col):
        out = jnp.zeros(gate.shape, F32)
        for n_, piece in enumerate(_bf16_split(col)):
            out = jnp.where(lane == n_, piece.astype(F32), out)
        return out.astype(BF16)

    def place_rows(stride):
        pos = lax.broadcasted_iota(jnp.int32, (stride, tc), 0)
        sel0, sel1 = [], []
        for e in range(N_EXPERTS):
            start = slab_ref[i * N_EXPERTS + e]
            sel0.append(jnp.where(dest[0:1, :] - start == pos, 1.0, 0.0))
            sel1.append(jnp.where(dest[1:2, :] - start == pos, 1.0, 0.0))
        sel0 = jnp.concatenate(sel0, axis=0).astype(BF16)
        sel1 = jnp.concatenate(sel1, axis=0).astype(BF16)
        rows = pl.ds(0, N_EXPERTS * stride)
        xslab_sc[slot, rows] = _dot(sel0 + sel1, h_ref[...])
        gslab_sc[slot, rows] = (_dot(sel0, gate_pieces(gate[:, 0:1]))
                                + _dot(sel1, gate_pieces(gate[:, 1:2])))

    compact = _slab_stride(slab_ref, n, i, tc) < tc
    pl.when(compact)(lambda: place_rows(tc // 2))
    pl.when(jnp.logical_not(compact))(lambda: place_rows(tc))

    @pl.when(i > 0)
    def _():
        slab_dma(i - 1, 1 - slot, "wait")

    slab_dma(i, slot, "start")

    @pl.when(i == n - 1)
    def _():
        slab_dma(i, slot, "wait")


def _dispatch(h, dest_t, gates, slab, fill, p_rows, tm, tc):
    t, d = h.shape
    nb = t // tc
    return pl.pallas_call(
        _dispatch_kernel,
        grid_spec=pltpu.PrefetchScalarGridSpec(
            num_scalar_prefetch=2,
            grid=(nb,),
            in_specs=[pl.BlockSpec((tc, d), lambda i, s, z: (i, 0)),
                      pl.BlockSpec((1, TOP_K, tc), lambda i, s, z: (i, 0, 0)),
                      pl.BlockSpec((tc, LANES), lambda i, s, z: (i, 0))],
            out_specs=[pl.BlockSpec(memory_space=pl.ANY), pl.BlockSpec(memory_space=pl.ANY)],
            scratch_shapes=[pltpu.VMEM((2, N_EXPERTS * tc, d), F32),
                            pltpu.VMEM((2, N_EXPERTS * tc, LANES), F32),
                            pltpu.VMEM((tm, d), F32), pltpu.VMEM((tm, LANES), F32),
                            pltpu.SemaphoreType.DMA((2,)), pltpu.SemaphoreType.DMA(())],
        ),
        out_shape=[jax.ShapeDtypeStruct((p_rows, d), F32), jax.ShapeDtypeStruct((p_rows, LANES), F32)],
        compiler_params=_params("arbitrary"),
        name="moe_dispatch",
    )(slab, fill, h, dest_t, gates)


def _moe_kernel(blk_e_ref, used_ref, xblk_ref, x_ref, gs_ref, wg_ref, wu_ref, wd_ref, o_ref,
                xb_sc, acc_sc):
    i = pl.program_id(0)
    f = pl.program_id(1)
    used = used_ref[i] > 0

    @pl.when(jnp.logical_and(used, f == 0))
    def _():
        xb_sc[...] = x_ref[...].astype(BF16)
        acc_sc[...] = jnp.zeros_like(acc_sc)

    @pl.when(used)
    def _():
        acc_sc[...] += _swiglu_tile(xb_sc[...], wg_ref, wu_ref, wd_ref)

    @pl.when(f == pl.num_programs(1) - 1)
    def _():
        gs = gs_ref[...]
        gate = gs[:, 0:1] + gs[:, 1:2] + gs[:, 2:3]
        o_ref[...] = jnp.where(used, acc_sc[...] * gate, 0.0)


def _moe_experts(xs, gs, blk_e, blk_used, xblk, wg, wu, wd, tm, tf=3584):
    p, d = xs.shape
    nblk = p // tm
    nf = wg.shape[2] // tf

    def fidx(i, f, used_ref):
        return jnp.where(used_ref[i] > 0, f, nf - 1)

    return pl.pallas_call(
        _moe_kernel,
        grid_spec=pltpu.PrefetchScalarGridSpec(
            num_scalar_prefetch=3,
            grid=(nblk, nf),
            in_specs=[pl.BlockSpec((tm, d), lambda i, f, be, us, xb: (xb[i], 0)),
                      pl.BlockSpec((tm, LANES), lambda i, f, be, us, xb: (xb[i], 0)),
                      pl.BlockSpec((None, d, tf), lambda i, f, be, us, xb: (be[i], 0, fidx(i, f, us))),
                      pl.BlockSpec((None, d, tf), lambda i, f, be, us, xb: (be[i], 0, fidx(i, f, us))),
                      pl.BlockSpec((None, tf, d), lambda i, f, be, us, xb: (be[i], fidx(i, f, us), 0))],
            out_specs=pl.BlockSpec((tm, d), lambda i, f, be, us, xb: (i, 0)),
            scratch_shapes=[pltpu.VMEM((tm, d), BF16), pltpu.VMEM((tm, d), F32)],
        ),
        out_shape=jax.ShapeDtypeStruct((p, d), F32),
        compiler_params=_params("arbitrary", "arbitrary"),
        name="moe_experts",
    )(blk_e, blk_used, xblk, xs, gs, wg, wu, wd)


def _combine_kernel(slab_ref, ys_hbm, dest_ref, x_ref, g_ref, o_ref, slab_sc, sem):
    i = pl.program_id(0)
    n = pl.num_programs(0)
    slot = i % 2
    tc = x_ref.shape[0]

    def slab_dma(step, sl, op):
        stride = _slab_stride(slab_ref, n, step, tc)
        for e in range(N_EXPERTS):
            start = pl.multiple_of(slab_ref[step * N_EXPERTS + e], SUBLANES)
            first = pl.multiple_of(e * stride, SUBLANES)
            for size, cond in _slab_sizes(slab_ref[(n + step) * N_EXPERTS + e], tc):
                @pl.when(cond)
                def _():
                    copy = pltpu.make_async_copy(ys_hbm.at[pl.ds(start, size)],
                                                 slab_sc.at[sl, pl.ds(first, size)], sem.at[sl])
                    getattr(copy, op)()

    @pl.when(i == 0)
    def _():
        slab_sc[...] = jnp.zeros_like(slab_sc)
        slab_dma(0, 0, "start")

    @pl.when(i + 1 < n)
    def _():
        slab_dma(i + 1, 1 - slot, "start")

    dest = dest_ref[...]
    slab_dma(i, slot, "wait")

    def gather_rows(stride):
        pos = lax.broadcasted_iota(jnp.int32, (tc, stride), 1)
        sel = []
        for e in range(N_EXPERTS):
            start = slab_ref[i * N_EXPERTS + e]
            hit = jnp.logical_or(dest[:, 0:1] - start == pos, dest[:, 1:2] - start == pos)
            sel.append(jnp.where(hit, 1.0, 0.0))
        sel = jnp.concatenate(sel, axis=1).astype(BF16)
        y = _dot(sel, slab_sc[slot, pl.ds(0, N_EXPERTS * stride)].astype(BF16))
        o_ref[...] = x_ref[...] + _rms(y, g_ref[...])

    compact = _slab_stride(slab_ref, n, i, tc) < tc
    pl.when(compact)(lambda: gather_rows(tc // 2))
    pl.when(jnp.logical_not(compact))(lambda: gather_rows(tc))


def _combine(ys, dest, slab, x, g, tc):
    t, d = x.shape
    return pl.pallas_call(
        _combine_kernel,
        grid_spec=pltpu.PrefetchScalarGridSpec(
            num_scalar_prefetch=1,
            grid=(t // tc,),
            in_specs=[pl.BlockSpec(memory_space=pl.ANY),
                      pl.BlockSpec((tc, LANES), lambda i, s: (i, 0)),
                      pl.BlockSpec((tc, d), lambda i, s: (i, 0)),
                      pl.BlockSpec((1, d), lambda i, s: (0, 0))],
            out_specs=pl.BlockSpec((tc, d), lambda i, s: (i, 0)),
            scratch_shapes=[pltpu.VMEM((2, N_EXPERTS * tc, d), F32), pltpu.SemaphoreType.DMA((2,))],
        ),
        out_shape=jax.ShapeDtypeStruct((t, d), F32),
        compiler_params=_params("arbitrary"),
        name="moe_combine",
    )(slab, ys, dest, x, g.reshape(1, d))


def _route(top_e, tm, tc):
    t = top_e.shape[0]
    a = t * TOP_K
    nb = t // tc
    e_flat = top_e.reshape(a)
    onehot = (e_flat[:, None] == jnp.arange(N_EXPERTS, dtype=jnp.int32)[None, :]).astype(jnp.int32)
    csum = jnp.cumsum(onehot, axis=0)
    rank = jnp.sum(onehot * (csum - 1), axis=1)
    upto = csum[TOP_K * tc - 1::TOP_K * tc]
    before = jnp.concatenate([jnp.zeros((1, N_EXPERTS), jnp.int32), upto[:nb - 1]], axis=0)
    rows = (upto - before + SUBLANES - 1) // SUBLANES * SUBLANES
    rows_before = jnp.cumsum(rows, axis=0) - rows
    total = jnp.sum(rows, axis=0)
    region = (total + 2 * tm - 1) // tm * tm
    rends = jnp.cumsum(region)
    rstarts = rends - region
    slab = (rstarts[None, :] + rows_before).astype(jnp.int32)
    shift = jnp.broadcast_to((slab - before)[:, None, :], (nb, TOP_K * tc, N_EXPERTS)).reshape(a, N_EXPERTS)
    dest = (rank + jnp.sum(onehot * shift, axis=1)).astype(jnp.int32).reshape(t, TOP_K)
    slab = jnp.concatenate([slab.reshape(nb * N_EXPERTS), rows.reshape(nb * N_EXPERTS)])
    row_end = rstarts + total
    fill = jnp.concatenate([row_end, rends - tm, rends[-1:] // tm]).astype(jnp.int32)
    nblk = -(-(a + nb * N_EXPERTS * (SUBLANES - 1)) // tm) + 2 * N_EXPERTS
    blk_start = jnp.arange(nblk, dtype=jnp.int32) * tm
    blk_e = jnp.minimum(jnp.sum((rends[None, :] <= blk_start[:, None]).astype(jnp.int32), axis=1),
                        N_EXPERTS - 1)
    blk_used = jnp.logical_and(blk_start < row_end[blk_e], blk_start < rends[-1]).astype(jnp.int32)
    first_used = jnp.argmax(blk_used).astype(jnp.int32)
    xblk = jnp.where(blk_used > 0, jnp.arange(nblk, dtype=jnp.int32), first_used)
    return dest, slab, fill, blk_e, blk_used, xblk, nblk * tm


def _moe(x, h, e_pad, gates, g_out, wg, wu, wd, tm=512, tc=256):
    t = x.shape[0]
    tc = min(tc, t)
    dest, slab, fill, blk_e, blk_used, xblk, p_rows = _route(e_pad[:, :TOP_K], tm, tc)
    dest_t = dest.reshape(t // tc, tc, TOP_K).transpose(0, 2, 1)
    dest_pad = jnp.zeros((t, LANES), jnp.int32).at[:, :TOP_K].set(dest)
    xs, gs = _dispatch(h, dest_t, gates, slab, fill, p_rows, tm, tc)
    ys = _moe_experts(xs, gs, blk_e, blk_used, xblk, wg, wu, wd, tm)
    return _combine(ys, dest_pad, slab, x, g_out, tc)


def kernel(x, mem, norm_g, mem_norm_g, w_mem_kv, a_w_in, a_sink, a_w_out, b_w_in, b_lambda,
           b_subln_g, b_w_out, ffn_w_gate, ffn_w_up, ffn_w_down, moe_w_router, moe_w_gate,
           moe_w_up, moe_w_down):
    batch, seq, d = x.shape
    n_mem = mem.shape[1]
    bf = lambda w: w.astype(BF16)
    xt = x.reshape(batch * seq, d)
    mkv = _norm_matmul(mem.reshape(batch * n_mem, d), mem_norm_g, bf(w_mem_kv),
                       jnp.ones((w_mem_kv.shape[1],), F32))

    g = norm_g[0]
    proj = _norm_matmul(xt, g[0], bf(a_w_in[0]), _query_col_scale(a_w_in.shape[2], A_Q_HEADS * HEAD_DIM))
    mix = _window_attention(proj, a_sink[0], batch, seq)
    mem_out = _mem_attention(proj, (A_Q_HEADS + 2 * A_KV_HEADS) * HEAD_DIM // MEM_WIDTH, mkv, batch, seq)
    xt = _out_proj(mix, mem_out, bf(a_w_out[0]), xt, g[1])
    xt = _ffn(xt, g[2], g[3], bf(ffn_w_gate[0]), bf(ffn_w_up[0]), bf(ffn_w_down[0]))

    g = norm_g[1]
    lambda_init = 0.8 - 0.6 * math.exp(-0.3 * 1)
    proj = _norm_matmul(xt, g[0], bf(b_w_in[0]), _query_col_scale(b_w_in.shape[2], MIX_WIDTH))
    mix = _diff_attention(proj, b_lambda[0], b_subln_g[0], lambda_init, batch, seq)
    mem_out = _mem_attention(proj, 3 * MIX_WIDTH // MEM_WIDTH, mkv, batch, seq)
    xt = _out_proj(mix, mem_out, bf(b_w_out[0]), xt, g[1])
    h, e_pad, gates = _router(xt, g[2], moe_w_router[0])
    xt = _moe(xt, h, e_pad, gates, g[3], bf(moe_w_gate[0]), bf(moe_w_up[0]), bf(moe_w_down[0]))
    return xt.reshape(batch, seq, d)
```

```python
import functools
import math

import jax
import jax.numpy as jnp
import numpy as np
from jax import lax
from jax.experimental import pallas as pl
from jax.experimental.pallas import tpu as pltpu

D_MODEL = 1024
HEAD_DIM = 64
MIX_WIDTH = 768
MEM_WIDTH = 256
MEM_HEADS = 4
A_Q_HEADS = 12
A_KV_HEADS = 4
A_GROUP = 3
WINDOW = 128
B_HEADS = 6
B_VDIM = 128
D_FF = 3584
N_EXPERTS = 8
TOP_K = 2
EPS = 1e-6
NEG_INF = -1e30
LOG2E = math.log2(math.e)
QK_SCALE = HEAD_DIM ** -0.5 * LOG2E
LANES = 128
SUBLANES = 8
VMEM_LIMIT_BYTES = 56 * 1024 * 1024

BF16 = jnp.bfloat16
F32 = jnp.float32


def _params(*sem):
    return pltpu.CompilerParams(dimension_semantics=sem, vmem_limit_bytes=VMEM_LIMIT_BYTES)


def _rms(x, g):
    return x * lax.rsqrt(jnp.mean(x * x, axis=-1, keepdims=True) + EPS) * g


def _dot(a, b):
    return jnp.dot(a, b, preferred_element_type=F32)


def _dot_nt(a, b):
    return lax.dot_general(a, b, (((1,), (1,)), ((), ())), preferred_element_type=F32)


def _alibi_slopes(n):
    return [2.0 ** (-8.0 * (i + 1) / n) for i in range(n)]


def _norm_matmul_kernel(x_ref, g_ref, w_ref, cs_ref, o_ref, *, chunk):
    h = _rms(x_ref[...], g_ref[...]).astype(BF16)
    for c in range(o_ref.shape[1] // chunk):
        sl = slice(c * chunk, (c + 1) * chunk)
        o_ref[:, sl] = (_dot(h, w_ref[:, sl]) * cs_ref[:, sl]).astype(o_ref.dtype)


def _norm_matmul(x, g, w, col_scale, tm=1024, chunk=512):
    t, d = x.shape
    n = w.shape[1]
    tm = min(tm, t)
    return pl.pallas_call(
        functools.partial(_norm_matmul_kernel, chunk=min(chunk, n)),
        grid=(t // tm,),
        in_specs=[pl.BlockSpec((tm, d), lambda i: (i, 0)),
                  pl.BlockSpec((1, d), lambda i: (0, 0)),
                  pl.BlockSpec((d, n), lambda i: (0, 0)),
                  pl.BlockSpec((1, n), lambda i: (0, 0))],
        out_specs=pl.BlockSpec((tm, n), lambda i: (i, 0)),
        out_shape=jax.ShapeDtypeStruct((t, n), BF16),
        compiler_params=_params("parallel"),
        name="norm_matmul",
    )(x, g.reshape(1, d), w, col_scale.reshape(1, n))


def _query_col_scale(n, mix_q_cols):
    cols = np.arange(n)
    return jnp.asarray(np.where((cols < mix_q_cols) | (cols >= n - MEM_WIDTH), QK_SCALE, 1.0), F32)


def _window_kernel(q_ref, kp_ref, kc_ref, kn_ref, vp_ref, vc_ref, vn_ref, bias_ref, o_ref):
    n = pl.program_id(1)
    nb = pl.num_programs(1)
    blk = kp_ref.shape[0]
    sub = q_ref.shape[0] // blk
    zeros = jnp.zeros((blk, kp_ref.shape[1]), kp_ref.dtype)
    k_all = jnp.concatenate([kp_ref[...], kc_ref[...], kn_ref[...]], axis=0)
    v_all = jnp.concatenate([vp_ref[...], vc_ref[...], vn_ref[...]], axis=0)
    col = lax.broadcasted_iota(jnp.int32, (1, 4 * blk), 1)
    half = lax.broadcasted_iota(jnp.int32, (4 * blk, LANES), 1) < HEAD_DIM
    low_half = jnp.where(half, 1.0, 0.0).astype(BF16)
    high_half = jnp.where(half, 0.0, 1.0).astype(BF16)
    for u in range(sub):
        k = jnp.concatenate([k_all[u * blk:(u + 3) * blk], zeros], axis=0)
        v = jnp.concatenate([v_all[u * blk:(u + 3) * blk], zeros], axis=0)
        edge = jnp.zeros((1, 4 * blk), F32)
        if u == 0:
            edge = jnp.where(jnp.logical_and(n == 0, col < blk), NEG_INF, edge)
        if u == sub - 1:
            edge = jnp.where(jnp.logical_and(n == nb - 1,
                                             jnp.logical_and(col >= 2 * blk, col < 3 * blk)), NEG_INF, edge)
        q = q_ref[u * blk:(u + 1) * blk, :]
        outs = []
        for kh in range(A_KV_HEADS):
            heads = range(kh * A_GROUP, (kh + 1) * A_GROUP)
            kv = slice(kh * HEAD_DIM, (kh + 1) * HEAD_DIM)
            qs = jnp.concatenate([q[:, h * HEAD_DIM:(h + 1) * HEAD_DIM] for h in heads], axis=0)
            s = _dot_nt(qs, k[:, kv]) + bias_ref[kh] + edge
            p = jnp.exp2(s - jnp.max(s, axis=-1, keepdims=True)).astype(BF16)
            low = kh % 2 == 0
            slab = v[:, (kh // 2) * LANES:(kh // 2 + 1) * LANES]
            own, other = (low_half, high_half) if low else (high_half, low_half)
            ov = _dot(p, jnp.concatenate([slab * own + other, own], axis=1))
            den = ov[:, LANES:]
            o = ov[:, :LANES] * (1.0 / jnp.where(den == 0.0, 1.0, den))
            o = o[:, :HEAD_DIM] if low else o[:, HEAD_DIM:]
            outs.extend(o[g * blk:(g + 1) * blk] for g in range(A_GROUP))
        o_ref[u * blk:(u + 1) * blk, :] = jnp.concatenate(outs, axis=-1).astype(o_ref.dtype)


def _window_bias(blk, sink):
    qi = np.arange(blk)[:, None]
    kj = np.arange(3 * blk)[None, :]
    dist = np.abs(blk + qi - kj)
    slopes = _alibi_slopes(A_Q_HEADS)
    tables = np.stack([np.where(dist <= WINDOW, -(slopes[h] * LOG2E) * dist, NEG_INF)
                       for h in range(A_Q_HEADS)])
    pad = jnp.full((A_Q_HEADS, blk, blk), NEG_INF, F32)
    pad = pad.at[:, :, 0].set(jnp.broadcast_to((sink.astype(F32) * LOG2E)[:, None], (A_Q_HEADS, blk)))
    full = jnp.concatenate([jnp.asarray(tables, F32), pad], axis=2)
    return full.reshape(A_KV_HEADS, A_GROUP * blk, 4 * blk)


def _window_attention(proj, sink, batch, seq, sub=4):
    blk = WINDOW
    nb = seq // blk
    sub = min(sub, nb)
    ns = nb // sub
    kcol = MIX_WIDTH // 256
    vcol = kcol + 1
    bias = _window_bias(blk, sink)

    def cur(col):
        return pl.BlockSpec((sub * blk, 256), lambda b, n: (b * ns + n, col))

    def prev(col):
        return pl.BlockSpec((blk, 256), lambda b, n: (b * nb + jnp.maximum(n * sub - 1, 0), col))

    def nxt(col):
        return pl.BlockSpec((blk, 256), lambda b, n: (b * nb + jnp.minimum((n + 1) * sub, nb - 1), col))

    return pl.pallas_call(
        _window_kernel,
        grid=(batch, ns),
        in_specs=[pl.BlockSpec((sub * blk, MIX_WIDTH), lambda b, n: (b * ns + n, 0)),
                  prev(kcol), cur(kcol), nxt(kcol), prev(vcol), cur(vcol), nxt(vcol),
                  pl.BlockSpec(bias.shape, lambda b, n: (0, 0, 0))],
        out_specs=pl.BlockSpec((sub * blk, MIX_WIDTH), lambda b, n: (b * ns + n, 0)),
        out_shape=jax.ShapeDtypeStruct((batch * seq, MIX_WIDTH), BF16),
        compiler_params=_params("parallel", "parallel"),
        name="window_attention",
    )(proj, proj, proj, proj, proj, proj, proj, bias)


def _mem_attn_kernel(q_ref, mkv_ref, o_ref):
    q = q_ref[...]
    mkv = mkv_ref[...]
    ones = jnp.ones((mkv.shape[0], LANES), BF16)
    outs = []
    for h in range(MEM_HEADS):
        sl = slice(h * HEAD_DIM, (h + 1) * HEAD_DIM)
        s = _dot_nt(q[:, sl], mkv[:, sl])
        p = jnp.exp2(s - jnp.max(s, axis=-1, keepdims=True)).astype(BF16)
        o = _dot(p, mkv[:, MEM_WIDTH + h * HEAD_DIM:MEM_WIDTH + (h + 1) * HEAD_DIM])
        outs.append(o * (1.0 / _dot(p, ones)[:, :HEAD_DIM]))
    o_ref[...] = jnp.concatenate(outs, axis=-1).astype(o_ref.dtype)


def _mem_attention(proj, qcol, mkv, batch, seq, tm=2048):
    tm = min(tm, seq)
    per_batch = seq // tm
    n_mem = mkv.shape[0] // batch
    return pl.pallas_call(
        _mem_attn_kernel,
        grid=(batch * per_batch,),
        in_specs=[pl.BlockSpec((tm, MEM_WIDTH), lambda i: (i, qcol)),
                  pl.BlockSpec((n_mem, 2 * MEM_WIDTH), lambda i: (i // per_batch, 0))],
        out_specs=pl.BlockSpec((tm, MEM_WIDTH), lambda i: (i, 0)),
        out_shape=jax.ShapeDtypeStruct((batch * seq, MEM_WIDTH), BF16),
        compiler_params=_params("parallel"),
        name="mem_attention",
    )(proj, mkv)


def _out_proj_kernel(mix_ref, mem_ref, w1_ref, w2_ref, x_ref, g_ref, o_ref):
    o = _dot(mix_ref[...], w1_ref[...]) + _dot(mem_ref[...], w2_ref[...])
    o_ref[...] = x_ref[...] + _rms(o, g_ref[...])


def _out_proj(mix, mem_out, w_out, x, g, tm=1024):
    t, d = x.shape
    tm = min(tm, t)
    w1 = w_out[:MIX_WIDTH]
    w2 = w_out[MIX_WIDTH:]
    return pl.pallas_call(
        _out_proj_kernel,
        grid=(t // tm,),
        in_specs=[pl.BlockSpec((tm, MIX_WIDTH), lambda i: (i, 0)),
                  pl.BlockSpec((tm, MEM_WIDTH), lambda i: (i, 0)),
                  pl.BlockSpec((MIX_WIDTH, d), lambda i: (0, 0)),
                  pl.BlockSpec((MEM_WIDTH, d), lambda i: (0, 0)),
                  pl.BlockSpec((tm, d), lambda i: (i, 0)),
                  pl.BlockSpec((1, d), lambda i: (0, 0))],
        out_specs=pl.BlockSpec((tm, d), lambda i: (i, 0)),
        out_shape=jax.ShapeDtypeStruct((t, d), F32),
        compiler_params=_params("parallel"),
        name="out_proj",
    )(mix, mem_out, w1, w2, x, g.reshape(1, d))


def _swiglu_tile(h, wg_ref, wu_ref, wd_ref, chunk=256):
    out = None
    for c in range(wg_ref.shape[1] // chunk):
        sl = slice(c * chunk, (c + 1) * chunk)
        a = _dot(h, wg_ref[:, sl])
        u = _dot(h, wu_ref[:, sl])
        z = (a * jax.nn.sigmoid(a) * u).astype(BF16)
        y = _dot(z, wd_ref[sl, :])
        out = y if out is None else out + y
    return out


def _ffn_kernel(x_ref, gin_ref, gout_ref, wg_ref, wu_ref, wd_ref, o_ref):
    x = x_ref[...]
    h = _rms(x, gin_ref[...]).astype(BF16)
    o_ref[...] = x + _rms(_swiglu_tile(h, wg_ref, wu_ref, wd_ref), gout_ref[...])


def _ffn(x, g_in, g_out, wg, wu, wd, tm=512):
    t, d = x.shape
    ff = wg.shape[1]
    tm = min(tm, t)
    return pl.pallas_call(
        _ffn_kernel,
        grid=(t // tm,),
        in_specs=[pl.BlockSpec((tm, d), lambda i: (i, 0)),
                  pl.BlockSpec((1, d), lambda i: (0, 0)),
                  pl.BlockSpec((1, d), lambda i: (0, 0)),
                  pl.BlockSpec((d, ff), lambda i: (0, 0)),
                  pl.BlockSpec((d, ff), lambda i: (0, 0)),
                  pl.BlockSpec((ff, d), lambda i: (0, 0))],
        out_specs=pl.BlockSpec((tm, d), lambda i: (i, 0)),
        out_shape=jax.ShapeDtypeStruct((t, d), F32),
        compiler_params=_params("parallel"),
        name="ffn",
    )(x, g_in.reshape(1, d), g_out.reshape(1, d), wg, wu, wd)


_POS_SPLIT = 256
_LEFT, _DIAG, _RIGHT = 0, 1, 2
_BOUND_LANE = 12
_CHUNK_LANE = 15
_OFFSET_LANE = 18
_ZERO_EXP2_ARG = 136.0
_MIN_DENOM = 2.0 ** -60


def _bf16_split(x):
    hi = x.astype(BF16)
    r = x - hi.astype(F32)
    mid = r.astype(BF16)
    lo = (r - mid.astype(F32)).astype(BF16)
    return hi, mid, lo


def _diff_attn_kernel(reach_ref, slope_ref, q_ref, k_ref, v_ref, qf_ref, kf_ref, dist_ref, lam_ref,
                      g_ref, o_ref, qa_sc, ka_sc, va_sc, k2_sc, m_sc, acc_sc, sa_sc, pa_sc, pb_sc, kt_sc, *, tk,
                      out_scale, lambda_init):
    seq = k_ref.shape[0]
    nk = seq // tk
    k = k_ref[...]
    ka_sc[:, :B_VDIM] = k
    va_sc[:, :B_VDIM] = v_ref[...]
    lane = lax.broadcasted_iota(jnp.int32, (seq, LANES), 1)
    va_sc[:, B_VDIM:] = jnp.where(lane == 0, 1.0, 0.0).astype(BF16)
    kf = kf_ref[...]
    flane = lax.broadcasted_iota(jnp.int32, kf.shape, 1)
    chunk_lanes = jnp.logical_and(flane >= _CHUNK_LANE, flane < _CHUNK_LANE + 3)
    for c in range(nk):
        ka_sc[c * tk:(c + 1) * tk, B_VDIM:] = jnp.where(chunk_lanes, float(c), kf).astype(BF16)
    for c in range(nk):
        kt_sc[:, c * tk:(c + 1) * tk] = ka_sc[c * tk:(c + 1) * tk, :].T
    kk = k.astype(F32)
    kk = kk * kk
    for c, sel in enumerate((lane < HEAD_DIM, lane >= HEAD_DIM)):
        n2 = jnp.sum(jnp.where(sel, kk, 0.0), axis=-1, keepdims=True)
        k2_sc[c] = jnp.broadcast_to(jnp.max(n2, axis=0, keepdims=True), k2_sc.shape[1:])

    def body(i, carry):
        i = jnp.asarray(i, jnp.int32)
        _diff_attn_block(i, reach_ref, slope_ref, q_ref, qf_ref, dist_ref, lam_ref, g_ref, o_ref, qa_sc,
                         ka_sc, va_sc, k2_sc, m_sc, acc_sc, sa_sc, pa_sc, pb_sc, kt_sc, tq=tk, nk=nk,
                         out_scale=out_scale, lambda_init=lambda_init)
        return carry

    lax.fori_loop(0, nk, body, 0)


def _diff_attn_block(i, reach_ref, slope_ref, q_ref, qf_ref, dist_ref, lam_ref, g_ref, o_ref, qa_sc,
                     ka_sc, va_sc, k2_sc, m_sc, acc_sc, sa_sc, pa_sc, pb_sc, kt_sc, *, tq, nk, out_scale,
                     lambda_init):
    h = pl.program_id(1)
    tk = tq
    slope = slope_ref[h]
    reach = reach_ref[h]
    rows = pl.ds(pl.multiple_of(i * tq, tq), tq)
    q = q_ref[rows, :]
    lane = lax.broadcasted_iota(jnp.int32, q.shape, 1)
    zero = jnp.zeros_like(q)
    q0 = jnp.where(lane < HEAD_DIM, q, zero)
    q1 = jnp.where(lane >= HEAD_DIM, q, zero)
    qq = q.astype(F32)
    qq = (qq * qq).astype(BF16)
    comp = lax.broadcasted_iota(jnp.int32, (B_VDIM, LANES), 0)
    n0 = _dot(qq, jnp.where(comp < HEAD_DIM, 1.0, 0.0).astype(BF16)) * (1.0 + 2.0 ** -7)
    n1 = _dot(qq, jnp.where(comp >= HEAD_DIM, 1.0, 0.0).astype(BF16)) * (1.0 + 2.0 ** -7)
    bound = jnp.concatenate([jnp.sqrt(n0) * jnp.sqrt(k2_sc[0][0:1, :]),
                             jnp.sqrt(n1) * jnp.sqrt(k2_sc[1][0:1, :])], axis=0)
    off = slope * (i * tk).astype(F32)
    qf = qf_ref[...]
    qf2 = jnp.concatenate([qf, qf], axis=0)

    def piece_lanes(x, first_lane):
        lane_ = lax.broadcasted_iota(jnp.int32, x.shape, 1)
        out = jnp.zeros_like(x)
        for n, piece in enumerate(_bf16_split(x)):
            out = jnp.where(lane_ == first_lane + n, piece.astype(F32), out)
        return out.astype(BF16)

    def build_queries(with_bound):
        feats = {_LEFT: -qf2, _DIAG: jnp.zeros_like(qf2), _RIGHT: qf2}
        if with_bound:
            base = piece_lanes(bound, _BOUND_LANE)
            side = piece_lanes(jnp.full((SUBLANES, LANES), off, F32), _OFFSET_LANE)[0:1]
            feats = {_LEFT: base + side - qf2, _DIAG: base, _RIGHT: base - side + qf2}
        for variant, feat in feats.items():
            qa_sc[variant, :, B_VDIM:] = feat
            qa_sc[variant, :tq, :B_VDIM] = q0
            qa_sc[variant, tq:, :B_VDIM] = q1

    def scores(j, s_ref):
        variant = jnp.where(j < i, _LEFT, jnp.where(j == i, _DIAG, _RIGHT))
        start = pl.multiple_of(j * tk, tk)
        s_ref[...] = _dot_nt(qa_sc[variant], ka_sc[pl.ds(start, tk), :])

    def diag_bias(j, s_ref):
        @pl.when(j == i)
        def _():
            bias = slope * dist_ref[...]
            s_ref[:tq] -= bias
            s_ref[tq:] -= bias

    def probs(j, p_ref):
        start = pl.multiple_of(j * tk, tk)
        s = _dot(qa_sc[jnp.where(j < i, _LEFT, _RIGHT)], kt_sc[:, pl.ds(start, tk)])
        p_ref[...] = jnp.exp2(s).astype(BF16)

    def values(j):
        return va_sc[pl.ds(pl.multiple_of(j * tk, tk), tk), :]

    build_queries(True)
    lo = jnp.maximum(i - reach, 0)
    count = jnp.minimum(i + reach + 1, nk) - lo - 1

    def off_diag(n):
        j = lo + jnp.minimum(n, count - 1)
        return j + (j >= i).astype(jnp.int32)

    s_diag = _dot(qa_sc[_DIAG], kt_sc[:, pl.ds(pl.multiple_of(i * tk, tk), tk)])
    probs(off_diag(0), pa_sc)
    bias = slope * dist_ref[...]
    p_diag = jnp.exp2(s_diag - jnp.concatenate([bias, bias], axis=0)).astype(BF16)
    acc_sc[...] = _dot(p_diag, values(i))

    def pair(tt, carry):
        n = 2 * tt
        probs(off_diag(n + 1), pb_sc)
        acc_sc[...] += _dot(pa_sc[...], values(off_diag(n)))
        probs(off_diag(n + 2), pa_sc)
        acc_sc[...] += _dot(pb_sc[...], values(off_diag(n + 1)))
        return carry

    lax.fori_loop(0, count // 2, pair, 0)

    @pl.when(count % 2 == 1)
    def _():
        acc_sc[...] += _dot(pa_sc[...], values(off_diag(count - 1)))

    denom_min = jnp.min(acc_sc[:, B_VDIM:B_VDIM + 1])

    @pl.when(jnp.logical_not(denom_min >= _MIN_DENOM))
    def _():
        build_queries(False)
        m_sc[...] = jnp.full_like(m_sc, NEG_INF)
        acc_sc[...] = jnp.zeros_like(acc_sc)

        def chunk(j, carry):
            scores(j, sa_sc)
            diag_bias(j, sa_sc)
            adj = jnp.where(j > i, off, jnp.where(j < i, -off, 0.0))
            start = pl.multiple_of(j * tk, tk)
            s = sa_sc[...]
            m_prev = m_sc[...]
            m_new = jnp.maximum(m_prev, jnp.max(s, axis=-1, keepdims=True) + adj)
            p = jnp.exp2(s - (m_new - adj)).astype(BF16)
            acc_sc[...] = jnp.exp2(m_prev - m_new) * acc_sc[...] + _dot(p, va_sc[pl.ds(start, tk), :])
            m_sc[...] = m_new
            return carry

        lax.fori_loop(0, nk, chunk, 0)

    lp = lam_ref[...]
    lam = (jnp.exp(jnp.sum(lp[0:1] * lp[1:2], axis=-1, keepdims=True))
           - jnp.exp(jnp.sum(lp[2:3] * lp[3:4], axis=-1, keepdims=True)) + lambda_init)
    acc = acc_sc[...]
    o0 = acc[:tq, :B_VDIM] * (1.0 / acc[:tq, B_VDIM:B_VDIM + 1])
    o1 = acc[tq:, :B_VDIM] * (1.0 / acc[tq:, B_VDIM:B_VDIM + 1])
    o_ref[rows, :] = (_rms(o0 - lam * o1, g_ref[...]) * out_scale).astype(o_ref.dtype)


def _bf16_pieces(x, n=3):
    out = []
    r = np.float64(x)
    for _ in range(n):
        p = np.float64(np.float32(r).astype(jnp.bfloat16).astype(np.float32))
        out.append(p)
        r = r - p
    return out


def _alibi_features(slopes2, t):
    pos = np.arange(t)
    hi = (pos // _POS_SPLIT) * _POS_SPLIT
    lo = pos % _POS_SPLIT
    qf = np.zeros((len(slopes2), t, LANES), np.float32)
    kf = np.zeros((len(slopes2), t, LANES), np.float32)
    for h, s in enumerate(slopes2):
        for n, piece in enumerate(_bf16_pieces(s)):
            for base, part in ((0, hi), (3, lo)):
                qf[h, :, base + n] = -piece
                kf[h, :, base + n] = part
                qf[h, :, 6 + base + n] = part
                kf[h, :, 6 + base + n] = piece
        for n, piece in enumerate(_bf16_pieces(s * t)):
            qf[h, :, _CHUNK_LANE + n] = -piece
        kf[h, :, _BOUND_LANE:_BOUND_LANE + 3] = -1.0
        kf[h, :, _OFFSET_LANE:_OFFSET_LANE + 3] = -1.0
    return jnp.asarray(qf, BF16), jnp.asarray(kf, BF16)


def _diff_attention(proj, b_lambda, subln_g, lambda_init, batch, seq, t=512):
    t = min(t, seq // 2)
    nq = seq // t
    assert seq % (2 * t) == 0
    kcol = MIX_WIDTH // B_VDIM
    vcol = 2 * kcol
    slopes2 = [s * LOG2E for s in _alibi_slopes(B_HEADS)]
    reach = [min(nq, int(math.floor((_ZERO_EXP2_ARG / s - 1.0) / t)) + 1) for s in slopes2]
    qf, kf = _alibi_features(slopes2, t)
    pos = np.arange(t)
    dist = jnp.asarray(np.abs(pos[:, None] - pos[None, :]), F32)
    kernel = functools.partial(_diff_attn_kernel, tk=t, out_scale=1.0 - lambda_init,
                               lambda_init=lambda_init)
    return pl.pallas_call(
        kernel,
        grid_spec=pltpu.PrefetchScalarGridSpec(
            num_scalar_prefetch=2,
            grid=(batch, B_HEADS),
            in_specs=[pl.BlockSpec((seq, B_VDIM), lambda b, h, r, s: (b, h)),
                      pl.BlockSpec((seq, B_VDIM), lambda b, h, r, s: (b, kcol + h)),
                      pl.BlockSpec((seq, B_VDIM), lambda b, h, r, s: (b, vcol + h)),
                      pl.BlockSpec((None, t, LANES), lambda b, h, r, s: (h, 0, 0)),
                      pl.BlockSpec((None, t, LANES), lambda b, h, r, s: (h, 0, 0)),
                      pl.BlockSpec((t, t), lambda b, h, r, s: (0, 0)),
                      pl.BlockSpec((4, HEAD_DIM), lambda b, h, r, s: (0, 0)),
                      pl.BlockSpec((1, B_VDIM), lambda b, h, r, s: (0, 0))],
            out_specs=pl.BlockSpec((seq, B_VDIM), lambda b, h, r, s: (b, h)),
            scratch_shapes=[pltpu.VMEM((3, 2 * t, 2 * LANES), BF16),
                            pltpu.VMEM((seq, 2 * LANES), BF16),
                            pltpu.VMEM((seq, 2 * LANES), BF16),
                            pltpu.VMEM((2, 8, LANES), F32),
                            pltpu.VMEM((2 * t, 1), F32),
                            pltpu.VMEM((2 * t, 2 * LANES), F32),
                            pltpu.VMEM((2 * t, t), F32),
                            pltpu.VMEM((2 * t, t), BF16),
                            pltpu.VMEM((2 * t, t), BF16),
                            pltpu.VMEM((2 * LANES, seq), BF16)],
        ),
        out_shape=jax.ShapeDtypeStruct((batch * seq, MIX_WIDTH), BF16),
        compiler_params=_params("parallel", "parallel"),
        name="diff_attention",
    )(jnp.asarray(reach, jnp.int32), jnp.asarray(slopes2, F32), proj, proj, proj, qf, kf, dist,
      b_lambda, subln_g.reshape(1, B_VDIM))


def _router_kernel(x_ref, g_ref, whi_ref, wlo_ref, h_ref, e_ref, gate_ref):
    h = _rms(x_ref[...], g_ref[...])
    h_hi = h.astype(BF16)
    h_lo = (h - h_hi.astype(F32)).astype(BF16)
    whi = whi_ref[...]
    logits = _dot(h_hi, whi) + (_dot(h_hi, wlo_ref[...]) + _dot(h_lo, whi))
    lane = lax.broadcasted_iota(jnp.int32, logits.shape, 1)
    logits = jnp.where(lane < N_EXPERTS, logits, -jnp.inf)
    m1 = jnp.max(logits, axis=-1, keepdims=True)
    i1 = jnp.min(jnp.where(logits == m1, lane, LANES), axis=-1, keepdims=True)
    rest = jnp.where(lane == i1, -jnp.inf, logits)
    m2 = jnp.max(rest, axis=-1, keepdims=True)
    i2 = jnp.min(jnp.where(rest == m2, lane, LANES), axis=-1, keepdims=True)
    r = jnp.exp(m2 - m1)
    g1 = 1.0 / (1.0 + r)
    g2 = r * g1
    e_ref[...] = jnp.where(lane == 0, i1, jnp.where(lane == 1, i2, 0))
    gate_ref[...] = jnp.where(lane == 0, g1, jnp.where(lane == 1, g2, 0.0))
    h_ref[...] = h_hi


def _router(x, g, w_router, tm=2048):
    t, d = x.shape
    tm = min(tm, t)
    wpad = jnp.zeros((d, LANES), F32).at[:, :N_EXPERTS].set(w_router)
    whi = wpad.astype(BF16)
    wlo = (wpad - whi.astype(F32)).astype(BF16)
    return pl.pallas_call(
        _router_kernel,
        grid=(t // tm,),
        in_specs=[pl.BlockSpec((tm, d), lambda i: (i, 0)),
                  pl.BlockSpec((1, d), lambda i: (0, 0)),
                  pl.BlockSpec((d, LANES), lambda i: (0, 0)),
                  pl.BlockSpec((d, LANES), lambda i: (0, 0))],
        out_specs=[pl.BlockSpec((tm, d), lambda i: (i, 0)),
                   pl.BlockSpec((tm, LANES), lambda i: (i, 0)),
                   pl.BlockSpec((tm, LANES), lambda i: (i, 0))],
        out_shape=[jax.ShapeDtypeStruct((t, d), BF16),
                   jax.ShapeDtypeStruct((t, LANES), jnp.int32),
                   jax.ShapeDtypeStruct((t, LANES), F32)],
        compiler_params=_params("parallel"),
        name="router",
    )(x, g.reshape(1, d), whi, wlo)


def _slab_sizes(rows, tc):
    quarter, half = tc // 4, tc // 2
    return ((quarter, rows <= quarter),
            (half, jnp.logical_and(rows > quarter, rows <= half)),
            (tc, rows > half))


def _slab_stride(slab_ref, n, step, tc):
    most = slab_ref[(n + step) * N_EXPERTS]
    for e in range(1, N_EXPERTS):
        most = jnp.maximum(most, slab_ref[(n + step) * N_EXPERTS + e])
    return jnp.where(most <= tc // 2, tc // 2, tc)


def _dispatch_kernel(slab_ref, fill_ref, h_ref, dest_ref, gate_ref, xs_hbm, gs_hbm,
                     xslab_sc, gslab_sc, zx_sc, zg_sc, sem, zsem):
    i = pl.program_id(0)
    n = pl.num_programs(0)
    slot = i % 2
    tc = h_ref.shape[0]
    tm = zx_sc.shape[0]

    def slab_dma(step, sl, op):
        stride = _slab_stride(slab_ref, n, step, tc)
        for e in range(N_EXPERTS):
            start = pl.multiple_of(slab_ref[step * N_EXPERTS + e], SUBLANES)
            first = pl.multiple_of(e * stride, SUBLANES)
            for size, cond in _slab_sizes(slab_ref[(n + step) * N_EXPERTS + e], tc):
                @pl.when(cond)
                def _():
                    for src, dst in ((xslab_sc, xs_hbm), (gslab_sc, gs_hbm)):
                        copy = pltpu.make_async_copy(src.at[sl, pl.ds(first, size)],
                                                     dst.at[pl.ds(start, size)], sem.at[sl])
                        getattr(copy, op)()

    def fill_copies(start):
        start = pl.multiple_of(start, SUBLANES)
        return [pltpu.make_async_copy(zx_sc, xs_hbm.at[pl.ds(start, tm)], zsem),
                pltpu.make_async_copy(zg_sc, gs_hbm.at[pl.ds(start, tm)], zsem)]

    @pl.when(i == 0)
    def _():
        zx_sc[...] = jnp.zeros_like(zx_sc)
        zg_sc[...] = jnp.zeros_like(zg_sc)
        for half in range(2):
            for e in range(N_EXPERTS):
                for c in fill_copies(fill_ref[half * N_EXPERTS + e]):
                    c.start()
            for e in range(N_EXPERTS):
                for c in fill_copies(fill_ref[half * N_EXPERTS + e]):
                    c.wait()
        first_free = fill_ref[2 * N_EXPERTS]
        n_tiles = xs_hbm.shape[0] // tm

        def start_tile(b, carry):
            for c in fill_copies(b * tm):
                c.start()
            return carry

        def wait_tile(b, carry):
            for c in fill_copies(b * tm):
                c.wait()
            return carry

        lax.fori_loop(first_free, n_tiles, start_tile, 0)
        lax.fori_loop(first_free, n_tiles, wait_tile, 0)

    dest = dest_ref[0]
    gate = gate_ref[...]
    lane = lax.broadcasted_iota(jnp.int32, gate.shape, 1)

    def gate_pieces(col):
        out = jnp.zeros(gate.shape, F32)
        for n_, piece in enumerate(_bf16_split(col)):
            out = jnp.where(lane == n_, piece.astype(F32), out)
        return out.astype(BF16)

    def place_rows(stride):
        pos = lax.broadcasted_iota(jnp.int32, (stride, tc), 0)
        sel0, sel1 = [], []
        for e in range(N_EXPERTS):
            start = slab_ref[i * N_EXPERTS + e]
            sel0.append(jnp.where(dest[0:1, :] - start == pos, 1.0, 0.0))
            sel1.append(jnp.where(dest[1:2, :] - start == pos, 1.0, 0.0))
        sel0 = jnp.concatenate(sel0, axis=0).astype(BF16)
        sel1 = jnp.concatenate(sel1, axis=0).astype(BF16)
        rows = pl.ds(0, N_EXPERTS * stride)
        xslab_sc[slot, rows] = _dot(sel0 + sel1, h_ref[...])
        gslab_sc[slot, rows] = (_dot(sel0, gate_pieces(gate[:, 0:1]))
                                + _dot(sel1, gate_pieces(gate[:, 1:2])))

    compact = _slab_stride(slab_ref, n, i, tc) < tc
    pl.when(compact)(lambda: place_rows(tc // 2))
    pl.when(jnp.logical_not(compact))(lambda: place_rows(tc))

    @pl.when(i > 0)
    def _():
        slab_dma(i - 1, 1 - slot, "wait")

    slab_dma(i, slot, "start")

    @pl.when(i == n - 1)
    def _():
        slab_dma(i, slot, "wait")


def _dispatch(h, dest_t, gates, slab, fill, p_rows, tm, tc):
    t, d = h.shape
    nb = t // tc
    return pl.pallas_call(
        _dispatch_kernel,
        grid_spec=pltpu.PrefetchScalarGridSpec(
            num_scalar_prefetch=2,
            grid=(nb,),
            in_specs=[pl.BlockSpec((tc, d), lambda i, s, z: (i, 0)),
                      pl.BlockSpec((1, TOP_K, tc), lambda i, s, z: (i, 0, 0)),
                      pl.BlockSpec((tc, LANES), lambda i, s, z: (i, 0))],
            out_specs=[pl.BlockSpec(memory_space=pl.ANY), pl.BlockSpec(memory_space=pl.ANY)],
            scratch_shapes=[pltpu.VMEM((2, N_EXPERTS * tc, d), F32),
                            pltpu.VMEM((2, N_EXPERTS * tc, LANES), F32),
                            pltpu.VMEM((tm, d), F32), pltpu.VMEM((tm, LANES), F32),
                            pltpu.SemaphoreType.DMA((2,)), pltpu.SemaphoreType.DMA(())],
        ),
        out_shape=[jax.ShapeDtypeStruct((p_rows, d), F32), jax.ShapeDtypeStruct((p_rows, LANES), F32)],
        compiler_params=_params("arbitrary"),
        name="moe_dispatch",
    )(slab, fill, h, dest_t, gates)


def _moe_kernel(blk_e_ref, used_ref, xblk_ref, x_ref, gs_ref, wg_ref, wu_ref, wd_ref, o_ref):
    used = used_ref[pl.program_id(0)] > 0

    @pl.when(used)
    def _():
        gs = gs_ref[...]
        gate = gs[:, 0:1] + gs[:, 1:2] + gs[:, 2:3]
        o_ref[...] = _swiglu_tile(x_ref[...].astype(BF16), wg_ref, wu_ref, wd_ref) * gate

    @pl.when(jnp.logical_not(used))
    def _():
        o_ref[...] = jnp.zeros_like(o_ref)


def _moe_experts(xs, gs, blk_e, blk_used, xblk, wg, wu, wd, tm):
    p, d = xs.shape
    ff = wg.shape[2]
    return pl.pallas_call(
        _moe_kernel,
        grid_spec=pltpu.PrefetchScalarGridSpec(
            num_scalar_prefetch=3,
            grid=(p // tm,),
            in_specs=[pl.BlockSpec((tm, d), lambda i, be, us, xb: (xb[i], 0)),
                      pl.BlockSpec((tm, LANES), lambda i, be, us, xb: (xb[i], 0)),
                      pl.BlockSpec((None, d, ff), lambda i, be, us, xb: (be[i], 0, 0)),
                      pl.BlockSpec((None, d, ff), lambda i, be, us, xb: (be[i], 0, 0)),
                      pl.BlockSpec((None, ff, d), lambda i, be, us, xb: (be[i], 0, 0))],
            out_specs=pl.BlockSpec((tm, d), lambda i, be, us, xb: (i, 0)),
        ),
        out_shape=jax.ShapeDtypeStruct((p, d), F32),
        compiler_params=_params("arbitrary"),
        name="moe_experts",
    )(blk_e, blk_used, xblk, xs, gs, wg, wu, wd)


def _combine_kernel(slab_ref, ys_hbm, dest_ref, x_ref, g_ref, o_ref, slab_sc, sem):
    i = pl.program_id(0)
    n = pl.num_programs(0)
    slot = i % 2
    tc = x_ref.shape[0]

    def slab_dma(step, sl, op):
        stride = _slab_stride(slab_ref, n, step, tc)
        for e in range(N_EXPERTS):
            start = pl.multiple_of(slab_ref[step * N_EXPERTS + e], SUBLANES)
            first = pl.multiple_of(e * stride, SUBLANES)
            for size, cond in _slab_sizes(slab_ref[(n + step) * N_EXPERTS + e], tc):
                @pl.when(cond)
                def _():
                    copy = pltpu.make_async_copy(ys_hbm.at[pl.ds(start, size)],
                                                 slab_sc.at[sl, pl.ds(first, size)], sem.at[sl])
                    getattr(copy, op)()

    @pl.when(i == 0)
    def _():
        slab_sc[...] = jnp.zeros_like(slab_sc)
        slab_dma(0, 0, "start")

    @pl.when(i + 1 < n)
    def _():
        slab_dma(i + 1, 1 - slot, "start")

    dest = dest_ref[...]
    slab_dma(i, slot, "wait")

    def gather_rows(stride):
        pos = lax.broadcasted_iota(jnp.int32, (tc, stride), 1)
        sel = []
        for e in range(N_EXPERTS):
            start = slab_ref[i * N_EXPERTS + e]
            hit = jnp.logical_or(dest[:, 0:1] - start == pos, dest[:, 1:2] - start == pos)
            sel.append(jnp.where(hit, 1.0, 0.0))
        sel = jnp.concatenate(sel, axis=1).astype(BF16)
        y = _dot(sel, slab_sc[slot, pl.ds(0, N_EXPERTS * stride)].astype(BF16))
        o_ref[...] = x_ref[...] + _rms(y, g_ref[...])

    compact = _slab_stride(slab_ref, n, i, tc) < tc
    pl.when(compact)(lambda: gather_rows(tc // 2))
    pl.when(jnp.logical_not(compact))(lambda: gather_rows(tc))


def _combine(ys, dest, slab, x, g, tc):
    t, d = x.shape
    return pl.pallas_call(
        _combine_kernel,
        grid_spec=pltpu.PrefetchScalarGridSpec(
            num_scalar_prefetch=1,
            grid=(t // tc,),
            in_specs=[pl.BlockSpec(memory_space=pl.ANY),
                      pl.BlockSpec((tc, LANES), lambda i, s: (i, 0)),
                      pl.BlockSpec((tc, d), lambda i, s: (i, 0)),
                      pl.BlockSpec((1, d), lambda i, s: (0, 0))],
            out_specs=pl.BlockSpec((tc, d), lambda i, s: (i, 0)),
            scratch_shapes=[pltpu.VMEM((2, N_EXPERTS * tc, d), F32), pltpu.SemaphoreType.DMA((2,))],
        ),
        out_shape=jax.ShapeDtypeStruct((t, d), F32),
        compiler_params=_params("arbitrary"),
        name="moe_combine",
    )(slab, ys, dest, x, g.reshape(1, d))


def _route(top_e, tm, tc):
    t = top_e.shape[0]
    a = t * TOP_K
    nb = t // tc
    e_flat = top_e.reshape(a)
    onehot = (e_flat[:, None] == jnp.arange(N_EXPERTS, dtype=jnp.int32)[None, :]).astype(jnp.int32)
    csum = jnp.cumsum(onehot, axis=0)
    rank = jnp.sum(onehot * (csum - 1), axis=1)
    upto = csum[TOP_K * tc - 1::TOP_K * tc]
    before = jnp.concatenate([jnp.zeros((1, N_EXPERTS), jnp.int32), upto[:nb - 1]], axis=0)
    rows = (upto - before + SUBLANES - 1) // SUBLANES * SUBLANES
    rows_before = jnp.cumsum(rows, axis=0) - rows
    total = jnp.sum(rows, axis=0)
    region = (total + 2 * tm - 1) // tm * tm
    rends = jnp.cumsum(region)
    rstarts = rends - region
    slab = (rstarts[None, :] + rows_before).astype(jnp.int32)
    shift = jnp.broadcast_to((slab - before)[:, None, :], (nb, TOP_K * tc, N_EXPERTS)).reshape(a, N_EXPERTS)
    dest = (rank + jnp.sum(onehot * shift, axis=1)).astype(jnp.int32).reshape(t, TOP_K)
    slab = jnp.concatenate([slab.reshape(nb * N_EXPERTS), rows.reshape(nb * N_EXPERTS)])
    row_end = rstarts + total
    fill = jnp.concatenate([row_end, rends - tm, rends[-1:] // tm]).astype(jnp.int32)
    nblk = -(-(a + nb * N_EXPERTS * (SUBLANES - 1)) // tm) + 2 * N_EXPERTS
    blk_start = jnp.arange(nblk, dtype=jnp.int32) * tm
    blk_e = jnp.minimum(jnp.sum((rends[None, :] <= blk_start[:, None]).astype(jnp.int32), axis=1),
                        N_EXPERTS - 1)
    blk_used = jnp.logical_and(blk_start < row_end[blk_e], blk_start < rends[-1]).astype(jnp.int32)
    first_used = jnp.argmax(blk_used).astype(jnp.int32)
    xblk = jnp.where(blk_used > 0, jnp.arange(nblk, dtype=jnp.int32), first_used)
    return dest, slab, fill, blk_e, blk_used, xblk, nblk * tm


def _moe(x, h, e_pad, gates, g_out, wg, wu, wd, tm=512, tc=256):
    t = x.shape[0]
    tc = min(tc, t)
    dest, slab, fill, blk_e, blk_used, xblk, p_rows = _route(e_pad[:, :TOP_K], tm, tc)
    dest_t = dest.reshape(t // tc, tc, TOP_K).transpose(0, 2, 1)
    dest_pad = jnp.zeros((t, LANES), jnp.int32).at[:, :TOP_K].set(dest)
    xs, gs = _dispatch(h, dest_t, gates, slab, fill, p_rows, tm, tc)
    ys = _moe_experts(xs, gs, blk_e, blk_used, xblk, wg, wu, wd, tm)
    return _combine(ys, dest_pad, slab, x, g_out, tc)


def kernel(x, mem, norm_g, mem_norm_g, w_mem_kv, a_w_in, a_sink, a_w_out, b_w_in, b_lambda,
           b_subln_g, b_w_out, ffn_w_gate, ffn_w_up, ffn_w_down, moe_w_router, moe_w_gate,
           moe_w_up, moe_w_down):
    batch, seq, d = x.shape
    n_mem = mem.shape[1]
    bf = lambda w: w.astype(BF16)
    xt = x.reshape(batch * seq, d)
    mkv = _norm_matmul(mem.reshape(batch * n_mem, d), mem_norm_g, bf(w_mem_kv),
                       jnp.ones((w_mem_kv.shape[1],), F32))

    g = norm_g[0]
    proj = _norm_matmul(xt, g[0], bf(a_w_in[0]), _query_col_scale(a_w_in.shape[2], A_Q_HEADS * HEAD_DIM))
    mix = _window_attention(proj, a_sink[0], batch, seq)
    mem_out = _mem_attention(proj, (A_Q_HEADS + 2 * A_KV_HEADS) * HEAD_DIM // MEM_WIDTH, mkv, batch, seq)
    xt = _out_proj(mix, mem_out, bf(a_w_out[0]), xt, g[1])
    xt = _ffn(xt, g[2], g[3], bf(ffn_w_gate[0]), bf(ffn_w_up[0]), bf(ffn_w_down[0]))

    g = norm_g[1]
    lambda_init = 0.8 - 0.6 * math.exp(-0.3 * 1)
    proj = _norm_matmul(xt, g[0], bf(b_w_in[0]), _query_col_scale(b_w_in.shape[2], MIX_WIDTH))
    mix = _diff_attention(proj, b_lambda[0], b_subln_g[0], lambda_init, batch, seq)
    mem_out = _mem_attention(proj, 3 * MIX_WIDTH // MEM_WIDTH, mkv, batch, seq)
    xt = _out_proj(mix, mem_out, bf(b_w_out[0]), xt, g[1])
    h, e_pad, gates = _router(xt, g[2], moe_w_router[0])
    xt = _moe(xt, h, e_pad, gates, g[3], bf(moe_w_gate[0]), bf(moe_w_up[0]), bf(moe_w_down[0]))
    return xt.reshape(batch, seq, d)
```

```python
import functools
import math

import jax
import jax.numpy as jnp
import numpy as np
from jax import lax
from jax.experimental import pallas as pl
from jax.experimental.pallas import tpu as pltpu

D_MODEL = 1024
HEAD_DIM = 64
MIX_WIDTH = 768
MEM_WIDTH = 256
MEM_HEADS = 4
A_Q_HEADS = 12
A_KV_HEADS = 4
A_GROUP = 3
WINDOW = 128
B_HEADS = 6
B_VDIM = 128
D_FF = 3584
N_EXPERTS = 8
TOP_K = 2
EPS = 1e-6
NEG_INF = -1e30
LOG2E = math.log2(math.e)
QK_SCALE = HEAD_DIM ** -0.5 * LOG2E
LANES = 128
SUBLANES = 8
VMEM_LIMIT_BYTES = 56 * 1024 * 1024

BF16 = jnp.bfloat16
F32 = jnp.float32


def _params(*sem):
    return pltpu.CompilerParams(dimension_semantics=sem, vmem_limit_bytes=VMEM_LIMIT_BYTES)


def _rms(x, g):
    return x * lax.rsqrt(jnp.mean(x * x, axis=-1, keepdims=True) + EPS) * g


def _dot(a, b):
    return jnp.dot(a, b, preferred_element_type=F32)


def _dot_nt(a, b):
    return lax.dot_general(a, b, (((1,), (1,)), ((), ())), preferred_element_type=F32)


def _alibi_slopes(n):
    return [2.0 ** (-8.0 * (i + 1) / n) for i in range(n)]


def _norm_matmul_kernel(x_ref, g_ref, w_ref, cs_ref, o_ref, *, chunk):
    h = _rms(x_ref[...], g_ref[...]).astype(BF16)
    for c in range(o_ref.shape[1] // chunk):
        sl = slice(c * chunk, (c + 1) * chunk)
        o_ref[:, sl] = (_dot(h, w_ref[:, sl]) * cs_ref[:, sl]).astype(o_ref.dtype)


def _norm_matmul(x, g, w, col_scale, tm=1024, chunk=512):
    t, d = x.shape
    n = w.shape[1]
    tm = min(tm, t)
    return pl.pallas_call(
        functools.partial(_norm_matmul_kernel, chunk=min(chunk, n)),
        grid=(t // tm,),
        in_specs=[pl.BlockSpec((tm, d), lambda i: (i, 0)),
                  pl.BlockSpec((1, d), lambda i: (0, 0)),
                  pl.BlockSpec((d, n), lambda i: (0, 0)),
                  pl.BlockSpec((1, n), lambda i: (0, 0))],
        out_specs=pl.BlockSpec((tm, n), lambda i: (i, 0)),
        out_shape=jax.ShapeDtypeStruct((t, n), BF16),
        compiler_params=_params("parallel"),
        name="norm_matmul",
    )(x, g.reshape(1, d), w, col_scale.reshape(1, n))


def _query_col_scale(n, mix_q_cols):
    cols = np.arange(n)
    return jnp.asarray(np.where((cols < mix_q_cols) | (cols >= n - MEM_WIDTH), QK_SCALE, 1.0), F32)


def _window_kernel(q_ref, kp_ref, kc_ref, kn_ref, vp_ref, vc_ref, vn_ref, bias_ref, o_ref):
    n = pl.program_id(1)
    nb = pl.num_programs(1)
    blk = kp_ref.shape[0]
    sub = q_ref.shape[0] // blk
    zeros = jnp.zeros((blk, kp_ref.shape[1]), kp_ref.dtype)
    k_all = jnp.concatenate([kp_ref[...], kc_ref[...], kn_ref[...]], axis=0)
    v_all = jnp.concatenate([vp_ref[...], vc_ref[...], vn_ref[...]], axis=0)
    col = lax.broadcasted_iota(jnp.int32, (1, 4 * blk), 1)
    half = lax.broadcasted_iota(jnp.int32, (4 * blk, LANES), 1) < HEAD_DIM
    low_half = jnp.where(half, 1.0, 0.0).astype(BF16)
    high_half = jnp.where(half, 0.0, 1.0).astype(BF16)
    for u in range(sub):
        k = jnp.concatenate([k_all[u * blk:(u + 3) * blk], zeros], axis=0)
        v = jnp.concatenate([v_all[u * blk:(u + 3) * blk], zeros], axis=0)
        edge = jnp.zeros((1, 4 * blk), F32)
        if u == 0:
            edge = jnp.where(jnp.logical_and(n == 0, col < blk), NEG_INF, edge)
        if u == sub - 1:
            edge = jnp.where(jnp.logical_and(n == nb - 1,
                                             jnp.logical_and(col >= 2 * blk, col < 3 * blk)), NEG_INF, edge)
        q = q_ref[u * blk:(u + 1) * blk, :]
        outs = []
        for kh in range(A_KV_HEADS):
            heads = range(kh * A_GROUP, (kh + 1) * A_GROUP)
            kv = slice(kh * HEAD_DIM, (kh + 1) * HEAD_DIM)
            qs = jnp.concatenate([q[:, h * HEAD_DIM:(h + 1) * HEAD_DIM] for h in heads], axis=0)
            s = _dot_nt(qs, k[:, kv]) + bias_ref[kh] + edge
            p = jnp.exp2(s - jnp.max(s, axis=-1, keepdims=True)).astype(BF16)
            low = kh % 2 == 0
            slab = v[:, (kh // 2) * LANES:(kh // 2 + 1) * LANES]
            own, other = (low_half, high_half) if low else (high_half, low_half)
            ov = _dot(p, jnp.concatenate([slab * own + other, own], axis=1))
            den = ov[:, LANES:]
            o = ov[:, :LANES] * (1.0 / jnp.where(den == 0.0, 1.0, den))
            o = o[:, :HEAD_DIM] if low else o[:, HEAD_DIM:]
            outs.extend(o[g * blk:(g + 1) * blk] for g in range(A_GROUP))
        o_ref[u * blk:(u + 1) * blk, :] = jnp.concatenate(outs, axis=-1).astype(o_ref.dtype)


def _window_bias(blk, sink):
    qi = np.arange(blk)[:, None]
    kj = np.arange(3 * blk)[None, :]
    dist = np.abs(blk + qi - kj)
    slopes = _alibi_slopes(A_Q_HEADS)
    tables = np.stack([np.where(dist <= WINDOW, -(slopes[h] * LOG2E) * dist, NEG_INF)
                       for h in range(A_Q_HEADS)])
    pad = jnp.full((A_Q_HEADS, blk, blk), NEG_INF, F32)
    pad = pad.at[:, :, 0].set(jnp.broadcast_to((sink.astype(F32) * LOG2E)[:, None], (A_Q_HEADS, blk)))
    full = jnp.concatenate([jnp.asarray(tables, F32), pad], axis=2)
    return full.reshape(A_KV_HEADS, A_GROUP * blk, 4 * blk)


def _window_attention(proj, sink, batch, seq, sub=4):
    blk = WINDOW
    nb = seq // blk
    sub = min(sub, nb)
    ns = nb // sub
    kcol = MIX_WIDTH // 256
    vcol = kcol + 1
    bias = _window_bias(blk, sink)

    def cur(col):
        return pl.BlockSpec((sub * blk, 256), lambda b, n: (b * ns + n, col))

    def prev(col):
        return pl.BlockSpec((blk, 256), lambda b, n: (b * nb + jnp.maximum(n * sub - 1, 0), col))

    def nxt(col):
        return pl.BlockSpec((blk, 256), lambda b, n: (b * nb + jnp.minimum((n + 1) * sub, nb - 1), col))

    return pl.pallas_call(
        _window_kernel,
        grid=(batch, ns),
        in_specs=[pl.BlockSpec((sub * blk, MIX_WIDTH), lambda b, n: (b * ns + n, 0)),
                  prev(kcol), cur(kcol), nxt(kcol), prev(vcol), cur(vcol), nxt(vcol),
                  pl.BlockSpec(bias.shape, lambda b, n: (0, 0, 0))],
        out_specs=pl.BlockSpec((sub * blk, MIX_WIDTH), lambda b, n: (b * ns + n, 0)),
        out_shape=jax.ShapeDtypeStruct((batch * seq, MIX_WIDTH), BF16),
        compiler_params=_params("parallel", "parallel"),
        name="window_attention",
    )(proj, proj, proj, proj, proj, proj, proj, bias)


def _mem_attn_kernel(q_ref, mkv_ref, o_ref):
    q = q_ref[...]
    mkv = mkv_ref[...]
    ones = jnp.ones((mkv.shape[0], LANES), BF16)
    outs = []
    for h in range(MEM_HEADS):
        sl = slice(h * HEAD_DIM, (h + 1) * HEAD_DIM)
        s = _dot_nt(q[:, sl], mkv[:, sl])
        p = jnp.exp2(s - jnp.max(s, axis=-1, keepdims=True)).astype(BF16)
        o = _dot(p, mkv[:, MEM_WIDTH + h * HEAD_DIM:MEM_WIDTH + (h + 1) * HEAD_DIM])
        outs.append(o * (1.0 / _dot(p, ones)[:, :HEAD_DIM]))
    o_ref[...] = jnp.concatenate(outs, axis=-1).astype(o_ref.dtype)


def _mem_attention(proj, qcol, mkv, batch, seq, tm=2048):
    tm = min(tm, seq)
    per_batch = seq // tm
    n_mem = mkv.shape[0] // batch
    return pl.pallas_call(
        _mem_attn_kernel,
        grid=(batch * per_batch,),
        in_specs=[pl.BlockSpec((tm, MEM_WIDTH), lambda i: (i, qcol)),
                  pl.BlockSpec((n_mem, 2 * MEM_WIDTH), lambda i: (i // per_batch, 0))],
        out_specs=pl.BlockSpec((tm, MEM_WIDTH), lambda i: (i, 0)),
        out_shape=jax.ShapeDtypeStruct((batch * seq, MEM_WIDTH), BF16),
        compiler_params=_params("parallel"),
        name="mem_attention",
    )(proj, mkv)


def _out_proj_kernel(mix_ref, mem_ref, w1_ref, w2_ref, x_ref, g_ref, o_ref):
    o = _dot(mix_ref[...], w1_ref[...]) + _dot(mem_ref[...], w2_ref[...])
    o_ref[...] = x_ref[...] + _rms(o, g_ref[...])


def _out_proj(mix, mem_out, w_out, x, g, tm=2048):
    t, d = x.shape
    tm = min(tm, t)
    w1 = w_out[:MIX_WIDTH]
    w2 = w_out[MIX_WIDTH:]
    return pl.pallas_call(
        _out_proj_kernel,
        grid=(t // tm,),
        in_specs=[pl.BlockSpec((tm, MIX_WIDTH), lambda i: (i, 0)),
                  pl.BlockSpec((tm, MEM_WIDTH), lambda i: (i, 0)),
                  pl.BlockSpec((MIX_WIDTH, d), lambda i: (0, 0)),
                  pl.BlockSpec((MEM_WIDTH, d), lambda i: (0, 0)),
                  pl.BlockSpec((tm, d), lambda i: (i, 0)),
                  pl.BlockSpec((1, d), lambda i: (0, 0))],
        out_specs=pl.BlockSpec((tm, d), lambda i: (i, 0)),
        out_shape=jax.ShapeDtypeStruct((t, d), F32),
        compiler_params=_params("parallel"),
        name="out_proj",
    )(mix, mem_out, w1, w2, x, g.reshape(1, d))


def _swiglu_tile(h, wg_ref, wu_ref, wd_ref, chunk=256):
    assert wg_ref.shape[1] % chunk == 0
    out = None
    for c in range(wg_ref.shape[1] // chunk):
        sl = slice(c * chunk, (c + 1) * chunk)
        a = _dot(h, wg_ref[:, sl])
        u = _dot(h, wu_ref[:, sl])
        z = (a * jax.nn.sigmoid(a) * u).astype(BF16)
        y = _dot(z, wd_ref[sl, :])
        out = y if out is None else out + y
    return out


def _ffn_kernel(x_ref, gin_ref, gout_ref, wg_ref, wu_ref, wd_ref, o_ref):
    x = x_ref[...]
    h = _rms(x, gin_ref[...]).astype(BF16)
    o_ref[...] = x + _rms(_swiglu_tile(h, wg_ref, wu_ref, wd_ref), gout_ref[...])


def _ffn(x, g_in, g_out, wg, wu, wd, tm=512):
    t, d = x.shape
    ff = wg.shape[1]
    tm = min(tm, t)
    return pl.pallas_call(
        _ffn_kernel,
        grid=(t // tm,),
        in_specs=[pl.BlockSpec((tm, d), lambda i: (i, 0)),
                  pl.BlockSpec((1, d), lambda i: (0, 0)),
                  pl.BlockSpec((1, d), lambda i: (0, 0)),
                  pl.BlockSpec((d, ff), lambda i: (0, 0)),
                  pl.BlockSpec((d, ff), lambda i: (0, 0)),
                  pl.BlockSpec((ff, d), lambda i: (0, 0))],
        out_specs=pl.BlockSpec((tm, d), lambda i: (i, 0)),
        out_shape=jax.ShapeDtypeStruct((t, d), F32),
        compiler_params=_params("parallel"),
        name="ffn",
    )(x, g_in.reshape(1, d), g_out.reshape(1, d), wg, wu, wd)


_POS_SPLIT = 256
_LEFT, _DIAG, _RIGHT = 0, 1, 2
_BOUND_LANE = 12
_CHUNK_LANE = 15
_OFFSET_LANE = 18
_ZERO_EXP2_ARG = 136.0
_MIN_DENOM = 2.0 ** -60


def _bf16_split(x):
    hi = x.astype(BF16)
    r = x - hi.astype(F32)
    mid = r.astype(BF16)
    lo = (r - mid.astype(F32)).astype(BF16)
    return hi, mid, lo


def _diff_attn_kernel(reach_ref, slope_ref, q_ref, k_ref, v_ref, qf_ref, kf_ref, dist_ref, lam_ref,
                      g_ref, o_ref, qa_sc, ka_sc, va_sc, k2_sc, m_sc, acc_sc, sa_sc, pa_sc, pb_sc, kt_sc, *, tk,
                      out_scale, lambda_init):
    seq = k_ref.shape[0]
    nk = seq // tk
    k = k_ref[...]
    ka_sc[:, :B_VDIM] = k
    va_sc[:, :B_VDIM] = v_ref[...]
    lane = lax.broadcasted_iota(jnp.int32, (seq, LANES), 1)
    va_sc[:, B_VDIM:] = jnp.where(lane == 0, 1.0, 0.0).astype(BF16)
    kf = kf_ref[...]
    flane = lax.broadcasted_iota(jnp.int32, kf.shape, 1)
    chunk_lanes = jnp.logical_and(flane >= _CHUNK_LANE, flane < _CHUNK_LANE + 3)
    for c in range(nk):
        ka_sc[c * tk:(c + 1) * tk, B_VDIM:] = jnp.where(chunk_lanes, float(c), kf).astype(BF16)
    for c in range(nk):
        kt_sc[:, c * tk:(c + 1) * tk] = ka_sc[c * tk:(c + 1) * tk, :].T
    kk = k.astype(F32)
    kk = kk * kk
    for c, sel in enumerate((lane < HEAD_DIM, lane >= HEAD_DIM)):
        n2 = jnp.sum(jnp.where(sel, kk, 0.0), axis=-1, keepdims=True)
        k2_sc[c] = jnp.broadcast_to(jnp.max(n2, axis=0, keepdims=True), k2_sc.shape[1:])

    def body(i, carry):
        i = jnp.asarray(i, jnp.int32)
        _diff_attn_block(i, reach_ref, slope_ref, q_ref, qf_ref, dist_ref, lam_ref, g_ref, o_ref, qa_sc,
                         ka_sc, va_sc, k2_sc, m_sc, acc_sc, sa_sc, pa_sc, pb_sc, kt_sc, tq=tk, nk=nk,
                         out_scale=out_scale, lambda_init=lambda_init)
        return carry

    lax.fori_loop(0, nk, body, 0)


def _diff_attn_block(i, reach_ref, slope_ref, q_ref, qf_ref, dist_ref, lam_ref, g_ref, o_ref, qa_sc,
                     ka_sc, va_sc, k2_sc, m_sc, acc_sc, sa_sc, pa_sc, pb_sc, kt_sc, *, tq, nk, out_scale,
                     lambda_init):
    h = pl.program_id(1)
    tk = tq
    slope = slope_ref[h]
    reach = reach_ref[h]
    rows = pl.ds(pl.multiple_of(i * tq, tq), tq)
    q = q_ref[rows, :]
    lane = lax.broadcasted_iota(jnp.int32, q.shape, 1)
    zero = jnp.zeros_like(q)
    q0 = jnp.where(lane < HEAD_DIM, q, zero)
    q1 = jnp.where(lane >= HEAD_DIM, q, zero)
    qq = q.astype(F32)
    qq = (qq * qq).astype(BF16)
    comp = lax.broadcasted_iota(jnp.int32, (B_VDIM, LANES), 0)
    n0 = _dot(qq, jnp.where(comp < HEAD_DIM, 1.0, 0.0).astype(BF16)) * (1.0 + 2.0 ** -7)
    n1 = _dot(qq, jnp.where(comp >= HEAD_DIM, 1.0, 0.0).astype(BF16)) * (1.0 + 2.0 ** -7)
    bound = jnp.concatenate([jnp.sqrt(n0) * jnp.sqrt(k2_sc[0][0:1, :]),
                             jnp.sqrt(n1) * jnp.sqrt(k2_sc[1][0:1, :])], axis=0)
    off = slope * (i * tk).astype(F32)
    qf = qf_ref[...]
    qf2 = jnp.concatenate([qf, qf], axis=0)

    def piece_lanes(x, first_lane):
        lane_ = lax.broadcasted_iota(jnp.int32, x.shape, 1)
        out = jnp.zeros_like(x)
        for n, piece in enumerate(_bf16_split(x)):
            out = jnp.where(lane_ == first_lane + n, piece.astype(F32), out)
        return out.astype(BF16)

    def build_queries(with_bound):
        feats = {_LEFT: -qf2, _DIAG: jnp.zeros_like(qf2), _RIGHT: qf2}
        if with_bound:
            base = piece_lanes(bound, _BOUND_LANE)
            side = piece_lanes(jnp.full((SUBLANES, LANES), off, F32), _OFFSET_LANE)[0:1]
            feats = {_LEFT: base + side - qf2, _DIAG: base, _RIGHT: base - side + qf2}
        for variant, feat in feats.items():
            qa_sc[variant, :, B_VDIM:] = feat
            qa_sc[variant, :tq, :B_VDIM] = q0
            qa_sc[variant, tq:, :B_VDIM] = q1

    def scores(j, s_ref):
        variant = jnp.where(j < i, _LEFT, jnp.where(j == i, _DIAG, _RIGHT))
        start = pl.multiple_of(j * tk, tk)
        s_ref[...] = _dot_nt(qa_sc[variant], ka_sc[pl.ds(start, tk), :])

    def diag_bias(j, s_ref):
        @pl.when(j == i)
        def _():
            bias = slope * dist_ref[...]
            s_ref[:tq] -= bias
            s_ref[tq:] -= bias

    def probs(j, p_ref):
        start = pl.multiple_of(j * tk, tk)
        s = _dot(qa_sc[jnp.where(j < i, _LEFT, _RIGHT)], kt_sc[:, pl.ds(start, tk)])
        p_ref[...] = jnp.exp2(s).astype(BF16)

    def values(j):
        return va_sc[pl.ds(pl.multiple_of(j * tk, tk), tk), :]

    build_queries(True)
    lo = jnp.maximum(i - reach, 0)
    count = jnp.minimum(i + reach + 1, nk) - lo - 1

    def off_diag(n):
        j = lo + jnp.minimum(n, count - 1)
        return j + (j >= i).astype(jnp.int32)

    s_diag = _dot(qa_sc[_DIAG], kt_sc[:, pl.ds(pl.multiple_of(i * tk, tk), tk)])
    probs(off_diag(0), pa_sc)
    bias = slope * dist_ref[...]
    p_diag = jnp.exp2(s_diag - jnp.concatenate([bias, bias], axis=0)).astype(BF16)
    acc_sc[...] = _dot(p_diag, values(i))

    def pair(tt, carry):
        n = 2 * tt
        probs(off_diag(n + 1), pb_sc)
        acc_sc[...] += _dot(pa_sc[...], values(off_diag(n)))
        probs(off_diag(n + 2), pa_sc)
        acc_sc[...] += _dot(pb_sc[...], values(off_diag(n + 1)))
        return carry

    lax.fori_loop(0, count // 2, pair, 0)

    @pl.when(count % 2 == 1)
    def _():
        acc_sc[...] += _dot(pa_sc[...], values(off_diag(count - 1)))

    denom_min = jnp.min(acc_sc[:, B_VDIM:B_VDIM + 1])

    @pl.when(jnp.logical_not(denom_min >= _MIN_DENOM))
    def _():
        build_queries(False)
        m_sc[...] = jnp.full_like(m_sc, NEG_INF)
        acc_sc[...] = jnp.zeros_like(acc_sc)

        def chunk(j, carry):
            scores(j, sa_sc)
            diag_bias(j, sa_sc)
            adj = jnp.where(j > i, off, jnp.where(j < i, -off, 0.0))
            start = pl.multiple_of(j * tk, tk)
            s = sa_sc[...]
            m_prev = m_sc[...]
            m_new = jnp.maximum(m_prev, jnp.max(s, axis=-1, keepdims=True) + adj)
            p = jnp.exp2(s - (m_new - adj)).astype(BF16)
            acc_sc[...] = jnp.exp2(m_prev - m_new) * acc_sc[...] + _dot(p, va_sc[pl.ds(start, tk), :])
            m_sc[...] = m_new
            return carry

        lax.fori_loop(0, nk, chunk, 0)

    lp = lam_ref[...]
    lam = (jnp.exp(jnp.sum(lp[0:1] * lp[1:2], axis=-1, keepdims=True))
           - jnp.exp(jnp.sum(lp[2:3] * lp[3:4], axis=-1, keepdims=True)) + lambda_init)
    acc = acc_sc[...]
    o0 = acc[:tq, :B_VDIM] * (1.0 / acc[:tq, B_VDIM:B_VDIM + 1])
    o1 = acc[tq:, :B_VDIM] * (1.0 / acc[tq:, B_VDIM:B_VDIM + 1])
    o_ref[rows, :] = (_rms(o0 - lam * o1, g_ref[...]) * out_scale).astype(o_ref.dtype)


def _bf16_pieces(x, n=3):
    out = []
    r = np.float64(x)
    for _ in range(n):
        p = np.float64(np.float32(r).astype(jnp.bfloat16).astype(np.float32))
        out.append(p)
        r = r - p
    return out


def _alibi_features(slopes2, t):
    pos = np.arange(t)
    hi = (pos // _POS_SPLIT) * _POS_SPLIT
    lo = pos % _POS_SPLIT
    qf = np.zeros((len(slopes2), t, LANES), np.float32)
    kf = np.zeros((len(slopes2), t, LANES), np.float32)
    for h, s in enumerate(slopes2):
        for n, piece in enumerate(_bf16_pieces(s)):
            for base, part in ((0, hi), (3, lo)):
                qf[h, :, base + n] = -piece
                kf[h, :, base + n] = part
                qf[h, :, 6 + base + n] = part
                kf[h, :, 6 + base + n] = piece
        for n, piece in enumerate(_bf16_pieces(s * t)):
            qf[h, :, _CHUNK_LANE + n] = -piece
        kf[h, :, _BOUND_LANE:_BOUND_LANE + 3] = -1.0
        kf[h, :, _OFFSET_LANE:_OFFSET_LANE + 3] = -1.0
    return jnp.asarray(qf, BF16), jnp.asarray(kf, BF16)


def _diff_attention(proj, b_lambda, subln_g, lambda_init, batch, seq, t=512):
    t = min(t, seq // 2)
    nq = seq // t
    assert seq % (2 * t) == 0
    kcol = MIX_WIDTH // B_VDIM
    vcol = 2 * kcol
    slopes2 = [s * LOG2E for s in _alibi_slopes(B_HEADS)]
    reach = [min(nq, int(math.floor((_ZERO_EXP2_ARG / s - 1.0) / t)) + 1) for s in slopes2]
    qf, kf = _alibi_features(slopes2, t)
    pos = np.arange(t)
    dist = jnp.asarray(np.abs(pos[:, None] - pos[None, :]), F32)
    kernel = functools.partial(_diff_attn_kernel, tk=t, out_scale=1.0 - lambda_init,
                               lambda_init=lambda_init)
    return pl.pallas_call(
        kernel,
        grid_spec=pltpu.PrefetchScalarGridSpec(
            num_scalar_prefetch=2,
            grid=(batch, B_HEADS),
            in_specs=[pl.BlockSpec((seq, B_VDIM), lambda b, h, r, s: (b, h)),
                      pl.BlockSpec((seq, B_VDIM), lambda b, h, r, s: (b, kcol + h)),
                      pl.BlockSpec((seq, B_VDIM), lambda b, h, r, s: (b, vcol + h)),
                      pl.BlockSpec((None, t, LANES), lambda b, h, r, s: (h, 0, 0)),
                      pl.BlockSpec((None, t, LANES), lambda b, h, r, s: (h, 0, 0)),
                      pl.BlockSpec((t, t), lambda b, h, r, s: (0, 0)),
                      pl.BlockSpec((4, HEAD_DIM), lambda b, h, r, s: (0, 0)),
                      pl.BlockSpec((1, B_VDIM), lambda b, h, r, s: (0, 0))],
            out_specs=pl.BlockSpec((seq, B_VDIM), lambda b, h, r, s: (b, h)),
            scratch_shapes=[pltpu.VMEM((3, 2 * t, 2 * LANES), BF16),
                            pltpu.VMEM((seq, 2 * LANES), BF16),
                            pltpu.VMEM((seq, 2 * LANES), BF16),
                            pltpu.VMEM((2, 8, LANES), F32),
                            pltpu.VMEM((2 * t, 1), F32),
                            pltpu.VMEM((2 * t, 2 * LANES), F32),
                            pltpu.VMEM((2 * t, t), F32),
                            pltpu.VMEM((2 * t, t), BF16),
                            pltpu.VMEM((2 * t, t), BF16),
                            pltpu.VMEM((2 * LANES, seq), BF16)],
        ),
        out_shape=jax.ShapeDtypeStruct((batch * seq, MIX_WIDTH), BF16),
        compiler_params=_params("parallel", "parallel"),
        name="diff_attention",
    )(jnp.asarray(reach, jnp.int32), jnp.asarray(slopes2, F32), proj, proj, proj, qf, kf, dist,
      b_lambda, subln_g.reshape(1, B_VDIM))


def _router_kernel(x_ref, g_ref, whi_ref, wlo_ref, h_ref, e_ref, gate_ref):
    h = _rms(x_ref[...], g_ref[...])
    h_hi = h.astype(BF16)
    h_lo = (h - h_hi.astype(F32)).astype(BF16)
    whi = whi_ref[...]
    logits = _dot(h_hi, whi) + (_dot(h_hi, wlo_ref[...]) + _dot(h_lo, whi))
    lane = lax.broadcasted_iota(jnp.int32, logits.shape, 1)
    logits = jnp.where(lane < N_EXPERTS, logits, -jnp.inf)
    m1 = jnp.max(logits, axis=-1, keepdims=True)
    i1 = jnp.min(jnp.where(logits == m1, lane, LANES), axis=-1, keepdims=True)
    rest = jnp.where(lane == i1, -jnp.inf, logits)
    m2 = jnp.max(rest, axis=-1, keepdims=True)
    i2 = jnp.min(jnp.where(rest == m2, lane, LANES), axis=-1, keepdims=True)
    r = jnp.exp(m2 - m1)
    g1 = 1.0 / (1.0 + r)
    g2 = r * g1
    e_ref[...] = jnp.where(lane == 0, i1, jnp.where(lane == 1, i2, 0))
    gate_ref[...] = jnp.where(lane == 0, g1, jnp.where(lane == 1, g2, 0.0))
    h_ref[...] = h_hi


def _router(x, g, w_router, tm=2048):
    t, d = x.shape
    tm = min(tm, t)
    wpad = jnp.zeros((d, LANES), F32).at[:, :N_EXPERTS].set(w_router)
    whi = wpad.astype(BF16)
    wlo = (wpad - whi.astype(F32)).astype(BF16)
    return pl.pallas_call(
        _router_kernel,
        grid=(t // tm,),
        in_specs=[pl.BlockSpec((tm, d), lambda i: (i, 0)),
                  pl.BlockSpec((1, d), lambda i: (0, 0)),
                  pl.BlockSpec((d, LANES), lambda i: (0, 0)),
                  pl.BlockSpec((d, LANES), lambda i: (0, 0))],
        out_specs=[pl.BlockSpec((tm, d), lambda i: (i, 0)),
                   pl.BlockSpec((tm, LANES), lambda i: (i, 0)),
                   pl.BlockSpec((tm, LANES), lambda i: (i, 0))],
        out_shape=[jax.ShapeDtypeStruct((t, d), BF16),
                   jax.ShapeDtypeStruct((t, LANES), jnp.int32),
                   jax.ShapeDtypeStruct((t, LANES), F32)],
        compiler_params=_params("parallel"),
        name="router",
    )(x, g.reshape(1, d), whi, wlo)


def _slab_sizes(rows, tc):
    quarter, half = tc // 4, tc // 2
    return ((quarter, rows <= quarter),
            (half, jnp.logical_and(rows > quarter, rows <= half)),
            (tc, rows > half))


def _slab_stride(slab_ref, n, step, tc):
    most = slab_ref[(n + step) * N_EXPERTS]
    for e in range(1, N_EXPERTS):
        most = jnp.maximum(most, slab_ref[(n + step) * N_EXPERTS + e])
    return jnp.where(most <= tc // 2, tc // 2, tc)


def _dispatch_kernel(slab_ref, fill_ref, h_ref, dest_ref, gate_ref, xs_hbm, gs_hbm,
                     xslab_sc, gslab_sc, zx_sc, zg_sc, sem, zsem):
    i = pl.program_id(0)
    n = pl.num_programs(0)
    slot = i % 2
    tc = h_ref.shape[0]
    tm = zx_sc.shape[0]

    def slab_dma(step, sl, op):
        stride = _slab_stride(slab_ref, n, step, tc)
        for e in range(N_EXPERTS):
            start = pl.multiple_of(slab_ref[step * N_EXPERTS + e], SUBLANES)
            first = pl.multiple_of(e * stride, SUBLANES)
            for size, cond in _slab_sizes(slab_ref[(n + step) * N_EXPERTS + e], tc):
                @pl.when(cond)
                def _():
                    for src, dst in ((xslab_sc, xs_hbm), (gslab_sc, gs_hbm)):
                        copy = pltpu.make_async_copy(src.at[sl, pl.ds(first, size)],
                                                     dst.at[pl.ds(start, size)], sem.at[sl])
                        getattr(copy, op)()

    def fill_copies(start):
        start = pl.multiple_of(start, SUBLANES)
        return [pltpu.make_async_copy(zx_sc, xs_hbm.at[pl.ds(start, tm)], zsem),
                pltpu.make_async_copy(zg_sc, gs_hbm.at[pl.ds(start, tm)], zsem)]

    @pl.when(i == 0)
    def _():
        zx_sc[...] = jnp.zeros_like(zx_sc)
        zg_sc[...] = jnp.zeros_like(zg_sc)
        for half in range(2):
            for e in range(N_EXPERTS):
                for c in fill_copies(fill_ref[half * N_EXPERTS + e]):
                    c.start()
            for e in range(N_EXPERTS):
                for c in fill_copies(fill_ref[half * N_EXPERTS + e]):
                    c.wait()
        first_free = fill_ref[2 * N_EXPERTS]
        n_tiles = xs_hbm.shape[0] // tm

        def start_tile(b, carry):
            for c in fill_copies(b * tm):
                c.start()
            return carry

        def wait_tile(b, carry):
            for c in fill_copies(b * tm):
                c.wait()
            return carry

        lax.fori_loop(first_free, n_tiles, start_tile, 0)
        lax.fori_loop(first_free, n_tiles, wait_tile, 0)

    dest = dest_ref[0]
    gate = gate_ref[...]
    lane = lax.broadcasted_iota(jnp.int32, gate.shape, 1)

    def gate_pieces(col):
        out = jnp.zeros(gate.shape, F32)
        for n_, piece in enumerate(_bf16_split(col)):
            out = jnp.where(lane == n_, piece.astype(F32), out)
        return out.astype(BF16)

    def place_rows(stride):
        pos = lax.broadcasted_iota(jnp.int32, (stride, tc), 0)
        sel0, sel1 = [], []
        for e in range(N_EXPERTS):
            start = slab_ref[i * N_EXPERTS + e]
            sel0.append(jnp.where(dest[0:1, :] - start == pos, 1.0, 0.0))
            sel1.append(jnp.where(dest[1:2, :] - start == pos, 1.0, 0.0))
        sel0 = jnp.concatenate(sel0, axis=0).astype(BF16)
        sel1 = jnp.concatenate(sel1, axis=0).astype(BF16)
        rows = pl.ds(0, N_EXPERTS * stride)
        xslab_sc[slot, rows] = _dot(sel0 + sel1, h_ref[...])
        gslab_sc[slot, rows] = (_dot(sel0, gate_pieces(gate[:, 0:1]))
                                + _dot(sel1, gate_pieces(gate[:, 1:2])))

    compact = _slab_stride(slab_ref, n, i, tc) < tc
    pl.when(compact)(lambda: place_rows(tc // 2))
    pl.when(jnp.logical_not(compact))(lambda: place_rows(tc))

    @pl.when(i > 0)
    def _():
        slab_dma(i - 1, 1 - slot, "wait")

    slab_dma(i, slot, "start")

    @pl.when(i == n - 1)
    def _():
        slab_dma(i, slot, "wait")


def _dispatch(h, dest_t, gates, slab, fill, p_rows, tm, tc):
    t, d = h.shape
    nb = t // tc
    return pl.pallas_call(
        _dispatch_kernel,
        grid_spec=pltpu.PrefetchScalarGridSpec(
            num_scalar_prefetch=2,
            grid=(nb,),
            in_specs=[pl.BlockSpec((tc, d), lambda i, s, z: (i, 0)),
                      pl.BlockSpec((1, TOP_K, tc), lambda i, s, z: (i, 0, 0)),
                      pl.BlockSpec((tc, LANES), lambda i, s, z: (i, 0))],
            out_specs=[pl.BlockSpec(memory_space=pl.ANY), pl.BlockSpec(memory_space=pl.ANY)],
            scratch_shapes=[pltpu.VMEM((2, N_EXPERTS * tc, d), F32),
                            pltpu.VMEM((2, N_EXPERTS * tc, LANES), F32),
                            pltpu.VMEM((tm, d), F32), pltpu.VMEM((tm, LANES), F32),
                            pltpu.SemaphoreType.DMA((2,)), pltpu.SemaphoreType.DMA(())],
        ),
        out_shape=[jax.ShapeDtypeStruct((p_rows, d), F32), jax.ShapeDtypeStruct((p_rows, LANES), F32)],
        compiler_params=_params("arbitrary"),
        name="moe_dispatch",
    )(slab, fill, h, dest_t, gates)


def _moe_kernel(blk_e_ref, used_ref, xblk_ref, x_ref, gs_ref, wg_ref, wu_ref, wd_ref, o_ref):
    used = used_ref[pl.program_id(0)] > 0

    @pl.when(used)
    def _():
        gs = gs_ref[...]
        gate = gs[:, 0:1] + gs[:, 1:2] + gs[:, 2:3]
        o_ref[...] = _swiglu_tile(x_ref[...].astype(BF16), wg_ref, wu_ref, wd_ref) * gate

    @pl.when(jnp.logical_not(used))
    def _():
        o_ref[...] = jnp.zeros_like(o_ref)


def _moe_experts(xs, gs, blk_e, blk_used, xblk, wg, wu, wd, tm):
    p, d = xs.shape
    ff = wg.shape[2]
    return pl.pallas_call(
        _moe_kernel,
        grid_spec=pltpu.PrefetchScalarGridSpec(
            num_scalar_prefetch=3,
            grid=(p // tm,),
            in_specs=[pl.BlockSpec((tm, d), lambda i, be, us, xb: (xb[i], 0)),
                      pl.BlockSpec((tm, LANES), lambda i, be, us, xb: (xb[i], 0)),
                      pl.BlockSpec((None, d, ff), lambda i, be, us, xb: (be[i], 0, 0)),
                      pl.BlockSpec((None, d, ff), lambda i, be, us, xb: (be[i], 0, 0)),
                      pl.BlockSpec((None, ff, d), lambda i, be, us, xb: (be[i], 0, 0))],
            out_specs=pl.BlockSpec((tm, d), lambda i, be, us, xb: (i, 0)),
        ),
        out_shape=jax.ShapeDtypeStruct((p, d), F32),
        compiler_params=_params("arbitrary"),
        name="moe_experts",
    )(blk_e, blk_used, xblk, xs, gs, wg, wu, wd)


def _combine_kernel(slab_ref, ys_hbm, dest_ref, x_ref, g_ref, o_ref, slab_sc, sem):
    i = pl.program_id(0)
    n = pl.num_programs(0)
    slot = i % 2
    tc = x_ref.shape[0]

    def slab_dma(step, sl, op):
        stride = _slab_stride(slab_ref, n, step, tc)
        for e in range(N_EXPERTS):
            start = pl.multiple_of(slab_ref[step * N_EXPERTS + e], SUBLANES)
            first = pl.multiple_of(e * stride, SUBLANES)
            for size, cond in _slab_sizes(slab_ref[(n + step) * N_EXPERTS + e], tc):
                @pl.when(cond)
                def _():
                    copy = pltpu.make_async_copy(ys_hbm.at[pl.ds(start, size)],
                                                 slab_sc.at[sl, pl.ds(first, size)], sem.at[sl])
                    getattr(copy, op)()

    @pl.when(i == 0)
    def _():
        slab_sc[...] = jnp.zeros_like(slab_sc)
        slab_dma(0, 0, "start")

    @pl.when(i + 1 < n)
    def _():
        slab_dma(i + 1, 1 - slot, "start")

    dest = dest_ref[...]
    slab_dma(i, slot, "wait")

    def gather_rows(stride):
        pos = lax.broadcasted_iota(jnp.int32, (tc, stride), 1)
        sel = []
        for e in range(N_EXPERTS):
            start = slab_ref[i * N_EXPERTS + e]
            hit = jnp.logical_or(dest[:, 0:1] - start == pos, dest[:, 1:2] - start == pos)
            sel.append(jnp.where(hit, 1.0, 0.0))
        sel = jnp.concatenate(sel, axis=1).astype(BF16)
        y = _dot(sel, slab_sc[slot, pl.ds(0, N_EXPERTS * stride)].astype(BF16))
        o_ref[...] = x_ref[...] + _rms(y, g_ref[...])

    compact = _slab_stride(slab_ref, n, i, tc) < tc
    pl.when(compact)(lambda: gather_rows(tc // 2))
    pl.when(jnp.logical_not(compact))(lambda: gather_rows(tc))


def _combine(ys, dest, slab, x, g, tc):
    t, d = x.shape
    return pl.pallas_call(
        _combine_kernel,
        grid_spec=pltpu.PrefetchScalarGridSpec(
            num_scalar_prefetch=1,
            grid=(t // tc,),
            in_specs=[pl.BlockSpec(memory_space=pl.ANY),
                      pl.BlockSpec((tc, LANES), lambda i, s: (i, 0)),
                      pl.BlockSpec((tc, d), lambda i, s: (i, 0)),
                      pl.BlockSpec((1, d), lambda i, s: (0, 0))],
            out_specs=pl.BlockSpec((tc, d), lambda i, s: (i, 0)),
            scratch_shapes=[pltpu.VMEM((2, N_EXPERTS * tc, d), F32), pltpu.SemaphoreType.DMA((2,))],
        ),
        out_shape=jax.ShapeDtypeStruct((t, d), F32),
        compiler_params=_params("arbitrary"),
        name="moe_combine",
    )(slab, ys, dest, x, g.reshape(1, d))


def _route(top_e, tm, tc):
    t = top_e.shape[0]
    a = t * TOP_K
    nb = t // tc
    e_flat = top_e.reshape(a)
    onehot = (e_flat[:, None] == jnp.arange(N_EXPERTS, dtype=jnp.int32)[None, :]).astype(jnp.int32)
    csum = jnp.cumsum(onehot, axis=0)
    rank = jnp.sum(onehot * (csum - 1), axis=1)
    upto = csum[TOP_K * tc - 1::TOP_K * tc]
    before = jnp.concatenate([jnp.zeros((1, N_EXPERTS), jnp.int32), upto[:nb - 1]], axis=0)
    rows = (upto - before + SUBLANES - 1) // SUBLANES * SUBLANES
    rows_before = jnp.cumsum(rows, axis=0) - rows
    total = jnp.sum(rows, axis=0)
    region = (total + 2 * tm - 1) // tm * tm
    rends = jnp.cumsum(region)
    rstarts = rends - region
    slab = (rstarts[None, :] + rows_before).astype(jnp.int32)
    shift = jnp.broadcast_to((slab - before)[:, None, :], (nb, TOP_K * tc, N_EXPERTS)).reshape(a, N_EXPERTS)
    dest = (rank + jnp.sum(onehot * shift, axis=1)).astype(jnp.int32).reshape(t, TOP_K)
    slab = jnp.concatenate([slab.reshape(nb * N_EXPERTS), rows.reshape(nb * N_EXPERTS)])
    row_end = rstarts + total
    fill = jnp.concatenate([row_end, rends - tm, rends[-1:] // tm]).astype(jnp.int32)
    nblk = -(-(a + nb * N_EXPERTS * (SUBLANES - 1)) // tm) + 2 * N_EXPERTS
    blk_start = jnp.arange(nblk, dtype=jnp.int32) * tm
    blk_e = jnp.minimum(jnp.sum((rends[None, :] <= blk_start[:, None]).astype(jnp.int32), axis=1),
                        N_EXPERTS - 1)
    blk_used = jnp.logical_and(blk_start < row_end[blk_e], blk_start < rends[-1]).astype(jnp.int32)
    first_used = jnp.argmax(blk_used).astype(jnp.int32)
    xblk = jnp.where(blk_used > 0, jnp.arange(nblk, dtype=jnp.int32), first_used)
    return dest, slab, fill, blk_e, blk_used, xblk, nblk * tm


def _moe(x, h, e_pad, gates, g_out, wg, wu, wd, tm=512, tc=256):
    t = x.shape[0]
    tc = min(tc, t)
    dest, slab, fill, blk_e, blk_used, xblk, p_rows = _route(e_pad[:, :TOP_K], tm, tc)
    dest_t = dest.reshape(t // tc, tc, TOP_K).transpose(0, 2, 1)
    dest_pad = jnp.zeros((t, LANES), jnp.int32).at[:, :TOP_K].set(dest)
    xs, gs = _dispatch(h, dest_t, gates, slab, fill, p_rows, tm, tc)
    ys = _moe_experts(xs, gs, blk_e, blk_used, xblk, wg, wu, wd, tm)
    return _combine(ys, dest_pad, slab, x, g_out, tc)


def kernel(x, mem, norm_g, mem_norm_g, w_mem_kv, a_w_in, a_sink, a_w_out, b_w_in, b_lambda,
           b_subln_g, b_w_out, ffn_w_gate, ffn_w_up, ffn_w_down, moe_w_router, moe_w_gate,
           moe_w_up, moe_w_down):
    batch, seq, d = x.shape
    n_mem = mem.shape[1]
    bf = lambda w: w.astype(BF16)
    xt = x.reshape(batch * seq, d)
    mkv = _norm_matmul(mem.reshape(batch * n_mem, d), mem_norm_g, bf(w_mem_kv),
                       jnp.ones((w_mem_kv.shape[1],), F32))

    g = norm_g[0]
    proj = _norm_matmul(xt, g[0], bf(a_w_in[0]), _query_col_scale(a_w_in.shape[2], A_Q_HEADS * HEAD_DIM))
    mix = _window_attention(proj, a_sink[0], batch, seq)
    mem_out = _mem_attention(proj, (A_Q_HEADS + 2 * A_KV_HEADS) * HEAD_DIM // MEM_WIDTH, mkv, batch, seq)
    xt = _out_proj(mix, mem_out, bf(a_w_out[0]), xt, g[1])
    xt = _ffn(xt, g[2], g[3], bf(ffn_w_gate[0]), bf(ffn_w_up[0]), bf(ffn_w_down[0]))

    g = norm_g[1]
    lambda_init = 0.8 - 0.6 * math.exp(-0.3 * 1)
    proj = _norm_matmul(xt, g[0], bf(b_w_in[0]), _query_col_scale(b_w_in.shape[2], MIX_WIDTH))
    mix = _diff_attention(proj, b_lambda[0], b_subln_g[0], lambda_init, batch, seq)
    mem_out = _mem_attention(proj, 3 * MIX_WIDTH // MEM_WIDTH, mkv, batch, seq)
    xt = _out_proj(mix, mem_out, bf(b_w_out[0]), xt, g[1])
    h, e_pad, gates = _router(xt, g[2], moe_w_router[0])
    xt = _moe(xt, h, e_pad, gates, g[3], bf(moe_w_gate[0]), bf(moe_w_up[0]), bf(moe_w_down[0]))
    return xt.reshape(batch, seq, d)
```
